```python
import jax, jax.numpy as jnp
from jax import lax
import numpy as np

D_MODEL = 1024
BATCH = 4
SEQ = 4096
DEPTH = 1
DEC_BATCH = 128
DEC_SEQ = 8
PAST_LEN = 8192
PAGE_SIZE = 128

MIX_WIDTH = D_MODEL
ATT_HEADS = 8
ATT_KV_HEADS = 2
ATT_HEAD_DIM = 64
ATT_GROUP = ATT_HEADS // ATT_KV_HEADS
WINDOW = 128
ATT_BLOCK = WINDOW
ATT_SCALE = ATT_HEAD_DIM ** -0.5
ML_HEADS = 4
ML_V_DIM = 128
ML_QK_DIM = 64
ML_CHUNK = 64
ML_SCALE = ML_QK_DIM ** -0.5
D_FF = 2816
CONV_W = 3
EPS = 1e-6

ATT_Q_W = ATT_HEADS * ATT_HEAD_DIM
ATT_KV_W = ATT_KV_HEADS * ATT_HEAD_DIM
ML_QK_W = ML_HEADS * ML_QK_DIM
ML_V_W = ML_HEADS * ML_V_DIM
IN_WIDTH = ATT_Q_W + 2 * ATT_KV_W + 2 * ML_QK_W + 2 * ML_V_W + 2 * ML_HEADS

kernel_name = "hymba_swa_sink_mlstm_convffn_step"


def rms_norm(x, w):
    xf = x.astype(jnp.float32)
    y = xf * lax.rsqrt(jnp.mean(xf * xf, axis=-1, keepdims=True) + EPS)
    return (y * w.astype(jnp.float32)).astype(x.dtype)


def split_projection(z):
    sizes = (ATT_Q_W, ATT_KV_W, ATT_KV_W, ML_QK_W, ML_QK_W, ML_V_W, ML_V_W, 2 * ML_HEADS)
    offsets = [int(o) for o in np.cumsum(sizes)[:-1]]
    return jnp.split(z, offsets, axis=-1)


def sink_softmax(s, mask, sinks):
    sink = sinks.astype(jnp.float32).reshape(ATT_KV_HEADS, ATT_GROUP, 1, 1)
    s = jnp.where(mask, s, -jnp.inf)
    m = jnp.maximum(jnp.max(s, axis=-1, keepdims=True), sink)
    e = jnp.exp(s - m)
    return e / (jnp.sum(e, axis=-1, keepdims=True) + jnp.exp(sink - m))


def swa_prompt(q, k, v, sinks):
    B, S = q.shape[:2]
    nb = S // ATT_BLOCK
    qb = q.reshape(B, nb, ATT_BLOCK, ATT_KV_HEADS, ATT_GROUP, ATT_HEAD_DIM)
    kb = k.reshape(B, nb, ATT_BLOCK, ATT_KV_HEADS, ATT_HEAD_DIM)
    vb = v.reshape(B, nb, ATT_BLOCK, ATT_KV_HEADS, ATT_HEAD_DIM)
    pad = ((0, 0), (1, 0), (0, 0), (0, 0), (0, 0))
    kk = jnp.concatenate([jnp.pad(kb, pad)[:, :-1], kb], axis=2)
    vv = jnp.concatenate([jnp.pad(vb, pad)[:, :-1], vb], axis=2)
    s = jnp.einsum('bnqkgd,bnskd->bnkgqs', qb, kk).astype(jnp.float32) * ATT_SCALE
    qi = jnp.arange(ATT_BLOCK)[:, None]
    kj = jnp.arange(2 * ATT_BLOCK)[None, :]
    diff = qi + ATT_BLOCK - kj
    blk = jnp.arange(nb)[:, None, None]
    valid_key = (blk * ATT_BLOCK - ATT_BLOCK + kj[None]) >= 0
    mask = (diff >= 0)[None] & (diff < WINDOW)[None] & valid_key
    p = sink_softmax(s, mask[None, :, None, None], sinks)
    o = jnp.einsum('bnkgqs,bnskd->bnqkgd', p.astype(vv.dtype), vv)
    return o.reshape(B, S, ATT_Q_W)


def swa_sample(q, k, v, cache_k, cache_v, sinks):
    B, T = q.shape[:2]
    keys = jnp.concatenate([cache_k.astype(k.dtype), k], axis=1)
    vals = jnp.concatenate([cache_v.astype(v.dtype), v], axis=1)
    qg = q.reshape(B, T, ATT_KV_HEADS, ATT_GROUP, ATT_HEAD_DIM)
    s = jnp.einsum('btkgd,bskd->bkgts', qg, keys).astype(jnp.float32) * ATT_SCALE
    i = jnp.arange(T)[:, None]
    j = jnp.arange(WINDOW + T)[None, :]
    diff = i + WINDOW - j
    mask = (diff >= 0) & (diff < WINDOW)
    p = sink_softmax(s, mask, sinks)
    o = jnp.einsum('bkgts,bskd->btkgd', p.astype(vals.dtype), vals).reshape(B, T, ATT_Q_W)
    return o, keys[:, -WINDOW:], vals[:, -WINDOW:]


def mlstm_chunk(carry, inp):
    C, n, m = carry
    q, k, v, ig, lf = inp
    L = q.shape[2]
    b = jnp.cumsum(lf, axis=-1)
    tril = jnp.arange(L)[:, None] >= jnp.arange(L)[None, :]
    D = jnp.where(tril, b[..., :, None] - b[..., None, :] + ig[..., None, :], -jnp.inf)
    inter = b + m[..., None]
    m_row = jnp.maximum(inter, jnp.max(D, axis=-1))
    w_inter = jnp.exp(inter - m_row)
    qk = jnp.einsum('bhtd,bhsd->bhts', q, k) * jnp.exp(D - m_row[..., None])
    num = jnp.einsum('bhts,bhsv->bhtv', qk, v) + w_inter[..., None] * jnp.einsum('bhvd,bhtd->bhtv', C, q)
    den = jnp.sum(qk, axis=-1) + w_inter * jnp.einsum('bhd,bhtd->bht', n, q)
    h = num / jnp.maximum(jnp.abs(den), jnp.exp(-m_row))[..., None]
    bL = b[..., -1]
    a = bL[..., None] - b + ig
    m_new = jnp.maximum(bL + m, jnp.max(a, axis=-1))
    sc = jnp.exp(bL + m - m_new)
    ws = jnp.exp(a - m_new[..., None])
    C_new = sc[..., None, None] * C + jnp.einsum('bhs,bhsv,bhsd->bhvd', ws, v, k)
    n_new = sc[..., None] * n + jnp.einsum('bhs,bhsd->bhd', ws, k)
    return (C_new, n_new, m_new), h


def mlstm(q, k, v, ig, lf, C0, n0, m0, chunk):
    B, T = q.shape[:2]
    nc = T // chunk

    def to_chunks(a):
        a = a.astype(jnp.float32).reshape((B, nc, chunk) + a.shape[2:])
        return jnp.moveaxis(a, (1, 2), (0, 3))

    xs = (to_chunks(q), to_chunks(k), to_chunks(v), to_chunks(ig), to_chunks(lf))
    carry0 = (C0.astype(jnp.float32), n0.astype(jnp.float32), m0.astype(jnp.float32))
    (C, n, m), h = lax.scan(mlstm_chunk, carry0, xs)
    h = jnp.moveaxis(h, (0, 3), (1, 2)).reshape(B, T, ML_HEADS, ML_V_DIM)
    return h, C, n, m


def layer(x, cache_k, cache_v, C0, n0, m0, conv_buf,
          norm_mix_w, w_in, b_gates, q_norm_w, k_norm_w, sinks, ml_norm_w, w_out,
          norm_ffn_w, w_ffn_in, conv_w, conv_b, w_down):
    prompt = cache_k is None
    B, T, _ = x.shape
    h = rms_norm(x, norm_mix_w)
    q_a, k_a, v_a, q_m, k_m, v_m, o_m, gl = split_projection(h @ w_in)
    q_a = rms_norm(q_a.reshape(B, T, ATT_HEADS, ATT_HEAD_DIM), q_norm_w)
    k_a = rms_norm(k_a.reshape(B, T, ATT_KV_HEADS, ATT_HEAD_DIM), k_norm_w)
    v_a = v_a.reshape(B, T, ATT_KV_HEADS, ATT_HEAD_DIM)
    if prompt:
        o_att = swa_prompt(q_a, k_a, v_a, sinks)
        new_k, new_v = k_a[:, -WINDOW:], v_a[:, -WINDOW:]
    else:
        o_att, new_k, new_v = swa_sample(q_a, k_a, v_a, cache_k, cache_v, sinks)
    q_m = q_m.reshape(B, T, ML_HEADS, ML_QK_DIM) * ML_SCALE
    k_m = k_m.reshape(B, T, ML_HEADS, ML_QK_DIM)
    v_m = v_m.reshape(B, T, ML_HEADS, ML_V_DIM)
    gl = gl.astype(jnp.float32) + b_gates.astype(jnp.float32)
    ig = gl[..., :ML_HEADS]
    lf = jax.nn.log_sigmoid(gl[..., ML_HEADS:])
    if prompt:
        C0 = jnp.zeros((B, ML_HEADS, ML_V_DIM, ML_QK_DIM), jnp.float32)
        n0 = jnp.zeros((B, ML_HEADS, ML_QK_DIM), jnp.float32)
        m0 = jnp.zeros((B, ML_HEADS), jnp.float32)
        chunk = ML_CHUNK
    else:
        chunk = T
    h_m, C, n, m = mlstm(q_m, k_m, v_m, ig, lf, C0, n0, m0, chunk)
    h_m = rms_norm(h_m, ml_norm_w.reshape(ML_HEADS, ML_V_DIM)).reshape(B, T, ML_V_W)
    o_ml = (h_m * jax.nn.sigmoid(o_m.astype(jnp.float32))).astype(x.dtype)
    x = x + jnp.concatenate([o_att, o_ml], axis=-1) @ w_out
    h2 = rms_norm(x, norm_ffn_w)
    g, u = jnp.split(h2 @ w_ffn_in, 2, axis=-1)
    if prompt:
        conv_buf = jnp.zeros((B, CONV_W - 1, D_FF), x.dtype)
    gp = jnp.concatenate([conv_buf.astype(g.dtype), g], axis=1)
    gc = conv_b + gp[:, 0:T] * conv_w[0] + gp[:, 1:T + 1] * conv_w[1] + gp[:, 2:T + 2] * conv_w[2]
    x = x + (jax.nn.silu(gc) * u) @ w_down
    new_conv = gp[:, -(CONV_W - 1):]
    dt = x.dtype
    return x, new_k, new_v, C.astype(dt), n.astype(dt), m.astype(dt), new_conv


def setup_inputs(seed: int = 0) -> dict:
    key = jax.random.key(seed)
    ks = jax.random.split(key, 24)
    f32 = jnp.float32
    nrm = lambda k, shape, s: jax.random.normal(k, shape, f32) * s
    b_gates = jnp.concatenate([
        nrm(ks[11], (DEPTH, ML_HEADS), 0.1),
        3.0 + nrm(ks[12], (DEPTH, ML_HEADS), 0.5)], axis=-1)
    return {
        "x_prompt": nrm(ks[0], (BATCH, SEQ, D_MODEL), 1.0),
        "x_sample": nrm(ks[1], (DEC_BATCH, DEC_SEQ, D_MODEL), 1.0),
        "cache_attn_k": nrm(ks[2], (DEPTH, DEC_BATCH, WINDOW, ATT_KV_HEADS, ATT_HEAD_DIM), 1.0),
        "cache_attn_v": nrm(ks[3], (DEPTH, DEC_BATCH, WINDOW, ATT_KV_HEADS, ATT_HEAD_DIM), 1.0),
        "state_mlstm_C": nrm(ks[4], (DEPTH, DEC_BATCH, ML_HEADS, ML_V_DIM, ML_QK_DIM), 0.1),
        "state_mlstm_n": nrm(ks[5], (DEPTH, DEC_BATCH, ML_HEADS, ML_QK_DIM), 0.1),
        "state_mlstm_m": nrm(ks[6], (DEPTH, DEC_BATCH, ML_HEADS), 1.0),
        "cache_ffn_conv": nrm(ks[7], (DEPTH, DEC_BATCH, CONV_W - 1, D_FF), 1.0),
        "norm_mix_w": 1.0 + nrm(ks[8], (DEPTH, D_MODEL), 0.02),
        "w_in": nrm(ks[9], (DEPTH, D_MODEL, IN_WIDTH), D_MODEL ** -0.5),
        "b_gates": b_gates,
        "q_norm_w": 1.0 + nrm(ks[13], (DEPTH, ATT_HEAD_DIM), 0.02),
        "k_norm_w": 1.0 + nrm(ks[14], (DEPTH, ATT_HEAD_DIM), 0.02),
        "sinks": nrm(ks[15], (DEPTH, ATT_HEADS), 0.5),
        "ml_norm_w": 1.0 + nrm(ks[16], (DEPTH, ML_V_W), 0.02),
        "w_out": nrm(ks[17], (DEPTH, MIX_WIDTH, D_MODEL), MIX_WIDTH ** -0.5),
        "norm_ffn_w": 1.0 + nrm(ks[18], (DEPTH, D_MODEL), 0.02),
        "w_ffn_in": nrm(ks[19], (DEPTH, D_MODEL, 2 * D_FF), D_MODEL ** -0.5),
        "conv_w": nrm(ks[20], (DEPTH, CONV_W, D_FF), CONV_W ** -0.5),
        "conv_b": nrm(ks[21], (DEPTH, D_FF), 0.02),
        "w_down": nrm(ks[22], (DEPTH, D_FF, D_MODEL), D_FF ** -0.5),
    }


def reference(x_prompt, x_sample, cache_attn_k, cache_attn_v, state_mlstm_C, state_mlstm_n,
              state_mlstm_m, cache_ffn_conv, norm_mix_w, w_in, b_gates, q_norm_w, k_norm_w,
              sinks, ml_norm_w, w_out, norm_ffn_w, w_ffn_in, conv_w, conv_b, w_down):
    yp, ys = x_prompt, x_sample
    sp = [[] for _ in range(6)]
    ss = [[] for _ in range(6)]
    for l in range(DEPTH):
        wl = (norm_mix_w[l], w_in[l], b_gates[l], q_norm_w[l], k_norm_w[l], sinks[l], ml_norm_w[l],
              w_out[l], norm_ffn_w[l], w_ffn_in[l], conv_w[l], conv_b[l], w_down[l])
        yp, *st_p = layer(yp, None, None, None, None, None, None, *wl)
        ys, *st_s = layer(ys, cache_attn_k[l], cache_attn_v[l], state_mlstm_C[l], state_mlstm_n[l],
                          state_mlstm_m[l], cache_ffn_conv[l], *wl)
        for i in range(6):
            sp[i].append(st_p[i])
            ss[i].append(st_s[i])
    k_p, v_p, C_p, n_p, m_p, conv_p = [jnp.stack(a) for a in sp]
    k_s, v_s, C_s, n_s, m_s, conv_s = [jnp.stack(a) for a in ss]
    return (yp, ys, k_p, v_p, C_p, n_p, m_p, conv_p, k_s, v_s, C_s, n_s, m_s, conv_s)
```

```python
import functools

import numpy as np
import jax
import jax.numpy as jnp
from jax import lax
from jax.experimental import pallas as pl
from jax.experimental.pallas import tpu as pltpu

F32 = jnp.float32
BF16 = jnp.bfloat16

D_MODEL = 1024
ATT_HEADS = 8
ATT_KV_HEADS = 2
ATT_HEAD_DIM = 64
ATT_GROUP = ATT_HEADS // ATT_KV_HEADS
WINDOW = 128
ML_HEADS = 4
ML_V_DIM = 128
ML_QK_DIM = 64
D_FF = 2816
CONV_W = 3
EPS = 1e-6
ATT_SCALE = ATT_HEAD_DIM ** -0.5
ML_SCALE = ML_QK_DIM ** -0.5

ATT_Q_W = ATT_HEADS * ATT_HEAD_DIM
ATT_KV_W = ATT_KV_HEADS * ATT_HEAD_DIM
ML_QK_W = ML_HEADS * ML_QK_DIM
ML_V_W = ML_HEADS * ML_V_DIM
N_GATES = 2 * ML_HEADS

LANES = 128
SUBLANES = 8

OFF_QA = 0
OFF_KV = OFF_QA + ATT_Q_W
OFF_QM = OFF_KV + 2 * ATT_KV_W
OFF_KM = OFF_QM + ML_QK_W
OFF_VM = OFF_KM + ML_QK_W
OFF_OM = OFF_VM + ML_V_W
OFF_GL = OFF_OM + ML_V_W
IN_WIDTH_PAD = OFF_GL + LANES

ATT_HEAD_ORDER = tuple(h for c in range(ATT_GROUP) for h in (c, c + ATT_GROUP))

ROW_TILE = 512
MIX_TILE = 256
CHUNK = 128
FF_CHUNK = 256
SAMPLE_BT = 16
VMEM_LIMIT = 56 * 1024 * 1024


def _dot(a, b):
    return jnp.dot(a, b, preferred_element_type=F32)


def _dot_nt(a, b):
    return lax.dot_general(a, b, (((1,), (1,)), ((), ())), preferred_element_type=F32)


def _dot_tn(a, b):
    return lax.dot_general(a, b, (((0,), (0,)), ((), ())), preferred_element_type=F32)


def _split3(x):
    hi = x.astype(BF16)
    r1 = x - hi.astype(F32)
    mid = r1.astype(BF16)
    lo = (r1 - mid.astype(F32)).astype(BF16)
    return hi, mid, lo


def _rms(x, w):
    ms = jnp.mean(x * x, axis=-1, keepdims=True)
    return x * lax.rsqrt(ms + EPS) * w


def _log_sigmoid(x):
    return jnp.minimum(x, 0.0) - jnp.log1p(jnp.exp(-jnp.abs(x)))


def _sigmoid(x):
    return 1.0 / (1.0 + jnp.exp(-x))


def _inproj_kernel(x_ref, nw_ref, w_ref, bg_ref, qnw_ref, knw_ref, gq_ref, gk_ref,
                   qa_ref, kv_ref, qm_ref, km_ref, vm_ref, om_ref, gt_ref):
    h = _rms(x_ref[...], nw_ref[...]).astype(BF16)

    def proj(lo, width):
        return _dot(h, w_ref[:, lo:lo + width])

    q = proj(OFF_QA, ATT_Q_W)
    q_ms = _dot((q * q).astype(BF16), gq_ref[...])
    qa_ref[...] = (q * lax.rsqrt(q_ms + EPS) * qnw_ref[...]).astype(BF16)

    kv = proj(OFF_KV, 2 * ATT_KV_W)
    k = kv[:, :ATT_KV_W]
    k_ms = _dot((k * k).astype(BF16), gk_ref[...])
    kv_ref[:, :ATT_KV_W] = k * lax.rsqrt(k_ms + EPS) * knw_ref[...]
    kv_ref[:, ATT_KV_W:] = kv[:, ATT_KV_W:]

    qm_ref[...] = (proj(OFF_QM, ML_QK_W) * ML_SCALE).astype(BF16)
    km_ref[...] = proj(OFF_KM, ML_QK_W).astype(BF16)
    vm_ref[...] = proj(OFF_VM, ML_V_W).astype(BF16)
    om_ref[...] = proj(OFF_OM, ML_V_W).astype(BF16)

    gl = proj(OFF_GL, LANES) + bg_ref[...]
    lane = lax.broadcasted_iota(jnp.int32, gl.shape, 1)
    g = jnp.where(lane < ML_HEADS, gl, _log_sigmoid(gl))
    gt_ref[...] = g.T[:N_GATES, :]


def _inproj(x2d, nw, w_in, bg, qnw, knw, gq, gk):
    n = x2d.shape[0]
    tm = ROW_TILE
    row = lambda w: pl.BlockSpec((tm, w), lambda i: (i, 0))
    full = lambda a: pl.BlockSpec(a.shape, lambda i: (0,) * a.ndim)
    return pl.pallas_call(
        _inproj_kernel,
        grid=(n // tm,),
        in_specs=[row(D_MODEL), full(nw), full(w_in), full(bg), full(qnw), full(knw),
                  full(gq), full(gk)],
        out_specs=[row(ATT_Q_W), row(2 * ATT_KV_W), row(ML_QK_W), row(ML_QK_W),
                   row(ML_V_W), row(ML_V_W), pl.BlockSpec((N_GATES, tm), lambda i: (0, i))],
        out_shape=[jax.ShapeDtypeStruct((n, ATT_Q_W), BF16),
                   jax.ShapeDtypeStruct((n, 2 * ATT_KV_W), F32),
                   jax.ShapeDtypeStruct((n, ML_QK_W), BF16),
                   jax.ShapeDtypeStruct((n, ML_QK_W), BF16),
                   jax.ShapeDtypeStruct((n, ML_V_W), BF16),
                   jax.ShapeDtypeStruct((n, ML_V_W), BF16),
                   jax.ShapeDtypeStruct((N_GATES, n), F32)],
        compiler_params=pltpu.CompilerParams(dimension_semantics=("arbitrary",),
                                             vmem_limit_bytes=VMEM_LIMIT),
        name="inproj",
    )(x2d, nw, w_in, bg, qnw, knw, gq, gk)


def _gate_forms(gates, seg_mask):
    L = gates.shape[1]
    m_bf = seg_mask.astype(F32).astype(BF16)
    r = lax.broadcasted_iota(jnp.int32, (L, L), 0)
    c = lax.broadcasted_iota(jnp.int32, (L, L), 1)
    eye = (r == c).astype(F32).astype(BF16)
    cum_row = jnp.zeros(gates.shape, F32)
    cum_col = jnp.zeros((L, gates.shape[0]), F32)
    raw_col = jnp.zeros((L, gates.shape[0]), F32)
    for part in _split3(gates):
        cum_row = cum_row + _dot_nt(part, m_bf)
        cum_col = cum_col + _dot_nt(m_bf, part)
        raw_col = raw_col + _dot_nt(eye, part)
    return cum_row, cum_col, raw_col


def _mlstm_intra(q_pad, k_pair, v_ext, seg_mask, b_c, b_r, ig_r, m_prev_c):
    dm = jnp.where(seg_mask, b_c - b_r + ig_r, -jnp.inf)
    inter = b_c + m_prev_c
    m_row = jnp.maximum(inter, jnp.max(dm, axis=-1, keepdims=True))
    w_inter = jnp.exp(inter - m_row)
    p = _dot_nt(q_pad, k_pair) * jnp.exp(dm - m_row)
    return _dot(p.astype(BF16), v_ext), m_row, w_inter


def _mlstm_out(pv, m_row, w_inter, q_c, q_n, mlnw_h, om_h):
    num = pv[:, :ML_V_DIM] + w_inter * q_c
    den = pv[:, ML_V_DIM:ML_V_DIM + 1] + w_inter * q_n
    hh = num / jnp.maximum(jnp.abs(den), jnp.exp(-m_row))
    return (_rms(hh, mlnw_h) * _sigmoid(om_h.astype(F32))).astype(BF16)


def _ones_col(rows):
    lane = lax.broadcasted_iota(jnp.int32, (rows, LANES), 1)
    return (lane == 0).astype(F32).astype(BF16)


def _prompt_mixer_kernel(sinks_ref, qa_ref, kvc_ref, kvp_ref, qm_ref, km_ref, vm_ref,
                         om_ref, gt_ref, x_ref, wout_ref, mlnw_ref,
                         x1_ref, c_ref, n_ref, m_ref, mix_scr, state_scr, m_scr):
    i = pl.program_id(1)
    L = CHUNK
    n_pairs = ML_HEADS // 2

    @pl.when(i == 0)
    def _():
        state_scr[...] = jnp.zeros(state_scr.shape, F32)
        m_scr[...] = jnp.zeros(m_scr.shape, F32)

    lane = lax.broadcasted_iota(jnp.int32, (L, LANES), 1)
    low = lane < ATT_HEAD_DIM
    ones_col = _ones_col(L)
    r = lax.broadcasted_iota(jnp.int32, (L, L), 0)
    c = lax.broadcasted_iota(jnp.int32, (L, L), 1)
    causal = c <= r

    qi = lax.broadcasted_iota(jnp.int32, (2 * L, 2 * L), 0) % L
    kj = lax.broadcasted_iota(jnp.int32, (2 * L, 2 * L), 1)
    band = (kj > qi) & (kj <= qi + WINDOW)
    top = lax.broadcasted_iota(jnp.int32, (2 * L, 1), 0) < L

    for j in range(MIX_TILE // L):
        rows = slice(j * L, (j + 1) * L)

        kv_cur = kvc_ref[rows, :]
        if j == 0:
            kv_prev = kvp_ref[...]
            band_j = band & jnp.logical_or(i > 0, kj >= L)
        else:
            kv_prev = kvc_ref[(j - 1) * L:j * L, :]
            band_j = band
        kk = jnp.concatenate([kv_prev[:, :ATT_KV_W], kv_cur[:, :ATT_KV_W]], axis=0).astype(BF16)
        vv = jnp.concatenate([kv_prev[:, ATT_KV_W:], kv_cur[:, ATT_KV_W:]], axis=0).astype(BF16)
        vvx = jnp.concatenate([vv, _ones_col(2 * L)], axis=1)
        for col in range(ATT_GROUP):
            qc = qa_ref[rows, col * LANES:(col + 1) * LANES]
            zero = jnp.zeros_like(qc)
            q2 = jnp.concatenate([jnp.where(low, qc, zero), jnp.where(low, zero, qc)], axis=0)
            sink = jnp.where(top, sinks_ref[ATT_HEAD_ORDER[2 * col]],
                             sinks_ref[ATT_HEAD_ORDER[2 * col + 1]])
            s = jnp.where(band_j, _dot_nt(q2, kk), -jnp.inf)
            m = jnp.maximum(jnp.max(s, axis=-1, keepdims=True), sink)
            e = jnp.exp(s - m)
            ox = _dot(e.astype(BF16), vvx)
            o = ox[:, :LANES] / (ox[:, LANES:LANES + 1] + jnp.exp(sink - m))
            mix_scr[rows, col * LANES:(col + 1) * LANES] = jnp.where(
                low, o[:L], o[L:]).astype(BF16)

        gates = gt_ref[:, rows]
        cum_row, cum_col, raw_col = _gate_forms(gates, causal)
        for p in range(n_pairs):
            qc = qm_ref[rows, p * LANES:(p + 1) * LANES]
            k_pair = km_ref[rows, p * LANES:(p + 1) * LANES]
            zero = jnp.zeros_like(qc)
            state = state_scr[p]
            state_bf = state.astype(BF16)
            upd = []
            scale = []
            for e in range(2):
                h = 2 * p + e
                q_pad = jnp.where(low, qc, zero) if e == 0 else jnp.where(low, zero, qc)
                v_h = vm_ref[rows, h * ML_V_DIM:(h + 1) * ML_V_DIM]
                v_ext = jnp.concatenate([v_h, ones_col], axis=1)
                b_c = cum_col[:, ML_HEADS + h:ML_HEADS + h + 1]
                b_r = cum_row[ML_HEADS + h:ML_HEADS + h + 1, :]
                ig_r = gates[h:h + 1, :]
                ig_c = raw_col[:, h:h + 1]
                m_prev = m_scr[h:h + 1, 0:1]
                pv, m_row, w_inter = _mlstm_intra(q_pad, k_pair, v_ext, causal, b_c, b_r, ig_r,
                                                  jnp.broadcast_to(m_prev, (L, 1)))
                qs = _dot_nt(q_pad, state_bf)
                mix_scr[rows, ATT_Q_W + h * ML_V_DIM:ATT_Q_W + (h + 1) * ML_V_DIM] = _mlstm_out(
                    pv, m_row, w_inter, qs[:, :ML_V_DIM], qs[:, ML_V_DIM:ML_V_DIM + 1],
                    mlnw_ref[:, h * ML_V_DIM:(h + 1) * ML_V_DIM],
                    om_ref[rows, h * ML_V_DIM:(h + 1) * ML_V_DIM])
                b_last = b_c[L - 1:L, :]
                a_c = b_last - b_c + ig_c
                m_new = jnp.maximum(b_last + m_prev, jnp.max(a_c, axis=0, keepdims=True))
                sc = jnp.exp(b_last + m_prev - m_new)
                wsv = (jnp.exp(a_c - m_new) * v_ext.astype(F32)).astype(BF16)
                upd.append(_dot_tn(wsv, k_pair))
                scale.append(sc)
                m_scr[h:h + 1, :] = jnp.broadcast_to(m_new, (1, LANES))
            low2 = lax.broadcasted_iota(jnp.int32, state.shape, 1) < ML_QK_DIM
            state_scr[p] = (jnp.where(low2, scale[0], scale[1]) * state
                            + jnp.where(low2, upd[0], upd[1]))

    x1_ref[...] = x_ref[...] + _dot(mix_scr[...], wout_ref[...])

    @pl.when(i == pl.num_programs(1) - 1)
    def _():
        for h in range(ML_HEADS):
            p, e = divmod(h, 2)
            c_ref[0, h] = state_scr[p, :ML_V_DIM, e * ML_QK_DIM:(e + 1) * ML_QK_DIM]
            n_ref[0, h:h + 1, :] = state_scr[p, ML_V_DIM:ML_V_DIM + 1,
                                             e * ML_QK_DIM:(e + 1) * ML_QK_DIM]
            m_ref[0, :, h:h + 1] = m_scr[h:h + 1, 0:1]


def _prompt_mixer(batch, seq, sinks, qa, kv, qm, km, vm, om, gt, x2d, wout, mlnw):
    tq = MIX_TILE
    nt = seq // tq
    sub = tq // CHUNK
    row = lambda w: pl.BlockSpec((tq, w), lambda b, i: (b * nt + i, 0))
    full = lambda a: pl.BlockSpec(a.shape, lambda b, i: (0,) * a.ndim)
    prev = pl.BlockSpec((CHUNK, 2 * ATT_KV_W),
                        lambda b, i: (jnp.maximum((b * nt + i) * sub - 1, 0), 0))
    return pl.pallas_call(
        _prompt_mixer_kernel,
        grid=(batch, nt),
        in_specs=[pl.BlockSpec(memory_space=pltpu.SMEM),
                  row(ATT_Q_W), row(2 * ATT_KV_W), prev, row(ML_QK_W), row(ML_QK_W),
                  row(ML_V_W), row(ML_V_W),
                  pl.BlockSpec((N_GATES, tq), lambda b, i: (0, b * nt + i)),
                  row(D_MODEL), full(wout), full(mlnw)],
        out_specs=[row(D_MODEL),
                   pl.BlockSpec((1, ML_HEADS, ML_V_DIM, ML_QK_DIM), lambda b, i: (b, 0, 0, 0)),
                   pl.BlockSpec((1, ML_HEADS, ML_QK_DIM), lambda b, i: (b, 0, 0)),
                   pl.BlockSpec((1, 1, ML_HEADS), lambda b, i: (b, 0, 0))],
        out_shape=[jax.ShapeDtypeStruct((batch * seq, D_MODEL), F32),
                   jax.ShapeDtypeStruct((batch, ML_HEADS, ML_V_DIM, ML_QK_DIM), F32),
                   jax.ShapeDtypeStruct((batch, ML_HEADS, ML_QK_DIM), F32),
                   jax.ShapeDtypeStruct((batch, 1, ML_HEADS), F32)],
        scratch_shapes=[pltpu.VMEM((tq, D_MODEL), BF16),
                        pltpu.VMEM((ML_HEADS // 2, 2 * LANES, LANES), F32),
                        pltpu.VMEM((SUBLANES, LANES), F32)],
        compiler_params=pltpu.CompilerParams(dimension_semantics=("arbitrary", "arbitrary"),
                                             vmem_limit_bytes=VMEM_LIMIT),
        name="prompt_mixer",
    )(sinks, qa, kv, kv, qm, km, vm, om, gt, x2d, wout, mlnw)


def _sample_mixer_kernel(t_len, sinks_ref, qa_ref, kv_ref, ck_ref, cv_ref, qm_ref, km_ref,
                         vm_ref, om_ref, gt_ref, c0_ref, n0_ref, m0_ref, x_ref, wout_ref,
                         mlnw_ref, x1_ref, nk_ref, nv_ref, c_ref, n_ref, m_ref, mix_scr):
    bt = SAMPLE_BT
    T = t_len
    L = bt * T
    n_stack = 2 * ATT_GROUP
    lane3 = lax.broadcasted_iota(jnp.int32, (bt, T, LANES), 2)
    low3 = lane3 < ATT_HEAD_DIM
    lane = lax.broadcasted_iota(jnp.int32, (L, LANES), 1)
    low = lane < ATT_HEAD_DIM

    qa3 = qa_ref[...].astype(F32).reshape(bt, T, ATT_Q_W)
    pieces = []
    for col in range(ATT_GROUP):
        qc = qa3[:, :, col * LANES:(col + 1) * LANES]
        pieces += [jnp.where(low3, qc, 0.0), jnp.where(low3, 0.0, qc)]
    q3 = jnp.concatenate(pieces, axis=1).astype(BF16)
    R = bt * n_stack * T
    q2 = q3.reshape(R, LANES)
    kv_new = kv_ref[...]
    k_new = kv_new[:, :ATT_KV_W]
    v_new = kv_new[:, ATT_KV_W:]
    ck = ck_ref[...]
    cv = cv_ref[...]
    s_c = jnp.einsum('bqd,bkd->bqk', q3, ck.astype(BF16),
                     preferred_element_type=F32).reshape(R, WINDOW)
    s_n = _dot_nt(q2, k_new.astype(BF16))
    row_c = lax.broadcasted_iota(jnp.int32, (R, WINDOW), 0)
    col_c = lax.broadcasted_iota(jnp.int32, (R, WINDOW), 1)
    s_c = jnp.where(col_c > row_c % T, s_c, -jnp.inf)
    row_n = lax.broadcasted_iota(jnp.int32, (R, L), 0)
    col_n = lax.broadcasted_iota(jnp.int32, (R, L), 1)
    valid_n = (row_n // (n_stack * T) == col_n // T) & (col_n % T <= row_n % T)
    s_n = jnp.where(valid_n, s_n, -jnp.inf)
    stack_id = (lax.broadcasted_iota(jnp.int32, (R, 1), 0) // T) % n_stack
    sink = jnp.zeros((R, 1), F32)
    for k_id in range(n_stack):
        sink = jnp.where(stack_id == k_id, sinks_ref[ATT_HEAD_ORDER[k_id]], sink)
    m = jnp.maximum(jnp.maximum(jnp.max(s_c, axis=-1, keepdims=True),
                                jnp.max(s_n, axis=-1, keepdims=True)), sink)
    e_c = jnp.exp(s_c - m)
    e_n = jnp.exp(s_n - m)
    denom = (jnp.sum(e_c, axis=-1, keepdims=True) + jnp.sum(e_n, axis=-1, keepdims=True)
             + jnp.exp(sink - m))
    o = jnp.einsum('bqk,bkd->bqd', e_c.astype(BF16).reshape(bt, n_stack * T, WINDOW),
                   cv.astype(BF16), preferred_element_type=F32).reshape(R, LANES)
    o = (o + _dot(e_n.astype(BF16), v_new.astype(BF16))) / denom
    o3 = o.reshape(bt, n_stack * T, LANES)
    for col in range(ATT_GROUP):
        lo_h = o3[:, (2 * col) * T:(2 * col + 1) * T, :]
        hi_h = o3[:, (2 * col + 1) * T:(2 * col + 2) * T, :]
        mix_scr[:, col * LANES:(col + 1) * LANES] = jnp.where(
            low3, lo_h, hi_h).reshape(L, LANES).astype(BF16)

    nk_ref[:, :WINDOW - T, :] = ck[:, T:, :]
    nk_ref[:, WINDOW - T:, :] = k_new.reshape(bt, T, ATT_KV_W)
    nv_ref[:, :WINDOW - T, :] = cv[:, T:, :]
    nv_ref[:, WINDOW - T:, :] = v_new.reshape(bt, T, ATT_KV_W)

    r = lax.broadcasted_iota(jnp.int32, (L, L), 0)
    c = lax.broadcasted_iota(jnp.int32, (L, L), 1)
    seg = (r // T == c // T) & (c <= r)
    gates = gt_ref[...]
    cum_row, cum_col, raw_col = _gate_forms(gates, seg)
    ones_col = _ones_col(L)
    qm = qm_ref[...]
    km = km_ref[...]
    qm_f = qm.astype(F32)
    km_f = km.astype(F32)
    for h in range(ML_HEADS):
        p, e = divmod(h, 2)
        qc = qm[:, p * LANES:(p + 1) * LANES]
        k_pair = km[:, p * LANES:(p + 1) * LANES]
        zero = jnp.zeros_like(qc)
        q_pad = jnp.where(low, qc, zero) if e == 0 else jnp.where(low, zero, qc)
        v_h = vm_ref[:, h * ML_V_DIM:(h + 1) * ML_V_DIM]
        v_ext = jnp.concatenate([v_h, ones_col], axis=1)
        b_c = cum_col[:, ML_HEADS + h:ML_HEADS + h + 1]
        b_r = cum_row[ML_HEADS + h:ML_HEADS + h + 1, :]
        ig_r = gates[h:h + 1, :]
        ig_c = raw_col[:, h:h + 1]
        m0 = m0_ref[:, :, h:h + 1]
        m_prev_c = jnp.broadcast_to(m0, (bt, T, 1)).reshape(L, 1)
        pv, m_row, w_inter = _mlstm_intra(q_pad, k_pair, v_ext, seg, b_c, b_r, ig_r, m_prev_c)
        q_h3 = qm_f[:, h * ML_QK_DIM:(h + 1) * ML_QK_DIM].reshape(bt, T, ML_QK_DIM)
        k_h3 = km_f[:, h * ML_QK_DIM:(h + 1) * ML_QK_DIM].reshape(bt, T, ML_QK_DIM)
        c0 = c0_ref[:, h]
        n0 = n0_ref[:, h:h + 1, :]
        q_c = jnp.einsum('btd,bvd->btv', q_h3.astype(BF16), c0.astype(BF16),
                         preferred_element_type=F32).reshape(L, ML_V_DIM)
        q_n = jnp.sum(q_h3 * n0, axis=-1, keepdims=True).reshape(L, 1)
        mix_scr[:, ATT_Q_W + h * ML_V_DIM:ATT_Q_W + (h + 1) * ML_V_DIM] = _mlstm_out(
            pv, m_row, w_inter, q_c, q_n, mlnw_ref[:, h * ML_V_DIM:(h + 1) * ML_V_DIM],
            om_ref[:, h * ML_V_DIM:(h + 1) * ML_V_DIM])
        b3 = b_c.reshape(bt, T, 1)
        b_last = b3[:, T - 1:T, :]
        a3 = b_last - b3 + ig_c.reshape(bt, T, 1)
        m_new = jnp.maximum(b_last + m0, jnp.max(a3, axis=1, keepdims=True))
        sc = jnp.exp(b_last + m0 - m_new)
        ws = jnp.exp(a3 - m_new)
        wsv = (ws * v_h.astype(F32).reshape(bt, T, ML_V_DIM)).astype(BF16)
        d_c = jnp.einsum('bsv,bsd->bvd', wsv, k_h3.astype(BF16), preferred_element_type=F32)
        c_ref[:, h] = sc * c0 + d_c
        n_ref[:, h:h + 1, :] = sc * n0 + jnp.sum(ws * k_h3, axis=1, keepdims=True)
        m_ref[:, :, h:h + 1] = m_new

    x1_ref[...] = x_ref[...] + _dot(mix_scr[...], wout_ref[...])


def _sample_mixer(nb, t_len, sinks, qa, kv, ck, cv, qm, km, vm, om, gt, c0, n0, m0, x2d, wout, mlnw):
    bt = SAMPLE_BT
    tl = bt * t_len
    row = lambda w: pl.BlockSpec((tl, w), lambda i: (i, 0))
    full = lambda a: pl.BlockSpec(a.shape, lambda i: (0,) * a.ndim)
    cache = pl.BlockSpec((bt, WINDOW, ATT_KV_W), lambda i: (i, 0, 0))
    c_spec = pl.BlockSpec((bt, ML_HEADS, ML_V_DIM, ML_QK_DIM), lambda i: (i, 0, 0, 0))
    n_spec = pl.BlockSpec((bt, ML_HEADS, ML_QK_DIM), lambda i: (i, 0, 0))
    m_spec = pl.BlockSpec((bt, 1, ML_HEADS), lambda i: (i, 0, 0))
    return pl.pallas_call(
        functools.partial(_sample_mixer_kernel, t_len),
        grid=(nb // bt,),
        in_specs=[pl.BlockSpec(memory_space=pltpu.SMEM),
                  row(ATT_Q_W), row(2 * ATT_KV_W), cache, cache, row(ML_QK_W), row(ML_QK_W),
                  row(ML_V_W), row(ML_V_W), pl.BlockSpec((N_GATES, tl), lambda i: (0, i)),
                  c_spec, n_spec, m_spec, row(D_MODEL), full(wout), full(mlnw)],
        out_specs=[row(D_MODEL), cache, cache, c_spec, n_spec, m_spec],
        out_shape=[jax.ShapeDtypeStruct((nb * t_len, D_MODEL), F32),
                   jax.ShapeDtypeStruct((nb, WINDOW, ATT_KV_W), F32),
                   jax.ShapeDtypeStruct((nb, WINDOW, ATT_KV_W), F32),
                   jax.ShapeDtypeStruct((nb, ML_HEADS, ML_V_DIM, ML_QK_DIM), F32),
                   jax.ShapeDtypeStruct((nb, ML_HEADS, ML_QK_DIM), F32),
                   jax.ShapeDtypeStruct((nb, 1, ML_HEADS), F32)],
        scratch_shapes=[pltpu.VMEM((tl, D_MODEL), BF16)],
        compiler_params=pltpu.CompilerParams(dimension_semantics=("arbitrary",),
                                             vmem_limit_bytes=VMEM_LIMIT),
        name="sample_mixer",
    )(sinks, qa, kv, ck, cv, qm, km, vm, om, gt, c0, n0, m0, x2d, wout, mlnw)


def _ffn_kernel(seq_rows, *refs):
    if seq_rows is None:
        (x_ref, nw_ref, wg_ref, wu_ref, cw_ref, cb_ref, wd_ref, y_ref, conv_ref, carry) = refs
        hist_ref = None
    else:
        (x_ref, hist_ref, nw_ref, wg_ref, wu_ref, cw_ref, cb_ref, wd_ref, y_ref, conv_ref) = refs
        carry = None
    tm = x_ref.shape[0]
    tf = FF_CHUNK
    n_hist = CONV_W - 1

    if carry is not None:
        @pl.when(pl.program_id(1) == 0)
        def _():
            carry[...] = jnp.zeros(carry.shape, F32)

    x = x_ref[...]
    h2 = _rms(x, nw_ref[...]).astype(BF16)
    row = lax.broadcasted_iota(jnp.int32, (tm, tf), 0)
    pos = row if seq_rows is None else row % seq_rows
    acc = x
    for f in range(D_FF // tf):
        cols = slice(f * tf, (f + 1) * tf)
        g = _dot(h2, wg_ref[:, cols])
        u = _dot(h2, wu_ref[:, cols])
        if seq_rows is None:
            hist = [jnp.broadcast_to(carry[SUBLANES - n_hist + k:SUBLANES - n_hist + k + 1, cols],
                                     (tm, tf)) for k in range(n_hist)]
            carry[SUBLANES - n_hist:, cols] = g[tm - n_hist:, :]
        else:
            nb = tm // seq_rows
            hist = [jnp.broadcast_to(hist_ref[:, k:k + 1, cols], (nb, seq_rows, tf)).reshape(tm, tf)
                    for k in range(n_hist)]
            conv_ref[:, :, cols] = g.reshape(nb, seq_rows, tf)[:, seq_rows - n_hist:, :]
        gc = cb_ref[:, cols] + g * cw_ref[CONV_W - 1:CONV_W, cols]
        for d in range(1, CONV_W):
            gm = pltpu.roll(g, d, axis=0)
            for t in range(d):
                gm = jnp.where(pos == t, hist[n_hist - d + t], gm)
            gc = gc + gm * cw_ref[CONV_W - 1 - d:CONV_W - d, cols]
        act = (gc * _sigmoid(gc) * u).astype(BF16)
        acc = acc + _dot(act, wd_ref[cols, :])
    y_ref[...] = acc

    if carry is not None:
        @pl.when(pl.program_id(1) == pl.num_programs(1) - 1)
        def _():
            conv_ref[0] = carry[SUBLANES - n_hist:, :]


def _ffn_prompt(batch, seq, x2d, nw, wg, wu, cw, cb, wd):
    tm = ROW_TILE
    nt = seq // tm
    full = lambda a: pl.BlockSpec(a.shape, lambda b, i: (0,) * a.ndim)
    row = pl.BlockSpec((tm, D_MODEL), lambda b, i: (b * nt + i, 0))
    return pl.pallas_call(
        functools.partial(_ffn_kernel, None),
        grid=(batch, nt),
        in_specs=[row, full(nw), full(wg), full(wu), full(cw), full(cb), full(wd)],
        out_specs=[row, pl.BlockSpec((1, CONV_W - 1, D_FF), lambda b, i: (b, 0, 0))],
        out_shape=[jax.ShapeDtypeStruct((batch * seq, D_MODEL), F32),
                   jax.ShapeDtypeStruct((batch, CONV_W - 1, D_FF), F32)],
        scratch_shapes=[pltpu.VMEM((SUBLANES, D_FF), F32)],
        compiler_params=pltpu.CompilerParams(dimension_semantics=("arbitrary", "arbitrary"),
                                             vmem_limit_bytes=VMEM_LIMIT),
        name="ffn_prompt",
    )(x2d, nw, wg, wu, cw, cb, wd)


def _ffn_sample(nb, t_len, x2d, hist, nw, wg, wu, cw, cb, wd):
    tm = ROW_TILE
    bt = tm // t_len
    full = lambda a: pl.BlockSpec(a.shape, lambda i: (0,) * a.ndim)
    row = pl.BlockSpec((tm, D_MODEL), lambda i: (i, 0))
    hist_spec = pl.BlockSpec((bt, CONV_W - 1, D_FF), lambda i: (i, 0, 0))
    return pl.pallas_call(
        functools.partial(_ffn_kernel, t_len),
        grid=(nb // bt,),
        in_specs=[row, hist_spec, full(nw), full(wg), full(wu), full(cw), full(cb), full(wd)],
        out_specs=[row, hist_spec],
        out_shape=[jax.ShapeDtypeStruct((nb * t_len, D_MODEL), F32),
                   jax.ShapeDtypeStruct((nb, CONV_W - 1, D_FF), F32)],
        compiler_params=pltpu.CompilerParams(dimension_semantics=("arbitrary",),
                                             vmem_limit_bytes=VMEM_LIMIT),
        name="ffn_sample",
    )(x2d, hist, nw, wg, wu, cw, cb, wd)


def _head_mean_matrix(width, head_dim):
    idx = np.arange(width) // head_dim
    return jnp.asarray((idx[:, None] == idx[None, :]).astype(np.float32) / head_dim, dtype=BF16)


def _layer_weights(norm_mix_w, w_in, b_gates, q_norm_w, k_norm_w, sinks, ml_norm_w, w_out,
                   norm_ffn_w, w_ffn_in, conv_w, conv_b, w_down):
    head_cols = np.concatenate([np.arange(h * ATT_HEAD_DIM, (h + 1) * ATT_HEAD_DIM)
                                for h in ATT_HEAD_ORDER])
    col_perm = np.concatenate([head_cols, np.arange(ATT_Q_W, w_in.shape[1])])
    w_in_p = jnp.pad(w_in[:, col_perm], ((0, 0), (0, IN_WIDTH_PAD - w_in.shape[1]))).astype(BF16)
    row_perm = np.concatenate([head_cols, np.arange(ATT_Q_W, w_out.shape[0])])
    return dict(
        nw=norm_mix_w.reshape(1, D_MODEL),
        w_in=w_in_p,
        bg=jnp.pad(b_gates, (0, LANES - N_GATES)).reshape(1, LANES),
        qnw=(jnp.tile(q_norm_w, ATT_HEADS) * ATT_SCALE).reshape(1, ATT_Q_W),
        knw=jnp.tile(k_norm_w, ATT_KV_HEADS).reshape(1, ATT_KV_W),
        gq=_head_mean_matrix(ATT_Q_W, ATT_HEAD_DIM),
        gk=_head_mean_matrix(ATT_KV_W, ATT_HEAD_DIM),
        sinks=sinks,
        mlnw=ml_norm_w.reshape(1, ML_V_W),
        wout=w_out[row_perm].astype(BF16),
        nfw=norm_ffn_w.reshape(1, D_MODEL),
        wg=w_ffn_in[:, :D_FF].astype(BF16),
        wu=w_ffn_in[:, D_FF:].astype(BF16),
        cw=conv_w,
        cb=conv_b.reshape(1, D_FF),
        wd=w_down.astype(BF16),
    )


def _prompt_layer(x, w):
    batch, seq, _ = x.shape
    x2d = x.reshape(batch * seq, D_MODEL)
    qa, kv, qm, km, vm, om, gt = _inproj(x2d, w["nw"], w["w_in"], w["bg"], w["qnw"], w["knw"],
                                         w["gq"], w["gk"])
    x1, c, n, m = _prompt_mixer(batch, seq, w["sinks"], qa, kv, qm, km, vm, om, gt, x2d,
                                w["wout"], w["mlnw"])
    y, conv = _ffn_prompt(batch, seq, x1, w["nfw"], w["wg"], w["wu"], w["cw"], w["cb"], w["wd"])
    kv_tail = kv.reshape(batch, seq, 2, ATT_KV_HEADS, ATT_HEAD_DIM)[:, seq - WINDOW:]
    return (y.reshape(batch, seq, D_MODEL), kv_tail[:, :, 0], kv_tail[:, :, 1], c, n,
            m.reshape(batch, ML_HEADS), conv)


def _sample_layer(x, ck, cv, c0, n0, m0, conv_buf, w):
    nb, t_len, _ = x.shape
    x2d = x.reshape(nb * t_len, D_MODEL)
    qa, kv, qm, km, vm, om, gt = _inproj(x2d, w["nw"], w["w_in"], w["bg"], w["qnw"], w["knw"],
                                         w["gq"], w["gk"])
    x1, nk, nv, c, n, m = _sample_mixer(
        nb, t_len, w["sinks"], qa, kv, ck.reshape(nb, WINDOW, ATT_KV_W),
        cv.reshape(nb, WINDOW, ATT_KV_W), qm, km, vm, om, gt, c0, n0,
        m0.reshape(nb, 1, ML_HEADS), x2d, w["wout"], w["mlnw"])
    y, conv = _ffn_sample(nb, t_len, x1, conv_buf, w["nfw"], w["wg"], w["wu"], w["cw"],
                          w["cb"], w["wd"])
    shape5 = (nb, WINDOW, ATT_KV_HEADS, ATT_HEAD_DIM)
    return (y.reshape(nb, t_len, D_MODEL), nk.reshape(shape5), nv.reshape(shape5), c, n,
            m.reshape(nb, ML_HEADS), conv)


def kernel(x_prompt, x_sample, cache_attn_k, cache_attn_v, state_mlstm_C, state_mlstm_n,
           state_mlstm_m, cache_ffn_conv, norm_mix_w, w_in, b_gates, q_norm_w, k_norm_w,
           sinks, ml_norm_w, w_out, norm_ffn_w, w_ffn_in, conv_w, conv_b, w_down):
    depth = w_in.shape[0]
    yp, ys = x_prompt, x_sample
    sp = [[] for _ in range(6)]
    ss = [[] for _ in range(6)]
    for l in range(depth):
        w = _layer_weights(norm_mix_w[l], w_in[l], b_gates[l], q_norm_w[l], k_norm_w[l], sinks[l],
                           ml_norm_w[l], w_out[l], norm_ffn_w[l], w_ffn_in[l], conv_w[l],
                           conv_b[l], w_down[l])
        yp, *st_p = _prompt_layer(yp, w)
        ys, *st_s = _sample_layer(ys, cache_attn_k[l], cache_attn_v[l], state_mlstm_C[l],
                                  state_mlstm_n[l], state_mlstm_m[l], cache_ffn_conv[l], w)
        for i in range(6):
            sp[i].append(st_p[i])
            ss[i].append(st_s[i])
    k_p, v_p, c_p, n_p, m_p, conv_p = [jnp.stack(a) for a in sp]
    k_s, v_s, c_s, n_s, m_s, conv_s = [jnp.stack(a) for a in ss]
    return (yp, ys, k_p, v_p, c_p, n_p, m_p, conv_p, k_s, v_s, c_s, n_s, m_s, conv_s)
```

```python
import functools

import numpy as np
import jax
import jax.numpy as jnp
from jax import lax
from jax.experimental import pallas as pl
from jax.experimental.pallas import tpu as pltpu

F32 = jnp.float32
BF16 = jnp.bfloat16

D_MODEL = 1024
ATT_HEADS = 8
ATT_KV_HEADS = 2
ATT_HEAD_DIM = 64
ATT_GROUP = ATT_HEADS // ATT_KV_HEADS
WINDOW = 128
ML_HEADS = 4
ML_V_DIM = 128
ML_QK_DIM = 64
D_FF = 2816
CONV_W = 3
EPS = 1e-6
ATT_SCALE = ATT_HEAD_DIM ** -0.5
ML_SCALE = ML_QK_DIM ** -0.5

ATT_Q_W = ATT_HEADS * ATT_HEAD_DIM
ATT_KV_W = ATT_KV_HEADS * ATT_HEAD_DIM
ML_QK_W = ML_HEADS * ML_QK_DIM
ML_V_W = ML_HEADS * ML_V_DIM
N_GATES = 2 * ML_HEADS

LANES = 128
SUBLANES = 8

OFF_QA = 0
OFF_KV = OFF_QA + ATT_Q_W
OFF_QM = OFF_KV + 2 * ATT_KV_W
OFF_KM = OFF_QM + ML_QK_W
OFF_VM = OFF_KM + ML_QK_W
OFF_OM = OFF_VM + ML_V_W
OFF_GL = OFF_OM + ML_V_W
IN_WIDTH_PAD = OFF_GL + LANES

ATT_HEAD_ORDER = tuple(h for c in range(ATT_GROUP) for h in (c, c + ATT_GROUP))

ROW_TILE = 512
MIX_TILE = 256
CHUNK = 128
FF_CHUNK = 256
SAMPLE_BT = 16
VMEM_LIMIT = 56 * 1024 * 1024


def _dot(a, b):
    return jnp.dot(a, b, preferred_element_type=F32)


def _dot_nt(a, b):
    return lax.dot_general(a, b, (((1,), (1,)), ((), ())), preferred_element_type=F32)


def _dot_tn(a, b):
    return lax.dot_general(a, b, (((0,), (0,)), ((), ())), preferred_element_type=F32)


def _split3(x):
    hi = x.astype(BF16)
    r1 = x - hi.astype(F32)
    mid = r1.astype(BF16)
    lo = (r1 - mid.astype(F32)).astype(BF16)
    return hi, mid, lo


def _rms(x, w):
    ms = jnp.mean(x * x, axis=-1, keepdims=True)
    return x * lax.rsqrt(ms + EPS) * w


def _log_sigmoid(x):
    return jnp.minimum(x, 0.0) - jnp.log1p(jnp.exp(-jnp.abs(x)))


def _sigmoid(x):
    return 1.0 / (1.0 + jnp.exp(-x))


def _inproj_kernel(x_ref, nw_ref, w_ref, bg_ref, qnw_ref, knw_ref, gq_ref, gk_ref,
                   qa_ref, kv_ref, qm_ref, km_ref, vm_ref, om_ref, gt_ref):
    h = _rms(x_ref[...], nw_ref[...]).astype(BF16)

    def proj(lo, width):
        return _dot(h, w_ref[:, lo:lo + width])

    q = proj(OFF_QA, ATT_Q_W)
    q_ms = _dot((q * q).astype(BF16), gq_ref[...])
    qa_ref[...] = (q * lax.rsqrt(q_ms + EPS) * qnw_ref[...]).astype(BF16)

    kv = proj(OFF_KV, 2 * ATT_KV_W)
    k = kv[:, :ATT_KV_W]
    k_ms = _dot((k * k).astype(BF16), gk_ref[...])
    kv_ref[:, :ATT_KV_W] = k * lax.rsqrt(k_ms + EPS) * knw_ref[...]
    kv_ref[:, ATT_KV_W:] = kv[:, ATT_KV_W:]

    qm_ref[...] = (proj(OFF_QM, ML_QK_W) * ML_SCALE).astype(BF16)
    km_ref[...] = proj(OFF_KM, ML_QK_W).astype(BF16)
    vm_ref[...] = proj(OFF_VM, ML_V_W).astype(BF16)
    om_ref[...] = proj(OFF_OM, ML_V_W).astype(BF16)

    gl = proj(OFF_GL, LANES) + bg_ref[...]
    lane = lax.broadcasted_iota(jnp.int32, gl.shape, 1)
    g = jnp.where(lane < ML_HEADS, gl, _log_sigmoid(gl))
    gt_ref[...] = g.T[:N_GATES, :]


def _inproj(x2d, nw, w_in, bg, qnw, knw, gq, gk):
    n = x2d.shape[0]
    tm = ROW_TILE
    row = lambda w: pl.BlockSpec((tm, w), lambda i: (i, 0))
    full = lambda a: pl.BlockSpec(a.shape, lambda i: (0,) * a.ndim)
    return pl.pallas_call(
        _inproj_kernel,
        grid=(n // tm,),
        in_specs=[row(D_MODEL), full(nw), full(w_in), full(bg), full(qnw), full(knw),
                  full(gq), full(gk)],
        out_specs=[row(ATT_Q_W), row(2 * ATT_KV_W), row(ML_QK_W), row(ML_QK_W),
                   row(ML_V_W), row(ML_V_W), pl.BlockSpec((N_GATES, tm), lambda i: (0, i))],
        out_shape=[jax.ShapeDtypeStruct((n, ATT_Q_W), BF16),
                   jax.ShapeDtypeStruct((n, 2 * ATT_KV_W), F32),
                   jax.ShapeDtypeStruct((n, ML_QK_W), BF16),
                   jax.ShapeDtypeStruct((n, ML_QK_W), BF16),
                   jax.ShapeDtypeStruct((n, ML_V_W), BF16),
                   jax.ShapeDtypeStruct((n, ML_V_W), BF16),
                   jax.ShapeDtypeStruct((N_GATES, n), F32)],
        compiler_params=pltpu.CompilerParams(dimension_semantics=("arbitrary",),
                                             vmem_limit_bytes=VMEM_LIMIT),
        name="inproj",
    )(x2d, nw, w_in, bg, qnw, knw, gq, gk)


def _gate_forms(gates, seg_mask):
    L = gates.shape[1]
    m_bf = seg_mask.astype(F32).astype(BF16)
    r = lax.broadcasted_iota(jnp.int32, (L, L), 0)
    c = lax.broadcasted_iota(jnp.int32, (L, L), 1)
    eye = (r == c).astype(F32).astype(BF16)
    cum_row = jnp.zeros(gates.shape, F32)
    cum_col = jnp.zeros((L, gates.shape[0]), F32)
    raw_col = jnp.zeros((L, gates.shape[0]), F32)
    for part in _split3(gates):
        cum_row = cum_row + _dot_nt(part, m_bf)
        cum_col = cum_col + _dot_nt(m_bf, part)
        raw_col = raw_col + _dot_nt(eye, part)
    return cum_row, cum_col, raw_col


def _mlstm_intra(q_pad, k_pair, v_ext, seg_mask, b_c, b_r, ig_r, m_prev_c):
    dm = jnp.where(seg_mask, b_c - b_r + ig_r, -jnp.inf)
    inter = b_c + m_prev_c
    m_row = jnp.maximum(inter, jnp.max(dm, axis=-1, keepdims=True))
    w_inter = jnp.exp(inter - m_row)
    p = _dot_nt(q_pad, k_pair) * jnp.exp(dm - m_row)
    return _dot(p.astype(BF16), v_ext), m_row, w_inter


def _mlstm_out(pv, m_row, w_inter, q_c, q_n, mlnw_h, om_h):
    num = pv[:, :ML_V_DIM] + w_inter * q_c
    den = pv[:, ML_V_DIM:ML_V_DIM + 1] + w_inter * q_n
    hh = num / jnp.maximum(jnp.abs(den), jnp.exp(-m_row))
    return (_rms(hh, mlnw_h) * _sigmoid(om_h.astype(F32))).astype(BF16)


def _ones_col(rows):
    lane = lax.broadcasted_iota(jnp.int32, (rows, LANES), 1)
    return (lane == 0).astype(F32).astype(BF16)


def _prompt_mixer_kernel(sinks_ref, qa_ref, kvc_ref, kvp_ref, qm_ref, km_ref, vm_ref,
                         om_ref, gt_ref, x_ref, wout_ref, mlnw_ref,
                         x1_ref, c_ref, n_ref, m_ref, mix_scr, state_scr, m_scr):
    i = pl.program_id(1)
    L = CHUNK
    n_pairs = ML_HEADS // 2

    @pl.when(i == 0)
    def _():
        state_scr[...] = jnp.zeros(state_scr.shape, F32)
        m_scr[...] = jnp.zeros(m_scr.shape, F32)

    lane = lax.broadcasted_iota(jnp.int32, (L, LANES), 1)
    low = lane < ATT_HEAD_DIM
    ones_col = _ones_col(L)
    r = lax.broadcasted_iota(jnp.int32, (L, L), 0)
    c = lax.broadcasted_iota(jnp.int32, (L, L), 1)
    causal = c <= r

    qi = lax.broadcasted_iota(jnp.int32, (2 * L, 2 * L), 0) % L
    kj = lax.broadcasted_iota(jnp.int32, (2 * L, 2 * L), 1)
    band = (kj > qi) & (kj <= qi + WINDOW)
    top = lax.broadcasted_iota(jnp.int32, (2 * L, 1), 0) < L

    for j in range(MIX_TILE // L):
        rows = slice(j * L, (j + 1) * L)

        kv_cur = kvc_ref[rows, :]
        if j == 0:
            kv_prev = kvp_ref[...]
            band_j = band & jnp.logical_or(i > 0, kj >= L)
        else:
            kv_prev = kvc_ref[(j - 1) * L:j * L, :]
            band_j = band
        kk = jnp.concatenate([kv_prev[:, :ATT_KV_W], kv_cur[:, :ATT_KV_W]], axis=0).astype(BF16)
        vv = jnp.concatenate([kv_prev[:, ATT_KV_W:], kv_cur[:, ATT_KV_W:]], axis=0).astype(BF16)
        vvx = jnp.concatenate([vv, _ones_col(2 * L)], axis=1)
        for col in range(ATT_GROUP):
            qc = qa_ref[rows, col * LANES:(col + 1) * LANES]
            zero = jnp.zeros_like(qc)
            q2 = jnp.concatenate([jnp.where(low, qc, zero), jnp.where(low, zero, qc)], axis=0)
            sink = jnp.where(top, sinks_ref[ATT_HEAD_ORDER[2 * col]],
                             sinks_ref[ATT_HEAD_ORDER[2 * col + 1]])
            s = jnp.where(band_j, _dot_nt(q2, kk), -jnp.inf)
            m = jnp.maximum(jnp.max(s, axis=-1, keepdims=True), sink)
            e = jnp.exp(s - m)
            ox = _dot(e.astype(BF16), vvx)
            o = ox[:, :LANES] / (ox[:, LANES:LANES + 1] + jnp.exp(sink - m))
            mix_scr[rows, col * LANES:(col + 1) * LANES] = jnp.where(
                low, o[:L], o[L:]).astype(BF16)

        gates = gt_ref[:, rows]
        cum_row, cum_col, raw_col = _gate_forms(gates, causal)
        for p in range(n_pairs):
            qc = qm_ref[rows, p * LANES:(p + 1) * LANES]
            k_pair = km_ref[rows, p * LANES:(p + 1) * LANES]
            zero = jnp.zeros_like(qc)
            state = state_scr[p]
            state_bf = state.astype(BF16)
            upd = []
            scale = []
            for e in range(2):
                h = 2 * p + e
                q_pad = jnp.where(low, qc, zero) if e == 0 else jnp.where(low, zero, qc)
                v_h = vm_ref[rows, h * ML_V_DIM:(h + 1) * ML_V_DIM]
                v_ext = jnp.concatenate([v_h, ones_col], axis=1)
                b_c = cum_col[:, ML_HEADS + h:ML_HEADS + h + 1]
                b_r = cum_row[ML_HEADS + h:ML_HEADS + h + 1, :]
                ig_r = gates[h:h + 1, :]
                ig_c = raw_col[:, h:h + 1]
                m_prev = m_scr[h:h + 1, 0:1]
                pv, m_row, w_inter = _mlstm_intra(q_pad, k_pair, v_ext, causal, b_c, b_r, ig_r,
                                                  jnp.broadcast_to(m_prev, (L, 1)))
                qs = _dot_nt(q_pad, state_bf)
                mix_scr[rows, ATT_Q_W + h * ML_V_DIM:ATT_Q_W + (h + 1) * ML_V_DIM] = _mlstm_out(
                    pv, m_row, w_inter, qs[:, :ML_V_DIM], qs[:, ML_V_DIM:ML_V_DIM + 1],
                    mlnw_ref[:, h * ML_V_DIM:(h + 1) * ML_V_DIM],
                    om_ref[rows, h * ML_V_DIM:(h + 1) * ML_V_DIM])
                b_last = b_c[L - 1:L, :]
                a_c = b_last - b_c + ig_c
                m_new = jnp.maximum(b_last + m_prev, jnp.max(a_c, axis=0, keepdims=True))
                sc = jnp.exp(b_last + m_prev - m_new)
                wsv = (jnp.exp(a_c - m_new) * v_ext.astype(F32)).astype(BF16)
                upd.append(_dot_tn(wsv, k_pair))
                scale.append(sc)
                m_scr[h:h + 1, :] = jnp.broadcast_to(m_new, (1, LANES))
            low2 = lax.broadcasted_iota(jnp.int32, state.shape, 1) < ML_QK_DIM
            state_scr[p] = (jnp.where(low2, scale[0], scale[1]) * state
                            + jnp.where(low2, upd[0], upd[1]))

    x1_ref[...] = x_ref[...] + _dot(mix_scr[...], wout_ref[...])

    @pl.when(i == pl.num_programs(1) - 1)
    def _():
        for h in range(ML_HEADS):
            p, e = divmod(h, 2)
            c_ref[0, h] = state_scr[p, :ML_V_DIM, e * ML_QK_DIM:(e + 1) * ML_QK_DIM]
            n_ref[0, h:h + 1, :] = state_scr[p, ML_V_DIM:ML_V_DIM + 1,
                                             e * ML_QK_DIM:(e + 1) * ML_QK_DIM]
            m_ref[0, :, h:h + 1] = m_scr[h:h + 1, 0:1]


def _prompt_mixer(batch, seq, sinks, qa, kv, qm, km, vm, om, gt, x2d, wout, mlnw):
    tq = MIX_TILE
    nt = seq // tq
    sub = tq // CHUNK
    row = lambda w: pl.BlockSpec((tq, w), lambda b, i: (b * nt + i, 0))
    full = lambda a: pl.BlockSpec(a.shape, lambda b, i: (0,) * a.ndim)
    prev = pl.BlockSpec((CHUNK, 2 * ATT_KV_W),
                        lambda b, i: (jnp.maximum((b * nt + i) * sub - 1, 0), 0))
    return pl.pallas_call(
        _prompt_mixer_kernel,
        grid=(batch, nt),
        in_specs=[pl.BlockSpec(memory_space=pltpu.SMEM),
                  row(ATT_Q_W), row(2 * ATT_KV_W), prev, row(ML_QK_W), row(ML_QK_W),
                  row(ML_V_W), row(ML_V_W),
                  pl.BlockSpec((N_GATES, tq), lambda b, i: (0, b * nt + i)),
                  row(D_MODEL), full(wout), full(mlnw)],
        out_specs=[row(D_MODEL),
                   pl.BlockSpec((1, ML_HEADS, ML_V_DIM, ML_QK_DIM), lambda b, i: (b, 0, 0, 0)),
                   pl.BlockSpec((1, ML_HEADS, ML_QK_DIM), lambda b, i: (b, 0, 0)),
                   pl.BlockSpec((1, 1, ML_HEADS), lambda b, i: (b, 0, 0))],
        out_shape=[jax.ShapeDtypeStruct((batch * seq, D_MODEL), F32),
                   jax.ShapeDtypeStruct((batch, ML_HEADS, ML_V_DIM, ML_QK_DIM), F32),
                   jax.ShapeDtypeStruct((batch, ML_HEADS, ML_QK_DIM), F32),
                   jax.ShapeDtypeStruct((batch, 1, ML_HEADS), F32)],
        scratch_shapes=[pltpu.VMEM((tq, D_MODEL), BF16),
                        pltpu.VMEM((ML_HEADS // 2, 2 * LANES, LANES), F32),
                        pltpu.VMEM((SUBLANES, LANES), F32)],
        compiler_params=pltpu.CompilerParams(dimension_semantics=("arbitrary", "arbitrary"),
                                             vmem_limit_bytes=VMEM_LIMIT),
        name="prompt_mixer",
    )(sinks, qa, kv, kv, qm, km, vm, om, gt, x2d, wout, mlnw)


def _sample_mixer_kernel(t_len, sinks_ref, qa_ref, kv_ref, ck_ref, cv_ref, qm_ref, km_ref,
                         vm_ref, om_ref, gt_ref, c0_ref, n0_ref, m0_ref, x_ref, wout_ref,
                         mlnw_ref, x1_ref, nk_ref, nv_ref, c_ref, n_ref, m_ref, mix_scr):
    bt = SAMPLE_BT
    T = t_len
    L = bt * T
    n_stack = 2 * ATT_GROUP
    lane3 = lax.broadcasted_iota(jnp.int32, (bt, T, LANES), 2)
    low3 = lane3 < ATT_HEAD_DIM
    lane = lax.broadcasted_iota(jnp.int32, (L, LANES), 1)
    low = lane < ATT_HEAD_DIM

    qa3 = qa_ref[...].astype(F32).reshape(bt, T, ATT_Q_W)
    pieces = []
    for col in range(ATT_GROUP):
        qc = qa3[:, :, col * LANES:(col + 1) * LANES]
        pieces += [jnp.where(low3, qc, 0.0), jnp.where(low3, 0.0, qc)]
    q3 = jnp.concatenate(pieces, axis=1).astype(BF16)
    R = bt * n_stack * T
    q2 = q3.reshape(R, LANES)
    kv_new = kv_ref[...]
    k_new = kv_new[:, :ATT_KV_W]
    v_new = kv_new[:, ATT_KV_W:]
    ck = ck_ref[...]
    cv = cv_ref[...]
    s_c = jnp.einsum('bqd,bkd->bqk', q3, ck.astype(BF16),
                     preferred_element_type=F32).reshape(R, WINDOW)
    s_n = _dot_nt(q2, k_new.astype(BF16))
    row_c = lax.broadcasted_iota(jnp.int32, (R, WINDOW), 0)
    col_c = lax.broadcasted_iota(jnp.int32, (R, WINDOW), 1)
    s_c = jnp.where(col_c > row_c % T, s_c, -jnp.inf)
    row_n = lax.broadcasted_iota(jnp.int32, (R, L), 0)
    col_n = lax.broadcasted_iota(jnp.int32, (R, L), 1)
    valid_n = (row_n // (n_stack * T) == col_n // T) & (col_n % T <= row_n % T)
    s_n = jnp.where(valid_n, s_n, -jnp.inf)
    stack_id = (lax.broadcasted_iota(jnp.int32, (R, 1), 0) // T) % n_stack
    sink = jnp.zeros((R, 1), F32)
    for k_id in range(n_stack):
        sink = jnp.where(stack_id == k_id, sinks_ref[ATT_HEAD_ORDER[k_id]], sink)
    m = jnp.maximum(jnp.maximum(jnp.max(s_c, axis=-1, keepdims=True),
                                jnp.max(s_n, axis=-1, keepdims=True)), sink)
    e_c = jnp.exp(s_c - m)
    e_n = jnp.exp(s_n - m)
    denom = (jnp.sum(e_c, axis=-1, keepdims=True) + jnp.sum(e_n, axis=-1, keepdims=True)
             + jnp.exp(sink - m))
    o = jnp.einsum('bqk,bkd->bqd', e_c.astype(BF16).reshape(bt, n_stack * T, WINDOW),
                   cv.astype(BF16), preferred_element_type=F32).reshape(R, LANES)
    o = (o + _dot(e_n.astype(BF16), v_new.astype(BF16))) / denom
    o3 = o.reshape(bt, n_stack * T, LANES)
    for col in range(ATT_GROUP):
        lo_h = o3[:, (2 * col) * T:(2 * col + 1) * T, :]
        hi_h = o3[:, (2 * col + 1) * T:(2 * col + 2) * T, :]
        mix_scr[:, col * LANES:(col + 1) * LANES] = jnp.where(
            low3, lo_h, hi_h).reshape(L, LANES).astype(BF16)

    nk_ref[:, :WINDOW - T, :] = ck[:, T:, :]
    nk_ref[:, WINDOW - T:, :] = k_new.reshape(bt, T, ATT_KV_W)
    nv_ref[:, :WINDOW - T, :] = cv[:, T:, :]
    nv_ref[:, WINDOW - T:, :] = v_new.reshape(bt, T, ATT_KV_W)

    r = lax.broadcasted_iota(jnp.int32, (L, L), 0)
    c = lax.broadcasted_iota(jnp.int32, (L, L), 1)
    seg = (r // T == c // T) & (c <= r)
    gates = gt_ref[...]
    cum_row, cum_col, raw_col = _gate_forms(gates, seg)
    ones_col = _ones_col(L)
    qm = qm_ref[...]
    km = km_ref[...]
    qm_f = qm.astype(F32)
    km_f = km.astype(F32)
    for h in range(ML_HEADS):
        p, e = divmod(h, 2)
        qc = qm[:, p * LANES:(p + 1) * LANES]
        k_pair = km[:, p * LANES:(p + 1) * LANES]
        zero = jnp.zeros_like(qc)
        q_pad = jnp.where(low, qc, zero) if e == 0 else jnp.where(low, zero, qc)
        v_h = vm_ref[:, h * ML_V_DIM:(h + 1) * ML_V_DIM]
        v_ext = jnp.concatenate([v_h, ones_col], axis=1)
        b_c = cum_col[:, ML_HEADS + h:ML_HEADS + h + 1]
        b_r = cum_row[ML_HEADS + h:ML_HEADS + h + 1, :]
        ig_r = gates[h:h + 1, :]
        ig_c = raw_col[:, h:h + 1]
        m0 = m0_ref[:, :, h:h + 1]
        m_prev_c = jnp.broadcast_to(m0, (bt, T, 1)).reshape(L, 1)
        pv, m_row, w_inter = _mlstm_intra(q_pad, k_pair, v_ext, seg, b_c, b_r, ig_r, m_prev_c)
        q_h3 = qm_f[:, h * ML_QK_DIM:(h + 1) * ML_QK_DIM].reshape(bt, T, ML_QK_DIM)
        k_h3 = km_f[:, h * ML_QK_DIM:(h + 1) * ML_QK_DIM].reshape(bt, T, ML_QK_DIM)
        c0 = c0_ref[:, h]
        n0 = n0_ref[:, h:h + 1, :]
        q_c = jnp.einsum('btd,bvd->btv', q_h3.astype(BF16), c0.astype(BF16),
                         preferred_element_type=F32).reshape(L, ML_V_DIM)
        q_n = jnp.sum(q_h3 * n0, axis=-1, keepdims=True).reshape(L, 1)
        mix_scr[:, ATT_Q_W + h * ML_V_DIM:ATT_Q_W + (h + 1) * ML_V_DIM] = _mlstm_out(
            pv, m_row, w_inter, q_c, q_n, mlnw_ref[:, h * ML_V_DIM:(h + 1) * ML_V_DIM],
            om_ref[:, h * ML_V_DIM:(h + 1) * ML_V_DIM])
        b3 = b_c.reshape(bt, T, 1)
        b_last = b3[:, T - 1:T, :]
        a3 = b_last - b3 + ig_c.reshape(bt, T, 1)
        m_new = jnp.maximum(b_last + m0, jnp.max(a3, axis=1, keepdims=True))
        sc = jnp.exp(b_last + m0 - m_new)
        ws = jnp.exp(a3 - m_new)
        wsv = (ws * v_h.astype(F32).reshape(bt, T, ML_V_DIM)).astype(BF16)
        d_c = jnp.einsum('bsv,bsd->bvd', wsv, k_h3.astype(BF16), preferred_element_type=F32)
        c_ref[:, h] = sc * c0 + d_c
        n_ref[:, h:h + 1, :] = sc * n0 + jnp.sum(ws * k_h3, axis=1, keepdims=True)
        m_ref[:, :, h:h + 1] = m_new

    x1_ref[...] = x_ref[...] + _dot(mix_scr[...], wout_ref[...])


def _sample_mixer(nb, t_len, sinks, qa, kv, ck, cv, qm, km, vm, om, gt, c0, n0, m0, x2d, wout, mlnw):
    bt = SAMPLE_BT
    tl = bt * t_len
    row = lambda w: pl.BlockSpec((tl, w), lambda i: (i, 0))
    full = lambda a: pl.BlockSpec(a.shape, lambda i: (0,) * a.ndim)
    cache = pl.BlockSpec((bt, WINDOW, ATT_KV_W), lambda i: (i, 0, 0))
    c_spec = pl.BlockSpec((bt, ML_HEADS, ML_V_DIM, ML_QK_DIM), lambda i: (i, 0, 0, 0))
    n_spec = pl.BlockSpec((bt, ML_HEADS, ML_QK_DIM), lambda i: (i, 0, 0))
    m_spec = pl.BlockSpec((bt, 1, ML_HEADS), lambda i: (i, 0, 0))
    return pl.pallas_call(
        functools.partial(_sample_mixer_kernel, t_len),
        grid=(nb // bt,),
        in_specs=[pl.BlockSpec(memory_space=pltpu.SMEM),
                  row(ATT_Q_W), row(2 * ATT_KV_W), cache, cache, row(ML_QK_W), row(ML_QK_W),
                  row(ML_V_W), row(ML_V_W), pl.BlockSpec((N_GATES, tl), lambda i: (0, i)),
                  c_spec, n_spec, m_spec, row(D_MODEL), full(wout), full(mlnw)],
        out_specs=[row(D_MODEL), cache, cache, c_spec, n_spec, m_spec],
        out_shape=[jax.ShapeDtypeStruct((nb * t_len, D_MODEL), F32),
                   jax.ShapeDtypeStruct((nb, WINDOW, ATT_KV_W), F32),
                   jax.ShapeDtypeStruct((nb, WINDOW, ATT_KV_W), F32),
                   jax.ShapeDtypeStruct((nb, ML_HEADS, ML_V_DIM, ML_QK_DIM), F32),
                   jax.ShapeDtypeStruct((nb, ML_HEADS, ML_QK_DIM), F32),
                   jax.ShapeDtypeStruct((nb, 1, ML_HEADS), F32)],
        scratch_shapes=[pltpu.VMEM((tl, D_MODEL), BF16)],
        compiler_params=pltpu.CompilerParams(dimension_semantics=("arbitrary",),
                                             vmem_limit_bytes=VMEM_LIMIT),
        name="sample_mixer",
    )(sinks, qa, kv, ck, cv, qm, km, vm, om, gt, c0, n0, m0, x2d, wout, mlnw)


def _ffn_kernel(seq_rows, *refs):
    if seq_rows is None:
        (x_ref, nw_ref, w_ref, cw_ref, cb_ref, wd_ref, y_ref, conv_ref,
         gbuf, act_scr, carry) = refs
        hist_ref = None
    else:
        (x_ref, hist_ref, nw_ref, w_ref, cw_ref, cb_ref, wd_ref, y_ref, conv_ref,
         gbuf, act_scr) = refs
        carry = None
    tm = x_ref.shape[0]
    tf = FF_CHUNK
    n_hist = CONV_W - 1
    rows = tm if seq_rows is None else seq_rows
    nseq = tm // rows
    base = SUBLANES
    n_chunks = D_FF // tf

    if carry is not None:
        @pl.when(pl.program_id(1) == 0)
        def _():
            carry[...] = jnp.zeros(carry.shape, F32)

    x = x_ref[...]
    h2 = _rms(x, nw_ref[...]).astype(BF16)

    def proj(f):
        return (_dot(h2, w_ref[:, f * tf:(f + 1) * tf]),
                _dot(h2, w_ref[:, D_FF + f * tf:D_FF + (f + 1) * tf]))

    nxt = proj(0)
    for f in range(n_chunks):
        g, u = nxt
        if f + 1 < n_chunks:
            nxt = proj(f + 1)
        cols = slice(f * tf, (f + 1) * tf)
        s = f % 2
        g3 = g.reshape(nseq, rows, tf)
        if seq_rows is None:
            gbuf[s, :, base - n_hist:base, :] = carry[:, SUBLANES - n_hist:, cols]
            carry[:, SUBLANES - n_hist:, cols] = g3[:, rows - n_hist:, :]
        else:
            gbuf[s, :, base - n_hist:base, :] = hist_ref[:, :, cols]
            conv_ref[:, :, cols] = g3[:, rows - n_hist:, :]
        gbuf[s, :, base:base + rows, :] = g3
        gc = cb_ref[:, cols] + g * cw_ref[CONV_W - 1:CONV_W, cols]
        for d in range(1, CONV_W):
            gm = gbuf[s, :, base - d:base - d + rows, :].reshape(tm, tf)
            gc = gc + gm * cw_ref[CONV_W - 1 - d:CONV_W - d, cols]
        act_scr[:, cols] = (gc * _sigmoid(gc) * u).astype(BF16)
    y_ref[...] = x + _dot(act_scr[...], wd_ref[...])

    if carry is not None:
        @pl.when(pl.program_id(1) == pl.num_programs(1) - 1)
        def _():
            conv_ref[...] = carry[:, SUBLANES - n_hist:, :]


def _ffn_scratch(tm, rows):
    return [pltpu.VMEM((2, tm // rows, SUBLANES + rows, FF_CHUNK), F32),
            pltpu.VMEM((tm, D_FF), BF16)]


def _ffn_prompt(batch, seq, x2d, nw, w, cw, cb, wd):
    tm = ROW_TILE
    nt = seq // tm
    full = lambda a: pl.BlockSpec(a.shape, lambda b, i: (0,) * a.ndim)
    once = lambda a: pl.BlockSpec(a.shape, lambda b, i: (0,) * a.ndim,
                                  pipeline_mode=pl.Buffered(1))
    row = pl.BlockSpec((tm, D_MODEL), lambda b, i: (b * nt + i, 0))
    return pl.pallas_call(
        functools.partial(_ffn_kernel, None),
        grid=(batch, nt),
        in_specs=[row, full(nw), once(w), full(cw), full(cb), once(wd)],
        out_specs=[row, pl.BlockSpec((1, CONV_W - 1, D_FF), lambda b, i: (b, 0, 0))],
        out_shape=[jax.ShapeDtypeStruct((batch * seq, D_MODEL), F32),
                   jax.ShapeDtypeStruct((batch, CONV_W - 1, D_FF), F32)],
        scratch_shapes=_ffn_scratch(tm, tm) + [pltpu.VMEM((1, SUBLANES, D_FF), F32)],
        compiler_params=pltpu.CompilerParams(dimension_semantics=("arbitrary", "arbitrary"),
                                             vmem_limit_bytes=VMEM_LIMIT),
        name="ffn_prompt",
    )(x2d, nw, w, cw, cb, wd)


def _ffn_sample(nb, t_len, x2d, hist, nw, w, cw, cb, wd):
    tm = ROW_TILE
    bt = tm // t_len
    full = lambda a: pl.BlockSpec(a.shape, lambda i: (0,) * a.ndim)
    once = lambda a: pl.BlockSpec(a.shape, lambda i: (0,) * a.ndim, pipeline_mode=pl.Buffered(1))
    row = pl.BlockSpec((tm, D_MODEL), lambda i: (i, 0))
    hist_spec = pl.BlockSpec((bt, CONV_W - 1, D_FF), lambda i: (i, 0, 0))
    return pl.pallas_call(
        functools.partial(_ffn_kernel, t_len),
        grid=(nb // bt,),
        in_specs=[row, hist_spec, full(nw), once(w), full(cw), full(cb), once(wd)],
        out_specs=[row, hist_spec],
        out_shape=[jax.ShapeDtypeStruct((nb * t_len, D_MODEL), F32),
                   jax.ShapeDtypeStruct((nb, CONV_W - 1, D_FF), F32)],
        scratch_shapes=_ffn_scratch(tm, t_len),
        compiler_params=pltpu.CompilerParams(dimension_semantics=("arbitrary",),
                                             vmem_limit_bytes=VMEM_LIMIT),
        name="ffn_sample",
    )(x2d, hist, nw, w, cw, cb, wd)


def _head_mean_matrix(width, head_dim):
    idx = np.arange(width) // head_dim
    return jnp.asarray((idx[:, None] == idx[None, :]).astype(np.float32) / head_dim, dtype=BF16)


def _layer_weights(norm_mix_w, w_in, b_gates, q_norm_w, k_norm_w, sinks, ml_norm_w, w_out,
                   norm_ffn_w, w_ffn_in, conv_w, conv_b, w_down):
    head_cols = np.concatenate([np.arange(h * ATT_HEAD_DIM, (h + 1) * ATT_HEAD_DIM)
                                for h in ATT_HEAD_ORDER])
    col_perm = np.concatenate([head_cols, np.arange(ATT_Q_W, w_in.shape[1])])
    w_in_p = jnp.pad(w_in[:, col_perm], ((0, 0), (0, IN_WIDTH_PAD - w_in.shape[1]))).astype(BF16)
    row_perm = np.concatenate([head_cols, np.arange(ATT_Q_W, w_out.shape[0])])
    return dict(
        nw=norm_mix_w.reshape(1, D_MODEL),
        w_in=w_in_p,
        bg=jnp.pad(b_gates, (0, LANES - N_GATES)).reshape(1, LANES),
        qnw=(jnp.tile(q_norm_w, ATT_HEADS) * ATT_SCALE).reshape(1, ATT_Q_W),
        knw=jnp.tile(k_norm_w, ATT_KV_HEADS).reshape(1, ATT_KV_W),
        gq=_head_mean_matrix(ATT_Q_W, ATT_HEAD_DIM),
        gk=_head_mean_matrix(ATT_KV_W, ATT_HEAD_DIM),
        sinks=sinks,
        mlnw=ml_norm_w.reshape(1, ML_V_W),
        wout=w_out[row_perm].astype(BF16),
        nfw=norm_ffn_w.reshape(1, D_MODEL),
        wff=w_ffn_in.astype(BF16),
        cw=conv_w,
        cb=conv_b.reshape(1, D_FF),
        wd=w_down.astype(BF16),
    )


def _prompt_layer(x, w):
    batch, seq, _ = x.shape
    x2d = x.reshape(batch * seq, D_MODEL)
    qa, kv, qm, km, vm, om, gt = _inproj(x2d, w["nw"], w["w_in"], w["bg"], w["qnw"], w["knw"],
                                         w["gq"], w["gk"])
    x1, c, n, m = _prompt_mixer(batch, seq, w["sinks"], qa, kv, qm, km, vm, om, gt, x2d,
                                w["wout"], w["mlnw"])
    y, conv = _ffn_prompt(batch, seq, x1, w["nfw"], w["wff"], w["cw"], w["cb"], w["wd"])
    kv_tail = kv.reshape(batch, seq, 2, ATT_KV_HEADS, ATT_HEAD_DIM)[:, seq - WINDOW:]
    return (y.reshape(batch, seq, D_MODEL), kv_tail[:, :, 0], kv_tail[:, :, 1], c, n,
            m.reshape(batch, ML_HEADS), conv)


def _sample_layer(x, ck, cv, c0, n0, m0, conv_buf, w):
    nb, t_len, _ = x.shape
    x2d = x.reshape(nb * t_len, D_MODEL)
    qa, kv, qm, km, vm, om, gt = _inproj(x2d, w["nw"], w["w_in"], w["bg"], w["qnw"], w["knw"],
                                         w["gq"], w["gk"])
    x1, nk, nv, c, n, m = _sample_mixer(
        nb, t_len, w["sinks"], qa, kv, ck.reshape(nb, WINDOW, ATT_KV_W),
        cv.reshape(nb, WINDOW, ATT_KV_W), qm, km, vm, om, gt, c0, n0,
        m0.reshape(nb, 1, ML_HEADS), x2d, w["wout"], w["mlnw"])
    y, conv = _ffn_sample(nb, t_len, x1, conv_buf, w["nfw"], w["wff"], w["cw"], w["cb"],
                          w["wd"])
    shape5 = (nb, WINDOW, ATT_KV_HEADS, ATT_HEAD_DIM)
    return (y.reshape(nb, t_len, D_MODEL), nk.reshape(shape5), nv.reshape(shape5), c, n,
            m.reshape(nb, ML_HEADS), conv)


def kernel(x_prompt, x_sample, cache_attn_k, cache_attn_v, state_mlstm_C, state_mlstm_n,
           state_mlstm_m, cache_ffn_conv, norm_mix_w, w_in, b_gates, q_norm_w, k_norm_w,
           sinks, ml_norm_w, w_out, norm_ffn_w, w_ffn_in, conv_w, conv_b, w_down):
    depth = w_in.shape[0]
    yp, ys = x_prompt, x_sample
    sp = [[] for _ in range(6)]
    ss = [[] for _ in range(6)]
    for l in range(depth):
        w = _layer_weights(norm_mix_w[l], w_in[l], b_gates[l], q_norm_w[l], k_norm_w[l], sinks[l],
                           ml_norm_w[l], w_out[l], norm_ffn_w[l], w_ffn_in[l], conv_w[l],
                           conv_b[l], w_down[l])
        yp, *st_p = _prompt_layer(yp, w)
        ys, *st_s = _sample_layer(ys, cache_attn_k[l], cache_attn_v[l], state_mlstm_C[l],
                                  state_mlstm_n[l], state_mlstm_m[l], cache_ffn_conv[l], w)
        for i in range(6):
            sp[i].append(st_p[i])
            ss[i].append(st_s[i])
    k_p, v_p, c_p, n_p, m_p, conv_p = [jnp.stack(a) for a in sp]
    k_s, v_s, c_s, n_s, m_s, conv_s = [jnp.stack(a) for a in ss]
    return (yp, ys, k_p, v_p, c_p, n_p, m_p, conv_p, k_s, v_s, c_s, n_s, m_s, conv_s)
```

```python
import functools

import numpy as np
import jax
import jax.numpy as jnp
from jax import lax
from jax.experimental import pallas as pl
from jax.experimental.pallas import tpu as pltpu

F32 = jnp.float32
BF16 = jnp.bfloat16

D_MODEL = 1024
ATT_HEADS = 8
ATT_KV_HEADS = 2
ATT_HEAD_DIM = 64
ATT_GROUP = ATT_HEADS // ATT_KV_HEADS
WINDOW = 128
ML_HEADS = 4
ML_V_DIM = 128
ML_QK_DIM = 64
D_FF = 2816
CONV_W = 3
EPS = 1e-6
ATT_SCALE = ATT_HEAD_DIM ** -0.5
ML_SCALE = ML_QK_DIM ** -0.5

ATT_Q_W = ATT_HEADS * ATT_HEAD_DIM
ATT_KV_W = ATT_KV_HEADS * ATT_HEAD_DIM
ML_QK_W = ML_HEADS * ML_QK_DIM
ML_V_W = ML_HEADS * ML_V_DIM
N_GATES = 2 * ML_HEADS
N_STACK = 2 * ATT_GROUP

LANES = 128
SUBLANES = 8

OFF_QA = 0
OFF_KV = OFF_QA + ATT_Q_W
OFF_QM = OFF_KV + 2 * ATT_KV_W
OFF_KM = OFF_QM + ML_QK_W
OFF_VM = OFF_KM + ML_QK_W
OFF_OM = OFF_VM + ML_V_W
OFF_GL = OFF_OM + ML_V_W
IN_WIDTH_PAD = OFF_GL + LANES

ATT_HEAD_ORDER = tuple(h for c in range(ATT_GROUP) for h in (c, c + ATT_GROUP))

ROW_TILE = 512
MIX_TILE = 256
FF_CHUNK = 256
SAMPLE_BT = 16
VMEM_LIMIT = 56 * 1024 * 1024


def _dot(a, b):
    return jnp.dot(a, b, preferred_element_type=F32)


def _dot_nt(a, b):
    return lax.dot_general(a, b, (((1,), (1,)), ((), ())), preferred_element_type=F32)


def _split3(x):
    hi = x.astype(BF16)
    r1 = x - hi.astype(F32)
    mid = r1.astype(BF16)
    lo = (r1 - mid.astype(F32)).astype(BF16)
    return hi, mid, lo


def _rms(x, w):
    ms = jnp.mean(x * x, axis=-1, keepdims=True)
    return x * lax.rsqrt(ms + EPS) * w


def _log_sigmoid(x):
    return jnp.minimum(x, 0.0) - jnp.log1p(jnp.exp(-jnp.abs(x)))


def _sigmoid(x):
    return 1.0 / (1.0 + jnp.exp(-x))


def _permute_head_rows(dst_ref, src_ref):
    for k, h in enumerate(ATT_HEAD_ORDER):
        dst_ref[k * ATT_HEAD_DIM:(k + 1) * ATT_HEAD_DIM, :] = (
            src_ref[h * ATT_HEAD_DIM:(h + 1) * ATT_HEAD_DIM, :])


def _inproj_kernel(x_ref, nw_ref, w_ref, bg_ref, qnw_ref, knw_ref, gq_ref, gk_ref,
                   qa_ref, kv_ref, qm_ref, km_ref, vm_ref, om_ref, gt_ref, wq_scr):
    @pl.when(pl.program_id(0) == 0)
    def _():
        _permute_head_rows(wq_scr, w_ref)

    h = _rms(x_ref[...], nw_ref[...]).astype(BF16)

    def proj(lo, width):
        return _dot_nt(h, w_ref[lo:lo + width, :])

    q = _dot_nt(h, wq_scr[...])
    q_ms = _dot((q * q).astype(BF16), gq_ref[...])
    qa_ref[...] = (q * lax.rsqrt(q_ms + EPS) * qnw_ref[...]).astype(BF16)

    kv = proj(OFF_KV, 2 * ATT_KV_W)
    k = kv[:, :ATT_KV_W]
    k_ms = _dot((k * k).astype(BF16), gk_ref[...])
    kv_ref[:, :ATT_KV_W] = k * lax.rsqrt(k_ms + EPS) * knw_ref[...]
    kv_ref[:, ATT_KV_W:] = kv[:, ATT_KV_W:]

    qm_ref[...] = (proj(OFF_QM, ML_QK_W) * ML_SCALE).astype(BF16)
    km_ref[...] = proj(OFF_KM, ML_QK_W).astype(BF16)
    vm_ref[...] = proj(OFF_VM, ML_V_W).astype(BF16)
    om_ref[...] = proj(OFF_OM, ML_V_W).astype(BF16)

    gl = proj(OFF_GL, LANES) + bg_ref[...]
    lane = lax.broadcasted_iota(jnp.int32, gl.shape, 1)
    g = jnp.where(lane < ML_HEADS, gl, _log_sigmoid(gl))
    gt_ref[...] = g.T[:N_GATES, :]


def _inproj(x2d, nw, w_in_t, bg, qnw, knw, gq, gk):
    n = x2d.shape[0]
    tm = ROW_TILE
    row = lambda w: pl.BlockSpec((tm, w), lambda i: (i, 0))
    full = lambda a: pl.BlockSpec(a.shape, lambda i: (0,) * a.ndim)
    once = lambda a: pl.BlockSpec(a.shape, lambda i: (0,) * a.ndim, pipeline_mode=pl.Buffered(1))
    return pl.pallas_call(
        _inproj_kernel,
        grid=(n // tm,),
        in_specs=[row(D_MODEL), full(nw), once(w_in_t), full(bg), full(qnw), full(knw),
                  full(gq), full(gk)],
        out_specs=[row(ATT_Q_W), row(2 * ATT_KV_W), row(ML_QK_W), row(ML_QK_W),
                   row(ML_V_W), row(ML_V_W), pl.BlockSpec((N_GATES, tm), lambda i: (0, i))],
        out_shape=[jax.ShapeDtypeStruct((n, ATT_Q_W), BF16),
                   jax.ShapeDtypeStruct((n, 2 * ATT_KV_W), F32),
                   jax.ShapeDtypeStruct((n, ML_QK_W), BF16),
                   jax.ShapeDtypeStruct((n, ML_QK_W), BF16),
                   jax.ShapeDtypeStruct((n, ML_V_W), BF16),
                   jax.ShapeDtypeStruct((n, ML_V_W), BF16),
                   jax.ShapeDtypeStruct((N_GATES, n), F32)],
        scratch_shapes=[pltpu.VMEM((ATT_Q_W, D_MODEL), BF16)],
        compiler_params=pltpu.CompilerParams(dimension_semantics=("arbitrary",),
                                             vmem_limit_bytes=VMEM_LIMIT),
        name="inproj",
    )(x2d, nw, w_in_t, bg, qnw, knw, gq, gk)


def _gate_forms(gates, seg_mask, want_raw_col):
    L = gates.shape[1]
    m_bf = seg_mask.astype(F32).astype(BF16)
    cum_row = jnp.zeros(gates.shape, F32)
    cum_col = jnp.zeros((L, gates.shape[0]), F32)
    raw_col = None
    if want_raw_col:
        r = lax.broadcasted_iota(jnp.int32, (L, L), 0)
        c = lax.broadcasted_iota(jnp.int32, (L, L), 1)
        eye = (r == c).astype(F32).astype(BF16)
        raw_col = jnp.zeros((L, gates.shape[0]), F32)
    for part in _split3(gates):
        cum_row = cum_row + _dot_nt(part, m_bf)
        cum_col = cum_col + _dot_nt(m_bf, part)
        if want_raw_col:
            raw_col = raw_col + _dot_nt(eye, part)
    return cum_row, cum_col, raw_col


def _mlstm_intra(q_pad, k_pair, v_ext, seg_mask, b_c, b_r, ig_r, m_prev_c):
    dm = jnp.where(seg_mask, b_c + (ig_r - b_r), -jnp.inf)
    inter = b_c + m_prev_c
    m_row = jnp.maximum(inter, jnp.max(dm, axis=-1, keepdims=True))
    w_inter = jnp.exp(inter - m_row)
    p = _dot_nt(q_pad, k_pair) * jnp.exp(dm - m_row)
    return _dot(p.astype(BF16), v_ext), m_row, w_inter


def _mlstm_out(pv, m_row, w_inter, q_c, q_n, mlnw_h, om_h):
    num = pv[:, :ML_V_DIM] + w_inter * q_c
    den = pv[:, ML_V_DIM:ML_V_DIM + 1] + w_inter * q_n
    hh = num / jnp.maximum(jnp.abs(den), jnp.exp(-m_row))
    return (_rms(hh, mlnw_h) * _sigmoid(om_h.astype(F32))).astype(BF16)


def _ones_col(rows):
    lane = lax.broadcasted_iota(jnp.int32, (rows, LANES), 1)
    return (lane == 0).astype(F32).astype(BF16)


def _stack_sinks(sinks_ref):
    idx = lax.broadcasted_iota(jnp.int32, (N_STACK, 1, 1), 0)
    sink = jnp.zeros((N_STACK, 1, 1), F32)
    for k, h in enumerate(ATT_HEAD_ORDER):
        sink = jnp.where(idx == k, sinks_ref[h], sink)
    return sink


def _prompt_mixer_kernel(sinks_ref, qa_ref, kvc_ref, kvp_ref, qm_ref, km_ref, vm_ref,
                         om_ref, gt_ref, x_ref, wout_ref, mlnw_ref,
                         x1_ref, ct_ref, nrow_ref, m_ref, kt_ref, vt_ref,
                         mix_scr, wperm_scr, state_scr, m_scr):
    b = pl.program_id(0)
    i = pl.program_id(1)
    A = WINDOW
    L = MIX_TILE
    n_pairs = ML_HEADS // 2

    @pl.when((b == 0) & (i == 0))
    def _():
        _permute_head_rows(wperm_scr, wout_ref)
        wperm_scr[ATT_Q_W:, :] = wout_ref[ATT_Q_W:, :]

    @pl.when(i == 0)
    def _():
        state_scr[...] = jnp.zeros(state_scr.shape, F32)
        m_scr[...] = jnp.zeros(m_scr.shape, F32)

    lane = lax.broadcasted_iota(jnp.int32, (A, LANES), 1)
    low = lane < ATT_HEAD_DIM
    qi = lax.broadcasted_iota(jnp.int32, (A, 2 * A), 0)
    kj = lax.broadcasted_iota(jnp.int32, (A, 2 * A), 1)
    band = (kj > qi) & (kj <= qi + WINDOW)
    sink = _stack_sinks(sinks_ref)
    kv_all = jnp.concatenate([kvp_ref[...], kvc_ref[...]], axis=0)
    ones2 = _ones_col(2 * A)
    for j in range(L // A):
        rows = slice(j * A, (j + 1) * A)
        kk = kv_all[j * A:(j + 2) * A, :ATT_KV_W].astype(BF16)
        vvx = jnp.concatenate([kv_all[j * A:(j + 2) * A, ATT_KV_W:].astype(BF16), ones2], axis=1)
        pieces = []
        for col in range(ATT_GROUP):
            qc = qa_ref[rows, col * LANES:(col + 1) * LANES]
            zero = jnp.zeros_like(qc)
            pieces += [jnp.where(low, qc, zero), jnp.where(low, zero, qc)]
        q8 = jnp.concatenate(pieces, axis=0)
        band_j = band & jnp.logical_or(i > 0, kj >= A) if j == 0 else band
        s = jnp.where(band_j[None], _dot_nt(q8, kk).reshape(N_STACK, A, 2 * A), -jnp.inf)
        m = jnp.maximum(jnp.max(s, axis=-1, keepdims=True), sink)
        e = jnp.exp(s - m).reshape(N_STACK * A, 2 * A)
        ox = _dot(e.astype(BF16), vvx).reshape(N_STACK, A, 2 * LANES)
        o = ox[:, :, :LANES] / (ox[:, :, LANES:LANES + 1] + jnp.exp(sink - m))
        for col in range(ATT_GROUP):
            mix_scr[rows, col * LANES:(col + 1) * LANES] = jnp.where(
                low, o[2 * col], o[2 * col + 1]).astype(BF16)

    r = lax.broadcasted_iota(jnp.int32, (L, L), 0)
    c = lax.broadcasted_iota(jnp.int32, (L, L), 1)
    causal = c <= r
    lane_l = lax.broadcasted_iota(jnp.int32, (L, LANES), 1)
    low_l = lane_l < ML_QK_DIM
    ones_l = _ones_col(L)
    gates = gt_ref[...]
    cum_row, cum_col, _ = _gate_forms(gates, causal, False)
    for p in range(n_pairs):
        qc = qm_ref[:, p * LANES:(p + 1) * LANES]
        k_pair = km_ref[:, p * LANES:(p + 1) * LANES]
        k_t = k_pair.astype(F32).T
        zero = jnp.zeros_like(qc)
        state = state_scr[p]
        state_bf = state.astype(BF16)
        new_state = []
        for e in range(2):
            h = 2 * p + e
            q_pad = jnp.where(low_l, qc, zero) if e == 0 else jnp.where(low_l, zero, qc)
            v_ext = jnp.concatenate([vm_ref[:, h * ML_V_DIM:(h + 1) * ML_V_DIM], ones_l], axis=1)
            b_c = cum_col[:, ML_HEADS + h:ML_HEADS + h + 1]
            b_r = cum_row[ML_HEADS + h:ML_HEADS + h + 1, :]
            ig_r = gates[h:h + 1, :]
            m_prev = m_scr[h:h + 1, 0:1]
            pv, m_row, w_inter = _mlstm_intra(q_pad, k_pair, v_ext, causal, b_c, b_r, ig_r, m_prev)
            qs = _dot(q_pad, state_bf)
            mix_scr[:, ATT_Q_W + h * ML_V_DIM:ATT_Q_W + (h + 1) * ML_V_DIM] = _mlstm_out(
                pv, m_row, w_inter, qs[:, :ML_V_DIM], qs[:, ML_V_DIM:ML_V_DIM + 1],
                mlnw_ref[:, h * ML_V_DIM:(h + 1) * ML_V_DIM],
                om_ref[:, h * ML_V_DIM:(h + 1) * ML_V_DIM])
            b_last = b_r[:, L - 1:L]
            a_r = b_last - b_r + ig_r
            m_new = jnp.maximum(b_last + m_prev, jnp.max(a_r, axis=-1, keepdims=True))
            sc = jnp.exp(b_last + m_prev - m_new)
            kw = (k_t * jnp.exp(a_r - m_new)).astype(BF16)
            new_state.append(sc * state + _dot(kw, v_ext))
            m_scr[h:h + 1, :] = jnp.broadcast_to(m_new, (1, LANES))
        first = lax.broadcasted_iota(jnp.int32, state.shape, 0) < ML_QK_DIM
        state_scr[p] = jnp.where(first, new_state[0], new_state[1])

    x1_ref[...] = x_ref[...] + _dot(mix_scr[...], wperm_scr[...])

    @pl.when(i == pl.num_programs(1) - 1)
    def _():
        for h in range(ML_HEADS):
            p, e = divmod(h, 2)
            ct_ref[0, h] = state_scr[p, e * ML_QK_DIM:(e + 1) * ML_QK_DIM, :ML_V_DIM]
            m_ref[0, :, h:h + 1] = m_scr[h:h + 1, 0:1]
        for p in range(n_pairs):
            nrow_ref[0, p:p + 1, :] = state_scr[p, :, ML_V_DIM:].T[0:1, :]
        kt_ref[0] = kvc_ref[L - WINDOW:, :ATT_KV_W].T
        vt_ref[0] = kvc_ref[L - WINDOW:, ATT_KV_W:].T


def _prompt_mixer(batch, seq, sinks, qa, kv, qm, km, vm, om, gt, x2d, wout, mlnw):
    tq = MIX_TILE
    nt = seq // tq
    sub = tq // WINDOW
    row = lambda w: pl.BlockSpec((tq, w), lambda b, i: (b * nt + i, 0))
    full = lambda a: pl.BlockSpec(a.shape, lambda b, i: (0,) * a.ndim)
    once = lambda a: pl.BlockSpec(a.shape, lambda b, i: (0,) * a.ndim,
                                  pipeline_mode=pl.Buffered(1))
    prev = pl.BlockSpec((WINDOW, 2 * ATT_KV_W),
                        lambda b, i: (jnp.maximum((b * nt + i) * sub - 1, 0), 0))
    per_batch = lambda *dims: pl.BlockSpec((1,) + dims, lambda b, i: (b,) + (0,) * len(dims))
    return pl.pallas_call(
        _prompt_mixer_kernel,
        grid=(batch, nt),
        in_specs=[pl.BlockSpec(memory_space=pltpu.SMEM),
                  row(ATT_Q_W), row(2 * ATT_KV_W), prev, row(ML_QK_W), row(ML_QK_W),
                  row(ML_V_W), row(ML_V_W),
                  pl.BlockSpec((N_GATES, tq), lambda b, i: (0, b * nt + i)),
                  row(D_MODEL), once(wout), full(mlnw)],
        out_specs=[row(D_MODEL),
                   per_batch(ML_HEADS, ML_QK_DIM, ML_V_DIM),
                   per_batch(ML_HEADS // 2, LANES),
                   per_batch(1, ML_HEADS),
                   per_batch(ATT_KV_W, WINDOW),
                   per_batch(ATT_KV_W, WINDOW)],
        out_shape=[jax.ShapeDtypeStruct((batch * seq, D_MODEL), F32),
                   jax.ShapeDtypeStruct((batch, ML_HEADS, ML_QK_DIM, ML_V_DIM), F32),
                   jax.ShapeDtypeStruct((batch, ML_HEADS // 2, LANES), F32),
                   jax.ShapeDtypeStruct((batch, 1, ML_HEADS), F32),
                   jax.ShapeDtypeStruct((batch, ATT_KV_W, WINDOW), F32),
                   jax.ShapeDtypeStruct((batch, ATT_KV_W, WINDOW), F32)],
        scratch_shapes=[pltpu.VMEM((tq, D_MODEL), BF16),
                        pltpu.VMEM((D_MODEL, D_MODEL), BF16),
                        pltpu.VMEM((ML_HEADS // 2, LANES, 2 * LANES), F32),
                        pltpu.VMEM((SUBLANES, LANES), F32)],
        compiler_params=pltpu.CompilerParams(dimension_semantics=("arbitrary", "arbitrary"),
                                             vmem_limit_bytes=VMEM_LIMIT),
        name="prompt_mixer",
    )(sinks, qa, kv, kv, qm, km, vm, om, gt, x2d, wout, mlnw)


def _sample_mixer_kernel(t_len, sinks_ref, qa_ref, kv_ref, ck_ref, cv_ref, qm_ref, km_ref,
                         vm_ref, om_ref, gt_ref, c0_ref, n0_ref, m0_ref, x_ref, wout_ref,
                         mlnw_ref, x1_ref, nk_ref, nv_ref, c_ref, n_ref, m_ref,
                         mix_scr, wperm_scr):
    bt = SAMPLE_BT
    T = t_len
    L = bt * T

    @pl.when(pl.program_id(0) == 0)
    def _():
        _permute_head_rows(wperm_scr, wout_ref)
        wperm_scr[ATT_Q_W:, :] = wout_ref[ATT_Q_W:, :]

    lane3 = lax.broadcasted_iota(jnp.int32, (bt, T, LANES), 2)
    low3 = lane3 < ATT_HEAD_DIM
    lane = lax.broadcasted_iota(jnp.int32, (L, LANES), 1)
    low = lane < ATT_HEAD_DIM

    qa3 = qa_ref[...].astype(F32).reshape(bt, T, ATT_Q_W)
    pieces = []
    for col in range(ATT_GROUP):
        qc = qa3[:, :, col * LANES:(col + 1) * LANES]
        pieces += [jnp.where(low3, qc, 0.0), jnp.where(low3, 0.0, qc)]
    q3 = jnp.concatenate(pieces, axis=1).astype(BF16)
    R = bt * N_STACK * T
    q2 = q3.reshape(R, LANES)
    kv_new = kv_ref[...]
    k_new = kv_new[:, :ATT_KV_W]
    v_new = kv_new[:, ATT_KV_W:]
    ck = ck_ref[...]
    cv = cv_ref[...]
    s_c = jnp.einsum('bqd,bdk->bqk', q3, ck.astype(BF16),
                     preferred_element_type=F32).reshape(R, WINDOW)
    s_n = _dot_nt(q2, k_new.astype(BF16))
    row_c = lax.broadcasted_iota(jnp.int32, (R, WINDOW), 0)
    col_c = lax.broadcasted_iota(jnp.int32, (R, WINDOW), 1)
    s_c = jnp.where(col_c > row_c % T, s_c, -jnp.inf)
    row_n = lax.broadcasted_iota(jnp.int32, (R, L), 0)
    col_n = lax.broadcasted_iota(jnp.int32, (R, L), 1)
    valid_n = (row_n // (N_STACK * T) == col_n // T) & (col_n % T <= row_n % T)
    s_n = jnp.where(valid_n, s_n, -jnp.inf)
    stack_id = (lax.broadcasted_iota(jnp.int32, (R, 1), 0) // T) % N_STACK
    sink = jnp.zeros((R, 1), F32)
    for k_id in range(N_STACK):
        sink = jnp.where(stack_id == k_id, sinks_ref[ATT_HEAD_ORDER[k_id]], sink)
    m = jnp.maximum(jnp.maximum(jnp.max(s_c, axis=-1, keepdims=True),
                                jnp.max(s_n, axis=-1, keepdims=True)), sink)
    e_c = jnp.exp(s_c - m)
    e_n = jnp.exp(s_n - m)
    denom = (jnp.sum(e_c, axis=-1, keepdims=True) + jnp.sum(e_n, axis=-1, keepdims=True)
             + jnp.exp(sink - m))
    o = jnp.einsum('bqk,bdk->bqd', e_c.astype(BF16).reshape(bt, N_STACK * T, WINDOW),
                   cv.astype(BF16), preferred_element_type=F32).reshape(R, LANES)
    o = (o + _dot(e_n.astype(BF16), v_new.astype(BF16))) / denom
    o3 = o.reshape(bt, N_STACK * T, LANES)
    for col in range(ATT_GROUP):
        lo_h = o3[:, (2 * col) * T:(2 * col + 1) * T, :]
        hi_h = o3[:, (2 * col + 1) * T:(2 * col + 2) * T, :]
        mix_scr[:, col * LANES:(col + 1) * LANES] = jnp.where(
            low3, lo_h, hi_h).reshape(L, LANES).astype(BF16)

    keep = lax.broadcasted_iota(jnp.int32, (ATT_KV_W, WINDOW), 1) < WINDOW - T
    k_new_t = k_new.T
    v_new_t = v_new.T
    for q in range(bt):
        shift = (WINDOW - T - q * T) % WINDOW
        nk_ref[q] = jnp.where(keep, pltpu.roll(ck[q], WINDOW - T, axis=1),
                              pltpu.roll(k_new_t, shift, axis=1))
        nv_ref[q] = jnp.where(keep, pltpu.roll(cv[q], WINDOW - T, axis=1),
                              pltpu.roll(v_new_t, shift, axis=1))

    r = lax.broadcasted_iota(jnp.int32, (L, L), 0)
    c = lax.broadcasted_iota(jnp.int32, (L, L), 1)
    seg = (r // T == c // T) & (c <= r)
    gates = gt_ref[...]
    cum_row, cum_col, raw_col = _gate_forms(gates, seg, True)
    ones_col = _ones_col(L)
    qm = qm_ref[...]
    km = km_ref[...]
    qm_f = qm.astype(F32)
    km_f = km.astype(F32)
    for h in range(ML_HEADS):
        p, e = divmod(h, 2)
        qc = qm[:, p * LANES:(p + 1) * LANES]
        k_pair = km[:, p * LANES:(p + 1) * LANES]
        zero = jnp.zeros_like(qc)
        q_pad = jnp.where(low, qc, zero) if e == 0 else jnp.where(low, zero, qc)
        v_h = vm_ref[:, h * ML_V_DIM:(h + 1) * ML_V_DIM]
        v_ext = jnp.concatenate([v_h, ones_col], axis=1)
        b_c = cum_col[:, ML_HEADS + h:ML_HEADS + h + 1]
        b_r = cum_row[ML_HEADS + h:ML_HEADS + h + 1, :]
        ig_r = gates[h:h + 1, :]
        ig_c = raw_col[:, h:h + 1]
        m0 = m0_ref[:, :, h:h + 1]
        m_prev_c = jnp.broadcast_to(m0, (bt, T, 1)).reshape(L, 1)
        pv, m_row, w_inter = _mlstm_intra(q_pad, k_pair, v_ext, seg, b_c, b_r, ig_r, m_prev_c)
        q_h3 = qm_f[:, h * ML_QK_DIM:(h + 1) * ML_QK_DIM].reshape(bt, T, ML_QK_DIM)
        k_h3 = km_f[:, h * ML_QK_DIM:(h + 1) * ML_QK_DIM].reshape(bt, T, ML_QK_DIM)
        c0 = c0_ref[:, h]
        n0 = n0_ref[:, h:h + 1, :]
        q_c = jnp.einsum('btd,bdv->btv', q_h3.astype(BF16), c0.astype(BF16),
                         preferred_element_type=F32).reshape(L, ML_V_DIM)
        q_n = jnp.sum(q_h3 * n0, axis=-1, keepdims=True).reshape(L, 1)
        mix_scr[:, ATT_Q_W + h * ML_V_DIM:ATT_Q_W + (h + 1) * ML_V_DIM] = _mlstm_out(
            pv, m_row, w_inter, q_c, q_n, mlnw_ref[:, h * ML_V_DIM:(h + 1) * ML_V_DIM],
            om_ref[:, h * ML_V_DIM:(h + 1) * ML_V_DIM])
        b3 = b_c.reshape(bt, T, 1)
        b_last = b3[:, T - 1:T, :]
        a3 = b_last - b3 + ig_c.reshape(bt, T, 1)
        m_new = jnp.maximum(b_last + m0, jnp.max(a3, axis=1, keepdims=True))
        sc = jnp.exp(b_last + m0 - m_new)
        ws = jnp.exp(a3 - m_new)
        kw = (ws * k_h3).astype(BF16)
        v3 = v_h.astype(F32).reshape(bt, T, ML_V_DIM).astype(BF16)
        d_c = jnp.einsum('bsd,bsv->bdv', kw, v3, preferred_element_type=F32)
        c_ref[:, h] = sc * c0 + d_c
        n_ref[:, h:h + 1, :] = sc * n0 + jnp.sum(ws * k_h3, axis=1, keepdims=True)
        m_ref[:, :, h:h + 1] = m_new

    x1_ref[...] = x_ref[...] + _dot(mix_scr[...], wperm_scr[...])


def _sample_mixer(nb, t_len, sinks, qa, kv, ck, cv, qm, km, vm, om, gt, c0, n0, m0, x2d, wout, mlnw):
    bt = SAMPLE_BT
    tl = bt * t_len
    row = lambda w: pl.BlockSpec((tl, w), lambda i: (i, 0))
    full = lambda a: pl.BlockSpec(a.shape, lambda i: (0,) * a.ndim)
    once = lambda a: pl.BlockSpec(a.shape, lambda i: (0,) * a.ndim, pipeline_mode=pl.Buffered(1))
    cache = pl.BlockSpec((bt, ATT_KV_W, WINDOW), lambda i: (i, 0, 0))
    c_spec = pl.BlockSpec((bt, ML_HEADS, ML_QK_DIM, ML_V_DIM), lambda i: (i, 0, 0, 0))
    n_spec = pl.BlockSpec((bt, ML_HEADS, ML_QK_DIM), lambda i: (i, 0, 0))
    m_spec = pl.BlockSpec((bt, 1, ML_HEADS), lambda i: (i, 0, 0))
    return pl.pallas_call(
        functools.partial(_sample_mixer_kernel, t_len),
        grid=(nb // bt,),
        in_specs=[pl.BlockSpec(memory_space=pltpu.SMEM),
                  row(ATT_Q_W), row(2 * ATT_KV_W), cache, cache, row(ML_QK_W), row(ML_QK_W),
                  row(ML_V_W), row(ML_V_W), pl.BlockSpec((N_GATES, tl), lambda i: (0, i)),
                  c_spec, n_spec, m_spec, row(D_MODEL), once(wout), full(mlnw)],
        out_specs=[row(D_MODEL), cache, cache, c_spec, n_spec, m_spec],
        out_shape=[jax.ShapeDtypeStruct((nb * t_len, D_MODEL), F32),
                   jax.ShapeDtypeStruct((nb, ATT_KV_W, WINDOW), F32),
                   jax.ShapeDtypeStruct((nb, ATT_KV_W, WINDOW), F32),
                   jax.ShapeDtypeStruct((nb, ML_HEADS, ML_QK_DIM, ML_V_DIM), F32),
                   jax.ShapeDtypeStruct((nb, ML_HEADS, ML_QK_DIM), F32),
                   jax.ShapeDtypeStruct((nb, 1, ML_HEADS), F32)],
        scratch_shapes=[pltpu.VMEM((tl, D_MODEL), BF16),
                        pltpu.VMEM((D_MODEL, D_MODEL), BF16)],
        compiler_params=pltpu.CompilerParams(dimension_semantics=("arbitrary",),
                                             vmem_limit_bytes=VMEM_LIMIT),
        name="sample_mixer",
    )(sinks, qa, kv, ck, cv, qm, km, vm, om, gt, c0, n0, m0, x2d, wout, mlnw)


def _ffn_kernel(seq_rows, *refs):
    if seq_rows is None:
        (x_ref, nw_ref, w_ref, cw_ref, cb_ref, wd_ref, y_ref, conv_ref,
         gbuf, act_scr, carry) = refs
        hist_ref = None
    else:
        (x_ref, hist_ref, nw_ref, w_ref, cw_ref, cb_ref, wd_ref, y_ref, conv_ref,
         gbuf, act_scr) = refs
        carry = None
    tm = x_ref.shape[0]
    tf = FF_CHUNK
    n_hist = CONV_W - 1
    rows = tm if seq_rows is None else seq_rows
    nseq = tm // rows
    base = SUBLANES
    n_chunks = D_FF // tf

    if carry is not None:
        @pl.when(pl.program_id(1) == 0)
        def _():
            carry[...] = jnp.zeros(carry.shape, F32)

    x = x_ref[...]
    h2 = _rms(x, nw_ref[...]).astype(BF16)

    def proj(f):
        return (_dot(h2, w_ref[:, f * tf:(f + 1) * tf]),
                _dot(h2, w_ref[:, D_FF + f * tf:D_FF + (f + 1) * tf]))

    nxt = proj(0)
    for f in range(n_chunks):
        g, u = nxt
        if f + 1 < n_chunks:
            nxt = proj(f + 1)
        cols = slice(f * tf, (f + 1) * tf)
        s = f % 2
        g3 = g.reshape(nseq, rows, tf)
        if seq_rows is None:
            gbuf[s, :, base - n_hist:base, :] = carry[:, SUBLANES - n_hist:, cols]
            carry[:, SUBLANES - n_hist:, cols] = g3[:, rows - n_hist:, :]
        else:
            gbuf[s, :, base - n_hist:base, :] = hist_ref[:, :, cols]
            conv_ref[:, :, cols] = g3[:, rows - n_hist:, :]
        gbuf[s, :, base:base + rows, :] = g3
        gc = cb_ref[:, cols] + g * cw_ref[CONV_W - 1:CONV_W, cols]
        for d in range(1, CONV_W):
            gm = gbuf[s, :, base - d:base - d + rows, :].reshape(tm, tf)
            gc = gc + gm * cw_ref[CONV_W - 1 - d:CONV_W - d, cols]
        act_scr[:, cols] = (gc * _sigmoid(gc) * u).astype(BF16)
    y_ref[...] = x + _dot(act_scr[...], wd_ref[...])

    if carry is not None:
        @pl.when(pl.program_id(1) == pl.num_programs(1) - 1)
        def _():
            conv_ref[...] = carry[:, SUBLANES - n_hist:, :]


def _ffn_scratch(tm, rows):
    return [pltpu.VMEM((2, tm // rows, SUBLANES + rows, FF_CHUNK), F32),
            pltpu.VMEM((tm, D_FF), BF16)]


def _ffn_prompt(batch, seq, x2d, nw, w, cw, cb, wd):
    tm = ROW_TILE
    nt = seq // tm
    full = lambda a: pl.BlockSpec(a.shape, lambda b, i: (0,) * a.ndim)
    once = lambda a: pl.BlockSpec(a.shape, lambda b, i: (0,) * a.ndim,
                                  pipeline_mode=pl.Buffered(1))
    row = pl.BlockSpec((tm, D_MODEL), lambda b, i: (b * nt + i, 0))
    return pl.pallas_call(
        functools.partial(_ffn_kernel, None),
        grid=(batch, nt),
        in_specs=[row, full(nw), once(w), full(cw), full(cb), once(wd)],
        out_specs=[row, pl.BlockSpec((1, CONV_W - 1, D_FF), lambda b, i: (b, 0, 0))],
        out_shape=[jax.ShapeDtypeStruct((batch * seq, D_MODEL), F32),
                   jax.ShapeDtypeStruct((batch, CONV_W - 1, D_FF), F32)],
        scratch_shapes=_ffn_scratch(tm, tm) + [pltpu.VMEM((1, SUBLANES, D_FF), F32)],
        compiler_params=pltpu.CompilerParams(dimension_semantics=("arbitrary", "arbitrary"),
                                             vmem_limit_bytes=VMEM_LIMIT),
        name="ffn_prompt",
    )(x2d, nw, w, cw, cb, wd)


def _ffn_sample(nb, t_len, x2d, hist, nw, w, cw, cb, wd):
    tm = ROW_TILE
    bt = tm // t_len
    full = lambda a: pl.BlockSpec(a.shape, lambda i: (0,) * a.ndim)
    once = lambda a: pl.BlockSpec(a.shape, lambda i: (0,) * a.ndim, pipeline_mode=pl.Buffered(1))
    row = pl.BlockSpec((tm, D_MODEL), lambda i: (i, 0))
    hist_spec = pl.BlockSpec((bt, CONV_W - 1, D_FF), lambda i: (i, 0, 0))
    return pl.pallas_call(
        functools.partial(_ffn_kernel, t_len),
        grid=(nb // bt,),
        in_specs=[row, hist_spec, full(nw), once(w), full(cw), full(cb), once(wd)],
        out_specs=[row, hist_spec],
        out_shape=[jax.ShapeDtypeStruct((nb * t_len, D_MODEL), F32),
                   jax.ShapeDtypeStruct((nb, CONV_W - 1, D_FF), F32)],
        scratch_shapes=_ffn_scratch(tm, t_len),
        compiler_params=pltpu.CompilerParams(dimension_semantics=("arbitrary",),
                                             vmem_limit_bytes=VMEM_LIMIT),
        name="ffn_sample",
    )(x2d, hist, nw, w, cw, cb, wd)


def _head_mean_matrix(width, head_dim):
    idx = np.arange(width) // head_dim
    return jnp.asarray((idx[:, None] == idx[None, :]).astype(np.float32) / head_dim, dtype=BF16)


def _layer_weights(norm_mix_w, w_in, b_gates, q_norm_w, k_norm_w, sinks, ml_norm_w, w_out,
                   norm_ffn_w, w_ffn_in, conv_w, conv_b, w_down):
    w_in_t = jnp.pad(w_in.T.astype(BF16), ((0, IN_WIDTH_PAD - w_in.shape[1]), (0, 0)))
    return dict(
        nw=norm_mix_w.reshape(1, D_MODEL),
        w_in_t=w_in_t,
        bg=jnp.pad(b_gates, (0, LANES - N_GATES)).reshape(1, LANES),
        qnw=(jnp.tile(q_norm_w, ATT_HEADS) * ATT_SCALE).reshape(1, ATT_Q_W),
        knw=jnp.tile(k_norm_w, ATT_KV_HEADS).reshape(1, ATT_KV_W),
        gq=_head_mean_matrix(ATT_Q_W, ATT_HEAD_DIM),
        gk=_head_mean_matrix(ATT_KV_W, ATT_HEAD_DIM),
        sinks=sinks,
        mlnw=ml_norm_w.reshape(1, ML_V_W),
        wout=w_out.astype(BF16),
        nfw=norm_ffn_w.reshape(1, D_MODEL),
        wff=w_ffn_in.astype(BF16),
        cw=conv_w,
        cb=conv_b.reshape(1, D_FF),
        wd=w_down.astype(BF16),
    )


def _cache_from_t(a_t):
    n = a_t.shape[0]
    return jnp.transpose(a_t.reshape(n, ATT_KV_HEADS, ATT_HEAD_DIM, WINDOW), (0, 3, 1, 2))


def _cache_to_t(a):
    n = a.shape[0]
    return jnp.transpose(a, (0, 2, 3, 1)).reshape(n, ATT_KV_W, WINDOW)


def _prompt_layer(x, w):
    batch, seq, _ = x.shape
    assert seq % ROW_TILE == 0 and seq % MIX_TILE == 0 and MIX_TILE % WINDOW == 0
    x2d = x.reshape(batch * seq, D_MODEL)
    qa, kv, qm, km, vm, om, gt = _inproj(x2d, w["nw"], w["w_in_t"], w["bg"], w["qnw"], w["knw"],
                                         w["gq"], w["gk"])
    x1, c_t, n_row, m, k_t, v_t = _prompt_mixer(batch, seq, w["sinks"], qa, kv, qm, km, vm, om,
                                                gt, x2d, w["wout"], w["mlnw"])
    y, conv = _ffn_prompt(batch, seq, x1, w["nfw"], w["wff"], w["cw"], w["cb"], w["wd"])
    return (y.reshape(batch, seq, D_MODEL), _cache_from_t(k_t), _cache_from_t(v_t),
            jnp.swapaxes(c_t, -1, -2), n_row.reshape(batch, ML_HEADS, ML_QK_DIM),
            m.reshape(batch, ML_HEADS), conv)


def _sample_layer(x, ck, cv, c0, n0, m0, conv_buf, w):
    nb, t_len, _ = x.shape
    assert t_len == SUBLANES and SAMPLE_BT * t_len == LANES and nb % SAMPLE_BT == 0
    assert (nb * t_len) % ROW_TILE == 0
    x2d = x.reshape(nb * t_len, D_MODEL)
    qa, kv, qm, km, vm, om, gt = _inproj(x2d, w["nw"], w["w_in_t"], w["bg"], w["qnw"], w["knw"],
                                         w["gq"], w["gk"])
    x1, nk_t, nv_t, c_t, n, m = _sample_mixer(
        nb, t_len, w["sinks"], qa, kv, _cache_to_t(ck), _cache_to_t(cv), qm, km, vm, om, gt,
        jnp.swapaxes(c0, -1, -2), n0, m0.reshape(nb, 1, ML_HEADS), x2d, w["wout"], w["mlnw"])
    y, conv = _ffn_sample(nb, t_len, x1, conv_buf, w["nfw"], w["wff"], w["cw"], w["cb"],
                          w["wd"])
    return (y.reshape(nb, t_len, D_MODEL), _cache_from_t(nk_t), _cache_from_t(nv_t),
            jnp.swapaxes(c_t, -1, -2), n, m.reshape(nb, ML_HEADS), conv)


def kernel(x_prompt, x_sample, cache_attn_k, cache_attn_v, state_mlstm_C, state_mlstm_n,
           state_mlstm_m, cache_ffn_conv, norm_mix_w, w_in, b_gates, q_norm_w, k_norm_w,
           sinks, ml_norm_w, w_out, norm_ffn_w, w_ffn_in, conv_w, conv_b, w_down):
    depth = w_in.shape[0]
    yp, ys = x_prompt, x_sample
    sp = [[] for _ in range(6)]
    ss = [[] for _ in range(6)]
    for l in range(depth):
        w = _layer_weights(norm_mix_w[l], w_in[l], b_gates[l], q_norm_w[l], k_norm_w[l], sinks[l],
                           ml_norm_w[l], w_out[l], norm_ffn_w[l], w_ffn_in[l], conv_w[l],
                           conv_b[l], w_down[l])
        yp, *st_p = _prompt_layer(yp, w)
        ys, *st_s = _sample_layer(ys, cache_attn_k[l], cache_attn_v[l], state_mlstm_C[l],
                                  state_mlstm_n[l], state_mlstm_m[l], cache_ffn_conv[l], w)
        for i in range(6):
            sp[i].append(st_p[i])
            ss[i].append(st_s[i])
    k_p, v_p, c_p, n_p, m_p, conv_p = [jnp.stack(a) for a in sp]
    k_s, v_s, c_s, n_s, m_s, conv_s = [jnp.stack(a) for a in ss]
    return (yp, ys, k_p, v_p, c_p, n_p, m_p, conv_p, k_s, v_s, c_s, n_s, m_s, conv_s)
```

```python
import functools

import numpy as np
import jax
import jax.numpy as jnp
from jax import lax
from jax.experimental import pallas as pl
from jax.experimental.pallas import tpu as pltpu

F32 = jnp.float32
BF16 = jnp.bfloat16

D_MODEL = 1024
ATT_HEADS = 8
ATT_KV_HEADS = 2
ATT_HEAD_DIM = 64
ATT_GROUP = ATT_HEADS // ATT_KV_HEADS
WINDOW = 128
ML_HEADS = 4
ML_V_DIM = 128
ML_QK_DIM = 64
D_FF = 2816
CONV_W = 3
EPS = 1e-6
ATT_SCALE = ATT_HEAD_DIM ** -0.5
ML_SCALE = ML_QK_DIM ** -0.5

ATT_Q_W = ATT_HEADS * ATT_HEAD_DIM
ATT_KV_W = ATT_KV_HEADS * ATT_HEAD_DIM
ML_QK_W = ML_HEADS * ML_QK_DIM
ML_V_W = ML_HEADS * ML_V_DIM
N_GATES = 2 * ML_HEADS
N_STACK = 2 * ATT_GROUP

LANES = 128
SUBLANES = 8

OFF_QA = 0
OFF_KV = OFF_QA + ATT_Q_W
OFF_QM = OFF_KV + 2 * ATT_KV_W
OFF_KM = OFF_QM + ML_QK_W
OFF_VM = OFF_KM + ML_QK_W
OFF_OM = OFF_VM + ML_V_W
OFF_GL = OFF_OM + ML_V_W
IN_WIDTH_PAD = OFF_GL + LANES

ATT_HEAD_ORDER = tuple(h for c in range(ATT_GROUP) for h in (c, c + ATT_GROUP))

ROW_TILE = 512
MIX_TILE = 256
FF_CHUNK = 256
SAMPLE_BT = 16
VMEM_LIMIT = 56 * 1024 * 1024


def _dot(a, b):
    return jnp.dot(a, b, preferred_element_type=F32)


def _dot_nt(a, b):
    return lax.dot_general(a, b, (((1,), (1,)), ((), ())), preferred_element_type=F32)


def _split3(x):
    hi = x.astype(BF16)
    r1 = x - hi.astype(F32)
    mid = r1.astype(BF16)
    lo = (r1 - mid.astype(F32)).astype(BF16)
    return hi, mid, lo


def _rms(x, w):
    ms = jnp.mean(x * x, axis=-1, keepdims=True)
    return x * lax.rsqrt(ms + EPS) * w


def _log_sigmoid(x):
    return jnp.minimum(x, 0.0) - jnp.log1p(jnp.exp(-jnp.abs(x)))


def _sigmoid(x):
    return 1.0 / (1.0 + jnp.exp(-x))


def _permute_head_rows(dst_ref, src_ref):
    for k, h in enumerate(ATT_HEAD_ORDER):
        dst_ref[k * ATT_HEAD_DIM:(k + 1) * ATT_HEAD_DIM, :] = (
            src_ref[h * ATT_HEAD_DIM:(h + 1) * ATT_HEAD_DIM, :])


def _inproj_kernel(x_ref, nw_ref, w_ref, bg_ref, qnw_ref, knw_ref, gq_ref, gk_ref,
                   qa_ref, kv_ref, qm_ref, km_ref, vm_ref, om_ref, gt_ref, wq_scr):
    @pl.when(pl.program_id(0) == 0)
    def _():
        _permute_head_rows(wq_scr, w_ref)

    h = _rms(x_ref[...], nw_ref[...]).astype(BF16)

    def proj(lo, width):
        return _dot_nt(h, w_ref[lo:lo + width, :])

    q = _dot_nt(h, wq_scr[...])
    q_ms = _dot((q * q).astype(BF16), gq_ref[...])
    qa_ref[...] = (q * lax.rsqrt(q_ms + EPS) * qnw_ref[...]).astype(BF16)

    kv = proj(OFF_KV, 2 * ATT_KV_W)
    k = kv[:, :ATT_KV_W]
    k_ms = _dot((k * k).astype(BF16), gk_ref[...])
    kv_ref[:, :ATT_KV_W] = k * lax.rsqrt(k_ms + EPS) * knw_ref[...]
    kv_ref[:, ATT_KV_W:] = kv[:, ATT_KV_W:]

    qm_ref[...] = (proj(OFF_QM, ML_QK_W) * ML_SCALE).astype(BF16)
    km_ref[...] = proj(OFF_KM, ML_QK_W).astype(BF16)
    vm_ref[...] = proj(OFF_VM, ML_V_W).astype(BF16)
    om_ref[...] = proj(OFF_OM, ML_V_W).astype(BF16)

    gl = proj(OFF_GL, LANES) + bg_ref[...]
    lane = lax.broadcasted_iota(jnp.int32, gl.shape, 1)
    g = jnp.where(lane < ML_HEADS, gl, _log_sigmoid(gl))
    gt_ref[...] = g.T[:N_GATES, :]


def _inproj(x2d, nw, w_in_t, bg, qnw, knw, gq, gk):
    n = x2d.shape[0]
    tm = ROW_TILE
    row = lambda w: pl.BlockSpec((tm, w), lambda i: (i, 0))
    full = lambda a: pl.BlockSpec(a.shape, lambda i: (0,) * a.ndim)
    once = lambda a: pl.BlockSpec(a.shape, lambda i: (0,) * a.ndim, pipeline_mode=pl.Buffered(1))
    return pl.pallas_call(
        _inproj_kernel,
        grid=(n // tm,),
        in_specs=[row(D_MODEL), full(nw), once(w_in_t), full(bg), full(qnw), full(knw),
                  full(gq), full(gk)],
        out_specs=[row(ATT_Q_W), row(2 * ATT_KV_W), row(ML_QK_W), row(ML_QK_W),
                   row(ML_V_W), row(ML_V_W), pl.BlockSpec((N_GATES, tm), lambda i: (0, i))],
        out_shape=[jax.ShapeDtypeStruct((n, ATT_Q_W), BF16),
                   jax.ShapeDtypeStruct((n, 2 * ATT_KV_W), F32),
                   jax.ShapeDtypeStruct((n, ML_QK_W), BF16),
                   jax.ShapeDtypeStruct((n, ML_QK_W), BF16),
                   jax.ShapeDtypeStruct((n, ML_V_W), BF16),
                   jax.ShapeDtypeStruct((n, ML_V_W), BF16),
                   jax.ShapeDtypeStruct((N_GATES, n), F32)],
        scratch_shapes=[pltpu.VMEM((ATT_Q_W, D_MODEL), BF16)],
        compiler_params=pltpu.CompilerParams(dimension_semantics=("arbitrary",),
                                             vmem_limit_bytes=VMEM_LIMIT),
        name="inproj",
    )(x2d, nw, w_in_t, bg, qnw, knw, gq, gk)


def _head_norm_t(z, head_dim, w_col):
    rows, tokens = z.shape
    z3 = z.reshape(rows // head_dim, head_dim, tokens)
    ms = jnp.mean(z3 * z3, axis=1, keepdims=True)
    return (z3 * lax.rsqrt(ms + EPS)).reshape(rows, tokens) * w_col


def _inproj_t_kernel(x_ref, nw_ref, w_ref, bg_ref, qnw_ref, knw_ref,
                     qa_ref, ks_ref, kv_ref, qm_ref, km_ref, vm_ref, om_ref, gt_ref):
    h = _rms(x_ref[...], nw_ref[...]).astype(BF16)

    def proj(lo, width):
        return _dot_nt(w_ref[lo:lo + width, :], h)

    qa_ref[...] = _head_norm_t(proj(OFF_QA, ATT_Q_W), ATT_HEAD_DIM, qnw_ref[...]).astype(BF16)
    kv = proj(OFF_KV, 2 * ATT_KV_W)
    k = _head_norm_t(kv[:ATT_KV_W], ATT_HEAD_DIM, knw_ref[...])
    kv_ref[:ATT_KV_W, :] = k
    kv_ref[ATT_KV_W:, :] = kv[ATT_KV_W:]
    ks_ref[...] = k.T.astype(BF16)
    qm_ref[...] = (proj(OFF_QM, ML_QK_W) * ML_SCALE).astype(BF16)
    km_ref[...] = proj(OFF_KM, ML_QK_W).astype(BF16)
    vm_ref[...] = proj(OFF_VM, ML_V_W).astype(BF16)
    om_ref[...] = proj(OFF_OM, ML_V_W).astype(BF16)
    gl = proj(OFF_GL, 2 * SUBLANES)[:N_GATES] + bg_ref[...]
    row = lax.broadcasted_iota(jnp.int32, gl.shape, 0)
    gt_ref[...] = jnp.where(row < ML_HEADS, gl, _log_sigmoid(gl))


def _inproj_t(x2d, nw, w_in_t, bg_col, qnw_col, knw_col):
    n = x2d.shape[0]
    tm = ROW_TILE
    full = lambda a: pl.BlockSpec(a.shape, lambda i: (0,) * a.ndim)
    once = lambda a: pl.BlockSpec(a.shape, lambda i: (0,) * a.ndim, pipeline_mode=pl.Buffered(1))
    col = lambda w: pl.BlockSpec((w, tm), lambda i: (0, i))
    return pl.pallas_call(
        _inproj_t_kernel,
        grid=(n // tm,),
        in_specs=[pl.BlockSpec((tm, D_MODEL), lambda i: (i, 0)), full(nw), once(w_in_t),
                  full(bg_col), full(qnw_col), full(knw_col)],
        out_specs=[col(ATT_Q_W), pl.BlockSpec((tm, ATT_KV_W), lambda i: (i, 0)),
                   col(2 * ATT_KV_W), col(ML_QK_W), col(ML_QK_W), col(ML_V_W), col(ML_V_W),
                   col(N_GATES)],
        out_shape=[jax.ShapeDtypeStruct((ATT_Q_W, n), BF16),
                   jax.ShapeDtypeStruct((n, ATT_KV_W), BF16),
                   jax.ShapeDtypeStruct((2 * ATT_KV_W, n), F32),
                   jax.ShapeDtypeStruct((ML_QK_W, n), BF16),
                   jax.ShapeDtypeStruct((ML_QK_W, n), BF16),
                   jax.ShapeDtypeStruct((ML_V_W, n), BF16),
                   jax.ShapeDtypeStruct((ML_V_W, n), BF16),
                   jax.ShapeDtypeStruct((N_GATES, n), F32)],
        compiler_params=pltpu.CompilerParams(dimension_semantics=("arbitrary",),
                                             vmem_limit_bytes=VMEM_LIMIT),
        name="inproj_t",
    )(x2d, nw, w_in_t, bg_col, qnw_col, knw_col)


def _gate_forms(gates, seg_mask, want_raw_col):
    L = gates.shape[1]
    m_bf = seg_mask.astype(F32).astype(BF16)
    cum_row = jnp.zeros(gates.shape, F32)
    cum_col = jnp.zeros((L, gates.shape[0]), F32)
    raw_col = None
    if want_raw_col:
        r = lax.broadcasted_iota(jnp.int32, (L, L), 0)
        c = lax.broadcasted_iota(jnp.int32, (L, L), 1)
        eye = (r == c).astype(F32).astype(BF16)
        raw_col = jnp.zeros((L, gates.shape[0]), F32)
    for part in _split3(gates):
        cum_row = cum_row + _dot_nt(part, m_bf)
        cum_col = cum_col + _dot_nt(m_bf, part)
        if want_raw_col:
            raw_col = raw_col + _dot_nt(eye, part)
    return cum_row, cum_col, raw_col


def _mlstm_intra(q_pad, k_pair, v_ext, seg_mask, b_c, b_r, ig_r, m_prev_c):
    dm = jnp.where(seg_mask, b_c + (ig_r - b_r), -jnp.inf)
    inter = b_c + m_prev_c
    m_row = jnp.maximum(inter, jnp.max(dm, axis=-1, keepdims=True))
    w_inter = jnp.exp(inter - m_row)
    p = _dot_nt(q_pad, k_pair) * jnp.exp(dm - m_row)
    return _dot(p.astype(BF16), v_ext), m_row, w_inter


def _mlstm_out(pv, m_row, w_inter, q_c, q_n, mlnw_h, om_h):
    num = pv[:, :ML_V_DIM] + w_inter * q_c
    den = pv[:, ML_V_DIM:ML_V_DIM + 1] + w_inter * q_n
    hh = num / jnp.maximum(jnp.abs(den), jnp.exp(-m_row))
    return (_rms(hh, mlnw_h) * _sigmoid(om_h.astype(F32))).astype(BF16)


def _ones_col(rows):
    lane = lax.broadcasted_iota(jnp.int32, (rows, LANES), 1)
    return (lane == 0).astype(F32).astype(BF16)


def _stack_sinks(sinks_ref):
    idx = lax.broadcasted_iota(jnp.int32, (N_STACK, 1, 1), 0)
    sink = jnp.zeros((N_STACK, 1, 1), F32)
    for k, h in enumerate(ATT_HEAD_ORDER):
        sink = jnp.where(idx == k, sinks_ref[h], sink)
    return sink


def _prompt_mixer_kernel(sinks_ref, qa_ref, kvc_ref, kvp_ref, qm_ref, km_ref, vm_ref,
                         om_ref, gt_ref, x_ref, wout_ref, mlnw_ref,
                         x1_ref, ct_ref, nrow_ref, m_ref, kt_ref, vt_ref,
                         mix_scr, wperm_scr, state_scr, m_scr, s_scr, e_scr):
    b = pl.program_id(0)
    i = pl.program_id(1)
    A = WINDOW
    L = MIX_TILE
    n_pairs = ML_HEADS // 2

    @pl.when((b == 0) & (i == 0))
    def _():
        _permute_head_rows(wperm_scr, wout_ref)
        wperm_scr[ATT_Q_W:, :] = wout_ref[ATT_Q_W:, :]

    @pl.when(i == 0)
    def _():
        state_scr[...] = jnp.zeros(state_scr.shape, F32)
        m_scr[...] = jnp.zeros(m_scr.shape, F32)

    lane = lax.broadcasted_iota(jnp.int32, (A, LANES), 1)
    low = lane < ATT_HEAD_DIM
    qi = lax.broadcasted_iota(jnp.int32, (A, 2 * A), 0)
    kj = lax.broadcasted_iota(jnp.int32, (A, 2 * A), 1)
    band = (kj > qi) & (kj <= qi + WINDOW)
    sink = _stack_sinks(sinks_ref)
    kv_all = jnp.concatenate([kvp_ref[...], kvc_ref[...]], axis=0)
    ones2 = _ones_col(2 * A)
    for j in range(L // A):
        rows = slice(j * A, (j + 1) * A)
        kk = kv_all[j * A:(j + 2) * A, :ATT_KV_W].astype(BF16)
        vvx = jnp.concatenate([kv_all[j * A:(j + 2) * A, ATT_KV_W:].astype(BF16), ones2], axis=1)
        pieces = []
        for col in range(ATT_GROUP):
            qc = qa_ref[rows, col * LANES:(col + 1) * LANES]
            zero = jnp.zeros_like(qc)
            pieces += [jnp.where(low, qc, zero), jnp.where(low, zero, qc)]
        q8 = jnp.concatenate(pieces, axis=0)
        band_j = band & jnp.logical_or(i > 0, kj >= A) if j == 0 else band
        s_scr[j] = _dot_nt(q8, kk).reshape(N_STACK, A, 2 * A)
        s = jnp.where(band_j[None], s_scr[j], -jnp.inf)
        m = jnp.maximum(jnp.max(s, axis=-1, keepdims=True), sink)
        e_scr[j] = jnp.exp(s - m).reshape(N_STACK * A, 2 * A).astype(BF16)
        ox = _dot(e_scr[j], vvx).reshape(N_STACK, A, 2 * LANES)
        o = ox[:, :, :LANES] / (ox[:, :, LANES:LANES + 1] + jnp.exp(sink - m))
        for col in range(ATT_GROUP):
            mix_scr[rows, col * LANES:(col + 1) * LANES] = jnp.where(
                low, o[2 * col], o[2 * col + 1]).astype(BF16)

    r = lax.broadcasted_iota(jnp.int32, (L, L), 0)
    c = lax.broadcasted_iota(jnp.int32, (L, L), 1)
    causal = c <= r
    lane_l = lax.broadcasted_iota(jnp.int32, (L, LANES), 1)
    low_l = lane_l < ML_QK_DIM
    ones_l = _ones_col(L)
    gates = gt_ref[...]
    cum_row, cum_col, _ = _gate_forms(gates, causal, False)
    for p in range(n_pairs):
        qc = qm_ref[:, p * LANES:(p + 1) * LANES]
        k_pair = km_ref[:, p * LANES:(p + 1) * LANES]
        k_t = k_pair.astype(F32).T
        zero = jnp.zeros_like(qc)
        state = state_scr[p]
        state_bf = state.astype(BF16)
        new_state = []
        for e in range(2):
            h = 2 * p + e
            q_pad = jnp.where(low_l, qc, zero) if e == 0 else jnp.where(low_l, zero, qc)
            v_ext = jnp.concatenate([vm_ref[:, h * ML_V_DIM:(h + 1) * ML_V_DIM], ones_l], axis=1)
            b_c = cum_col[:, ML_HEADS + h:ML_HEADS + h + 1]
            b_r = cum_row[ML_HEADS + h:ML_HEADS + h + 1, :]
            ig_r = gates[h:h + 1, :]
            m_prev = m_scr[h:h + 1, 0:1]
            pv, m_row, w_inter = _mlstm_intra(q_pad, k_pair, v_ext, causal, b_c, b_r, ig_r, m_prev)
            qs = _dot(q_pad, state_bf)
            mix_scr[:, ATT_Q_W + h * ML_V_DIM:ATT_Q_W + (h + 1) * ML_V_DIM] = _mlstm_out(
                pv, m_row, w_inter, qs[:, :ML_V_DIM], qs[:, ML_V_DIM:ML_V_DIM + 1],
                mlnw_ref[:, h * ML_V_DIM:(h + 1) * ML_V_DIM],
                om_ref[:, h * ML_V_DIM:(h + 1) * ML_V_DIM])
            b_last = b_r[:, L - 1:L]
            a_r = b_last - b_r + ig_r
            m_new = jnp.maximum(b_last + m_prev, jnp.max(a_r, axis=-1, keepdims=True))
            sc = jnp.exp(b_last + m_prev - m_new)
            kw = (k_t * jnp.exp(a_r - m_new)).astype(BF16)
            new_state.append(sc * state + _dot(kw, v_ext))
            m_scr[h:h + 1, :] = jnp.broadcast_to(m_new, (1, LANES))
        first = lax.broadcasted_iota(jnp.int32, state.shape, 0) < ML_QK_DIM
        state_scr[p] = jnp.where(first, new_state[0], new_state[1])

    x1_ref[...] = x_ref[...] + _dot(mix_scr[...], wperm_scr[...])

    @pl.when(i == pl.num_programs(1) - 1)
    def _():
        for h in range(ML_HEADS):
            p, e = divmod(h, 2)
            ct_ref[0, h] = state_scr[p, e * ML_QK_DIM:(e + 1) * ML_QK_DIM, :ML_V_DIM]
            m_ref[0, :, h:h + 1] = m_scr[h:h + 1, 0:1]
        for p in range(n_pairs):
            nrow_ref[0, p:p + 1, :] = state_scr[p, :, ML_V_DIM:].T[0:1, :]
        kt_ref[0] = kvc_ref[L - WINDOW:, :ATT_KV_W].T
        vt_ref[0] = kvc_ref[L - WINDOW:, ATT_KV_W:].T


def _prompt_mixer(batch, seq, sinks, qa, kv, qm, km, vm, om, gt, x2d, wout, mlnw):
    tq = MIX_TILE
    nt = seq // tq
    sub = tq // WINDOW
    row = lambda w: pl.BlockSpec((tq, w), lambda b, i: (b * nt + i, 0))
    full = lambda a: pl.BlockSpec(a.shape, lambda b, i: (0,) * a.ndim)
    once = lambda a: pl.BlockSpec(a.shape, lambda b, i: (0,) * a.ndim,
                                  pipeline_mode=pl.Buffered(1))
    prev = pl.BlockSpec((WINDOW, 2 * ATT_KV_W),
                        lambda b, i: (jnp.maximum((b * nt + i) * sub - 1, 0), 0))
    per_batch = lambda *dims: pl.BlockSpec((1,) + dims, lambda b, i: (b,) + (0,) * len(dims))
    return pl.pallas_call(
        _prompt_mixer_kernel,
        grid=(batch, nt),
        in_specs=[pl.BlockSpec(memory_space=pltpu.SMEM),
                  row(ATT_Q_W), row(2 * ATT_KV_W), prev, row(ML_QK_W), row(ML_QK_W),
                  row(ML_V_W), row(ML_V_W),
                  pl.BlockSpec((N_GATES, tq), lambda b, i: (0, b * nt + i)),
                  row(D_MODEL), once(wout), full(mlnw)],
        out_specs=[row(D_MODEL),
                   per_batch(ML_HEADS, ML_QK_DIM, ML_V_DIM),
                   per_batch(ML_HEADS // 2, LANES),
                   per_batch(1, ML_HEADS),
                   per_batch(ATT_KV_W, WINDOW),
                   per_batch(ATT_KV_W, WINDOW)],
        out_shape=[jax.ShapeDtypeStruct((batch * seq, D_MODEL), F32),
                   jax.ShapeDtypeStruct((batch, ML_HEADS, ML_QK_DIM, ML_V_DIM), F32),
                   jax.ShapeDtypeStruct((batch, ML_HEADS // 2, LANES), F32),
                   jax.ShapeDtypeStruct((batch, 1, ML_HEADS), F32),
                   jax.ShapeDtypeStruct((batch, ATT_KV_W, WINDOW), F32),
                   jax.ShapeDtypeStruct((batch, ATT_KV_W, WINDOW), F32)],
        scratch_shapes=[pltpu.VMEM((tq, D_MODEL), BF16),
                        pltpu.VMEM((D_MODEL, D_MODEL), BF16),
                        pltpu.VMEM((ML_HEADS // 2, LANES, 2 * LANES), F32),
                        pltpu.VMEM((SUBLANES, LANES), F32),
                        pltpu.VMEM((sub, N_STACK, WINDOW, 2 * WINDOW), F32),
                        pltpu.VMEM((sub, N_STACK * WINDOW, 2 * WINDOW), BF16)],
        compiler_params=pltpu.CompilerParams(dimension_semantics=("arbitrary", "arbitrary"),
                                             vmem_limit_bytes=VMEM_LIMIT),
        name="prompt_mixer",
    )(sinks, qa, kv, kv, qm, km, vm, om, gt, x2d, wout, mlnw)


def _prompt_mixer_t_kernel(sinks_ref, qa_ref, ksc_ref, ksp_ref, kvc_ref, kvp_ref, qm_ref, km_ref,
                           vm_ref, om_ref, gt_ref, x_ref, wout_ref, mlnw_ref,
                           x1_ref, ct_ref, nrow_ref, m_ref, kt_ref, vt_ref,
                           mix_scr, state_scr, m_scr):
    i = pl.program_id(1)
    A = WINDOW
    L = MIX_TILE
    n_pairs = ML_HEADS // 2
    NQ = ATT_HEADS * A

    @pl.when(i == 0)
    def _():
        state_scr[...] = jnp.zeros(state_scr.shape, F32)
        m_scr[...] = jnp.zeros(m_scr.shape, F32)

    kj = lax.broadcasted_iota(jnp.int32, (2 * A, NQ), 0)
    lane_q = lax.broadcasted_iota(jnp.int32, (2 * A, NQ), 1)
    qi = lane_q % A
    band = (kj > qi) & (kj <= qi + WINDOW)
    head_of_lane = lax.broadcasted_iota(jnp.int32, (1, NQ), 1) // A
    sink = jnp.zeros((1, NQ), F32)
    for h in range(ATT_HEADS):
        sink = jnp.where(head_of_lane == h, sinks_ref[h], sink)
    k_all = jnp.concatenate([ksp_ref[...], ksc_ref[...]], axis=0)
    v_all = jnp.concatenate([kvp_ref[ATT_KV_W:, :], kvc_ref[ATT_KV_W:, :]], axis=1).astype(BF16)
    zero_q = jnp.zeros((ATT_HEAD_DIM, A), BF16)
    for j in range(L // A):
        cols = slice(j * A, (j + 1) * A)
        kk = k_all[j * A:(j + 2) * A, :]
        vt = v_all[:, j * A:(j + 2) * A]
        pieces = []
        for h in range(ATT_HEADS):
            q_h = qa_ref[h * ATT_HEAD_DIM:(h + 1) * ATT_HEAD_DIM, cols]
            pieces.append(jnp.concatenate([q_h, zero_q] if h < ATT_GROUP else [zero_q, q_h],
                                          axis=0))
        q_t = jnp.concatenate(pieces, axis=1)
        band_j = band & jnp.logical_or(i > 0, kj >= A) if j == 0 else band
        s = jnp.where(band_j, _dot(kk, q_t), -jnp.inf)
        m = jnp.maximum(jnp.max(s, axis=0, keepdims=True), sink)
        e = jnp.exp(s - m)
        denom = jnp.sum(e, axis=0, keepdims=True) + jnp.exp(sink - m)
        o = _dot(vt, e.astype(BF16)) / denom
        for h in range(ATT_HEADS):
            g = h // ATT_GROUP
            mix_scr[h * ATT_HEAD_DIM:(h + 1) * ATT_HEAD_DIM, cols] = (
                o[g * ATT_HEAD_DIM:(g + 1) * ATT_HEAD_DIM, h * A:(h + 1) * A].astype(BF16))

    r = lax.broadcasted_iota(jnp.int32, (L, L), 0)
    c = lax.broadcasted_iota(jnp.int32, (L, L), 1)
    causal = r <= c
    u_bf = causal.astype(F32).astype(BF16)
    gates = gt_ref[...]
    cum_row = jnp.zeros(gates.shape, F32)
    for part in _split3(gates):
        cum_row = cum_row + _dot(part, u_bf)
    ig_rows = gates[:ML_HEADS]
    b_rows = cum_row[ML_HEADS:]
    key_parts = jnp.concatenate(list(_split3(ig_rows - b_rows))
                                + [jnp.zeros((ML_HEADS, L), BF16)], axis=0)
    part_row = lax.broadcasted_iota(jnp.int32, (4 * ML_HEADS, L), 0)
    row128 = lax.broadcasted_iota(jnp.int32, (LANES, L), 0)
    ones_rows = (row128 == 0).astype(F32).astype(BF16)
    for p in range(n_pairs):
        q_c = qm_ref[p * LANES:(p + 1) * LANES, :]
        k_pair = km_ref[p * LANES:(p + 1) * LANES, :]
        zero = jnp.zeros_like(q_c)
        state = state_scr[p]
        state_bf = state.astype(BF16)
        new_state = []
        for e_id in range(2):
            h = 2 * p + e_id
            head_rows = (row128 < ML_QK_DIM) if e_id == 0 else (row128 >= ML_QK_DIM)
            q_pad = jnp.where(head_rows, q_c, zero)
            b_r = b_rows[h:h + 1, :]
            ig_r = ig_rows[h:h + 1, :]
            sel = ((part_row % ML_HEADS == h) & (part_row < 3 * ML_HEADS)).astype(F32).astype(BF16)
            key_bc = lax.dot_general(key_parts, sel, (((0,), (0,)), ((), ())),
                                     preferred_element_type=F32)
            m_prev = m_scr[h:h + 1, 0:1]
            dm = jnp.where(causal, b_r + key_bc, -jnp.inf)
            inter = b_r + m_prev
            m_row = jnp.maximum(inter, jnp.max(dm, axis=0, keepdims=True))
            w_inter = jnp.exp(inter - m_row)
            qk = lax.dot_general(k_pair, q_pad, (((0,), (0,)), ((), ())),
                                 preferred_element_type=F32)
            p_t = (qk * jnp.exp(dm - m_row)).astype(BF16)
            v_ext = jnp.concatenate([vm_ref[h * ML_V_DIM:(h + 1) * ML_V_DIM, :], ones_rows], axis=0)
            num = _dot(v_ext, p_t) + w_inter * _dot(state_bf, q_pad)
            den = num[ML_V_DIM:ML_V_DIM + 1, :]
            hh = num[:ML_V_DIM] / jnp.maximum(jnp.abs(den), jnp.exp(-m_row))
            ms = jnp.mean(hh * hh, axis=0, keepdims=True)
            gate = _sigmoid(om_ref[h * ML_V_DIM:(h + 1) * ML_V_DIM, :].astype(F32))
            mix_scr[ATT_Q_W + h * ML_V_DIM:ATT_Q_W + (h + 1) * ML_V_DIM, :] = (
                hh * lax.rsqrt(ms + EPS) * mlnw_ref[h * ML_V_DIM:(h + 1) * ML_V_DIM, :]
                * gate).astype(BF16)
            b_last = b_r[:, L - 1:L]
            a_r = b_last - b_r + ig_r
            m_new = jnp.maximum(b_last + m_prev, jnp.max(a_r, axis=-1, keepdims=True))
            sc = jnp.exp(b_last + m_prev - m_new)
            wsv = (v_ext.astype(F32) * jnp.exp(a_r - m_new)).astype(BF16)
            new_state.append(sc * state + _dot_nt(wsv, k_pair))
            m_scr[h:h + 1, :] = jnp.broadcast_to(m_new, (1, LANES))
        first = lax.broadcasted_iota(jnp.int32, state.shape, 1) < ML_QK_DIM
        state_scr[p] = jnp.where(first, new_state[0], new_state[1])

    x1_ref[...] = x_ref[...] + lax.dot_general(
        mix_scr[...], wout_ref[...], (((0,), (0,)), ((), ())), preferred_element_type=F32)

    @pl.when(i == pl.num_programs(1) - 1)
    def _():
        for p in range(n_pairs):
            c_t = state_scr[p, :ML_V_DIM, :].T
            for e_id in range(2):
                ct_ref[0, 2 * p + e_id] = c_t[e_id * ML_QK_DIM:(e_id + 1) * ML_QK_DIM, :]
            nrow_ref[0, p:p + 1, :] = state_scr[p, ML_V_DIM:ML_V_DIM + 1, :]
        for h in range(ML_HEADS):
            m_ref[0, :, h:h + 1] = m_scr[h:h + 1, 0:1]
        kt_ref[0] = kvc_ref[:ATT_KV_W, L - WINDOW:]
        vt_ref[0] = kvc_ref[ATT_KV_W:, L - WINDOW:]


def _prompt_mixer_t(batch, seq, sinks, qa, ks, kv, qm, km, vm, om, gt, x2d, wout, mlnw_col):
    tq = MIX_TILE
    nt = seq // tq
    sub = tq // WINDOW
    col = lambda w: pl.BlockSpec((w, tq), lambda b, i: (0, b * nt + i))
    full = lambda a: pl.BlockSpec(a.shape, lambda b, i: (0,) * a.ndim)
    once = lambda a: pl.BlockSpec(a.shape, lambda b, i: (0,) * a.ndim,
                                  pipeline_mode=pl.Buffered(1))
    prev_block = lambda b, i: jnp.maximum((b * nt + i) * sub - 1, 0)
    per_batch = lambda *dims: pl.BlockSpec((1,) + dims, lambda b, i: (b,) + (0,) * len(dims))
    return pl.pallas_call(
        _prompt_mixer_t_kernel,
        grid=(batch, nt),
        in_specs=[pl.BlockSpec(memory_space=pltpu.SMEM),
                  col(ATT_Q_W),
                  pl.BlockSpec((tq, ATT_KV_W), lambda b, i: (b * nt + i, 0)),
                  pl.BlockSpec((WINDOW, ATT_KV_W), lambda b, i: (prev_block(b, i), 0)),
                  col(2 * ATT_KV_W),
                  pl.BlockSpec((2 * ATT_KV_W, WINDOW), lambda b, i: (0, prev_block(b, i))),
                  col(ML_QK_W), col(ML_QK_W), col(ML_V_W), col(ML_V_W), col(N_GATES),
                  pl.BlockSpec((tq, D_MODEL), lambda b, i: (b * nt + i, 0)),
                  once(wout), full(mlnw_col)],
        out_specs=[pl.BlockSpec((tq, D_MODEL), lambda b, i: (b * nt + i, 0)),
                   per_batch(ML_HEADS, ML_QK_DIM, ML_V_DIM),
                   per_batch(ML_HEADS // 2, LANES),
                   per_batch(1, ML_HEADS),
                   per_batch(ATT_KV_W, WINDOW),
                   per_batch(ATT_KV_W, WINDOW)],
        out_shape=[jax.ShapeDtypeStruct((batch * seq, D_MODEL), F32),
                   jax.ShapeDtypeStruct((batch, ML_HEADS, ML_QK_DIM, ML_V_DIM), F32),
                   jax.ShapeDtypeStruct((batch, ML_HEADS // 2, LANES), F32),
                   jax.ShapeDtypeStruct((batch, 1, ML_HEADS), F32),
                   jax.ShapeDtypeStruct((batch, ATT_KV_W, WINDOW), F32),
                   jax.ShapeDtypeStruct((batch, ATT_KV_W, WINDOW), F32)],
        scratch_shapes=[pltpu.VMEM((D_MODEL, tq), BF16),
                        pltpu.VMEM((ML_HEADS // 2, 2 * LANES, LANES), F32),
                        pltpu.VMEM((SUBLANES, LANES), F32)],
        compiler_params=pltpu.CompilerParams(dimension_semantics=("arbitrary", "arbitrary"),
                                             vmem_limit_bytes=VMEM_LIMIT),
        name="prompt_mixer_t",
    )(sinks, qa, ks, ks, kv, kv, qm, km, vm, om, gt, x2d, wout, mlnw_col)


def _sample_mixer_kernel(t_len, sinks_ref, qa_ref, kv_ref, ck_ref, cv_ref, qm_ref, km_ref,
                         vm_ref, om_ref, gt_ref, c0_ref, n0_ref, m0_ref, x_ref, wout_ref,
                         mlnw_ref, x1_ref, nk_ref, nv_ref, c_ref, n_ref, m_ref,
                         mix_scr, wperm_scr):
    bt = SAMPLE_BT
    T = t_len
    L = bt * T

    @pl.when(pl.program_id(0) == 0)
    def _():
        _permute_head_rows(wperm_scr, wout_ref)
        wperm_scr[ATT_Q_W:, :] = wout_ref[ATT_Q_W:, :]

    lane3 = lax.broadcasted_iota(jnp.int32, (bt, T, LANES), 2)
    low3 = lane3 < ATT_HEAD_DIM
    lane = lax.broadcasted_iota(jnp.int32, (L, LANES), 1)
    low = lane < ATT_HEAD_DIM

    qa3 = qa_ref[...].astype(F32).reshape(bt, T, ATT_Q_W)
    pieces = []
    for col in range(ATT_GROUP):
        qc = qa3[:, :, col * LANES:(col + 1) * LANES]
        pieces += [jnp.where(low3, qc, 0.0), jnp.where(low3, 0.0, qc)]
    q3 = jnp.concatenate(pieces, axis=1).astype(BF16)
    R = bt * N_STACK * T
    q2 = q3.reshape(R, LANES)
    kv_new = kv_ref[...]
    k_new = kv_new[:, :ATT_KV_W]
    v_new = kv_new[:, ATT_KV_W:]
    ck = ck_ref[...]
    cv = cv_ref[...]
    s_c = jnp.einsum('bqd,bdk->bqk', q3, ck.astype(BF16),
                     preferred_element_type=F32).reshape(R, WINDOW)
    s_n = _dot_nt(q2, k_new.astype(BF16))
    row_c = lax.broadcasted_iota(jnp.int32, (R, WINDOW), 0)
    col_c = lax.broadcasted_iota(jnp.int32, (R, WINDOW), 1)
    s_c = jnp.where(col_c > row_c % T, s_c, -jnp.inf)
    row_n = lax.broadcasted_iota(jnp.int32, (R, L), 0)
    col_n = lax.broadcasted_iota(jnp.int32, (R, L), 1)
    valid_n = (row_n // (N_STACK * T) == col_n // T) & (col_n % T <= row_n % T)
    s_n = jnp.where(valid_n, s_n, -jnp.inf)
    stack_id = (lax.broadcasted_iota(jnp.int32, (R, 1), 0) // T) % N_STACK
    sink = jnp.zeros((R, 1), F32)
    for k_id in range(N_STACK):
        sink = jnp.where(stack_id == k_id, sinks_ref[ATT_HEAD_ORDER[k_id]], sink)
    m = jnp.maximum(jnp.maximum(jnp.max(s_c, axis=-1, keepdims=True),
                                jnp.max(s_n, axis=-1, keepdims=True)), sink)
    e_c = jnp.exp(s_c - m)
    e_n = jnp.exp(s_n - m)
    denom = (jnp.sum(e_c, axis=-1, keepdims=True) + jnp.sum(e_n, axis=-1, keepdims=True)
             + jnp.exp(sink - m))
    o = jnp.einsum('bqk,bdk->bqd', e_c.astype(BF16).reshape(bt, N_STACK * T, WINDOW),
                   cv.astype(BF16), preferred_element_type=F32).reshape(R, LANES)
    o = (o + _dot(e_n.astype(BF16), v_new.astype(BF16))) / denom
    o3 = o.reshape(bt, N_STACK * T, LANES)
    for col in range(ATT_GROUP):
        lo_h = o3[:, (2 * col) * T:(2 * col + 1) * T, :]
        hi_h = o3[:, (2 * col + 1) * T:(2 * col + 2) * T, :]
        mix_scr[:, col * LANES:(col + 1) * LANES] = jnp.where(
            low3, lo_h, hi_h).reshape(L, LANES).astype(BF16)

    keep = lax.broadcasted_iota(jnp.int32, (ATT_KV_W, WINDOW), 1) < WINDOW - T
    k_new_t = k_new.T
    v_new_t = v_new.T
    for q in range(bt):
        shift = (WINDOW - T - q * T) % WINDOW
        nk_ref[q] = jnp.where(keep, pltpu.roll(ck[q], WINDOW - T, axis=1),
                              pltpu.roll(k_new_t, shift, axis=1))
        nv_ref[q] = jnp.where(keep, pltpu.roll(cv[q], WINDOW - T, axis=1),
                              pltpu.roll(v_new_t, shift, axis=1))

    r = lax.broadcasted_iota(jnp.int32, (L, L), 0)
    c = lax.broadcasted_iota(jnp.int32, (L, L), 1)
    seg = (r // T == c // T) & (c <= r)
    gates = gt_ref[...]
    cum_row, cum_col, raw_col = _gate_forms(gates, seg, True)
    ones_col = _ones_col(L)
    qm = qm_ref[...]
    km = km_ref[...]
    qm_f = qm.astype(F32)
    km_f = km.astype(F32)
    for h in range(ML_HEADS):
        p, e = divmod(h, 2)
        qc = qm[:, p * LANES:(p + 1) * LANES]
        k_pair = km[:, p * LANES:(p + 1) * LANES]
        zero = jnp.zeros_like(qc)
        q_pad = jnp.where(low, qc, zero) if e == 0 else jnp.where(low, zero, qc)
        v_h = vm_ref[:, h * ML_V_DIM:(h + 1) * ML_V_DIM]
        v_ext = jnp.concatenate([v_h, ones_col], axis=1)
        b_c = cum_col[:, ML_HEADS + h:ML_HEADS + h + 1]
        b_r = cum_row[ML_HEADS + h:ML_HEADS + h + 1, :]
        ig_r = gates[h:h + 1, :]
        ig_c = raw_col[:, h:h + 1]
        m0 = m0_ref[:, :, h:h + 1]
        m_prev_c = jnp.broadcast_to(m0, (bt, T, 1)).reshape(L, 1)
        pv, m_row, w_inter = _mlstm_intra(q_pad, k_pair, v_ext, seg, b_c, b_r, ig_r, m_prev_c)
        q_h3 = qm_f[:, h * ML_QK_DIM:(h + 1) * ML_QK_DIM].reshape(bt, T, ML_QK_DIM)
        k_h3 = km_f[:, h * ML_QK_DIM:(h + 1) * ML_QK_DIM].reshape(bt, T, ML_QK_DIM)
        c0 = c0_ref[:, h]
        n0 = n0_ref[:, h:h + 1, :]
        q_c = jnp.einsum('btd,bdv->btv', q_h3.astype(BF16), c0.astype(BF16),
                         preferred_element_type=F32).reshape(L, ML_V_DIM)
        q_n = jnp.sum(q_h3 * n0, axis=-1, keepdims=True).reshape(L, 1)
        mix_scr[:, ATT_Q_W + h * ML_V_DIM:ATT_Q_W + (h + 1) * ML_V_DIM] = _mlstm_out(
            pv, m_row, w_inter, q_c, q_n, mlnw_ref[:, h * ML_V_DIM:(h + 1) * ML_V_DIM],
            om_ref[:, h * ML_V_DIM:(h + 1) * ML_V_DIM])
        b3 = b_c.reshape(bt, T, 1)
        b_last = b3[:, T - 1:T, :]
        a3 = b_last - b3 + ig_c.reshape(bt, T, 1)
        m_new = jnp.maximum(b_last + m0, jnp.max(a3, axis=1, keepdims=True))
        sc = jnp.exp(b_last + m0 - m_new)
        ws = jnp.exp(a3 - m_new)
        kw = (ws * k_h3).astype(BF16)
        v3 = v_h.astype(F32).reshape(bt, T, ML_V_DIM).astype(BF16)
        d_c = jnp.einsum('bsd,bsv->bdv', kw, v3, preferred_element_type=F32)
        c_ref[:, h] = sc * c0 + d_c
        n_ref[:, h:h + 1, :] = sc * n0 + jnp.sum(ws * k_h3, axis=1, keepdims=True)
        m_ref[:, :, h:h + 1] = m_new

    x1_ref[...] = x_ref[...] + _dot(mix_scr[...], wperm_scr[...])


def _sample_mixer(nb, t_len, sinks, qa, kv, ck, cv, qm, km, vm, om, gt, c0, n0, m0, x2d, wout, mlnw):
    bt = SAMPLE_BT
    tl = bt * t_len
    row = lambda w: pl.BlockSpec((tl, w), lambda i: (i, 0))
    full = lambda a: pl.BlockSpec(a.shape, lambda i: (0,) * a.ndim)
    once = lambda a: pl.BlockSpec(a.shape, lambda i: (0,) * a.ndim, pipeline_mode=pl.Buffered(1))
    cache = pl.BlockSpec((bt, ATT_KV_W, WINDOW), lambda i: (i, 0, 0))
    c_spec = pl.BlockSpec((bt, ML_HEADS, ML_QK_DIM, ML_V_DIM), lambda i: (i, 0, 0, 0))
    n_spec = pl.BlockSpec((bt, ML_HEADS, ML_QK_DIM), lambda i: (i, 0, 0))
    m_spec = pl.BlockSpec((bt, 1, ML_HEADS), lambda i: (i, 0, 0))
    return pl.pallas_call(
        functools.partial(_sample_mixer_kernel, t_len),
        grid=(nb // bt,),
        in_specs=[pl.BlockSpec(memory_space=pltpu.SMEM),
                  row(ATT_Q_W), row(2 * ATT_KV_W), cache, cache, row(ML_QK_W), row(ML_QK_W),
                  row(ML_V_W), row(ML_V_W), pl.BlockSpec((N_GATES, tl), lambda i: (0, i)),
                  c_spec, n_spec, m_spec, row(D_MODEL), once(wout), full(mlnw)],
        out_specs=[row(D_MODEL), cache, cache, c_spec, n_spec, m_spec],
        out_shape=[jax.ShapeDtypeStruct((nb * t_len, D_MODEL), F32),
                   jax.ShapeDtypeStruct((nb, ATT_KV_W, WINDOW), F32),
                   jax.ShapeDtypeStruct((nb, ATT_KV_W, WINDOW), F32),
                   jax.ShapeDtypeStruct((nb, ML_HEADS, ML_QK_DIM, ML_V_DIM), F32),
                   jax.ShapeDtypeStruct((nb, ML_HEADS, ML_QK_DIM), F32),
                   jax.ShapeDtypeStruct((nb, 1, ML_HEADS), F32)],
        scratch_shapes=[pltpu.VMEM((tl, D_MODEL), BF16),
                        pltpu.VMEM((D_MODEL, D_MODEL), BF16)],
        compiler_params=pltpu.CompilerParams(dimension_semantics=("arbitrary",),
                                             vmem_limit_bytes=VMEM_LIMIT),
        name="sample_mixer",
    )(sinks, qa, kv, ck, cv, qm, km, vm, om, gt, c0, n0, m0, x2d, wout, mlnw)


def _ffn_kernel(seq_rows, *refs):
    if seq_rows is None:
        (x_ref, nw_ref, w_ref, cw_ref, cb_ref, wd_ref, y_ref, conv_ref,
         gbuf, act_scr, carry) = refs
        hist_ref = None
    else:
        (x_ref, hist_ref, nw_ref, w_ref, cw_ref, cb_ref, wd_ref, y_ref, conv_ref,
         gbuf, act_scr) = refs
        carry = None
    tm = x_ref.shape[0]
    tf = FF_CHUNK
    n_hist = CONV_W - 1
    rows = tm if seq_rows is None else seq_rows
    nseq = tm // rows
    base = SUBLANES
    n_chunks = D_FF // tf

    if carry is not None:
        @pl.when(pl.program_id(1) == 0)
        def _():
            carry[...] = jnp.zeros(carry.shape, F32)

    x = x_ref[...]
    h2 = _rms(x, nw_ref[...]).astype(BF16)

    def proj(f):
        return (_dot(h2, w_ref[:, f * tf:(f + 1) * tf]),
                _dot(h2, w_ref[:, D_FF + f * tf:D_FF + (f + 1) * tf]))

    nxt = proj(0)
    for f in range(n_chunks):
        g, u = nxt
        if f + 1 < n_chunks:
            nxt = proj(f + 1)
        cols = slice(f * tf, (f + 1) * tf)
        s = f % 2
        g3 = g.reshape(nseq, rows, tf)
        if seq_rows is None:
            gbuf[s, :, base - n_hist:base, :] = carry[:, SUBLANES - n_hist:, cols]
            carry[:, SUBLANES - n_hist:, cols] = g3[:, rows - n_hist:, :]
        else:
            gbuf[s, :, base - n_hist:base, :] = hist_ref[:, :, cols]
            conv_ref[:, :, cols] = g3[:, rows - n_hist:, :]
        gbuf[s, :, base:base + rows, :] = g3
        gc = cb_ref[:, cols] + g * cw_ref[CONV_W - 1:CONV_W, cols]
        for d in range(1, CONV_W):
            gm = gbuf[s, :, base - d:base - d + rows, :].reshape(tm, tf)
            gc = gc + gm * cw_ref[CONV_W - 1 - d:CONV_W - d, cols]
        act_scr[:, cols] = (gc * _sigmoid(gc) * u).astype(BF16)
    y_ref[...] = x + _dot(act_scr[...], wd_ref[...])

    if carry is not None:
        @pl.when(pl.program_id(1) == pl.num_programs(1) - 1)
        def _():
            conv_ref[...] = carry[:, SUBLANES - n_hist:, :]


def _ffn_scratch(tm, rows):
    return [pltpu.VMEM((2, tm // rows, SUBLANES + rows, FF_CHUNK), F32),
            pltpu.VMEM((tm, D_FF), BF16)]


def _ffn_prompt(batch, seq, x2d, nw, w, cw, cb, wd):
    tm = ROW_TILE
    nt = seq // tm
    full = lambda a: pl.BlockSpec(a.shape, lambda b, i: (0,) * a.ndim)
    once = lambda a: pl.BlockSpec(a.shape, lambda b, i: (0,) * a.ndim,
                                  pipeline_mode=pl.Buffered(1))
    row = pl.BlockSpec((tm, D_MODEL), lambda b, i: (b * nt + i, 0))
    return pl.pallas_call(
        functools.partial(_ffn_kernel, None),
        grid=(batch, nt),
        in_specs=[row, full(nw), once(w), full(cw), full(cb), once(wd)],
        out_specs=[row, pl.BlockSpec((1, CONV_W - 1, D_FF), lambda b, i: (b, 0, 0))],
        out_shape=[jax.ShapeDtypeStruct((batch * seq, D_MODEL), F32),
                   jax.ShapeDtypeStruct((batch, CONV_W - 1, D_FF), F32)],
        scratch_shapes=_ffn_scratch(tm, tm) + [pltpu.VMEM((1, SUBLANES, D_FF), F32)],
        compiler_params=pltpu.CompilerParams(dimension_semantics=("arbitrary", "arbitrary"),
                                             vmem_limit_bytes=VMEM_LIMIT),
        name="ffn_prompt",
    )(x2d, nw, w, cw, cb, wd)


def _ffn_sample(nb, t_len, x2d, hist, nw, w, cw, cb, wd):
    tm = ROW_TILE
    bt = tm // t_len
    full = lambda a: pl.BlockSpec(a.shape, lambda i: (0,) * a.ndim)
    once = lambda a: pl.BlockSpec(a.shape, lambda i: (0,) * a.ndim, pipeline_mode=pl.Buffered(1))
    row = pl.BlockSpec((tm, D_MODEL), lambda i: (i, 0))
    hist_spec = pl.BlockSpec((bt, CONV_W - 1, D_FF), lambda i: (i, 0, 0))
    return pl.pallas_call(
        functools.partial(_ffn_kernel, t_len),
        grid=(nb // bt,),
        in_specs=[row, hist_spec, full(nw), once(w), full(cw), full(cb), once(wd)],
        out_specs=[row, hist_spec],
        out_shape=[jax.ShapeDtypeStruct((nb * t_len, D_MODEL), F32),
                   jax.ShapeDtypeStruct((nb, CONV_W - 1, D_FF), F32)],
        scratch_shapes=_ffn_scratch(tm, t_len),
        compiler_params=pltpu.CompilerParams(dimension_semantics=("arbitrary",),
                                             vmem_limit_bytes=VMEM_LIMIT),
        name="ffn_sample",
    )(x2d, hist, nw, w, cw, cb, wd)


def _head_mean_matrix(width, head_dim):
    idx = np.arange(width) // head_dim
    return jnp.asarray((idx[:, None] == idx[None, :]).astype(np.float32) / head_dim, dtype=BF16)


def _layer_weights(norm_mix_w, w_in, b_gates, q_norm_w, k_norm_w, sinks, ml_norm_w, w_out,
                   norm_ffn_w, w_ffn_in, conv_w, conv_b, w_down):
    w_in_t = jnp.pad(w_in.T.astype(BF16), ((0, IN_WIDTH_PAD - w_in.shape[1]), (0, 0)))
    return dict(
        nw=norm_mix_w.reshape(1, D_MODEL),
        w_in_t=w_in_t,
        bg=jnp.pad(b_gates, (0, LANES - N_GATES)).reshape(1, LANES),
        qnw=(jnp.tile(q_norm_w, ATT_HEADS) * ATT_SCALE).reshape(1, ATT_Q_W),
        knw=jnp.tile(k_norm_w, ATT_KV_HEADS).reshape(1, ATT_KV_W),
        gq=_head_mean_matrix(ATT_Q_W, ATT_HEAD_DIM),
        gk=_head_mean_matrix(ATT_KV_W, ATT_HEAD_DIM),
        bg_col=b_gates.reshape(N_GATES, 1),
        qnw_col=(jnp.tile(q_norm_w, ATT_HEADS) * ATT_SCALE).reshape(ATT_Q_W, 1),
        knw_col=jnp.tile(k_norm_w, ATT_KV_HEADS).reshape(ATT_KV_W, 1),
        mlnw_col=ml_norm_w.reshape(ML_V_W, 1),
        sinks=sinks,
        mlnw=ml_norm_w.reshape(1, ML_V_W),
        wout=w_out.astype(BF16),
        nfw=norm_ffn_w.reshape(1, D_MODEL),
        wff=w_ffn_in.astype(BF16),
        cw=conv_w,
        cb=conv_b.reshape(1, D_FF),
        wd=w_down.astype(BF16),
    )


def _cache_from_t(a_t):
    n = a_t.shape[0]
    return jnp.transpose(a_t.reshape(n, ATT_KV_HEADS, ATT_HEAD_DIM, WINDOW), (0, 3, 1, 2))


def _cache_to_t(a):
    n = a.shape[0]
    return jnp.transpose(a, (0, 2, 3, 1)).reshape(n, ATT_KV_W, WINDOW)


def _prompt_layer(x, w):
    batch, seq, _ = x.shape
    assert seq % ROW_TILE == 0 and seq % MIX_TILE == 0 and MIX_TILE % WINDOW == 0
    x2d = x.reshape(batch * seq, D_MODEL)
    qa, ks, kv, qm, km, vm, om, gt = _inproj_t(x2d, w["nw"], w["w_in_t"], w["bg_col"],
                                               w["qnw_col"], w["knw_col"])
    x1, c_t, n_row, m, k_t, v_t = _prompt_mixer_t(batch, seq, w["sinks"], qa, ks, kv, qm, km, vm,
                                                  om, gt, x2d, w["wout"], w["mlnw_col"])
    y, conv = _ffn_prompt(batch, seq, x1, w["nfw"], w["wff"], w["cw"], w["cb"], w["wd"])
    return (y.reshape(batch, seq, D_MODEL), _cache_from_t(k_t), _cache_from_t(v_t),
            jnp.swapaxes(c_t, -1, -2), n_row.reshape(batch, ML_HEADS, ML_QK_DIM),
            m.reshape(batch, ML_HEADS), conv)


def _sample_layer(x, ck, cv, c0, n0, m0, conv_buf, w):
    nb, t_len, _ = x.shape
    assert t_len == SUBLANES and SAMPLE_BT * t_len == LANES and nb % SAMPLE_BT == 0
    assert (nb * t_len) % ROW_TILE == 0
    x2d = x.reshape(nb * t_len, D_MODEL)
    qa, kv, qm, km, vm, om, gt = _inproj(x2d, w["nw"], w["w_in_t"], w["bg"], w["qnw"], w["knw"],
                                         w["gq"], w["gk"])
    x1, nk_t, nv_t, c_t, n, m = _sample_mixer(
        nb, t_len, w["sinks"], qa, kv, _cache_to_t(ck), _cache_to_t(cv), qm, km, vm, om, gt,
        jnp.swapaxes(c0, -1, -2), n0, m0.reshape(nb, 1, ML_HEADS), x2d, w["wout"], w["mlnw"])
    y, conv = _ffn_sample(nb, t_len, x1, conv_buf, w["nfw"], w["wff"], w["cw"], w["cb"],
                          w["wd"])
    return (y.reshape(nb, t_len, D_MODEL), _cache_from_t(nk_t), _cache_from_t(nv_t),
            jnp.swapaxes(c_t, -1, -2), n, m.reshape(nb, ML_HEADS), conv)


def kernel(x_prompt, x_sample, cache_attn_k, cache_attn_v, state_mlstm_C, state_mlstm_n,
           state_mlstm_m, cache_ffn_conv, norm_mix_w, w_in, b_gates, q_norm_w, k_norm_w,
           sinks, ml_norm_w, w_out, norm_ffn_w, w_ffn_in, conv_w, conv_b, w_down):
    depth = w_in.shape[0]
    yp, ys = x_prompt, x_sample
    sp = [[] for _ in range(6)]
    ss = [[] for _ in range(6)]
    for l in range(depth):
        w = _layer_weights(norm_mix_w[l], w_in[l], b_gates[l], q_norm_w[l], k_norm_w[l], sinks[l],
                           ml_norm_w[l], w_out[l], norm_ffn_w[l], w_ffn_in[l], conv_w[l],
                           conv_b[l], w_down[l])
        yp, *st_p = _prompt_layer(yp, w)
        ys, *st_s = _sample_layer(ys, cache_attn_k[l], cache_attn_v[l], state_mlstm_C[l],
                                  state_mlstm_n[l], state_mlstm_m[l], cache_ffn_conv[l], w)
        for i in range(6):
            sp[i].append(st_p[i])
            ss[i].append(st_s[i])
    k_p, v_p, c_p, n_p, m_p, conv_p = [jnp.stack(a) for a in sp]
    k_s, v_s, c_s, n_s, m_s, conv_s = [jnp.stack(a) for a in ss]
    return (yp, ys, k_p, v_p, c_p, n_p, m_p, conv_p, k_s, v_s, c_s, n_s, m_s, conv_s)
```

```python
import functools

import numpy as np
import jax
import jax.numpy as jnp
from jax import lax
from jax.experimental import pallas as pl
from jax.experimental.pallas import tpu as pltpu

F32 = jnp.float32
BF16 = jnp.bfloat16

D_MODEL = 1024
ATT_HEADS = 8
ATT_KV_HEADS = 2
ATT_HEAD_DIM = 64
ATT_GROUP = ATT_HEADS // ATT_KV_HEADS
WINDOW = 128
ML_HEADS = 4
ML_V_DIM = 128
ML_QK_DIM = 64
D_FF = 2816
CONV_W = 3
EPS = 1e-6
ATT_SCALE = ATT_HEAD_DIM ** -0.5
ML_SCALE = ML_QK_DIM ** -0.5
LOG2_E = 1.4426950408889634

ATT_Q_W = ATT_HEADS * ATT_HEAD_DIM
ATT_KV_W = ATT_KV_HEADS * ATT_HEAD_DIM
ML_QK_W = ML_HEADS * ML_QK_DIM
ML_V_W = ML_HEADS * ML_V_DIM
N_GATES = 2 * ML_HEADS
N_STACK = 2 * ATT_GROUP

LANES = 128
SUBLANES = 8

OFF_QA = 0
OFF_KV = OFF_QA + ATT_Q_W
OFF_QM = OFF_KV + 2 * ATT_KV_W
OFF_KM = OFF_QM + ML_QK_W
OFF_VM = OFF_KM + ML_QK_W
OFF_OM = OFF_VM + ML_V_W
OFF_GL = OFF_OM + ML_V_W
IN_WIDTH_PAD = OFF_GL + LANES

ATT_HEAD_ORDER = tuple(h for c in range(ATT_GROUP) for h in (c, c + ATT_GROUP))

ROW_TILE = 512
MIX_TILE = 512
ML_CHUNK = 256
FF_CHUNK = 256
SAMPLE_BT = 16
VMEM_LIMIT = 56 * 1024 * 1024


def _dot(a, b):
    return jnp.dot(a, b, preferred_element_type=F32)


def _dot_nt(a, b):
    return lax.dot_general(a, b, (((1,), (1,)), ((), ())), preferred_element_type=F32)


def _split3(x):
    hi = x.astype(BF16)
    r1 = x - hi.astype(F32)
    mid = r1.astype(BF16)
    lo = (r1 - mid.astype(F32)).astype(BF16)
    return hi, mid, lo


def _rms(x, w):
    ms = jnp.mean(x * x, axis=-1, keepdims=True)
    return x * lax.rsqrt(ms + EPS) * w


def _log_sigmoid(x):
    return jnp.minimum(x, 0.0) - jnp.log1p(jnp.exp(-jnp.abs(x)))


def _sigmoid(x):
    return 1.0 / (1.0 + jnp.exp(-x))


def _permute_head_rows(dst_ref, src_ref):
    for k, h in enumerate(ATT_HEAD_ORDER):
        dst_ref[k * ATT_HEAD_DIM:(k + 1) * ATT_HEAD_DIM, :] = (
            src_ref[h * ATT_HEAD_DIM:(h + 1) * ATT_HEAD_DIM, :])


def _inproj_kernel(x_ref, nw_ref, w_ref, bg_ref, qnw_ref, knw_ref, gq_ref, gk_ref,
                   qa_ref, kv_ref, qm_ref, km_ref, vm_ref, om_ref, gt_ref, wq_scr):
    @pl.when(pl.program_id(0) == 0)
    def _():
        _permute_head_rows(wq_scr, w_ref)

    h = _rms(x_ref[...], nw_ref[...]).astype(BF16)

    def proj(lo, width):
        return _dot_nt(h, w_ref[lo:lo + width, :])

    q = _dot_nt(h, wq_scr[...])
    q_ms = _dot((q * q).astype(BF16), gq_ref[...])
    qa_ref[...] = (q * lax.rsqrt(q_ms + EPS) * qnw_ref[...]).astype(BF16)

    kv = proj(OFF_KV, 2 * ATT_KV_W)
    k = kv[:, :ATT_KV_W]
    k_ms = _dot((k * k).astype(BF16), gk_ref[...])
    kv_ref[:, :ATT_KV_W] = k * lax.rsqrt(k_ms + EPS) * knw_ref[...]
    kv_ref[:, ATT_KV_W:] = kv[:, ATT_KV_W:]

    qm_ref[...] = (proj(OFF_QM, ML_QK_W) * ML_SCALE).astype(BF16)
    km_ref[...] = proj(OFF_KM, ML_QK_W).astype(BF16)
    vm_ref[...] = proj(OFF_VM, ML_V_W).astype(BF16)
    om_ref[...] = proj(OFF_OM, ML_V_W).astype(BF16)

    gl = proj(OFF_GL, LANES) + bg_ref[...]
    lane = lax.broadcasted_iota(jnp.int32, gl.shape, 1)
    g = jnp.where(lane < ML_HEADS, gl, _log_sigmoid(gl))
    gt_ref[...] = g.T[:N_GATES, :]


def _inproj(x2d, nw, w_in_t, bg, qnw, knw, gq, gk):
    n = x2d.shape[0]
    tm = ROW_TILE
    row = lambda w: pl.BlockSpec((tm, w), lambda i: (i, 0))
    full = lambda a: pl.BlockSpec(a.shape, lambda i: (0,) * a.ndim)
    once = lambda a: pl.BlockSpec(a.shape, lambda i: (0,) * a.ndim, pipeline_mode=pl.Buffered(1))
    return pl.pallas_call(
        _inproj_kernel,
        grid=(n // tm,),
        in_specs=[row(D_MODEL), full(nw), once(w_in_t), full(bg), full(qnw), full(knw),
                  full(gq), full(gk)],
        out_specs=[row(ATT_Q_W), row(2 * ATT_KV_W), row(ML_QK_W), row(ML_QK_W),
                   row(ML_V_W), row(ML_V_W), pl.BlockSpec((N_GATES, tm), lambda i: (0, i))],
        out_shape=[jax.ShapeDtypeStruct((n, ATT_Q_W), BF16),
                   jax.ShapeDtypeStruct((n, 2 * ATT_KV_W), F32),
                   jax.ShapeDtypeStruct((n, ML_QK_W), BF16),
                   jax.ShapeDtypeStruct((n, ML_QK_W), BF16),
                   jax.ShapeDtypeStruct((n, ML_V_W), BF16),
                   jax.ShapeDtypeStruct((n, ML_V_W), BF16),
                   jax.ShapeDtypeStruct((N_GATES, n), F32)],
        scratch_shapes=[pltpu.VMEM((ATT_Q_W, D_MODEL), BF16)],
        compiler_params=pltpu.CompilerParams(dimension_semantics=("arbitrary",),
                                             vmem_limit_bytes=VMEM_LIMIT),
        name="inproj",
    )(x2d, nw, w_in_t, bg, qnw, knw, gq, gk)


def _head_norm_t(z, head_dim, w_col):
    rows, tokens = z.shape
    z3 = z.reshape(rows // head_dim, head_dim, tokens)
    ms = jnp.mean(z3 * z3, axis=1, keepdims=True)
    return (z3 * lax.rsqrt(ms + EPS)).reshape(rows, tokens) * w_col


def _inproj_t_kernel(x_ref, nw_ref, w_ref, bg_ref, qnw_ref, knw_ref,
                     qa_ref, ks_ref, kv_ref, qm_ref, km_ref, vm_ref, om_ref, gt_ref):
    h = _rms(x_ref[...], nw_ref[...]).astype(BF16)

    def proj(lo, width):
        return _dot_nt(w_ref[lo:lo + width, :], h)

    qa_ref[...] = _head_norm_t(proj(OFF_QA, ATT_Q_W), ATT_HEAD_DIM, qnw_ref[...]).astype(BF16)
    kv = proj(OFF_KV, 2 * ATT_KV_W)
    k = _head_norm_t(kv[:ATT_KV_W], ATT_HEAD_DIM, knw_ref[...])
    kv_ref[:ATT_KV_W, :] = k
    kv_ref[ATT_KV_W:, :] = kv[ATT_KV_W:]
    ks_ref[...] = k.T.astype(BF16)
    qm_ref[...] = (proj(OFF_QM, ML_QK_W) * ML_SCALE).astype(BF16)
    km_ref[...] = proj(OFF_KM, ML_QK_W).astype(BF16)
    vm_ref[...] = proj(OFF_VM, ML_V_W).astype(BF16)
    om_ref[...] = proj(OFF_OM, ML_V_W).astype(BF16)
    gl = proj(OFF_GL, 2 * SUBLANES)[:N_GATES] + bg_ref[...]
    row = lax.broadcasted_iota(jnp.int32, gl.shape, 0)
    gt_ref[...] = jnp.where(row < ML_HEADS, gl, _log_sigmoid(gl))


def _inproj_t(x2d, nw, w_in_t, bg_col, qnw_col, knw_col):
    n = x2d.shape[0]
    tm = ROW_TILE
    full = lambda a: pl.BlockSpec(a.shape, lambda i: (0,) * a.ndim)
    once = lambda a: pl.BlockSpec(a.shape, lambda i: (0,) * a.ndim, pipeline_mode=pl.Buffered(1))
    col = lambda w: pl.BlockSpec((w, tm), lambda i: (0, i))
    return pl.pallas_call(
        _inproj_t_kernel,
        grid=(n // tm,),
        in_specs=[pl.BlockSpec((tm, D_MODEL), lambda i: (i, 0)), full(nw), once(w_in_t),
                  full(bg_col), full(qnw_col), full(knw_col)],
        out_specs=[col(ATT_Q_W), pl.BlockSpec((tm, ATT_KV_W), lambda i: (i, 0)),
                   col(2 * ATT_KV_W), col(ML_QK_W), col(ML_QK_W), col(ML_V_W), col(ML_V_W),
                   col(N_GATES)],
        out_shape=[jax.ShapeDtypeStruct((ATT_Q_W, n), BF16),
                   jax.ShapeDtypeStruct((n, ATT_KV_W), BF16),
                   jax.ShapeDtypeStruct((2 * ATT_KV_W, n), F32),
                   jax.ShapeDtypeStruct((ML_QK_W, n), BF16),
                   jax.ShapeDtypeStruct((ML_QK_W, n), BF16),
                   jax.ShapeDtypeStruct((ML_V_W, n), BF16),
                   jax.ShapeDtypeStruct((ML_V_W, n), BF16),
                   jax.ShapeDtypeStruct((N_GATES, n), F32)],
        compiler_params=pltpu.CompilerParams(dimension_semantics=("arbitrary",),
                                             vmem_limit_bytes=VMEM_LIMIT),
        name="inproj_t",
    )(x2d, nw, w_in_t, bg_col, qnw_col, knw_col)


def _gate_forms(gates, seg_mask, want_raw_col):
    L = gates.shape[1]
    m_bf = seg_mask.astype(F32).astype(BF16)
    cum_row = jnp.zeros(gates.shape, F32)
    cum_col = jnp.zeros((L, gates.shape[0]), F32)
    raw_col = None
    if want_raw_col:
        r = lax.broadcasted_iota(jnp.int32, (L, L), 0)
        c = lax.broadcasted_iota(jnp.int32, (L, L), 1)
        eye = (r == c).astype(F32).astype(BF16)
        raw_col = jnp.zeros((L, gates.shape[0]), F32)
    for part in _split3(gates):
        cum_row = cum_row + _dot_nt(part, m_bf)
        cum_col = cum_col + _dot_nt(m_bf, part)
        if want_raw_col:
            raw_col = raw_col + _dot_nt(eye, part)
    return cum_row, cum_col, raw_col


def _mlstm_intra(q_pad, k_pair, v_ext, seg_mask, b_c, b_r, ig_r, m_prev_c):
    dm = jnp.where(seg_mask, b_c + (ig_r - b_r), -jnp.inf)
    inter = b_c + m_prev_c
    m_row = jnp.maximum(inter, jnp.max(dm, axis=-1, keepdims=True))
    w_inter = jnp.exp(inter - m_row)
    p = _dot_nt(q_pad, k_pair) * jnp.exp(dm - m_row)
    return _dot(p.astype(BF16), v_ext), m_row, w_inter


def _mlstm_out(pv, m_row, w_inter, q_c, q_n, mlnw_h, om_h):
    num = pv[:, :ML_V_DIM] + w_inter * q_c
    den = pv[:, ML_V_DIM:ML_V_DIM + 1] + w_inter * q_n
    hh = num / jnp.maximum(jnp.abs(den), jnp.exp(-m_row))
    return (_rms(hh, mlnw_h) * _sigmoid(om_h.astype(F32))).astype(BF16)


def _ones_col(rows):
    lane = lax.broadcasted_iota(jnp.int32, (rows, LANES), 1)
    return (lane == 0).astype(F32).astype(BF16)


def _prompt_mixer_t_kernel(sinks_ref, qa_ref, ksc_ref, ksp_ref, kvc_ref, kvp_ref, qm_ref, km_ref,
                           vm_ref, om_ref, gt_ref, x_ref, wout_ref, mlnw_ref,
                           x1_ref, ct_ref, nrow_ref, m_ref, kt_ref, vt_ref,
                           mix_scr, state_scr, m_scr, band_scr, causal_scr, tri_scr,
                           s_scr_a, s_scr_b, e_scr):
    i = pl.program_id(1)
    A = WINDOW
    L = MIX_TILE
    C = ML_CHUNK
    n_pairs = ML_HEADS // 2

    @pl.when(i == 0)
    def _():
        state_scr[...] = jnp.zeros(state_scr.shape, F32)
        m_scr[...] = jnp.zeros(m_scr.shape, F32)
        kj = lax.broadcasted_iota(jnp.int32, (2 * A, A), 0)
        qi = lax.broadcasted_iota(jnp.int32, (2 * A, A), 1)
        band = (kj > qi) & (kj <= qi + WINDOW)
        band_scr[0] = jnp.where(band, 0.0, -jnp.inf)
        band_scr[1] = jnp.where(band & (kj >= A), 0.0, -jnp.inf)
        r = lax.broadcasted_iota(jnp.int32, (C, C), 0)
        c = lax.broadcasted_iota(jnp.int32, (C, C), 1)
        causal_scr[...] = jnp.where(r <= c, 0.0, -jnp.inf)
        tri_scr[...] = (r <= c).astype(F32).astype(BF16)

    k_all = jnp.concatenate([ksp_ref[...], ksc_ref[...]], axis=0)
    v_all = jnp.concatenate([kvp_ref[ATT_KV_W:, :], kvc_ref[ATT_KV_W:, :]], axis=1).astype(BF16)
    zero_q = jnp.zeros((ATT_HEAD_DIM, A), BF16)
    slot = lax.rem(i, 2)
    s_bufs = (s_scr_a, s_scr_b)

    def stage_scores(j):
        pieces = []
        for h in range(ATT_HEADS):
            q_h = qa_ref[h * ATT_HEAD_DIM:(h + 1) * ATT_HEAD_DIM, j * A:(j + 1) * A]
            pieces.append(jnp.concatenate([q_h, zero_q] if h < ATT_GROUP else [zero_q, q_h],
                                          axis=0))
        s_bufs[j % 2][slot] = _dot(k_all[j * A:(j + 2) * A, :], jnp.concatenate(pieces, axis=1))

    stage_scores(0)
    for j in range(L // A):
        cols = slice(j * A, (j + 1) * A)
        vt = v_all[:, j * A:(j + 2) * A]
        if j + 1 < L // A:
            stage_scores(j + 1)
        s_buf = s_bufs[j % 2]
        bias = jnp.where(i > 0, band_scr[0], band_scr[1]) if j == 0 else band_scr[0]
        m_rows = []
        for h in range(ATT_HEADS):
            sb = s_buf[slot, :, h * A:(h + 1) * A] + bias
            m_rows.append(jnp.maximum(jnp.max(sb, axis=0, keepdims=True),
                                      sinks_ref[h] * LOG2_E))
        inv_rows = []
        for h in range(ATT_HEADS):
            e = jnp.exp2(s_buf[slot, :, h * A:(h + 1) * A] + (bias - m_rows[h]))
            e_scr[:, h * A:(h + 1) * A] = e.astype(BF16)
            inv_rows.append(1.0 / (jnp.sum(e, axis=0, keepdims=True)
                                   + jnp.exp2(sinks_ref[h] * LOG2_E - m_rows[h])))
        o = _dot(vt, e_scr[...])
        for h in range(ATT_HEADS):
            g = h // ATT_GROUP
            mix_scr[h * ATT_HEAD_DIM:(h + 1) * ATT_HEAD_DIM, cols] = (
                o[g * ATT_HEAD_DIM:(g + 1) * ATT_HEAD_DIM, h * A:(h + 1) * A]
                * inv_rows[h]).astype(BF16)

    row128 = lax.broadcasted_iota(jnp.int32, (LANES, C), 0)
    ones_rows = (row128 == 0).astype(F32).astype(BF16)
    for ci in range(L // C):
        tok = slice(ci * C, (ci + 1) * C)
        gates = gt_ref[:, tok] * LOG2_E
        cum_row = jnp.zeros(gates.shape, F32)
        for part in _split3(gates):
            cum_row = cum_row + _dot(part, tri_scr[...])
        ig_rows = gates[:ML_HEADS]
        b_rows = cum_row[ML_HEADS:]
        key_cols = jnp.concatenate([ig_rows - b_rows, jnp.zeros((LANES - ML_HEADS, C), F32)],
                                   axis=0).T
        for p in range(n_pairs):
            q_c = qm_ref[p * LANES:(p + 1) * LANES, tok]
            k_pair = km_ref[p * LANES:(p + 1) * LANES, tok]
            zero = jnp.zeros_like(q_c)
            state = state_scr[p]
            state_bf = state.astype(BF16)
            new_state = []
            for e_id in range(2):
                h = 2 * p + e_id
                v_rows = slice(h * ML_V_DIM, (h + 1) * ML_V_DIM)
                head_rows = (row128 < ML_QK_DIM) if e_id == 0 else (row128 >= ML_QK_DIM)
                q_pad = jnp.where(head_rows, q_c, zero)
                b_r = b_rows[h:h + 1, :]
                ig_r = ig_rows[h:h + 1, :]
                m_prev = m_scr[h:h + 1, 0:1]
                dm = (b_r + key_cols[:, h:h + 1]) + causal_scr[...]
                inter = b_r + m_prev
                m_row = jnp.maximum(inter, jnp.max(dm, axis=0, keepdims=True))
                w_inter = jnp.exp2(inter - m_row)
                qk = lax.dot_general(k_pair, q_pad, (((0,), (0,)), ((), ())),
                                     preferred_element_type=F32)
                p_t = (qk * jnp.exp2(dm - m_row)).astype(BF16)
                v_ext = jnp.concatenate([vm_ref[v_rows, tok], ones_rows], axis=0)
                num = _dot(v_ext, p_t) + w_inter * _dot(state_bf, q_pad)
                den = num[ML_V_DIM:ML_V_DIM + 1, :]
                hh = num[:ML_V_DIM] * (1.0 / jnp.maximum(jnp.abs(den), jnp.exp2(-m_row)))
                ms = jnp.mean(hh * hh, axis=0, keepdims=True)
                gate = _sigmoid(om_ref[v_rows, tok].astype(F32))
                mix_scr[ATT_Q_W + h * ML_V_DIM:ATT_Q_W + (h + 1) * ML_V_DIM, tok] = (
                    hh * lax.rsqrt(ms + EPS) * mlnw_ref[v_rows, :] * gate).astype(BF16)
                b_last = b_r[:, C - 1:C]
                a_r = b_last - b_r + ig_r
                m_new = jnp.maximum(b_last + m_prev, jnp.max(a_r, axis=-1, keepdims=True))
                sc = jnp.exp2(b_last + m_prev - m_new)
                wsv = (v_ext.astype(F32) * jnp.exp2(a_r - m_new)).astype(BF16)
                new_state.append(sc * state + _dot_nt(wsv, k_pair))
                m_scr[h:h + 1, :] = jnp.broadcast_to(m_new, (1, LANES))
            first = lax.broadcasted_iota(jnp.int32, state.shape, 1) < ML_QK_DIM
            state_scr[p] = jnp.where(first, new_state[0], new_state[1])

    x1_ref[...] = x_ref[...] + lax.dot_general(
        mix_scr[...], wout_ref[...], (((0,), (0,)), ((), ())), preferred_element_type=F32)

    @pl.when(i == pl.num_programs(1) - 1)
    def _():
        for p in range(n_pairs):
            c_t = state_scr[p, :ML_V_DIM, :].T
            for e_id in range(2):
                ct_ref[0, 2 * p + e_id] = c_t[e_id * ML_QK_DIM:(e_id + 1) * ML_QK_DIM, :]
            nrow_ref[0, p:p + 1, :] = state_scr[p, ML_V_DIM:ML_V_DIM + 1, :]
        for h in range(ML_HEADS):
            m_ref[0, :, h:h + 1] = m_scr[h:h + 1, 0:1] * (1.0 / LOG2_E)
        kt_ref[0] = kvc_ref[:ATT_KV_W, L - WINDOW:]
        vt_ref[0] = kvc_ref[ATT_KV_W:, L - WINDOW:]


def _prompt_mixer_t(batch, seq, sinks, qa, ks, kv, qm, km, vm, om, gt, x2d, wout, mlnw_col):
    tq = MIX_TILE
    nt = seq // tq
    sub = tq // WINDOW
    col = lambda w: pl.BlockSpec((w, tq), lambda b, i: (0, b * nt + i))
    full = lambda a: pl.BlockSpec(a.shape, lambda b, i: (0,) * a.ndim)
    once = lambda a: pl.BlockSpec(a.shape, lambda b, i: (0,) * a.ndim,
                                  pipeline_mode=pl.Buffered(1))
    prev_block = lambda b, i: jnp.maximum((b * nt + i) * sub - 1, 0)
    per_batch = lambda *dims: pl.BlockSpec((1,) + dims, lambda b, i: (b,) + (0,) * len(dims))
    return pl.pallas_call(
        _prompt_mixer_t_kernel,
        grid=(batch, nt),
        in_specs=[pl.BlockSpec(memory_space=pltpu.SMEM),
                  col(ATT_Q_W),
                  pl.BlockSpec((tq, ATT_KV_W), lambda b, i: (b * nt + i, 0)),
                  pl.BlockSpec((WINDOW, ATT_KV_W), lambda b, i: (prev_block(b, i), 0)),
                  col(2 * ATT_KV_W),
                  pl.BlockSpec((2 * ATT_KV_W, WINDOW), lambda b, i: (0, prev_block(b, i))),
                  col(ML_QK_W), col(ML_QK_W), col(ML_V_W), col(ML_V_W), col(N_GATES),
                  pl.BlockSpec((tq, D_MODEL), lambda b, i: (b * nt + i, 0)),
                  once(wout), full(mlnw_col)],
        out_specs=[pl.BlockSpec((tq, D_MODEL), lambda b, i: (b * nt + i, 0)),
                   per_batch(ML_HEADS, ML_QK_DIM, ML_V_DIM),
                   per_batch(ML_HEADS // 2, LANES),
                   per_batch(1, ML_HEADS),
                   per_batch(ATT_KV_W, WINDOW),
                   per_batch(ATT_KV_W, WINDOW)],
        out_shape=[jax.ShapeDtypeStruct((batch * seq, D_MODEL), F32),
                   jax.ShapeDtypeStruct((batch, ML_HEADS, ML_QK_DIM, ML_V_DIM), F32),
                   jax.ShapeDtypeStruct((batch, ML_HEADS // 2, LANES), F32),
                   jax.ShapeDtypeStruct((batch, 1, ML_HEADS), F32),
                   jax.ShapeDtypeStruct((batch, ATT_KV_W, WINDOW), F32),
                   jax.ShapeDtypeStruct((batch, ATT_KV_W, WINDOW), F32)],
        scratch_shapes=[pltpu.VMEM((D_MODEL, tq), BF16),
                        pltpu.VMEM((ML_HEADS // 2, 2 * LANES, LANES), F32),
                        pltpu.VMEM((SUBLANES, LANES), F32),
                        pltpu.VMEM((2, 2 * WINDOW, WINDOW), F32),
                        pltpu.VMEM((ML_CHUNK, ML_CHUNK), F32),
                        pltpu.VMEM((ML_CHUNK, ML_CHUNK), BF16),
                        pltpu.VMEM((2, 2 * WINDOW, ATT_HEADS * WINDOW), F32),
                        pltpu.VMEM((2, 2 * WINDOW, ATT_HEADS * WINDOW), F32),
                        pltpu.VMEM((2 * WINDOW, ATT_HEADS * WINDOW), BF16)],
        compiler_params=pltpu.CompilerParams(dimension_semantics=("arbitrary", "arbitrary"),
                                             vmem_limit_bytes=VMEM_LIMIT),
        name="prompt_mixer_t",
    )(sinks, qa, ks, ks, kv, kv, qm, km, vm, om, gt, x2d, wout, mlnw_col)


def _sample_mixer_kernel(t_len, sinks_ref, qa_ref, kv_ref, ck_ref, cv_ref, qm_ref, km_ref,
                         vm_ref, om_ref, gt_ref, c0_ref, n0_ref, m0_ref, x_ref, wout_ref,
                         mlnw_ref, x1_ref, nk_ref, nv_ref, c_ref, n_ref, m_ref,
                         mix_scr, wperm_scr):
    bt = SAMPLE_BT
    T = t_len
    L = bt * T

    @pl.when(pl.program_id(0) == 0)
    def _():
        _permute_head_rows(wperm_scr, wout_ref)
        wperm_scr[ATT_Q_W:, :] = wout_ref[ATT_Q_W:, :]

    lane3 = lax.broadcasted_iota(jnp.int32, (bt, T, LANES), 2)
    low3 = lane3 < ATT_HEAD_DIM
    lane = lax.broadcasted_iota(jnp.int32, (L, LANES), 1)
    low = lane < ATT_HEAD_DIM

    qa3 = qa_ref[...].astype(F32).reshape(bt, T, ATT_Q_W)
    pieces = []
    for col in range(ATT_GROUP):
        qc = qa3[:, :, col * LANES:(col + 1) * LANES]
        pieces += [jnp.where(low3, qc, 0.0), jnp.where(low3, 0.0, qc)]
    q3 = jnp.concatenate(pieces, axis=1).astype(BF16)
    R = bt * N_STACK * T
    q2 = q3.reshape(R, LANES)
    kv_new = kv_ref[...]
    k_new = kv_new[:, :ATT_KV_W]
    v_new = kv_new[:, ATT_KV_W:]
    ck = ck_ref[...]
    cv = cv_ref[...]
    s_c = jnp.einsum('bqd,bdk->bqk', q3, ck.astype(BF16),
                     preferred_element_type=F32).reshape(R, WINDOW)
    s_n = _dot_nt(q2, k_new.astype(BF16))
    row_c = lax.broadcasted_iota(jnp.int32, (R, WINDOW), 0)
    col_c = lax.broadcasted_iota(jnp.int32, (R, WINDOW), 1)
    s_c = jnp.where(col_c > row_c % T, s_c, -jnp.inf)
    row_n = lax.broadcasted_iota(jnp.int32, (R, L), 0)
    col_n = lax.broadcasted_iota(jnp.int32, (R, L), 1)
    valid_n = (row_n // (N_STACK * T) == col_n // T) & (col_n % T <= row_n % T)
    s_n = jnp.where(valid_n, s_n, -jnp.inf)
    stack_id = (lax.broadcasted_iota(jnp.int32, (R, 1), 0) // T) % N_STACK
    sink = jnp.zeros((R, 1), F32)
    for k_id in range(N_STACK):
        sink = jnp.where(stack_id == k_id, sinks_ref[ATT_HEAD_ORDER[k_id]], sink)
    m = jnp.maximum(jnp.maximum(jnp.max(s_c, axis=-1, keepdims=True),
                                jnp.max(s_n, axis=-1, keepdims=True)), sink)
    e_c = jnp.exp(s_c - m)
    e_n = jnp.exp(s_n - m)
    denom = (jnp.sum(e_c, axis=-1, keepdims=True) + jnp.sum(e_n, axis=-1, keepdims=True)
             + jnp.exp(sink - m))
    o = jnp.einsum('bqk,bdk->bqd', e_c.astype(BF16).reshape(bt, N_STACK * T, WINDOW),
                   cv.astype(BF16), preferred_element_type=F32).reshape(R, LANES)
    o = (o + _dot(e_n.astype(BF16), v_new.astype(BF16))) / denom
    o3 = o.reshape(bt, N_STACK * T, LANES)
    for col in range(ATT_GROUP):
        lo_h = o3[:, (2 * col) * T:(2 * col + 1) * T, :]
        hi_h = o3[:, (2 * col + 1) * T:(2 * col + 2) * T, :]
        mix_scr[:, col * LANES:(col + 1) * LANES] = jnp.where(
            low3, lo_h, hi_h).reshape(L, LANES).astype(BF16)

    keep = lax.broadcasted_iota(jnp.int32, (ATT_KV_W, WINDOW), 1) < WINDOW - T
    k_new_t = k_new.T
    v_new_t = v_new.T
    for q in range(bt):
        shift = (WINDOW - T - q * T) % WINDOW
        nk_ref[q] = jnp.where(keep, pltpu.roll(ck[q], WINDOW - T, axis=1),
                              pltpu.roll(k_new_t, shift, axis=1))
        nv_ref[q] = jnp.where(keep, pltpu.roll(cv[q], WINDOW - T, axis=1),
                              pltpu.roll(v_new_t, shift, axis=1))

    r = lax.broadcasted_iota(jnp.int32, (L, L), 0)
    c = lax.broadcasted_iota(jnp.int32, (L, L), 1)
    seg = (r // T == c // T) & (c <= r)
    gates = gt_ref[...]
    cum_row, cum_col, raw_col = _gate_forms(gates, seg, True)
    ones_col = _ones_col(L)
    qm = qm_ref[...]
    km = km_ref[...]
    qm_f = qm.astype(F32)
    km_f = km.astype(F32)
    for h in range(ML_HEADS):
        p, e = divmod(h, 2)
        qc = qm[:, p * LANES:(p + 1) * LANES]
        k_pair = km[:, p * LANES:(p + 1) * LANES]
        zero = jnp.zeros_like(qc)
        q_pad = jnp.where(low, qc, zero) if e == 0 else jnp.where(low, zero, qc)
        v_h = vm_ref[:, h * ML_V_DIM:(h + 1) * ML_V_DIM]
        v_ext = jnp.concatenate([v_h, ones_col], axis=1)
        b_c = cum_col[:, ML_HEADS + h:ML_HEADS + h + 1]
        b_r = cum_row[ML_HEADS + h:ML_HEADS + h + 1, :]
        ig_r = gates[h:h + 1, :]
        ig_c = raw_col[:, h:h + 1]
        m0 = m0_ref[:, :, h:h + 1]
        m_prev_c = jnp.broadcast_to(m0, (bt, T, 1)).reshape(L, 1)
        pv, m_row, w_inter = _mlstm_intra(q_pad, k_pair, v_ext, seg, b_c, b_r, ig_r, m_prev_c)
        q_h3 = qm_f[:, h * ML_QK_DIM:(h + 1) * ML_QK_DIM].reshape(bt, T, ML_QK_DIM)
        k_h3 = km_f[:, h * ML_QK_DIM:(h + 1) * ML_QK_DIM].reshape(bt, T, ML_QK_DIM)
        c0 = c0_ref[:, h]
        n0 = n0_ref[:, h:h + 1, :]
        q_c = jnp.einsum('btd,bdv->btv', q_h3.astype(BF16), c0.astype(BF16),
                         preferred_element_type=F32).reshape(L, ML_V_DIM)
        q_n = jnp.sum(q_h3 * n0, axis=-1, keepdims=True).reshape(L, 1)
        mix_scr[:, ATT_Q_W + h * ML_V_DIM:ATT_Q_W + (h + 1) * ML_V_DIM] = _mlstm_out(
            pv, m_row, w_inter, q_c, q_n, mlnw_ref[:, h * ML_V_DIM:(h + 1) * ML_V_DIM],
            om_ref[:, h * ML_V_DIM:(h + 1) * ML_V_DIM])
        b3 = b_c.reshape(bt, T, 1)
        b_last = b3[:, T - 1:T, :]
        a3 = b_last - b3 + ig_c.reshape(bt, T, 1)
        m_new = jnp.maximum(b_last + m0, jnp.max(a3, axis=1, keepdims=True))
        sc = jnp.exp(b_last + m0 - m_new)
        ws = jnp.exp(a3 - m_new)
        kw = (ws * k_h3).astype(BF16)
        v3 = v_h.astype(F32).reshape(bt, T, ML_V_DIM).astype(BF16)
        d_c = jnp.einsum('bsd,bsv->bdv', kw, v3, preferred_element_type=F32)
        c_ref[:, h] = sc * c0 + d_c
        n_ref[:, h:h + 1, :] = sc * n0 + jnp.sum(ws * k_h3, axis=1, keepdims=True)
        m_ref[:, :, h:h + 1] = m_new

    x1_ref[...] = x_ref[...] + _dot(mix_scr[...], wperm_scr[...])


def _sample_mixer(nb, t_len, sinks, qa, kv, ck, cv, qm, km, vm, om, gt, c0, n0, m0, x2d, wout, mlnw):
    bt = SAMPLE_BT
    tl = bt * t_len
    row = lambda w: pl.BlockSpec((tl, w), lambda i: (i, 0))
    full = lambda a: pl.BlockSpec(a.shape, lambda i: (0,) * a.ndim)
    once = lambda a: pl.BlockSpec(a.shape, lambda i: (0,) * a.ndim, pipeline_mode=pl.Buffered(1))
    cache = pl.BlockSpec((bt, ATT_KV_W, WINDOW), lambda i: (i, 0, 0))
    c_spec = pl.BlockSpec((bt, ML_HEADS, ML_QK_DIM, ML_V_DIM), lambda i: (i, 0, 0, 0))
    n_spec = pl.BlockSpec((bt, ML_HEADS, ML_QK_DIM), lambda i: (i, 0, 0))
    m_spec = pl.BlockSpec((bt, 1, ML_HEADS), lambda i: (i, 0, 0))
    return pl.pallas_call(
        functools.partial(_sample_mixer_kernel, t_len),
        grid=(nb // bt,),
        in_specs=[pl.BlockSpec(memory_space=pltpu.SMEM),
                  row(ATT_Q_W), row(2 * ATT_KV_W), cache, cache, row(ML_QK_W), row(ML_QK_W),
                  row(ML_V_W), row(ML_V_W), pl.BlockSpec((N_GATES, tl), lambda i: (0, i)),
                  c_spec, n_spec, m_spec, row(D_MODEL), once(wout), full(mlnw)],
        out_specs=[row(D_MODEL), cache, cache, c_spec, n_spec, m_spec],
        out_shape=[jax.ShapeDtypeStruct((nb * t_len, D_MODEL), F32),
                   jax.ShapeDtypeStruct((nb, ATT_KV_W, WINDOW), F32),
                   jax.ShapeDtypeStruct((nb, ATT_KV_W, WINDOW), F32),
                   jax.ShapeDtypeStruct((nb, ML_HEADS, ML_QK_DIM, ML_V_DIM), F32),
                   jax.ShapeDtypeStruct((nb, ML_HEADS, ML_QK_DIM), F32),
                   jax.ShapeDtypeStruct((nb, 1, ML_HEADS), F32)],
        scratch_shapes=[pltpu.VMEM((tl, D_MODEL), BF16),
                        pltpu.VMEM((D_MODEL, D_MODEL), BF16)],
        compiler_params=pltpu.CompilerParams(dimension_semantics=("arbitrary",),
                                             vmem_limit_bytes=VMEM_LIMIT),
        name="sample_mixer",
    )(sinks, qa, kv, ck, cv, qm, km, vm, om, gt, c0, n0, m0, x2d, wout, mlnw)


def _ffn_kernel(seq_rows, *refs):
    if seq_rows is None:
        (x_ref, nw_ref, w_ref, cw_ref, cb_ref, wd_ref, y_ref, conv_ref,
         gbuf, act_scr, carry) = refs
        hist_ref = None
    else:
        (x_ref, hist_ref, nw_ref, w_ref, cw_ref, cb_ref, wd_ref, y_ref, conv_ref,
         gbuf, act_scr) = refs
        carry = None
    tm = x_ref.shape[0]
    tf = FF_CHUNK
    n_hist = CONV_W - 1
    rows = tm if seq_rows is None else seq_rows
    nseq = tm // rows
    base = SUBLANES
    n_chunks = D_FF // tf

    if carry is not None:
        @pl.when(pl.program_id(1) == 0)
        def _():
            carry[...] = jnp.zeros(carry.shape, F32)

    x = x_ref[...]
    h2 = _rms(x, nw_ref[...]).astype(BF16)

    def proj(f):
        return (_dot(h2, w_ref[:, f * tf:(f + 1) * tf]),
                _dot(h2, w_ref[:, D_FF + f * tf:D_FF + (f + 1) * tf]))

    nxt = proj(0)
    for f in range(n_chunks):
        g, u = nxt
        if f + 1 < n_chunks:
            nxt = proj(f + 1)
        cols = slice(f * tf, (f + 1) * tf)
        s = f % 2
        g3 = g.reshape(nseq, rows, tf)
        if seq_rows is None:
            gbuf[s, :, base - n_hist:base, :] = carry[:, SUBLANES - n_hist:, cols]
            carry[:, SUBLANES - n_hist:, cols] = g3[:, rows - n_hist:, :]
        else:
            gbuf[s, :, base - n_hist:base, :] = hist_ref[:, :, cols]
            conv_ref[:, :, cols] = g3[:, rows - n_hist:, :]
        gbuf[s, :, base:base + rows, :] = g3
        gc = cb_ref[:, cols] + g * cw_ref[CONV_W - 1:CONV_W, cols]
        for d in range(1, CONV_W):
            gm = gbuf[s, :, base - d:base - d + rows, :].reshape(tm, tf)
            gc = gc + gm * cw_ref[CONV_W - 1 - d:CONV_W - d, cols]
        act_scr[:, cols] = (gc * _sigmoid(gc) * u).astype(BF16)
    y_ref[...] = x + _dot(act_scr[...], wd_ref[...])

    if carry is not None:
        @pl.when(pl.program_id(1) == pl.num_programs(1) - 1)
        def _():
            conv_ref[...] = carry[:, SUBLANES - n_hist:, :]


def _ffn_scratch(tm, rows):
    return [pltpu.VMEM((2, tm // rows, SUBLANES + rows, FF_CHUNK), F32),
            pltpu.VMEM((tm, D_FF), BF16)]


def _ffn_prompt(batch, seq, x2d, nw, w, cw, cb, wd):
    tm = ROW_TILE
    nt = seq // tm
    full = lambda a: pl.BlockSpec(a.shape, lambda b, i: (0,) * a.ndim)
    once = lambda a: pl.BlockSpec(a.shape, lambda b, i: (0,) * a.ndim,
                                  pipeline_mode=pl.Buffered(1))
    row = pl.BlockSpec((tm, D_MODEL), lambda b, i: (b * nt + i, 0))
    return pl.pallas_call(
        functools.partial(_ffn_kernel, None),
        grid=(batch, nt),
        in_specs=[row, full(nw), once(w), full(cw), full(cb), once(wd)],
        out_specs=[row, pl.BlockSpec((1, CONV_W - 1, D_FF), lambda b, i: (b, 0, 0))],
        out_shape=[jax.ShapeDtypeStruct((batch * seq, D_MODEL), F32),
                   jax.ShapeDtypeStruct((batch, CONV_W - 1, D_FF), F32)],
        scratch_shapes=_ffn_scratch(tm, tm) + [pltpu.VMEM((1, SUBLANES, D_FF), F32)],
        compiler_params=pltpu.CompilerParams(dimension_semantics=("arbitrary", "arbitrary"),
                                             vmem_limit_bytes=VMEM_LIMIT),
        name="ffn_prompt",
    )(x2d, nw, w, cw, cb, wd)


def _ffn_sample(nb, t_len, x2d, hist, nw, w, cw, cb, wd):
    tm = ROW_TILE
    bt = tm // t_len
    full = lambda a: pl.BlockSpec(a.shape, lambda i: (0,) * a.ndim)
    once = lambda a: pl.BlockSpec(a.shape, lambda i: (0,) * a.ndim, pipeline_mode=pl.Buffered(1))
    row = pl.BlockSpec((tm, D_MODEL), lambda i: (i, 0))
    hist_spec = pl.BlockSpec((bt, CONV_W - 1, D_FF), lambda i: (i, 0, 0))
    return pl.pallas_call(
        functools.partial(_ffn_kernel, t_len),
        grid=(nb // bt,),
        in_specs=[row, hist_spec, full(nw), once(w), full(cw), full(cb), once(wd)],
        out_specs=[row, hist_spec],
        out_shape=[jax.ShapeDtypeStruct((nb * t_len, D_MODEL), F32),
                   jax.ShapeDtypeStruct((nb, CONV_W - 1, D_FF), F32)],
        scratch_shapes=_ffn_scratch(tm, t_len),
        compiler_params=pltpu.CompilerParams(dimension_semantics=("arbitrary",),
                                             vmem_limit_bytes=VMEM_LIMIT),
        name="ffn_sample",
    )(x2d, hist, nw, w, cw, cb, wd)


def _head_mean_matrix(width, head_dim):
    idx = np.arange(width) // head_dim
    return jnp.asarray((idx[:, None] == idx[None, :]).astype(np.float32) / head_dim, dtype=BF16)


def _layer_weights(norm_mix_w, w_in, b_gates, q_norm_w, k_norm_w, sinks, ml_norm_w, w_out,
                   norm_ffn_w, w_ffn_in, conv_w, conv_b, w_down):
    w_in_t = jnp.pad(w_in.T.astype(BF16), ((0, IN_WIDTH_PAD - w_in.shape[1]), (0, 0)))
    return dict(
        nw=norm_mix_w.reshape(1, D_MODEL),
        w_in_t=w_in_t,
        bg=jnp.pad(b_gates, (0, LANES - N_GATES)).reshape(1, LANES),
        qnw=(jnp.tile(q_norm_w, ATT_HEADS) * ATT_SCALE).reshape(1, ATT_Q_W),
        knw=jnp.tile(k_norm_w, ATT_KV_HEADS).reshape(1, ATT_KV_W),
        gq=_head_mean_matrix(ATT_Q_W, ATT_HEAD_DIM),
        gk=_head_mean_matrix(ATT_KV_W, ATT_HEAD_DIM),
        bg_col=b_gates.reshape(N_GATES, 1),
        qnw_col=(jnp.tile(q_norm_w, ATT_HEADS) * (ATT_SCALE * LOG2_E)).reshape(ATT_Q_W, 1),
        knw_col=jnp.tile(k_norm_w, ATT_KV_HEADS).reshape(ATT_KV_W, 1),
        mlnw_col=ml_norm_w.reshape(ML_V_W, 1),
        sinks=sinks,
        mlnw=ml_norm_w.reshape(1, ML_V_W),
        wout=w_out.astype(BF16),
        nfw=norm_ffn_w.reshape(1, D_MODEL),
        wff=w_ffn_in.astype(BF16),
        cw=conv_w,
        cb=conv_b.reshape(1, D_FF),
        wd=w_down.astype(BF16),
    )


def _cache_from_t(a_t):
    n = a_t.shape[0]
    return jnp.transpose(a_t.reshape(n, ATT_KV_HEADS, ATT_HEAD_DIM, WINDOW), (0, 3, 1, 2))


def _cache_to_t(a):
    n = a.shape[0]
    return jnp.transpose(a, (0, 2, 3, 1)).reshape(n, ATT_KV_W, WINDOW)


def _prompt_layer(x, w):
    batch, seq, _ = x.shape
    assert seq % ROW_TILE == 0 and seq % MIX_TILE == 0 and MIX_TILE % WINDOW == 0
    x2d = x.reshape(batch * seq, D_MODEL)
    qa, ks, kv, qm, km, vm, om, gt = _inproj_t(x2d, w["nw"], w["w_in_t"], w["bg_col"],
                                               w["qnw_col"], w["knw_col"])
    x1, c_t, n_row, m, k_t, v_t = _prompt_mixer_t(batch, seq, w["sinks"], qa, ks, kv, qm, km, vm,
                                                  om, gt, x2d, w["wout"], w["mlnw_col"])
    y, conv = _ffn_prompt(batch, seq, x1, w["nfw"], w["wff"], w["cw"], w["cb"], w["wd"])
    return (y.reshape(batch, seq, D_MODEL), _cache_from_t(k_t), _cache_from_t(v_t),
            jnp.swapaxes(c_t, -1, -2), n_row.reshape(batch, ML_HEADS, ML_QK_DIM),
            m.reshape(batch, ML_HEADS), conv)


def _sample_layer(x, ck, cv, c0, n0, m0, conv_buf, w):
    nb, t_len, _ = x.shape
    assert t_len == SUBLANES and SAMPLE_BT * t_len == LANES and nb % SAMPLE_BT == 0
    assert (nb * t_len) % ROW_TILE == 0
    x2d = x.reshape(nb * t_len, D_MODEL)
    qa, kv, qm, km, vm, om, gt = _inproj(x2d, w["nw"], w["w_in_t"], w["bg"], w["qnw"], w["knw"],
                                         w["gq"], w["gk"])
    x1, nk_t, nv_t, c_t, n, m = _sample_mixer(
        nb, t_len, w["sinks"], qa, kv, _cache_to_t(ck), _cache_to_t(cv), qm, km, vm, om, gt,
        jnp.swapaxes(c0, -1, -2), n0, m0.reshape(nb, 1, ML_HEADS), x2d, w["wout"], w["mlnw"])
    y, conv = _ffn_sample(nb, t_len, x1, conv_buf, w["nfw"], w["wff"], w["cw"], w["cb"],
                          w["wd"])
    return (y.reshape(nb, t_len, D_MODEL), _cache_from_t(nk_t), _cache_from_t(nv_t),
            jnp.swapaxes(c_t, -1, -2), n, m.reshape(nb, ML_HEADS), conv)


def kernel(x_prompt, x_sample, cache_attn_k, cache_attn_v, state_mlstm_C, state_mlstm_n,
           state_mlstm_m, cache_ffn_conv, norm_mix_w, w_in, b_gates, q_norm_w, k_norm_w,
           sinks, ml_norm_w, w_out, norm_ffn_w, w_ffn_in, conv_w, conv_b, w_down):
    depth = w_in.shape[0]
    yp, ys = x_prompt, x_sample
    sp = [[] for _ in range(6)]
    ss = [[] for _ in range(6)]
    for l in range(depth):
        w = _layer_weights(norm_mix_w[l], w_in[l], b_gates[l], q_norm_w[l], k_norm_w[l], sinks[l],
                           ml_norm_w[l], w_out[l], norm_ffn_w[l], w_ffn_in[l], conv_w[l],
                           conv_b[l], w_down[l])
        yp, *st_p = _prompt_layer(yp, w)
        ys, *st_s = _sample_layer(ys, cache_attn_k[l], cache_attn_v[l], state_mlstm_C[l],
                                  state_mlstm_n[l], state_mlstm_m[l], cache_ffn_conv[l], w)
        for i in range(6):
            sp[i].append(st_p[i])
            ss[i].append(st_s[i])
    k_p, v_p, c_p, n_p, m_p, conv_p = [jnp.stack(a) for a in sp]
    k_s, v_s, c_s, n_s, m_s, conv_s = [jnp.stack(a) for a in ss]
    return (yp, ys, k_p, v_p, c_p, n_p, m_p, conv_p, k_s, v_s, c_s, n_s, m_s, conv_s)
```

```python
import functools

import numpy as np
import jax
import jax.numpy as jnp
from jax import lax
from jax.experimental import pallas as pl
from jax.experimental.pallas import tpu as pltpu

F32 = jnp.float32
BF16 = jnp.bfloat16

D_MODEL = 1024
ATT_HEADS = 8
ATT_KV_HEADS = 2
ATT_HEAD_DIM = 64
ATT_GROUP = ATT_HEADS // ATT_KV_HEADS
WINDOW = 128
ML_HEADS = 4
ML_V_DIM = 128
ML_QK_DIM = 64
D_FF = 2816
CONV_W = 3
EPS = 1e-6
ATT_SCALE = ATT_HEAD_DIM ** -0.5
ML_SCALE = ML_QK_DIM ** -0.5
LOG2_E = 1.4426950408889634

ATT_Q_W = ATT_HEADS * ATT_HEAD_DIM
ATT_KV_W = ATT_KV_HEADS * ATT_HEAD_DIM
ML_QK_W = ML_HEADS * ML_QK_DIM
ML_V_W = ML_HEADS * ML_V_DIM
N_GATES = 2 * ML_HEADS
N_STACK = 2 * ATT_GROUP

LANES = 128
SUBLANES = 8

OFF_QA = 0
OFF_KV = OFF_QA + ATT_Q_W
OFF_QM = OFF_KV + 2 * ATT_KV_W
OFF_KM = OFF_QM + ML_QK_W
OFF_VM = OFF_KM + ML_QK_W
OFF_OM = OFF_VM + ML_V_W
OFF_GL = OFF_OM + ML_V_W
IN_WIDTH_PAD = OFF_GL + LANES

ATT_HEAD_ORDER = tuple(h for c in range(ATT_GROUP) for h in (c, c + ATT_GROUP))

ROW_TILE = 512
INPROJ_SUB = 256
MIX_TILE = 512
ML_CHUNK = 256
FF_CHUNK = 256
SAMPLE_BT = 16
VMEM_LIMIT = 56 * 1024 * 1024


def _dot(a, b):
    return jnp.dot(a, b, preferred_element_type=F32)


def _dot_nt(a, b):
    return lax.dot_general(a, b, (((1,), (1,)), ((), ())), preferred_element_type=F32)


def _split3(x):
    hi = x.astype(BF16)
    r1 = x - hi.astype(F32)
    mid = r1.astype(BF16)
    lo = (r1 - mid.astype(F32)).astype(BF16)
    return hi, mid, lo


def _rms(x, w):
    ms = jnp.mean(x * x, axis=-1, keepdims=True)
    return x * lax.rsqrt(ms + EPS) * w


def _log_sigmoid(x):
    return jnp.minimum(x, 0.0) - jnp.log1p(jnp.exp(-jnp.abs(x)))


def _sigmoid(x):
    return 1.0 / (1.0 + jnp.exp(-x))


def _permute_head_rows(dst_ref, src_ref):
    for k, h in enumerate(ATT_HEAD_ORDER):
        dst_ref[k * ATT_HEAD_DIM:(k + 1) * ATT_HEAD_DIM, :] = (
            src_ref[h * ATT_HEAD_DIM:(h + 1) * ATT_HEAD_DIM, :])


def _inproj_kernel(x_ref, nw_ref, w_ref, bg_ref, qnw_ref, knw_ref, gq_ref, gk_ref,
                   qa_ref, kv_ref, qm_ref, km_ref, vm_ref, om_ref, gt_ref, wq_scr):
    @pl.when(pl.program_id(0) == 0)
    def _():
        _permute_head_rows(wq_scr, w_ref)

    h = _rms(x_ref[...], nw_ref[...]).astype(BF16)

    def proj(lo, width):
        return _dot_nt(h, w_ref[lo:lo + width, :])

    q = _dot_nt(h, wq_scr[...])
    q_ms = _dot((q * q).astype(BF16), gq_ref[...])
    qa_ref[...] = (q * lax.rsqrt(q_ms + EPS) * qnw_ref[...]).astype(BF16)

    kv = proj(OFF_KV, 2 * ATT_KV_W)
    k = kv[:, :ATT_KV_W]
    k_ms = _dot((k * k).astype(BF16), gk_ref[...])
    kv_ref[:, :ATT_KV_W] = k * lax.rsqrt(k_ms + EPS) * knw_ref[...]
    kv_ref[:, ATT_KV_W:] = kv[:, ATT_KV_W:]

    qm_ref[...] = (proj(OFF_QM, ML_QK_W) * ML_SCALE).astype(BF16)
    km_ref[...] = proj(OFF_KM, ML_QK_W).astype(BF16)
    vm_ref[...] = proj(OFF_VM, ML_V_W).astype(BF16)
    om_ref[...] = proj(OFF_OM, ML_V_W).astype(BF16)

    gl = proj(OFF_GL, LANES) + bg_ref[...]
    lane = lax.broadcasted_iota(jnp.int32, gl.shape, 1)
    g = jnp.where(lane < ML_HEADS, gl, _log_sigmoid(gl))
    gt_ref[...] = g.T[:N_GATES, :]


def _inproj(x2d, nw, w_in_t, bg, qnw, knw, gq, gk):
    n = x2d.shape[0]
    tm = ROW_TILE
    row = lambda w: pl.BlockSpec((tm, w), lambda i: (i, 0))
    full = lambda a: pl.BlockSpec(a.shape, lambda i: (0,) * a.ndim)
    once = lambda a: pl.BlockSpec(a.shape, lambda i: (0,) * a.ndim, pipeline_mode=pl.Buffered(1))
    return pl.pallas_call(
        _inproj_kernel,
        grid=(n // tm,),
        in_specs=[row(D_MODEL), full(nw), once(w_in_t), full(bg), full(qnw), full(knw),
                  full(gq), full(gk)],
        out_specs=[row(ATT_Q_W), row(2 * ATT_KV_W), row(ML_QK_W), row(ML_QK_W),
                   row(ML_V_W), row(ML_V_W), pl.BlockSpec((N_GATES, tm), lambda i: (0, i))],
        out_shape=[jax.ShapeDtypeStruct((n, ATT_Q_W), BF16),
                   jax.ShapeDtypeStruct((n, 2 * ATT_KV_W), F32),
                   jax.ShapeDtypeStruct((n, ML_QK_W), BF16),
                   jax.ShapeDtypeStruct((n, ML_QK_W), BF16),
                   jax.ShapeDtypeStruct((n, ML_V_W), BF16),
                   jax.ShapeDtypeStruct((n, ML_V_W), BF16),
                   jax.ShapeDtypeStruct((N_GATES, n), F32)],
        scratch_shapes=[pltpu.VMEM((ATT_Q_W, D_MODEL), BF16)],
        compiler_params=pltpu.CompilerParams(dimension_semantics=("arbitrary",),
                                             vmem_limit_bytes=VMEM_LIMIT),
        name="inproj",
    )(x2d, nw, w_in_t, bg, qnw, knw, gq, gk)


def _head_norm_t(z, head_dim, w_col):
    rows, tokens = z.shape
    z3 = z.reshape(rows // head_dim, head_dim, tokens)
    ms = jnp.mean(z3 * z3, axis=1, keepdims=True)
    return (z3 * lax.rsqrt(ms + EPS)).reshape(rows, tokens) * w_col


def _inproj_t_kernel(x_ref, nw_ref, w_ref, bg_ref, qnw_ref, knw_ref,
                     qa_ref, ks_ref, kv_ref, qm_ref, km_ref, vm_ref, om_ref, gt_ref):
    tm = x_ref.shape[0]
    sub = INPROJ_SUB
    hs = [_rms(x_ref[c * sub:(c + 1) * sub, :], nw_ref[...]).astype(BF16)
          for c in range(tm // sub)]
    for c, h in enumerate(hs):
        tok = slice(c * sub, (c + 1) * sub)

        def proj(lo, width):
            return _dot_nt(w_ref[lo:lo + width, :], h)

        qa_ref[:, tok] = _head_norm_t(proj(OFF_QA, ATT_Q_W), ATT_HEAD_DIM,
                                      qnw_ref[...]).astype(BF16)
        kv = proj(OFF_KV, 2 * ATT_KV_W)
        k = _head_norm_t(kv[:ATT_KV_W], ATT_HEAD_DIM, knw_ref[...])
        kv_ref[:ATT_KV_W, tok] = k
        kv_ref[ATT_KV_W:, tok] = kv[ATT_KV_W:]
        ks_ref[tok, :] = k.T.astype(BF16)
        qm_ref[:, tok] = (proj(OFF_QM, ML_QK_W) * ML_SCALE).astype(BF16)
        km_ref[:, tok] = proj(OFF_KM, ML_QK_W).astype(BF16)
        vm_ref[:, tok] = proj(OFF_VM, ML_V_W).astype(BF16)
        om_ref[:, tok] = proj(OFF_OM, ML_V_W).astype(BF16)
        gl = proj(OFF_GL, 2 * SUBLANES)[:N_GATES] + bg_ref[...]
        row = lax.broadcasted_iota(jnp.int32, gl.shape, 0)
        gt_ref[:, tok] = jnp.where(row < ML_HEADS, gl, _log_sigmoid(gl))


def _inproj_t(x2d, nw, w_in_t, bg_col, qnw_col, knw_col):
    n = x2d.shape[0]
    tm = ROW_TILE
    full = lambda a: pl.BlockSpec(a.shape, lambda i: (0,) * a.ndim)
    once = lambda a: pl.BlockSpec(a.shape, lambda i: (0,) * a.ndim, pipeline_mode=pl.Buffered(1))
    col = lambda w: pl.BlockSpec((w, tm), lambda i: (0, i))
    return pl.pallas_call(
        _inproj_t_kernel,
        grid=(n // tm,),
        in_specs=[pl.BlockSpec((tm, D_MODEL), lambda i: (i, 0)), full(nw), once(w_in_t),
                  full(bg_col), full(qnw_col), full(knw_col)],
        out_specs=[col(ATT_Q_W), pl.BlockSpec((tm, ATT_KV_W), lambda i: (i, 0)),
                   col(2 * ATT_KV_W), col(ML_QK_W), col(ML_QK_W), col(ML_V_W), col(ML_V_W),
                   col(N_GATES)],
        out_shape=[jax.ShapeDtypeStruct((ATT_Q_W, n), BF16),
                   jax.ShapeDtypeStruct((n, ATT_KV_W), BF16),
                   jax.ShapeDtypeStruct((2 * ATT_KV_W, n), F32),
                   jax.ShapeDtypeStruct((ML_QK_W, n), BF16),
                   jax.ShapeDtypeStruct((ML_QK_W, n), BF16),
                   jax.ShapeDtypeStruct((ML_V_W, n), BF16),
                   jax.ShapeDtypeStruct((ML_V_W, n), BF16),
                   jax.ShapeDtypeStruct((N_GATES, n), F32)],
        compiler_params=pltpu.CompilerParams(dimension_semantics=("arbitrary",),
                                             vmem_limit_bytes=VMEM_LIMIT),
        name="inproj_t",
    )(x2d, nw, w_in_t, bg_col, qnw_col, knw_col)


def _gate_forms(gates, seg_mask, want_raw_col):
    L = gates.shape[1]
    m_bf = seg_mask.astype(F32).astype(BF16)
    cum_row = jnp.zeros(gates.shape, F32)
    cum_col = jnp.zeros((L, gates.shape[0]), F32)
    raw_col = None
    if want_raw_col:
        r = lax.broadcasted_iota(jnp.int32, (L, L), 0)
        c = lax.broadcasted_iota(jnp.int32, (L, L), 1)
        eye = (r == c).astype(F32).astype(BF16)
        raw_col = jnp.zeros((L, gates.shape[0]), F32)
    for part in _split3(gates):
        cum_row = cum_row + _dot_nt(part, m_bf)
        cum_col = cum_col + _dot_nt(m_bf, part)
        if want_raw_col:
            raw_col = raw_col + _dot_nt(eye, part)
    return cum_row, cum_col, raw_col


def _mlstm_intra(q_pad, k_pair, v_ext, seg_mask, b_c, b_r, ig_r, m_prev_c):
    dm = jnp.where(seg_mask, b_c + (ig_r - b_r), -jnp.inf)
    inter = b_c + m_prev_c
    m_row = jnp.maximum(inter, jnp.max(dm, axis=-1, keepdims=True))
    w_inter = jnp.exp(inter - m_row)
    p = _dot_nt(q_pad, k_pair) * jnp.exp(dm - m_row)
    return _dot(p.astype(BF16), v_ext), m_row, w_inter


def _mlstm_out(pv, m_row, w_inter, q_c, q_n, mlnw_h, om_h):
    num = pv[:, :ML_V_DIM] + w_inter * q_c
    den = pv[:, ML_V_DIM:ML_V_DIM + 1] + w_inter * q_n
    hh = num / jnp.maximum(jnp.abs(den), jnp.exp(-m_row))
    return (_rms(hh, mlnw_h) * _sigmoid(om_h.astype(F32))).astype(BF16)


def _ones_col(rows):
    lane = lax.broadcasted_iota(jnp.int32, (rows, LANES), 1)
    return (lane == 0).astype(F32).astype(BF16)


def _prompt_mixer_t_kernel(sinks_ref, qa_ref, ksc_ref, ksp_ref, kvc_ref, kvp_ref, qm_ref, km_ref,
                           vm_ref, om_ref, gt_ref, x_ref, wout_ref, mlnw_ref,
                           x1_ref, ct_ref, nrow_ref, m_ref, kt_ref, vt_ref,
                           mix_scr, state_scr, m_scr, band_scr, causal_scr, tri_scr,
                           s_scr_a, s_scr_b, e_scr):
    i = pl.program_id(1)
    A = WINDOW
    L = MIX_TILE
    C = ML_CHUNK
    n_pairs = ML_HEADS // 2

    @pl.when(i == 0)
    def _():
        state_scr[...] = jnp.zeros(state_scr.shape, F32)
        m_scr[...] = jnp.zeros(m_scr.shape, F32)
        kj = lax.broadcasted_iota(jnp.int32, (2 * A, A), 0)
        qi = lax.broadcasted_iota(jnp.int32, (2 * A, A), 1)
        band = (kj > qi) & (kj <= qi + WINDOW)
        band_scr[0] = jnp.where(band, 0.0, -jnp.inf)
        band_scr[1] = jnp.where(band & (kj >= A), 0.0, -jnp.inf)
        r = lax.broadcasted_iota(jnp.int32, (C, C), 0)
        c = lax.broadcasted_iota(jnp.int32, (C, C), 1)
        causal_scr[...] = jnp.where(r <= c, 0.0, -jnp.inf)
        tri_scr[...] = (r <= c).astype(F32).astype(BF16)

    k_all = jnp.concatenate([ksp_ref[...], ksc_ref[...]], axis=0)
    v_all = jnp.concatenate([kvp_ref[ATT_KV_W:, :], kvc_ref[ATT_KV_W:, :]], axis=1).astype(BF16)
    zero_q = jnp.zeros((ATT_HEAD_DIM, A), BF16)
    slot = lax.rem(i, 2)
    s_bufs = (s_scr_a, s_scr_b)

    def stage_scores(j):
        pieces = []
        for h in range(ATT_HEADS):
            q_h = qa_ref[h * ATT_HEAD_DIM:(h + 1) * ATT_HEAD_DIM, j * A:(j + 1) * A]
            pieces.append(jnp.concatenate([q_h, zero_q] if h < ATT_GROUP else [zero_q, q_h],
                                          axis=0))
        s_bufs[j % 2][slot] = _dot(k_all[j * A:(j + 2) * A, :], jnp.concatenate(pieces, axis=1))

    stage_scores(0)
    for j in range(L // A):
        cols = slice(j * A, (j + 1) * A)
        vt = v_all[:, j * A:(j + 2) * A]
        if j + 1 < L // A:
            stage_scores(j + 1)
        s_buf = s_bufs[j % 2]
        bias = jnp.where(i > 0, band_scr[0], band_scr[1]) if j == 0 else band_scr[0]
        m_rows = []
        for h in range(ATT_HEADS):
            sb = s_buf[slot, :, h * A:(h + 1) * A] + bias
            m_rows.append(jnp.maximum(jnp.max(sb, axis=0, keepdims=True),
                                      sinks_ref[h] * LOG2_E))
        inv_rows = []
        for h in range(ATT_HEADS):
            e = jnp.exp2(s_buf[slot, :, h * A:(h + 1) * A] + (bias - m_rows[h]))
            e_scr[:, h * A:(h + 1) * A] = e.astype(BF16)
            inv_rows.append(1.0 / (jnp.sum(e, axis=0, keepdims=True)
                                   + jnp.exp2(sinks_ref[h] * LOG2_E - m_rows[h])))
        o = _dot(vt, e_scr[...])
        for h in range(ATT_HEADS):
            g = h // ATT_GROUP
            mix_scr[h * ATT_HEAD_DIM:(h + 1) * ATT_HEAD_DIM, cols] = (
                o[g * ATT_HEAD_DIM:(g + 1) * ATT_HEAD_DIM, h * A:(h + 1) * A]
                * inv_rows[h]).astype(BF16)

    row128 = lax.broadcasted_iota(jnp.int32, (LANES, C), 0)
    ones_rows = (row128 == 0).astype(F32).astype(BF16)
    for ci in range(L // C):
        tok = slice(ci * C, (ci + 1) * C)
        gates = gt_ref[:, tok] * LOG2_E
        cum_row = jnp.zeros(gates.shape, F32)
        for part in _split3(gates):
            cum_row = cum_row + _dot(part, tri_scr[...])
        ig_rows = gates[:ML_HEADS]
        b_rows = cum_row[ML_HEADS:]
        key_cols = jnp.concatenate([ig_rows - b_rows, jnp.zeros((LANES - ML_HEADS, C), F32)],
                                   axis=0).T
        for p in range(n_pairs):
            q_c = qm_ref[p * LANES:(p + 1) * LANES, tok]
            k_pair = km_ref[p * LANES:(p + 1) * LANES, tok]
            zero = jnp.zeros_like(q_c)
            state = state_scr[p]
            state_bf = state.astype(BF16)
            new_state = []
            for e_id in range(2):
                h = 2 * p + e_id
                v_rows = slice(h * ML_V_DIM, (h + 1) * ML_V_DIM)
                head_rows = (row128 < ML_QK_DIM) if e_id == 0 else (row128 >= ML_QK_DIM)
                q_pad = jnp.where(head_rows, q_c, zero)
                b_r = b_rows[h:h + 1, :]
                ig_r = ig_rows[h:h + 1, :]
                m_prev = m_scr[h:h + 1, 0:1]
                dm = (b_r + key_cols[:, h:h + 1]) + causal_scr[...]
                inter = b_r + m_prev
                m_row = jnp.maximum(inter, jnp.max(dm, axis=0, keepdims=True))
                w_inter = jnp.exp2(inter - m_row)
                qk = lax.dot_general(k_pair, q_pad, (((0,), (0,)), ((), ())),
                                     preferred_element_type=F32)
                p_t = (qk * jnp.exp2(dm - m_row)).astype(BF16)
                v_ext = jnp.concatenate([vm_ref[v_rows, tok], ones_rows], axis=0)
                num = _dot(v_ext, p_t) + w_inter * _dot(state_bf, q_pad)
                den = num[ML_V_DIM:ML_V_DIM + 1, :]
                hh = num[:ML_V_DIM] * (1.0 / jnp.maximum(jnp.abs(den), jnp.exp2(-m_row)))
                ms = jnp.mean(hh * hh, axis=0, keepdims=True)
                gate = _sigmoid(om_ref[v_rows, tok].astype(F32))
                mix_scr[ATT_Q_W + h * ML_V_DIM:ATT_Q_W + (h + 1) * ML_V_DIM, tok] = (
                    hh * lax.rsqrt(ms + EPS) * mlnw_ref[v_rows, :] * gate).astype(BF16)
                b_last = b_r[:, C - 1:C]
                a_r = b_last - b_r + ig_r
                m_new = jnp.maximum(b_last + m_prev, jnp.max(a_r, axis=-1, keepdims=True))
                sc = jnp.exp2(b_last + m_prev - m_new)
                wsv = (v_ext.astype(F32) * jnp.exp2(a_r - m_new)).astype(BF16)
                new_state.append(sc * state + _dot_nt(wsv, k_pair))
                m_scr[h:h + 1, :] = jnp.broadcast_to(m_new, (1, LANES))
            first = lax.broadcasted_iota(jnp.int32, state.shape, 1) < ML_QK_DIM
            state_scr[p] = jnp.where(first, new_state[0], new_state[1])

    x1_ref[...] = x_ref[...] + lax.dot_general(
        mix_scr[...], wout_ref[...], (((0,), (0,)), ((), ())), preferred_element_type=F32)

    @pl.when(i == pl.num_programs(1) - 1)
    def _():
        for p in range(n_pairs):
            c_t = state_scr[p, :ML_V_DIM, :].T
            for e_id in range(2):
                ct_ref[0, 2 * p + e_id] = c_t[e_id * ML_QK_DIM:(e_id + 1) * ML_QK_DIM, :]
            nrow_ref[0, p:p + 1, :] = state_scr[p, ML_V_DIM:ML_V_DIM + 1, :]
        for h in range(ML_HEADS):
            m_ref[0, :, h:h + 1] = m_scr[h:h + 1, 0:1] * (1.0 / LOG2_E)
        kt_ref[0] = kvc_ref[:ATT_KV_W, L - WINDOW:]
        vt_ref[0] = kvc_ref[ATT_KV_W:, L - WINDOW:]


def _prompt_mixer_t(batch, seq, sinks, qa, ks, kv, qm, km, vm, om, gt, x2d, wout, mlnw_col):
    tq = MIX_TILE
    nt = seq // tq
    sub = tq // WINDOW
    col = lambda w: pl.BlockSpec((w, tq), lambda b, i: (0, b * nt + i))
    full = lambda a: pl.BlockSpec(a.shape, lambda b, i: (0,) * a.ndim)
    once = lambda a: pl.BlockSpec(a.shape, lambda b, i: (0,) * a.ndim,
                                  pipeline_mode=pl.Buffered(1))
    prev_block = lambda b, i: jnp.maximum((b * nt + i) * sub - 1, 0)
    per_batch = lambda *dims: pl.BlockSpec((1,) + dims, lambda b, i: (b,) + (0,) * len(dims))
    return pl.pallas_call(
        _prompt_mixer_t_kernel,
        grid=(batch, nt),
        in_specs=[pl.BlockSpec(memory_space=pltpu.SMEM),
                  col(ATT_Q_W),
                  pl.BlockSpec((tq, ATT_KV_W), lambda b, i: (b * nt + i, 0)),
                  pl.BlockSpec((WINDOW, ATT_KV_W), lambda b, i: (prev_block(b, i), 0)),
                  col(2 * ATT_KV_W),
                  pl.BlockSpec((2 * ATT_KV_W, WINDOW), lambda b, i: (0, prev_block(b, i))),
                  col(ML_QK_W), col(ML_QK_W), col(ML_V_W), col(ML_V_W), col(N_GATES),
                  pl.BlockSpec((tq, D_MODEL), lambda b, i: (b * nt + i, 0)),
                  once(wout), full(mlnw_col)],
        out_specs=[pl.BlockSpec((tq, D_MODEL), lambda b, i: (b * nt + i, 0)),
                   per_batch(ML_HEADS, ML_QK_DIM, ML_V_DIM),
                   per_batch(ML_HEADS // 2, LANES),
                   per_batch(1, ML_HEADS),
                   per_batch(ATT_KV_W, WINDOW),
                   per_batch(ATT_KV_W, WINDOW)],
        out_shape=[jax.ShapeDtypeStruct((batch * seq, D_MODEL), F32),
                   jax.ShapeDtypeStruct((batch, ML_HEADS, ML_QK_DIM, ML_V_DIM), F32),
                   jax.ShapeDtypeStruct((batch, ML_HEADS // 2, LANES), F32),
                   jax.ShapeDtypeStruct((batch, 1, ML_HEADS), F32),
                   jax.ShapeDtypeStruct((batch, ATT_KV_W, WINDOW), F32),
                   jax.ShapeDtypeStruct((batch, ATT_KV_W, WINDOW), F32)],
        scratch_shapes=[pltpu.VMEM((D_MODEL, tq), BF16),
                        pltpu.VMEM((ML_HEADS // 2, 2 * LANES, LANES), F32),
                        pltpu.VMEM((SUBLANES, LANES), F32),
                        pltpu.VMEM((2, 2 * WINDOW, WINDOW), F32),
                        pltpu.VMEM((ML_CHUNK, ML_CHUNK), F32),
                        pltpu.VMEM((ML_CHUNK, ML_CHUNK), BF16),
                        pltpu.VMEM((2, 2 * WINDOW, ATT_HEADS * WINDOW), F32),
                        pltpu.VMEM((2, 2 * WINDOW, ATT_HEADS * WINDOW), F32),
                        pltpu.VMEM((2 * WINDOW, ATT_HEADS * WINDOW), BF16)],
        compiler_params=pltpu.CompilerParams(dimension_semantics=("arbitrary", "arbitrary"),
                                             vmem_limit_bytes=VMEM_LIMIT),
        name="prompt_mixer_t",
    )(sinks, qa, ks, ks, kv, kv, qm, km, vm, om, gt, x2d, wout, mlnw_col)


def _sample_mixer_kernel(t_len, sinks_ref, qa_ref, kv_ref, ck_ref, cv_ref, qm_ref, km_ref,
                         vm_ref, om_ref, gt_ref, c0_ref, n0_ref, m0_ref, x_ref, wout_ref,
                         mlnw_ref, x1_ref, nk_ref, nv_ref, c_ref, n_ref, m_ref,
                         mix_scr, wperm_scr):
    bt = SAMPLE_BT
    T = t_len
    L = bt * T

    @pl.when(pl.program_id(0) == 0)
    def _():
        _permute_head_rows(wperm_scr, wout_ref)
        wperm_scr[ATT_Q_W:, :] = wout_ref[ATT_Q_W:, :]

    lane3 = lax.broadcasted_iota(jnp.int32, (bt, T, LANES), 2)
    low3 = lane3 < ATT_HEAD_DIM
    lane = lax.broadcasted_iota(jnp.int32, (L, LANES), 1)
    low = lane < ATT_HEAD_DIM

    qa3 = qa_ref[...].astype(F32).reshape(bt, T, ATT_Q_W)
    pieces = []
    for col in range(ATT_GROUP):
        qc = qa3[:, :, col * LANES:(col + 1) * LANES]
        pieces += [jnp.where(low3, qc, 0.0), jnp.where(low3, 0.0, qc)]
    q3 = jnp.concatenate(pieces, axis=1).astype(BF16)
    R = bt * N_STACK * T
    q2 = q3.reshape(R, LANES)
    kv_new = kv_ref[...]
    k_new = kv_new[:, :ATT_KV_W]
    v_new = kv_new[:, ATT_KV_W:]
    ck = ck_ref[...]
    cv = cv_ref[...]
    s_c = jnp.einsum('bqd,bdk->bqk', q3, ck.astype(BF16),
                     preferred_element_type=F32).reshape(R, WINDOW)
    s_n = _dot_nt(q2, k_new.astype(BF16))
    row_c = lax.broadcasted_iota(jnp.int32, (R, WINDOW), 0)
    col_c = lax.broadcasted_iota(jnp.int32, (R, WINDOW), 1)
    s_c = jnp.where(col_c > row_c % T, s_c, -jnp.inf)
    row_n = lax.broadcasted_iota(jnp.int32, (R, L), 0)
    col_n = lax.broadcasted_iota(jnp.int32, (R, L), 1)
    valid_n = (row_n // (N_STACK * T) == col_n // T) & (col_n % T <= row_n % T)
    s_n = jnp.where(valid_n, s_n, -jnp.inf)
    stack_id = (lax.broadcasted_iota(jnp.int32, (R, 1), 0) // T) % N_STACK
    sink = jnp.zeros((R, 1), F32)
    for k_id in range(N_STACK):
        sink = jnp.where(stack_id == k_id, sinks_ref[ATT_HEAD_ORDER[k_id]], sink)
    m = jnp.maximum(jnp.maximum(jnp.max(s_c, axis=-1, keepdims=True),
                                jnp.max(s_n, axis=-1, keepdims=True)), sink)
    e_c = jnp.exp(s_c - m)
    e_n = jnp.exp(s_n - m)
    denom = (jnp.sum(e_c, axis=-1, keepdims=True) + jnp.sum(e_n, axis=-1, keepdims=True)
             + jnp.exp(sink - m))
    o = jnp.einsum('bqk,bdk->bqd', e_c.astype(BF16).reshape(bt, N_STACK * T, WINDOW),
                   cv.astype(BF16), preferred_element_type=F32).reshape(R, LANES)
    o = (o + _dot(e_n.astype(BF16), v_new.astype(BF16))) / denom
    o3 = o.reshape(bt, N_STACK * T, LANES)
    for col in range(ATT_GROUP):
        lo_h = o3[:, (2 * col) * T:(2 * col + 1) * T, :]
        hi_h = o3[:, (2 * col + 1) * T:(2 * col + 2) * T, :]
        mix_scr[:, col * LANES:(col + 1) * LANES] = jnp.where(
            low3, lo_h, hi_h).reshape(L, LANES).astype(BF16)

    keep = lax.broadcasted_iota(jnp.int32, (ATT_KV_W, WINDOW), 1) < WINDOW - T
    k_new_t = k_new.T
    v_new_t = v_new.T
    for q in range(bt):
        shift = (WINDOW - T - q * T) % WINDOW
        nk_ref[q] = jnp.where(keep, pltpu.roll(ck[q], WINDOW - T, axis=1),
                              pltpu.roll(k_new_t, shift, axis=1))
        nv_ref[q] = jnp.where(keep, pltpu.roll(cv[q], WINDOW - T, axis=1),
                              pltpu.roll(v_new_t, shift, axis=1))

    r = lax.broadcasted_iota(jnp.int32, (L, L), 0)
    c = lax.broadcasted_iota(jnp.int32, (L, L), 1)
    seg = (r // T == c // T) & (r <= c)
    seg_bias = jnp.where(seg, 0.0, -jnp.inf)
    seg_bf = seg.astype(F32).astype(BF16)
    gates = gt_ref[...] * LOG2_E
    cum_row = jnp.zeros(gates.shape, F32)
    for part in _split3(gates):
        cum_row = cum_row + _dot(part, seg_bf)
    ig_rows = gates[:ML_HEADS]
    b_rows = cum_row[ML_HEADS:]
    gate_cols = jnp.concatenate([ig_rows, b_rows, jnp.zeros((LANES - N_GATES, L), F32)],
                                axis=0).T

    def col_to_row(x_col):
        return jnp.broadcast_to(x_col, (L, LANES)).T[0:1, :]

    ones_rows = (r[:LANES] == 0).astype(F32).astype(BF16)
    qm = qm_ref[...]
    km = km_ref[...]
    qm_f = qm.astype(F32)
    km_f = km.astype(F32)
    n_rep = bt * ML_QK_DIM // LANES
    bd_row = lax.broadcasted_iota(jnp.int32, (L, bt * ML_QK_DIM), 0) // T
    bd_lane = lax.broadcasted_iota(jnp.int32, (L, bt * ML_QK_DIM), 1) // ML_QK_DIM
    block_diag = bd_row == bd_lane

    def spread(x_pair, e):
        other = pltpu.roll(x_pair, ML_QK_DIM, axis=1)
        twice = jnp.where(low, x_pair, other) if e == 0 else jnp.where(low, other, x_pair)
        return jnp.where(block_diag, jnp.concatenate([twice] * n_rep, axis=1), 0.0).astype(BF16)

    for h in range(ML_HEADS):
        p, e = divmod(h, 2)
        qc = qm[:, p * LANES:(p + 1) * LANES]
        k_pair = km[:, p * LANES:(p + 1) * LANES]
        zero = jnp.zeros_like(qc)
        q_pad = jnp.where(low, qc, zero) if e == 0 else jnp.where(low, zero, qc)
        v_h = vm_ref[:, h * ML_V_DIM:(h + 1) * ML_V_DIM]
        v_ext_t = jnp.concatenate([v_h.astype(F32).T.astype(BF16), ones_rows], axis=0)
        ig_c = gate_cols[:, h:h + 1]
        b_c = gate_cols[:, ML_HEADS + h:ML_HEADS + h + 1]
        b_r = b_rows[h:h + 1, :]
        m0 = m0_ref[:, :, h:h + 1] * LOG2_E
        inter = b_r + col_to_row(jnp.broadcast_to(m0, (bt, T, 1)).reshape(L, 1))
        dm = (b_r + (ig_c - b_c)) + seg_bias
        m_row = jnp.maximum(inter, jnp.max(dm, axis=0, keepdims=True))
        w_inter = jnp.exp2(inter - m_row)
        p_t = (_dot_nt(k_pair, q_pad) * jnp.exp2(dm - m_row)).astype(BF16)
        num_t = _dot(v_ext_t, p_t)
        q_h3 = qm_f[:, h * ML_QK_DIM:(h + 1) * ML_QK_DIM].reshape(bt, T, ML_QK_DIM)
        k_h3 = km_f[:, h * ML_QK_DIM:(h + 1) * ML_QK_DIM].reshape(bt, T, ML_QK_DIM)
        c0 = c0_ref[:, h]
        n0 = n0_ref[:, h:h + 1, :]
        q_c_t = _dot(spread(qm_f[:, p * LANES:(p + 1) * LANES], e),
                     c0.astype(BF16).reshape(bt * ML_QK_DIM, ML_V_DIM)).T
        q_n_r = col_to_row(jnp.sum(q_h3 * n0, axis=-1, keepdims=True).reshape(L, 1))
        num = num_t[:ML_V_DIM] + w_inter * q_c_t
        den = num_t[ML_V_DIM:ML_V_DIM + 1] + w_inter * q_n_r
        hh = num * (1.0 / jnp.maximum(jnp.abs(den), jnp.exp2(-m_row)))
        ms = jnp.mean(hh * hh, axis=0, keepdims=True)
        mix_scr[:, ATT_Q_W + h * ML_V_DIM:ATT_Q_W + (h + 1) * ML_V_DIM] = (
            (hh * lax.rsqrt(ms + EPS)).T * mlnw_ref[:, h * ML_V_DIM:(h + 1) * ML_V_DIM]
            * _sigmoid(om_ref[:, h * ML_V_DIM:(h + 1) * ML_V_DIM].astype(F32))).astype(BF16)
        b3 = b_c.reshape(bt, T, 1)
        b_last = b3[:, T - 1:T, :]
        a3 = b_last - b3 + ig_c.reshape(bt, T, 1)
        m_new = jnp.maximum(b_last + m0, jnp.max(a3, axis=1, keepdims=True))
        sc = jnp.exp2(b_last + m0 - m_new)
        ws = jnp.exp2(a3 - m_new)
        kw = spread(km_f[:, p * LANES:(p + 1) * LANES] * ws.reshape(L, 1), e)
        d_c = lax.dot_general(kw, v_h, (((0,), (0,)), ((), ())), preferred_element_type=F32)
        c_ref[:, h] = sc * c0 + d_c.reshape(bt, ML_QK_DIM, ML_V_DIM)
        n_ref[:, h:h + 1, :] = sc * n0 + jnp.sum(ws * k_h3, axis=1, keepdims=True)
        m_ref[:, :, h:h + 1] = m_new * (1.0 / LOG2_E)

    x1_ref[...] = x_ref[...] + _dot(mix_scr[...], wperm_scr[...])


def _sample_mixer(nb, t_len, sinks, qa, kv, ck, cv, qm, km, vm, om, gt, c0, n0, m0, x2d, wout, mlnw):
    bt = SAMPLE_BT
    tl = bt * t_len
    row = lambda w: pl.BlockSpec((tl, w), lambda i: (i, 0))
    full = lambda a: pl.BlockSpec(a.shape, lambda i: (0,) * a.ndim)
    once = lambda a: pl.BlockSpec(a.shape, lambda i: (0,) * a.ndim, pipeline_mode=pl.Buffered(1))
    cache = pl.BlockSpec((bt, ATT_KV_W, WINDOW), lambda i: (i, 0, 0))
    c_spec = pl.BlockSpec((bt, ML_HEADS, ML_QK_DIM, ML_V_DIM), lambda i: (i, 0, 0, 0))
    n_spec = pl.BlockSpec((bt, ML_HEADS, ML_QK_DIM), lambda i: (i, 0, 0))
    m_spec = pl.BlockSpec((bt, 1, ML_HEADS), lambda i: (i, 0, 0))
    return pl.pallas_call(
        functools.partial(_sample_mixer_kernel, t_len),
        grid=(nb // bt,),
        in_specs=[pl.BlockSpec(memory_space=pltpu.SMEM),
                  row(ATT_Q_W), row(2 * ATT_KV_W), cache, cache, row(ML_QK_W), row(ML_QK_W),
                  row(ML_V_W), row(ML_V_W), pl.BlockSpec((N_GATES, tl), lambda i: (0, i)),
                  c_spec, n_spec, m_spec, row(D_MODEL), once(wout), full(mlnw)],
        out_specs=[row(D_MODEL), cache, cache, c_spec, n_spec, m_spec],
        out_shape=[jax.ShapeDtypeStruct((nb * t_len, D_MODEL), F32),
                   jax.ShapeDtypeStruct((nb, ATT_KV_W, WINDOW), F32),
                   jax.ShapeDtypeStruct((nb, ATT_KV_W, WINDOW), F32),
                   jax.ShapeDtypeStruct((nb, ML_HEADS, ML_QK_DIM, ML_V_DIM), F32),
                   jax.ShapeDtypeStruct((nb, ML_HEADS, ML_QK_DIM), F32),
                   jax.ShapeDtypeStruct((nb, 1, ML_HEADS), F32)],
        scratch_shapes=[pltpu.VMEM((tl, D_MODEL), BF16),
                        pltpu.VMEM((D_MODEL, D_MODEL), BF16)],
        compiler_params=pltpu.CompilerParams(dimension_semantics=("arbitrary",),
                                             vmem_limit_bytes=VMEM_LIMIT),
        name="sample_mixer",
    )(sinks, qa, kv, ck, cv, qm, km, vm, om, gt, c0, n0, m0, x2d, wout, mlnw)


def _ffn_kernel(seq_rows, *refs):
    if seq_rows is None:
        (x_ref, nw_ref, w_ref, cw_ref, cb_ref, wd_ref, y_ref, conv_ref,
         gbuf, act_scr, carry) = refs
        hist_ref = None
    else:
        (x_ref, hist_ref, nw_ref, w_ref, cw_ref, cb_ref, wd_ref, y_ref, conv_ref,
         gbuf, act_scr) = refs
        carry = None
    tm = x_ref.shape[0]
    tf = FF_CHUNK
    n_hist = CONV_W - 1
    rows = tm if seq_rows is None else seq_rows
    nseq = tm // rows
    base = SUBLANES
    n_chunks = D_FF // tf

    if carry is not None:
        @pl.when(pl.program_id(1) == 0)
        def _():
            carry[...] = jnp.zeros(carry.shape, F32)

    x = x_ref[...]
    h2 = _rms(x, nw_ref[...]).astype(BF16)

    def proj(f):
        return (_dot(h2, w_ref[:, f * tf:(f + 1) * tf]),
                _dot(h2, w_ref[:, D_FF + f * tf:D_FF + (f + 1) * tf]))

    nxt = proj(0)
    for f in range(n_chunks):
        g, u = nxt
        if f + 1 < n_chunks:
            nxt = proj(f + 1)
        cols = slice(f * tf, (f + 1) * tf)
        s = f % 2
        g3 = g.reshape(nseq, rows, tf)
        if seq_rows is None:
            gbuf[s, :, base - n_hist:base, :] = carry[:, SUBLANES - n_hist:, cols]
            carry[:, SUBLANES - n_hist:, cols] = g3[:, rows - n_hist:, :]
        else:
            gbuf[s, :, base - n_hist:base, :] = hist_ref[:, :, cols]
            conv_ref[:, :, cols] = g3[:, rows - n_hist:, :]
        gbuf[s, :, base:base + rows, :] = g3
        gc = cb_ref[:, cols] + g * cw_ref[CONV_W - 1:CONV_W, cols]
        for d in range(1, CONV_W):
            gm = gbuf[s, :, base - d:base - d + rows, :].reshape(tm, tf)
            gc = gc + gm * cw_ref[CONV_W - 1 - d:CONV_W - d, cols]
        act_scr[:, cols] = (gc * _sigmoid(gc) * u).astype(BF16)
    y_ref[...] = x + _dot(act_scr[...], wd_ref[...])

    if carry is not None:
        @pl.when(pl.program_id(1) == pl.num_programs(1) - 1)
        def _():
            conv_ref[...] = carry[:, SUBLANES - n_hist:, :]


def _ffn_scratch(tm, rows):
    return [pltpu.VMEM((2, tm // rows, SUBLANES + rows, FF_CHUNK), F32),
            pltpu.VMEM((tm, D_FF), BF16)]


def _ffn_prompt(batch, seq, x2d, nw, w, cw, cb, wd):
    tm = ROW_TILE
    nt = seq // tm
    full = lambda a: pl.BlockSpec(a.shape, lambda b, i: (0,) * a.ndim)
    once = lambda a: pl.BlockSpec(a.shape, lambda b, i: (0,) * a.ndim,
                                  pipeline_mode=pl.Buffered(1))
    row = pl.BlockSpec((tm, D_MODEL), lambda b, i: (b * nt + i, 0))
    return pl.pallas_call(
        functools.partial(_ffn_kernel, None),
        grid=(batch, nt),
        in_specs=[row, full(nw), once(w), full(cw), full(cb), once(wd)],
        out_specs=[row, pl.BlockSpec((1, CONV_W - 1, D_FF), lambda b, i: (b, 0, 0))],
        out_shape=[jax.ShapeDtypeStruct((batch * seq, D_MODEL), F32),
                   jax.ShapeDtypeStruct((batch, CONV_W - 1, D_FF), F32)],
        scratch_shapes=_ffn_scratch(tm, tm) + [pltpu.VMEM((1, SUBLANES, D_FF), F32)],
        compiler_params=pltpu.CompilerParams(dimension_semantics=("arbitrary", "arbitrary"),
                                             vmem_limit_bytes=VMEM_LIMIT),
        name="ffn_prompt",
    )(x2d, nw, w, cw, cb, wd)


def _ffn_sample(nb, t_len, x2d, hist, nw, w, cw, cb, wd):
    tm = ROW_TILE
    bt = tm // t_len
    full = lambda a: pl.BlockSpec(a.shape, lambda i: (0,) * a.ndim)
    once = lambda a: pl.BlockSpec(a.shape, lambda i: (0,) * a.ndim, pipeline_mode=pl.Buffered(1))
    row = pl.BlockSpec((tm, D_MODEL), lambda i: (i, 0))
    hist_spec = pl.BlockSpec((bt, CONV_W - 1, D_FF), lambda i: (i, 0, 0))
    return pl.pallas_call(
        functools.partial(_ffn_kernel, t_len),
        grid=(nb // bt,),
        in_specs=[row, hist_spec, full(nw), once(w), full(cw), full(cb), once(wd)],
        out_specs=[row, hist_spec],
        out_shape=[jax.ShapeDtypeStruct((nb * t_len, D_MODEL), F32),
                   jax.ShapeDtypeStruct((nb, CONV_W - 1, D_FF), F32)],
        scratch_shapes=_ffn_scratch(tm, t_len),
        compiler_params=pltpu.CompilerParams(dimension_semantics=("arbitrary",),
                                             vmem_limit_bytes=VMEM_LIMIT),
        name="ffn_sample",
    )(x2d, hist, nw, w, cw, cb, wd)


def _head_mean_matrix(width, head_dim):
    idx = np.arange(width) // head_dim
    return jnp.asarray((idx[:, None] == idx[None, :]).astype(np.float32) / head_dim, dtype=BF16)


def _layer_weights(norm_mix_w, w_in, b_gates, q_norm_w, k_norm_w, sinks, ml_norm_w, w_out,
                   norm_ffn_w, w_ffn_in, conv_w, conv_b, w_down):
    w_in_t = jnp.pad(w_in.T.astype(BF16), ((0, IN_WIDTH_PAD - w_in.shape[1]), (0, 0)))
    return dict(
        nw=norm_mix_w.reshape(1, D_MODEL),
        w_in_t=w_in_t,
        bg=jnp.pad(b_gates, (0, LANES - N_GATES)).reshape(1, LANES),
        qnw=(jnp.tile(q_norm_w, ATT_HEADS) * ATT_SCALE).reshape(1, ATT_Q_W),
        knw=jnp.tile(k_norm_w, ATT_KV_HEADS).reshape(1, ATT_KV_W),
        gq=_head_mean_matrix(ATT_Q_W, ATT_HEAD_DIM),
        gk=_head_mean_matrix(ATT_KV_W, ATT_HEAD_DIM),
        bg_col=b_gates.reshape(N_GATES, 1),
        qnw_col=(jnp.tile(q_norm_w, ATT_HEADS) * (ATT_SCALE * LOG2_E)).reshape(ATT_Q_W, 1),
        knw_col=jnp.tile(k_norm_w, ATT_KV_HEADS).reshape(ATT_KV_W, 1),
        mlnw_col=ml_norm_w.reshape(ML_V_W, 1),
        sinks=sinks,
        mlnw=ml_norm_w.reshape(1, ML_V_W),
        wout=w_out.astype(BF16),
        nfw=norm_ffn_w.reshape(1, D_MODEL),
        wff=w_ffn_in.astype(BF16),
        cw=conv_w,
        cb=conv_b.reshape(1, D_FF),
        wd=w_down.astype(BF16),
    )


def _cache_from_t(a_t):
    n = a_t.shape[0]
    return jnp.transpose(a_t.reshape(n, ATT_KV_HEADS, ATT_HEAD_DIM, WINDOW), (0, 3, 1, 2))


def _cache_to_t(a):
    n = a.shape[0]
    return jnp.transpose(a, (0, 2, 3, 1)).reshape(n, ATT_KV_W, WINDOW)


def _prompt_layer(x, w):
    batch, seq, _ = x.shape
    assert seq % ROW_TILE == 0 and seq % MIX_TILE == 0 and MIX_TILE % WINDOW == 0
    x2d = x.reshape(batch * seq, D_MODEL)
    qa, ks, kv, qm, km, vm, om, gt = _inproj_t(x2d, w["nw"], w["w_in_t"], w["bg_col"],
                                               w["qnw_col"], w["knw_col"])
    x1, c_t, n_row, m, k_t, v_t = _prompt_mixer_t(batch, seq, w["sinks"], qa, ks, kv, qm, km, vm,
                                                  om, gt, x2d, w["wout"], w["mlnw_col"])
    y, conv = _ffn_prompt(batch, seq, x1, w["nfw"], w["wff"], w["cw"], w["cb"], w["wd"])
    return (y.reshape(batch, seq, D_MODEL), _cache_from_t(k_t), _cache_from_t(v_t),
            jnp.swapaxes(c_t, -1, -2), n_row.reshape(batch, ML_HEADS, ML_QK_DIM),
            m.reshape(batch, ML_HEADS), conv)


def _sample_layer(x, ck, cv, c0, n0, m0, conv_buf, w):
    nb, t_len, _ = x.shape
    assert t_len == SUBLANES and SAMPLE_BT * t_len == LANES and nb % SAMPLE_BT == 0
    assert (nb * t_len) % ROW_TILE == 0
    x2d = x.reshape(nb * t_len, D_MODEL)
    qa, kv, qm, km, vm, om, gt = _inproj(x2d, w["nw"], w["w_in_t"], w["bg"], w["qnw"], w["knw"],
                                         w["gq"], w["gk"])
    x1, nk_t, nv_t, c_t, n, m = _sample_mixer(
        nb, t_len, w["sinks"], qa, kv, _cache_to_t(ck), _cache_to_t(cv), qm, km, vm, om, gt,
        jnp.swapaxes(c0, -1, -2), n0, m0.reshape(nb, 1, ML_HEADS), x2d, w["wout"], w["mlnw"])
    y, conv = _ffn_sample(nb, t_len, x1, conv_buf, w["nfw"], w["wff"], w["cw"], w["cb"],
                          w["wd"])
    return (y.reshape(nb, t_len, D_MODEL), _cache_from_t(nk_t), _cache_from_t(nv_t),
            jnp.swapaxes(c_t, -1, -2), n, m.reshape(nb, ML_HEADS), conv)


def kernel(x_prompt, x_sample, cache_attn_k, cache_attn_v, state_mlstm_C, state_mlstm_n,
           state_mlstm_m, cache_ffn_conv, norm_mix_w, w_in, b_gates, q_norm_w, k_norm_w,
           sinks, ml_norm_w, w_out, norm_ffn_w, w_ffn_in, conv_w, conv_b, w_down):
    depth = w_in.shape[0]
    yp, ys = x_prompt, x_sample
    sp = [[] for _ in range(6)]
    ss = [[] for _ in range(6)]
    for l in range(depth):
        w = _layer_weights(norm_mix_w[l], w_in[l], b_gates[l], q_norm_w[l], k_norm_w[l], sinks[l],
                           ml_norm_w[l], w_out[l], norm_ffn_w[l], w_ffn_in[l], conv_w[l],
                           conv_b[l], w_down[l])
        yp, *st_p = _prompt_layer(yp, w)
        ys, *st_s = _sample_layer(ys, cache_attn_k[l], cache_attn_v[l], state_mlstm_C[l],
                                  state_mlstm_n[l], state_mlstm_m[l], cache_ffn_conv[l], w)
        for i in range(6):
            sp[i].append(st_p[i])
            ss[i].append(st_s[i])
    k_p, v_p, c_p, n_p, m_p, conv_p = [jnp.stack(a) for a in sp]
    k_s, v_s, c_s, n_s, m_s, conv_s = [jnp.stack(a) for a in ss]
    return (yp, ys, k_p, v_p, c_p, n_p, m_p, conv_p, k_s, v_s, c_s, n_s, m_s, conv_s)
```

```python
import functools

import numpy as np
import jax
import jax.numpy as jnp
from jax import lax
from jax.experimental import pallas as pl
from jax.experimental.pallas import tpu as pltpu

F32 = jnp.float32
BF16 = jnp.bfloat16

D_MODEL = 1024
ATT_HEADS = 8
ATT_KV_HEADS = 2
ATT_HEAD_DIM = 64
ATT_GROUP = ATT_HEADS // ATT_KV_HEADS
WINDOW = 128
ML_HEADS = 4
ML_V_DIM = 128
ML_QK_DIM = 64
D_FF = 2816
CONV_W = 3
EPS = 1e-6
ATT_SCALE = ATT_HEAD_DIM ** -0.5
ML_SCALE = ML_QK_DIM ** -0.5
LOG2_E = 1.4426950408889634

ATT_Q_W = ATT_HEADS * ATT_HEAD_DIM
ATT_KV_W = ATT_KV_HEADS * ATT_HEAD_DIM
ML_QK_W = ML_HEADS * ML_QK_DIM
ML_V_W = ML_HEADS * ML_V_DIM
N_GATES = 2 * ML_HEADS
N_STACK = 2 * ATT_GROUP

LANES = 128
SUBLANES = 8

OFF_QA = 0
OFF_KV = OFF_QA + ATT_Q_W
OFF_QM = OFF_KV + 2 * ATT_KV_W
OFF_KM = OFF_QM + ML_QK_W
OFF_VM = OFF_KM + ML_QK_W
OFF_OM = OFF_VM + ML_V_W
OFF_GL = OFF_OM + ML_V_W
IN_WIDTH_PAD = OFF_GL + LANES

ATT_HEAD_ORDER = tuple(h for c in range(ATT_GROUP) for h in (c, c + ATT_GROUP))

ROW_TILE = 512
INPROJ_SUB = 256
MIX_TILE = 512
ML_CHUNK = 256
FF_CHUNK = 256
SAMPLE_BT = 16
VMEM_LIMIT = 56 * 1024 * 1024


def _dot(a, b):
    return jnp.dot(a, b, preferred_element_type=F32)


def _dot_nt(a, b):
    return lax.dot_general(a, b, (((1,), (1,)), ((), ())), preferred_element_type=F32)


def _split3(x):
    hi = x.astype(BF16)
    r1 = x - hi.astype(F32)
    mid = r1.astype(BF16)
    lo = (r1 - mid.astype(F32)).astype(BF16)
    return hi, mid, lo


def _rms(x, w):
    ms = jnp.mean(x * x, axis=-1, keepdims=True)
    return x * lax.rsqrt(ms + EPS) * w


def _log_sigmoid(x):
    return jnp.minimum(x, 0.0) - jnp.log1p(jnp.exp(-jnp.abs(x)))


def _sigmoid(x):
    return 1.0 / (1.0 + jnp.exp(-x))


def _permute_head_rows(dst_ref, src_ref):
    for k, h in enumerate(ATT_HEAD_ORDER):
        dst_ref[k * ATT_HEAD_DIM:(k + 1) * ATT_HEAD_DIM, :] = (
            src_ref[h * ATT_HEAD_DIM:(h + 1) * ATT_HEAD_DIM, :])


def _inproj_kernel(x_ref, nw_ref, w_ref, bg_ref, qnw_ref, knw_ref, gq_ref, gk_ref,
                   qa_ref, kv_ref, qm_ref, km_ref, vm_ref, om_ref, gt_ref, wq_scr):
    @pl.when(pl.program_id(0) == 0)
    def _():
        _permute_head_rows(wq_scr, w_ref)

    h = _rms(x_ref[...], nw_ref[...]).astype(BF16)

    def proj(lo, width):
        return _dot_nt(h, w_ref[lo:lo + width, :])

    q = _dot_nt(h, wq_scr[...])
    q_ms = _dot((q * q).astype(BF16), gq_ref[...])
    qa_ref[...] = (q * lax.rsqrt(q_ms + EPS) * qnw_ref[...]).astype(BF16)

    kv = proj(OFF_KV, 2 * ATT_KV_W)
    k = kv[:, :ATT_KV_W]
    k_ms = _dot((k * k).astype(BF16), gk_ref[...])
    kv_ref[:, :ATT_KV_W] = k * lax.rsqrt(k_ms + EPS) * knw_ref[...]
    kv_ref[:, ATT_KV_W:] = kv[:, ATT_KV_W:]

    qm_ref[...] = (proj(OFF_QM, ML_QK_W) * ML_SCALE).astype(BF16)
    km_ref[...] = proj(OFF_KM, ML_QK_W).astype(BF16)
    vm_ref[...] = proj(OFF_VM, ML_V_W).astype(BF16)
    om_ref[...] = proj(OFF_OM, ML_V_W).astype(BF16)

    gl = proj(OFF_GL, LANES) + bg_ref[...]
    lane = lax.broadcasted_iota(jnp.int32, gl.shape, 1)
    g = jnp.where(lane < ML_HEADS, gl, _log_sigmoid(gl))
    gt_ref[...] = g.T[:N_GATES, :]


def _inproj(x2d, nw, w_in_t, bg, qnw, knw, gq, gk):
    n = x2d.shape[0]
    tm = ROW_TILE
    row = lambda w: pl.BlockSpec((tm, w), lambda i: (i, 0))
    full = lambda a: pl.BlockSpec(a.shape, lambda i: (0,) * a.ndim)
    once = lambda a: pl.BlockSpec(a.shape, lambda i: (0,) * a.ndim, pipeline_mode=pl.Buffered(1))
    return pl.pallas_call(
        _inproj_kernel,
        grid=(n // tm,),
        in_specs=[row(D_MODEL), full(nw), once(w_in_t), full(bg), full(qnw), full(knw),
                  full(gq), full(gk)],
        out_specs=[row(ATT_Q_W), row(2 * ATT_KV_W), row(ML_QK_W), row(ML_QK_W),
                   row(ML_V_W), row(ML_V_W), pl.BlockSpec((N_GATES, tm), lambda i: (0, i))],
        out_shape=[jax.ShapeDtypeStruct((n, ATT_Q_W), BF16),
                   jax.ShapeDtypeStruct((n, 2 * ATT_KV_W), F32),
                   jax.ShapeDtypeStruct((n, ML_QK_W), BF16),
                   jax.ShapeDtypeStruct((n, ML_QK_W), BF16),
                   jax.ShapeDtypeStruct((n, ML_V_W), BF16),
                   jax.ShapeDtypeStruct((n, ML_V_W), BF16),
                   jax.ShapeDtypeStruct((N_GATES, n), F32)],
        scratch_shapes=[pltpu.VMEM((ATT_Q_W, D_MODEL), BF16)],
        compiler_params=pltpu.CompilerParams(dimension_semantics=("arbitrary",),
                                             vmem_limit_bytes=VMEM_LIMIT),
        name="inproj",
    )(x2d, nw, w_in_t, bg, qnw, knw, gq, gk)


def _head_norm_t(z, head_dim, w_col):
    rows, tokens = z.shape
    z3 = z.reshape(rows // head_dim, head_dim, tokens)
    ms = jnp.mean(z3 * z3, axis=1, keepdims=True)
    return (z3 * lax.rsqrt(ms + EPS)).reshape(rows, tokens) * w_col


def _inproj_t_kernel(x_ref, nw_ref, w_ref, bg_ref, qnw_ref, knw_ref,
                     qa_ref, ks_ref, kv_ref, qm_ref, km_ref, vm_ref, om_ref, gt_ref):
    tm = x_ref.shape[0]
    sub = INPROJ_SUB
    hs = [_rms(x_ref[c * sub:(c + 1) * sub, :], nw_ref[...]).astype(BF16)
          for c in range(tm // sub)]
    for c, h in enumerate(hs):
        tok = slice(c * sub, (c + 1) * sub)

        def proj(lo, width):
            return _dot_nt(w_ref[lo:lo + width, :], h)

        qa_ref[:, tok] = _head_norm_t(proj(OFF_QA, ATT_Q_W), ATT_HEAD_DIM,
                                      qnw_ref[...]).astype(BF16)
        kv = proj(OFF_KV, 2 * ATT_KV_W)
        k = _head_norm_t(kv[:ATT_KV_W], ATT_HEAD_DIM, knw_ref[...])
        kv_ref[:ATT_KV_W, tok] = k
        kv_ref[ATT_KV_W:, tok] = kv[ATT_KV_W:]
        ks_ref[tok, :] = k.T.astype(BF16)
        qm_ref[:, tok] = (proj(OFF_QM, ML_QK_W) * ML_SCALE).astype(BF16)
        km_ref[:, tok] = proj(OFF_KM, ML_QK_W).astype(BF16)
        vm_ref[:, tok] = proj(OFF_VM, ML_V_W).astype(BF16)
        om_ref[:, tok] = proj(OFF_OM, ML_V_W).astype(BF16)
        gl = proj(OFF_GL, 2 * SUBLANES)[:N_GATES] + bg_ref[...]
        row = lax.broadcasted_iota(jnp.int32, gl.shape, 0)
        gt_ref[:, tok] = jnp.where(row < ML_HEADS, gl, _log_sigmoid(gl))


def _inproj_t(x2d, nw, w_in_t, bg_col, qnw_col, knw_col):
    n = x2d.shape[0]
    tm = ROW_TILE
    full = lambda a: pl.BlockSpec(a.shape, lambda i: (0,) * a.ndim)
    once = lambda a: pl.BlockSpec(a.shape, lambda i: (0,) * a.ndim, pipeline_mode=pl.Buffered(1))
    col = lambda w: pl.BlockSpec((None, w, tm), lambda i: (i, 0, 0))
    slab = lambda w, dt: jax.ShapeDtypeStruct((n // tm, w, tm), dt)
    return pl.pallas_call(
        _inproj_t_kernel,
        grid=(n // tm,),
        in_specs=[pl.BlockSpec((tm, D_MODEL), lambda i: (i, 0)), full(nw), once(w_in_t),
                  full(bg_col), full(qnw_col), full(knw_col)],
        out_specs=[col(ATT_Q_W), pl.BlockSpec((tm, ATT_KV_W), lambda i: (i, 0)),
                   col(2 * ATT_KV_W), col(ML_QK_W), col(ML_QK_W), col(ML_V_W), col(ML_V_W),
                   col(N_GATES)],
        out_shape=[slab(ATT_Q_W, BF16),
                   jax.ShapeDtypeStruct((n, ATT_KV_W), BF16),
                   slab(2 * ATT_KV_W, F32), slab(ML_QK_W, BF16), slab(ML_QK_W, BF16),
                   slab(ML_V_W, BF16), slab(ML_V_W, BF16), slab(N_GATES, F32)],
        compiler_params=pltpu.CompilerParams(dimension_semantics=("arbitrary",),
                                             vmem_limit_bytes=VMEM_LIMIT),
        name="inproj_t",
    )(x2d, nw, w_in_t, bg_col, qnw_col, knw_col)


def _gate_forms(gates, seg_mask, want_raw_col):
    L = gates.shape[1]
    m_bf = seg_mask.astype(F32).astype(BF16)
    cum_row = jnp.zeros(gates.shape, F32)
    cum_col = jnp.zeros((L, gates.shape[0]), F32)
    raw_col = None
    if want_raw_col:
        r = lax.broadcasted_iota(jnp.int32, (L, L), 0)
        c = lax.broadcasted_iota(jnp.int32, (L, L), 1)
        eye = (r == c).astype(F32).astype(BF16)
        raw_col = jnp.zeros((L, gates.shape[0]), F32)
    for part in _split3(gates):
        cum_row = cum_row + _dot_nt(part, m_bf)
        cum_col = cum_col + _dot_nt(m_bf, part)
        if want_raw_col:
            raw_col = raw_col + _dot_nt(eye, part)
    return cum_row, cum_col, raw_col


def _mlstm_intra(q_pad, k_pair, v_ext, seg_mask, b_c, b_r, ig_r, m_prev_c):
    dm = jnp.where(seg_mask, b_c + (ig_r - b_r), -jnp.inf)
    inter = b_c + m_prev_c
    m_row = jnp.maximum(inter, jnp.max(dm, axis=-1, keepdims=True))
    w_inter = jnp.exp(inter - m_row)
    p = _dot_nt(q_pad, k_pair) * jnp.exp(dm - m_row)
    return _dot(p.astype(BF16), v_ext), m_row, w_inter


def _mlstm_out(pv, m_row, w_inter, q_c, q_n, mlnw_h, om_h):
    num = pv[:, :ML_V_DIM] + w_inter * q_c
    den = pv[:, ML_V_DIM:ML_V_DIM + 1] + w_inter * q_n
    hh = num / jnp.maximum(jnp.abs(den), jnp.exp(-m_row))
    return (_rms(hh, mlnw_h) * _sigmoid(om_h.astype(F32))).astype(BF16)


def _ones_col(rows):
    lane = lax.broadcasted_iota(jnp.int32, (rows, LANES), 1)
    return (lane == 0).astype(F32).astype(BF16)


def _prompt_mixer_t_kernel(sinks_ref, qa_ref, ksc_ref, ksp_ref, kvc_ref, kvp_ref, qm_ref, km_ref,
                           vm_ref, om_ref, gt_ref, x_ref, wout_ref, mlnw_ref,
                           x1_ref, ct_ref, nrow_ref, m_ref, kt_ref, vt_ref,
                           mix_scr, state_scr, m_scr, band_scr, causal_scr, tri_scr,
                           s_scr_a, s_scr_b, e_scr):
    i = pl.program_id(1)
    A = WINDOW
    L = MIX_TILE
    C = ML_CHUNK
    n_pairs = ML_HEADS // 2

    @pl.when(i == 0)
    def _():
        state_scr[...] = jnp.zeros(state_scr.shape, F32)
        m_scr[...] = jnp.zeros(m_scr.shape, F32)
        kj = lax.broadcasted_iota(jnp.int32, (2 * A, A), 0)
        qi = lax.broadcasted_iota(jnp.int32, (2 * A, A), 1)
        band = (kj > qi) & (kj <= qi + WINDOW)
        band_scr[0] = jnp.where(band, 0.0, -jnp.inf)
        band_scr[1] = jnp.where(band & (kj >= A), 0.0, -jnp.inf)
        r = lax.broadcasted_iota(jnp.int32, (C, C), 0)
        c = lax.broadcasted_iota(jnp.int32, (C, C), 1)
        causal_scr[...] = jnp.where(r <= c, 0.0, -jnp.inf)
        tri_scr[...] = (r <= c).astype(F32).astype(BF16)

    k_all = jnp.concatenate([ksp_ref[...], ksc_ref[...]], axis=0)
    v_all = jnp.concatenate([kvp_ref[ATT_KV_W:, :], kvc_ref[ATT_KV_W:, :]], axis=1).astype(BF16)
    zero_q = jnp.zeros((ATT_HEAD_DIM, A), BF16)
    slot = lax.rem(i, 2)
    s_bufs = (s_scr_a, s_scr_b)

    def stage_scores(j):
        pieces = []
        for h in range(ATT_HEADS):
            q_h = qa_ref[h * ATT_HEAD_DIM:(h + 1) * ATT_HEAD_DIM, j * A:(j + 1) * A]
            pieces.append(jnp.concatenate([q_h, zero_q] if h < ATT_GROUP else [zero_q, q_h],
                                          axis=0))
        s_bufs[j % 2][slot] = _dot(k_all[j * A:(j + 2) * A, :], jnp.concatenate(pieces, axis=1))

    stage_scores(0)
    for j in range(L // A):
        cols = slice(j * A, (j + 1) * A)
        vt = v_all[:, j * A:(j + 2) * A]
        if j + 1 < L // A:
            stage_scores(j + 1)
        s_buf = s_bufs[j % 2]
        bias = jnp.where(i > 0, band_scr[0], band_scr[1]) if j == 0 else band_scr[0]
        m_rows = []
        for h in range(ATT_HEADS):
            sb = s_buf[slot, :, h * A:(h + 1) * A] + bias
            m_rows.append(jnp.maximum(jnp.max(sb, axis=0, keepdims=True),
                                      sinks_ref[h] * LOG2_E))
        inv_rows = []
        for h in range(ATT_HEADS):
            e = jnp.exp2(s_buf[slot, :, h * A:(h + 1) * A] + (bias - m_rows[h]))
            e_scr[:, h * A:(h + 1) * A] = e.astype(BF16)
            inv_rows.append(1.0 / (jnp.sum(e, axis=0, keepdims=True)
                                   + jnp.exp2(sinks_ref[h] * LOG2_E - m_rows[h])))
        o = _dot(vt, e_scr[...])
        for h in range(ATT_HEADS):
            g = h // ATT_GROUP
            mix_scr[h * ATT_HEAD_DIM:(h + 1) * ATT_HEAD_DIM, cols] = (
                o[g * ATT_HEAD_DIM:(g + 1) * ATT_HEAD_DIM, h * A:(h + 1) * A]
                * inv_rows[h]).astype(BF16)

    row128 = lax.broadcasted_iota(jnp.int32, (LANES, C), 0)
    ones_rows = (row128 == 0).astype(F32).astype(BF16)
    for ci in range(L // C):
        tok = slice(ci * C, (ci + 1) * C)
        gates = gt_ref[:, tok] * LOG2_E
        cum_row = jnp.zeros(gates.shape, F32)
        for part in _split3(gates):
            cum_row = cum_row + _dot(part, tri_scr[...])
        ig_rows = gates[:ML_HEADS]
        b_rows = cum_row[ML_HEADS:]
        key_cols = jnp.concatenate([ig_rows - b_rows, jnp.zeros((LANES - ML_HEADS, C), F32)],
                                   axis=0).T
        for p in range(n_pairs):
            q_c = qm_ref[p * LANES:(p + 1) * LANES, tok]
            k_pair = km_ref[p * LANES:(p + 1) * LANES, tok]
            zero = jnp.zeros_like(q_c)
            state = state_scr[p]
            state_bf = state.astype(BF16)
            new_state = []
            for e_id in range(2):
                h = 2 * p + e_id
                v_rows = slice(h * ML_V_DIM, (h + 1) * ML_V_DIM)
                head_rows = (row128 < ML_QK_DIM) if e_id == 0 else (row128 >= ML_QK_DIM)
                q_pad = jnp.where(head_rows, q_c, zero)
                b_r = b_rows[h:h + 1, :]
                ig_r = ig_rows[h:h + 1, :]
                m_prev = m_scr[h:h + 1, 0:1]
                dm = (b_r + key_cols[:, h:h + 1]) + causal_scr[...]
                inter = b_r + m_prev
                m_row = jnp.maximum(inter, jnp.max(dm, axis=0, keepdims=True))
                w_inter = jnp.exp2(inter - m_row)
                qk = lax.dot_general(k_pair, q_pad, (((0,), (0,)), ((), ())),
                                     preferred_element_type=F32)
                p_t = (qk * jnp.exp2(dm - m_row)).astype(BF16)
                v_ext = jnp.concatenate([vm_ref[v_rows, tok], ones_rows], axis=0)
                num = _dot(v_ext, p_t) + w_inter * _dot(state_bf, q_pad)
                den = num[ML_V_DIM:ML_V_DIM + 1, :]
                hh = num[:ML_V_DIM] * (1.0 / jnp.maximum(jnp.abs(den), jnp.exp2(-m_row)))
                ms = jnp.mean(hh * hh, axis=0, keepdims=True)
                gate = _sigmoid(om_ref[v_rows, tok].astype(F32))
                mix_scr[ATT_Q_W + h * ML_V_DIM:ATT_Q_W + (h + 1) * ML_V_DIM, tok] = (
                    hh * lax.rsqrt(ms + EPS) * mlnw_ref[v_rows, :] * gate).astype(BF16)
                b_last = b_r[:, C - 1:C]
                a_r = b_last - b_r + ig_r
                m_new = jnp.maximum(b_last + m_prev, jnp.max(a_r, axis=-1, keepdims=True))
                sc = jnp.exp2(b_last + m_prev - m_new)
                wsv = (v_ext.astype(F32) * jnp.exp2(a_r - m_new)).astype(BF16)
                new_state.append(sc * state + _dot_nt(wsv, k_pair))
                m_scr[h:h + 1, :] = jnp.broadcast_to(m_new, (1, LANES))
            first = lax.broadcasted_iota(jnp.int32, state.shape, 1) < ML_QK_DIM
            state_scr[p] = jnp.where(first, new_state[0], new_state[1])

    x1_ref[...] = x_ref[...] + lax.dot_general(
        mix_scr[...], wout_ref[...], (((0,), (0,)), ((), ())), preferred_element_type=F32)

    @pl.when(i == pl.num_programs(1) - 1)
    def _():
        for p in range(n_pairs):
            c_t = state_scr[p, :ML_V_DIM, :].T
            for e_id in range(2):
                ct_ref[0, 2 * p + e_id] = c_t[e_id * ML_QK_DIM:(e_id + 1) * ML_QK_DIM, :]
            nrow_ref[0, p:p + 1, :] = state_scr[p, ML_V_DIM:ML_V_DIM + 1, :]
        for h in range(ML_HEADS):
            m_ref[0, :, h:h + 1] = m_scr[h:h + 1, 0:1] * (1.0 / LOG2_E)
        kt_ref[0] = kvc_ref[:ATT_KV_W, L - WINDOW:]
        vt_ref[0] = kvc_ref[ATT_KV_W:, L - WINDOW:]


def _prompt_mixer_t(batch, seq, sinks, qa, ks, kv, qm, km, vm, om, gt, x2d, wout, mlnw_col):
    tq = MIX_TILE
    nt = seq // tq
    sub = tq // WINDOW
    col = lambda w: pl.BlockSpec((None, w, tq), lambda b, i: (b * nt + i, 0, 0))
    full = lambda a: pl.BlockSpec(a.shape, lambda b, i: (0,) * a.ndim)
    once = lambda a: pl.BlockSpec(a.shape, lambda b, i: (0,) * a.ndim,
                                  pipeline_mode=pl.Buffered(1))
    prev_block = lambda b, i: jnp.maximum((b * nt + i) * sub - 1, 0)
    per_batch = lambda *dims: pl.BlockSpec((1,) + dims, lambda b, i: (b,) + (0,) * len(dims))
    return pl.pallas_call(
        _prompt_mixer_t_kernel,
        grid=(batch, nt),
        in_specs=[pl.BlockSpec(memory_space=pltpu.SMEM),
                  col(ATT_Q_W),
                  pl.BlockSpec((tq, ATT_KV_W), lambda b, i: (b * nt + i, 0)),
                  pl.BlockSpec((WINDOW, ATT_KV_W), lambda b, i: (prev_block(b, i), 0)),
                  col(2 * ATT_KV_W),
                  pl.BlockSpec((None, 2 * ATT_KV_W, WINDOW),
                               lambda b, i: (prev_block(b, i) // sub, 0, prev_block(b, i) % sub)),
                  col(ML_QK_W), col(ML_QK_W), col(ML_V_W), col(ML_V_W), col(N_GATES),
                  pl.BlockSpec((tq, D_MODEL), lambda b, i: (b * nt + i, 0)),
                  once(wout), full(mlnw_col)],
        out_specs=[pl.BlockSpec((tq, D_MODEL), lambda b, i: (b * nt + i, 0)),
                   per_batch(ML_HEADS, ML_QK_DIM, ML_V_DIM),
                   per_batch(ML_HEADS // 2, LANES),
                   per_batch(1, ML_HEADS),
                   per_batch(ATT_KV_W, WINDOW),
                   per_batch(ATT_KV_W, WINDOW)],
        out_shape=[jax.ShapeDtypeStruct((batch * seq, D_MODEL), F32),
                   jax.ShapeDtypeStruct((batch, ML_HEADS, ML_QK_DIM, ML_V_DIM), F32),
                   jax.ShapeDtypeStruct((batch, ML_HEADS // 2, LANES), F32),
                   jax.ShapeDtypeStruct((batch, 1, ML_HEADS), F32),
                   jax.ShapeDtypeStruct((batch, ATT_KV_W, WINDOW), F32),
                   jax.ShapeDtypeStruct((batch, ATT_KV_W, WINDOW), F32)],
        scratch_shapes=[pltpu.VMEM((D_MODEL, tq), BF16),
                        pltpu.VMEM((ML_HEADS // 2, 2 * LANES, LANES), F32),
                        pltpu.VMEM((SUBLANES, LANES), F32),
                        pltpu.VMEM((2, 2 * WINDOW, WINDOW), F32),
                        pltpu.VMEM((ML_CHUNK, ML_CHUNK), F32),
                        pltpu.VMEM((ML_CHUNK, ML_CHUNK), BF16),
                        pltpu.VMEM((2, 2 * WINDOW, ATT_HEADS * WINDOW), F32),
                        pltpu.VMEM((2, 2 * WINDOW, ATT_HEADS * WINDOW), F32),
                        pltpu.VMEM((2 * WINDOW, ATT_HEADS * WINDOW), BF16)],
        compiler_params=pltpu.CompilerParams(dimension_semantics=("arbitrary", "arbitrary"),
                                             vmem_limit_bytes=VMEM_LIMIT),
        name="prompt_mixer_t",
    )(sinks, qa, ks, ks, kv, kv, qm, km, vm, om, gt, x2d, wout, mlnw_col)


def _sample_mixer_kernel(t_len, sinks_ref, qa_ref, kv_ref, ck_ref, cv_ref, qm_ref, km_ref,
                         vm_ref, om_ref, gt_ref, c0_ref, n0_ref, m0_ref, x_ref, wout_ref,
                         mlnw_ref, x1_ref, nk_ref, nv_ref, c_ref, n_ref, m_ref,
                         mix_scr, wperm_scr):
    bt = SAMPLE_BT
    T = t_len
    L = bt * T

    @pl.when(pl.program_id(0) == 0)
    def _():
        _permute_head_rows(wperm_scr, wout_ref)
        wperm_scr[ATT_Q_W:, :] = wout_ref[ATT_Q_W:, :]

    lane3 = lax.broadcasted_iota(jnp.int32, (bt, T, LANES), 2)
    low3 = lane3 < ATT_HEAD_DIM
    lane = lax.broadcasted_iota(jnp.int32, (L, LANES), 1)
    low = lane < ATT_HEAD_DIM

    qa3 = qa_ref[...].astype(F32).reshape(bt, T, ATT_Q_W)
    pieces = []
    for col in range(ATT_GROUP):
        qc = qa3[:, :, col * LANES:(col + 1) * LANES]
        pieces += [jnp.where(low3, qc, 0.0), jnp.where(low3, 0.0, qc)]
    q3 = jnp.concatenate(pieces, axis=1).astype(BF16)
    R = bt * N_STACK * T
    q2 = q3.reshape(R, LANES)
    kv_new = kv_ref[...]
    k_new = kv_new[:, :ATT_KV_W]
    v_new = kv_new[:, ATT_KV_W:]
    ck = ck_ref[...]
    cv = cv_ref[...]
    s_c = jnp.einsum('bqd,bdk->bqk', q3, ck.astype(BF16),
                     preferred_element_type=F32).reshape(R, WINDOW)
    s_n = _dot_nt(q2, k_new.astype(BF16))
    row_c = lax.broadcasted_iota(jnp.int32, (R, WINDOW), 0)
    col_c = lax.broadcasted_iota(jnp.int32, (R, WINDOW), 1)
    s_c = jnp.where(col_c > row_c % T, s_c, -jnp.inf)
    row_n = lax.broadcasted_iota(jnp.int32, (R, L), 0)
    col_n = lax.broadcasted_iota(jnp.int32, (R, L), 1)
    valid_n = (row_n // (N_STACK * T) == col_n // T) & (col_n % T <= row_n % T)
    s_n = jnp.where(valid_n, s_n, -jnp.inf)
    stack_id = (lax.broadcasted_iota(jnp.int32, (R, 1), 0) // T) % N_STACK
    sink = jnp.zeros((R, 1), F32)
    for k_id in range(N_STACK):
        sink = jnp.where(stack_id == k_id, sinks_ref[ATT_HEAD_ORDER[k_id]], sink)
    m = jnp.maximum(jnp.maximum(jnp.max(s_c, axis=-1, keepdims=True),
                                jnp.max(s_n, axis=-1, keepdims=True)), sink)
    e_c = jnp.exp(s_c - m)
    e_n = jnp.exp(s_n - m)
    denom = (jnp.sum(e_c, axis=-1, keepdims=True) + jnp.sum(e_n, axis=-1, keepdims=True)
             + jnp.exp(sink - m))
    o = jnp.einsum('bqk,bdk->bqd', e_c.astype(BF16).reshape(bt, N_STACK * T, WINDOW),
                   cv.astype(BF16), preferred_element_type=F32).reshape(R, LANES)
    o = (o + _dot(e_n.astype(BF16), v_new.astype(BF16))) / denom
    o3 = o.reshape(bt, N_STACK * T, LANES)
    for col in range(ATT_GROUP):
        lo_h = o3[:, (2 * col) * T:(2 * col + 1) * T, :]
        hi_h = o3[:, (2 * col + 1) * T:(2 * col + 2) * T, :]
        mix_scr[:, col * LANES:(col + 1) * LANES] = jnp.where(
            low3, lo_h, hi_h).reshape(L, LANES).astype(BF16)

    keep = lax.broadcasted_iota(jnp.int32, (ATT_KV_W, WINDOW), 1) < WINDOW - T
    k_new_t = k_new.T
    v_new_t = v_new.T
    for q in range(bt):
        shift = (WINDOW - T - q * T) % WINDOW
        nk_ref[q] = jnp.where(keep, pltpu.roll(ck[q], WINDOW - T, axis=1),
                              pltpu.roll(k_new_t, shift, axis=1))
        nv_ref[q] = jnp.where(keep, pltpu.roll(cv[q], WINDOW - T, axis=1),
                              pltpu.roll(v_new_t, shift, axis=1))

    r = lax.broadcasted_iota(jnp.int32, (L, L), 0)
    c = lax.broadcasted_iota(jnp.int32, (L, L), 1)
    seg = (r // T == c // T) & (r <= c)
    seg_bias = jnp.where(seg, 0.0, -jnp.inf)
    seg_bf = seg.astype(F32).astype(BF16)
    gates = gt_ref[...] * LOG2_E
    cum_row = jnp.zeros(gates.shape, F32)
    for part in _split3(gates):
        cum_row = cum_row + _dot(part, seg_bf)
    ig_rows = gates[:ML_HEADS]
    b_rows = cum_row[ML_HEADS:]
    gate_cols = jnp.concatenate([ig_rows, b_rows, jnp.zeros((LANES - N_GATES, L), F32)],
                                axis=0).T

    def col_to_row(x_col):
        return jnp.broadcast_to(x_col, (L, LANES)).T[0:1, :]

    ones_rows = (r[:LANES] == 0).astype(F32).astype(BF16)
    qm = qm_ref[...]
    km = km_ref[...]
    qm_f = qm.astype(F32)
    km_f = km.astype(F32)
    n_rep = bt * ML_QK_DIM // LANES
    bd_row = lax.broadcasted_iota(jnp.int32, (L, bt * ML_QK_DIM), 0) // T
    bd_lane = lax.broadcasted_iota(jnp.int32, (L, bt * ML_QK_DIM), 1) // ML_QK_DIM
    block_diag = bd_row == bd_lane

    def spread(x_pair, e):
        other = pltpu.roll(x_pair, ML_QK_DIM, axis=1)
        twice = jnp.where(low, x_pair, other) if e == 0 else jnp.where(low, other, x_pair)
        return jnp.where(block_diag, jnp.concatenate([twice] * n_rep, axis=1), 0.0).astype(BF16)

    for h in range(ML_HEADS):
        p, e = divmod(h, 2)
        qc = qm[:, p * LANES:(p + 1) * LANES]
        k_pair = km[:, p * LANES:(p + 1) * LANES]
        zero = jnp.zeros_like(qc)
        q_pad = jnp.where(low, qc, zero) if e == 0 else jnp.where(low, zero, qc)
        v_h = vm_ref[:, h * ML_V_DIM:(h + 1) * ML_V_DIM]
        v_ext_t = jnp.concatenate([v_h.astype(F32).T.astype(BF16), ones_rows], axis=0)
        ig_c = gate_cols[:, h:h + 1]
        b_c = gate_cols[:, ML_HEADS + h:ML_HEADS + h + 1]
        b_r = b_rows[h:h + 1, :]
        m0 = m0_ref[:, :, h:h + 1] * LOG2_E
        inter = b_r + col_to_row(jnp.broadcast_to(m0, (bt, T, 1)).reshape(L, 1))
        dm = (b_r + (ig_c - b_c)) + seg_bias
        m_row = jnp.maximum(inter, jnp.max(dm, axis=0, keepdims=True))
        w_inter = jnp.exp2(inter - m_row)
        p_t = (_dot_nt(k_pair, q_pad) * jnp.exp2(dm - m_row)).astype(BF16)
        num_t = _dot(v_ext_t, p_t)
        q_h3 = qm_f[:, h * ML_QK_DIM:(h + 1) * ML_QK_DIM].reshape(bt, T, ML_QK_DIM)
        k_h3 = km_f[:, h * ML_QK_DIM:(h + 1) * ML_QK_DIM].reshape(bt, T, ML_QK_DIM)
        c0 = c0_ref[:, h]
        n0 = n0_ref[:, h:h + 1, :]
        q_c_t = _dot(spread(qm_f[:, p * LANES:(p + 1) * LANES], e),
                     c0.astype(BF16).reshape(bt * ML_QK_DIM, ML_V_DIM)).T
        q_n_r = col_to_row(jnp.sum(q_h3 * n0, axis=-1, keepdims=True).reshape(L, 1))
        num = num_t[:ML_V_DIM] + w_inter * q_c_t
        den = num_t[ML_V_DIM:ML_V_DIM + 1] + w_inter * q_n_r
        hh = num * (1.0 / jnp.maximum(jnp.abs(den), jnp.exp2(-m_row)))
        ms = jnp.mean(hh * hh, axis=0, keepdims=True)
        mix_scr[:, ATT_Q_W + h * ML_V_DIM:ATT_Q_W + (h + 1) * ML_V_DIM] = (
            (hh * lax.rsqrt(ms + EPS)).T * mlnw_ref[:, h * ML_V_DIM:(h + 1) * ML_V_DIM]
            * _sigmoid(om_ref[:, h * ML_V_DIM:(h + 1) * ML_V_DIM].astype(F32))).astype(BF16)
        b3 = b_c.reshape(bt, T, 1)
        b_last = b3[:, T - 1:T, :]
        a3 = b_last - b3 + ig_c.reshape(bt, T, 1)
        m_new = jnp.maximum(b_last + m0, jnp.max(a3, axis=1, keepdims=True))
        sc = jnp.exp2(b_last + m0 - m_new)
        ws = jnp.exp2(a3 - m_new)
        kw = spread(km_f[:, p * LANES:(p + 1) * LANES] * ws.reshape(L, 1), e)
        d_c = lax.dot_general(kw, v_h, (((0,), (0,)), ((), ())), preferred_element_type=F32)
        c_ref[:, h] = sc * c0 + d_c.reshape(bt, ML_QK_DIM, ML_V_DIM)
        n_ref[:, h:h + 1, :] = sc * n0 + jnp.sum(ws * k_h3, axis=1, keepdims=True)
        m_ref[:, :, h:h + 1] = m_new * (1.0 / LOG2_E)

    x1_ref[...] = x_ref[...] + _dot(mix_scr[...], wperm_scr[...])


def _sample_mixer(nb, t_len, sinks, qa, kv, ck, cv, qm, km, vm, om, gt, c0, n0, m0, x2d, wout, mlnw):
    bt = SAMPLE_BT
    tl = bt * t_len
    row = lambda w: pl.BlockSpec((tl, w), lambda i: (i, 0))
    full = lambda a: pl.BlockSpec(a.shape, lambda i: (0,) * a.ndim)
    once = lambda a: pl.BlockSpec(a.shape, lambda i: (0,) * a.ndim, pipeline_mode=pl.Buffered(1))
    cache = pl.BlockSpec((bt, ATT_KV_W, WINDOW), lambda i: (i, 0, 0))
    c_spec = pl.BlockSpec((bt, ML_HEADS, ML_QK_DIM, ML_V_DIM), lambda i: (i, 0, 0, 0))
    n_spec = pl.BlockSpec((bt, ML_HEADS, ML_QK_DIM), lambda i: (i, 0, 0))
    m_spec = pl.BlockSpec((bt, 1, ML_HEADS), lambda i: (i, 0, 0))
    return pl.pallas_call(
        functools.partial(_sample_mixer_kernel, t_len),
        grid=(nb // bt,),
        in_specs=[pl.BlockSpec(memory_space=pltpu.SMEM),
                  row(ATT_Q_W), row(2 * ATT_KV_W), cache, cache, row(ML_QK_W), row(ML_QK_W),
                  row(ML_V_W), row(ML_V_W), pl.BlockSpec((N_GATES, tl), lambda i: (0, i)),
                  c_spec, n_spec, m_spec, row(D_MODEL), once(wout), full(mlnw)],
        out_specs=[row(D_MODEL), cache, cache, c_spec, n_spec, m_spec],
        out_shape=[jax.ShapeDtypeStruct((nb * t_len, D_MODEL), F32),
                   jax.ShapeDtypeStruct((nb, ATT_KV_W, WINDOW), F32),
                   jax.ShapeDtypeStruct((nb, ATT_KV_W, WINDOW), F32),
                   jax.ShapeDtypeStruct((nb, ML_HEADS, ML_QK_DIM, ML_V_DIM), F32),
                   jax.ShapeDtypeStruct((nb, ML_HEADS, ML_QK_DIM), F32),
                   jax.ShapeDtypeStruct((nb, 1, ML_HEADS), F32)],
        scratch_shapes=[pltpu.VMEM((tl, D_MODEL), BF16),
                        pltpu.VMEM((D_MODEL, D_MODEL), BF16)],
        compiler_params=pltpu.CompilerParams(dimension_semantics=("arbitrary",),
                                             vmem_limit_bytes=VMEM_LIMIT),
        name="sample_mixer",
    )(sinks, qa, kv, ck, cv, qm, km, vm, om, gt, c0, n0, m0, x2d, wout, mlnw)


def _ffn_kernel(seq_rows, *refs):
    if seq_rows is None:
        (x_ref, nw_ref, w_ref, cw_ref, cb_ref, wd_ref, y_ref, conv_ref,
         gbuf, act_scr, carry) = refs
        hist_ref = None
    else:
        (x_ref, hist_ref, nw_ref, w_ref, cw_ref, cb_ref, wd_ref, y_ref, conv_ref,
         gbuf, act_scr) = refs
        carry = None
    tm = x_ref.shape[0]
    tf = FF_CHUNK
    n_hist = CONV_W - 1
    rows = tm if seq_rows is None else seq_rows
    nseq = tm // rows
    base = SUBLANES
    n_chunks = D_FF // tf

    if carry is not None:
        @pl.when(pl.program_id(1) == 0)
        def _():
            carry[...] = jnp.zeros(carry.shape, F32)

    x = x_ref[...]
    h2 = _rms(x, nw_ref[...]).astype(BF16)

    def proj(f):
        return (_dot(h2, w_ref[:, f * tf:(f + 1) * tf]),
                _dot(h2, w_ref[:, D_FF + f * tf:D_FF + (f + 1) * tf]))

    nxt = proj(0)
    for f in range(n_chunks):
        g, u = nxt
        if f + 1 < n_chunks:
            nxt = proj(f + 1)
        cols = slice(f * tf, (f + 1) * tf)
        s = f % 2
        g3 = g.reshape(nseq, rows, tf)
        if seq_rows is None:
            gbuf[s, :, base - n_hist:base, :] = carry[:, SUBLANES - n_hist:, cols]
            carry[:, SUBLANES - n_hist:, cols] = g3[:, rows - n_hist:, :]
        else:
            gbuf[s, :, base - n_hist:base, :] = hist_ref[:, :, cols]
            conv_ref[:, :, cols] = g3[:, rows - n_hist:, :]
        gbuf[s, :, base:base + rows, :] = g3
        gc = cb_ref[:, cols] + g * cw_ref[CONV_W - 1:CONV_W, cols]
        for d in range(1, CONV_W):
            gm = gbuf[s, :, base - d:base - d + rows, :].reshape(tm, tf)
            gc = gc + gm * cw_ref[CONV_W - 1 - d:CONV_W - d, cols]
        act_scr[:, cols] = (gc * _sigmoid(gc) * u).astype(BF16)
    y_ref[...] = x + _dot(act_scr[...], wd_ref[...])

    if carry is not None:
        @pl.when(pl.program_id(1) == pl.num_programs(1) - 1)
        def _():
            conv_ref[...] = carry[:, SUBLANES - n_hist:, :]


def _ffn_scratch(tm, rows):
    return [pltpu.VMEM((2, tm // rows, SUBLANES + rows, FF_CHUNK), F32),
            pltpu.VMEM((tm, D_FF), BF16)]


def _ffn_prompt(batch, seq, x2d, nw, w, cw, cb, wd):
    tm = ROW_TILE
    nt = seq // tm
    full = lambda a: pl.BlockSpec(a.shape, lambda b, i: (0,) * a.ndim)
    once = lambda a: pl.BlockSpec(a.shape, lambda b, i: (0,) * a.ndim,
                                  pipeline_mode=pl.Buffered(1))
    row = pl.BlockSpec((tm, D_MODEL), lambda b, i: (b * nt + i, 0))
    return pl.pallas_call(
        functools.partial(_ffn_kernel, None),
        grid=(batch, nt),
        in_specs=[row, full(nw), once(w), full(cw), full(cb), once(wd)],
        out_specs=[row, pl.BlockSpec((1, CONV_W - 1, D_FF), lambda b, i: (b, 0, 0))],
        out_shape=[jax.ShapeDtypeStruct((batch * seq, D_MODEL), F32),
                   jax.ShapeDtypeStruct((batch, CONV_W - 1, D_FF), F32)],
        scratch_shapes=_ffn_scratch(tm, tm) + [pltpu.VMEM((1, SUBLANES, D_FF), F32)],
        compiler_params=pltpu.CompilerParams(dimension_semantics=("arbitrary", "arbitrary"),
                                             vmem_limit_bytes=VMEM_LIMIT),
        name="ffn_prompt",
    )(x2d, nw, w, cw, cb, wd)


def _ffn_sample(nb, t_len, x2d, hist, nw, w, cw, cb, wd):
    tm = ROW_TILE
    bt = tm // t_len
    full = lambda a: pl.BlockSpec(a.shape, lambda i: (0,) * a.ndim)
    once = lambda a: pl.BlockSpec(a.shape, lambda i: (0,) * a.ndim, pipeline_mode=pl.Buffered(1))
    row = pl.BlockSpec((tm, D_MODEL), lambda i: (i, 0))
    hist_spec = pl.BlockSpec((bt, CONV_W - 1, D_FF), lambda i: (i, 0, 0))
    return pl.pallas_call(
        functools.partial(_ffn_kernel, t_len),
        grid=(nb // bt,),
        in_specs=[row, hist_spec, full(nw), once(w), full(cw), full(cb), once(wd)],
        out_specs=[row, hist_spec],
        out_shape=[jax.ShapeDtypeStruct((nb * t_len, D_MODEL), F32),
                   jax.ShapeDtypeStruct((nb, CONV_W - 1, D_FF), F32)],
        scratch_shapes=_ffn_scratch(tm, t_len),
        compiler_params=pltpu.CompilerParams(dimension_semantics=("arbitrary",),
                                             vmem_limit_bytes=VMEM_LIMIT),
        name="ffn_sample",
    )(x2d, hist, nw, w, cw, cb, wd)


def _head_mean_matrix(width, head_dim):
    idx = np.arange(width) // head_dim
    return jnp.asarray((idx[:, None] == idx[None, :]).astype(np.float32) / head_dim, dtype=BF16)


def _layer_weights(norm_mix_w, w_in, b_gates, q_norm_w, k_norm_w, sinks, ml_norm_w, w_out,
                   norm_ffn_w, w_ffn_in, conv_w, conv_b, w_down):
    w_in_t = jnp.pad(w_in.T.astype(BF16), ((0, IN_WIDTH_PAD - w_in.shape[1]), (0, 0)))
    return dict(
        nw=norm_mix_w.reshape(1, D_MODEL),
        w_in_t=w_in_t,
        bg=jnp.pad(b_gates, (0, LANES - N_GATES)).reshape(1, LANES),
        qnw=(jnp.tile(q_norm_w, ATT_HEADS) * ATT_SCALE).reshape(1, ATT_Q_W),
        knw=jnp.tile(k_norm_w, ATT_KV_HEADS).reshape(1, ATT_KV_W),
        gq=_head_mean_matrix(ATT_Q_W, ATT_HEAD_DIM),
        gk=_head_mean_matrix(ATT_KV_W, ATT_HEAD_DIM),
        bg_col=b_gates.reshape(N_GATES, 1),
        qnw_col=(jnp.tile(q_norm_w, ATT_HEADS) * (ATT_SCALE * LOG2_E)).reshape(ATT_Q_W, 1),
        knw_col=jnp.tile(k_norm_w, ATT_KV_HEADS).reshape(ATT_KV_W, 1),
        mlnw_col=ml_norm_w.reshape(ML_V_W, 1),
        sinks=sinks,
        mlnw=ml_norm_w.reshape(1, ML_V_W),
        wout=w_out.astype(BF16),
        nfw=norm_ffn_w.reshape(1, D_MODEL),
        wff=w_ffn_in.astype(BF16),
        cw=conv_w,
        cb=conv_b.reshape(1, D_FF),
        wd=w_down.astype(BF16),
    )


def _cache_from_t(a_t):
    n = a_t.shape[0]
    return jnp.transpose(a_t.reshape(n, ATT_KV_HEADS, ATT_HEAD_DIM, WINDOW), (0, 3, 1, 2))


def _cache_to_t(a):
    n = a.shape[0]
    return jnp.transpose(a, (0, 2, 3, 1)).reshape(n, ATT_KV_W, WINDOW)


def _prompt_layer(x, w):
    batch, seq, _ = x.shape
    assert seq % ROW_TILE == 0 and MIX_TILE == ROW_TILE and MIX_TILE % WINDOW == 0
    x2d = x.reshape(batch * seq, D_MODEL)
    qa, ks, kv, qm, km, vm, om, gt = _inproj_t(x2d, w["nw"], w["w_in_t"], w["bg_col"],
                                               w["qnw_col"], w["knw_col"])
    x1, c_t, n_row, m, k_t, v_t = _prompt_mixer_t(batch, seq, w["sinks"], qa, ks, kv, qm, km, vm,
                                                  om, gt, x2d, w["wout"], w["mlnw_col"])
    y, conv = _ffn_prompt(batch, seq, x1, w["nfw"], w["wff"], w["cw"], w["cb"], w["wd"])
    return (y.reshape(batch, seq, D_MODEL), _cache_from_t(k_t), _cache_from_t(v_t),
            jnp.swapaxes(c_t, -1, -2), n_row.reshape(batch, ML_HEADS, ML_QK_DIM),
            m.reshape(batch, ML_HEADS), conv)


def _sample_layer(x, ck, cv, c0, n0, m0, conv_buf, w):
    nb, t_len, _ = x.shape
    assert t_len == SUBLANES and SAMPLE_BT * t_len == LANES and nb % SAMPLE_BT == 0
    assert (nb * t_len) % ROW_TILE == 0
    x2d = x.reshape(nb * t_len, D_MODEL)
    qa, kv, qm, km, vm, om, gt = _inproj(x2d, w["nw"], w["w_in_t"], w["bg"], w["qnw"], w["knw"],
                                         w["gq"], w["gk"])
    x1, nk_t, nv_t, c_t, n, m = _sample_mixer(
        nb, t_len, w["sinks"], qa, kv, _cache_to_t(ck), _cache_to_t(cv), qm, km, vm, om, gt,
        jnp.swapaxes(c0, -1, -2), n0, m0.reshape(nb, 1, ML_HEADS), x2d, w["wout"], w["mlnw"])
    y, conv = _ffn_sample(nb, t_len, x1, conv_buf, w["nfw"], w["wff"], w["cw"], w["cb"],
                          w["wd"])
    return (y.reshape(nb, t_len, D_MODEL), _cache_from_t(nk_t), _cache_from_t(nv_t),
            jnp.swapaxes(c_t, -1, -2), n, m.reshape(nb, ML_HEADS), conv)


def kernel(x_prompt, x_sample, cache_attn_k, cache_attn_v, state_mlstm_C, state_mlstm_n,
           state_mlstm_m, cache_ffn_conv, norm_mix_w, w_in, b_gates, q_norm_w, k_norm_w,
           sinks, ml_norm_w, w_out, norm_ffn_w, w_ffn_in, conv_w, conv_b, w_down):
    depth = w_in.shape[0]
    yp, ys = x_prompt, x_sample
    sp = [[] for _ in range(6)]
    ss = [[] for _ in range(6)]
    for l in range(depth):
        w = _layer_weights(norm_mix_w[l], w_in[l], b_gates[l], q_norm_w[l], k_norm_w[l], sinks[l],
                           ml_norm_w[l], w_out[l], norm_ffn_w[l], w_ffn_in[l], conv_w[l],
                           conv_b[l], w_down[l])
        yp, *st_p = _prompt_layer(yp, w)
        ys, *st_s = _sample_layer(ys, cache_attn_k[l], cache_attn_v[l], state_mlstm_C[l],
                                  state_mlstm_n[l], state_mlstm_m[l], cache_ffn_conv[l], w)
        for i in range(6):
            sp[i].append(st_p[i])
            ss[i].append(st_s[i])
    k_p, v_p, c_p, n_p, m_p, conv_p = [jnp.stack(a) for a in sp]
    k_s, v_s, c_s, n_s, m_s, conv_s = [jnp.stack(a) for a in ss]
    return (yp, ys, k_p, v_p, c_p, n_p, m_p, conv_p, k_s, v_s, c_s, n_s, m_s, conv_s)
```

```python
import functools

import numpy as np
import jax
import jax.numpy as jnp
from jax import lax
from jax.experimental import pallas as pl
from jax.experimental.pallas import tpu as pltpu

F32 = jnp.float32
BF16 = jnp.bfloat16

D_MODEL = 1024
ATT_HEADS = 8
ATT_KV_HEADS = 2
ATT_HEAD_DIM = 64
ATT_GROUP = ATT_HEADS // ATT_KV_HEADS
WINDOW = 128
ML_HEADS = 4
ML_V_DIM = 128
ML_QK_DIM = 64
D_FF = 2816
CONV_W = 3
EPS = 1e-6
ATT_SCALE = ATT_HEAD_DIM ** -0.5
ML_SCALE = ML_QK_DIM ** -0.5
LOG2_E = 1.4426950408889634

ATT_Q_W = ATT_HEADS * ATT_HEAD_DIM
ATT_KV_W = ATT_KV_HEADS * ATT_HEAD_DIM
ML_QK_W = ML_HEADS * ML_QK_DIM
ML_V_W = ML_HEADS * ML_V_DIM
N_GATES = 2 * ML_HEADS
N_STACK = 2 * ATT_GROUP

LANES = 128
SUBLANES = 8

OFF_QA = 0
OFF_KV = OFF_QA + ATT_Q_W
OFF_QM = OFF_KV + 2 * ATT_KV_W
OFF_KM = OFF_QM + ML_QK_W
OFF_VM = OFF_KM + ML_QK_W
OFF_OM = OFF_VM + ML_V_W
OFF_GL = OFF_OM + ML_V_W
IN_WIDTH_PAD = OFF_GL + LANES

ATT_HEAD_ORDER = tuple(h for c in range(ATT_GROUP) for h in (c, c + ATT_GROUP))

ROW_TILE = 512
INPROJ_SUB = 256
MIX_TILE = 512
ML_CHUNK = 256
FF_CHUNK = 256
SAMPLE_BT = 16
VMEM_LIMIT = 56 * 1024 * 1024


def _dot(a, b):
    return jnp.dot(a, b, preferred_element_type=F32)


def _dot_nt(a, b):
    return lax.dot_general(a, b, (((1,), (1,)), ((), ())), preferred_element_type=F32)


def _split3(x):
    hi = x.astype(BF16)
    r1 = x - hi.astype(F32)
    mid = r1.astype(BF16)
    lo = (r1 - mid.astype(F32)).astype(BF16)
    return hi, mid, lo


def _rms(x, w):
    ms = jnp.mean(x * x, axis=-1, keepdims=True)
    return x * lax.rsqrt(ms + EPS) * w


def _log_sigmoid(x):
    return jnp.minimum(x, 0.0) - jnp.log1p(jnp.exp(-jnp.abs(x)))


def _sigmoid(x):
    return 1.0 / (1.0 + jnp.exp(-x))


def _permute_head_rows(dst_ref, src_ref):
    for k, h in enumerate(ATT_HEAD_ORDER):
        dst_ref[k * ATT_HEAD_DIM:(k + 1) * ATT_HEAD_DIM, :] = (
            src_ref[h * ATT_HEAD_DIM:(h + 1) * ATT_HEAD_DIM, :])


def _inproj_kernel(x_ref, nw_ref, w_ref, bg_ref, qnw_ref, knw_ref, gq_ref, gk_ref,
                   qa_ref, kv_ref, qm_ref, km_ref, vm_ref, om_ref, gt_ref, wq_scr):
    @pl.when(pl.program_id(0) == 0)
    def _():
        _permute_head_rows(wq_scr, w_ref)

    h = _rms(x_ref[...], nw_ref[...]).astype(BF16)

    def proj(lo, width):
        return _dot_nt(h, w_ref[lo:lo + width, :])

    q = _dot_nt(h, wq_scr[...])
    q_ms = _dot((q * q).astype(BF16), gq_ref[...])
    qa_ref[...] = (q * lax.rsqrt(q_ms + EPS) * qnw_ref[...]).astype(BF16)

    kv = proj(OFF_KV, 2 * ATT_KV_W)
    k = kv[:, :ATT_KV_W]
    k_ms = _dot((k * k).astype(BF16), gk_ref[...])
    kv_ref[:, :ATT_KV_W] = k * lax.rsqrt(k_ms + EPS) * knw_ref[...]
    kv_ref[:, ATT_KV_W:] = kv[:, ATT_KV_W:]

    qm_ref[...] = (proj(OFF_QM, ML_QK_W) * ML_SCALE).astype(BF16)
    km_ref[...] = proj(OFF_KM, ML_QK_W).astype(BF16)
    vm_ref[...] = proj(OFF_VM, ML_V_W).astype(BF16)
    om_ref[...] = proj(OFF_OM, ML_V_W).astype(BF16)

    gl = proj(OFF_GL, LANES) + bg_ref[...]
    lane = lax.broadcasted_iota(jnp.int32, gl.shape, 1)
    g = jnp.where(lane < ML_HEADS, gl, _log_sigmoid(gl))
    gt_ref[...] = g.T[:N_GATES, :]


def _inproj(x2d, nw, w_in_t, bg, qnw, knw, gq, gk):
    n = x2d.shape[0]
    tm = ROW_TILE
    row = lambda w: pl.BlockSpec((tm, w), lambda i: (i, 0))
    full = lambda a: pl.BlockSpec(a.shape, lambda i: (0,) * a.ndim)
    once = lambda a: pl.BlockSpec(a.shape, lambda i: (0,) * a.ndim, pipeline_mode=pl.Buffered(1))
    return pl.pallas_call(
        _inproj_kernel,
        grid=(n // tm,),
        in_specs=[row(D_MODEL), full(nw), once(w_in_t), full(bg), full(qnw), full(knw),
                  full(gq), full(gk)],
        out_specs=[row(ATT_Q_W), row(2 * ATT_KV_W), row(ML_QK_W), row(ML_QK_W),
                   row(ML_V_W), row(ML_V_W), pl.BlockSpec((N_GATES, tm), lambda i: (0, i))],
        out_shape=[jax.ShapeDtypeStruct((n, ATT_Q_W), BF16),
                   jax.ShapeDtypeStruct((n, 2 * ATT_KV_W), F32),
                   jax.ShapeDtypeStruct((n, ML_QK_W), BF16),
                   jax.ShapeDtypeStruct((n, ML_QK_W), BF16),
                   jax.ShapeDtypeStruct((n, ML_V_W), BF16),
                   jax.ShapeDtypeStruct((n, ML_V_W), BF16),
                   jax.ShapeDtypeStruct((N_GATES, n), F32)],
        scratch_shapes=[pltpu.VMEM((ATT_Q_W, D_MODEL), BF16)],
        compiler_params=pltpu.CompilerParams(dimension_semantics=("arbitrary",),
                                             vmem_limit_bytes=VMEM_LIMIT),
        name="inproj",
    )(x2d, nw, w_in_t, bg, qnw, knw, gq, gk)


def _head_norm_t(z, head_dim, w_col):
    rows, tokens = z.shape
    z3 = z.reshape(rows // head_dim, head_dim, tokens)
    ms = jnp.mean(z3 * z3, axis=1, keepdims=True)
    return (z3 * lax.rsqrt(ms + EPS)).reshape(rows, tokens) * w_col


def _inproj_t_kernel(x_ref, nw_ref, w_ref, bg_ref, qnw_ref, knw_ref,
                     qa_ref, ks_ref, kv_ref, qm_ref, km_ref, vm_ref, om_ref, gt_ref):
    tm = x_ref.shape[0]
    sub = INPROJ_SUB
    hs = [_rms(x_ref[c * sub:(c + 1) * sub, :], nw_ref[...]).astype(BF16)
          for c in range(tm // sub)]
    for c, h in enumerate(hs):
        tok = slice(c * sub, (c + 1) * sub)

        def proj(lo, width):
            return _dot_nt(w_ref[lo:lo + width, :], h)

        qa_ref[:, tok] = _head_norm_t(proj(OFF_QA, ATT_Q_W), ATT_HEAD_DIM,
                                      qnw_ref[...]).astype(BF16)
        kv = proj(OFF_KV, 2 * ATT_KV_W)
        k = _head_norm_t(kv[:ATT_KV_W], ATT_HEAD_DIM, knw_ref[...])
        kv_ref[:ATT_KV_W, tok] = k
        kv_ref[ATT_KV_W:, tok] = kv[ATT_KV_W:]
        ks_ref[tok, :] = k.T.astype(BF16)
        qm_ref[:, tok] = (proj(OFF_QM, ML_QK_W) * ML_SCALE).astype(BF16)
        km_ref[:, tok] = proj(OFF_KM, ML_QK_W).astype(BF16)
        vm_ref[:, tok] = proj(OFF_VM, ML_V_W).astype(BF16)
        om_ref[:, tok] = proj(OFF_OM, ML_V_W).astype(BF16)
        gl = proj(OFF_GL, 2 * SUBLANES)[:N_GATES] + bg_ref[...]
        row = lax.broadcasted_iota(jnp.int32, gl.shape, 0)
        gt_ref[:, tok] = jnp.where(row < ML_HEADS, gl, _log_sigmoid(gl))


def _inproj_t(x2d, nw, w_in_t, bg_col, qnw_col, knw_col):
    n = x2d.shape[0]
    tm = ROW_TILE
    full = lambda a: pl.BlockSpec(a.shape, lambda i: (0,) * a.ndim)
    once = lambda a: pl.BlockSpec(a.shape, lambda i: (0,) * a.ndim, pipeline_mode=pl.Buffered(1))
    col = lambda w: pl.BlockSpec((None, w, tm), lambda i: (i, 0, 0))
    slab = lambda w, dt: jax.ShapeDtypeStruct((n // tm, w, tm), dt)
    return pl.pallas_call(
        _inproj_t_kernel,
        grid=(n // tm,),
        in_specs=[pl.BlockSpec((tm, D_MODEL), lambda i: (i, 0)), full(nw), once(w_in_t),
                  full(bg_col), full(qnw_col), full(knw_col)],
        out_specs=[col(ATT_Q_W), pl.BlockSpec((tm, ATT_KV_W), lambda i: (i, 0)),
                   col(2 * ATT_KV_W), col(ML_QK_W), col(ML_QK_W), col(ML_V_W), col(ML_V_W),
                   col(N_GATES)],
        out_shape=[slab(ATT_Q_W, BF16),
                   jax.ShapeDtypeStruct((n, ATT_KV_W), BF16),
                   slab(2 * ATT_KV_W, F32), slab(ML_QK_W, BF16), slab(ML_QK_W, BF16),
                   slab(ML_V_W, BF16), slab(ML_V_W, BF16), slab(N_GATES, F32)],
        compiler_params=pltpu.CompilerParams(dimension_semantics=("arbitrary",),
                                             vmem_limit_bytes=VMEM_LIMIT),
        name="inproj_t",
    )(x2d, nw, w_in_t, bg_col, qnw_col, knw_col)


def _gate_forms(gates, seg_mask, want_raw_col):
    L = gates.shape[1]
    m_bf = seg_mask.astype(F32).astype(BF16)
    cum_row = jnp.zeros(gates.shape, F32)
    cum_col = jnp.zeros((L, gates.shape[0]), F32)
    raw_col = None
    if want_raw_col:
        r = lax.broadcasted_iota(jnp.int32, (L, L), 0)
        c = lax.broadcasted_iota(jnp.int32, (L, L), 1)
        eye = (r == c).astype(F32).astype(BF16)
        raw_col = jnp.zeros((L, gates.shape[0]), F32)
    for part in _split3(gates):
        cum_row = cum_row + _dot_nt(part, m_bf)
        cum_col = cum_col + _dot_nt(m_bf, part)
        if want_raw_col:
            raw_col = raw_col + _dot_nt(eye, part)
    return cum_row, cum_col, raw_col


def _mlstm_intra(q_pad, k_pair, v_ext, seg_mask, b_c, b_r, ig_r, m_prev_c):
    dm = jnp.where(seg_mask, b_c + (ig_r - b_r), -jnp.inf)
    inter = b_c + m_prev_c
    m_row = jnp.maximum(inter, jnp.max(dm, axis=-1, keepdims=True))
    w_inter = jnp.exp(inter - m_row)
    p = _dot_nt(q_pad, k_pair) * jnp.exp(dm - m_row)
    return _dot(p.astype(BF16), v_ext), m_row, w_inter


def _mlstm_out(pv, m_row, w_inter, q_c, q_n, mlnw_h, om_h):
    num = pv[:, :ML_V_DIM] + w_inter * q_c
    den = pv[:, ML_V_DIM:ML_V_DIM + 1] + w_inter * q_n
    hh = num / jnp.maximum(jnp.abs(den), jnp.exp(-m_row))
    return (_rms(hh, mlnw_h) * _sigmoid(om_h.astype(F32))).astype(BF16)


def _ones_col(rows):
    lane = lax.broadcasted_iota(jnp.int32, (rows, LANES), 1)
    return (lane == 0).astype(F32).astype(BF16)


def _prompt_mixer_t_kernel(sinks_ref, qa_ref, ksc_ref, ksp_ref, kvc_ref, kvp_ref, qm_ref, km_ref,
                           vm_ref, om_ref, gt_ref, x_ref, wout_ref, mlnw_ref,
                           x1_ref, ct_ref, nrow_ref, m_ref, kt_ref, vt_ref,
                           mix_scr, state_scr, m_scr, band_scr, causal_scr, tri_scr,
                           s_scr_a, s_scr_b, e_scr):
    i = pl.program_id(1)
    A = WINDOW
    L = MIX_TILE
    C = ML_CHUNK
    n_pairs = ML_HEADS // 2

    @pl.when(i == 0)
    def _():
        state_scr[...] = jnp.zeros(state_scr.shape, F32)
        m_scr[...] = jnp.zeros(m_scr.shape, F32)
        kj = lax.broadcasted_iota(jnp.int32, (2 * A, A), 0)
        qi = lax.broadcasted_iota(jnp.int32, (2 * A, A), 1)
        band = (kj > qi) & (kj <= qi + WINDOW)
        band_scr[0] = jnp.where(band, 0.0, -jnp.inf)
        band_scr[1] = jnp.where(band & (kj >= A), 0.0, -jnp.inf)
        r = lax.broadcasted_iota(jnp.int32, (C, C), 0)
        c = lax.broadcasted_iota(jnp.int32, (C, C), 1)
        causal_scr[...] = jnp.where(r <= c, 0.0, -jnp.inf)
        tri_scr[...] = (r <= c).astype(F32).astype(BF16)

    k_all = jnp.concatenate([ksp_ref[...], ksc_ref[...]], axis=0)
    v_all = jnp.concatenate([kvp_ref[ATT_KV_W:, :], kvc_ref[ATT_KV_W:, :]], axis=1).astype(BF16)
    zero_q = jnp.zeros((ATT_HEAD_DIM, A), BF16)
    slot = 0
    s_bufs = (s_scr_a, s_scr_b)

    def stage_scores(j):
        pieces = []
        for h in range(ATT_HEADS):
            q_h = qa_ref[h * ATT_HEAD_DIM:(h + 1) * ATT_HEAD_DIM, j * A:(j + 1) * A]
            pieces.append(jnp.concatenate([q_h, zero_q] if h < ATT_GROUP else [zero_q, q_h],
                                          axis=0))
        s_bufs[j % 2][slot] = _dot(k_all[j * A:(j + 2) * A, :], jnp.concatenate(pieces, axis=1))

    stage_scores(0)
    for j in range(L // A):
        cols = slice(j * A, (j + 1) * A)
        vt = v_all[:, j * A:(j + 2) * A]
        if j + 1 < L // A:
            stage_scores(j + 1)
        s_buf = s_bufs[j % 2]
        bias = jnp.where(i > 0, band_scr[0], band_scr[1]) if j == 0 else band_scr[0]
        m_rows = []
        for h in range(ATT_HEADS):
            sb = s_buf[slot, :, h * A:(h + 1) * A] + bias
            m_rows.append(jnp.maximum(jnp.max(sb, axis=0, keepdims=True),
                                      sinks_ref[h] * LOG2_E))
        inv_rows = []
        for h in range(ATT_HEADS):
            e = jnp.exp2(s_buf[slot, :, h * A:(h + 1) * A] + (bias - m_rows[h]))
            e_scr[:, h * A:(h + 1) * A] = e.astype(BF16)
            inv_rows.append(1.0 / (jnp.sum(e, axis=0, keepdims=True)
                                   + jnp.exp2(sinks_ref[h] * LOG2_E - m_rows[h])))
        o = _dot(vt, e_scr[...])
        for h in range(ATT_HEADS):
            g = h // ATT_GROUP
            mix_scr[h * ATT_HEAD_DIM:(h + 1) * ATT_HEAD_DIM, cols] = (
                o[g * ATT_HEAD_DIM:(g + 1) * ATT_HEAD_DIM, h * A:(h + 1) * A]
                * inv_rows[h]).astype(BF16)

    row128 = lax.broadcasted_iota(jnp.int32, (LANES, C), 0)
    ones_rows = (row128 == 0).astype(F32).astype(BF16)
    for ci in range(L // C):
        tok = slice(ci * C, (ci + 1) * C)
        gates = gt_ref[:, tok] * LOG2_E
        cum_row = jnp.zeros(gates.shape, F32)
        for part in _split3(gates):
            cum_row = cum_row + _dot(part, tri_scr[...])
        ig_rows = gates[:ML_HEADS]
        b_rows = cum_row[ML_HEADS:]
        key_cols = jnp.concatenate([ig_rows - b_rows, jnp.zeros((LANES - ML_HEADS, C), F32)],
                                   axis=0).T
        for p in range(n_pairs):
            q_c = qm_ref[p * LANES:(p + 1) * LANES, tok]
            k_pair = km_ref[p * LANES:(p + 1) * LANES, tok]
            zero = jnp.zeros_like(q_c)
            state = state_scr[p]
            state_bf = state.astype(BF16)
            new_state = []
            for e_id in range(2):
                h = 2 * p + e_id
                v_rows = slice(h * ML_V_DIM, (h + 1) * ML_V_DIM)
                head_rows = (row128 < ML_QK_DIM) if e_id == 0 else (row128 >= ML_QK_DIM)
                q_pad = jnp.where(head_rows, q_c, zero)
                b_r = b_rows[h:h + 1, :]
                ig_r = ig_rows[h:h + 1, :]
                m_prev = m_scr[h:h + 1, 0:1]
                dm = (b_r + key_cols[:, h:h + 1]) + causal_scr[...]
                inter = b_r + m_prev
                m_row = jnp.maximum(inter, jnp.max(dm, axis=0, keepdims=True))
                w_inter = jnp.exp2(inter - m_row)
                qk = lax.dot_general(k_pair, q_pad, (((0,), (0,)), ((), ())),
                                     preferred_element_type=F32)
                p_t = (qk * jnp.exp2(dm - m_row)).astype(BF16)
                v_ext = jnp.concatenate([vm_ref[v_rows, tok], ones_rows], axis=0)
                num = _dot(v_ext, p_t) + w_inter * _dot(state_bf, q_pad)
                den = num[ML_V_DIM:ML_V_DIM + 1, :]
                hh = num[:ML_V_DIM] * (1.0 / jnp.maximum(jnp.abs(den), jnp.exp2(-m_row)))
                ms = jnp.mean(hh * hh, axis=0, keepdims=True)
                gate = _sigmoid(om_ref[v_rows, tok].astype(F32))
                mix_scr[ATT_Q_W + h * ML_V_DIM:ATT_Q_W + (h + 1) * ML_V_DIM, tok] = (
                    hh * lax.rsqrt(ms + EPS) * mlnw_ref[v_rows, :] * gate).astype(BF16)
                b_last = b_r[:, C - 1:C]
                a_r = b_last - b_r + ig_r
                m_new = jnp.maximum(b_last + m_prev, jnp.max(a_r, axis=-1, keepdims=True))
                sc = jnp.exp2(b_last + m_prev - m_new)
                wsv = (v_ext.astype(F32) * jnp.exp2(a_r - m_new)).astype(BF16)
                new_state.append(sc * state + _dot_nt(wsv, k_pair))
                m_scr[h:h + 1, :] = jnp.broadcast_to(m_new, (1, LANES))
            first = lax.broadcasted_iota(jnp.int32, state.shape, 1) < ML_QK_DIM
            state_scr[p] = jnp.where(first, new_state[0], new_state[1])

    x1_ref[...] = x_ref[...] + lax.dot_general(
        mix_scr[...], wout_ref[...], (((0,), (0,)), ((), ())), preferred_element_type=F32)

    @pl.when(i == pl.num_programs(1) - 1)
    def _():
        for p in range(n_pairs):
            c_t = state_scr[p, :ML_V_DIM, :].T
            for e_id in range(2):
                ct_ref[0, 2 * p + e_id] = c_t[e_id * ML_QK_DIM:(e_id + 1) * ML_QK_DIM, :]
            nrow_ref[0, p:p + 1, :] = state_scr[p, ML_V_DIM:ML_V_DIM + 1, :]
        for h in range(ML_HEADS):
            m_ref[0, :, h:h + 1] = m_scr[h:h + 1, 0:1] * (1.0 / LOG2_E)
        kt_ref[0] = kvc_ref[:ATT_KV_W, L - WINDOW:]
        vt_ref[0] = kvc_ref[ATT_KV_W:, L - WINDOW:]


def _prompt_mixer_t(batch, seq, sinks, qa, ks, kv, qm, km, vm, om, gt, x2d, wout, mlnw_col):
    tq = MIX_TILE
    nt = seq // tq
    sub = tq // WINDOW
    col = lambda w: pl.BlockSpec((None, w, tq), lambda b, i: (b * nt + i, 0, 0))
    full = lambda a: pl.BlockSpec(a.shape, lambda b, i: (0,) * a.ndim)
    once = lambda a: pl.BlockSpec(a.shape, lambda b, i: (0,) * a.ndim,
                                  pipeline_mode=pl.Buffered(1))
    prev_block = lambda b, i: jnp.maximum((b * nt + i) * sub - 1, 0)
    per_batch = lambda *dims: pl.BlockSpec((1,) + dims, lambda b, i: (b,) + (0,) * len(dims))
    return pl.pallas_call(
        _prompt_mixer_t_kernel,
        grid=(batch, nt),
        in_specs=[pl.BlockSpec(memory_space=pltpu.SMEM),
                  col(ATT_Q_W),
                  pl.BlockSpec((tq, ATT_KV_W), lambda b, i: (b * nt + i, 0)),
                  pl.BlockSpec((WINDOW, ATT_KV_W), lambda b, i: (prev_block(b, i), 0)),
                  col(2 * ATT_KV_W),
                  pl.BlockSpec((None, 2 * ATT_KV_W, WINDOW),
                               lambda b, i: (prev_block(b, i) // sub, 0, prev_block(b, i) % sub)),
                  col(ML_QK_W), col(ML_QK_W), col(ML_V_W), col(ML_V_W), col(N_GATES),
                  pl.BlockSpec((tq, D_MODEL), lambda b, i: (b * nt + i, 0)),
                  once(wout), full(mlnw_col)],
        out_specs=[pl.BlockSpec((tq, D_MODEL), lambda b, i: (b * nt + i, 0)),
                   per_batch(ML_HEADS, ML_QK_DIM, ML_V_DIM),
                   per_batch(ML_HEADS // 2, LANES),
                   per_batch(1, ML_HEADS),
                   per_batch(ATT_KV_W, WINDOW),
                   per_batch(ATT_KV_W, WINDOW)],
        out_shape=[jax.ShapeDtypeStruct((batch * seq, D_MODEL), F32),
                   jax.ShapeDtypeStruct((batch, ML_HEADS, ML_QK_DIM, ML_V_DIM), F32),
                   jax.ShapeDtypeStruct((batch, ML_HEADS // 2, LANES), F32),
                   jax.ShapeDtypeStruct((batch, 1, ML_HEADS), F32),
                   jax.ShapeDtypeStruct((batch, ATT_KV_W, WINDOW), F32),
                   jax.ShapeDtypeStruct((batch, ATT_KV_W, WINDOW), F32)],
        scratch_shapes=[pltpu.VMEM((D_MODEL, tq), BF16),
                        pltpu.VMEM((ML_HEADS // 2, 2 * LANES, LANES), F32),
                        pltpu.VMEM((SUBLANES, LANES), F32),
                        pltpu.VMEM((2, 2 * WINDOW, WINDOW), F32),
                        pltpu.VMEM((ML_CHUNK, ML_CHUNK), F32),
                        pltpu.VMEM((ML_CHUNK, ML_CHUNK), BF16),
                        pltpu.VMEM((2, 2 * WINDOW, ATT_HEADS * WINDOW), F32),
                        pltpu.VMEM((2, 2 * WINDOW, ATT_HEADS * WINDOW), F32),
                        pltpu.VMEM((2 * WINDOW, ATT_HEADS * WINDOW), BF16)],
        compiler_params=pltpu.CompilerParams(dimension_semantics=("arbitrary", "arbitrary"),
                                             vmem_limit_bytes=VMEM_LIMIT),
        name="prompt_mixer_t",
    )(sinks, qa, ks, ks, kv, kv, qm, km, vm, om, gt, x2d, wout, mlnw_col)


def _sample_mixer_kernel(t_len, sinks_ref, qa_ref, kv_ref, ck_ref, cv_ref, qm_ref, km_ref,
                         vm_ref, om_ref, gt_ref, c0_ref, n0_ref, m0_ref, x_ref, wout_ref,
                         mlnw_ref, x1_ref, nk_ref, nv_ref, c_ref, n_ref, m_ref,
                         mix_scr, wperm_scr):
    bt = SAMPLE_BT
    T = t_len
    L = bt * T

    @pl.when(pl.program_id(0) == 0)
    def _():
        _permute_head_rows(wperm_scr, wout_ref)
        wperm_scr[ATT_Q_W:, :] = wout_ref[ATT_Q_W:, :]

    lane3 = lax.broadcasted_iota(jnp.int32, (bt, T, LANES), 2)
    low3 = lane3 < ATT_HEAD_DIM
    lane = lax.broadcasted_iota(jnp.int32, (L, LANES), 1)
    low = lane < ATT_HEAD_DIM

    qa3 = qa_ref[...].astype(F32).reshape(bt, T, ATT_Q_W)
    pieces = []
    for col in range(ATT_GROUP):
        qc = qa3[:, :, col * LANES:(col + 1) * LANES]
        pieces += [jnp.where(low3, qc, 0.0), jnp.where(low3, 0.0, qc)]
    q3 = jnp.concatenate(pieces, axis=1).astype(BF16)
    R = bt * N_STACK * T
    q2 = q3.reshape(R, LANES)
    kv_new = kv_ref[...]
    k_new = kv_new[:, :ATT_KV_W]
    v_new = kv_new[:, ATT_KV_W:]
    ck = ck_ref[...]
    cv = cv_ref[...]
    s_c = jnp.einsum('bqd,bdk->bqk', q3, ck.astype(BF16),
                     preferred_element_type=F32).reshape(R, WINDOW)
    s_n = _dot_nt(q2, k_new.astype(BF16))
    row_c = lax.broadcasted_iota(jnp.int32, (R, WINDOW), 0)
    col_c = lax.broadcasted_iota(jnp.int32, (R, WINDOW), 1)
    s_c = jnp.where(col_c > row_c % T, s_c, -jnp.inf)
    row_n = lax.broadcasted_iota(jnp.int32, (R, L), 0)
    col_n = lax.broadcasted_iota(jnp.int32, (R, L), 1)
    valid_n = (row_n // (N_STACK * T) == col_n // T) & (col_n % T <= row_n % T)
    s_n = jnp.where(valid_n, s_n, -jnp.inf)
    stack_id = (lax.broadcasted_iota(jnp.int32, (R, 1), 0) // T) % N_STACK
    sink = jnp.zeros((R, 1), F32)
    for k_id in range(N_STACK):
        sink = jnp.where(stack_id == k_id, sinks_ref[ATT_HEAD_ORDER[k_id]], sink)
    m = jnp.maximum(jnp.maximum(jnp.max(s_c, axis=-1, keepdims=True),
                                jnp.max(s_n, axis=-1, keepdims=True)), sink)
    e_c = jnp.exp(s_c - m)
    e_n = jnp.exp(s_n - m)
    denom = (jnp.sum(e_c, axis=-1, keepdims=True) + jnp.sum(e_n, axis=-1, keepdims=True)
             + jnp.exp(sink - m))
    o = jnp.einsum('bqk,bdk->bqd', e_c.astype(BF16).reshape(bt, N_STACK * T, WINDOW),
                   cv.astype(BF16), preferred_element_type=F32).reshape(R, LANES)
    o = (o + _dot(e_n.astype(BF16), v_new.astype(BF16))) / denom
    o3 = o.reshape(bt, N_STACK * T, LANES)
    for col in range(ATT_GROUP):
        lo_h = o3[:, (2 * col) * T:(2 * col + 1) * T, :]
        hi_h = o3[:, (2 * col + 1) * T:(2 * col + 2) * T, :]
        mix_scr[:, col * LANES:(col + 1) * LANES] = jnp.where(
            low3, lo_h, hi_h).reshape(L, LANES).astype(BF16)

    keep = lax.broadcasted_iota(jnp.int32, (ATT_KV_W, WINDOW), 1) < WINDOW - T
    k_new_t = k_new.T
    v_new_t = v_new.T
    for q in range(bt):
        shift = (WINDOW - T - q * T) % WINDOW
        nk_ref[q] = jnp.where(keep, pltpu.roll(ck[q], WINDOW - T, axis=1),
                              pltpu.roll(k_new_t, shift, axis=1))
        nv_ref[q] = jnp.where(keep, pltpu.roll(cv[q], WINDOW - T, axis=1),
                              pltpu.roll(v_new_t, shift, axis=1))

    r = lax.broadcasted_iota(jnp.int32, (L, L), 0)
    c = lax.broadcasted_iota(jnp.int32, (L, L), 1)
    seg = (r // T == c // T) & (r <= c)
    seg_bias = jnp.where(seg, 0.0, -jnp.inf)
    seg_bf = seg.astype(F32).astype(BF16)
    gates = gt_ref[...] * LOG2_E
    cum_row = jnp.zeros(gates.shape, F32)
    for part in _split3(gates):
        cum_row = cum_row + _dot(part, seg_bf)
    ig_rows = gates[:ML_HEADS]
    b_rows = cum_row[ML_HEADS:]
    gate_cols = jnp.concatenate([ig_rows, b_rows, jnp.zeros((LANES - N_GATES, L), F32)],
                                axis=0).T

    def col_to_row(x_col):
        return jnp.broadcast_to(x_col, (L, LANES)).T[0:1, :]

    ones_rows = (r[:LANES] == 0).astype(F32).astype(BF16)
    qm = qm_ref[...]
    km = km_ref[...]
    qm_f = qm.astype(F32)
    km_f = km.astype(F32)
    n_rep = bt * ML_QK_DIM // LANES
    bd_row = lax.broadcasted_iota(jnp.int32, (L, bt * ML_QK_DIM), 0) // T
    bd_lane = lax.broadcasted_iota(jnp.int32, (L, bt * ML_QK_DIM), 1) // ML_QK_DIM
    block_diag = bd_row == bd_lane

    def spread(x_pair, e):
        other = pltpu.roll(x_pair, ML_QK_DIM, axis=1)
        twice = jnp.where(low, x_pair, other) if e == 0 else jnp.where(low, other, x_pair)
        return jnp.where(block_diag, jnp.concatenate([twice] * n_rep, axis=1), 0.0).astype(BF16)

    for h in range(ML_HEADS):
        p, e = divmod(h, 2)
        qc = qm[:, p * LANES:(p + 1) * LANES]
        k_pair = km[:, p * LANES:(p + 1) * LANES]
        zero = jnp.zeros_like(qc)
        q_pad = jnp.where(low, qc, zero) if e == 0 else jnp.where(low, zero, qc)
        v_h = vm_ref[:, h * ML_V_DIM:(h + 1) * ML_V_DIM]
        v_ext_t = jnp.concatenate([v_h.astype(F32).T.astype(BF16), ones_rows], axis=0)
        ig_c = gate_cols[:, h:h + 1]
        b_c = gate_cols[:, ML_HEADS + h:ML_HEADS + h + 1]
        b_r = b_rows[h:h + 1, :]
        m0 = m0_ref[:, :, h:h + 1] * LOG2_E
        inter = b_r + col_to_row(jnp.broadcast_to(m0, (bt, T, 1)).reshape(L, 1))
        dm = (b_r + (ig_c - b_c)) + seg_bias
        m_row = jnp.maximum(inter, jnp.max(dm, axis=0, keepdims=True))
        w_inter = jnp.exp2(inter - m_row)
        p_t = (_dot_nt(k_pair, q_pad) * jnp.exp2(dm - m_row)).astype(BF16)
        num_t = _dot(v_ext_t, p_t)
        q_h3 = qm_f[:, h * ML_QK_DIM:(h + 1) * ML_QK_DIM].reshape(bt, T, ML_QK_DIM)
        k_h3 = km_f[:, h * ML_QK_DIM:(h + 1) * ML_QK_DIM].reshape(bt, T, ML_QK_DIM)
        c0 = c0_ref[:, h]
        n0 = n0_ref[:, h:h + 1, :]
        q_c_t = _dot(spread(qm_f[:, p * LANES:(p + 1) * LANES], e),
                     c0.astype(BF16).reshape(bt * ML_QK_DIM, ML_V_DIM)).T
        q_n_r = col_to_row(jnp.sum(q_h3 * n0, axis=-1, keepdims=True).reshape(L, 1))
        num = num_t[:ML_V_DIM] + w_inter * q_c_t
        den = num_t[ML_V_DIM:ML_V_DIM + 1] + w_inter * q_n_r
        hh = num * (1.0 / jnp.maximum(jnp.abs(den), jnp.exp2(-m_row)))
        ms = jnp.mean(hh * hh, axis=0, keepdims=True)
        mix_scr[:, ATT_Q_W + h * ML_V_DIM:ATT_Q_W + (h + 1) * ML_V_DIM] = (
            (hh * lax.rsqrt(ms + EPS)).T * mlnw_ref[:, h * ML_V_DIM:(h + 1) * ML_V_DIM]
            * _sigmoid(om_ref[:, h * ML_V_DIM:(h + 1) * ML_V_DIM].astype(F32))).astype(BF16)
        b3 = b_c.reshape(bt, T, 1)
        b_last = b3[:, T - 1:T, :]
        a3 = b_last - b3 + ig_c.reshape(bt, T, 1)
        m_new = jnp.maximum(b_last + m0, jnp.max(a3, axis=1, keepdims=True))
        sc = jnp.exp2(b_last + m0 - m_new)
        ws = jnp.exp2(a3 - m_new)
        kw = spread(km_f[:, p * LANES:(p + 1) * LANES] * ws.reshape(L, 1), e)
        d_c = lax.dot_general(kw, v_h, (((0,), (0,)), ((), ())), preferred_element_type=F32)
        c_ref[:, h] = sc * c0 + d_c.reshape(bt, ML_QK_DIM, ML_V_DIM)
        n_ref[:, h:h + 1, :] = sc * n0 + jnp.sum(ws * k_h3, axis=1, keepdims=True)
        m_ref[:, :, h:h + 1] = m_new * (1.0 / LOG2_E)

    x1_ref[...] = x_ref[...] + _dot(mix_scr[...], wperm_scr[...])


def _sample_mixer(nb, t_len, sinks, qa, kv, ck, cv, qm, km, vm, om, gt, c0, n0, m0, x2d, wout, mlnw):
    bt = SAMPLE_BT
    tl = bt * t_len
    row = lambda w: pl.BlockSpec((tl, w), lambda i: (i, 0))
    full = lambda a: pl.BlockSpec(a.shape, lambda i: (0,) * a.ndim)
    once = lambda a: pl.BlockSpec(a.shape, lambda i: (0,) * a.ndim, pipeline_mode=pl.Buffered(1))
    cache = pl.BlockSpec((bt, ATT_KV_W, WINDOW), lambda i: (i, 0, 0))
    c_spec = pl.BlockSpec((bt, ML_HEADS, ML_QK_DIM, ML_V_DIM), lambda i: (i, 0, 0, 0))
    n_spec = pl.BlockSpec((bt, ML_HEADS, ML_QK_DIM), lambda i: (i, 0, 0))
    m_spec = pl.BlockSpec((bt, 1, ML_HEADS), lambda i: (i, 0, 0))
    return pl.pallas_call(
        functools.partial(_sample_mixer_kernel, t_len),
        grid=(nb // bt,),
        in_specs=[pl.BlockSpec(memory_space=pltpu.SMEM),
                  row(ATT_Q_W), row(2 * ATT_KV_W), cache, cache, row(ML_QK_W), row(ML_QK_W),
                  row(ML_V_W), row(ML_V_W), pl.BlockSpec((N_GATES, tl), lambda i: (0, i)),
                  c_spec, n_spec, m_spec, row(D_MODEL), once(wout), full(mlnw)],
        out_specs=[row(D_MODEL), cache, cache, c_spec, n_spec, m_spec],
        out_shape=[jax.ShapeDtypeStruct((nb * t_len, D_MODEL), F32),
                   jax.ShapeDtypeStruct((nb, ATT_KV_W, WINDOW), F32),
                   jax.ShapeDtypeStruct((nb, ATT_KV_W, WINDOW), F32),
                   jax.ShapeDtypeStruct((nb, ML_HEADS, ML_QK_DIM, ML_V_DIM), F32),
                   jax.ShapeDtypeStruct((nb, ML_HEADS, ML_QK_DIM), F32),
                   jax.ShapeDtypeStruct((nb, 1, ML_HEADS), F32)],
        scratch_shapes=[pltpu.VMEM((tl, D_MODEL), BF16),
                        pltpu.VMEM((D_MODEL, D_MODEL), BF16)],
        compiler_params=pltpu.CompilerParams(dimension_semantics=("arbitrary",),
                                             vmem_limit_bytes=VMEM_LIMIT),
        name="sample_mixer",
    )(sinks, qa, kv, ck, cv, qm, km, vm, om, gt, c0, n0, m0, x2d, wout, mlnw)


def _ffn_kernel(seq_rows, *refs):
    if seq_rows is None:
        (x_ref, nw_ref, w_ref, cw_ref, cb_ref, wd_ref, y_ref, conv_ref,
         gbuf, act_scr, carry) = refs
        hist_ref = None
    else:
        (x_ref, hist_ref, nw_ref, w_ref, cw_ref, cb_ref, wd_ref, y_ref, conv_ref,
         gbuf, act_scr) = refs
        carry = None
    tm = x_ref.shape[0]
    tf = FF_CHUNK
    n_hist = CONV_W - 1
    rows = tm if seq_rows is None else seq_rows
    nseq = tm // rows
    base = SUBLANES
    n_chunks = D_FF // tf

    if carry is not None:
        @pl.when(pl.program_id(1) == 0)
        def _():
            carry[...] = jnp.zeros(carry.shape, F32)

    x = x_ref[...]
    h2 = _rms(x, nw_ref[...]).astype(BF16)

    def proj(f):
        return (_dot(h2, w_ref[:, f * tf:(f + 1) * tf]),
                _dot(h2, w_ref[:, D_FF + f * tf:D_FF + (f + 1) * tf]))

    nxt = proj(0)
    for f in range(n_chunks):
        g, u = nxt
        if f + 1 < n_chunks:
            nxt = proj(f + 1)
        cols = slice(f * tf, (f + 1) * tf)
        s = f % 2
        g3 = g.reshape(nseq, rows, tf)
        if seq_rows is None:
            gbuf[s, :, base - n_hist:base, :] = carry[:, SUBLANES - n_hist:, cols]
            carry[:, SUBLANES - n_hist:, cols] = g3[:, rows - n_hist:, :]
        else:
            gbuf[s, :, base - n_hist:base, :] = hist_ref[:, :, cols]
            conv_ref[:, :, cols] = g3[:, rows - n_hist:, :]
        gbuf[s, :, base:base + rows, :] = g3
        gc = cb_ref[:, cols] + g * cw_ref[CONV_W - 1:CONV_W, cols]
        for d in range(1, CONV_W):
            gm = gbuf[s, :, base - d:base - d + rows, :].reshape(tm, tf)
            gc = gc + gm * cw_ref[CONV_W - 1 - d:CONV_W - d, cols]
        act_scr[:, cols] = (gc * _sigmoid(gc) * u).astype(BF16)
    y_ref[...] = x + _dot(act_scr[...], wd_ref[...])

    if carry is not None:
        @pl.when(pl.program_id(1) == pl.num_programs(1) - 1)
        def _():
            conv_ref[...] = carry[:, SUBLANES - n_hist:, :]


def _ffn_scratch(tm, rows):
    return [pltpu.VMEM((2, tm // rows, SUBLANES + rows, FF_CHUNK), F32),
            pltpu.VMEM((tm, D_FF), BF16)]


def _ffn_prompt(batch, seq, x2d, nw, w, cw, cb, wd):
    tm = ROW_TILE
    nt = seq // tm
    full = lambda a: pl.BlockSpec(a.shape, lambda b, i: (0,) * a.ndim)
    once = lambda a: pl.BlockSpec(a.shape, lambda b, i: (0,) * a.ndim,
                                  pipeline_mode=pl.Buffered(1))
    row = pl.BlockSpec((tm, D_MODEL), lambda b, i: (b * nt + i, 0))
    return pl.pallas_call(
        functools.partial(_ffn_kernel, None),
        grid=(batch, nt),
        in_specs=[row, full(nw), once(w), full(cw), full(cb), once(wd)],
        out_specs=[row, pl.BlockSpec((1, CONV_W - 1, D_FF), lambda b, i: (b, 0, 0))],
        out_shape=[jax.ShapeDtypeStruct((batch * seq, D_MODEL), F32),
                   jax.ShapeDtypeStruct((batch, CONV_W - 1, D_FF), F32)],
        scratch_shapes=_ffn_scratch(tm, tm) + [pltpu.VMEM((1, SUBLANES, D_FF), F32)],
        compiler_params=pltpu.CompilerParams(dimension_semantics=("arbitrary", "arbitrary"),
                                             vmem_limit_bytes=VMEM_LIMIT),
        name="ffn_prompt",
    )(x2d, nw, w, cw, cb, wd)


def _ffn_sample(nb, t_len, x2d, hist, nw, w, cw, cb, wd):
    tm = ROW_TILE
    bt = tm // t_len
    full = lambda a: pl.BlockSpec(a.shape, lambda i: (0,) * a.ndim)
    once = lambda a: pl.BlockSpec(a.shape, lambda i: (0,) * a.ndim, pipeline_mode=pl.Buffered(1))
    row = pl.BlockSpec((tm, D_MODEL), lambda i: (i, 0))
    hist_spec = pl.BlockSpec((bt, CONV_W - 1, D_FF), lambda i: (i, 0, 0))
    return pl.pallas_call(
        functools.partial(_ffn_kernel, t_len),
        grid=(nb // bt,),
        in_specs=[row, hist_spec, full(nw), once(w), full(cw), full(cb), once(wd)],
        out_specs=[row, hist_spec],
        out_shape=[jax.ShapeDtypeStruct((nb * t_len, D_MODEL), F32),
                   jax.ShapeDtypeStruct((nb, CONV_W - 1, D_FF), F32)],
        scratch_shapes=_ffn_scratch(tm, t_len),
        compiler_params=pltpu.CompilerParams(dimension_semantics=("arbitrary",),
                                             vmem_limit_bytes=VMEM_LIMIT),
        name="ffn_sample",
    )(x2d, hist, nw, w, cw, cb, wd)


def _head_mean_matrix(width, head_dim):
    idx = np.arange(width) // head_dim
    return jnp.asarray((idx[:, None] == idx[None, :]).astype(np.float32) / head_dim, dtype=BF16)


def _layer_weights(norm_mix_w, w_in, b_gates, q_norm_w, k_norm_w, sinks, ml_norm_w, w_out,
                   norm_ffn_w, w_ffn_in, conv_w, conv_b, w_down):
    w_in_t = jnp.pad(w_in.T.astype(BF16), ((0, IN_WIDTH_PAD - w_in.shape[1]), (0, 0)))
    return dict(
        nw=norm_mix_w.reshape(1, D_MODEL),
        w_in_t=w_in_t,
        bg=jnp.pad(b_gates, (0, LANES - N_GATES)).reshape(1, LANES),
        qnw=(jnp.tile(q_norm_w, ATT_HEADS) * ATT_SCALE).reshape(1, ATT_Q_W),
        knw=jnp.tile(k_norm_w, ATT_KV_HEADS).reshape(1, ATT_KV_W),
        gq=_head_mean_matrix(ATT_Q_W, ATT_HEAD_DIM),
        gk=_head_mean_matrix(ATT_KV_W, ATT_HEAD_DIM),
        bg_col=b_gates.reshape(N_GATES, 1),
        qnw_col=(jnp.tile(q_norm_w, ATT_HEADS) * (ATT_SCALE * LOG2_E)).reshape(ATT_Q_W, 1),
        knw_col=jnp.tile(k_norm_w, ATT_KV_HEADS).reshape(ATT_KV_W, 1),
        mlnw_col=ml_norm_w.reshape(ML_V_W, 1),
        sinks=sinks,
        mlnw=ml_norm_w.reshape(1, ML_V_W),
        wout=w_out.astype(BF16),
        nfw=norm_ffn_w.reshape(1, D_MODEL),
        wff=w_ffn_in.astype(BF16),
        cw=conv_w,
        cb=conv_b.reshape(1, D_FF),
        wd=w_down.astype(BF16),
    )


def _cache_from_t(a_t):
    n = a_t.shape[0]
    return jnp.transpose(a_t.reshape(n, ATT_KV_HEADS, ATT_HEAD_DIM, WINDOW), (0, 3, 1, 2))


def _cache_to_t(a):
    n = a.shape[0]
    return jnp.transpose(a, (0, 2, 3, 1)).reshape(n, ATT_KV_W, WINDOW)


def _prompt_layer(x, w):
    batch, seq, _ = x.shape
    assert seq % ROW_TILE == 0 and MIX_TILE == ROW_TILE and MIX_TILE % WINDOW == 0
    x2d = x.reshape(batch * seq, D_MODEL)
    qa, ks, kv, qm, km, vm, om, gt = _inproj_t(x2d, w["nw"], w["w_in_t"], w["bg_col"],
                                               w["qnw_col"], w["knw_col"])
    x1, c_t, n_row, m, k_t, v_t = _prompt_mixer_t(batch, seq, w["sinks"], qa, ks, kv, qm, km, vm,
                                                  om, gt, x2d, w["wout"], w["mlnw_col"])
    y, conv = _ffn_prompt(batch, seq, x1, w["nfw"], w["wff"], w["cw"], w["cb"], w["wd"])
    return (y.reshape(batch, seq, D_MODEL), _cache_from_t(k_t), _cache_from_t(v_t),
            jnp.swapaxes(c_t, -1, -2), n_row.reshape(batch, ML_HEADS, ML_QK_DIM),
            m.reshape(batch, ML_HEADS), conv)


def _sample_layer(x, ck, cv, c0, n0, m0, conv_buf, w):
    nb, t_len, _ = x.shape
    assert t_len == SUBLANES and SAMPLE_BT * t_len == LANES and nb % SAMPLE_BT == 0
    assert (nb * t_len) % ROW_TILE == 0
    x2d = x.reshape(nb * t_len, D_MODEL)
    qa, kv, qm, km, vm, om, gt = _inproj(x2d, w["nw"], w["w_in_t"], w["bg"], w["qnw"], w["knw"],
                                         w["gq"], w["gk"])
    x1, nk_t, nv_t, c_t, n, m = _sample_mixer(
        nb, t_len, w["sinks"], qa, kv, _cache_to_t(ck), _cache_to_t(cv), qm, km, vm, om, gt,
        jnp.swapaxes(c0, -1, -2), n0, m0.reshape(nb, 1, ML_HEADS), x2d, w["wout"], w["mlnw"])
    y, conv = _ffn_sample(nb, t_len, x1, conv_buf, w["nfw"], w["wff"], w["cw"], w["cb"],
                          w["wd"])
    return (y.reshape(nb, t_len, D_MODEL), _cache_from_t(nk_t), _cache_from_t(nv_t),
            jnp.swapaxes(c_t, -1, -2), n, m.reshape(nb, ML_HEADS), conv)


def kernel(x_prompt, x_sample, cache_attn_k, cache_attn_v, state_mlstm_C, state_mlstm_n,
           state_mlstm_m, cache_ffn_conv, norm_mix_w, w_in, b_gates, q_norm_w, k_norm_w,
           sinks, ml_norm_w, w_out, norm_ffn_w, w_ffn_in, conv_w, conv_b, w_down):
    depth = w_in.shape[0]
    yp, ys = x_prompt, x_sample
    sp = [[] for _ in range(6)]
    ss = [[] for _ in range(6)]
    for l in range(depth):
        w = _layer_weights(norm_mix_w[l], w_in[l], b_gates[l], q_norm_w[l], k_norm_w[l], sinks[l],
                           ml_norm_w[l], w_out[l], norm_ffn_w[l], w_ffn_in[l], conv_w[l],
                           conv_b[l], w_down[l])
        yp, *st_p = _prompt_layer(yp, w)
        ys, *st_s = _sample_layer(ys, cache_attn_k[l], cache_attn_v[l], state_mlstm_C[l],
                                  state_mlstm_n[l], state_mlstm_m[l], cache_ffn_conv[l], w)
        for i in range(6):
            sp[i].append(st_p[i])
            ss[i].append(st_s[i])
    k_p, v_p, c_p, n_p, m_p, conv_p = [jnp.stack(a) for a in sp]
    k_s, v_s, c_s, n_s, m_s, conv_s = [jnp.stack(a) for a in ss]
    return (yp, ys, k_p, v_p, c_p, n_p, m_p, conv_p, k_s, v_s, c_s, n_s, m_s, conv_s)
```

```python
import functools

import numpy as np
import jax
import jax.numpy as jnp
from jax import lax
from jax.experimental import pallas as pl
from jax.experimental.pallas import tpu as pltpu

F32 = jnp.float32
BF16 = jnp.bfloat16

D_MODEL = 1024
ATT_HEADS = 8
ATT_KV_HEADS = 2
ATT_HEAD_DIM = 64
ATT_GROUP = ATT_HEADS // ATT_KV_HEADS
WINDOW = 128
ML_HEADS = 4
ML_V_DIM = 128
ML_QK_DIM = 64
D_FF = 2816
CONV_W = 3
EPS = 1e-6
ATT_SCALE = ATT_HEAD_DIM ** -0.5
ML_SCALE = ML_QK_DIM ** -0.5
LOG2_E = 1.4426950408889634

ATT_Q_W = ATT_HEADS * ATT_HEAD_DIM
ATT_KV_W = ATT_KV_HEADS * ATT_HEAD_DIM
ML_QK_W = ML_HEADS * ML_QK_DIM
ML_V_W = ML_HEADS * ML_V_DIM
N_GATES = 2 * ML_HEADS
N_STACK = 2 * ATT_GROUP

LANES = 128
SUBLANES = 8

OFF_QA = 0
OFF_KV = OFF_QA + ATT_Q_W
OFF_QM = OFF_KV + 2 * ATT_KV_W
OFF_KM = OFF_QM + ML_QK_W
OFF_VM = OFF_KM + ML_QK_W
OFF_OM = OFF_VM + ML_V_W
OFF_GL = OFF_OM + ML_V_W
IN_WIDTH_PAD = OFF_GL + LANES

ATT_HEAD_ORDER = tuple(h for c in range(ATT_GROUP) for h in (c, c + ATT_GROUP))

ROW_TILE = 512
INPROJ_SUB = 256
MIX_TILE = 512
ML_CHUNK = 256
FF_CHUNK = 256
SAMPLE_BT = 16
VMEM_LIMIT = 56 * 1024 * 1024


def _dot(a, b):
    return jnp.dot(a, b, preferred_element_type=F32)


def _dot_nt(a, b):
    return lax.dot_general(a, b, (((1,), (1,)), ((), ())), preferred_element_type=F32)


def _split3(x):
    hi = x.astype(BF16)
    r1 = x - hi.astype(F32)
    mid = r1.astype(BF16)
    lo = (r1 - mid.astype(F32)).astype(BF16)
    return hi, mid, lo


def _rms(x, w):
    ms = jnp.mean(x * x, axis=-1, keepdims=True)
    return x * lax.rsqrt(ms + EPS) * w


def _log_sigmoid(x):
    return jnp.minimum(x, 0.0) - jnp.log1p(jnp.exp(-jnp.abs(x)))


def _sigmoid(x):
    return 1.0 / (1.0 + jnp.exp(-x))


def _permute_head_rows(dst_ref, src_ref):
    for k, h in enumerate(ATT_HEAD_ORDER):
        dst_ref[k * ATT_HEAD_DIM:(k + 1) * ATT_HEAD_DIM, :] = (
            src_ref[h * ATT_HEAD_DIM:(h + 1) * ATT_HEAD_DIM, :])


def _inproj_kernel(x_ref, nw_ref, w_ref, bg_ref, qnw_ref, knw_ref, gq_ref, gk_ref,
                   qa_ref, kv_ref, qm_ref, km_ref, vm_ref, om_ref, gt_ref, wq_scr):
    @pl.when(pl.program_id(0) == 0)
    def _():
        _permute_head_rows(wq_scr, w_ref)

    h = _rms(x_ref[...], nw_ref[...]).astype(BF16)

    def proj(lo, width):
        return _dot_nt(h, w_ref[lo:lo + width, :])

    q = _dot_nt(h, wq_scr[...])
    q_ms = _dot((q * q).astype(BF16), gq_ref[...])
    qa_ref[...] = (q * lax.rsqrt(q_ms + EPS) * qnw_ref[...]).astype(BF16)

    kv = proj(OFF_KV, 2 * ATT_KV_W)
    k = kv[:, :ATT_KV_W]
    k_ms = _dot((k * k).astype(BF16), gk_ref[...])
    kv_ref[:, :ATT_KV_W] = k * lax.rsqrt(k_ms + EPS) * knw_ref[...]
    kv_ref[:, ATT_KV_W:] = kv[:, ATT_KV_W:]

    qm_ref[...] = (proj(OFF_QM, ML_QK_W) * ML_SCALE).astype(BF16)
    km_ref[...] = proj(OFF_KM, ML_QK_W).astype(BF16)
    vm_ref[...] = proj(OFF_VM, ML_V_W).astype(BF16)
    om_ref[...] = proj(OFF_OM, ML_V_W).astype(BF16)

    gl = proj(OFF_GL, LANES) + bg_ref[...]
    lane = lax.broadcasted_iota(jnp.int32, gl.shape, 1)
    g = jnp.where(lane < ML_HEADS, gl, _log_sigmoid(gl))
    gt_ref[...] = g.T[:N_GATES, :]


def _inproj(x2d, nw, w_in_t, bg, qnw, knw, gq, gk):
    n = x2d.shape[0]
    tm = ROW_TILE
    row = lambda w: pl.BlockSpec((tm, w), lambda i: (i, 0))
    full = lambda a: pl.BlockSpec(a.shape, lambda i: (0,) * a.ndim)
    once = lambda a: pl.BlockSpec(a.shape, lambda i: (0,) * a.ndim, pipeline_mode=pl.Buffered(1))
    return pl.pallas_call(
        _inproj_kernel,
        grid=(n // tm,),
        in_specs=[row(D_MODEL), full(nw), once(w_in_t), full(bg), full(qnw), full(knw),
                  full(gq), full(gk)],
        out_specs=[row(ATT_Q_W), row(2 * ATT_KV_W), row(ML_QK_W), row(ML_QK_W),
                   row(ML_V_W), row(ML_V_W), pl.BlockSpec((N_GATES, tm), lambda i: (0, i))],
        out_shape=[jax.ShapeDtypeStruct((n, ATT_Q_W), BF16),
                   jax.ShapeDtypeStruct((n, 2 * ATT_KV_W), F32),
                   jax.ShapeDtypeStruct((n, ML_QK_W), BF16),
                   jax.ShapeDtypeStruct((n, ML_QK_W), BF16),
                   jax.ShapeDtypeStruct((n, ML_V_W), BF16),
                   jax.ShapeDtypeStruct((n, ML_V_W), BF16),
                   jax.ShapeDtypeStruct((N_GATES, n), F32)],
        scratch_shapes=[pltpu.VMEM((ATT_Q_W, D_MODEL), BF16)],
        compiler_params=pltpu.CompilerParams(dimension_semantics=("arbitrary",),
                                             vmem_limit_bytes=VMEM_LIMIT),
        name="inproj",
    )(x2d, nw, w_in_t, bg, qnw, knw, gq, gk)


def _head_norm_t(z, head_dim, w_col):
    rows, tokens = z.shape
    z3 = z.reshape(rows // head_dim, head_dim, tokens)
    ms = jnp.mean(z3 * z3, axis=1, keepdims=True)
    return (z3 * lax.rsqrt(ms + EPS)).reshape(rows, tokens) * w_col


def _inproj_t_kernel(x_ref, nw_ref, w_ref, bg_ref, qnw_ref, knw_ref,
                     qa_ref, ks_ref, kv_ref, qm_ref, km_ref, vm_ref, om_ref, gt_ref):
    tm = x_ref.shape[0]
    sub = INPROJ_SUB
    hs = [_rms(x_ref[c * sub:(c + 1) * sub, :], nw_ref[...]).astype(BF16)
          for c in range(tm // sub)]
    for c, h in enumerate(hs):
        tok = slice(c * sub, (c + 1) * sub)

        def proj(lo, width):
            return _dot_nt(w_ref[lo:lo + width, :], h)

        qa_ref[:, tok] = _head_norm_t(proj(OFF_QA, ATT_Q_W), ATT_HEAD_DIM,
                                      qnw_ref[...]).astype(BF16)
        kv = proj(OFF_KV, 2 * ATT_KV_W)
        k = _head_norm_t(kv[:ATT_KV_W], ATT_HEAD_DIM, knw_ref[...])
        kv_ref[:ATT_KV_W, tok] = k
        kv_ref[ATT_KV_W:, tok] = kv[ATT_KV_W:]
        ks_ref[tok, :] = k.T.astype(BF16)
        qm_ref[:, tok] = (proj(OFF_QM, ML_QK_W) * ML_SCALE).astype(BF16)
        km_ref[:, tok] = proj(OFF_KM, ML_QK_W).astype(BF16)
        vm_ref[:, tok] = proj(OFF_VM, ML_V_W).astype(BF16)
        om_ref[:, tok] = proj(OFF_OM, ML_V_W).astype(BF16)
        gl = proj(OFF_GL, 2 * SUBLANES)[:N_GATES] + bg_ref[...]
        row = lax.broadcasted_iota(jnp.int32, gl.shape, 0)
        gt_ref[:, tok] = jnp.where(row < ML_HEADS, gl, _log_sigmoid(gl))


def _inproj_t(x2d, nw, w_in_t, bg_col, qnw_col, knw_col):
    n = x2d.shape[0]
    tm = ROW_TILE
    full = lambda a: pl.BlockSpec(a.shape, lambda i: (0,) * a.ndim)
    once = lambda a: pl.BlockSpec(a.shape, lambda i: (0,) * a.ndim, pipeline_mode=pl.Buffered(1))
    col = lambda w: pl.BlockSpec((None, w, tm), lambda i: (i, 0, 0))
    slab = lambda w, dt: jax.ShapeDtypeStruct((n // tm, w, tm), dt)
    return pl.pallas_call(
        _inproj_t_kernel,
        grid=(n // tm,),
        in_specs=[pl.BlockSpec((tm, D_MODEL), lambda i: (i, 0)), full(nw), once(w_in_t),
                  full(bg_col), full(qnw_col), full(knw_col)],
        out_specs=[col(ATT_Q_W), pl.BlockSpec((tm, ATT_KV_W), lambda i: (i, 0)),
                   col(2 * ATT_KV_W), col(ML_QK_W), col(ML_QK_W), col(ML_V_W), col(ML_V_W),
                   col(N_GATES)],
        out_shape=[slab(ATT_Q_W, BF16),
                   jax.ShapeDtypeStruct((n, ATT_KV_W), BF16),
                   slab(2 * ATT_KV_W, F32), slab(ML_QK_W, BF16), slab(ML_QK_W, BF16),
                   slab(ML_V_W, BF16), slab(ML_V_W, BF16), slab(N_GATES, F32)],
        compiler_params=pltpu.CompilerParams(dimension_semantics=("arbitrary",),
                                             vmem_limit_bytes=VMEM_LIMIT),
        name="inproj_t",
    )(x2d, nw, w_in_t, bg_col, qnw_col, knw_col)


def _gate_forms(gates, seg_mask, want_raw_col):
    L = gates.shape[1]
    m_bf = seg_mask.astype(F32).astype(BF16)
    cum_row = jnp.zeros(gates.shape, F32)
    cum_col = jnp.zeros((L, gates.shape[0]), F32)
    raw_col = None
    if want_raw_col:
        r = lax.broadcasted_iota(jnp.int32, (L, L), 0)
        c = lax.broadcasted_iota(jnp.int32, (L, L), 1)
        eye = (r == c).astype(F32).astype(BF16)
        raw_col = jnp.zeros((L, gates.shape[0]), F32)
    for part in _split3(gates):
        cum_row = cum_row + _dot_nt(part, m_bf)
        cum_col = cum_col + _dot_nt(m_bf, part)
        if want_raw_col:
            raw_col = raw_col + _dot_nt(eye, part)
    return cum_row, cum_col, raw_col


def _mlstm_intra(q_pad, k_pair, v_ext, seg_mask, b_c, b_r, ig_r, m_prev_c):
    dm = jnp.where(seg_mask, b_c + (ig_r - b_r), -jnp.inf)
    inter = b_c + m_prev_c
    m_row = jnp.maximum(inter, jnp.max(dm, axis=-1, keepdims=True))
    w_inter = jnp.exp(inter - m_row)
    p = _dot_nt(q_pad, k_pair) * jnp.exp(dm - m_row)
    return _dot(p.astype(BF16), v_ext), m_row, w_inter


def _mlstm_out(pv, m_row, w_inter, q_c, q_n, mlnw_h, om_h):
    num = pv[:, :ML_V_DIM] + w_inter * q_c
    den = pv[:, ML_V_DIM:ML_V_DIM + 1] + w_inter * q_n
    hh = num / jnp.maximum(jnp.abs(den), jnp.exp(-m_row))
    return (_rms(hh, mlnw_h) * _sigmoid(om_h.astype(F32))).astype(BF16)


def _ones_col(rows):
    lane = lax.broadcasted_iota(jnp.int32, (rows, LANES), 1)
    return (lane == 0).astype(F32).astype(BF16)


def _prompt_mixer_t_kernel(sinks_ref, qa_ref, ksc_ref, ksp_ref, kvc_ref, kvp_ref, qm_ref, km_ref,
                           vm_ref, om_ref, gt_ref, x_ref, wout_ref, mlnw_ref,
                           x1_ref, ct_ref, nrow_ref, m_ref, kt_ref, vt_ref,
                           mix_scr, state_scr, m_scr, band_scr, causal_scr, tri_scr,
                           s_scr_a, s_scr_b, e_scr):
    i = pl.program_id(1)
    A = WINDOW
    L = MIX_TILE
    C = ML_CHUNK
    n_pairs = ML_HEADS // 2

    @pl.when(i == 0)
    def _():
        state_scr[...] = jnp.zeros(state_scr.shape, F32)
        m_scr[...] = jnp.zeros(m_scr.shape, F32)
        kj = lax.broadcasted_iota(jnp.int32, (2 * A, A), 0)
        qi = lax.broadcasted_iota(jnp.int32, (2 * A, A), 1)
        band = (kj > qi) & (kj <= qi + WINDOW)
        band_scr[0] = jnp.where(band, 0.0, -jnp.inf)
        band_scr[1] = jnp.where(band & (kj >= A), 0.0, -jnp.inf)
        r = lax.broadcasted_iota(jnp.int32, (C, C), 0)
        c = lax.broadcasted_iota(jnp.int32, (C, C), 1)
        causal_scr[...] = jnp.where(r <= c, 0.0, -jnp.inf)
        tri_scr[...] = (r <= c).astype(F32).astype(BF16)

    k_all = jnp.concatenate([ksp_ref[...], ksc_ref[...]], axis=0)
    v_all = jnp.concatenate([kvp_ref[ATT_KV_W:, :], kvc_ref[ATT_KV_W:, :]], axis=1).astype(BF16)
    zero_q = jnp.zeros((ATT_HEAD_DIM, A), BF16)
    slot = 0
    s_bufs = (s_scr_a, s_scr_b)

    def stage_scores(j):
        pieces = []
        for h in range(ATT_HEADS):
            q_h = qa_ref[h * ATT_HEAD_DIM:(h + 1) * ATT_HEAD_DIM, j * A:(j + 1) * A]
            pieces.append(jnp.concatenate([q_h, zero_q] if h < ATT_GROUP else [zero_q, q_h],
                                          axis=0))
        s_bufs[j % 2][slot] = _dot(k_all[j * A:(j + 2) * A, :], jnp.concatenate(pieces, axis=1))

    def attend(j):
        cols = slice(j * A, (j + 1) * A)
        vt = v_all[:, j * A:(j + 2) * A]
        if j + 1 < L // A:
            stage_scores(j + 1)
        s_buf = s_bufs[j % 2]
        bias = jnp.where(i > 0, band_scr[0], band_scr[1]) if j == 0 else band_scr[0]
        m_rows = []
        for h in range(ATT_HEADS):
            sb = s_buf[slot, :, h * A:(h + 1) * A] + bias
            m_rows.append(jnp.maximum(jnp.max(sb, axis=0, keepdims=True),
                                      sinks_ref[h] * LOG2_E))
        inv_rows = []
        for h in range(ATT_HEADS):
            e = jnp.exp2(s_buf[slot, :, h * A:(h + 1) * A] + (bias - m_rows[h]))
            e_scr[:, h * A:(h + 1) * A] = e.astype(BF16)
            inv_rows.append(1.0 / (jnp.sum(e, axis=0, keepdims=True)
                                   + jnp.exp2(sinks_ref[h] * LOG2_E - m_rows[h])))
        o = _dot(vt, e_scr[...])
        for h in range(ATT_HEADS):
            g = h // ATT_GROUP
            mix_scr[j // (C // A), h * ATT_HEAD_DIM:(h + 1) * ATT_HEAD_DIM,
                    (j % (C // A)) * A:(j % (C // A) + 1) * A] = (
                o[g * ATT_HEAD_DIM:(g + 1) * ATT_HEAD_DIM, h * A:(h + 1) * A]
                * inv_rows[h]).astype(BF16)

    row128 = lax.broadcasted_iota(jnp.int32, (LANES, C), 0)
    ones_rows = (row128 == 0).astype(F32).astype(BF16)

    def mlstm_chunk(ci):
        tok = slice(ci * C, (ci + 1) * C)
        gates = gt_ref[:, tok] * LOG2_E
        cum_row = jnp.zeros(gates.shape, F32)
        for part in _split3(gates):
            cum_row = cum_row + _dot(part, tri_scr[...])
        ig_rows = gates[:ML_HEADS]
        b_rows = cum_row[ML_HEADS:]
        key_cols = jnp.concatenate([ig_rows - b_rows, jnp.zeros((LANES - ML_HEADS, C), F32)],
                                   axis=0).T
        for p in range(n_pairs):
            q_c = qm_ref[p * LANES:(p + 1) * LANES, tok]
            k_pair = km_ref[p * LANES:(p + 1) * LANES, tok]
            zero = jnp.zeros_like(q_c)
            state = state_scr[p]
            state_bf = state.astype(BF16)
            new_state = []
            for e_id in range(2):
                h = 2 * p + e_id
                v_rows = slice(h * ML_V_DIM, (h + 1) * ML_V_DIM)
                head_rows = (row128 < ML_QK_DIM) if e_id == 0 else (row128 >= ML_QK_DIM)
                q_pad = jnp.where(head_rows, q_c, zero)
                b_r = b_rows[h:h + 1, :]
                ig_r = ig_rows[h:h + 1, :]
                m_prev = m_scr[h:h + 1, 0:1]
                dm = (b_r + key_cols[:, h:h + 1]) + causal_scr[...]
                inter = b_r + m_prev
                m_row = jnp.maximum(inter, jnp.max(dm, axis=0, keepdims=True))
                w_inter = jnp.exp2(inter - m_row)
                qk = lax.dot_general(k_pair, q_pad, (((0,), (0,)), ((), ())),
                                     preferred_element_type=F32)
                p_t = (qk * jnp.exp2(dm - m_row)).astype(BF16)
                v_ext = jnp.concatenate([vm_ref[v_rows, tok], ones_rows], axis=0)
                num = _dot(v_ext, p_t) + w_inter * _dot(state_bf, q_pad)
                den = num[ML_V_DIM:ML_V_DIM + 1, :]
                hh = num[:ML_V_DIM] * (1.0 / jnp.maximum(jnp.abs(den), jnp.exp2(-m_row)))
                ms = jnp.mean(hh * hh, axis=0, keepdims=True)
                gate = _sigmoid(om_ref[v_rows, tok].astype(F32))
                mix_scr[ci, ATT_Q_W + h * ML_V_DIM:ATT_Q_W + (h + 1) * ML_V_DIM, :] = (
                    hh * lax.rsqrt(ms + EPS) * mlnw_ref[v_rows, :] * gate).astype(BF16)
                b_last = b_r[:, C - 1:C]
                a_r = b_last - b_r + ig_r
                m_new = jnp.maximum(b_last + m_prev, jnp.max(a_r, axis=-1, keepdims=True))
                sc = jnp.exp2(b_last + m_prev - m_new)
                wsv = (v_ext.astype(F32) * jnp.exp2(a_r - m_new)).astype(BF16)
                new_state.append(sc * state + _dot_nt(wsv, k_pair))
                m_scr[h:h + 1, :] = jnp.broadcast_to(m_new, (1, LANES))
            first = lax.broadcasted_iota(jnp.int32, state.shape, 1) < ML_QK_DIM
            state_scr[p] = jnp.where(first, new_state[0], new_state[1])

    stage_scores(0)
    for ci in range(L // C):
        for j in range(ci * (C // A), (ci + 1) * (C // A)):
            attend(j)
        mlstm_chunk(ci)
        tok = slice(ci * C, (ci + 1) * C)
        x1_ref[tok, :] = x_ref[tok, :] + lax.dot_general(
            mix_scr[ci], wout_ref[...], (((0,), (0,)), ((), ())),
            preferred_element_type=F32)

    @pl.when(i == pl.num_programs(1) - 1)
    def _():
        for p in range(n_pairs):
            c_t = state_scr[p, :ML_V_DIM, :].T
            for e_id in range(2):
                ct_ref[0, 2 * p + e_id] = c_t[e_id * ML_QK_DIM:(e_id + 1) * ML_QK_DIM, :]
            nrow_ref[0, p:p + 1, :] = state_scr[p, ML_V_DIM:ML_V_DIM + 1, :]
        for h in range(ML_HEADS):
            m_ref[0, :, h:h + 1] = m_scr[h:h + 1, 0:1] * (1.0 / LOG2_E)
        kt_ref[0] = kvc_ref[:ATT_KV_W, L - WINDOW:]
        vt_ref[0] = kvc_ref[ATT_KV_W:, L - WINDOW:]


def _prompt_mixer_t(batch, seq, sinks, qa, ks, kv, qm, km, vm, om, gt, x2d, wout, mlnw_col):
    tq = MIX_TILE
    nt = seq // tq
    sub = tq // WINDOW
    col = lambda w: pl.BlockSpec((None, w, tq), lambda b, i: (b * nt + i, 0, 0))
    full = lambda a: pl.BlockSpec(a.shape, lambda b, i: (0,) * a.ndim)
    once = lambda a: pl.BlockSpec(a.shape, lambda b, i: (0,) * a.ndim,
                                  pipeline_mode=pl.Buffered(1))
    prev_block = lambda b, i: jnp.maximum((b * nt + i) * sub - 1, 0)
    per_batch = lambda *dims: pl.BlockSpec((1,) + dims, lambda b, i: (b,) + (0,) * len(dims))
    return pl.pallas_call(
        _prompt_mixer_t_kernel,
        grid=(batch, nt),
        in_specs=[pl.BlockSpec(memory_space=pltpu.SMEM),
                  col(ATT_Q_W),
                  pl.BlockSpec((tq, ATT_KV_W), lambda b, i: (b * nt + i, 0)),
                  pl.BlockSpec((WINDOW, ATT_KV_W), lambda b, i: (prev_block(b, i), 0)),
                  col(2 * ATT_KV_W),
                  pl.BlockSpec((None, 2 * ATT_KV_W, WINDOW),
                               lambda b, i: (prev_block(b, i) // sub, 0, prev_block(b, i) % sub)),
                  col(ML_QK_W), col(ML_QK_W), col(ML_V_W), col(ML_V_W), col(N_GATES),
                  pl.BlockSpec((tq, D_MODEL), lambda b, i: (b * nt + i, 0)),
                  once(wout), full(mlnw_col)],
        out_specs=[pl.BlockSpec((tq, D_MODEL), lambda b, i: (b * nt + i, 0)),
                   per_batch(ML_HEADS, ML_QK_DIM, ML_V_DIM),
                   per_batch(ML_HEADS // 2, LANES),
                   per_batch(1, ML_HEADS),
                   per_batch(ATT_KV_W, WINDOW),
                   per_batch(ATT_KV_W, WINDOW)],
        out_shape=[jax.ShapeDtypeStruct((batch * seq, D_MODEL), F32),
                   jax.ShapeDtypeStruct((batch, ML_HEADS, ML_QK_DIM, ML_V_DIM), F32),
                   jax.ShapeDtypeStruct((batch, ML_HEADS // 2, LANES), F32),
                   jax.ShapeDtypeStruct((batch, 1, ML_HEADS), F32),
                   jax.ShapeDtypeStruct((batch, ATT_KV_W, WINDOW), F32),
                   jax.ShapeDtypeStruct((batch, ATT_KV_W, WINDOW), F32)],
        scratch_shapes=[pltpu.VMEM((tq // ML_CHUNK, D_MODEL, ML_CHUNK), BF16),
                        pltpu.VMEM((ML_HEADS // 2, 2 * LANES, LANES), F32),
                        pltpu.VMEM((SUBLANES, LANES), F32),
                        pltpu.VMEM((2, 2 * WINDOW, WINDOW), F32),
                        pltpu.VMEM((ML_CHUNK, ML_CHUNK), F32),
                        pltpu.VMEM((ML_CHUNK, ML_CHUNK), BF16),
                        pltpu.VMEM((2, 2 * WINDOW, ATT_HEADS * WINDOW), F32),
                        pltpu.VMEM((2, 2 * WINDOW, ATT_HEADS * WINDOW), F32),
                        pltpu.VMEM((2 * WINDOW, ATT_HEADS * WINDOW), BF16)],
        compiler_params=pltpu.CompilerParams(dimension_semantics=("arbitrary", "arbitrary"),
                                             vmem_limit_bytes=VMEM_LIMIT),
        name="prompt_mixer_t",
    )(sinks, qa, ks, ks, kv, kv, qm, km, vm, om, gt, x2d, wout, mlnw_col)


def _sample_mixer_kernel(t_len, sinks_ref, qa_ref, kv_ref, ck_ref, cv_ref, qm_ref, km_ref,
                         vm_ref, om_ref, gt_ref, c0_ref, n0_ref, m0_ref, x_ref, wout_ref,
                         mlnw_ref, x1_ref, nk_ref, nv_ref, c_ref, n_ref, m_ref,
                         mix_scr, wperm_scr):
    bt = SAMPLE_BT
    T = t_len
    L = bt * T

    @pl.when(pl.program_id(0) == 0)
    def _():
        _permute_head_rows(wperm_scr, wout_ref)
        wperm_scr[ATT_Q_W:, :] = wout_ref[ATT_Q_W:, :]

    lane3 = lax.broadcasted_iota(jnp.int32, (bt, T, LANES), 2)
    low3 = lane3 < ATT_HEAD_DIM
    lane = lax.broadcasted_iota(jnp.int32, (L, LANES), 1)
    low = lane < ATT_HEAD_DIM

    qa3 = qa_ref[...].astype(F32).reshape(bt, T, ATT_Q_W)
    pieces = []
    for col in range(ATT_GROUP):
        qc = qa3[:, :, col * LANES:(col + 1) * LANES]
        pieces += [jnp.where(low3, qc, 0.0), jnp.where(low3, 0.0, qc)]
    q3 = jnp.concatenate(pieces, axis=1).astype(BF16)
    R = bt * N_STACK * T
    q2 = q3.reshape(R, LANES)
    kv_new = kv_ref[...]
    k_new = kv_new[:, :ATT_KV_W]
    v_new = kv_new[:, ATT_KV_W:]
    ck = ck_ref[...]
    cv = cv_ref[...]
    s_c = jnp.einsum('bqd,bdk->bqk', q3, ck.astype(BF16),
                     preferred_element_type=F32).reshape(R, WINDOW)
    s_n = _dot_nt(q2, k_new.astype(BF16))
    row_c = lax.broadcasted_iota(jnp.int32, (R, WINDOW), 0)
    col_c = lax.broadcasted_iota(jnp.int32, (R, WINDOW), 1)
    s_c = jnp.where(col_c > row_c % T, s_c, -jnp.inf)
    row_n = lax.broadcasted_iota(jnp.int32, (R, L), 0)
    col_n = lax.broadcasted_iota(jnp.int32, (R, L), 1)
    valid_n = (row_n // (N_STACK * T) == col_n // T) & (col_n % T <= row_n % T)
    s_n = jnp.where(valid_n, s_n, -jnp.inf)
    stack_id = (lax.broadcasted_iota(jnp.int32, (R, 1), 0) // T) % N_STACK
    sink = jnp.zeros((R, 1), F32)
    for k_id in range(N_STACK):
        sink = jnp.where(stack_id == k_id, sinks_ref[ATT_HEAD_ORDER[k_id]], sink)
    m = jnp.maximum(jnp.maximum(jnp.max(s_c, axis=-1, keepdims=True),
                                jnp.max(s_n, axis=-1, keepdims=True)), sink)
    e_c = jnp.exp(s_c - m)
    e_n = jnp.exp(s_n - m)
    denom = (jnp.sum(e_c, axis=-1, keepdims=True) + jnp.sum(e_n, axis=-1, keepdims=True)
             + jnp.exp(sink - m))
    o = jnp.einsum('bqk,bdk->bqd', e_c.astype(BF16).reshape(bt, N_STACK * T, WINDOW),
                   cv.astype(BF16), preferred_element_type=F32).reshape(R, LANES)
    o = (o + _dot(e_n.astype(BF16), v_new.astype(BF16))) / denom
    o3 = o.reshape(bt, N_STACK * T, LANES)
    for col in range(ATT_GROUP):
        lo_h = o3[:, (2 * col) * T:(2 * col + 1) * T, :]
        hi_h = o3[:, (2 * col + 1) * T:(2 * col + 2) * T, :]
        mix_scr[:, col * LANES:(col + 1) * LANES] = jnp.where(
            low3, lo_h, hi_h).reshape(L, LANES).astype(BF16)

    keep = lax.broadcasted_iota(jnp.int32, (ATT_KV_W, WINDOW), 1) < WINDOW - T
    k_new_t = k_new.T
    v_new_t = v_new.T
    for q in range(bt):
        shift = (WINDOW - T - q * T) % WINDOW
        nk_ref[q] = jnp.where(keep, pltpu.roll(ck[q], WINDOW - T, axis=1),
                              pltpu.roll(k_new_t, shift, axis=1))
        nv_ref[q] = jnp.where(keep, pltpu.roll(cv[q], WINDOW - T, axis=1),
                              pltpu.roll(v_new_t, shift, axis=1))

    r = lax.broadcasted_iota(jnp.int32, (L, L), 0)
    c = lax.broadcasted_iota(jnp.int32, (L, L), 1)
    seg = (r // T == c // T) & (r <= c)
    seg_bias = jnp.where(seg, 0.0, -jnp.inf)
    seg_bf = seg.astype(F32).astype(BF16)
    gates = gt_ref[...] * LOG2_E
    cum_row = jnp.zeros(gates.shape, F32)
    for part in _split3(gates):
        cum_row = cum_row + _dot(part, seg_bf)
    ig_rows = gates[:ML_HEADS]
    b_rows = cum_row[ML_HEADS:]
    gate_cols = jnp.concatenate([ig_rows, b_rows, jnp.zeros((LANES - N_GATES, L), F32)],
                                axis=0).T

    def col_to_row(x_col):
        return jnp.broadcast_to(x_col, (L, LANES)).T[0:1, :]

    ones_rows = (r[:LANES] == 0).astype(F32).astype(BF16)
    qm = qm_ref[...]
    km = km_ref[...]
    qm_f = qm.astype(F32)
    km_f = km.astype(F32)
    n_rep = bt * ML_QK_DIM // LANES
    bd_row = lax.broadcasted_iota(jnp.int32, (L, bt * ML_QK_DIM), 0) // T
    bd_lane = lax.broadcasted_iota(jnp.int32, (L, bt * ML_QK_DIM), 1) // ML_QK_DIM
    block_diag = bd_row == bd_lane

    def spread(x_pair, e):
        other = pltpu.roll(x_pair, ML_QK_DIM, axis=1)
        twice = jnp.where(low, x_pair, other) if e == 0 else jnp.where(low, other, x_pair)
        return jnp.where(block_diag, jnp.concatenate([twice] * n_rep, axis=1), 0.0).astype(BF16)

    for h in range(ML_HEADS):
        p, e = divmod(h, 2)
        qc = qm[:, p * LANES:(p + 1) * LANES]
        k_pair = km[:, p * LANES:(p + 1) * LANES]
        zero = jnp.zeros_like(qc)
        q_pad = jnp.where(low, qc, zero) if e == 0 else jnp.where(low, zero, qc)
        v_h = vm_ref[:, h * ML_V_DIM:(h + 1) * ML_V_DIM]
        v_ext_t = jnp.concatenate([v_h.astype(F32).T.astype(BF16), ones_rows], axis=0)
        ig_c = gate_cols[:, h:h + 1]
        b_c = gate_cols[:, ML_HEADS + h:ML_HEADS + h + 1]
        b_r = b_rows[h:h + 1, :]
        m0 = m0_ref[:, :, h:h + 1] * LOG2_E
        inter = b_r + col_to_row(jnp.broadcast_to(m0, (bt, T, 1)).reshape(L, 1))
        dm = (b_r + (ig_c - b_c)) + seg_bias
        m_row = jnp.maximum(inter, jnp.max(dm, axis=0, keepdims=True))
        w_inter = jnp.exp2(inter - m_row)
        p_t = (_dot_nt(k_pair, q_pad) * jnp.exp2(dm - m_row)).astype(BF16)
        num_t = _dot(v_ext_t, p_t)
        q_h3 = qm_f[:, h * ML_QK_DIM:(h + 1) * ML_QK_DIM].reshape(bt, T, ML_QK_DIM)
        k_h3 = km_f[:, h * ML_QK_DIM:(h + 1) * ML_QK_DIM].reshape(bt, T, ML_QK_DIM)
        c0 = c0_ref[:, h]
        n0 = n0_ref[:, h:h + 1, :]
        q_c_t = _dot(spread(qm_f[:, p * LANES:(p + 1) * LANES], e),
                     c0.astype(BF16).reshape(bt * ML_QK_DIM, ML_V_DIM)).T
        q_n_r = col_to_row(jnp.sum(q_h3 * n0, axis=-1, keepdims=True).reshape(L, 1))
        num = num_t[:ML_V_DIM] + w_inter * q_c_t
        den = num_t[ML_V_DIM:ML_V_DIM + 1] + w_inter * q_n_r
        hh = num * (1.0 / jnp.maximum(jnp.abs(den), jnp.exp2(-m_row)))
        ms = jnp.mean(hh * hh, axis=0, keepdims=True)
        mix_scr[:, ATT_Q_W + h * ML_V_DIM:ATT_Q_W + (h + 1) * ML_V_DIM] = (
            (hh * lax.rsqrt(ms + EPS)).T * mlnw_ref[:, h * ML_V_DIM:(h + 1) * ML_V_DIM]
            * _sigmoid(om_ref[:, h * ML_V_DIM:(h + 1) * ML_V_DIM].astype(F32))).astype(BF16)
        b3 = b_c.reshape(bt, T, 1)
        b_last = b3[:, T - 1:T, :]
        a3 = b_last - b3 + ig_c.reshape(bt, T, 1)
        m_new = jnp.maximum(b_last + m0, jnp.max(a3, axis=1, keepdims=True))
        sc = jnp.exp2(b_last + m0 - m_new)
        ws = jnp.exp2(a3 - m_new)
        kw = spread(km_f[:, p * LANES:(p + 1) * LANES] * ws.reshape(L, 1), e)
        d_c = lax.dot_general(kw, v_h, (((0,), (0,)), ((), ())), preferred_element_type=F32)
        c_ref[:, h] = sc * c0 + d_c.reshape(bt, ML_QK_DIM, ML_V_DIM)
        n_ref[:, h:h + 1, :] = sc * n0 + jnp.sum(ws * k_h3, axis=1, keepdims=True)
        m_ref[:, :, h:h + 1] = m_new * (1.0 / LOG2_E)

    x1_ref[...] = x_ref[...] + _dot(mix_scr[...], wperm_scr[...])


def _sample_mixer(nb, t_len, sinks, qa, kv, ck, cv, qm, km, vm, om, gt, c0, n0, m0, x2d, wout, mlnw):
    bt = SAMPLE_BT
    tl = bt * t_len
    row = lambda w: pl.BlockSpec((tl, w), lambda i: (i, 0))
    full = lambda a: pl.BlockSpec(a.shape, lambda i: (0,) * a.ndim)
    once = lambda a: pl.BlockSpec(a.shape, lambda i: (0,) * a.ndim, pipeline_mode=pl.Buffered(1))
    cache = pl.BlockSpec((bt, ATT_KV_W, WINDOW), lambda i: (i, 0, 0))
    c_spec = pl.BlockSpec((bt, ML_HEADS, ML_QK_DIM, ML_V_DIM), lambda i: (i, 0, 0, 0))
    n_spec = pl.BlockSpec((bt, ML_HEADS, ML_QK_DIM), lambda i: (i, 0, 0))
    m_spec = pl.BlockSpec((bt, 1, ML_HEADS), lambda i: (i, 0, 0))
    return pl.pallas_call(
        functools.partial(_sample_mixer_kernel, t_len),
        grid=(nb // bt,),
        in_specs=[pl.BlockSpec(memory_space=pltpu.SMEM),
                  row(ATT_Q_W), row(2 * ATT_KV_W), cache, cache, row(ML_QK_W), row(ML_QK_W),
                  row(ML_V_W), row(ML_V_W), pl.BlockSpec((N_GATES, tl), lambda i: (0, i)),
                  c_spec, n_spec, m_spec, row(D_MODEL), once(wout), full(mlnw)],
        out_specs=[row(D_MODEL), cache, cache, c_spec, n_spec, m_spec],
        out_shape=[jax.ShapeDtypeStruct((nb * t_len, D_MODEL), F32),
                   jax.ShapeDtypeStruct((nb, ATT_KV_W, WINDOW), F32),
                   jax.ShapeDtypeStruct((nb, ATT_KV_W, WINDOW), F32),
                   jax.ShapeDtypeStruct((nb, ML_HEADS, ML_QK_DIM, ML_V_DIM), F32),
                   jax.ShapeDtypeStruct((nb, ML_HEADS, ML_QK_DIM), F32),
                   jax.ShapeDtypeStruct((nb, 1, ML_HEADS), F32)],
        scratch_shapes=[pltpu.VMEM((tl, D_MODEL), BF16),
                        pltpu.VMEM((D_MODEL, D_MODEL), BF16)],
        compiler_params=pltpu.CompilerParams(dimension_semantics=("arbitrary",),
                                             vmem_limit_bytes=VMEM_LIMIT),
        name="sample_mixer",
    )(sinks, qa, kv, ck, cv, qm, km, vm, om, gt, c0, n0, m0, x2d, wout, mlnw)


def _ffn_kernel(seq_rows, *refs):
    if seq_rows is None:
        (x_ref, nw_ref, w_ref, cw_ref, cb_ref, wd_ref, y_ref, conv_ref,
         gbuf, act_scr, carry) = refs
        hist_ref = None
    else:
        (x_ref, hist_ref, nw_ref, w_ref, cw_ref, cb_ref, wd_ref, y_ref, conv_ref,
         gbuf, act_scr) = refs
        carry = None
    tm = x_ref.shape[0]
    tf = FF_CHUNK
    n_hist = CONV_W - 1
    rows = tm if seq_rows is None else seq_rows
    nseq = tm // rows
    base = SUBLANES
    n_chunks = D_FF // tf

    if carry is not None:
        @pl.when(pl.program_id(1) == 0)
        def _():
            carry[...] = jnp.zeros(carry.shape, F32)

    x = x_ref[...]
    h2 = _rms(x, nw_ref[...]).astype(BF16)

    def proj(f):
        return (_dot(h2, w_ref[:, f * tf:(f + 1) * tf]),
                _dot(h2, w_ref[:, D_FF + f * tf:D_FF + (f + 1) * tf]))

    nxt = proj(0)
    for f in range(n_chunks):
        g, u = nxt
        if f + 1 < n_chunks:
            nxt = proj(f + 1)
        cols = slice(f * tf, (f + 1) * tf)
        s = f % 2
        g3 = g.reshape(nseq, rows, tf)
        if seq_rows is None:
            gbuf[s, :, base - n_hist:base, :] = carry[:, SUBLANES - n_hist:, cols]
            carry[:, SUBLANES - n_hist:, cols] = g3[:, rows - n_hist:, :]
        else:
            gbuf[s, :, base - n_hist:base, :] = hist_ref[:, :, cols]
            conv_ref[:, :, cols] = g3[:, rows - n_hist:, :]
        gbuf[s, :, base:base + rows, :] = g3
        gc = cb_ref[:, cols] + g * cw_ref[CONV_W - 1:CONV_W, cols]
        for d in range(1, CONV_W):
            gm = gbuf[s, :, base - d:base - d + rows, :].reshape(tm, tf)
            gc = gc + gm * cw_ref[CONV_W - 1 - d:CONV_W - d, cols]
        act_scr[:, cols] = (gc * _sigmoid(gc) * u).astype(BF16)
    y_ref[...] = x + _dot(act_scr[...], wd_ref[...])

    if carry is not None:
        @pl.when(pl.program_id(1) == pl.num_programs(1) - 1)
        def _():
            conv_ref[...] = carry[:, SUBLANES - n_hist:, :]


def _ffn_scratch(tm, rows):
    return [pltpu.VMEM((2, tm // rows, SUBLANES + rows, FF_CHUNK), F32),
            pltpu.VMEM((tm, D_FF), BF16)]


def _ffn_prompt(batch, seq, x2d, nw, w, cw, cb, wd):
    tm = ROW_TILE
    nt = seq // tm
    full = lambda a: pl.BlockSpec(a.shape, lambda b, i: (0,) * a.ndim)
    once = lambda a: pl.BlockSpec(a.shape, lambda b, i: (0,) * a.ndim,
                                  pipeline_mode=pl.Buffered(1))
    row = pl.BlockSpec((tm, D_MODEL), lambda b, i: (b * nt + i, 0))
    return pl.pallas_call(
        functools.partial(_ffn_kernel, None),
        grid=(batch, nt),
        in_specs=[row, full(nw), once(w), full(cw), full(cb), once(wd)],
        out_specs=[row, pl.BlockSpec((1, CONV_W - 1, D_FF), lambda b, i: (b, 0, 0))],
        out_shape=[jax.ShapeDtypeStruct((batch * seq, D_MODEL), F32),
                   jax.ShapeDtypeStruct((batch, CONV_W - 1, D_FF), F32)],
        scratch_shapes=_ffn_scratch(tm, tm) + [pltpu.VMEM((1, SUBLANES, D_FF), F32)],
        compiler_params=pltpu.CompilerParams(dimension_semantics=("arbitrary", "arbitrary"),
                                             vmem_limit_bytes=VMEM_LIMIT),
        name="ffn_prompt",
    )(x2d, nw, w, cw, cb, wd)


def _ffn_sample(nb, t_len, x2d, hist, nw, w, cw, cb, wd):
    tm = ROW_TILE
    bt = tm // t_len
    full = lambda a: pl.BlockSpec(a.shape, lambda i: (0,) * a.ndim)
    once = lambda a: pl.BlockSpec(a.shape, lambda i: (0,) * a.ndim, pipeline_mode=pl.Buffered(1))
    row = pl.BlockSpec((tm, D_MODEL), lambda i: (i, 0))
    hist_spec = pl.BlockSpec((bt, CONV_W - 1, D_FF), lambda i: (i, 0, 0))
    return pl.pallas_call(
        functools.partial(_ffn_kernel, t_len),
        grid=(nb // bt,),
        in_specs=[row, hist_spec, full(nw), once(w), full(cw), full(cb), once(wd)],
        out_specs=[row, hist_spec],
        out_shape=[jax.ShapeDtypeStruct((nb * t_len, D_MODEL), F32),
                   jax.ShapeDtypeStruct((nb, CONV_W - 1, D_FF), F32)],
        scratch_shapes=_ffn_scratch(tm, t_len),
        compiler_params=pltpu.CompilerParams(dimension_semantics=("arbitrary",),
                                             vmem_limit_bytes=VMEM_LIMIT),
        name="ffn_sample",
    )(x2d, hist, nw, w, cw, cb, wd)


def _head_mean_matrix(width, head_dim):
    idx = np.arange(width) // head_dim
    return jnp.asarray((idx[:, None] == idx[None, :]).astype(np.float32) / head_dim, dtype=BF16)


def _layer_weights(norm_mix_w, w_in, b_gates, q_norm_w, k_norm_w, sinks, ml_norm_w, w_out,
                   norm_ffn_w, w_ffn_in, conv_w, conv_b, w_down):
    w_in_t = jnp.pad(w_in.T.astype(BF16), ((0, IN_WIDTH_PAD - w_in.shape[1]), (0, 0)))
    return dict(
        nw=norm_mix_w.reshape(1, D_MODEL),
        w_in_t=w_in_t,
        bg=jnp.pad(b_gates, (0, LANES - N_GATES)).reshape(1, LANES),
        qnw=(jnp.tile(q_norm_w, ATT_HEADS) * ATT_SCALE).reshape(1, ATT_Q_W),
        knw=jnp.tile(k_norm_w, ATT_KV_HEADS).reshape(1, ATT_KV_W),
        gq=_head_mean_matrix(ATT_Q_W, ATT_HEAD_DIM),
        gk=_head_mean_matrix(ATT_KV_W, ATT_HEAD_DIM),
        bg_col=b_gates.reshape(N_GATES, 1),
        qnw_col=(jnp.tile(q_norm_w, ATT_HEADS) * (ATT_SCALE * LOG2_E)).reshape(ATT_Q_W, 1),
        knw_col=jnp.tile(k_norm_w, ATT_KV_HEADS).reshape(ATT_KV_W, 1),
        mlnw_col=ml_norm_w.reshape(ML_V_W, 1),
        sinks=sinks,
        mlnw=ml_norm_w.reshape(1, ML_V_W),
        wout=w_out.astype(BF16),
        nfw=norm_ffn_w.reshape(1, D_MODEL),
        wff=w_ffn_in.astype(BF16),
        cw=conv_w,
        cb=conv_b.reshape(1, D_FF),
        wd=w_down.astype(BF16),
    )


def _cache_from_t(a_t):
    n = a_t.shape[0]
    return jnp.transpose(a_t.reshape(n, ATT_KV_HEADS, ATT_HEAD_DIM, WINDOW), (0, 3, 1, 2))


def _cache_to_t(a):
    n = a.shape[0]
    return jnp.transpose(a, (0, 2, 3, 1)).reshape(n, ATT_KV_W, WINDOW)


def _prompt_layer(x, w):
    batch, seq, _ = x.shape
    assert seq % ROW_TILE == 0 and MIX_TILE == ROW_TILE and MIX_TILE % WINDOW == 0
    x2d = x.reshape(batch * seq, D_MODEL)
    qa, ks, kv, qm, km, vm, om, gt = _inproj_t(x2d, w["nw"], w["w_in_t"], w["bg_col"],
                                               w["qnw_col"], w["knw_col"])
    x1, c_t, n_row, m, k_t, v_t = _prompt_mixer_t(batch, seq, w["sinks"], qa, ks, kv, qm, km, vm,
                                                  om, gt, x2d, w["wout"], w["mlnw_col"])
    y, conv = _ffn_prompt(batch, seq, x1, w["nfw"], w["wff"], w["cw"], w["cb"], w["wd"])
    return (y.reshape(batch, seq, D_MODEL), _cache_from_t(k_t), _cache_from_t(v_t),
            jnp.swapaxes(c_t, -1, -2), n_row.reshape(batch, ML_HEADS, ML_QK_DIM),
            m.reshape(batch, ML_HEADS), conv)


def _sample_layer(x, ck, cv, c0, n0, m0, conv_buf, w):
    nb, t_len, _ = x.shape
    assert t_len == SUBLANES and SAMPLE_BT * t_len == LANES and nb % SAMPLE_BT == 0
    assert (nb * t_len) % ROW_TILE == 0
    x2d = x.reshape(nb * t_len, D_MODEL)
    qa, kv, qm, km, vm, om, gt = _inproj(x2d, w["nw"], w["w_in_t"], w["bg"], w["qnw"], w["knw"],
                                         w["gq"], w["gk"])
    x1, nk_t, nv_t, c_t, n, m = _sample_mixer(
        nb, t_len, w["sinks"], qa, kv, _cache_to_t(ck), _cache_to_t(cv), qm, km, vm, om, gt,
        jnp.swapaxes(c0, -1, -2), n0, m0.reshape(nb, 1, ML_HEADS), x2d, w["wout"], w["mlnw"])
    y, conv = _ffn_sample(nb, t_len, x1, conv_buf, w["nfw"], w["wff"], w["cw"], w["cb"],
                          w["wd"])
    return (y.reshape(nb, t_len, D_MODEL), _cache_from_t(nk_t), _cache_from_t(nv_t),
            jnp.swapaxes(c_t, -1, -2), n, m.reshape(nb, ML_HEADS), conv)


def kernel(x_prompt, x_sample, cache_attn_k, cache_attn_v, state_mlstm_C, state_mlstm_n,
           state_mlstm_m, cache_ffn_conv, norm_mix_w, w_in, b_gates, q_norm_w, k_norm_w,
           sinks, ml_norm_w, w_out, norm_ffn_w, w_ffn_in, conv_w, conv_b, w_down):
    depth = w_in.shape[0]
    yp, ys = x_prompt, x_sample
    sp = [[] for _ in range(6)]
    ss = [[] for _ in range(6)]
    for l in range(depth):
        w = _layer_weights(norm_mix_w[l], w_in[l], b_gates[l], q_norm_w[l], k_norm_w[l], sinks[l],
                           ml_norm_w[l], w_out[l], norm_ffn_w[l], w_ffn_in[l], conv_w[l],
                           conv_b[l], w_down[l])
        yp, *st_p = _prompt_layer(yp, w)
        ys, *st_s = _sample_layer(ys, cache_attn_k[l], cache_attn_v[l], state_mlstm_C[l],
                                  state_mlstm_n[l], state_mlstm_m[l], cache_ffn_conv[l], w)
        for i in range(6):
            sp[i].append(st_p[i])
            ss[i].append(st_s[i])
    k_p, v_p, c_p, n_p, m_p, conv_p = [jnp.stack(a) for a in sp]
    k_s, v_s, c_s, n_s, m_s, conv_s = [jnp.stack(a) for a in ss]
    return (yp, ys, k_p, v_p, c_p, n_p, m_p, conv_p, k_s, v_s, c_s, n_s, m_s, conv_s)
```

```python
import functools

import numpy as np
import jax
import jax.numpy as jnp
from jax import lax
from jax.experimental import pallas as pl
from jax.experimental.pallas import tpu as pltpu

F32 = jnp.float32
BF16 = jnp.bfloat16

D_MODEL = 1024
ATT_HEADS = 8
ATT_KV_HEADS = 2
ATT_HEAD_DIM = 64
ATT_GROUP = ATT_HEADS // ATT_KV_HEADS
WINDOW = 128
ML_HEADS = 4
ML_V_DIM = 128
ML_QK_DIM = 64
D_FF = 2816
CONV_W = 3
EPS = 1e-6
ATT_SCALE = ATT_HEAD_DIM ** -0.5
ML_SCALE = ML_QK_DIM ** -0.5
LOG2_E = 1.4426950408889634

ATT_Q_W = ATT_HEADS * ATT_HEAD_DIM
ATT_KV_W = ATT_KV_HEADS * ATT_HEAD_DIM
ML_QK_W = ML_HEADS * ML_QK_DIM
ML_V_W = ML_HEADS * ML_V_DIM
N_GATES = 2 * ML_HEADS
N_STACK = 2 * ATT_GROUP

LANES = 128
SUBLANES = 8

OFF_QA = 0
OFF_KV = OFF_QA + ATT_Q_W
OFF_QM = OFF_KV + 2 * ATT_KV_W
OFF_KM = OFF_QM + ML_QK_W
OFF_VM = OFF_KM + ML_QK_W
OFF_OM = OFF_VM + ML_V_W
OFF_GL = OFF_OM + ML_V_W
IN_WIDTH_PAD = OFF_GL + LANES

ATT_HEAD_ORDER = tuple(h for c in range(ATT_GROUP) for h in (c, c + ATT_GROUP))

ROW_TILE = 512
FFN_TILE = 1024
INPROJ_SUB = 256
MIX_TILE = 512
ML_CHUNK = 256
FF_CHUNK = 256
SAMPLE_BT = 16
VMEM_LIMIT = 56 * 1024 * 1024


def _dot(a, b):
    return jnp.dot(a, b, preferred_element_type=F32)


def _dot_nt(a, b):
    return lax.dot_general(a, b, (((1,), (1,)), ((), ())), preferred_element_type=F32)


def _split3(x):
    hi = x.astype(BF16)
    r1 = x - hi.astype(F32)
    mid = r1.astype(BF16)
    lo = (r1 - mid.astype(F32)).astype(BF16)
    return hi, mid, lo


def _rms(x, w):
    ms = jnp.mean(x * x, axis=-1, keepdims=True)
    return x * lax.rsqrt(ms + EPS) * w


def _log_sigmoid(x):
    return jnp.minimum(x, 0.0) - jnp.log1p(jnp.exp(-jnp.abs(x)))


def _sigmoid(x):
    return 1.0 / (1.0 + jnp.exp(-x))


def _permute_head_rows(dst_ref, src_ref):
    for k, h in enumerate(ATT_HEAD_ORDER):
        dst_ref[k * ATT_HEAD_DIM:(k + 1) * ATT_HEAD_DIM, :] = (
            src_ref[h * ATT_HEAD_DIM:(h + 1) * ATT_HEAD_DIM, :])


def _inproj_kernel(x_ref, nw_ref, w_ref, bg_ref, qnw_ref, knw_ref, gq_ref, gk_ref,
                   qa_ref, kv_ref, qm_ref, km_ref, vm_ref, om_ref, gt_ref, wq_scr):
    @pl.when(pl.program_id(0) == 0)
    def _():
        _permute_head_rows(wq_scr, w_ref)

    h = _rms(x_ref[...], nw_ref[...]).astype(BF16)

    def proj(lo, width):
        return _dot_nt(h, w_ref[lo:lo + width, :])

    q = _dot_nt(h, wq_scr[...])
    q_ms = _dot((q * q).astype(BF16), gq_ref[...])
    qa_ref[...] = (q * lax.rsqrt(q_ms + EPS) * qnw_ref[...]).astype(BF16)

    kv = proj(OFF_KV, 2 * ATT_KV_W)
    k = kv[:, :ATT_KV_W]
    k_ms = _dot((k * k).astype(BF16), gk_ref[...])
    kv_ref[:, :ATT_KV_W] = k * lax.rsqrt(k_ms + EPS) * knw_ref[...]
    kv_ref[:, ATT_KV_W:] = kv[:, ATT_KV_W:]

    qm_ref[...] = (proj(OFF_QM, ML_QK_W) * ML_SCALE).astype(BF16)
    km_ref[...] = proj(OFF_KM, ML_QK_W).astype(BF16)
    vm_ref[...] = proj(OFF_VM, ML_V_W).astype(BF16)
    om_ref[...] = proj(OFF_OM, ML_V_W).astype(BF16)

    gl = proj(OFF_GL, LANES) + bg_ref[...]
    lane = lax.broadcasted_iota(jnp.int32, gl.shape, 1)
    g = jnp.where(lane < ML_HEADS, gl, _log_sigmoid(gl))
    gt_ref[...] = g.T[:N_GATES, :]


def _inproj(x2d, nw, w_in_t, bg, qnw, knw, gq, gk):
    n = x2d.shape[0]
    tm = ROW_TILE
    row = lambda w: pl.BlockSpec((tm, w), lambda i: (i, 0))
    full = lambda a: pl.BlockSpec(a.shape, lambda i: (0,) * a.ndim)
    once = lambda a: pl.BlockSpec(a.shape, lambda i: (0,) * a.ndim, pipeline_mode=pl.Buffered(1))
    return pl.pallas_call(
        _inproj_kernel,
        grid=(n // tm,),
        in_specs=[row(D_MODEL), full(nw), once(w_in_t), full(bg), full(qnw), full(knw),
                  full(gq), full(gk)],
        out_specs=[row(ATT_Q_W), row(2 * ATT_KV_W), row(ML_QK_W), row(ML_QK_W),
                   row(ML_V_W), row(ML_V_W), pl.BlockSpec((N_GATES, tm), lambda i: (0, i))],
        out_shape=[jax.ShapeDtypeStruct((n, ATT_Q_W), BF16),
                   jax.ShapeDtypeStruct((n, 2 * ATT_KV_W), F32),
                   jax.ShapeDtypeStruct((n, ML_QK_W), BF16),
                   jax.ShapeDtypeStruct((n, ML_QK_W), BF16),
                   jax.ShapeDtypeStruct((n, ML_V_W), BF16),
                   jax.ShapeDtypeStruct((n, ML_V_W), BF16),
                   jax.ShapeDtypeStruct((N_GATES, n), F32)],
        scratch_shapes=[pltpu.VMEM((ATT_Q_W, D_MODEL), BF16)],
        compiler_params=pltpu.CompilerParams(dimension_semantics=("arbitrary",),
                                             vmem_limit_bytes=VMEM_LIMIT),
        name="inproj",
    )(x2d, nw, w_in_t, bg, qnw, knw, gq, gk)


def _head_norm_t(z, head_dim, w_col):
    rows, tokens = z.shape
    z3 = z.reshape(rows // head_dim, head_dim, tokens)
    ms = jnp.mean(z3 * z3, axis=1, keepdims=True)
    return (z3 * lax.rsqrt(ms + EPS)).reshape(rows, tokens) * w_col


def _inproj_t_kernel(x_ref, nw_ref, w_ref, bg_ref, qnw_ref, knw_ref,
                     qa_ref, ks_ref, kv_ref, qm_ref, km_ref, vm_ref, om_ref, gt_ref):
    tm = x_ref.shape[0]
    sub = INPROJ_SUB
    hs = [_rms(x_ref[c * sub:(c + 1) * sub, :], nw_ref[...]).astype(BF16)
          for c in range(tm // sub)]
    for c, h in enumerate(hs):
        tok = slice(c * sub, (c + 1) * sub)

        def proj(lo, width):
            return _dot_nt(w_ref[lo:lo + width, :], h)

        qa_ref[:, tok] = _head_norm_t(proj(OFF_QA, ATT_Q_W), ATT_HEAD_DIM,
                                      qnw_ref[...]).astype(BF16)
        kv = proj(OFF_KV, 2 * ATT_KV_W)
        k = _head_norm_t(kv[:ATT_KV_W], ATT_HEAD_DIM, knw_ref[...])
        kv_ref[:ATT_KV_W, tok] = k
        kv_ref[ATT_KV_W:, tok] = kv[ATT_KV_W:]
        ks_ref[tok, :] = k.T.astype(BF16)
        qm_ref[:, tok] = (proj(OFF_QM, ML_QK_W) * ML_SCALE).astype(BF16)
        km_ref[:, tok] = proj(OFF_KM, ML_QK_W).astype(BF16)
        vm_ref[:, tok] = proj(OFF_VM, ML_V_W).astype(BF16)
        om_ref[:, tok] = proj(OFF_OM, ML_V_W).astype(BF16)
        gl = proj(OFF_GL, 2 * SUBLANES)[:N_GATES] + bg_ref[...]
        row = lax.broadcasted_iota(jnp.int32, gl.shape, 0)
        gt_ref[:, tok] = jnp.where(row < ML_HEADS, gl, _log_sigmoid(gl))


def _inproj_t(x2d, nw, w_in_t, bg_col, qnw_col, knw_col):
    n = x2d.shape[0]
    tm = ROW_TILE
    full = lambda a: pl.BlockSpec(a.shape, lambda i: (0,) * a.ndim)
    once = lambda a: pl.BlockSpec(a.shape, lambda i: (0,) * a.ndim, pipeline_mode=pl.Buffered(1))
    col = lambda w: pl.BlockSpec((None, w, tm), lambda i: (i, 0, 0))
    slab = lambda w, dt: jax.ShapeDtypeStruct((n // tm, w, tm), dt)
    return pl.pallas_call(
        _inproj_t_kernel,
        grid=(n // tm,),
        in_specs=[pl.BlockSpec((tm, D_MODEL), lambda i: (i, 0)), full(nw), once(w_in_t),
                  full(bg_col), full(qnw_col), full(knw_col)],
        out_specs=[col(ATT_Q_W), pl.BlockSpec((tm, ATT_KV_W), lambda i: (i, 0)),
                   col(2 * ATT_KV_W), col(ML_QK_W), col(ML_QK_W), col(ML_V_W), col(ML_V_W),
                   col(N_GATES)],
        out_shape=[slab(ATT_Q_W, BF16),
                   jax.ShapeDtypeStruct((n, ATT_KV_W), BF16),
                   slab(2 * ATT_KV_W, F32), slab(ML_QK_W, BF16), slab(ML_QK_W, BF16),
                   slab(ML_V_W, BF16), slab(ML_V_W, BF16), slab(N_GATES, F32)],
        compiler_params=pltpu.CompilerParams(dimension_semantics=("arbitrary",),
                                             vmem_limit_bytes=VMEM_LIMIT),
        name="inproj_t",
    )(x2d, nw, w_in_t, bg_col, qnw_col, knw_col)


def _gate_forms(gates, seg_mask, want_raw_col):
    L = gates.shape[1]
    m_bf = seg_mask.astype(F32).astype(BF16)
    cum_row = jnp.zeros(gates.shape, F32)
    cum_col = jnp.zeros((L, gates.shape[0]), F32)
    raw_col = None
    if want_raw_col:
        r = lax.broadcasted_iota(jnp.int32, (L, L), 0)
        c = lax.broadcasted_iota(jnp.int32, (L, L), 1)
        eye = (r == c).astype(F32).astype(BF16)
        raw_col = jnp.zeros((L, gates.shape[0]), F32)
    for part in _split3(gates):
        cum_row = cum_row + _dot_nt(part, m_bf)
        cum_col = cum_col + _dot_nt(m_bf, part)
        if want_raw_col:
            raw_col = raw_col + _dot_nt(eye, part)
    return cum_row, cum_col, raw_col


def _mlstm_intra(q_pad, k_pair, v_ext, seg_mask, b_c, b_r, ig_r, m_prev_c):
    dm = jnp.where(seg_mask, b_c + (ig_r - b_r), -jnp.inf)
    inter = b_c + m_prev_c
    m_row = jnp.maximum(inter, jnp.max(dm, axis=-1, keepdims=True))
    w_inter = jnp.exp(inter - m_row)
    p = _dot_nt(q_pad, k_pair) * jnp.exp(dm - m_row)
    return _dot(p.astype(BF16), v_ext), m_row, w_inter


def _mlstm_out(pv, m_row, w_inter, q_c, q_n, mlnw_h, om_h):
    num = pv[:, :ML_V_DIM] + w_inter * q_c
    den = pv[:, ML_V_DIM:ML_V_DIM + 1] + w_inter * q_n
    hh = num / jnp.maximum(jnp.abs(den), jnp.exp(-m_row))
    return (_rms(hh, mlnw_h) * _sigmoid(om_h.astype(F32))).astype(BF16)


def _ones_col(rows):
    lane = lax.broadcasted_iota(jnp.int32, (rows, LANES), 1)
    return (lane == 0).astype(F32).astype(BF16)


def _prompt_mixer_t_kernel(sinks_ref, qa_ref, ksc_ref, ksp_ref, kvc_ref, kvp_ref, qm_ref, km_ref,
                           vm_ref, om_ref, gt_ref, x_ref, wout_ref, mlnw_ref,
                           x1_ref, ct_ref, nrow_ref, m_ref, kt_ref, vt_ref,
                           mix_scr, state_scr, m_scr, band_scr, causal_scr, tri_scr,
                           s_scr_a, s_scr_b, e_scr):
    i = pl.program_id(1)
    A = WINDOW
    L = MIX_TILE
    C = ML_CHUNK
    n_pairs = ML_HEADS // 2

    @pl.when(i == 0)
    def _():
        state_scr[...] = jnp.zeros(state_scr.shape, F32)
        m_scr[...] = jnp.zeros(m_scr.shape, F32)
        kj = lax.broadcasted_iota(jnp.int32, (2 * A, A), 0)
        qi = lax.broadcasted_iota(jnp.int32, (2 * A, A), 1)
        band = (kj > qi) & (kj <= qi + WINDOW)
        band_scr[0] = jnp.where(band, 0.0, -jnp.inf)
        band_scr[1] = jnp.where(band & (kj >= A), 0.0, -jnp.inf)
        r = lax.broadcasted_iota(jnp.int32, (C, C), 0)
        c = lax.broadcasted_iota(jnp.int32, (C, C), 1)
        causal_scr[...] = jnp.where(r <= c, 0.0, -jnp.inf)
        tri_scr[...] = (r <= c).astype(F32).astype(BF16)

    k_all = jnp.concatenate([ksp_ref[...], ksc_ref[...]], axis=0)
    v_all = jnp.concatenate([kvp_ref[ATT_KV_W:, :], kvc_ref[ATT_KV_W:, :]], axis=1).astype(BF16)
    zero_q = jnp.zeros((ATT_HEAD_DIM, A), BF16)
    slot = 0
    s_bufs = (s_scr_a, s_scr_b)

    def stage_scores(j):
        pieces = []
        for h in range(ATT_HEADS):
            q_h = qa_ref[h * ATT_HEAD_DIM:(h + 1) * ATT_HEAD_DIM, j * A:(j + 1) * A]
            pieces.append(jnp.concatenate([q_h, zero_q] if h < ATT_GROUP else [zero_q, q_h],
                                          axis=0))
        s_bufs[j % 2][slot] = _dot(k_all[j * A:(j + 2) * A, :], jnp.concatenate(pieces, axis=1))

    def attend(j):
        cols = slice(j * A, (j + 1) * A)
        vt = v_all[:, j * A:(j + 2) * A]
        if j + 1 < L // A:
            stage_scores(j + 1)
        s_buf = s_bufs[j % 2]
        bias = jnp.where(i > 0, band_scr[0], band_scr[1]) if j == 0 else band_scr[0]
        m_rows = []
        for h in range(ATT_HEADS):
            sb = s_buf[slot, :, h * A:(h + 1) * A] + bias
            m_rows.append(jnp.maximum(jnp.max(sb, axis=0, keepdims=True),
                                      sinks_ref[h] * LOG2_E))
        inv_rows = []
        for h in range(ATT_HEADS):
            e = jnp.exp2(s_buf[slot, :, h * A:(h + 1) * A] + (bias - m_rows[h]))
            e_scr[:, h * A:(h + 1) * A] = e.astype(BF16)
            inv_rows.append(1.0 / (jnp.sum(e, axis=0, keepdims=True)
                                   + jnp.exp2(sinks_ref[h] * LOG2_E - m_rows[h])))
        o = _dot(vt, e_scr[...])
        for h in range(ATT_HEADS):
            g = h // ATT_GROUP
            mix_scr[h * ATT_HEAD_DIM:(h + 1) * ATT_HEAD_DIM, cols] = (
                o[g * ATT_HEAD_DIM:(g + 1) * ATT_HEAD_DIM, h * A:(h + 1) * A]
                * inv_rows[h]).astype(BF16)

    row128 = lax.broadcasted_iota(jnp.int32, (LANES, C), 0)
    ones_rows = (row128 == 0).astype(F32).astype(BF16)

    def mlstm_chunk(ci):
        tok = slice(ci * C, (ci + 1) * C)
        gates = gt_ref[:, tok] * LOG2_E
        cum_row = jnp.zeros(gates.shape, F32)
        for part in _split3(gates):
            cum_row = cum_row + _dot(part, tri_scr[...])
        ig_rows = gates[:ML_HEADS]
        b_rows = cum_row[ML_HEADS:]
        key_cols = jnp.concatenate([ig_rows - b_rows, jnp.zeros((LANES - ML_HEADS, C), F32)],
                                   axis=0).T
        for p in range(n_pairs):
            q_c = qm_ref[p * LANES:(p + 1) * LANES, tok]
            k_pair = km_ref[p * LANES:(p + 1) * LANES, tok]
            zero = jnp.zeros_like(q_c)
            state = state_scr[p]
            state_bf = state.astype(BF16)
            new_state = []
            for e_id in range(2):
                h = 2 * p + e_id
                v_rows = slice(h * ML_V_DIM, (h + 1) * ML_V_DIM)
                head_rows = (row128 < ML_QK_DIM) if e_id == 0 else (row128 >= ML_QK_DIM)
                q_pad = jnp.where(head_rows, q_c, zero)
                b_r = b_rows[h:h + 1, :]
                ig_r = ig_rows[h:h + 1, :]
                m_prev = m_scr[h:h + 1, 0:1]
                dm = (b_r + key_cols[:, h:h + 1]) + causal_scr[...]
                inter = b_r + m_prev
                m_row = jnp.maximum(inter, jnp.max(dm, axis=0, keepdims=True))
                w_inter = jnp.exp2(inter - m_row)
                qk = lax.dot_general(k_pair, q_pad, (((0,), (0,)), ((), ())),
                                     preferred_element_type=F32)
                p_t = (qk * jnp.exp2(dm - m_row)).astype(BF16)
                v_ext = jnp.concatenate([vm_ref[v_rows, tok], ones_rows], axis=0)
                num = _dot(v_ext, p_t) + w_inter * _dot(state_bf, q_pad)
                den = num[ML_V_DIM:ML_V_DIM + 1, :]
                hh = num[:ML_V_DIM] * (1.0 / jnp.maximum(jnp.abs(den), jnp.exp2(-m_row)))
                ms = jnp.mean(hh * hh, axis=0, keepdims=True)
                gate = _sigmoid(om_ref[v_rows, tok].astype(F32))
                mix_scr[ATT_Q_W + h * ML_V_DIM:ATT_Q_W + (h + 1) * ML_V_DIM, tok] = (
                    hh * lax.rsqrt(ms + EPS) * mlnw_ref[v_rows, :] * gate).astype(BF16)
                b_last = b_r[:, C - 1:C]
                a_r = b_last - b_r + ig_r
                m_new = jnp.maximum(b_last + m_prev, jnp.max(a_r, axis=-1, keepdims=True))
                sc = jnp.exp2(b_last + m_prev - m_new)
                wsv = (v_ext.astype(F32) * jnp.exp2(a_r - m_new)).astype(BF16)
                new_state.append(sc * state + _dot_nt(wsv, k_pair))
                m_scr[h:h + 1, :] = jnp.broadcast_to(m_new, (1, LANES))
            first = lax.broadcasted_iota(jnp.int32, state.shape, 1) < ML_QK_DIM
            state_scr[p] = jnp.where(first, new_state[0], new_state[1])

    stage_scores(0)
    for j in range(L // A):
        attend(j)
    for ci in range(L // C):
        mlstm_chunk(ci)
    x1_ref[...] = x_ref[...] + lax.dot_general(
        mix_scr[...], wout_ref[...], (((0,), (0,)), ((), ())), preferred_element_type=F32)

    @pl.when(i == pl.num_programs(1) - 1)
    def _():
        for p in range(n_pairs):
            c_t = state_scr[p, :ML_V_DIM, :].T
            for e_id in range(2):
                ct_ref[0, 2 * p + e_id] = c_t[e_id * ML_QK_DIM:(e_id + 1) * ML_QK_DIM, :]
            nrow_ref[0, p:p + 1, :] = state_scr[p, ML_V_DIM:ML_V_DIM + 1, :]
        for h in range(ML_HEADS):
            m_ref[0, :, h:h + 1] = m_scr[h:h + 1, 0:1] * (1.0 / LOG2_E)
        kt_ref[0] = kvc_ref[:ATT_KV_W, L - WINDOW:]
        vt_ref[0] = kvc_ref[ATT_KV_W:, L - WINDOW:]


def _prompt_mixer_t(batch, seq, sinks, qa, ks, kv, qm, km, vm, om, gt, x2d, wout, mlnw_col):
    tq = MIX_TILE
    nt = seq // tq
    sub = tq // WINDOW
    col = lambda w: pl.BlockSpec((None, w, tq), lambda b, i: (b * nt + i, 0, 0))
    full = lambda a: pl.BlockSpec(a.shape, lambda b, i: (0,) * a.ndim)
    once = lambda a: pl.BlockSpec(a.shape, lambda b, i: (0,) * a.ndim,
                                  pipeline_mode=pl.Buffered(1))
    prev_block = lambda b, i: jnp.maximum((b * nt + i) * sub - 1, 0)
    per_batch = lambda *dims: pl.BlockSpec((1,) + dims, lambda b, i: (b,) + (0,) * len(dims))
    return pl.pallas_call(
        _prompt_mixer_t_kernel,
        grid=(batch, nt),
        in_specs=[pl.BlockSpec(memory_space=pltpu.SMEM),
                  col(ATT_Q_W),
                  pl.BlockSpec((tq, ATT_KV_W), lambda b, i: (b * nt + i, 0)),
                  pl.BlockSpec((WINDOW, ATT_KV_W), lambda b, i: (prev_block(b, i), 0)),
                  col(2 * ATT_KV_W),
                  pl.BlockSpec((None, 2 * ATT_KV_W, WINDOW),
                               lambda b, i: (prev_block(b, i) // sub, 0, prev_block(b, i) % sub)),
                  col(ML_QK_W), col(ML_QK_W), col(ML_V_W), col(ML_V_W), col(N_GATES),
                  pl.BlockSpec((tq, D_MODEL), lambda b, i: (b * nt + i, 0)),
                  once(wout), full(mlnw_col)],
        out_specs=[pl.BlockSpec((tq, D_MODEL), lambda b, i: (b * nt + i, 0)),
                   per_batch(ML_HEADS, ML_QK_DIM, ML_V_DIM),
                   per_batch(ML_HEADS // 2, LANES),
                   per_batch(1, ML_HEADS),
                   per_batch(ATT_KV_W, WINDOW),
                   per_batch(ATT_KV_W, WINDOW)],
        out_shape=[jax.ShapeDtypeStruct((batch * seq, D_MODEL), F32),
                   jax.ShapeDtypeStruct((batch, ML_HEADS, ML_QK_DIM, ML_V_DIM), F32),
                   jax.ShapeDtypeStruct((batch, ML_HEADS // 2, LANES), F32),
                   jax.ShapeDtypeStruct((batch, 1, ML_HEADS), F32),
                   jax.ShapeDtypeStruct((batch, ATT_KV_W, WINDOW), F32),
                   jax.ShapeDtypeStruct((batch, ATT_KV_W, WINDOW), F32)],
        scratch_shapes=[pltpu.VMEM((D_MODEL, tq), BF16),
                        pltpu.VMEM((ML_HEADS // 2, 2 * LANES, LANES), F32),
                        pltpu.VMEM((SUBLANES, LANES), F32),
                        pltpu.VMEM((2, 2 * WINDOW, WINDOW), F32),
                        pltpu.VMEM((ML_CHUNK, ML_CHUNK), F32),
                        pltpu.VMEM((ML_CHUNK, ML_CHUNK), BF16),
                        pltpu.VMEM((2, 2 * WINDOW, ATT_HEADS * WINDOW), F32),
                        pltpu.VMEM((2, 2 * WINDOW, ATT_HEADS * WINDOW), F32),
                        pltpu.VMEM((2 * WINDOW, ATT_HEADS * WINDOW), BF16)],
        compiler_params=pltpu.CompilerParams(dimension_semantics=("arbitrary", "arbitrary"),
                                             vmem_limit_bytes=VMEM_LIMIT),
        name="prompt_mixer_t",
    )(sinks, qa, ks, ks, kv, kv, qm, km, vm, om, gt, x2d, wout, mlnw_col)


def _sample_mixer_kernel(t_len, sinks_ref, qa_ref, kv_ref, ck_ref, cv_ref, qm_ref, km_ref,
                         vm_ref, om_ref, gt_ref, c0_ref, n0_ref, m0_ref, x_ref, wout_ref,
                         mlnw_ref, x1_ref, nk_ref, nv_ref, c_ref, n_ref, m_ref,
                         mix_scr, wperm_scr):
    bt = SAMPLE_BT
    T = t_len
    L = bt * T

    @pl.when(pl.program_id(0) == 0)
    def _():
        _permute_head_rows(wperm_scr, wout_ref)
        wperm_scr[ATT_Q_W:, :] = wout_ref[ATT_Q_W:, :]

    lane3 = lax.broadcasted_iota(jnp.int32, (bt, T, LANES), 2)
    low3 = lane3 < ATT_HEAD_DIM
    lane = lax.broadcasted_iota(jnp.int32, (L, LANES), 1)
    low = lane < ATT_HEAD_DIM

    qa3 = qa_ref[...].astype(F32).reshape(bt, T, ATT_Q_W)
    pieces = []
    for col in range(ATT_GROUP):
        qc = qa3[:, :, col * LANES:(col + 1) * LANES]
        pieces += [jnp.where(low3, qc, 0.0), jnp.where(low3, 0.0, qc)]
    q3 = jnp.concatenate(pieces, axis=1).astype(BF16)
    R = bt * N_STACK * T
    q2 = q3.reshape(R, LANES)
    kv_new = kv_ref[...]
    k_new = kv_new[:, :ATT_KV_W]
    v_new = kv_new[:, ATT_KV_W:]
    ck = ck_ref[...]
    cv = cv_ref[...]
    s_c = jnp.einsum('bqd,bdk->bqk', q3, ck.astype(BF16),
                     preferred_element_type=F32).reshape(R, WINDOW)
    s_n = _dot_nt(q2, k_new.astype(BF16))
    row_c = lax.broadcasted_iota(jnp.int32, (R, WINDOW), 0)
    col_c = lax.broadcasted_iota(jnp.int32, (R, WINDOW), 1)
    s_c = jnp.where(col_c > row_c % T, s_c, -jnp.inf)
    row_n = lax.broadcasted_iota(jnp.int32, (R, L), 0)
    col_n = lax.broadcasted_iota(jnp.int32, (R, L), 1)
    valid_n = (row_n // (N_STACK * T) == col_n // T) & (col_n % T <= row_n % T)
    s_n = jnp.where(valid_n, s_n, -jnp.inf)
    stack_id = (lax.broadcasted_iota(jnp.int32, (R, 1), 0) // T) % N_STACK
    sink = jnp.zeros((R, 1), F32)
    for k_id in range(N_STACK):
        sink = jnp.where(stack_id == k_id, sinks_ref[ATT_HEAD_ORDER[k_id]], sink)
    m = jnp.maximum(jnp.maximum(jnp.max(s_c, axis=-1, keepdims=True),
                                jnp.max(s_n, axis=-1, keepdims=True)), sink)
    e_c = jnp.exp(s_c - m)
    e_n = jnp.exp(s_n - m)
    denom = (jnp.sum(e_c, axis=-1, keepdims=True) + jnp.sum(e_n, axis=-1, keepdims=True)
             + jnp.exp(sink - m))
    o = jnp.einsum('bqk,bdk->bqd', e_c.astype(BF16).reshape(bt, N_STACK * T, WINDOW),
                   cv.astype(BF16), preferred_element_type=F32).reshape(R, LANES)
    o = (o + _dot(e_n.astype(BF16), v_new.astype(BF16))) / denom
    o3 = o.reshape(bt, N_STACK * T, LANES)
    for col in range(ATT_GROUP):
        lo_h = o3[:, (2 * col) * T:(2 * col + 1) * T, :]
        hi_h = o3[:, (2 * col + 1) * T:(2 * col + 2) * T, :]
        mix_scr[:, col * LANES:(col + 1) * LANES] = jnp.where(
            low3, lo_h, hi_h).reshape(L, LANES).astype(BF16)

    keep = lax.broadcasted_iota(jnp.int32, (ATT_KV_W, WINDOW), 1) < WINDOW - T
    k_new_t = k_new.T
    v_new_t = v_new.T
    for q in range(bt):
        shift = (WINDOW - T - q * T) % WINDOW
        nk_ref[q] = jnp.where(keep, pltpu.roll(ck[q], WINDOW - T, axis=1),
                              pltpu.roll(k_new_t, shift, axis=1))
        nv_ref[q] = jnp.where(keep, pltpu.roll(cv[q], WINDOW - T, axis=1),
                              pltpu.roll(v_new_t, shift, axis=1))

    r = lax.broadcasted_iota(jnp.int32, (L, L), 0)
    c = lax.broadcasted_iota(jnp.int32, (L, L), 1)
    seg = (r // T == c // T) & (r <= c)
    seg_bias = jnp.where(seg, 0.0, -jnp.inf)
    seg_bf = seg.astype(F32).astype(BF16)
    gates = gt_ref[...] * LOG2_E
    cum_row = jnp.zeros(gates.shape, F32)
    for part in _split3(gates):
        cum_row = cum_row + _dot(part, seg_bf)
    ig_rows = gates[:ML_HEADS]
    b_rows = cum_row[ML_HEADS:]
    gate_cols = jnp.concatenate([ig_rows, b_rows, jnp.zeros((LANES - N_GATES, L), F32)],
                                axis=0).T

    def col_to_row(x_col):
        return jnp.broadcast_to(x_col, (L, LANES)).T[0:1, :]

    ones_rows = (r[:LANES] == 0).astype(F32).astype(BF16)
    qm = qm_ref[...]
    km = km_ref[...]
    qm_f = qm.astype(F32)
    km_f = km.astype(F32)
    n_rep = bt * ML_QK_DIM // LANES
    bd_row = lax.broadcasted_iota(jnp.int32, (L, bt * ML_QK_DIM), 0) // T
    bd_lane = lax.broadcasted_iota(jnp.int32, (L, bt * ML_QK_DIM), 1) // ML_QK_DIM
    block_diag = bd_row == bd_lane

    def spread(x_pair, e):
        other = pltpu.roll(x_pair, ML_QK_DIM, axis=1)
        twice = jnp.where(low, x_pair, other) if e == 0 else jnp.where(low, other, x_pair)
        return jnp.where(block_diag, jnp.concatenate([twice] * n_rep, axis=1), 0.0).astype(BF16)

    for h in range(ML_HEADS):
        p, e = divmod(h, 2)
        qc = qm[:, p * LANES:(p + 1) * LANES]
        k_pair = km[:, p * LANES:(p + 1) * LANES]
        zero = jnp.zeros_like(qc)
        q_pad = jnp.where(low, qc, zero) if e == 0 else jnp.where(low, zero, qc)
        v_h = vm_ref[:, h * ML_V_DIM:(h + 1) * ML_V_DIM]
        v_ext_t = jnp.concatenate([v_h.astype(F32).T.astype(BF16), ones_rows], axis=0)
        ig_c = gate_cols[:, h:h + 1]
        b_c = gate_cols[:, ML_HEADS + h:ML_HEADS + h + 1]
        b_r = b_rows[h:h + 1, :]
        m0 = m0_ref[:, :, h:h + 1] * LOG2_E
        inter = b_r + col_to_row(jnp.broadcast_to(m0, (bt, T, 1)).reshape(L, 1))
        dm = (b_r + (ig_c - b_c)) + seg_bias
        m_row = jnp.maximum(inter, jnp.max(dm, axis=0, keepdims=True))
        w_inter = jnp.exp2(inter - m_row)
        p_t = (_dot_nt(k_pair, q_pad) * jnp.exp2(dm - m_row)).astype(BF16)
        num_t = _dot(v_ext_t, p_t)
        q_h3 = qm_f[:, h * ML_QK_DIM:(h + 1) * ML_QK_DIM].reshape(bt, T, ML_QK_DIM)
        k_h3 = km_f[:, h * ML_QK_DIM:(h + 1) * ML_QK_DIM].reshape(bt, T, ML_QK_DIM)
        c0 = c0_ref[:, h]
        n0 = n0_ref[:, h:h + 1, :]
        q_c_t = _dot(spread(qm_f[:, p * LANES:(p + 1) * LANES], e),
                     c0.astype(BF16).reshape(bt * ML_QK_DIM, ML_V_DIM)).T
        q_n_r = col_to_row(jnp.sum(q_h3 * n0, axis=-1, keepdims=True).reshape(L, 1))
        num = num_t[:ML_V_DIM] + w_inter * q_c_t
        den = num_t[ML_V_DIM:ML_V_DIM + 1] + w_inter * q_n_r
        hh = num * (1.0 / jnp.maximum(jnp.abs(den), jnp.exp2(-m_row)))
        ms = jnp.mean(hh * hh, axis=0, keepdims=True)
        mix_scr[:, ATT_Q_W + h * ML_V_DIM:ATT_Q_W + (h + 1) * ML_V_DIM] = (
            (hh * lax.rsqrt(ms + EPS)).T * mlnw_ref[:, h * ML_V_DIM:(h + 1) * ML_V_DIM]
            * _sigmoid(om_ref[:, h * ML_V_DIM:(h + 1) * ML_V_DIM].astype(F32))).astype(BF16)
        b3 = b_c.reshape(bt, T, 1)
        b_last = b3[:, T - 1:T, :]
        a3 = b_last - b3 + ig_c.reshape(bt, T, 1)
        m_new = jnp.maximum(b_last + m0, jnp.max(a3, axis=1, keepdims=True))
        sc = jnp.exp2(b_last + m0 - m_new)
        ws = jnp.exp2(a3 - m_new)
        kw = spread(km_f[:, p * LANES:(p + 1) * LANES] * ws.reshape(L, 1), e)
        d_c = lax.dot_general(kw, v_h, (((0,), (0,)), ((), ())), preferred_element_type=F32)
        c_ref[:, h] = sc * c0 + d_c.reshape(bt, ML_QK_DIM, ML_V_DIM)
        n_ref[:, h:h + 1, :] = sc * n0 + jnp.sum(ws * k_h3, axis=1, keepdims=True)
        m_ref[:, :, h:h + 1] = m_new * (1.0 / LOG2_E)

    x1_ref[...] = x_ref[...] + _dot(mix_scr[...], wperm_scr[...])


def _sample_mixer(nb, t_len, sinks, qa, kv, ck, cv, qm, km, vm, om, gt, c0, n0, m0, x2d, wout, mlnw):
    bt = SAMPLE_BT
    tl = bt * t_len
    row = lambda w: pl.BlockSpec((tl, w), lambda i: (i, 0))
    full = lambda a: pl.BlockSpec(a.shape, lambda i: (0,) * a.ndim)
    once = lambda a: pl.BlockSpec(a.shape, lambda i: (0,) * a.ndim, pipeline_mode=pl.Buffered(1))
    cache = pl.BlockSpec((bt, ATT_KV_W, WINDOW), lambda i: (i, 0, 0))
    c_spec = pl.BlockSpec((bt, ML_HEADS, ML_QK_DIM, ML_V_DIM), lambda i: (i, 0, 0, 0))
    n_spec = pl.BlockSpec((bt, ML_HEADS, ML_QK_DIM), lambda i: (i, 0, 0))
    m_spec = pl.BlockSpec((bt, 1, ML_HEADS), lambda i: (i, 0, 0))
    return pl.pallas_call(
        functools.partial(_sample_mixer_kernel, t_len),
        grid=(nb // bt,),
        in_specs=[pl.BlockSpec(memory_space=pltpu.SMEM),
                  row(ATT_Q_W), row(2 * ATT_KV_W), cache, cache, row(ML_QK_W), row(ML_QK_W),
                  row(ML_V_W), row(ML_V_W), pl.BlockSpec((N_GATES, tl), lambda i: (0, i)),
                  c_spec, n_spec, m_spec, row(D_MODEL), once(wout), full(mlnw)],
        out_specs=[row(D_MODEL), cache, cache, c_spec, n_spec, m_spec],
        out_shape=[jax.ShapeDtypeStruct((nb * t_len, D_MODEL), F32),
                   jax.ShapeDtypeStruct((nb, ATT_KV_W, WINDOW), F32),
                   jax.ShapeDtypeStruct((nb, ATT_KV_W, WINDOW), F32),
                   jax.ShapeDtypeStruct((nb, ML_HEADS, ML_QK_DIM, ML_V_DIM), F32),
                   jax.ShapeDtypeStruct((nb, ML_HEADS, ML_QK_DIM), F32),
                   jax.ShapeDtypeStruct((nb, 1, ML_HEADS), F32)],
        scratch_shapes=[pltpu.VMEM((tl, D_MODEL), BF16),
                        pltpu.VMEM((D_MODEL, D_MODEL), BF16)],
        compiler_params=pltpu.CompilerParams(dimension_semantics=("arbitrary",),
                                             vmem_limit_bytes=VMEM_LIMIT),
        name="sample_mixer",
    )(sinks, qa, kv, ck, cv, qm, km, vm, om, gt, c0, n0, m0, x2d, wout, mlnw)


def _ffn_kernel(seq_rows, *refs):
    if seq_rows is None:
        (x_ref, nw_ref, w_ref, cw_ref, cb_ref, wd_ref, y_ref, conv_ref,
         gbuf, act_scr, carry) = refs
        hist_ref = None
    else:
        (x_ref, hist_ref, nw_ref, w_ref, cw_ref, cb_ref, wd_ref, y_ref, conv_ref,
         gbuf, act_scr) = refs
        carry = None
    tm = x_ref.shape[0]
    tf = FF_CHUNK
    n_hist = CONV_W - 1
    rows = tm if seq_rows is None else seq_rows
    nseq = tm // rows
    base = SUBLANES
    n_chunks = D_FF // tf

    if carry is not None:
        @pl.when(pl.program_id(1) == 0)
        def _():
            carry[...] = jnp.zeros(carry.shape, F32)

    x = x_ref[...]
    h2 = _rms(x, nw_ref[...]).astype(BF16)

    def proj(f):
        return (_dot(h2, w_ref[:, f * tf:(f + 1) * tf]),
                _dot(h2, w_ref[:, D_FF + f * tf:D_FF + (f + 1) * tf]))

    nxt = proj(0)
    for f in range(n_chunks):
        g, u = nxt
        if f + 1 < n_chunks:
            nxt = proj(f + 1)
        cols = slice(f * tf, (f + 1) * tf)
        s = f % 2
        g3 = g.reshape(nseq, rows, tf)
        if seq_rows is None:
            gbuf[s, :, base - n_hist:base, :] = carry[:, SUBLANES - n_hist:, cols]
            carry[:, SUBLANES - n_hist:, cols] = g3[:, rows - n_hist:, :]
        else:
            gbuf[s, :, base - n_hist:base, :] = hist_ref[:, :, cols]
            conv_ref[:, :, cols] = g3[:, rows - n_hist:, :]
        gbuf[s, :, base:base + rows, :] = g3
        gc = cb_ref[:, cols] + g * cw_ref[CONV_W - 1:CONV_W, cols]
        for d in range(1, CONV_W):
            gm = gbuf[s, :, base - d:base - d + rows, :].reshape(tm, tf)
            gc = gc + gm * cw_ref[CONV_W - 1 - d:CONV_W - d, cols]
        act_scr[:, cols] = (gc * _sigmoid(gc) * u).astype(BF16)
    y_ref[...] = x + _dot(act_scr[...], wd_ref[...])

    if carry is not None:
        @pl.when(pl.program_id(1) == pl.num_programs(1) - 1)
        def _():
            conv_ref[...] = carry[:, SUBLANES - n_hist:, :]


def _ffn_scratch(tm, rows):
    return [pltpu.VMEM((2, tm // rows, SUBLANES + rows, FF_CHUNK), F32),
            pltpu.VMEM((tm, D_FF), BF16)]


def _ffn_prompt(batch, seq, x2d, nw, w, cw, cb, wd):
    tm = FFN_TILE
    nt = seq // tm
    full = lambda a: pl.BlockSpec(a.shape, lambda b, i: (0,) * a.ndim)
    once = lambda a: pl.BlockSpec(a.shape, lambda b, i: (0,) * a.ndim,
                                  pipeline_mode=pl.Buffered(1))
    row = pl.BlockSpec((tm, D_MODEL), lambda b, i: (b * nt + i, 0))
    return pl.pallas_call(
        functools.partial(_ffn_kernel, None),
        grid=(batch, nt),
        in_specs=[row, full(nw), once(w), full(cw), full(cb), once(wd)],
        out_specs=[row, pl.BlockSpec((1, CONV_W - 1, D_FF), lambda b, i: (b, 0, 0))],
        out_shape=[jax.ShapeDtypeStruct((batch * seq, D_MODEL), F32),
                   jax.ShapeDtypeStruct((batch, CONV_W - 1, D_FF), F32)],
        scratch_shapes=_ffn_scratch(tm, tm) + [pltpu.VMEM((1, SUBLANES, D_FF), F32)],
        compiler_params=pltpu.CompilerParams(dimension_semantics=("arbitrary", "arbitrary"),
                                             vmem_limit_bytes=VMEM_LIMIT),
        name="ffn_prompt",
    )(x2d, nw, w, cw, cb, wd)


def _ffn_sample(nb, t_len, x2d, hist, nw, w, cw, cb, wd):
    tm = ROW_TILE
    bt = tm // t_len
    full = lambda a: pl.BlockSpec(a.shape, lambda i: (0,) * a.ndim)
    once = lambda a: pl.BlockSpec(a.shape, lambda i: (0,) * a.ndim, pipeline_mode=pl.Buffered(1))
    row = pl.BlockSpec((tm, D_MODEL), lambda i: (i, 0))
    hist_spec = pl.BlockSpec((bt, CONV_W - 1, D_FF), lambda i: (i, 0, 0))
    return pl.pallas_call(
        functools.partial(_ffn_kernel, t_len),
        grid=(nb // bt,),
        in_specs=[row, hist_spec, full(nw), once(w), full(cw), full(cb), once(wd)],
        out_specs=[row, hist_spec],
        out_shape=[jax.ShapeDtypeStruct((nb * t_len, D_MODEL), F32),
                   jax.ShapeDtypeStruct((nb, CONV_W - 1, D_FF), F32)],
        scratch_shapes=_ffn_scratch(tm, t_len),
        compiler_params=pltpu.CompilerParams(dimension_semantics=("arbitrary",),
                                             vmem_limit_bytes=VMEM_LIMIT),
        name="ffn_sample",
    )(x2d, hist, nw, w, cw, cb, wd)


def _head_mean_matrix(width, head_dim):
    idx = np.arange(width) // head_dim
    return jnp.asarray((idx[:, None] == idx[None, :]).astype(np.float32) / head_dim, dtype=BF16)


def _layer_weights(norm_mix_w, w_in, b_gates, q_norm_w, k_norm_w, sinks, ml_norm_w, w_out,
                   norm_ffn_w, w_ffn_in, conv_w, conv_b, w_down):
    w_in_t = jnp.pad(w_in.T.astype(BF16), ((0, IN_WIDTH_PAD - w_in.shape[1]), (0, 0)))
    return dict(
        nw=norm_mix_w.reshape(1, D_MODEL),
        w_in_t=w_in_t,
        bg=jnp.pad(b_gates, (0, LANES - N_GATES)).reshape(1, LANES),
        qnw=(jnp.tile(q_norm_w, ATT_HEADS) * ATT_SCALE).reshape(1, ATT_Q_W),
        knw=jnp.tile(k_norm_w, ATT_KV_HEADS).reshape(1, ATT_KV_W),
        gq=_head_mean_matrix(ATT_Q_W, ATT_HEAD_DIM),
        gk=_head_mean_matrix(ATT_KV_W, ATT_HEAD_DIM),
        bg_col=b_gates.reshape(N_GATES, 1),
        qnw_col=(jnp.tile(q_norm_w, ATT_HEADS) * (ATT_SCALE * LOG2_E)).reshape(ATT_Q_W, 1),
        knw_col=jnp.tile(k_norm_w, ATT_KV_HEADS).reshape(ATT_KV_W, 1),
        mlnw_col=ml_norm_w.reshape(ML_V_W, 1),
        sinks=sinks,
        mlnw=ml_norm_w.reshape(1, ML_V_W),
        wout=w_out.astype(BF16),
        nfw=norm_ffn_w.reshape(1, D_MODEL),
        wff=w_ffn_in.astype(BF16),
        cw=conv_w,
        cb=conv_b.reshape(1, D_FF),
        wd=w_down.astype(BF16),
    )


def _cache_from_t(a_t):
    n = a_t.shape[0]
    return jnp.transpose(a_t.reshape(n, ATT_KV_HEADS, ATT_HEAD_DIM, WINDOW), (0, 3, 1, 2))


def _cache_to_t(a):
    n = a.shape[0]
    return jnp.transpose(a, (0, 2, 3, 1)).reshape(n, ATT_KV_W, WINDOW)


def _prompt_layer(x, w):
    batch, seq, _ = x.shape
    assert seq % ROW_TILE == 0 and MIX_TILE == ROW_TILE and MIX_TILE % WINDOW == 0
    assert seq % FFN_TILE == 0
    x2d = x.reshape(batch * seq, D_MODEL)
    qa, ks, kv, qm, km, vm, om, gt = _inproj_t(x2d, w["nw"], w["w_in_t"], w["bg_col"],
                                               w["qnw_col"], w["knw_col"])
    x1, c_t, n_row, m, k_t, v_t = _prompt_mixer_t(batch, seq, w["sinks"], qa, ks, kv, qm, km, vm,
                                                  om, gt, x2d, w["wout"], w["mlnw_col"])
    y, conv = _ffn_prompt(batch, seq, x1, w["nfw"], w["wff"], w["cw"], w["cb"], w["wd"])
    return (y.reshape(batch, seq, D_MODEL), _cache_from_t(k_t), _cache_from_t(v_t),
            jnp.swapaxes(c_t, -1, -2), n_row.reshape(batch, ML_HEADS, ML_QK_DIM),
            m.reshape(batch, ML_HEADS), conv)


def _sample_layer(x, ck, cv, c0, n0, m0, conv_buf, w):
    nb, t_len, _ = x.shape
    assert t_len == SUBLANES and SAMPLE_BT * t_len == LANES and nb % SAMPLE_BT == 0
    assert (nb * t_len) % ROW_TILE == 0
    x2d = x.reshape(nb * t_len, D_MODEL)
    qa, kv, qm, km, vm, om, gt = _inproj(x2d, w["nw"], w["w_in_t"], w["bg"], w["qnw"], w["knw"],
                                         w["gq"], w["gk"])
    x1, nk_t, nv_t, c_t, n, m = _sample_mixer(
        nb, t_len, w["sinks"], qa, kv, _cache_to_t(ck), _cache_to_t(cv), qm, km, vm, om, gt,
        jnp.swapaxes(c0, -1, -2), n0, m0.reshape(nb, 1, ML_HEADS), x2d, w["wout"], w["mlnw"])
    y, conv = _ffn_sample(nb, t_len, x1, conv_buf, w["nfw"], w["wff"], w["cw"], w["cb"],
                          w["wd"])
    return (y.reshape(nb, t_len, D_MODEL), _cache_from_t(nk_t), _cache_from_t(nv_t),
            jnp.swapaxes(c_t, -1, -2), n, m.reshape(nb, ML_HEADS), conv)


def kernel(x_prompt, x_sample, cache_attn_k, cache_attn_v, state_mlstm_C, state_mlstm_n,
           state_mlstm_m, cache_ffn_conv, norm_mix_w, w_in, b_gates, q_norm_w, k_norm_w,
           sinks, ml_norm_w, w_out, norm_ffn_w, w_ffn_in, conv_w, conv_b, w_down):
    depth = w_in.shape[0]
    yp, ys = x_prompt, x_sample
    sp = [[] for _ in range(6)]
    ss = [[] for _ in range(6)]
    for l in range(depth):
        w = _layer_weights(norm_mix_w[l], w_in[l], b_gates[l], q_norm_w[l], k_norm_w[l], sinks[l],
                           ml_norm_w[l], w_out[l], norm_ffn_w[l], w_ffn_in[l], conv_w[l],
                           conv_b[l], w_down[l])
        yp, *st_p = _prompt_layer(yp, w)
        ys, *st_s = _sample_layer(ys, cache_attn_k[l], cache_attn_v[l], state_mlstm_C[l],
                                  state_mlstm_n[l], state_mlstm_m[l], cache_ffn_conv[l], w)
        for i in range(6):
            sp[i].append(st_p[i])
            ss[i].append(st_s[i])
    k_p, v_p, c_p, n_p, m_p, conv_p = [jnp.stack(a) for a in sp]
    k_s, v_s, c_s, n_s, m_s, conv_s = [jnp.stack(a) for a in ss]
    return (yp, ys, k_p, v_p, c_p, n_p, m_p, conv_p, k_s, v_s, c_s, n_s, m_s, conv_s)
```

```python
import functools

import numpy as np
import jax
import jax.numpy as jnp
from jax import lax
from jax.experimental import pallas as pl
from jax.experimental.pallas import tpu as pltpu

F32 = jnp.float32
BF16 = jnp.bfloat16

D_MODEL = 1024
ATT_HEADS = 8
ATT_KV_HEADS = 2
ATT_HEAD_DIM = 64
ATT_GROUP = ATT_HEADS // ATT_KV_HEADS
WINDOW = 128
ML_HEADS = 4
ML_V_DIM = 128
ML_QK_DIM = 64
D_FF = 2816
CONV_W = 3
EPS = 1e-6
ATT_SCALE = ATT_HEAD_DIM ** -0.5
ML_SCALE = ML_QK_DIM ** -0.5
LOG2_E = 1.4426950408889634

ATT_Q_W = ATT_HEADS * ATT_HEAD_DIM
ATT_KV_W = ATT_KV_HEADS * ATT_HEAD_DIM
ML_QK_W = ML_HEADS * ML_QK_DIM
ML_V_W = ML_HEADS * ML_V_DIM
N_GATES = 2 * ML_HEADS
N_STACK = 2 * ATT_GROUP

LANES = 128
SUBLANES = 8

OFF_QA = 0
OFF_KV = OFF_QA + ATT_Q_W
OFF_QM = OFF_KV + 2 * ATT_KV_W
OFF_KM = OFF_QM + ML_QK_W
OFF_VM = OFF_KM + ML_QK_W
OFF_OM = OFF_VM + ML_V_W
OFF_GL = OFF_OM + ML_V_W
IN_WIDTH_PAD = OFF_GL + LANES

ATT_HEAD_ORDER = tuple(h for c in range(ATT_GROUP) for h in (c, c + ATT_GROUP))

ROW_TILE = 512
FFN_TILE = 1024
INPROJ_SUB = 256
MIX_TILE = 512
ML_CHUNK = 256
FF_CHUNK = 256
SAMPLE_BT = 16
VMEM_LIMIT = 56 * 1024 * 1024


def _dot(a, b):
    return jnp.dot(a, b, preferred_element_type=F32)


def _dot_nt(a, b):
    return lax.dot_general(a, b, (((1,), (1,)), ((), ())), preferred_element_type=F32)


def _split3(x):
    hi = x.astype(BF16)
    r1 = x - hi.astype(F32)
    mid = r1.astype(BF16)
    lo = (r1 - mid.astype(F32)).astype(BF16)
    return hi, mid, lo


def _rms(x, w):
    ms = jnp.mean(x * x, axis=-1, keepdims=True)
    return x * lax.rsqrt(ms + EPS) * w


def _log_sigmoid(x):
    return jnp.minimum(x, 0.0) - jnp.log1p(jnp.exp(-jnp.abs(x)))


def _sigmoid(x):
    return 1.0 / (1.0 + jnp.exp(-x))


def _permute_head_rows(dst_ref, src_ref):
    for k, h in enumerate(ATT_HEAD_ORDER):
        dst_ref[k * ATT_HEAD_DIM:(k + 1) * ATT_HEAD_DIM, :] = (
            src_ref[h * ATT_HEAD_DIM:(h + 1) * ATT_HEAD_DIM, :])


def _inproj_kernel(x_ref, nw_ref, w_ref, bg_ref, qnw_ref, knw_ref, gq_ref, gk_ref,
                   qa_ref, kv_ref, qm_ref, km_ref, vm_ref, om_ref, gt_ref, wq_scr):
    @pl.when(pl.program_id(0) == 0)
    def _():
        _permute_head_rows(wq_scr, w_ref)

    h = _rms(x_ref[...], nw_ref[...]).astype(BF16)

    def proj(lo, width):
        return _dot_nt(h, w_ref[lo:lo + width, :])

    q = _dot_nt(h, wq_scr[...])
    q_ms = _dot((q * q).astype(BF16), gq_ref[...])
    qa_ref[...] = (q * lax.rsqrt(q_ms + EPS) * qnw_ref[...]).astype(BF16)

    kv = proj(OFF_KV, 2 * ATT_KV_W)
    k = kv[:, :ATT_KV_W]
    k_ms = _dot((k * k).astype(BF16), gk_ref[...])
    kv_ref[:, :ATT_KV_W] = k * lax.rsqrt(k_ms + EPS) * knw_ref[...]
    kv_ref[:, ATT_KV_W:] = kv[:, ATT_KV_W:]

    qm_ref[...] = (proj(OFF_QM, ML_QK_W) * ML_SCALE).astype(BF16)
    km_ref[...] = proj(OFF_KM, ML_QK_W).astype(BF16)
    vm_ref[...] = proj(OFF_VM, ML_V_W).astype(BF16)
    om_ref[...] = proj(OFF_OM, ML_V_W).astype(BF16)

    gl = proj(OFF_GL, LANES) + bg_ref[...]
    lane = lax.broadcasted_iota(jnp.int32, gl.shape, 1)
    g = jnp.where(lane < ML_HEADS, gl, _log_sigmoid(gl))
    gt_ref[...] = g.T[:N_GATES, :]


def _inproj(x2d, nw, w_in_t, bg, qnw, knw, gq, gk):
    n = x2d.shape[0]
    tm = ROW_TILE
    row = lambda w: pl.BlockSpec((tm, w), lambda i: (i, 0))
    full = lambda a: pl.BlockSpec(a.shape, lambda i: (0,) * a.ndim)
    once = lambda a: pl.BlockSpec(a.shape, lambda i: (0,) * a.ndim, pipeline_mode=pl.Buffered(1))
    return pl.pallas_call(
        _inproj_kernel,
        grid=(n // tm,),
        in_specs=[row(D_MODEL), full(nw), once(w_in_t), full(bg), full(qnw), full(knw),
                  full(gq), full(gk)],
        out_specs=[row(ATT_Q_W), row(2 * ATT_KV_W), row(ML_QK_W), row(ML_QK_W),
                   row(ML_V_W), row(ML_V_W), pl.BlockSpec((N_GATES, tm), lambda i: (0, i))],
        out_shape=[jax.ShapeDtypeStruct((n, ATT_Q_W), BF16),
                   jax.ShapeDtypeStruct((n, 2 * ATT_KV_W), F32),
                   jax.ShapeDtypeStruct((n, ML_QK_W), BF16),
                   jax.ShapeDtypeStruct((n, ML_QK_W), BF16),
                   jax.ShapeDtypeStruct((n, ML_V_W), BF16),
                   jax.ShapeDtypeStruct((n, ML_V_W), BF16),
                   jax.ShapeDtypeStruct((N_GATES, n), F32)],
        scratch_shapes=[pltpu.VMEM((ATT_Q_W, D_MODEL), BF16)],
        compiler_params=pltpu.CompilerParams(dimension_semantics=("arbitrary",),
                                             vmem_limit_bytes=VMEM_LIMIT),
        name="inproj",
    )(x2d, nw, w_in_t, bg, qnw, knw, gq, gk)


def _head_norm_t(z, head_dim, w_col):
    rows, tokens = z.shape
    z3 = z.reshape(rows // head_dim, head_dim, tokens)
    ms = jnp.mean(z3 * z3, axis=1, keepdims=True)
    return (z3 * lax.rsqrt(ms + EPS)).reshape(rows, tokens) * w_col


def _inproj_t_kernel(x_ref, nw_ref, w_ref, bg_ref, qnw_ref, knw_ref,
                     qa_ref, ks_ref, kv_ref, qm_ref, km_ref, vm_ref, om_ref, gt_ref):
    tm = x_ref.shape[0]
    sub = INPROJ_SUB
    hs = [_rms(x_ref[c * sub:(c + 1) * sub, :], nw_ref[...]).astype(BF16)
          for c in range(tm // sub)]
    for c, h in enumerate(hs):
        tok = slice(c * sub, (c + 1) * sub)

        def proj(lo, width):
            return _dot_nt(w_ref[lo:lo + width, :], h)

        qa_ref[:, tok] = _head_norm_t(proj(OFF_QA, ATT_Q_W), ATT_HEAD_DIM,
                                      qnw_ref[...]).astype(BF16)
        kv = proj(OFF_KV, 2 * ATT_KV_W)
        k = _head_norm_t(kv[:ATT_KV_W], ATT_HEAD_DIM, knw_ref[...])
        kv_ref[:ATT_KV_W, tok] = k
        kv_ref[ATT_KV_W:, tok] = kv[ATT_KV_W:]
        ks_ref[tok, :] = k.T.astype(BF16)
        qm_ref[:, tok] = (proj(OFF_QM, ML_QK_W) * ML_SCALE).astype(BF16)
        km_ref[:, tok] = proj(OFF_KM, ML_QK_W).astype(BF16)
        vm_ref[:, tok] = proj(OFF_VM, ML_V_W).astype(BF16)
        om_ref[:, tok] = proj(OFF_OM, ML_V_W).astype(BF16)
        gl = proj(OFF_GL, 2 * SUBLANES)[:N_GATES] + bg_ref[...]
        row = lax.broadcasted_iota(jnp.int32, gl.shape, 0)
        gt_ref[:, tok] = jnp.where(row < ML_HEADS, gl, _log_sigmoid(gl))


def _inproj_t(x2d, nw, w_in_t, bg_col, qnw_col, knw_col):
    n = x2d.shape[0]
    tm = ROW_TILE
    full = lambda a: pl.BlockSpec(a.shape, lambda i: (0,) * a.ndim)
    once = lambda a: pl.BlockSpec(a.shape, lambda i: (0,) * a.ndim, pipeline_mode=pl.Buffered(1))
    col = lambda w: pl.BlockSpec((None, w, tm), lambda i: (i, 0, 0))
    slab = lambda w, dt: jax.ShapeDtypeStruct((n // tm, w, tm), dt)
    return pl.pallas_call(
        _inproj_t_kernel,
        grid=(n // tm,),
        in_specs=[pl.BlockSpec((tm, D_MODEL), lambda i: (i, 0)), full(nw), once(w_in_t),
                  full(bg_col), full(qnw_col), full(knw_col)],
        out_specs=[col(ATT_Q_W), pl.BlockSpec((tm, ATT_KV_W), lambda i: (i, 0)),
                   col(2 * ATT_KV_W), col(ML_QK_W), col(ML_QK_W), col(ML_V_W), col(ML_V_W),
                   col(N_GATES)],
        out_shape=[slab(ATT_Q_W, BF16),
                   jax.ShapeDtypeStruct((n, ATT_KV_W), BF16),
                   slab(2 * ATT_KV_W, F32), slab(ML_QK_W, BF16), slab(ML_QK_W, BF16),
                   slab(ML_V_W, BF16), slab(ML_V_W, BF16), slab(N_GATES, F32)],
        compiler_params=pltpu.CompilerParams(dimension_semantics=("arbitrary",),
                                             vmem_limit_bytes=VMEM_LIMIT),
        name="inproj_t",
    )(x2d, nw, w_in_t, bg_col, qnw_col, knw_col)


def _gate_forms(gates, seg_mask, want_raw_col):
    L = gates.shape[1]
    m_bf = seg_mask.astype(F32).astype(BF16)
    cum_row = jnp.zeros(gates.shape, F32)
    cum_col = jnp.zeros((L, gates.shape[0]), F32)
    raw_col = None
    if want_raw_col:
        r = lax.broadcasted_iota(jnp.int32, (L, L), 0)
        c = lax.broadcasted_iota(jnp.int32, (L, L), 1)
        eye = (r == c).astype(F32).astype(BF16)
        raw_col = jnp.zeros((L, gates.shape[0]), F32)
    for part in _split3(gates):
        cum_row = cum_row + _dot_nt(part, m_bf)
        cum_col = cum_col + _dot_nt(m_bf, part)
        if want_raw_col:
            raw_col = raw_col + _dot_nt(eye, part)
    return cum_row, cum_col, raw_col


def _mlstm_intra(q_pad, k_pair, v_ext, seg_mask, b_c, b_r, ig_r, m_prev_c):
    dm = jnp.where(seg_mask, b_c + (ig_r - b_r), -jnp.inf)
    inter = b_c + m_prev_c
    m_row = jnp.maximum(inter, jnp.max(dm, axis=-1, keepdims=True))
    w_inter = jnp.exp(inter - m_row)
    p = _dot_nt(q_pad, k_pair) * jnp.exp(dm - m_row)
    return _dot(p.astype(BF16), v_ext), m_row, w_inter


def _mlstm_out(pv, m_row, w_inter, q_c, q_n, mlnw_h, om_h):
    num = pv[:, :ML_V_DIM] + w_inter * q_c
    den = pv[:, ML_V_DIM:ML_V_DIM + 1] + w_inter * q_n
    hh = num / jnp.maximum(jnp.abs(den), jnp.exp(-m_row))
    return (_rms(hh, mlnw_h) * _sigmoid(om_h.astype(F32))).astype(BF16)


def _ones_col(rows):
    lane = lax.broadcasted_iota(jnp.int32, (rows, LANES), 1)
    return (lane == 0).astype(F32).astype(BF16)


def _prompt_mixer_t_kernel(sinks_ref, qa_ref, ksc_ref, ksp_ref, kvc_ref, kvp_ref, qm_ref, km_ref,
                           vm_ref, om_ref, gt_ref, x_ref, wout_ref, mlnw_ref,
                           x1_ref, ct_ref, nrow_ref, m_ref, kt_ref, vt_ref,
                           mix_scr, state_scr, m_scr, band_scr, causal_scr, tri_scr,
                           s_scr_a, s_scr_b, e_scr):
    i = pl.program_id(1)
    A = WINDOW
    L = MIX_TILE
    C = ML_CHUNK
    n_pairs = ML_HEADS // 2

    @pl.when(i == 0)
    def _():
        state_scr[...] = jnp.zeros(state_scr.shape, F32)
        m_scr[...] = jnp.zeros(m_scr.shape, F32)
        kj = lax.broadcasted_iota(jnp.int32, (2 * A, A), 0)
        qi = lax.broadcasted_iota(jnp.int32, (2 * A, A), 1)
        band = (kj > qi) & (kj <= qi + WINDOW)
        band_scr[0] = jnp.where(band, 0.0, -jnp.inf)
        band_scr[1] = jnp.where(band & (kj >= A), 0.0, -jnp.inf)
        r = lax.broadcasted_iota(jnp.int32, (C, C), 0)
        c = lax.broadcasted_iota(jnp.int32, (C, C), 1)
        causal_scr[...] = jnp.where(r <= c, 0.0, -jnp.inf)
        tri_scr[...] = (r <= c).astype(F32).astype(BF16)

    k_all = jnp.concatenate([ksp_ref[...], ksc_ref[...]], axis=0)
    v_all = jnp.concatenate([kvp_ref[ATT_KV_W:, :], kvc_ref[ATT_KV_W:, :]], axis=1).astype(BF16)
    zero_q = jnp.zeros((ATT_HEAD_DIM, A), BF16)
    slot = 0
    s_bufs = (s_scr_a, s_scr_b)

    def stage_scores(j):
        pieces = []
        for h in range(ATT_HEADS):
            q_h = qa_ref[h * ATT_HEAD_DIM:(h + 1) * ATT_HEAD_DIM, j * A:(j + 1) * A]
            pieces.append(jnp.concatenate([q_h, zero_q] if h < ATT_GROUP else [zero_q, q_h],
                                          axis=0))
        s_bufs[j % 2][slot] = _dot(k_all[j * A:(j + 2) * A, :], jnp.concatenate(pieces, axis=1))

    def attend(j):
        cols = slice(j * A, (j + 1) * A)
        vt = v_all[:, j * A:(j + 2) * A]
        if j + 1 < L // A:
            stage_scores(j + 1)
        s_buf = s_bufs[j % 2]
        bias = jnp.where(i > 0, band_scr[0], band_scr[1]) if j == 0 else band_scr[0]
        m_rows = []
        for h in range(ATT_HEADS):
            sb = s_buf[slot, :, h * A:(h + 1) * A] + bias
            m_rows.append(jnp.maximum(jnp.max(sb, axis=0, keepdims=True),
                                      sinks_ref[h] * LOG2_E))
        inv_rows = []
        for h in range(ATT_HEADS):
            e = jnp.exp2(s_buf[slot, :, h * A:(h + 1) * A] + (bias - m_rows[h]))
            e_scr[:, h * A:(h + 1) * A] = e.astype(BF16)
            inv_rows.append(1.0 / (jnp.sum(e, axis=0, keepdims=True)
                                   + jnp.exp2(sinks_ref[h] * LOG2_E - m_rows[h])))
        o = _dot(vt, e_scr[...])
        for h in range(ATT_HEADS):
            g = h // ATT_GROUP
            mix_scr[h * ATT_HEAD_DIM:(h + 1) * ATT_HEAD_DIM, cols] = (
                o[g * ATT_HEAD_DIM:(g + 1) * ATT_HEAD_DIM, h * A:(h + 1) * A]
                * inv_rows[h]).astype(BF16)

    row128 = lax.broadcasted_iota(jnp.int32, (LANES, C), 0)
    ones_rows = (row128 == 0).astype(F32).astype(BF16)

    def mlstm_chunk(ci):
        tok = slice(ci * C, (ci + 1) * C)
        gates = gt_ref[:, tok] * LOG2_E
        cum_row = jnp.zeros(gates.shape, F32)
        for part in _split3(gates):
            cum_row = cum_row + _dot(part, tri_scr[...])
        ig_rows = gates[:ML_HEADS]
        b_rows = cum_row[ML_HEADS:]
        key_cols = jnp.concatenate([ig_rows - b_rows, jnp.zeros((LANES - ML_HEADS, C), F32)],
                                   axis=0).T
        for p in range(n_pairs):
            q_c = qm_ref[p * LANES:(p + 1) * LANES, tok]
            k_pair = km_ref[p * LANES:(p + 1) * LANES, tok]
            zero = jnp.zeros_like(q_c)
            state = state_scr[p]
            state_bf = state.astype(BF16)
            new_state = []
            for e_id in range(2):
                h = 2 * p + e_id
                v_rows = slice(h * ML_V_DIM, (h + 1) * ML_V_DIM)
                head_rows = (row128 < ML_QK_DIM) if e_id == 0 else (row128 >= ML_QK_DIM)
                q_pad = jnp.where(head_rows, q_c, zero)
                b_r = b_rows[h:h + 1, :]
                ig_r = ig_rows[h:h + 1, :]
                m_prev = m_scr[h:h + 1, 0:1]
                dm = (b_r + key_cols[:, h:h + 1]) + causal_scr[...]
                inter = b_r + m_prev
                m_row = jnp.maximum(inter, jnp.max(dm, axis=0, keepdims=True))
                w_inter = jnp.exp2(inter - m_row)
                qk = lax.dot_general(k_pair, q_pad, (((0,), (0,)), ((), ())),
                                     preferred_element_type=F32)
                p_t = (qk * jnp.exp2(dm - m_row)).astype(BF16)
                v_ext = jnp.concatenate([vm_ref[v_rows, tok], ones_rows], axis=0)
                num = _dot(v_ext, p_t) + w_inter * _dot(state_bf, q_pad)
                den = num[ML_V_DIM:ML_V_DIM + 1, :]
                hh = num[:ML_V_DIM] * (1.0 / jnp.maximum(jnp.abs(den), jnp.exp2(-m_row)))
                ms = jnp.mean(hh * hh, axis=0, keepdims=True)
                gate = _sigmoid(om_ref[v_rows, tok].astype(F32))
                mix_scr[ATT_Q_W + h * ML_V_DIM:ATT_Q_W + (h + 1) * ML_V_DIM, tok] = (
                    hh * lax.rsqrt(ms + EPS) * mlnw_ref[v_rows, :] * gate).astype(BF16)
                b_last = b_r[:, C - 1:C]
                a_r = b_last - b_r + ig_r
                m_new = jnp.maximum(b_last + m_prev, jnp.max(a_r, axis=-1, keepdims=True))
                sc = jnp.exp2(b_last + m_prev - m_new)
                wsv = (v_ext.astype(F32) * jnp.exp2(a_r - m_new)).astype(BF16)
                new_state.append(sc * state + _dot_nt(wsv, k_pair))
                m_scr[h:h + 1, :] = jnp.broadcast_to(m_new, (1, LANES))
            first = lax.broadcasted_iota(jnp.int32, state.shape, 1) < ML_QK_DIM
            state_scr[p] = jnp.where(first, new_state[0], new_state[1])
            yield

    stage_scores(0)
    pairs = (step for ci in range(L // C) for step in mlstm_chunk(ci))
    for j in range(L // A):
        attend(j)
        next(pairs, None)
    for _ in pairs:
        pass
    x1_ref[...] = x_ref[...] + lax.dot_general(
        mix_scr[...], wout_ref[...], (((0,), (0,)), ((), ())), preferred_element_type=F32)

    @pl.when(i == pl.num_programs(1) - 1)
    def _():
        for p in range(n_pairs):
            c_t = state_scr[p, :ML_V_DIM, :].T
            for e_id in range(2):
                ct_ref[0, 2 * p + e_id] = c_t[e_id * ML_QK_DIM:(e_id + 1) * ML_QK_DIM, :]
            nrow_ref[0, p:p + 1, :] = state_scr[p, ML_V_DIM:ML_V_DIM + 1, :]
        for h in range(ML_HEADS):
            m_ref[0, :, h:h + 1] = m_scr[h:h + 1, 0:1] * (1.0 / LOG2_E)
        kt_ref[0] = kvc_ref[:ATT_KV_W, L - WINDOW:]
        vt_ref[0] = kvc_ref[ATT_KV_W:, L - WINDOW:]


def _prompt_mixer_t(batch, seq, sinks, qa, ks, kv, qm, km, vm, om, gt, x2d, wout, mlnw_col):
    tq = MIX_TILE
    nt = seq // tq
    sub = tq // WINDOW
    col = lambda w: pl.BlockSpec((None, w, tq), lambda b, i: (b * nt + i, 0, 0))
    full = lambda a: pl.BlockSpec(a.shape, lambda b, i: (0,) * a.ndim)
    once = lambda a: pl.BlockSpec(a.shape, lambda b, i: (0,) * a.ndim,
                                  pipeline_mode=pl.Buffered(1))
    prev_block = lambda b, i: jnp.maximum((b * nt + i) * sub - 1, 0)
    per_batch = lambda *dims: pl.BlockSpec((1,) + dims, lambda b, i: (b,) + (0,) * len(dims))
    return pl.pallas_call(
        _prompt_mixer_t_kernel,
        grid=(batch, nt),
        in_specs=[pl.BlockSpec(memory_space=pltpu.SMEM),
                  col(ATT_Q_W),
                  pl.BlockSpec((tq, ATT_KV_W), lambda b, i: (b * nt + i, 0)),
                  pl.BlockSpec((WINDOW, ATT_KV_W), lambda b, i: (prev_block(b, i), 0)),
                  col(2 * ATT_KV_W),
                  pl.BlockSpec((None, 2 * ATT_KV_W, WINDOW),
                               lambda b, i: (prev_block(b, i) // sub, 0, prev_block(b, i) % sub)),
                  col(ML_QK_W), col(ML_QK_W), col(ML_V_W), col(ML_V_W), col(N_GATES),
                  pl.BlockSpec((tq, D_MODEL), lambda b, i: (b * nt + i, 0)),
                  once(wout), full(mlnw_col)],
        out_specs=[pl.BlockSpec((tq, D_MODEL), lambda b, i: (b * nt + i, 0)),
                   per_batch(ML_HEADS, ML_QK_DIM, ML_V_DIM),
                   per_batch(ML_HEADS // 2, LANES),
                   per_batch(1, ML_HEADS),
                   per_batch(ATT_KV_W, WINDOW),
                   per_batch(ATT_KV_W, WINDOW)],
        out_shape=[jax.ShapeDtypeStruct((batch * seq, D_MODEL), F32),
                   jax.ShapeDtypeStruct((batch, ML_HEADS, ML_QK_DIM, ML_V_DIM), F32),
                   jax.ShapeDtypeStruct((batch, ML_HEADS // 2, LANES), F32),
                   jax.ShapeDtypeStruct((batch, 1, ML_HEADS), F32),
                   jax.ShapeDtypeStruct((batch, ATT_KV_W, WINDOW), F32),
                   jax.ShapeDtypeStruct((batch, ATT_KV_W, WINDOW), F32)],
        scratch_shapes=[pltpu.VMEM((D_MODEL, tq), BF16),
                        pltpu.VMEM((ML_HEADS // 2, 2 * LANES, LANES), F32),
                        pltpu.VMEM((SUBLANES, LANES), F32),
                        pltpu.VMEM((2, 2 * WINDOW, WINDOW), F32),
                        pltpu.VMEM((ML_CHUNK, ML_CHUNK), F32),
                        pltpu.VMEM((ML_CHUNK, ML_CHUNK), BF16),
                        pltpu.VMEM((2, 2 * WINDOW, ATT_HEADS * WINDOW), F32),
                        pltpu.VMEM((2, 2 * WINDOW, ATT_HEADS * WINDOW), F32),
                        pltpu.VMEM((2 * WINDOW, ATT_HEADS * WINDOW), BF16)],
        compiler_params=pltpu.CompilerParams(dimension_semantics=("arbitrary", "arbitrary"),
                                             vmem_limit_bytes=VMEM_LIMIT),
        name="prompt_mixer_t",
    )(sinks, qa, ks, ks, kv, kv, qm, km, vm, om, gt, x2d, wout, mlnw_col)


def _sample_mixer_kernel(t_len, sinks_ref, qa_ref, kv_ref, ck_ref, cv_ref, qm_ref, km_ref,
                         vm_ref, om_ref, gt_ref, c0_ref, n0_ref, m0_ref, x_ref, wout_ref,
                         mlnw_ref, x1_ref, nk_ref, nv_ref, c_ref, n_ref, m_ref,
                         mix_scr, wperm_scr):
    bt = SAMPLE_BT
    T = t_len
    L = bt * T

    @pl.when(pl.program_id(0) == 0)
    def _():
        _permute_head_rows(wperm_scr, wout_ref)
        wperm_scr[ATT_Q_W:, :] = wout_ref[ATT_Q_W:, :]

    lane3 = lax.broadcasted_iota(jnp.int32, (bt, T, LANES), 2)
    low3 = lane3 < ATT_HEAD_DIM
    lane = lax.broadcasted_iota(jnp.int32, (L, LANES), 1)
    low = lane < ATT_HEAD_DIM

    qa3 = qa_ref[...].astype(F32).reshape(bt, T, ATT_Q_W)
    pieces = []
    for col in range(ATT_GROUP):
        qc = qa3[:, :, col * LANES:(col + 1) * LANES]
        pieces += [jnp.where(low3, qc, 0.0), jnp.where(low3, 0.0, qc)]
    q3 = jnp.concatenate(pieces, axis=1).astype(BF16)
    R = bt * N_STACK * T
    q2 = q3.reshape(R, LANES)
    kv_new = kv_ref[...]
    k_new = kv_new[:, :ATT_KV_W]
    v_new = kv_new[:, ATT_KV_W:]
    ck = ck_ref[...]
    cv = cv_ref[...]
    s_c = jnp.einsum('bqd,bdk->bqk', q3, ck.astype(BF16),
                     preferred_element_type=F32).reshape(R, WINDOW)
    s_n = _dot_nt(q2, k_new.astype(BF16))
    row_c = lax.broadcasted_iota(jnp.int32, (R, WINDOW), 0)
    col_c = lax.broadcasted_iota(jnp.int32, (R, WINDOW), 1)
    s_c = jnp.where(col_c > row_c % T, s_c, -jnp.inf)
    row_n = lax.broadcasted_iota(jnp.int32, (R, L), 0)
    col_n = lax.broadcasted_iota(jnp.int32, (R, L), 1)
    valid_n = (row_n // (N_STACK * T) == col_n // T) & (col_n % T <= row_n % T)
    s_n = jnp.where(valid_n, s_n, -jnp.inf)
    stack_id = (lax.broadcasted_iota(jnp.int32, (R, 1), 0) // T) % N_STACK
    sink = jnp.zeros((R, 1), F32)
    for k_id in range(N_STACK):
        sink = jnp.where(stack_id == k_id, sinks_ref[ATT_HEAD_ORDER[k_id]], sink)
    m = jnp.maximum(jnp.maximum(jnp.max(s_c, axis=-1, keepdims=True),
                                jnp.max(s_n, axis=-1, keepdims=True)), sink)
    e_c = jnp.exp(s_c - m)
    e_n = jnp.exp(s_n - m)
    denom = (jnp.sum(e_c, axis=-1, keepdims=True) + jnp.sum(e_n, axis=-1, keepdims=True)
             + jnp.exp(sink - m))
    o = jnp.einsum('bqk,bdk->bqd', e_c.astype(BF16).reshape(bt, N_STACK * T, WINDOW),
                   cv.astype(BF16), preferred_element_type=F32).reshape(R, LANES)
    o = (o + _dot(e_n.astype(BF16), v_new.astype(BF16))) / denom
    o3 = o.reshape(bt, N_STACK * T, LANES)
    for col in range(ATT_GROUP):
        lo_h = o3[:, (2 * col) * T:(2 * col + 1) * T, :]
        hi_h = o3[:, (2 * col + 1) * T:(2 * col + 2) * T, :]
        mix_scr[:, col * LANES:(col + 1) * LANES] = jnp.where(
            low3, lo_h, hi_h).reshape(L, LANES).astype(BF16)

    keep = lax.broadcasted_iota(jnp.int32, (ATT_KV_W, WINDOW), 1) < WINDOW - T
    k_new_t = k_new.T
    v_new_t = v_new.T
    for q in range(bt):
        shift = (WINDOW - T - q * T) % WINDOW
        nk_ref[q] = jnp.where(keep, pltpu.roll(ck[q], WINDOW - T, axis=1),
                              pltpu.roll(k_new_t, shift, axis=1))
        nv_ref[q] = jnp.where(keep, pltpu.roll(cv[q], WINDOW - T, axis=1),
                              pltpu.roll(v_new_t, shift, axis=1))

    r = lax.broadcasted_iota(jnp.int32, (L, L), 0)
    c = lax.broadcasted_iota(jnp.int32, (L, L), 1)
    seg = (r // T == c // T) & (r <= c)
    seg_bias = jnp.where(seg, 0.0, -jnp.inf)
    seg_bf = seg.astype(F32).astype(BF16)
    gates = gt_ref[...] * LOG2_E
    cum_row = jnp.zeros(gates.shape, F32)
    for part in _split3(gates):
        cum_row = cum_row + _dot(part, seg_bf)
    ig_rows = gates[:ML_HEADS]
    b_rows = cum_row[ML_HEADS:]
    gate_cols = jnp.concatenate([ig_rows, b_rows, jnp.zeros((LANES - N_GATES, L), F32)],
                                axis=0).T

    def col_to_row(x_col):
        return jnp.broadcast_to(x_col, (L, LANES)).T[0:1, :]

    ones_rows = (r[:LANES] == 0).astype(F32).astype(BF16)
    qm = qm_ref[...]
    km = km_ref[...]
    qm_f = qm.astype(F32)
    km_f = km.astype(F32)
    n_rep = bt * ML_QK_DIM // LANES
    bd_row = lax.broadcasted_iota(jnp.int32, (L, bt * ML_QK_DIM), 0) // T
    bd_lane = lax.broadcasted_iota(jnp.int32, (L, bt * ML_QK_DIM), 1) // ML_QK_DIM
    block_diag = bd_row == bd_lane

    def spread(x_pair, e):
        other = pltpu.roll(x_pair, ML_QK_DIM, axis=1)
        twice = jnp.where(low, x_pair, other) if e == 0 else jnp.where(low, other, x_pair)
        return jnp.where(block_diag, jnp.concatenate([twice] * n_rep, axis=1), 0.0).astype(BF16)

    for h in range(ML_HEADS):
        p, e = divmod(h, 2)
        qc = qm[:, p * LANES:(p + 1) * LANES]
        k_pair = km[:, p * LANES:(p + 1) * LANES]
        zero = jnp.zeros_like(qc)
        q_pad = jnp.where(low, qc, zero) if e == 0 else jnp.where(low, zero, qc)
        v_h = vm_ref[:, h * ML_V_DIM:(h + 1) * ML_V_DIM]
        v_ext_t = jnp.concatenate([v_h.astype(F32).T.astype(BF16), ones_rows], axis=0)
        ig_c = gate_cols[:, h:h + 1]
        b_c = gate_cols[:, ML_HEADS + h:ML_HEADS + h + 1]
        b_r = b_rows[h:h + 1, :]
        m0 = m0_ref[:, :, h:h + 1] * LOG2_E
        inter = b_r + col_to_row(jnp.broadcast_to(m0, (bt, T, 1)).reshape(L, 1))
        dm = (b_r + (ig_c - b_c)) + seg_bias
        m_row = jnp.maximum(inter, jnp.max(dm, axis=0, keepdims=True))
        w_inter = jnp.exp2(inter - m_row)
        p_t = (_dot_nt(k_pair, q_pad) * jnp.exp2(dm - m_row)).astype(BF16)
        num_t = _dot(v_ext_t, p_t)
        q_h3 = qm_f[:, h * ML_QK_DIM:(h + 1) * ML_QK_DIM].reshape(bt, T, ML_QK_DIM)
        k_h3 = km_f[:, h * ML_QK_DIM:(h + 1) * ML_QK_DIM].reshape(bt, T, ML_QK_DIM)
        c0 = c0_ref[:, h]
        n0 = n0_ref[:, h:h + 1, :]
        q_c_t = _dot(spread(qm_f[:, p * LANES:(p + 1) * LANES], e),
                     c0.astype(BF16).reshape(bt * ML_QK_DIM, ML_V_DIM)).T
        q_n_r = col_to_row(jnp.sum(q_h3 * n0, axis=-1, keepdims=True).reshape(L, 1))
        num = num_t[:ML_V_DIM] + w_inter * q_c_t
        den = num_t[ML_V_DIM:ML_V_DIM + 1] + w_inter * q_n_r
        hh = num * (1.0 / jnp.maximum(jnp.abs(den), jnp.exp2(-m_row)))
        ms = jnp.mean(hh * hh, axis=0, keepdims=True)
        mix_scr[:, ATT_Q_W + h * ML_V_DIM:ATT_Q_W + (h + 1) * ML_V_DIM] = (
            (hh * lax.rsqrt(ms + EPS)).T * mlnw_ref[:, h * ML_V_DIM:(h + 1) * ML_V_DIM]
            * _sigmoid(om_ref[:, h * ML_V_DIM:(h + 1) * ML_V_DIM].astype(F32))).astype(BF16)
        b3 = b_c.reshape(bt, T, 1)
        b_last = b3[:, T - 1:T, :]
        a3 = b_last - b3 + ig_c.reshape(bt, T, 1)
        m_new = jnp.maximum(b_last + m0, jnp.max(a3, axis=1, keepdims=True))
        sc = jnp.exp2(b_last + m0 - m_new)
        ws = jnp.exp2(a3 - m_new)
        kw = spread(km_f[:, p * LANES:(p + 1) * LANES] * ws.reshape(L, 1), e)
        d_c = lax.dot_general(kw, v_h, (((0,), (0,)), ((), ())), preferred_element_type=F32)
        c_ref[:, h] = sc * c0 + d_c.reshape(bt, ML_QK_DIM, ML_V_DIM)
        n_ref[:, h:h + 1, :] = sc * n0 + jnp.sum(ws * k_h3, axis=1, keepdims=True)
        m_ref[:, :, h:h + 1] = m_new * (1.0 / LOG2_E)

    x1_ref[...] = x_ref[...] + _dot(mix_scr[...], wperm_scr[...])


def _sample_mixer(nb, t_len, sinks, qa, kv, ck, cv, qm, km, vm, om, gt, c0, n0, m0, x2d, wout, mlnw):
    bt = SAMPLE_BT
    tl = bt * t_len
    row = lambda w: pl.BlockSpec((tl, w), lambda i: (i, 0))
    full = lambda a: pl.BlockSpec(a.shape, lambda i: (0,) * a.ndim)
    once = lambda a: pl.BlockSpec(a.shape, lambda i: (0,) * a.ndim, pipeline_mode=pl.Buffered(1))
    cache = pl.BlockSpec((bt, ATT_KV_W, WINDOW), lambda i: (i, 0, 0))
    c_spec = pl.BlockSpec((bt, ML_HEADS, ML_QK_DIM, ML_V_DIM), lambda i: (i, 0, 0, 0))
    n_spec = pl.BlockSpec((bt, ML_HEADS, ML_QK_DIM), lambda i: (i, 0, 0))
    m_spec = pl.BlockSpec((bt, 1, ML_HEADS), lambda i: (i, 0, 0))
    return pl.pallas_call(
        functools.partial(_sample_mixer_kernel, t_len),
        grid=(nb // bt,),
        in_specs=[pl.BlockSpec(memory_space=pltpu.SMEM),
                  row(ATT_Q_W), row(2 * ATT_KV_W), cache, cache, row(ML_QK_W), row(ML_QK_W),
                  row(ML_V_W), row(ML_V_W), pl.BlockSpec((N_GATES, tl), lambda i: (0, i)),
                  c_spec, n_spec, m_spec, row(D_MODEL), once(wout), full(mlnw)],
        out_specs=[row(D_MODEL), cache, cache, c_spec, n_spec, m_spec],
        out_shape=[jax.ShapeDtypeStruct((nb * t_len, D_MODEL), F32),
                   jax.ShapeDtypeStruct((nb, ATT_KV_W, WINDOW), F32),
                   jax.ShapeDtypeStruct((nb, ATT_KV_W, WINDOW), F32),
                   jax.ShapeDtypeStruct((nb, ML_HEADS, ML_QK_DIM, ML_V_DIM), F32),
                   jax.ShapeDtypeStruct((nb, ML_HEADS, ML_QK_DIM), F32),
                   jax.ShapeDtypeStruct((nb, 1, ML_HEADS), F32)],
        scratch_shapes=[pltpu.VMEM((tl, D_MODEL), BF16),
                        pltpu.VMEM((D_MODEL, D_MODEL), BF16)],
        compiler_params=pltpu.CompilerParams(dimension_semantics=("arbitrary",),
                                             vmem_limit_bytes=VMEM_LIMIT),
        name="sample_mixer",
    )(sinks, qa, kv, ck, cv, qm, km, vm, om, gt, c0, n0, m0, x2d, wout, mlnw)


def _ffn_kernel(seq_rows, *refs):
    if seq_rows is None:
        (x_ref, nw_ref, w_ref, cw_ref, cb_ref, wd_ref, y_ref, conv_ref,
         gbuf, act_scr, carry) = refs
        hist_ref = None
    else:
        (x_ref, hist_ref, nw_ref, w_ref, cw_ref, cb_ref, wd_ref, y_ref, conv_ref,
         gbuf, act_scr) = refs
        carry = None
    tm = x_ref.shape[0]
    tf = FF_CHUNK
    n_hist = CONV_W - 1
    rows = tm if seq_rows is None else seq_rows
    nseq = tm // rows
    base = SUBLANES
    n_chunks = D_FF // tf

    if carry is not None:
        @pl.when(pl.program_id(1) == 0)
        def _():
            carry[...] = jnp.zeros(carry.shape, F32)

    x = x_ref[...]
    h2 = _rms(x, nw_ref[...]).astype(BF16)

    def proj(f):
        return (_dot(h2, w_ref[:, f * tf:(f + 1) * tf]),
                _dot(h2, w_ref[:, D_FF + f * tf:D_FF + (f + 1) * tf]))

    nxt = proj(0)
    for f in range(n_chunks):
        g, u = nxt
        if f + 1 < n_chunks:
            nxt = proj(f + 1)
        cols = slice(f * tf, (f + 1) * tf)
        s = f % 2
        g3 = g.reshape(nseq, rows, tf)
        if seq_rows is None:
            gbuf[s, :, base - n_hist:base, :] = carry[:, SUBLANES - n_hist:, cols]
            carry[:, SUBLANES - n_hist:, cols] = g3[:, rows - n_hist:, :]
        else:
            gbuf[s, :, base - n_hist:base, :] = hist_ref[:, :, cols]
            conv_ref[:, :, cols] = g3[:, rows - n_hist:, :]
        gbuf[s, :, base:base + rows, :] = g3
        gc = cb_ref[:, cols] + g * cw_ref[CONV_W - 1:CONV_W, cols]
        for d in range(1, CONV_W):
            gm = gbuf[s, :, base - d:base - d + rows, :].reshape(tm, tf)
            gc = gc + gm * cw_ref[CONV_W - 1 - d:CONV_W - d, cols]
        act_scr[:, cols] = (gc * _sigmoid(gc) * u).astype(BF16)
    y_ref[...] = x + _dot(act_scr[...], wd_ref[...])

    if carry is not None:
        @pl.when(pl.program_id(1) == pl.num_programs(1) - 1)
        def _():
            conv_ref[...] = carry[:, SUBLANES - n_hist:, :]


def _ffn_scratch(tm, rows):
    return [pltpu.VMEM((2, tm // rows, SUBLANES + rows, FF_CHUNK), F32),
            pltpu.VMEM((tm, D_FF), BF16)]


def _ffn_prompt(batch, seq, x2d, nw, w, cw, cb, wd):
    tm = FFN_TILE
    nt = seq // tm
    full = lambda a: pl.BlockSpec(a.shape, lambda b, i: (0,) * a.ndim)
    once = lambda a: pl.BlockSpec(a.shape, lambda b, i: (0,) * a.ndim,
                                  pipeline_mode=pl.Buffered(1))
    row = pl.BlockSpec((tm, D_MODEL), lambda b, i: (b * nt + i, 0))
    return pl.pallas_call(
        functools.partial(_ffn_kernel, None),
        grid=(batch, nt),
        in_specs=[row, full(nw), once(w), full(cw), full(cb), once(wd)],
        out_specs=[row, pl.BlockSpec((1, CONV_W - 1, D_FF), lambda b, i: (b, 0, 0))],
        out_shape=[jax.ShapeDtypeStruct((batch * seq, D_MODEL), F32),
                   jax.ShapeDtypeStruct((batch, CONV_W - 1, D_FF), F32)],
        scratch_shapes=_ffn_scratch(tm, tm) + [pltpu.VMEM((1, SUBLANES, D_FF), F32)],
        compiler_params=pltpu.CompilerParams(dimension_semantics=("arbitrary", "arbitrary"),
                                             vmem_limit_bytes=VMEM_LIMIT),
        name="ffn_prompt",
    )(x2d, nw, w, cw, cb, wd)


def _ffn_sample(nb, t_len, x2d, hist, nw, w, cw, cb, wd):
    tm = ROW_TILE
    bt = tm // t_len
    full = lambda a: pl.BlockSpec(a.shape, lambda i: (0,) * a.ndim)
    once = lambda a: pl.BlockSpec(a.shape, lambda i: (0,) * a.ndim, pipeline_mode=pl.Buffered(1))
    row = pl.BlockSpec((tm, D_MODEL), lambda i: (i, 0))
    hist_spec = pl.BlockSpec((bt, CONV_W - 1, D_FF), lambda i: (i, 0, 0))
    return pl.pallas_call(
        functools.partial(_ffn_kernel, t_len),
        grid=(nb // bt,),
        in_specs=[row, hist_spec, full(nw), once(w), full(cw), full(cb), once(wd)],
        out_specs=[row, hist_spec],
        out_shape=[jax.ShapeDtypeStruct((nb * t_len, D_MODEL), F32),
                   jax.ShapeDtypeStruct((nb, CONV_W - 1, D_FF), F32)],
        scratch_shapes=_ffn_scratch(tm, t_len),
        compiler_params=pltpu.CompilerParams(dimension_semantics=("arbitrary",),
                                             vmem_limit_bytes=VMEM_LIMIT),
        name="ffn_sample",
    )(x2d, hist, nw, w, cw, cb, wd)


def _head_mean_matrix(width, head_dim):
    idx = np.arange(width) // head_dim
    return jnp.asarray((idx[:, None] == idx[None, :]).astype(np.float32) / head_dim, dtype=BF16)


def _layer_weights(norm_mix_w, w_in, b_gates, q_norm_w, k_norm_w, sinks, ml_norm_w, w_out,
                   norm_ffn_w, w_ffn_in, conv_w, conv_b, w_down):
    w_in_t = jnp.pad(w_in.T.astype(BF16), ((0, IN_WIDTH_PAD - w_in.shape[1]), (0, 0)))
    return dict(
        nw=norm_mix_w.reshape(1, D_MODEL),
        w_in_t=w_in_t,
        bg=jnp.pad(b_gates, (0, LANES - N_GATES)).reshape(1, LANES),
        qnw=(jnp.tile(q_norm_w, ATT_HEADS) * ATT_SCALE).reshape(1, ATT_Q_W),
        knw=jnp.tile(k_norm_w, ATT_KV_HEADS).reshape(1, ATT_KV_W),
        gq=_head_mean_matrix(ATT_Q_W, ATT_HEAD_DIM),
        gk=_head_mean_matrix(ATT_KV_W, ATT_HEAD_DIM),
        bg_col=b_gates.reshape(N_GATES, 1),
        qnw_col=(jnp.tile(q_norm_w, ATT_HEADS) * (ATT_SCALE * LOG2_E)).reshape(ATT_Q_W, 1),
        knw_col=jnp.tile(k_norm_w, ATT_KV_HEADS).reshape(ATT_KV_W, 1),
        mlnw_col=ml_norm_w.reshape(ML_V_W, 1),
        sinks=sinks,
        mlnw=ml_norm_w.reshape(1, ML_V_W),
        wout=w_out.astype(BF16),
        nfw=norm_ffn_w.reshape(1, D_MODEL),
        wff=w_ffn_in.astype(BF16),
        cw=conv_w,
        cb=conv_b.reshape(1, D_FF),
        wd=w_down.astype(BF16),
    )


def _cache_from_t(a_t):
    n = a_t.shape[0]
    return jnp.transpose(a_t.reshape(n, ATT_KV_HEADS, ATT_HEAD_DIM, WINDOW), (0, 3, 1, 2))


def _cache_to_t(a):
    n = a.shape[0]
    return jnp.transpose(a, (0, 2, 3, 1)).reshape(n, ATT_KV_W, WINDOW)


def _prompt_layer(x, w):
    batch, seq, _ = x.shape
    assert seq % ROW_TILE == 0 and MIX_TILE == ROW_TILE and MIX_TILE % WINDOW == 0
    assert seq % FFN_TILE == 0
    x2d = x.reshape(batch * seq, D_MODEL)
    qa, ks, kv, qm, km, vm, om, gt = _inproj_t(x2d, w["nw"], w["w_in_t"], w["bg_col"],
                                               w["qnw_col"], w["knw_col"])
    x1, c_t, n_row, m, k_t, v_t = _prompt_mixer_t(batch, seq, w["sinks"], qa, ks, kv, qm, km, vm,
                                                  om, gt, x2d, w["wout"], w["mlnw_col"])
    y, conv = _ffn_prompt(batch, seq, x1, w["nfw"], w["wff"], w["cw"], w["cb"], w["wd"])
    return (y.reshape(batch, seq, D_MODEL), _cache_from_t(k_t), _cache_from_t(v_t),
            jnp.swapaxes(c_t, -1, -2), n_row.reshape(batch, ML_HEADS, ML_QK_DIM),
            m.reshape(batch, ML_HEADS), conv)


def _sample_layer(x, ck, cv, c0, n0, m0, conv_buf, w):
    nb, t_len, _ = x.shape
    assert t_len == SUBLANES and SAMPLE_BT * t_len == LANES and nb % SAMPLE_BT == 0
    assert (nb * t_len) % ROW_TILE == 0
    x2d = x.reshape(nb * t_len, D_MODEL)
    qa, kv, qm, km, vm, om, gt = _inproj(x2d, w["nw"], w["w_in_t"], w["bg"], w["qnw"], w["knw"],
                                         w["gq"], w["gk"])
    x1, nk_t, nv_t, c_t, n, m = _sample_mixer(
        nb, t_len, w["sinks"], qa, kv, _cache_to_t(ck), _cache_to_t(cv), qm, km, vm, om, gt,
        jnp.swapaxes(c0, -1, -2), n0, m0.reshape(nb, 1, ML_HEADS), x2d, w["wout"], w["mlnw"])
    y, conv = _ffn_sample(nb, t_len, x1, conv_buf, w["nfw"], w["wff"], w["cw"], w["cb"],
                          w["wd"])
    return (y.reshape(nb, t_len, D_MODEL), _cache_from_t(nk_t), _cache_from_t(nv_t),
            jnp.swapaxes(c_t, -1, -2), n, m.reshape(nb, ML_HEADS), conv)


def kernel(x_prompt, x_sample, cache_attn_k, cache_attn_v, state_mlstm_C, state_mlstm_n,
           state_mlstm_m, cache_ffn_conv, norm_mix_w, w_in, b_gates, q_norm_w, k_norm_w,
           sinks, ml_norm_w, w_out, norm_ffn_w, w_ffn_in, conv_w, conv_b, w_down):
    depth = w_in.shape[0]
    yp, ys = x_prompt, x_sample
    sp = [[] for _ in range(6)]
    ss = [[] for _ in range(6)]
    for l in range(depth):
        w = _layer_weights(norm_mix_w[l], w_in[l], b_gates[l], q_norm_w[l], k_norm_w[l], sinks[l],
                           ml_norm_w[l], w_out[l], norm_ffn_w[l], w_ffn_in[l], conv_w[l],
                           conv_b[l], w_down[l])
        yp, *st_p = _prompt_layer(yp, w)
        ys, *st_s = _sample_layer(ys, cache_attn_k[l], cache_attn_v[l], state_mlstm_C[l],
                                  state_mlstm_n[l], state_mlstm_m[l], cache_ffn_conv[l], w)
        for i in range(6):
            sp[i].append(st_p[i])
            ss[i].append(st_s[i])
    k_p, v_p, c_p, n_p, m_p, conv_p = [jnp.stack(a) for a in sp]
    k_s, v_s, c_s, n_s, m_s, conv_s = [jnp.stack(a) for a in ss]
    return (yp, ys, k_p, v_p, c_p, n_p, m_p, conv_p, k_s, v_s, c_s, n_s, m_s, conv_s)
```

```python
import functools

import numpy as np
import jax
import jax.numpy as jnp
from jax import lax
from jax.experimental import pallas as pl
from jax.experimental.pallas import tpu as pltpu

F32 = jnp.float32
BF16 = jnp.bfloat16

D_MODEL = 1024
ATT_HEADS = 8
ATT_KV_HEADS = 2
ATT_HEAD_DIM = 64
ATT_GROUP = ATT_HEADS // ATT_KV_HEADS
WINDOW = 128
ML_HEADS = 4
ML_V_DIM = 128
ML_QK_DIM = 64
D_FF = 2816
CONV_W = 3
EPS = 1e-6
ATT_SCALE = ATT_HEAD_DIM ** -0.5
ML_SCALE = ML_QK_DIM ** -0.5
LOG2_E = 1.4426950408889634

ATT_Q_W = ATT_HEADS * ATT_HEAD_DIM
ATT_KV_W = ATT_KV_HEADS * ATT_HEAD_DIM
ML_QK_W = ML_HEADS * ML_QK_DIM
ML_V_W = ML_HEADS * ML_V_DIM
N_GATES = 2 * ML_HEADS
N_STACK = 2 * ATT_GROUP

LANES = 128
SUBLANES = 8

OFF_QA = 0
OFF_KV = OFF_QA + ATT_Q_W
OFF_QM = OFF_KV + 2 * ATT_KV_W
OFF_KM = OFF_QM + ML_QK_W
OFF_VM = OFF_KM + ML_QK_W
OFF_OM = OFF_VM + ML_V_W
OFF_GL = OFF_OM + ML_V_W
IN_WIDTH_PAD = OFF_GL + LANES

ATT_HEAD_ORDER = tuple(h for c in range(ATT_GROUP) for h in (c, c + ATT_GROUP))

ROW_TILE = 512
FFN_TILE = 1024
INPROJ_SUB = 256
MIX_TILE = 512
ML_CHUNK = 256
OUT_COLS = 256
FF_CHUNK = 256
SAMPLE_BT = 16
VMEM_LIMIT = 56 * 1024 * 1024


def _dot(a, b):
    return jnp.dot(a, b, preferred_element_type=F32)


def _dot_nt(a, b):
    return lax.dot_general(a, b, (((1,), (1,)), ((), ())), preferred_element_type=F32)


def _split3(x):
    hi = x.astype(BF16)
    r1 = x - hi.astype(F32)
    mid = r1.astype(BF16)
    lo = (r1 - mid.astype(F32)).astype(BF16)
    return hi, mid, lo


def _rms(x, w):
    ms = jnp.mean(x * x, axis=-1, keepdims=True)
    return x * lax.rsqrt(ms + EPS) * w


def _log_sigmoid(x):
    return jnp.minimum(x, 0.0) - jnp.log1p(jnp.exp(-jnp.abs(x)))


def _sigmoid(x):
    return 1.0 / (1.0 + jnp.exp(-x))


def _permute_head_rows(dst_ref, src_ref):
    for k, h in enumerate(ATT_HEAD_ORDER):
        dst_ref[k * ATT_HEAD_DIM:(k + 1) * ATT_HEAD_DIM, :] = (
            src_ref[h * ATT_HEAD_DIM:(h + 1) * ATT_HEAD_DIM, :])


def _inproj_kernel(x_ref, nw_ref, w_ref, bg_ref, qnw_ref, knw_ref, gq_ref, gk_ref,
                   qa_ref, kv_ref, qm_ref, km_ref, vm_ref, om_ref, gt_ref, wq_scr):
    @pl.when(pl.program_id(0) == 0)
    def _():
        _permute_head_rows(wq_scr, w_ref)

    h = _rms(x_ref[...], nw_ref[...]).astype(BF16)

    def proj(lo, width):
        return _dot_nt(h, w_ref[lo:lo + width, :])

    q = _dot_nt(h, wq_scr[...])
    q_ms = _dot((q * q).astype(BF16), gq_ref[...])
    qa_ref[...] = (q * lax.rsqrt(q_ms + EPS) * qnw_ref[...]).astype(BF16)

    kv = proj(OFF_KV, 2 * ATT_KV_W)
    k = kv[:, :ATT_KV_W]
    k_ms = _dot((k * k).astype(BF16), gk_ref[...])
    kv_ref[:, :ATT_KV_W] = k * lax.rsqrt(k_ms + EPS) * knw_ref[...]
    kv_ref[:, ATT_KV_W:] = kv[:, ATT_KV_W:]

    qm_ref[...] = (proj(OFF_QM, ML_QK_W) * ML_SCALE).astype(BF16)
    km_ref[...] = proj(OFF_KM, ML_QK_W).astype(BF16)
    vm_ref[...] = proj(OFF_VM, ML_V_W).astype(BF16)
    om_ref[...] = proj(OFF_OM, ML_V_W).astype(BF16)

    gl = proj(OFF_GL, LANES) + bg_ref[...]
    lane = lax.broadcasted_iota(jnp.int32, gl.shape, 1)
    g = jnp.where(lane < ML_HEADS, gl, _log_sigmoid(gl))
    gt_ref[...] = g.T[:N_GATES, :]


def _inproj(x2d, nw, w_in_t, bg, qnw, knw, gq, gk):
    n = x2d.shape[0]
    tm = ROW_TILE
    row = lambda w: pl.BlockSpec((tm, w), lambda i: (i, 0))
    full = lambda a: pl.BlockSpec(a.shape, lambda i: (0,) * a.ndim)
    once = lambda a: pl.BlockSpec(a.shape, lambda i: (0,) * a.ndim, pipeline_mode=pl.Buffered(1))
    return pl.pallas_call(
        _inproj_kernel,
        grid=(n // tm,),
        in_specs=[row(D_MODEL), full(nw), once(w_in_t), full(bg), full(qnw), full(knw),
                  full(gq), full(gk)],
        out_specs=[row(ATT_Q_W), row(2 * ATT_KV_W), row(ML_QK_W), row(ML_QK_W),
                   row(ML_V_W), row(ML_V_W), pl.BlockSpec((N_GATES, tm), lambda i: (0, i))],
        out_shape=[jax.ShapeDtypeStruct((n, ATT_Q_W), BF16),
                   jax.ShapeDtypeStruct((n, 2 * ATT_KV_W), F32),
                   jax.ShapeDtypeStruct((n, ML_QK_W), BF16),
                   jax.ShapeDtypeStruct((n, ML_QK_W), BF16),
                   jax.ShapeDtypeStruct((n, ML_V_W), BF16),
                   jax.ShapeDtypeStruct((n, ML_V_W), BF16),
                   jax.ShapeDtypeStruct((N_GATES, n), F32)],
        scratch_shapes=[pltpu.VMEM((ATT_Q_W, D_MODEL), BF16)],
        compiler_params=pltpu.CompilerParams(dimension_semantics=("arbitrary",),
                                             vmem_limit_bytes=VMEM_LIMIT),
        name="inproj",
    )(x2d, nw, w_in_t, bg, qnw, knw, gq, gk)


def _head_norm_t(z, head_dim, w_col):
    rows, tokens = z.shape
    z3 = z.reshape(rows // head_dim, head_dim, tokens)
    ms = jnp.mean(z3 * z3, axis=1, keepdims=True)
    return (z3 * lax.rsqrt(ms + EPS)).reshape(rows, tokens) * w_col


def _inproj_t_kernel(x_ref, nw_ref, w_ref, bg_ref, qnw_ref, knw_ref,
                     qa_ref, ks_ref, kv_ref, qm_ref, km_ref, vm_ref, om_ref, gt_ref):
    tm = x_ref.shape[0]
    sub = INPROJ_SUB
    hs = [_rms(x_ref[c * sub:(c + 1) * sub, :], nw_ref[...]).astype(BF16)
          for c in range(tm // sub)]
    for c, h in enumerate(hs):
        tok = slice(c * sub, (c + 1) * sub)

        def proj(lo, width):
            return _dot_nt(w_ref[lo:lo + width, :], h)

        qa_ref[:, tok] = _head_norm_t(proj(OFF_QA, ATT_Q_W), ATT_HEAD_DIM,
                                      qnw_ref[...]).astype(BF16)
        kv = proj(OFF_KV, 2 * ATT_KV_W)
        k = _head_norm_t(kv[:ATT_KV_W], ATT_HEAD_DIM, knw_ref[...])
        kv_ref[:ATT_KV_W, tok] = k
        kv_ref[ATT_KV_W:, tok] = kv[ATT_KV_W:]
        ks_ref[tok, :] = k.T.astype(BF16)
        qm_ref[:, tok] = (proj(OFF_QM, ML_QK_W) * ML_SCALE).astype(BF16)
        km_ref[:, tok] = proj(OFF_KM, ML_QK_W).astype(BF16)
        vm_ref[:, tok] = proj(OFF_VM, ML_V_W).astype(BF16)
        om_ref[:, tok] = proj(OFF_OM, ML_V_W).astype(BF16)
        gl = proj(OFF_GL, 2 * SUBLANES)[:N_GATES] + bg_ref[...]
        row = lax.broadcasted_iota(jnp.int32, gl.shape, 0)
        gt_ref[:, tok] = jnp.where(row < ML_HEADS, gl, _log_sigmoid(gl))


def _inproj_t(x2d, nw, w_in_t, bg_col, qnw_col, knw_col):
    n = x2d.shape[0]
    tm = ROW_TILE
    full = lambda a: pl.BlockSpec(a.shape, lambda i: (0,) * a.ndim)
    once = lambda a: pl.BlockSpec(a.shape, lambda i: (0,) * a.ndim, pipeline_mode=pl.Buffered(1))
    col = lambda w: pl.BlockSpec((None, w, tm), lambda i: (i, 0, 0))
    slab = lambda w, dt: jax.ShapeDtypeStruct((n // tm, w, tm), dt)
    return pl.pallas_call(
        _inproj_t_kernel,
        grid=(n // tm,),
        in_specs=[pl.BlockSpec((tm, D_MODEL), lambda i: (i, 0)), full(nw), once(w_in_t),
                  full(bg_col), full(qnw_col), full(knw_col)],
        out_specs=[col(ATT_Q_W), pl.BlockSpec((tm, ATT_KV_W), lambda i: (i, 0)),
                   col(2 * ATT_KV_W), col(ML_QK_W), col(ML_QK_W), col(ML_V_W), col(ML_V_W),
                   col(N_GATES)],
        out_shape=[slab(ATT_Q_W, BF16),
                   jax.ShapeDtypeStruct((n, ATT_KV_W), BF16),
                   slab(2 * ATT_KV_W, F32), slab(ML_QK_W, BF16), slab(ML_QK_W, BF16),
                   slab(ML_V_W, BF16), slab(ML_V_W, BF16), slab(N_GATES, F32)],
        compiler_params=pltpu.CompilerParams(dimension_semantics=("arbitrary",),
                                             vmem_limit_bytes=VMEM_LIMIT),
        name="inproj_t",
    )(x2d, nw, w_in_t, bg_col, qnw_col, knw_col)


def _gate_forms(gates, seg_mask, want_raw_col):
    L = gates.shape[1]
    m_bf = seg_mask.astype(F32).astype(BF16)
    cum_row = jnp.zeros(gates.shape, F32)
    cum_col = jnp.zeros((L, gates.shape[0]), F32)
    raw_col = None
    if want_raw_col:
        r = lax.broadcasted_iota(jnp.int32, (L, L), 0)
        c = lax.broadcasted_iota(jnp.int32, (L, L), 1)
        eye = (r == c).astype(F32).astype(BF16)
        raw_col = jnp.zeros((L, gates.shape[0]), F32)
    for part in _split3(gates):
        cum_row = cum_row + _dot_nt(part, m_bf)
        cum_col = cum_col + _dot_nt(m_bf, part)
        if want_raw_col:
            raw_col = raw_col + _dot_nt(eye, part)
    return cum_row, cum_col, raw_col


def _mlstm_intra(q_pad, k_pair, v_ext, seg_mask, b_c, b_r, ig_r, m_prev_c):
    dm = jnp.where(seg_mask, b_c + (ig_r - b_r), -jnp.inf)
    inter = b_c + m_prev_c
    m_row = jnp.maximum(inter, jnp.max(dm, axis=-1, keepdims=True))
    w_inter = jnp.exp(inter - m_row)
    p = _dot_nt(q_pad, k_pair) * jnp.exp(dm - m_row)
    return _dot(p.astype(BF16), v_ext), m_row, w_inter


def _mlstm_out(pv, m_row, w_inter, q_c, q_n, mlnw_h, om_h):
    num = pv[:, :ML_V_DIM] + w_inter * q_c
    den = pv[:, ML_V_DIM:ML_V_DIM + 1] + w_inter * q_n
    hh = num / jnp.maximum(jnp.abs(den), jnp.exp(-m_row))
    return (_rms(hh, mlnw_h) * _sigmoid(om_h.astype(F32))).astype(BF16)


def _ones_col(rows):
    lane = lax.broadcasted_iota(jnp.int32, (rows, LANES), 1)
    return (lane == 0).astype(F32).astype(BF16)


def _prompt_mixer_t_kernel(sinks_ref, qa_ref, ksc_ref, ksp_ref, kvc_ref, kvp_ref, qm_ref, km_ref,
                           vm_ref, om_ref, gt_ref, x_ref, wout_ref, mlnw_ref,
                           x1_ref, ct_ref, nrow_ref, m_ref, kt_ref, vt_ref,
                           mix_scr, state_scr, m_scr, band_scr, causal_scr, tri_scr,
                           s_scr_a, s_scr_b, e_scr):
    i = pl.program_id(1)
    A = WINDOW
    L = MIX_TILE
    C = ML_CHUNK
    n_pairs = ML_HEADS // 2

    @pl.when(i == 0)
    def _():
        state_scr[...] = jnp.zeros(state_scr.shape, F32)
        m_scr[...] = jnp.zeros(m_scr.shape, F32)
        kj = lax.broadcasted_iota(jnp.int32, (2 * A, A), 0)
        qi = lax.broadcasted_iota(jnp.int32, (2 * A, A), 1)
        band = (kj > qi) & (kj <= qi + WINDOW)
        band_scr[0] = jnp.where(band, 0.0, -jnp.inf)
        band_scr[1] = jnp.where(band & (kj >= A), 0.0, -jnp.inf)
        r = lax.broadcasted_iota(jnp.int32, (C, C), 0)
        c = lax.broadcasted_iota(jnp.int32, (C, C), 1)
        causal_scr[...] = jnp.where(r <= c, 0.0, -jnp.inf)
        tri_scr[...] = (r <= c).astype(F32).astype(BF16)

    k_all = jnp.concatenate([ksp_ref[...], ksc_ref[...]], axis=0)
    v_all = jnp.concatenate([kvp_ref[ATT_KV_W:, :], kvc_ref[ATT_KV_W:, :]], axis=1).astype(BF16)
    zero_q = jnp.zeros((ATT_HEAD_DIM, A), BF16)
    slot = 0
    s_bufs = (s_scr_a, s_scr_b)

    def stage_scores(j):
        pieces = []
        for h in range(ATT_HEADS):
            q_h = qa_ref[h * ATT_HEAD_DIM:(h + 1) * ATT_HEAD_DIM, j * A:(j + 1) * A]
            pieces.append(jnp.concatenate([q_h, zero_q] if h < ATT_GROUP else [zero_q, q_h],
                                          axis=0))
        s_bufs[j % 2][slot] = _dot(k_all[j * A:(j + 2) * A, :], jnp.concatenate(pieces, axis=1))

    def attend(j):
        cols = slice(j * A, (j + 1) * A)
        vt = v_all[:, j * A:(j + 2) * A]
        if j + 1 < L // A:
            stage_scores(j + 1)
        s_buf = s_bufs[j % 2]
        bias = jnp.where(i > 0, band_scr[0], band_scr[1]) if j == 0 else band_scr[0]
        m_rows = []
        for h in range(ATT_HEADS):
            sb = s_buf[slot, :, h * A:(h + 1) * A] + bias
            m_rows.append(jnp.maximum(jnp.max(sb, axis=0, keepdims=True),
                                      sinks_ref[h] * LOG2_E))
        inv_rows = []
        for h in range(ATT_HEADS):
            e = jnp.exp2(s_buf[slot, :, h * A:(h + 1) * A] + (bias - m_rows[h]))
            e_scr[:, h * A:(h + 1) * A] = e.astype(BF16)
            inv_rows.append(1.0 / (jnp.sum(e, axis=0, keepdims=True)
                                   + jnp.exp2(sinks_ref[h] * LOG2_E - m_rows[h])))
        o = _dot(vt, e_scr[...])
        for h in range(ATT_HEADS):
            g = h // ATT_GROUP
            mix_scr[h * ATT_HEAD_DIM:(h + 1) * ATT_HEAD_DIM, cols] = (
                o[g * ATT_HEAD_DIM:(g + 1) * ATT_HEAD_DIM, h * A:(h + 1) * A]
                * inv_rows[h]).astype(BF16)

    row128 = lax.broadcasted_iota(jnp.int32, (LANES, C), 0)
    ones_rows = (row128 == 0).astype(F32).astype(BF16)

    def mlstm_chunk(ci):
        tok = slice(ci * C, (ci + 1) * C)
        gates = gt_ref[:, tok] * LOG2_E
        cum_row = jnp.zeros(gates.shape, F32)
        for part in _split3(gates):
            cum_row = cum_row + _dot(part, tri_scr[...])
        ig_rows = gates[:ML_HEADS]
        b_rows = cum_row[ML_HEADS:]
        key_cols = jnp.concatenate([ig_rows - b_rows, jnp.zeros((LANES - ML_HEADS, C), F32)],
                                   axis=0).T
        for p in range(n_pairs):
            q_c = qm_ref[p * LANES:(p + 1) * LANES, tok]
            k_pair = km_ref[p * LANES:(p + 1) * LANES, tok]
            zero = jnp.zeros_like(q_c)
            state = state_scr[p]
            state_bf = state.astype(BF16)
            new_state = []
            for e_id in range(2):
                h = 2 * p + e_id
                v_rows = slice(h * ML_V_DIM, (h + 1) * ML_V_DIM)
                head_rows = (row128 < ML_QK_DIM) if e_id == 0 else (row128 >= ML_QK_DIM)
                q_pad = jnp.where(head_rows, q_c, zero)
                b_r = b_rows[h:h + 1, :]
                ig_r = ig_rows[h:h + 1, :]
                m_prev = m_scr[h:h + 1, 0:1]
                dm = (b_r + key_cols[:, h:h + 1]) + causal_scr[...]
                inter = b_r + m_prev
                m_row = jnp.maximum(inter, jnp.max(dm, axis=0, keepdims=True))
                w_inter = jnp.exp2(inter - m_row)
                qk = lax.dot_general(k_pair, q_pad, (((0,), (0,)), ((), ())),
                                     preferred_element_type=F32)
                p_t = (qk * jnp.exp2(dm - m_row)).astype(BF16)
                v_ext = jnp.concatenate([vm_ref[v_rows, tok], ones_rows], axis=0)
                num = _dot(v_ext, p_t) + w_inter * _dot(state_bf, q_pad)
                den = num[ML_V_DIM:ML_V_DIM + 1, :]
                hh = num[:ML_V_DIM] * (1.0 / jnp.maximum(jnp.abs(den), jnp.exp2(-m_row)))
                ms = jnp.mean(hh * hh, axis=0, keepdims=True)
                gate = _sigmoid(om_ref[v_rows, tok].astype(F32))
                mix_scr[ATT_Q_W + h * ML_V_DIM:ATT_Q_W + (h + 1) * ML_V_DIM, tok] = (
                    hh * lax.rsqrt(ms + EPS) * mlnw_ref[v_rows, :] * gate).astype(BF16)
                b_last = b_r[:, C - 1:C]
                a_r = b_last - b_r + ig_r
                m_new = jnp.maximum(b_last + m_prev, jnp.max(a_r, axis=-1, keepdims=True))
                sc = jnp.exp2(b_last + m_prev - m_new)
                wsv = (v_ext.astype(F32) * jnp.exp2(a_r - m_new)).astype(BF16)
                new_state.append(sc * state + _dot_nt(wsv, k_pair))
                m_scr[h:h + 1, :] = jnp.broadcast_to(m_new, (1, LANES))
            first = lax.broadcasted_iota(jnp.int32, state.shape, 1) < ML_QK_DIM
            state_scr[p] = jnp.where(first, new_state[0], new_state[1])
            yield

    def out_proj(ci):
        tok = slice(ci * C, (ci + 1) * C)
        mix_t = mix_scr[:, tok].T
        for n in range(D_MODEL // OUT_COLS):
            nc = slice(n * OUT_COLS, (n + 1) * OUT_COLS)
            x1_ref[tok, nc] = x_ref[tok, nc] + _dot(mix_t, wout_ref[:, nc])
            yield

    stage_scores(0)
    n_sub = C // A
    pairs = (step for ci in range(L // C) for step in mlstm_chunk(ci))
    projs = iter(())
    for j in range(L // A):
        if j and j % n_sub == 0:
            projs = out_proj(j // n_sub - 1)
        attend(j)
        next(projs, None)
        next(pairs, None)
        next(projs, None)
    for _ in pairs:
        pass
    for _ in projs:
        pass
    for _ in out_proj(L // C - 1):
        pass

    @pl.when(i == pl.num_programs(1) - 1)
    def _():
        for p in range(n_pairs):
            c_t = state_scr[p, :ML_V_DIM, :].T
            for e_id in range(2):
                ct_ref[0, 2 * p + e_id] = c_t[e_id * ML_QK_DIM:(e_id + 1) * ML_QK_DIM, :]
            nrow_ref[0, p:p + 1, :] = state_scr[p, ML_V_DIM:ML_V_DIM + 1, :]
        for h in range(ML_HEADS):
            m_ref[0, :, h:h + 1] = m_scr[h:h + 1, 0:1] * (1.0 / LOG2_E)
        kt_ref[0] = kvc_ref[:ATT_KV_W, L - WINDOW:]
        vt_ref[0] = kvc_ref[ATT_KV_W:, L - WINDOW:]


def _prompt_mixer_t(batch, seq, sinks, qa, ks, kv, qm, km, vm, om, gt, x2d, wout, mlnw_col):
    tq = MIX_TILE
    nt = seq // tq
    sub = tq // WINDOW
    col = lambda w: pl.BlockSpec((None, w, tq), lambda b, i: (b * nt + i, 0, 0))
    full = lambda a: pl.BlockSpec(a.shape, lambda b, i: (0,) * a.ndim)
    once = lambda a: pl.BlockSpec(a.shape, lambda b, i: (0,) * a.ndim,
                                  pipeline_mode=pl.Buffered(1))
    prev_block = lambda b, i: jnp.maximum((b * nt + i) * sub - 1, 0)
    per_batch = lambda *dims: pl.BlockSpec((1,) + dims, lambda b, i: (b,) + (0,) * len(dims))
    return pl.pallas_call(
        _prompt_mixer_t_kernel,
        grid=(batch, nt),
        in_specs=[pl.BlockSpec(memory_space=pltpu.SMEM),
                  col(ATT_Q_W),
                  pl.BlockSpec((tq, ATT_KV_W), lambda b, i: (b * nt + i, 0)),
                  pl.BlockSpec((WINDOW, ATT_KV_W), lambda b, i: (prev_block(b, i), 0)),
                  col(2 * ATT_KV_W),
                  pl.BlockSpec((None, 2 * ATT_KV_W, WINDOW),
                               lambda b, i: (prev_block(b, i) // sub, 0, prev_block(b, i) % sub)),
                  col(ML_QK_W), col(ML_QK_W), col(ML_V_W), col(ML_V_W), col(N_GATES),
                  pl.BlockSpec((tq, D_MODEL), lambda b, i: (b * nt + i, 0)),
                  once(wout), full(mlnw_col)],
        out_specs=[pl.BlockSpec((tq, D_MODEL), lambda b, i: (b * nt + i, 0)),
                   per_batch(ML_HEADS, ML_QK_DIM, ML_V_DIM),
                   per_batch(ML_HEADS // 2, LANES),
                   per_batch(1, ML_HEADS),
                   per_batch(ATT_KV_W, WINDOW),
                   per_batch(ATT_KV_W, WINDOW)],
        out_shape=[jax.ShapeDtypeStruct((batch * seq, D_MODEL), F32),
                   jax.ShapeDtypeStruct((batch, ML_HEADS, ML_QK_DIM, ML_V_DIM), F32),
                   jax.ShapeDtypeStruct((batch, ML_HEADS // 2, LANES), F32),
                   jax.ShapeDtypeStruct((batch, 1, ML_HEADS), F32),
                   jax.ShapeDtypeStruct((batch, ATT_KV_W, WINDOW), F32),
                   jax.ShapeDtypeStruct((batch, ATT_KV_W, WINDOW), F32)],
        scratch_shapes=[pltpu.VMEM((D_MODEL, tq), BF16),
                        pltpu.VMEM((ML_HEADS // 2, 2 * LANES, LANES), F32),
                        pltpu.VMEM((SUBLANES, LANES), F32),
                        pltpu.VMEM((2, 2 * WINDOW, WINDOW), F32),
                        pltpu.VMEM((ML_CHUNK, ML_CHUNK), F32),
                        pltpu.VMEM((ML_CHUNK, ML_CHUNK), BF16),
                        pltpu.VMEM((2, 2 * WINDOW, ATT_HEADS * WINDOW), F32),
                        pltpu.VMEM((2, 2 * WINDOW, ATT_HEADS * WINDOW), F32),
                        pltpu.VMEM((2 * WINDOW, ATT_HEADS * WINDOW), BF16)],
        compiler_params=pltpu.CompilerParams(dimension_semantics=("arbitrary", "arbitrary"),
                                             vmem_limit_bytes=VMEM_LIMIT),
        name="prompt_mixer_t",
    )(sinks, qa, ks, ks, kv, kv, qm, km, vm, om, gt, x2d, wout, mlnw_col)


def _sample_mixer_kernel(t_len, sinks_ref, qa_ref, kv_ref, ck_ref, cv_ref, qm_ref, km_ref,
                         vm_ref, om_ref, gt_ref, c0_ref, n0_ref, m0_ref, x_ref, wout_ref,
                         mlnw_ref, x1_ref, nk_ref, nv_ref, c_ref, n_ref, m_ref,
                         mix_scr, wperm_scr):
    bt = SAMPLE_BT
    T = t_len
    L = bt * T

    @pl.when(pl.program_id(0) == 0)
    def _():
        _permute_head_rows(wperm_scr, wout_ref)
        wperm_scr[ATT_Q_W:, :] = wout_ref[ATT_Q_W:, :]

    lane3 = lax.broadcasted_iota(jnp.int32, (bt, T, LANES), 2)
    low3 = lane3 < ATT_HEAD_DIM
    lane = lax.broadcasted_iota(jnp.int32, (L, LANES), 1)
    low = lane < ATT_HEAD_DIM

    qa3 = qa_ref[...].astype(F32).reshape(bt, T, ATT_Q_W)
    pieces = []
    for col in range(ATT_GROUP):
        qc = qa3[:, :, col * LANES:(col + 1) * LANES]
        pieces += [jnp.where(low3, qc, 0.0), jnp.where(low3, 0.0, qc)]
    q3 = jnp.concatenate(pieces, axis=1).astype(BF16)
    R = bt * N_STACK * T
    q2 = q3.reshape(R, LANES)
    kv_new = kv_ref[...]
    k_new = kv_new[:, :ATT_KV_W]
    v_new = kv_new[:, ATT_KV_W:]
    ck = ck_ref[...]
    cv = cv_ref[...]
    s_c = jnp.einsum('bqd,bdk->bqk', q3, ck.astype(BF16),
                     preferred_element_type=F32).reshape(R, WINDOW)
    s_n = _dot_nt(q2, k_new.astype(BF16))
    row_c = lax.broadcasted_iota(jnp.int32, (R, WINDOW), 0)
    col_c = lax.broadcasted_iota(jnp.int32, (R, WINDOW), 1)
    s_c = jnp.where(col_c > row_c % T, s_c, -jnp.inf)
    row_n = lax.broadcasted_iota(jnp.int32, (R, L), 0)
    col_n = lax.broadcasted_iota(jnp.int32, (R, L), 1)
    valid_n = (row_n // (N_STACK * T) == col_n // T) & (col_n % T <= row_n % T)
    s_n = jnp.where(valid_n, s_n, -jnp.inf)
    stack_id = (lax.broadcasted_iota(jnp.int32, (R, 1), 0) // T) % N_STACK
    sink = jnp.zeros((R, 1), F32)
    for k_id in range(N_STACK):
        sink = jnp.where(stack_id == k_id, sinks_ref[ATT_HEAD_ORDER[k_id]], sink)
    m = jnp.maximum(jnp.maximum(jnp.max(s_c, axis=-1, keepdims=True),
                                jnp.max(s_n, axis=-1, keepdims=True)), sink)
    e_c = jnp.exp(s_c - m)
    e_n = jnp.exp(s_n - m)
    denom = (jnp.sum(e_c, axis=-1, keepdims=True) + jnp.sum(e_n, axis=-1, keepdims=True)
             + jnp.exp(sink - m))
    o = jnp.einsum('bqk,bdk->bqd', e_c.astype(BF16).reshape(bt, N_STACK * T, WINDOW),
                   cv.astype(BF16), preferred_element_type=F32).reshape(R, LANES)
    o = (o + _dot(e_n.astype(BF16), v_new.astype(BF16))) / denom
    o3 = o.reshape(bt, N_STACK * T, LANES)
    for col in range(ATT_GROUP):
        lo_h = o3[:, (2 * col) * T:(2 * col + 1) * T, :]
        hi_h = o3[:, (2 * col + 1) * T:(2 * col + 2) * T, :]
        mix_scr[:, col * LANES:(col + 1) * LANES] = jnp.where(
            low3, lo_h, hi_h).reshape(L, LANES).astype(BF16)

    keep = lax.broadcasted_iota(jnp.int32, (ATT_KV_W, WINDOW), 1) < WINDOW - T
    k_new_t = k_new.T
    v_new_t = v_new.T
    for q in range(bt):
        shift = (WINDOW - T - q * T) % WINDOW
        nk_ref[q] = jnp.where(keep, pltpu.roll(ck[q], WINDOW - T, axis=1),
                              pltpu.roll(k_new_t, shift, axis=1))
        nv_ref[q] = jnp.where(keep, pltpu.roll(cv[q], WINDOW - T, axis=1),
                              pltpu.roll(v_new_t, shift, axis=1))

    r = lax.broadcasted_iota(jnp.int32, (L, L), 0)
    c = lax.broadcasted_iota(jnp.int32, (L, L), 1)
    seg = (r // T == c // T) & (r <= c)
    seg_bias = jnp.where(seg, 0.0, -jnp.inf)
    seg_bf = seg.astype(F32).astype(BF16)
    gates = gt_ref[...] * LOG2_E
    cum_row = jnp.zeros(gates.shape, F32)
    for part in _split3(gates):
        cum_row = cum_row + _dot(part, seg_bf)
    ig_rows = gates[:ML_HEADS]
    b_rows = cum_row[ML_HEADS:]
    gate_cols = jnp.concatenate([ig_rows, b_rows, jnp.zeros((LANES - N_GATES, L), F32)],
                                axis=0).T

    def col_to_row(x_col):
        return jnp.broadcast_to(x_col, (L, LANES)).T[0:1, :]

    ones_rows = (r[:LANES] == 0).astype(F32).astype(BF16)
    qm = qm_ref[...]
    km = km_ref[...]
    qm_f = qm.astype(F32)
    km_f = km.astype(F32)
    n_rep = bt * ML_QK_DIM // LANES
    bd_row = lax.broadcasted_iota(jnp.int32, (L, bt * ML_QK_DIM), 0) // T
    bd_lane = lax.broadcasted_iota(jnp.int32, (L, bt * ML_QK_DIM), 1) // ML_QK_DIM
    block_diag = bd_row == bd_lane

    def spread(x_pair, e):
        other = pltpu.roll(x_pair, ML_QK_DIM, axis=1)
        twice = jnp.where(low, x_pair, other) if e == 0 else jnp.where(low, other, x_pair)
        return jnp.where(block_diag, jnp.concatenate([twice] * n_rep, axis=1), 0.0).astype(BF16)

    for h in range(ML_HEADS):
        p, e = divmod(h, 2)
        qc = qm[:, p * LANES:(p + 1) * LANES]
        k_pair = km[:, p * LANES:(p + 1) * LANES]
        zero = jnp.zeros_like(qc)
        q_pad = jnp.where(low, qc, zero) if e == 0 else jnp.where(low, zero, qc)
        v_h = vm_ref[:, h * ML_V_DIM:(h + 1) * ML_V_DIM]
        v_ext_t = jnp.concatenate([v_h.astype(F32).T.astype(BF16), ones_rows], axis=0)
        ig_c = gate_cols[:, h:h + 1]
        b_c = gate_cols[:, ML_HEADS + h:ML_HEADS + h + 1]
        b_r = b_rows[h:h + 1, :]
        m0 = m0_ref[:, :, h:h + 1] * LOG2_E
        inter = b_r + col_to_row(jnp.broadcast_to(m0, (bt, T, 1)).reshape(L, 1))
        dm = (b_r + (ig_c - b_c)) + seg_bias
        m_row = jnp.maximum(inter, jnp.max(dm, axis=0, keepdims=True))
        w_inter = jnp.exp2(inter - m_row)
        p_t = (_dot_nt(k_pair, q_pad) * jnp.exp2(dm - m_row)).astype(BF16)
        num_t = _dot(v_ext_t, p_t)
        q_h3 = qm_f[:, h * ML_QK_DIM:(h + 1) * ML_QK_DIM].reshape(bt, T, ML_QK_DIM)
        k_h3 = km_f[:, h * ML_QK_DIM:(h + 1) * ML_QK_DIM].reshape(bt, T, ML_QK_DIM)
        c0 = c0_ref[:, h]
        n0 = n0_ref[:, h:h + 1, :]
        q_c_t = _dot(spread(qm_f[:, p * LANES:(p + 1) * LANES], e),
                     c0.astype(BF16).reshape(bt * ML_QK_DIM, ML_V_DIM)).T
        q_n_r = col_to_row(jnp.sum(q_h3 * n0, axis=-1, keepdims=True).reshape(L, 1))
        num = num_t[:ML_V_DIM] + w_inter * q_c_t
        den = num_t[ML_V_DIM:ML_V_DIM + 1] + w_inter * q_n_r
        hh = num * (1.0 / jnp.maximum(jnp.abs(den), jnp.exp2(-m_row)))
        ms = jnp.mean(hh * hh, axis=0, keepdims=True)
        mix_scr[:, ATT_Q_W + h * ML_V_DIM:ATT_Q_W + (h + 1) * ML_V_DIM] = (
            (hh * lax.rsqrt(ms + EPS)).T * mlnw_ref[:, h * ML_V_DIM:(h + 1) * ML_V_DIM]
            * _sigmoid(om_ref[:, h * ML_V_DIM:(h + 1) * ML_V_DIM].astype(F32))).astype(BF16)
        b3 = b_c.reshape(bt, T, 1)
        b_last = b3[:, T - 1:T, :]
        a3 = b_last - b3 + ig_c.reshape(bt, T, 1)
        m_new = jnp.maximum(b_last + m0, jnp.max(a3, axis=1, keepdims=True))
        sc = jnp.exp2(b_last + m0 - m_new)
        ws = jnp.exp2(a3 - m_new)
        kw = spread(km_f[:, p * LANES:(p + 1) * LANES] * ws.reshape(L, 1), e)
        d_c = lax.dot_general(kw, v_h, (((0,), (0,)), ((), ())), preferred_element_type=F32)
        c_ref[:, h] = sc * c0 + d_c.reshape(bt, ML_QK_DIM, ML_V_DIM)
        n_ref[:, h:h + 1, :] = sc * n0 + jnp.sum(ws * k_h3, axis=1, keepdims=True)
        m_ref[:, :, h:h + 1] = m_new * (1.0 / LOG2_E)

    x1_ref[...] = x_ref[...] + _dot(mix_scr[...], wperm_scr[...])


def _sample_mixer(nb, t_len, sinks, qa, kv, ck, cv, qm, km, vm, om, gt, c0, n0, m0, x2d, wout, mlnw):
    bt = SAMPLE_BT
    tl = bt * t_len
    row = lambda w: pl.BlockSpec((tl, w), lambda i: (i, 0))
    full = lambda a: pl.BlockSpec(a.shape, lambda i: (0,) * a.ndim)
    once = lambda a: pl.BlockSpec(a.shape, lambda i: (0,) * a.ndim, pipeline_mode=pl.Buffered(1))
    cache = pl.BlockSpec((bt, ATT_KV_W, WINDOW), lambda i: (i, 0, 0))
    c_spec = pl.BlockSpec((bt, ML_HEADS, ML_QK_DIM, ML_V_DIM), lambda i: (i, 0, 0, 0))
    n_spec = pl.BlockSpec((bt, ML_HEADS, ML_QK_DIM), lambda i: (i, 0, 0))
    m_spec = pl.BlockSpec((bt, 1, ML_HEADS), lambda i: (i, 0, 0))
    return pl.pallas_call(
        functools.partial(_sample_mixer_kernel, t_len),
        grid=(nb // bt,),
        in_specs=[pl.BlockSpec(memory_space=pltpu.SMEM),
                  row(ATT_Q_W), row(2 * ATT_KV_W), cache, cache, row(ML_QK_W), row(ML_QK_W),
                  row(ML_V_W), row(ML_V_W), pl.BlockSpec((N_GATES, tl), lambda i: (0, i)),
                  c_spec, n_spec, m_spec, row(D_MODEL), once(wout), full(mlnw)],
        out_specs=[row(D_MODEL), cache, cache, c_spec, n_spec, m_spec],
        out_shape=[jax.ShapeDtypeStruct((nb * t_len, D_MODEL), F32),
                   jax.ShapeDtypeStruct((nb, ATT_KV_W, WINDOW), F32),
                   jax.ShapeDtypeStruct((nb, ATT_KV_W, WINDOW), F32),
                   jax.ShapeDtypeStruct((nb, ML_HEADS, ML_QK_DIM, ML_V_DIM), F32),
                   jax.ShapeDtypeStruct((nb, ML_HEADS, ML_QK_DIM), F32),
                   jax.ShapeDtypeStruct((nb, 1, ML_HEADS), F32)],
        scratch_shapes=[pltpu.VMEM((tl, D_MODEL), BF16),
                        pltpu.VMEM((D_MODEL, D_MODEL), BF16)],
        compiler_params=pltpu.CompilerParams(dimension_semantics=("arbitrary",),
                                             vmem_limit_bytes=VMEM_LIMIT),
        name="sample_mixer",
    )(sinks, qa, kv, ck, cv, qm, km, vm, om, gt, c0, n0, m0, x2d, wout, mlnw)


def _ffn_kernel(seq_rows, *refs):
    if seq_rows is None:
        (x_ref, nw_ref, w_ref, cw_ref, cb_ref, wd_ref, y_ref, conv_ref,
         gbuf, act_scr, carry) = refs
        hist_ref = None
    else:
        (x_ref, hist_ref, nw_ref, w_ref, cw_ref, cb_ref, wd_ref, y_ref, conv_ref,
         gbuf, act_scr) = refs
        carry = None
    tm = x_ref.shape[0]
    tf = FF_CHUNK
    n_hist = CONV_W - 1
    rows = tm if seq_rows is None else seq_rows
    nseq = tm // rows
    base = SUBLANES
    n_chunks = D_FF // tf

    if carry is not None:
        @pl.when(pl.program_id(1) == 0)
        def _():
            carry[...] = jnp.zeros(carry.shape, F32)

    x = x_ref[...]
    h2 = _rms(x, nw_ref[...]).astype(BF16)

    def proj(f):
        return (_dot(h2, w_ref[:, f * tf:(f + 1) * tf]),
                _dot(h2, w_ref[:, D_FF + f * tf:D_FF + (f + 1) * tf]))

    nxt = proj(0)
    for f in range(n_chunks):
        g, u = nxt
        if f + 1 < n_chunks:
            nxt = proj(f + 1)
        cols = slice(f * tf, (f + 1) * tf)
        s = f % 2
        g3 = g.reshape(nseq, rows, tf)
        if seq_rows is None:
            gbuf[s, :, base - n_hist:base, :] = carry[:, SUBLANES - n_hist:, cols]
            carry[:, SUBLANES - n_hist:, cols] = g3[:, rows - n_hist:, :]
        else:
            gbuf[s, :, base - n_hist:base, :] = hist_ref[:, :, cols]
            conv_ref[:, :, cols] = g3[:, rows - n_hist:, :]
        gbuf[s, :, base:base + rows, :] = g3
        gc = cb_ref[:, cols] + g * cw_ref[CONV_W - 1:CONV_W, cols]
        for d in range(1, CONV_W):
            gm = gbuf[s, :, base - d:base - d + rows, :].reshape(tm, tf)
            gc = gc + gm * cw_ref[CONV_W - 1 - d:CONV_W - d, cols]
        act_scr[:, cols] = (gc * _sigmoid(gc) * u).astype(BF16)
    y_ref[...] = x + _dot(act_scr[...], wd_ref[...])

    if carry is not None:
        @pl.when(pl.program_id(1) == pl.num_programs(1) - 1)
        def _():
            conv_ref[...] = carry[:, SUBLANES - n_hist:, :]


def _ffn_scratch(tm, rows):
    return [pltpu.VMEM((2, tm // rows, SUBLANES + rows, FF_CHUNK), F32),
            pltpu.VMEM((tm, D_FF), BF16)]


def _ffn_prompt(batch, seq, x2d, nw, w, cw, cb, wd):
    tm = FFN_TILE
    nt = seq // tm
    full = lambda a: pl.BlockSpec(a.shape, lambda b, i: (0,) * a.ndim)
    once = lambda a: pl.BlockSpec(a.shape, lambda b, i: (0,) * a.ndim,
                                  pipeline_mode=pl.Buffered(1))
    row = pl.BlockSpec((tm, D_MODEL), lambda b, i: (b * nt + i, 0))
    return pl.pallas_call(
        functools.partial(_ffn_kernel, None),
        grid=(batch, nt),
        in_specs=[row, full(nw), once(w), full(cw), full(cb), once(wd)],
        out_specs=[row, pl.BlockSpec((1, CONV_W - 1, D_FF), lambda b, i: (b, 0, 0))],
        out_shape=[jax.ShapeDtypeStruct((batch * seq, D_MODEL), F32),
                   jax.ShapeDtypeStruct((batch, CONV_W - 1, D_FF), F32)],
        scratch_shapes=_ffn_scratch(tm, tm) + [pltpu.VMEM((1, SUBLANES, D_FF), F32)],
        compiler_params=pltpu.CompilerParams(dimension_semantics=("arbitrary", "arbitrary"),
                                             vmem_limit_bytes=VMEM_LIMIT),
        name="ffn_prompt",
    )(x2d, nw, w, cw, cb, wd)


def _ffn_sample(nb, t_len, x2d, hist, nw, w, cw, cb, wd):
    tm = ROW_TILE
    bt = tm // t_len
    full = lambda a: pl.BlockSpec(a.shape, lambda i: (0,) * a.ndim)
    once = lambda a: pl.BlockSpec(a.shape, lambda i: (0,) * a.ndim, pipeline_mode=pl.Buffered(1))
    row = pl.BlockSpec((tm, D_MODEL), lambda i: (i, 0))
    hist_spec = pl.BlockSpec((bt, CONV_W - 1, D_FF), lambda i: (i, 0, 0))
    return pl.pallas_call(
        functools.partial(_ffn_kernel, t_len),
        grid=(nb // bt,),
        in_specs=[row, hist_spec, full(nw), once(w), full(cw), full(cb), once(wd)],
        out_specs=[row, hist_spec],
        out_shape=[jax.ShapeDtypeStruct((nb * t_len, D_MODEL), F32),
                   jax.ShapeDtypeStruct((nb, CONV_W - 1, D_FF), F32)],
        scratch_shapes=_ffn_scratch(tm, t_len),
        compiler_params=pltpu.CompilerParams(dimension_semantics=("arbitrary",),
                                             vmem_limit_bytes=VMEM_LIMIT),
        name="ffn_sample",
    )(x2d, hist, nw, w, cw, cb, wd)


def _head_mean_matrix(width, head_dim):
    idx = np.arange(width) // head_dim
    return jnp.asarray((idx[:, None] == idx[None, :]).astype(np.float32) / head_dim, dtype=BF16)


def _layer_weights(norm_mix_w, w_in, b_gates, q_norm_w, k_norm_w, sinks, ml_norm_w, w_out,
                   norm_ffn_w, w_ffn_in, conv_w, conv_b, w_down):
    w_in_t = jnp.pad(w_in.T.astype(BF16), ((0, IN_WIDTH_PAD - w_in.shape[1]), (0, 0)))
    return dict(
        nw=norm_mix_w.reshape(1, D_MODEL),
        w_in_t=w_in_t,
        bg=jnp.pad(b_gates, (0, LANES - N_GATES)).reshape(1, LANES),
        qnw=(jnp.tile(q_norm_w, ATT_HEADS) * ATT_SCALE).reshape(1, ATT_Q_W),
        knw=jnp.tile(k_norm_w, ATT_KV_HEADS).reshape(1, ATT_KV_W),
        gq=_head_mean_matrix(ATT_Q_W, ATT_HEAD_DIM),
        gk=_head_mean_matrix(ATT_KV_W, ATT_HEAD_DIM),
        bg_col=b_gates.reshape(N_GATES, 1),
        qnw_col=(jnp.tile(q_norm_w, ATT_HEADS) * (ATT_SCALE * LOG2_E)).reshape(ATT_Q_W, 1),
        knw_col=jnp.tile(k_norm_w, ATT_KV_HEADS).reshape(ATT_KV_W, 1),
        mlnw_col=ml_norm_w.reshape(ML_V_W, 1),
        sinks=sinks,
        mlnw=ml_norm_w.reshape(1, ML_V_W),
        wout=w_out.astype(BF16),
        nfw=norm_ffn_w.reshape(1, D_MODEL),
        wff=w_ffn_in.astype(BF16),
        cw=conv_w,
        cb=conv_b.reshape(1, D_FF),
        wd=w_down.astype(BF16),
    )


def _cache_from_t(a_t):
    n = a_t.shape[0]
    return jnp.transpose(a_t.reshape(n, ATT_KV_HEADS, ATT_HEAD_DIM, WINDOW), (0, 3, 1, 2))


def _cache_to_t(a):
    n = a.shape[0]
    return jnp.transpose(a, (0, 2, 3, 1)).reshape(n, ATT_KV_W, WINDOW)


def _prompt_layer(x, w):
    batch, seq, _ = x.shape
    assert seq % ROW_TILE == 0 and MIX_TILE == ROW_TILE and MIX_TILE % WINDOW == 0
    assert seq % FFN_TILE == 0
    x2d = x.reshape(batch * seq, D_MODEL)
    qa, ks, kv, qm, km, vm, om, gt = _inproj_t(x2d, w["nw"], w["w_in_t"], w["bg_col"],
                                               w["qnw_col"], w["knw_col"])
    x1, c_t, n_row, m, k_t, v_t = _prompt_mixer_t(batch, seq, w["sinks"], qa, ks, kv, qm, km, vm,
                                                  om, gt, x2d, w["wout"], w["mlnw_col"])
    y, conv = _ffn_prompt(batch, seq, x1, w["nfw"], w["wff"], w["cw"], w["cb"], w["wd"])
    return (y.reshape(batch, seq, D_MODEL), _cache_from_t(k_t), _cache_from_t(v_t),
            jnp.swapaxes(c_t, -1, -2), n_row.reshape(batch, ML_HEADS, ML_QK_DIM),
            m.reshape(batch, ML_HEADS), conv)


def _sample_layer(x, ck, cv, c0, n0, m0, conv_buf, w):
    nb, t_len, _ = x.shape
    assert t_len == SUBLANES and SAMPLE_BT * t_len == LANES and nb % SAMPLE_BT == 0
    assert (nb * t_len) % ROW_TILE == 0
    x2d = x.reshape(nb * t_len, D_MODEL)
    qa, kv, qm, km, vm, om, gt = _inproj(x2d, w["nw"], w["w_in_t"], w["bg"], w["qnw"], w["knw"],
                                         w["gq"], w["gk"])
    x1, nk_t, nv_t, c_t, n, m = _sample_mixer(
        nb, t_len, w["sinks"], qa, kv, _cache_to_t(ck), _cache_to_t(cv), qm, km, vm, om, gt,
        jnp.swapaxes(c0, -1, -2), n0, m0.reshape(nb, 1, ML_HEADS), x2d, w["wout"], w["mlnw"])
    y, conv = _ffn_sample(nb, t_len, x1, conv_buf, w["nfw"], w["wff"], w["cw"], w["cb"],
                          w["wd"])
    return (y.reshape(nb, t_len, D_MODEL), _cache_from_t(nk_t), _cache_from_t(nv_t),
            jnp.swapaxes(c_t, -1, -2), n, m.reshape(nb, ML_HEADS), conv)


def kernel(x_prompt, x_sample, cache_attn_k, cache_attn_v, state_mlstm_C, state_mlstm_n,
           state_mlstm_m, cache_ffn_conv, norm_mix_w, w_in, b_gates, q_norm_w, k_norm_w,
           sinks, ml_norm_w, w_out, norm_ffn_w, w_ffn_in, conv_w, conv_b, w_down):
    depth = w_in.shape[0]
    yp, ys = x_prompt, x_sample
    sp = [[] for _ in range(6)]
    ss = [[] for _ in range(6)]
    for l in range(depth):
        w = _layer_weights(norm_mix_w[l], w_in[l], b_gates[l], q_norm_w[l], k_norm_w[l], sinks[l],
                           ml_norm_w[l], w_out[l], norm_ffn_w[l], w_ffn_in[l], conv_w[l],
                           conv_b[l], w_down[l])
        yp, *st_p = _prompt_layer(yp, w)
        ys, *st_s = _sample_layer(ys, cache_attn_k[l], cache_attn_v[l], state_mlstm_C[l],
                                  state_mlstm_n[l], state_mlstm_m[l], cache_ffn_conv[l], w)
        for i in range(6):
            sp[i].append(st_p[i])
            ss[i].append(st_s[i])
    k_p, v_p, c_p, n_p, m_p, conv_p = [jnp.stack(a) for a in sp]
    k_s, v_s, c_s, n_s, m_s, conv_s = [jnp.stack(a) for a in ss]
    return (yp, ys, k_p, v_p, c_p, n_p, m_p, conv_p, k_s, v_s, c_s, n_s, m_s, conv_s)
```

```python
import functools

import numpy as np
import jax
import jax.numpy as jnp
from jax import lax
from jax.experimental import pallas as pl
from jax.experimental.pallas import tpu as pltpu

F32 = jnp.float32
BF16 = jnp.bfloat16

D_MODEL = 1024
ATT_HEADS = 8
ATT_KV_HEADS = 2
ATT_HEAD_DIM = 64
ATT_GROUP = ATT_HEADS // ATT_KV_HEADS
WINDOW = 128
ML_HEADS = 4
ML_V_DIM = 128
ML_QK_DIM = 64
D_FF = 2816
CONV_W = 3
EPS = 1e-6
ATT_SCALE = ATT_HEAD_DIM ** -0.5
ML_SCALE = ML_QK_DIM ** -0.5
LOG2_E = 1.4426950408889634

ATT_Q_W = ATT_HEADS * ATT_HEAD_DIM
ATT_KV_W = ATT_KV_HEADS * ATT_HEAD_DIM
ML_QK_W = ML_HEADS * ML_QK_DIM
ML_V_W = ML_HEADS * ML_V_DIM
N_GATES = 2 * ML_HEADS
N_STACK = 2 * ATT_GROUP

LANES = 128
SUBLANES = 8

OFF_QA = 0
OFF_KV = OFF_QA + ATT_Q_W
OFF_QM = OFF_KV + 2 * ATT_KV_W
OFF_KM = OFF_QM + ML_QK_W
OFF_VM = OFF_KM + ML_QK_W
OFF_OM = OFF_VM + ML_V_W
OFF_GL = OFF_OM + ML_V_W
IN_WIDTH_PAD = OFF_GL + LANES

ATT_HEAD_ORDER = tuple(h for c in range(ATT_GROUP) for h in (c, c + ATT_GROUP))

ROW_TILE = 512
FFN_TILE = 1024
INPROJ_SUB = 256
NORM_ROWS = 64
MIX_TILE = 512
ML_CHUNK = 256
OUT_COLS = 256
FF_CHUNK = 256
SAMPLE_BT = 16
VMEM_LIMIT = 56 * 1024 * 1024


def _dot(a, b):
    return jnp.dot(a, b, preferred_element_type=F32)


def _dot_nt(a, b):
    return lax.dot_general(a, b, (((1,), (1,)), ((), ())), preferred_element_type=F32)


def _split3(x):
    hi = x.astype(BF16)
    r1 = x - hi.astype(F32)
    mid = r1.astype(BF16)
    lo = (r1 - mid.astype(F32)).astype(BF16)
    return hi, mid, lo


def _rms(x, w):
    ms = jnp.mean(x * x, axis=-1, keepdims=True)
    return x * lax.rsqrt(ms + EPS) * w


def _log_sigmoid(x):
    return jnp.minimum(x, 0.0) - jnp.log1p(jnp.exp(-jnp.abs(x)))


def _sigmoid(x):
    return 1.0 / (1.0 + jnp.exp(-x))


def _permute_head_rows(dst_ref, src_ref):
    for k, h in enumerate(ATT_HEAD_ORDER):
        dst_ref[k * ATT_HEAD_DIM:(k + 1) * ATT_HEAD_DIM, :] = (
            src_ref[h * ATT_HEAD_DIM:(h + 1) * ATT_HEAD_DIM, :])


def _inproj_kernel(x_ref, nw_ref, w_ref, bg_ref, qnw_ref, knw_ref, gq_ref, gk_ref,
                   qa_ref, kv_ref, qm_ref, km_ref, vm_ref, om_ref, gt_ref, wq_scr):
    @pl.when(pl.program_id(0) == 0)
    def _():
        _permute_head_rows(wq_scr, w_ref)

    h = _rms(x_ref[...], nw_ref[...]).astype(BF16)

    def proj(lo, width):
        return _dot_nt(h, w_ref[lo:lo + width, :])

    q = _dot_nt(h, wq_scr[...])
    q_ms = _dot((q * q).astype(BF16), gq_ref[...])
    qa_ref[...] = (q * lax.rsqrt(q_ms + EPS) * qnw_ref[...]).astype(BF16)

    kv = proj(OFF_KV, 2 * ATT_KV_W)
    k = kv[:, :ATT_KV_W]
    k_ms = _dot((k * k).astype(BF16), gk_ref[...])
    kv_ref[:, :ATT_KV_W] = k * lax.rsqrt(k_ms + EPS) * knw_ref[...]
    kv_ref[:, ATT_KV_W:] = kv[:, ATT_KV_W:]

    qm_ref[...] = (proj(OFF_QM, ML_QK_W) * ML_SCALE).astype(BF16)
    km_ref[...] = proj(OFF_KM, ML_QK_W).astype(BF16)
    vm_ref[...] = proj(OFF_VM, ML_V_W).astype(BF16)
    om_ref[...] = proj(OFF_OM, ML_V_W).astype(BF16)

    gl = proj(OFF_GL, LANES) + bg_ref[...]
    lane = lax.broadcasted_iota(jnp.int32, gl.shape, 1)
    g = jnp.where(lane < ML_HEADS, gl, _log_sigmoid(gl))
    gt_ref[...] = g.T[:N_GATES, :]


def _inproj(x2d, nw, w_in_t, bg, qnw, knw, gq, gk):
    n = x2d.shape[0]
    tm = ROW_TILE
    row = lambda w: pl.BlockSpec((tm, w), lambda i: (i, 0))
    full = lambda a: pl.BlockSpec(a.shape, lambda i: (0,) * a.ndim)
    once = lambda a: pl.BlockSpec(a.shape, lambda i: (0,) * a.ndim, pipeline_mode=pl.Buffered(1))
    return pl.pallas_call(
        _inproj_kernel,
        grid=(n // tm,),
        in_specs=[row(D_MODEL), full(nw), once(w_in_t), full(bg), full(qnw), full(knw),
                  full(gq), full(gk)],
        out_specs=[row(ATT_Q_W), row(2 * ATT_KV_W), row(ML_QK_W), row(ML_QK_W),
                   row(ML_V_W), row(ML_V_W), pl.BlockSpec((N_GATES, tm), lambda i: (0, i))],
        out_shape=[jax.ShapeDtypeStruct((n, ATT_Q_W), BF16),
                   jax.ShapeDtypeStruct((n, 2 * ATT_KV_W), F32),
                   jax.ShapeDtypeStruct((n, ML_QK_W), BF16),
                   jax.ShapeDtypeStruct((n, ML_QK_W), BF16),
                   jax.ShapeDtypeStruct((n, ML_V_W), BF16),
                   jax.ShapeDtypeStruct((n, ML_V_W), BF16),
                   jax.ShapeDtypeStruct((N_GATES, n), F32)],
        scratch_shapes=[pltpu.VMEM((ATT_Q_W, D_MODEL), BF16)],
        compiler_params=pltpu.CompilerParams(dimension_semantics=("arbitrary",),
                                             vmem_limit_bytes=VMEM_LIMIT),
        name="inproj",
    )(x2d, nw, w_in_t, bg, qnw, knw, gq, gk)


def _head_norm_t(z, head_dim, w_col):
    rows, tokens = z.shape
    z3 = z.reshape(rows // head_dim, head_dim, tokens)
    ms = jnp.mean(z3 * z3, axis=1, keepdims=True)
    return (z3 * lax.rsqrt(ms + EPS)).reshape(rows, tokens) * w_col


def _inproj_t_kernel(x_ref, nw_ref, w_ref, bg_ref, qnw_ref, knw_ref,
                     qa_ref, ks_ref, kv_ref, qm_ref, km_ref, vm_ref, om_ref, gt_ref, h_scr):
    tm = x_ref.shape[0]
    sub = INPROJ_SUB

    def norm_rows(c):
        for r0 in range(c * sub, (c + 1) * sub, NORM_ROWS):
            rows = slice(r0, r0 + NORM_ROWS)
            h_scr[rows, :] = _rms(x_ref[rows, :], nw_ref[...]).astype(BF16)
            yield

    def project(c):
        tok = slice(c * sub, (c + 1) * sub)
        h = h_scr[tok, :]

        def proj(lo, width):
            return _dot_nt(w_ref[lo:lo + width, :], h)

        qa_ref[:, tok] = _head_norm_t(proj(OFF_QA, ATT_Q_W), ATT_HEAD_DIM,
                                      qnw_ref[...]).astype(BF16)
        yield
        kv = proj(OFF_KV, 2 * ATT_KV_W)
        k = _head_norm_t(kv[:ATT_KV_W], ATT_HEAD_DIM, knw_ref[...])
        kv_ref[:ATT_KV_W, tok] = k
        kv_ref[ATT_KV_W:, tok] = kv[ATT_KV_W:]
        ks_ref[tok, :] = k.T.astype(BF16)
        qm_ref[:, tok] = (proj(OFF_QM, ML_QK_W) * ML_SCALE).astype(BF16)
        yield
        km_ref[:, tok] = proj(OFF_KM, ML_QK_W).astype(BF16)
        vm_ref[:, tok] = proj(OFF_VM, ML_V_W).astype(BF16)
        yield
        om_ref[:, tok] = proj(OFF_OM, ML_V_W).astype(BF16)
        gl = proj(OFF_GL, 2 * SUBLANES)[:N_GATES] + bg_ref[...]
        row = lax.broadcasted_iota(jnp.int32, gl.shape, 0)
        gt_ref[:, tok] = jnp.where(row < ML_HEADS, gl, _log_sigmoid(gl))
        yield

    for _ in norm_rows(0):
        pass
    for c in range(tm // sub):
        norms = norm_rows(c + 1) if c + 1 < tm // sub else iter(())
        for _ in project(c):
            next(norms, None)
        for _ in norms:
            pass


def _inproj_t(x2d, nw, w_in_t, bg_col, qnw_col, knw_col):
    n = x2d.shape[0]
    tm = ROW_TILE
    full = lambda a: pl.BlockSpec(a.shape, lambda i: (0,) * a.ndim)
    once = lambda a: pl.BlockSpec(a.shape, lambda i: (0,) * a.ndim, pipeline_mode=pl.Buffered(1))
    col = lambda w: pl.BlockSpec((None, w, tm), lambda i: (i, 0, 0))
    slab = lambda w, dt: jax.ShapeDtypeStruct((n // tm, w, tm), dt)
    return pl.pallas_call(
        _inproj_t_kernel,
        grid=(n // tm,),
        in_specs=[pl.BlockSpec((tm, D_MODEL), lambda i: (i, 0)), full(nw), once(w_in_t),
                  full(bg_col), full(qnw_col), full(knw_col)],
        out_specs=[col(ATT_Q_W), pl.BlockSpec((tm, ATT_KV_W), lambda i: (i, 0)),
                   col(2 * ATT_KV_W), col(ML_QK_W), col(ML_QK_W), col(ML_V_W), col(ML_V_W),
                   col(N_GATES)],
        out_shape=[slab(ATT_Q_W, BF16),
                   jax.ShapeDtypeStruct((n, ATT_KV_W), BF16),
                   slab(2 * ATT_KV_W, F32), slab(ML_QK_W, BF16), slab(ML_QK_W, BF16),
                   slab(ML_V_W, BF16), slab(ML_V_W, BF16), slab(N_GATES, F32)],
        scratch_shapes=[pltpu.VMEM((tm, D_MODEL), BF16)],
        compiler_params=pltpu.CompilerParams(dimension_semantics=("arbitrary",),
                                             vmem_limit_bytes=VMEM_LIMIT),
        name="inproj_t",
    )(x2d, nw, w_in_t, bg_col, qnw_col, knw_col)


def _gate_forms(gates, seg_mask, want_raw_col):
    L = gates.shape[1]
    m_bf = seg_mask.astype(F32).astype(BF16)
    cum_row = jnp.zeros(gates.shape, F32)
    cum_col = jnp.zeros((L, gates.shape[0]), F32)
    raw_col = None
    if want_raw_col:
        r = lax.broadcasted_iota(jnp.int32, (L, L), 0)
        c = lax.broadcasted_iota(jnp.int32, (L, L), 1)
        eye = (r == c).astype(F32).astype(BF16)
        raw_col = jnp.zeros((L, gates.shape[0]), F32)
    for part in _split3(gates):
        cum_row = cum_row + _dot_nt(part, m_bf)
        cum_col = cum_col + _dot_nt(m_bf, part)
        if want_raw_col:
            raw_col = raw_col + _dot_nt(eye, part)
    return cum_row, cum_col, raw_col


def _mlstm_intra(q_pad, k_pair, v_ext, seg_mask, b_c, b_r, ig_r, m_prev_c):
    dm = jnp.where(seg_mask, b_c + (ig_r - b_r), -jnp.inf)
    inter = b_c + m_prev_c
    m_row = jnp.maximum(inter, jnp.max(dm, axis=-1, keepdims=True))
    w_inter = jnp.exp(inter - m_row)
    p = _dot_nt(q_pad, k_pair) * jnp.exp(dm - m_row)
    return _dot(p.astype(BF16), v_ext), m_row, w_inter


def _mlstm_out(pv, m_row, w_inter, q_c, q_n, mlnw_h, om_h):
    num = pv[:, :ML_V_DIM] + w_inter * q_c
    den = pv[:, ML_V_DIM:ML_V_DIM + 1] + w_inter * q_n
    hh = num / jnp.maximum(jnp.abs(den), jnp.exp(-m_row))
    return (_rms(hh, mlnw_h) * _sigmoid(om_h.astype(F32))).astype(BF16)


def _ones_col(rows):
    lane = lax.broadcasted_iota(jnp.int32, (rows, LANES), 1)
    return (lane == 0).astype(F32).astype(BF16)


def _prompt_mixer_t_kernel(sinks_ref, qa_ref, ksc_ref, ksp_ref, kvc_ref, kvp_ref, qm_ref, km_ref,
                           vm_ref, om_ref, gt_ref, x_ref, wout_ref, mlnw_ref,
                           x1_ref, ct_ref, nrow_ref, m_ref, kt_ref, vt_ref,
                           mix_scr, state_scr, m_scr, band_scr, causal_scr, tri_scr,
                           s_scr_a, s_scr_b, e_scr):
    i = pl.program_id(1)
    A = WINDOW
    L = MIX_TILE
    C = ML_CHUNK
    n_pairs = ML_HEADS // 2

    @pl.when(i == 0)
    def _():
        state_scr[...] = jnp.zeros(state_scr.shape, F32)
        m_scr[...] = jnp.zeros(m_scr.shape, F32)
        kj = lax.broadcasted_iota(jnp.int32, (2 * A, A), 0)
        qi = lax.broadcasted_iota(jnp.int32, (2 * A, A), 1)
        band = (kj > qi) & (kj <= qi + WINDOW)
        band_scr[0] = jnp.where(band, 0.0, -jnp.inf)
        band_scr[1] = jnp.where(band & (kj >= A), 0.0, -jnp.inf)
        r = lax.broadcasted_iota(jnp.int32, (C, C), 0)
        c = lax.broadcasted_iota(jnp.int32, (C, C), 1)
        causal_scr[...] = jnp.where(r <= c, 0.0, -jnp.inf)
        tri_scr[...] = (r <= c).astype(F32).astype(BF16)

    k_all = jnp.concatenate([ksp_ref[...], ksc_ref[...]], axis=0)
    v_all = jnp.concatenate([kvp_ref[ATT_KV_W:, :], kvc_ref[ATT_KV_W:, :]], axis=1).astype(BF16)
    zero_q = jnp.zeros((ATT_HEAD_DIM, A), BF16)
    slot = 0
    s_bufs = (s_scr_a, s_scr_b)

    def stage_scores(j):
        pieces = []
        for h in range(ATT_HEADS):
            q_h = qa_ref[h * ATT_HEAD_DIM:(h + 1) * ATT_HEAD_DIM, j * A:(j + 1) * A]
            pieces.append(jnp.concatenate([q_h, zero_q] if h < ATT_GROUP else [zero_q, q_h],
                                          axis=0))
        s_bufs[j % 2][slot] = _dot(k_all[j * A:(j + 2) * A, :], jnp.concatenate(pieces, axis=1))

    def attend(j):
        cols = slice(j * A, (j + 1) * A)
        vt = v_all[:, j * A:(j + 2) * A]
        if j + 1 < L // A:
            stage_scores(j + 1)
        s_buf = s_bufs[j % 2]
        bias = jnp.where(i > 0, band_scr[0], band_scr[1]) if j == 0 else band_scr[0]
        m_rows = []
        for h in range(ATT_HEADS):
            sb = s_buf[slot, :, h * A:(h + 1) * A] + bias
            m_rows.append(jnp.maximum(jnp.max(sb, axis=0, keepdims=True),
                                      sinks_ref[h] * LOG2_E))
        inv_rows = []
        for h in range(ATT_HEADS):
            e = jnp.exp2(s_buf[slot, :, h * A:(h + 1) * A] + (bias - m_rows[h]))
            e_scr[:, h * A:(h + 1) * A] = e.astype(BF16)
            inv_rows.append(1.0 / (jnp.sum(e, axis=0, keepdims=True)
                                   + jnp.exp2(sinks_ref[h] * LOG2_E - m_rows[h])))
        o = _dot(vt, e_scr[...])
        for h in range(ATT_HEADS):
            g = h // ATT_GROUP
            mix_scr[h * ATT_HEAD_DIM:(h + 1) * ATT_HEAD_DIM, cols] = (
                o[g * ATT_HEAD_DIM:(g + 1) * ATT_HEAD_DIM, h * A:(h + 1) * A]
                * inv_rows[h]).astype(BF16)

    row128 = lax.broadcasted_iota(jnp.int32, (LANES, C), 0)
    ones_rows = (row128 == 0).astype(F32).astype(BF16)

    def mlstm_chunk(ci):
        tok = slice(ci * C, (ci + 1) * C)
        gates = gt_ref[:, tok] * LOG2_E
        cum_row = jnp.zeros(gates.shape, F32)
        for part in _split3(gates):
            cum_row = cum_row + _dot(part, tri_scr[...])
        ig_rows = gates[:ML_HEADS]
        b_rows = cum_row[ML_HEADS:]
        key_cols = jnp.concatenate([ig_rows - b_rows, jnp.zeros((LANES - ML_HEADS, C), F32)],
                                   axis=0).T
        for p in range(n_pairs):
            q_c = qm_ref[p * LANES:(p + 1) * LANES, tok]
            k_pair = km_ref[p * LANES:(p + 1) * LANES, tok]
            zero = jnp.zeros_like(q_c)
            state = state_scr[p]
            state_bf = state.astype(BF16)
            new_state = []
            for e_id in range(2):
                h = 2 * p + e_id
                v_rows = slice(h * ML_V_DIM, (h + 1) * ML_V_DIM)
                head_rows = (row128 < ML_QK_DIM) if e_id == 0 else (row128 >= ML_QK_DIM)
                q_pad = jnp.where(head_rows, q_c, zero)
                b_r = b_rows[h:h + 1, :]
                ig_r = ig_rows[h:h + 1, :]
                m_prev = m_scr[h:h + 1, 0:1]
                dm = (b_r + key_cols[:, h:h + 1]) + causal_scr[...]
                inter = b_r + m_prev
                m_row = jnp.maximum(inter, jnp.max(dm, axis=0, keepdims=True))
                w_inter = jnp.exp2(inter - m_row)
                qk = lax.dot_general(k_pair, q_pad, (((0,), (0,)), ((), ())),
                                     preferred_element_type=F32)
                p_t = (qk * jnp.exp2(dm - m_row)).astype(BF16)
                v_ext = jnp.concatenate([vm_ref[v_rows, tok], ones_rows], axis=0)
                num = _dot(v_ext, p_t) + w_inter * _dot(state_bf, q_pad)
                den = num[ML_V_DIM:ML_V_DIM + 1, :]
                hh = num[:ML_V_DIM] * (1.0 / jnp.maximum(jnp.abs(den), jnp.exp2(-m_row)))
                ms = jnp.mean(hh * hh, axis=0, keepdims=True)
                gate = _sigmoid(om_ref[v_rows, tok].astype(F32))
                mix_scr[ATT_Q_W + h * ML_V_DIM:ATT_Q_W + (h + 1) * ML_V_DIM, tok] = (
                    hh * lax.rsqrt(ms + EPS) * mlnw_ref[v_rows, :] * gate).astype(BF16)
                b_last = b_r[:, C - 1:C]
                a_r = b_last - b_r + ig_r
                m_new = jnp.maximum(b_last + m_prev, jnp.max(a_r, axis=-1, keepdims=True))
                sc = jnp.exp2(b_last + m_prev - m_new)
                wsv = (v_ext.astype(F32) * jnp.exp2(a_r - m_new)).astype(BF16)
                new_state.append(sc * state + _dot_nt(wsv, k_pair))
                m_scr[h:h + 1, :] = jnp.broadcast_to(m_new, (1, LANES))
            first = lax.broadcasted_iota(jnp.int32, state.shape, 1) < ML_QK_DIM
            state_scr[p] = jnp.where(first, new_state[0], new_state[1])
            yield

    def out_proj(ci):
        tok = slice(ci * C, (ci + 1) * C)
        mix_t = mix_scr[:, tok].T
        for n in range(D_MODEL // OUT_COLS):
            nc = slice(n * OUT_COLS, (n + 1) * OUT_COLS)
            x1_ref[tok, nc] = x_ref[tok, nc] + _dot(mix_t, wout_ref[:, nc])
            yield

    stage_scores(0)
    n_sub = C // A
    pairs = (step for ci in range(L // C) for step in mlstm_chunk(ci))
    projs = iter(())
    for j in range(L // A):
        if j and j % n_sub == 0:
            projs = out_proj(j // n_sub - 1)
        attend(j)
        next(projs, None)
        next(pairs, None)
        next(projs, None)
    for _ in pairs:
        pass
    for _ in projs:
        pass
    for _ in out_proj(L // C - 1):
        pass

    @pl.when(i == pl.num_programs(1) - 1)
    def _():
        for p in range(n_pairs):
            c_t = state_scr[p, :ML_V_DIM, :].T
            for e_id in range(2):
                ct_ref[0, 2 * p + e_id] = c_t[e_id * ML_QK_DIM:(e_id + 1) * ML_QK_DIM, :]
            nrow_ref[0, p:p + 1, :] = state_scr[p, ML_V_DIM:ML_V_DIM + 1, :]
        for h in range(ML_HEADS):
            m_ref[0, :, h:h + 1] = m_scr[h:h + 1, 0:1] * (1.0 / LOG2_E)
        kt_ref[0] = kvc_ref[:ATT_KV_W, L - WINDOW:]
        vt_ref[0] = kvc_ref[ATT_KV_W:, L - WINDOW:]


def _prompt_mixer_t(batch, seq, sinks, qa, ks, kv, qm, km, vm, om, gt, x2d, wout, mlnw_col):
    tq = MIX_TILE
    nt = seq // tq
    sub = tq // WINDOW
    col = lambda w: pl.BlockSpec((None, w, tq), lambda b, i: (b * nt + i, 0, 0))
    full = lambda a: pl.BlockSpec(a.shape, lambda b, i: (0,) * a.ndim)
    once = lambda a: pl.BlockSpec(a.shape, lambda b, i: (0,) * a.ndim,
                                  pipeline_mode=pl.Buffered(1))
    prev_block = lambda b, i: jnp.maximum((b * nt + i) * sub - 1, 0)
    per_batch = lambda *dims: pl.BlockSpec((1,) + dims, lambda b, i: (b,) + (0,) * len(dims))
    return pl.pallas_call(
        _prompt_mixer_t_kernel,
        grid=(batch, nt),
        in_specs=[pl.BlockSpec(memory_space=pltpu.SMEM),
                  col(ATT_Q_W),
                  pl.BlockSpec((tq, ATT_KV_W), lambda b, i: (b * nt + i, 0)),
                  pl.BlockSpec((WINDOW, ATT_KV_W), lambda b, i: (prev_block(b, i), 0)),
                  col(2 * ATT_KV_W),
                  pl.BlockSpec((None, 2 * ATT_KV_W, WINDOW),
                               lambda b, i: (prev_block(b, i) // sub, 0, prev_block(b, i) % sub)),
                  col(ML_QK_W), col(ML_QK_W), col(ML_V_W), col(ML_V_W), col(N_GATES),
                  pl.BlockSpec((tq, D_MODEL), lambda b, i: (b * nt + i, 0)),
                  once(wout), full(mlnw_col)],
        out_specs=[pl.BlockSpec((tq, D_MODEL), lambda b, i: (b * nt + i, 0)),
                   per_batch(ML_HEADS, ML_QK_DIM, ML_V_DIM),
                   per_batch(ML_HEADS // 2, LANES),
                   per_batch(1, ML_HEADS),
                   per_batch(ATT_KV_W, WINDOW),
                   per_batch(ATT_KV_W, WINDOW)],
        out_shape=[jax.ShapeDtypeStruct((batch * seq, D_MODEL), F32),
                   jax.ShapeDtypeStruct((batch, ML_HEADS, ML_QK_DIM, ML_V_DIM), F32),
                   jax.ShapeDtypeStruct((batch, ML_HEADS // 2, LANES), F32),
                   jax.ShapeDtypeStruct((batch, 1, ML_HEADS), F32),
                   jax.ShapeDtypeStruct((batch, ATT_KV_W, WINDOW), F32),
                   jax.ShapeDtypeStruct((batch, ATT_KV_W, WINDOW), F32)],
        scratch_shapes=[pltpu.VMEM((D_MODEL, tq), BF16),
                        pltpu.VMEM((ML_HEADS // 2, 2 * LANES, LANES), F32),
                        pltpu.VMEM((SUBLANES, LANES), F32),
                        pltpu.VMEM((2, 2 * WINDOW, WINDOW), F32),
                        pltpu.VMEM((ML_CHUNK, ML_CHUNK), F32),
                        pltpu.VMEM((ML_CHUNK, ML_CHUNK), BF16),
                        pltpu.VMEM((2, 2 * WINDOW, ATT_HEADS * WINDOW), F32),
                        pltpu.VMEM((2, 2 * WINDOW, ATT_HEADS * WINDOW), F32),
                        pltpu.VMEM((2 * WINDOW, ATT_HEADS * WINDOW), BF16)],
        compiler_params=pltpu.CompilerParams(dimension_semantics=("arbitrary", "arbitrary"),
                                             vmem_limit_bytes=VMEM_LIMIT),
        name="prompt_mixer_t",
    )(sinks, qa, ks, ks, kv, kv, qm, km, vm, om, gt, x2d, wout, mlnw_col)


def _sample_mixer_kernel(t_len, sinks_ref, qa_ref, kv_ref, ck_ref, cv_ref, qm_ref, km_ref,
                         vm_ref, om_ref, gt_ref, c0_ref, n0_ref, m0_ref, x_ref, wout_ref,
                         mlnw_ref, x1_ref, nk_ref, nv_ref, c_ref, n_ref, m_ref,
                         mix_scr, wperm_scr):
    bt = SAMPLE_BT
    T = t_len
    L = bt * T

    @pl.when(pl.program_id(0) == 0)
    def _():
        _permute_head_rows(wperm_scr, wout_ref)
        wperm_scr[ATT_Q_W:, :] = wout_ref[ATT_Q_W:, :]

    lane3 = lax.broadcasted_iota(jnp.int32, (bt, T, LANES), 2)
    low3 = lane3 < ATT_HEAD_DIM
    lane = lax.broadcasted_iota(jnp.int32, (L, LANES), 1)
    low = lane < ATT_HEAD_DIM

    qa3 = qa_ref[...].astype(F32).reshape(bt, T, ATT_Q_W)
    pieces = []
    for col in range(ATT_GROUP):
        qc = qa3[:, :, col * LANES:(col + 1) * LANES]
        pieces += [jnp.where(low3, qc, 0.0), jnp.where(low3, 0.0, qc)]
    q3 = jnp.concatenate(pieces, axis=1).astype(BF16)
    R = bt * N_STACK * T
    q2 = q3.reshape(R, LANES)
    kv_new = kv_ref[...]
    k_new = kv_new[:, :ATT_KV_W]
    v_new = kv_new[:, ATT_KV_W:]
    ck = ck_ref[...]
    cv = cv_ref[...]
    s_c = jnp.einsum('bqd,bdk->bqk', q3, ck.astype(BF16),
                     preferred_element_type=F32).reshape(R, WINDOW)
    s_n = _dot_nt(q2, k_new.astype(BF16))
    row_c = lax.broadcasted_iota(jnp.int32, (R, WINDOW), 0)
    col_c = lax.broadcasted_iota(jnp.int32, (R, WINDOW), 1)
    s_c = jnp.where(col_c > row_c % T, s_c, -jnp.inf)
    row_n = lax.broadcasted_iota(jnp.int32, (R, L), 0)
    col_n = lax.broadcasted_iota(jnp.int32, (R, L), 1)
    valid_n = (row_n // (N_STACK * T) == col_n // T) & (col_n % T <= row_n % T)
    s_n = jnp.where(valid_n, s_n, -jnp.inf)
    stack_id = (lax.broadcasted_iota(jnp.int32, (R, 1), 0) // T) % N_STACK
    sink = jnp.zeros((R, 1), F32)
    for k_id in range(N_STACK):
        sink = jnp.where(stack_id == k_id, sinks_ref[ATT_HEAD_ORDER[k_id]], sink)
    m = jnp.maximum(jnp.maximum(jnp.max(s_c, axis=-1, keepdims=True),
                                jnp.max(s_n, axis=-1, keepdims=True)), sink)
    e_c = jnp.exp(s_c - m)
    e_n = jnp.exp(s_n - m)
    denom = (jnp.sum(e_c, axis=-1, keepdims=True) + jnp.sum(e_n, axis=-1, keepdims=True)
             + jnp.exp(sink - m))
    o = jnp.einsum('bqk,bdk->bqd', e_c.astype(BF16).reshape(bt, N_STACK * T, WINDOW),
                   cv.astype(BF16), preferred_element_type=F32).reshape(R, LANES)
    o = (o + _dot(e_n.astype(BF16), v_new.astype(BF16))) / denom
    o3 = o.reshape(bt, N_STACK * T, LANES)
    for col in range(ATT_GROUP):
        lo_h = o3[:, (2 * col) * T:(2 * col + 1) * T, :]
        hi_h = o3[:, (2 * col + 1) * T:(2 * col + 2) * T, :]
        mix_scr[:, col * LANES:(col + 1) * LANES] = jnp.where(
            low3, lo_h, hi_h).reshape(L, LANES).astype(BF16)

    keep = lax.broadcasted_iota(jnp.int32, (ATT_KV_W, WINDOW), 1) < WINDOW - T
    k_new_t = k_new.T
    v_new_t = v_new.T
    def roll_caches():
        for q in range(bt):
            shift = (WINDOW - T - q * T) % WINDOW
            nk_ref[q] = jnp.where(keep, pltpu.roll(ck_ref[q], WINDOW - T, axis=1),
                                  pltpu.roll(k_new_t, shift, axis=1))
            nv_ref[q] = jnp.where(keep, pltpu.roll(cv_ref[q], WINDOW - T, axis=1),
                                  pltpu.roll(v_new_t, shift, axis=1))
            if (q + 1) % (bt // ML_HEADS) == 0:
                yield

    rolls = roll_caches()

    r = lax.broadcasted_iota(jnp.int32, (L, L), 0)
    c = lax.broadcasted_iota(jnp.int32, (L, L), 1)
    seg = (r // T == c // T) & (r <= c)
    seg_bias = jnp.where(seg, 0.0, -jnp.inf)
    seg_bf = seg.astype(F32).astype(BF16)
    gates = gt_ref[...] * LOG2_E
    cum_row = jnp.zeros(gates.shape, F32)
    for part in _split3(gates):
        cum_row = cum_row + _dot(part, seg_bf)
    ig_rows = gates[:ML_HEADS]
    b_rows = cum_row[ML_HEADS:]
    gate_cols = jnp.concatenate([ig_rows, b_rows, jnp.zeros((LANES - N_GATES, L), F32)],
                                axis=0).T

    def col_to_row(x_col):
        return jnp.broadcast_to(x_col, (L, LANES)).T[0:1, :]

    ones_rows = (r[:LANES] == 0).astype(F32).astype(BF16)
    qm = qm_ref[...]
    km = km_ref[...]
    qm_f = qm.astype(F32)
    km_f = km.astype(F32)
    n_rep = bt * ML_QK_DIM // LANES
    bd_row = lax.broadcasted_iota(jnp.int32, (L, bt * ML_QK_DIM), 0) // T
    bd_lane = lax.broadcasted_iota(jnp.int32, (L, bt * ML_QK_DIM), 1) // ML_QK_DIM
    block_diag = bd_row == bd_lane

    def spread(x_pair, e):
        other = pltpu.roll(x_pair, ML_QK_DIM, axis=1)
        twice = jnp.where(low, x_pair, other) if e == 0 else jnp.where(low, other, x_pair)
        return jnp.where(block_diag, jnp.concatenate([twice] * n_rep, axis=1), 0.0).astype(BF16)

    def head_stages(h):
        p, e = divmod(h, 2)
        qc = qm[:, p * LANES:(p + 1) * LANES]
        k_pair = km[:, p * LANES:(p + 1) * LANES]
        zero = jnp.zeros_like(qc)
        q_pad = jnp.where(low, qc, zero) if e == 0 else jnp.where(low, zero, qc)
        v_h = vm_ref[:, h * ML_V_DIM:(h + 1) * ML_V_DIM]
        v_ext_t = jnp.concatenate([v_h.astype(F32).T.astype(BF16), ones_rows], axis=0)
        ig_c = gate_cols[:, h:h + 1]
        b_c = gate_cols[:, ML_HEADS + h:ML_HEADS + h + 1]
        b_r = b_rows[h:h + 1, :]
        yield
        m0 = m0_ref[:, :, h:h + 1] * LOG2_E
        inter = b_r + col_to_row(jnp.broadcast_to(m0, (bt, T, 1)).reshape(L, 1))
        dm = (b_r + (ig_c - b_c)) + seg_bias
        m_row = jnp.maximum(inter, jnp.max(dm, axis=0, keepdims=True))
        w_inter = jnp.exp2(inter - m_row)
        yield
        p_t = (_dot_nt(k_pair, q_pad) * jnp.exp2(dm - m_row)).astype(BF16)
        num_t = _dot(v_ext_t, p_t)
        yield
        q_h3 = qm_f[:, h * ML_QK_DIM:(h + 1) * ML_QK_DIM].reshape(bt, T, ML_QK_DIM)
        k_h3 = km_f[:, h * ML_QK_DIM:(h + 1) * ML_QK_DIM].reshape(bt, T, ML_QK_DIM)
        c0 = c0_ref[:, h]
        n0 = n0_ref[:, h:h + 1, :]
        q_c_t = _dot(spread(qm_f[:, p * LANES:(p + 1) * LANES], e),
                     c0.astype(BF16).reshape(bt * ML_QK_DIM, ML_V_DIM)).T
        q_n_r = col_to_row(jnp.sum(q_h3 * n0, axis=-1, keepdims=True).reshape(L, 1))
        yield
        num =num_t[:ML_V_DIM] + w_inter * q_c_t
        den = num_t[ML_V_DIM:ML_V_DIM + 1] + w_inter * q_n_r
        hh = num * (1.0 / jnp.maximum(jnp.abs(den), jnp.exp2(-m_row)))
        ms = jnp.mean(hh * hh, axis=0, keepdims=True)
        yield
        mix_scr[:, ATT_Q_W + h * ML_V_DIM:ATT_Q_W + (h + 1) * ML_V_DIM] = (
            (hh * lax.rsqrt(ms + EPS)).T * mlnw_ref[:, h * ML_V_DIM:(h + 1) * ML_V_DIM]
            * _sigmoid(om_ref[:, h * ML_V_DIM:(h + 1) * ML_V_DIM].astype(F32))).astype(BF16)
        yield
        b3 = b_c.reshape(bt, T, 1)
        b_last = b3[:, T - 1:T, :]
        a3 = b_last - b3 + ig_c.reshape(bt, T, 1)
        m_new = jnp.maximum(b_last + m0, jnp.max(a3, axis=1, keepdims=True))
        sc = jnp.exp2(b_last + m0 - m_new)
        ws = jnp.exp2(a3 - m_new)
        yield
        kw = spread(km_f[:, p * LANES:(p + 1) * LANES] * ws.reshape(L, 1), e)
        d_c = lax.dot_general(kw, v_h, (((0,), (0,)), ((), ())), preferred_element_type=F32)
        c_ref[:, h] = sc * c0 + d_c.reshape(bt, ML_QK_DIM, ML_V_DIM)
        n_ref[:, h:h + 1, :] = sc * n0 + jnp.sum(ws * k_h3, axis=1, keepdims=True)
        m_ref[:, :, h:h + 1] = m_new * (1.0 / LOG2_E)
        yield

    for _ in zip(*[head_stages(h) for h in range(ML_HEADS)]):
        next(rolls, None)
    for _ in rolls:
        pass

    x1_ref[...] = x_ref[...] + _dot(mix_scr[...], wperm_scr[...])


def _sample_mixer(nb, t_len, sinks, qa, kv, ck, cv, qm, km, vm, om, gt, c0, n0, m0, x2d, wout, mlnw):
    bt = SAMPLE_BT
    tl = bt * t_len
    row = lambda w: pl.BlockSpec((tl, w), lambda i: (i, 0))
    full = lambda a: pl.BlockSpec(a.shape, lambda i: (0,) * a.ndim)
    once = lambda a: pl.BlockSpec(a.shape, lambda i: (0,) * a.ndim, pipeline_mode=pl.Buffered(1))
    cache = pl.BlockSpec((bt, ATT_KV_W, WINDOW), lambda i: (i, 0, 0))
    c_spec = pl.BlockSpec((bt, ML_HEADS, ML_QK_DIM, ML_V_DIM), lambda i: (i, 0, 0, 0))
    n_spec = pl.BlockSpec((bt, ML_HEADS, ML_QK_DIM), lambda i: (i, 0, 0))
    m_spec = pl.BlockSpec((bt, 1, ML_HEADS), lambda i: (i, 0, 0))
    return pl.pallas_call(
        functools.partial(_sample_mixer_kernel, t_len),
        grid=(nb // bt,),
        in_specs=[pl.BlockSpec(memory_space=pltpu.SMEM),
                  row(ATT_Q_W), row(2 * ATT_KV_W), cache, cache, row(ML_QK_W), row(ML_QK_W),
                  row(ML_V_W), row(ML_V_W), pl.BlockSpec((N_GATES, tl), lambda i: (0, i)),
                  c_spec, n_spec, m_spec, row(D_MODEL), once(wout), full(mlnw)],
        out_specs=[row(D_MODEL), cache, cache, c_spec, n_spec, m_spec],
        out_shape=[jax.ShapeDtypeStruct((nb * t_len, D_MODEL), F32),
                   jax.ShapeDtypeStruct((nb, ATT_KV_W, WINDOW), F32),
                   jax.ShapeDtypeStruct((nb, ATT_KV_W, WINDOW), F32),
                   jax.ShapeDtypeStruct((nb, ML_HEADS, ML_QK_DIM, ML_V_DIM), F32),
                   jax.ShapeDtypeStruct((nb, ML_HEADS, ML_QK_DIM), F32),
                   jax.ShapeDtypeStruct((nb, 1, ML_HEADS), F32)],
        scratch_shapes=[pltpu.VMEM((tl, D_MODEL), BF16),
                        pltpu.VMEM((D_MODEL, D_MODEL), BF16)],
        compiler_params=pltpu.CompilerParams(dimension_semantics=("arbitrary",),
                                             vmem_limit_bytes=VMEM_LIMIT),
        name="sample_mixer",
    )(sinks, qa, kv, ck, cv, qm, km, vm, om, gt, c0, n0, m0, x2d, wout, mlnw)


def _ffn_kernel(seq_rows, *refs):
    if seq_rows is None:
        (x_ref, nw_ref, w_ref, cw_ref, cb_ref, wd_ref, y_ref, conv_ref,
         gbuf, act_scr, carry) = refs
        hist_ref = None
    else:
        (x_ref, hist_ref, nw_ref, w_ref, cw_ref, cb_ref, wd_ref, y_ref, conv_ref,
         gbuf, act_scr) = refs
        carry = None
    tm = x_ref.shape[0]
    tf = FF_CHUNK
    n_hist = CONV_W - 1
    rows = tm if seq_rows is None else seq_rows
    nseq = tm // rows
    base = SUBLANES
    n_chunks = D_FF // tf

    if carry is not None:
        @pl.when(pl.program_id(1) == 0)
        def _():
            carry[...] = jnp.zeros(carry.shape, F32)

    x = x_ref[...]
    h2 = _rms(x, nw_ref[...]).astype(BF16)

    def proj(f):
        return (_dot(h2, w_ref[:, f * tf:(f + 1) * tf]),
                _dot(h2, w_ref[:, D_FF + f * tf:D_FF + (f + 1) * tf]))

    nxt = proj(0)
    for f in range(n_chunks):
        g, u = nxt
        if f + 1 < n_chunks:
            nxt = proj(f + 1)
        cols = slice(f * tf, (f + 1) * tf)
        s = f % 2
        g3 = g.reshape(nseq, rows, tf)
        if seq_rows is None:
            gbuf[s, :, base - n_hist:base, :] = carry[:, SUBLANES - n_hist:, cols]
            carry[:, SUBLANES - n_hist:, cols] = g3[:, rows - n_hist:, :]
        else:
            gbuf[s, :, base - n_hist:base, :] = hist_ref[:, :, cols]
            conv_ref[:, :, cols] = g3[:, rows - n_hist:, :]
        gbuf[s, :, base:base + rows, :] = g3
        gc = cb_ref[:, cols] + g * cw_ref[CONV_W - 1:CONV_W, cols]
        for d in range(1, CONV_W):
            gm = gbuf[s, :, base - d:base - d + rows, :].reshape(tm, tf)
            gc = gc + gm * cw_ref[CONV_W - 1 - d:CONV_W - d, cols]
        act_scr[:, cols] = (gc * _sigmoid(gc) * u).astype(BF16)
    y_ref[...] = x + _dot(act_scr[...], wd_ref[...])

    if carry is not None:
        @pl.when(pl.program_id(1) == pl.num_programs(1) - 1)
        def _():
            conv_ref[...] = carry[:, SUBLANES - n_hist:, :]


def _ffn_scratch(tm, rows):
    return [pltpu.VMEM((2, tm // rows, SUBLANES + rows, FF_CHUNK), F32),
            pltpu.VMEM((tm, D_FF), BF16)]


def _ffn_prompt(batch, seq, x2d, nw, w, cw, cb, wd):
    tm = FFN_TILE
    nt = seq // tm
    full = lambda a: pl.BlockSpec(a.shape, lambda b, i: (0,) * a.ndim)
    once = lambda a: pl.BlockSpec(a.shape, lambda b, i: (0,) * a.ndim,
                                  pipeline_mode=pl.Buffered(1))
    row = pl.BlockSpec((tm, D_MODEL), lambda b, i: (b * nt + i, 0))
    return pl.pallas_call(
        functools.partial(_ffn_kernel, None),
        grid=(batch, nt),
        in_specs=[row, full(nw), once(w), full(cw), full(cb), once(wd)],
        out_specs=[row, pl.BlockSpec((1, CONV_W - 1, D_FF), lambda b, i: (b, 0, 0))],
        out_shape=[jax.ShapeDtypeStruct((batch * seq, D_MODEL), F32),
                   jax.ShapeDtypeStruct((batch, CONV_W - 1, D_FF), F32)],
        scratch_shapes=_ffn_scratch(tm, tm) + [pltpu.VMEM((1, SUBLANES, D_FF), F32)],
        compiler_params=pltpu.CompilerParams(dimension_semantics=("arbitrary", "arbitrary"),
                                             vmem_limit_bytes=VMEM_LIMIT),
        name="ffn_prompt",
    )(x2d, nw, w, cw, cb, wd)


def _ffn_sample(nb, t_len, x2d, hist, nw, w, cw, cb, wd):
    tm = ROW_TILE
    bt = tm // t_len
    full = lambda a: pl.BlockSpec(a.shape, lambda i: (0,) * a.ndim)
    once = lambda a: pl.BlockSpec(a.shape, lambda i: (0,) * a.ndim, pipeline_mode=pl.Buffered(1))
    row = pl.BlockSpec((tm, D_MODEL), lambda i: (i, 0))
    hist_spec = pl.BlockSpec((bt, CONV_W - 1, D_FF), lambda i: (i, 0, 0))
    return pl.pallas_call(
        functools.partial(_ffn_kernel, t_len),
        grid=(nb // bt,),
        in_specs=[row, hist_spec, full(nw), once(w), full(cw), full(cb), once(wd)],
        out_specs=[row, hist_spec],
        out_shape=[jax.ShapeDtypeStruct((nb * t_len, D_MODEL), F32),
                   jax.ShapeDtypeStruct((nb, CONV_W - 1, D_FF), F32)],
        scratch_shapes=_ffn_scratch(tm, t_len),
        compiler_params=pltpu.CompilerParams(dimension_semantics=("arbitrary",),
                                             vmem_limit_bytes=VMEM_LIMIT),
        name="ffn_sample",
    )(x2d, hist, nw, w, cw, cb, wd)


def _head_mean_matrix(width, head_dim):
    idx = np.arange(width) // head_dim
    return jnp.asarray((idx[:, None] == idx[None, :]).astype(np.float32) / head_dim, dtype=BF16)


def _layer_weights(norm_mix_w, w_in, b_gates, q_norm_w, k_norm_w, sinks, ml_norm_w, w_out,
                   norm_ffn_w, w_ffn_in, conv_w, conv_b, w_down):
    w_in_t = jnp.pad(w_in.T.astype(BF16), ((0, IN_WIDTH_PAD - w_in.shape[1]), (0, 0)))
    return dict(
        nw=norm_mix_w.reshape(1, D_MODEL),
        w_in_t=w_in_t,
        bg=jnp.pad(b_gates, (0, LANES - N_GATES)).reshape(1, LANES),
        qnw=(jnp.tile(q_norm_w, ATT_HEADS) * ATT_SCALE).reshape(1, ATT_Q_W),
        knw=jnp.tile(k_norm_w, ATT_KV_HEADS).reshape(1, ATT_KV_W),
        gq=_head_mean_matrix(ATT_Q_W, ATT_HEAD_DIM),
        gk=_head_mean_matrix(ATT_KV_W, ATT_HEAD_DIM),
        bg_col=b_gates.reshape(N_GATES, 1),
        qnw_col=(jnp.tile(q_norm_w, ATT_HEADS) * (ATT_SCALE * LOG2_E)).reshape(ATT_Q_W, 1),
        knw_col=jnp.tile(k_norm_w, ATT_KV_HEADS).reshape(ATT_KV_W, 1),
        mlnw_col=ml_norm_w.reshape(ML_V_W, 1),
        sinks=sinks,
        mlnw=ml_norm_w.reshape(1, ML_V_W),
        wout=w_out.astype(BF16),
        nfw=norm_ffn_w.reshape(1, D_MODEL),
        wff=w_ffn_in.astype(BF16),
        cw=conv_w,
        cb=conv_b.reshape(1, D_FF),
        wd=w_down.astype(BF16),
    )


def _cache_from_t(a_t):
    n = a_t.shape[0]
    return jnp.transpose(a_t.reshape(n, ATT_KV_HEADS, ATT_HEAD_DIM, WINDOW), (0, 3, 1, 2))


def _cache_to_t(a):
    n = a.shape[0]
    return jnp.transpose(a, (0, 2, 3, 1)).reshape(n, ATT_KV_W, WINDOW)


def _prompt_layer(x, w):
    batch, seq, _ = x.shape
    assert seq % ROW_TILE == 0 and MIX_TILE == ROW_TILE and MIX_TILE % WINDOW == 0
    assert seq % FFN_TILE == 0
    x2d = x.reshape(batch * seq, D_MODEL)
    qa, ks, kv, qm, km, vm, om, gt = _inproj_t(x2d, w["nw"], w["w_in_t"], w["bg_col"],
                                               w["qnw_col"], w["knw_col"])
    x1, c_t, n_row, m, k_t, v_t = _prompt_mixer_t(batch, seq, w["sinks"], qa, ks, kv, qm, km, vm,
                                                  om, gt, x2d, w["wout"], w["mlnw_col"])
    y, conv = _ffn_prompt(batch, seq, x1, w["nfw"], w["wff"], w["cw"], w["cb"], w["wd"])
    return (y.reshape(batch, seq, D_MODEL), _cache_from_t(k_t), _cache_from_t(v_t),
            jnp.swapaxes(c_t, -1, -2), n_row.reshape(batch, ML_HEADS, ML_QK_DIM),
            m.reshape(batch, ML_HEADS), conv)


def _sample_layer(x, ck, cv, c0, n0, m0, conv_buf, w):
    nb, t_len, _ = x.shape
    assert t_len == SUBLANES and SAMPLE_BT * t_len == LANES and nb % SAMPLE_BT == 0
    assert (nb * t_len) % ROW_TILE == 0
    x2d = x.reshape(nb * t_len, D_MODEL)
    qa, kv, qm, km, vm, om, gt = _inproj(x2d, w["nw"], w["w_in_t"], w["bg"], w["qnw"], w["knw"],
                                         w["gq"], w["gk"])
    x1, nk_t, nv_t, c_t, n, m = _sample_mixer(
        nb, t_len, w["sinks"], qa, kv, _cache_to_t(ck), _cache_to_t(cv), qm, km, vm, om, gt,
        jnp.swapaxes(c0, -1, -2), n0, m0.reshape(nb, 1, ML_HEADS), x2d, w["wout"], w["mlnw"])
    y, conv = _ffn_sample(nb, t_len, x1, conv_buf, w["nfw"], w["wff"], w["cw"], w["cb"],
                          w["wd"])
    return (y.reshape(nb, t_len, D_MODEL), _cache_from_t(nk_t), _cache_from_t(nv_t),
            jnp.swapaxes(c_t, -1, -2), n, m.reshape(nb, ML_HEADS), conv)


def kernel(x_prompt, x_sample, cache_attn_k, cache_attn_v, state_mlstm_C, state_mlstm_n,
           state_mlstm_m, cache_ffn_conv, norm_mix_w, w_in, b_gates, q_norm_w, k_norm_w,
           sinks, ml_norm_w, w_out, norm_ffn_w, w_ffn_in, conv_w, conv_b, w_down):
    depth = w_in.shape[0]
    yp, ys = x_prompt, x_sample
    sp = [[] for _ in range(6)]
    ss = [[] for _ in range(6)]
    for l in range(depth):
        w = _layer_weights(norm_mix_w[l], w_in[l], b_gates[l], q_norm_w[l], k_norm_w[l], sinks[l],
                           ml_norm_w[l], w_out[l], norm_ffn_w[l], w_ffn_in[l], conv_w[l],
                           conv_b[l], w_down[l])
        yp, *st_p = _prompt_layer(yp, w)
        ys, *st_s = _sample_layer(ys, cache_attn_k[l], cache_attn_v[l], state_mlstm_C[l],
                                  state_mlstm_n[l], state_mlstm_m[l], cache_ffn_conv[l], w)
        for i in range(6):
            sp[i].append(st_p[i])
            ss[i].append(st_s[i])
    k_p, v_p, c_p, n_p, m_p, conv_p = [jnp.stack(a) for a in sp]
    k_s, v_s, c_s, n_s, m_s, conv_s = [jnp.stack(a) for a in ss]
    return (yp, ys, k_p, v_p, c_p, n_p, m_p, conv_p, k_s, v_s, c_s, n_s, m_s, conv_s)
```

```python
import functools

import numpy as np
import jax
import jax.numpy as jnp
from jax import lax
from jax.experimental import pallas as pl
from jax.experimental.pallas import tpu as pltpu

F32 = jnp.float32
BF16 = jnp.bfloat16

D_MODEL = 1024
ATT_HEADS = 8
ATT_KV_HEADS = 2
ATT_HEAD_DIM = 64
ATT_GROUP = ATT_HEADS // ATT_KV_HEADS
WINDOW = 128
ML_HEADS = 4
ML_V_DIM = 128
ML_QK_DIM = 64
D_FF = 2816
CONV_W = 3
EPS = 1e-6
ATT_SCALE = ATT_HEAD_DIM ** -0.5
ML_SCALE = ML_QK_DIM ** -0.5
LOG2_E = 1.4426950408889634

ATT_Q_W = ATT_HEADS * ATT_HEAD_DIM
ATT_KV_W = ATT_KV_HEADS * ATT_HEAD_DIM
ML_QK_W = ML_HEADS * ML_QK_DIM
ML_V_W = ML_HEADS * ML_V_DIM
N_GATES = 2 * ML_HEADS
N_STACK = 2 * ATT_GROUP

LANES = 128
SUBLANES = 8

OFF_QA = 0
OFF_KV = OFF_QA + ATT_Q_W
OFF_QM = OFF_KV + 2 * ATT_KV_W
OFF_KM = OFF_QM + ML_QK_W
OFF_VM = OFF_KM + ML_QK_W
OFF_OM = OFF_VM + ML_V_W
OFF_GL = OFF_OM + ML_V_W
IN_WIDTH_PAD = OFF_GL + LANES

ATT_HEAD_ORDER = tuple(h for c in range(ATT_GROUP) for h in (c, c + ATT_GROUP))

ROW_TILE = 512
FFN_TILE = 1024
INPROJ_SUB = 256
NORM_ROWS = 64
MIX_TILE = 256
ML_CHUNK = 256
OUT_COLS = 256
FF_CHUNK = 256
SAMPLE_BT = 16
VMEM_LIMIT = 56 * 1024 * 1024


def _dot(a, b):
    return jnp.dot(a, b, preferred_element_type=F32)


def _dot_nt(a, b):
    return lax.dot_general(a, b, (((1,), (1,)), ((), ())), preferred_element_type=F32)


def _split3(x):
    hi = x.astype(BF16)
    r1 = x - hi.astype(F32)
    mid = r1.astype(BF16)
    lo = (r1 - mid.astype(F32)).astype(BF16)
    return hi, mid, lo


def _rms(x, w):
    ms = jnp.mean(x * x, axis=-1, keepdims=True)
    return x * lax.rsqrt(ms + EPS) * w


def _log_sigmoid(x):
    return jnp.minimum(x, 0.0) - jnp.log1p(jnp.exp(-jnp.abs(x)))


def _sigmoid(x):
    return 1.0 / (1.0 + jnp.exp(-x))


def _permute_head_rows(dst_ref, src_ref):
    for k, h in enumerate(ATT_HEAD_ORDER):
        dst_ref[k * ATT_HEAD_DIM:(k + 1) * ATT_HEAD_DIM, :] = (
            src_ref[h * ATT_HEAD_DIM:(h + 1) * ATT_HEAD_DIM, :])


def _inproj_kernel(x_ref, nw_ref, w_ref, bg_ref, qnw_ref, knw_ref, gq_ref, gk_ref,
                   qa_ref, kv_ref, qm_ref, km_ref, vm_ref, om_ref, gt_ref, wq_scr):
    @pl.when(pl.program_id(0) == 0)
    def _():
        _permute_head_rows(wq_scr, w_ref)

    h = _rms(x_ref[...], nw_ref[...]).astype(BF16)

    def proj(lo, width):
        return _dot_nt(h, w_ref[lo:lo + width, :])

    q = _dot_nt(h, wq_scr[...])
    q_ms = _dot((q * q).astype(BF16), gq_ref[...])
    qa_ref[...] = (q * lax.rsqrt(q_ms + EPS) * qnw_ref[...]).astype(BF16)

    kv = proj(OFF_KV, 2 * ATT_KV_W)
    k = kv[:, :ATT_KV_W]
    k_ms = _dot((k * k).astype(BF16), gk_ref[...])
    kv_ref[:, :ATT_KV_W] = k * lax.rsqrt(k_ms + EPS) * knw_ref[...]
    kv_ref[:, ATT_KV_W:] = kv[:, ATT_KV_W:]

    qm_ref[...] = (proj(OFF_QM, ML_QK_W) * ML_SCALE).astype(BF16)
    km_ref[...] = proj(OFF_KM, ML_QK_W).astype(BF16)
    vm_ref[...] = proj(OFF_VM, ML_V_W).astype(BF16)
    om_ref[...] = proj(OFF_OM, ML_V_W).astype(BF16)

    gl = proj(OFF_GL, LANES) + bg_ref[...]
    lane = lax.broadcasted_iota(jnp.int32, gl.shape, 1)
    g = jnp.where(lane < ML_HEADS, gl, _log_sigmoid(gl))
    gt_ref[...] = g.T[:N_GATES, :]


def _inproj(x2d, nw, w_in_t, bg, qnw, knw, gq, gk):
    n = x2d.shape[0]
    tm = ROW_TILE
    row = lambda w: pl.BlockSpec((tm, w), lambda i: (i, 0))
    full = lambda a: pl.BlockSpec(a.shape, lambda i: (0,) * a.ndim)
    once = lambda a: pl.BlockSpec(a.shape, lambda i: (0,) * a.ndim, pipeline_mode=pl.Buffered(1))
    return pl.pallas_call(
        _inproj_kernel,
        grid=(n // tm,),
        in_specs=[row(D_MODEL), full(nw), once(w_in_t), full(bg), full(qnw), full(knw),
                  full(gq), full(gk)],
        out_specs=[row(ATT_Q_W), row(2 * ATT_KV_W), row(ML_QK_W), row(ML_QK_W),
                   row(ML_V_W), row(ML_V_W), pl.BlockSpec((N_GATES, tm), lambda i: (0, i))],
        out_shape=[jax.ShapeDtypeStruct((n, ATT_Q_W), BF16),
                   jax.ShapeDtypeStruct((n, 2 * ATT_KV_W), F32),
                   jax.ShapeDtypeStruct((n, ML_QK_W), BF16),
                   jax.ShapeDtypeStruct((n, ML_QK_W), BF16),
                   jax.ShapeDtypeStruct((n, ML_V_W), BF16),
                   jax.ShapeDtypeStruct((n, ML_V_W), BF16),
                   jax.ShapeDtypeStruct((N_GATES, n), F32)],
        scratch_shapes=[pltpu.VMEM((ATT_Q_W, D_MODEL), BF16)],
        compiler_params=pltpu.CompilerParams(dimension_semantics=("arbitrary",),
                                             vmem_limit_bytes=VMEM_LIMIT),
        name="inproj",
    )(x2d, nw, w_in_t, bg, qnw, knw, gq, gk)


def _head_norm_t(z, head_dim, w_col):
    rows, tokens = z.shape
    z3 = z.reshape(rows // head_dim, head_dim, tokens)
    ms = jnp.mean(z3 * z3, axis=1, keepdims=True)
    return (z3 * lax.rsqrt(ms + EPS)).reshape(rows, tokens) * w_col


def _inproj_t_kernel(x_ref, nw_ref, w_ref, bg_ref, qnw_ref, knw_ref,
                     qa_ref, ks_ref, kv_ref, qm_ref, km_ref, vm_ref, om_ref, gt_ref, h_scr):
    tm = x_ref.shape[0]
    sub = INPROJ_SUB

    def norm_rows(c):
        for r0 in range(c * sub, (c + 1) * sub, NORM_ROWS):
            rows = slice(r0, r0 + NORM_ROWS)
            h_scr[rows, :] = _rms(x_ref[rows, :], nw_ref[...]).astype(BF16)
            yield

    def project(c):
        tok = slice(c * sub, (c + 1) * sub)
        h = h_scr[tok, :]

        def proj(lo, width):
            return _dot_nt(w_ref[lo:lo + width, :], h)

        qa_ref[:, tok] = _head_norm_t(proj(OFF_QA, ATT_Q_W), ATT_HEAD_DIM,
                                      qnw_ref[...]).astype(BF16)
        yield
        kv = proj(OFF_KV, 2 * ATT_KV_W)
        k = _head_norm_t(kv[:ATT_KV_W], ATT_HEAD_DIM, knw_ref[...])
        kv_ref[:ATT_KV_W, tok] = k
        kv_ref[ATT_KV_W:, tok] = kv[ATT_KV_W:]
        ks_ref[tok, :] = k.T.astype(BF16)
        qm_ref[:, tok] = (proj(OFF_QM, ML_QK_W) * ML_SCALE).astype(BF16)
        yield
        km_ref[:, tok] = proj(OFF_KM, ML_QK_W).astype(BF16)
        vm_ref[:, tok] = proj(OFF_VM, ML_V_W).astype(BF16)
        yield
        om_ref[:, tok] = proj(OFF_OM, ML_V_W).astype(BF16)
        gl = proj(OFF_GL, 2 * SUBLANES)[:N_GATES] + bg_ref[...]
        row = lax.broadcasted_iota(jnp.int32, gl.shape, 0)
        gt_ref[:, tok] = jnp.where(row < ML_HEADS, gl, _log_sigmoid(gl))
        yield

    for _ in norm_rows(0):
        pass
    for c in range(tm // sub):
        norms = norm_rows(c + 1) if c + 1 < tm // sub else iter(())
        for _ in project(c):
            next(norms, None)
        for _ in norms:
            pass


def _inproj_t(x2d, nw, w_in_t, bg_col, qnw_col, knw_col):
    n = x2d.shape[0]
    tm = ROW_TILE
    full = lambda a: pl.BlockSpec(a.shape, lambda i: (0,) * a.ndim)
    once = lambda a: pl.BlockSpec(a.shape, lambda i: (0,) * a.ndim, pipeline_mode=pl.Buffered(1))
    col = lambda w: pl.BlockSpec((None, w, tm), lambda i: (i, 0, 0))
    slab = lambda w, dt: jax.ShapeDtypeStruct((n // tm, w, tm), dt)
    return pl.pallas_call(
        _inproj_t_kernel,
        grid=(n // tm,),
        in_specs=[pl.BlockSpec((tm, D_MODEL), lambda i: (i, 0)), full(nw), once(w_in_t),
                  full(bg_col), full(qnw_col), full(knw_col)],
        out_specs=[col(ATT_Q_W), pl.BlockSpec((tm, ATT_KV_W), lambda i: (i, 0)),
                   col(2 * ATT_KV_W), col(ML_QK_W), col(ML_QK_W), col(ML_V_W), col(ML_V_W),
                   col(N_GATES)],
        out_shape=[slab(ATT_Q_W, BF16),
                   jax.ShapeDtypeStruct((n, ATT_KV_W), BF16),
                   slab(2 * ATT_KV_W, F32), slab(ML_QK_W, BF16), slab(ML_QK_W, BF16),
                   slab(ML_V_W, BF16), slab(ML_V_W, BF16), slab(N_GATES, F32)],
        scratch_shapes=[pltpu.VMEM((tm, D_MODEL), BF16)],
        compiler_params=pltpu.CompilerParams(dimension_semantics=("arbitrary",),
                                             vmem_limit_bytes=VMEM_LIMIT),
        name="inproj_t",
    )(x2d, nw, w_in_t, bg_col, qnw_col, knw_col)


def _gate_forms(gates, seg_mask, want_raw_col):
    L = gates.shape[1]
    m_bf = seg_mask.astype(F32).astype(BF16)
    cum_row = jnp.zeros(gates.shape, F32)
    cum_col = jnp.zeros((L, gates.shape[0]), F32)
    raw_col = None
    if want_raw_col:
        r = lax.broadcasted_iota(jnp.int32, (L, L), 0)
        c = lax.broadcasted_iota(jnp.int32, (L, L), 1)
        eye = (r == c).astype(F32).astype(BF16)
        raw_col = jnp.zeros((L, gates.shape[0]), F32)
    for part in _split3(gates):
        cum_row = cum_row + _dot_nt(part, m_bf)
        cum_col = cum_col + _dot_nt(m_bf, part)
        if want_raw_col:
            raw_col = raw_col + _dot_nt(eye, part)
    return cum_row, cum_col, raw_col


def _mlstm_intra(q_pad, k_pair, v_ext, seg_mask, b_c, b_r, ig_r, m_prev_c):
    dm = jnp.where(seg_mask, b_c + (ig_r - b_r), -jnp.inf)
    inter = b_c + m_prev_c
    m_row = jnp.maximum(inter, jnp.max(dm, axis=-1, keepdims=True))
    w_inter = jnp.exp(inter - m_row)
    p = _dot_nt(q_pad, k_pair) * jnp.exp(dm - m_row)
    return _dot(p.astype(BF16), v_ext), m_row, w_inter


def _mlstm_out(pv, m_row, w_inter, q_c, q_n, mlnw_h, om_h):
    num = pv[:, :ML_V_DIM] + w_inter * q_c
    den = pv[:, ML_V_DIM:ML_V_DIM + 1] + w_inter * q_n
    hh = num / jnp.maximum(jnp.abs(den), jnp.exp(-m_row))
    return (_rms(hh, mlnw_h) * _sigmoid(om_h.astype(F32))).astype(BF16)


def _ones_col(rows):
    lane = lax.broadcasted_iota(jnp.int32, (rows, LANES), 1)
    return (lane == 0).astype(F32).astype(BF16)


def _prompt_mixer_t_kernel(sinks_ref, qa_ref, ksc_ref, ksp_ref, kvc_ref, kvp_ref, qm_ref, km_ref,
                           vm_ref, om_ref, gt_ref, x_ref, wout_ref, mlnw_ref,
                           x1_ref, ct_ref, nrow_ref, m_ref, kt_ref, vt_ref,
                           mix_scr, mixt_scr, state_scr, m_scr, band_scr, causal_scr, tri_scr,
                           s_scr_a, s_scr_b, e_scr):
    i = pl.program_id(1)
    n_chunks = pl.num_programs(1) - 1
    A = WINDOW
    L = MIX_TILE
    C = ML_CHUNK
    n_pairs = ML_HEADS // 2

    @pl.when(i == 0)
    def _():
        state_scr[...] = jnp.zeros(state_scr.shape, F32)
        m_scr[...] = jnp.zeros(m_scr.shape, F32)
        mix_scr[...] = jnp.zeros(mix_scr.shape, BF16)
        kj = lax.broadcasted_iota(jnp.int32, (2 * A, A), 0)
        qi = lax.broadcasted_iota(jnp.int32, (2 * A, A), 1)
        band = (kj > qi) & (kj <= qi + WINDOW)
        band_scr[0] = jnp.where(band, 0.0, -jnp.inf)
        band_scr[1] = jnp.where(band & (kj >= A), 0.0, -jnp.inf)
        r = lax.broadcasted_iota(jnp.int32, (C, C), 0)
        c = lax.broadcasted_iota(jnp.int32, (C, C), 1)
        causal_scr[...] = jnp.where(r <= c, 0.0, -jnp.inf)
        tri_scr[...] = (r <= c).astype(F32).astype(BF16)

    k_all = jnp.concatenate([ksp_ref[...], ksc_ref[...]], axis=0)
    v_all = jnp.concatenate([kvp_ref[ATT_KV_W:, :], kvc_ref[ATT_KV_W:, :]], axis=1).astype(BF16)
    zero_q = jnp.zeros((ATT_HEAD_DIM, A), BF16)
    slot = 0
    s_bufs = (s_scr_a, s_scr_b)

    def stage_scores(j):
        pieces = []
        for h in range(ATT_HEADS):
            q_h = qa_ref[h * ATT_HEAD_DIM:(h + 1) * ATT_HEAD_DIM, j * A:(j + 1) * A]
            pieces.append(jnp.concatenate([q_h, zero_q] if h < ATT_GROUP else [zero_q, q_h],
                                          axis=0))
        s_bufs[j % 2][slot] = _dot(k_all[j * A:(j + 2) * A, :], jnp.concatenate(pieces, axis=1))

    def attend(j):
        cols = slice(j * A, (j + 1) * A)
        vt = v_all[:, j * A:(j + 2) * A]
        if j + 1 < L // A:
            stage_scores(j + 1)
        s_buf = s_bufs[j % 2]
        bias = jnp.where(i > 0, band_scr[0], band_scr[1]) if j == 0 else band_scr[0]
        m_rows = []
        for h in range(ATT_HEADS):
            sb = s_buf[slot, :, h * A:(h + 1) * A] + bias
            m_rows.append(jnp.maximum(jnp.max(sb, axis=0, keepdims=True),
                                      sinks_ref[h] * LOG2_E))
        inv_rows = []
        for h in range(ATT_HEADS):
            e = jnp.exp2(s_buf[slot, :, h * A:(h + 1) * A] + (bias - m_rows[h]))
            e_scr[:, h * A:(h + 1) * A] = e.astype(BF16)
            inv_rows.append(1.0 / (jnp.sum(e, axis=0, keepdims=True)
                                   + jnp.exp2(sinks_ref[h] * LOG2_E - m_rows[h])))
        o = _dot(vt, e_scr[...])
        for h in range(ATT_HEADS):
            g = h // ATT_GROUP
            mix_scr[h * ATT_HEAD_DIM:(h + 1) * ATT_HEAD_DIM, cols] = (
                o[g * ATT_HEAD_DIM:(g + 1) * ATT_HEAD_DIM, h * A:(h + 1) * A]
                * inv_rows[h]).astype(BF16)

    row128 = lax.broadcasted_iota(jnp.int32, (LANES, C), 0)
    ones_rows = (row128 == 0).astype(F32).astype(BF16)

    def mlstm_chunk(ci):
        tok = slice(ci * C, (ci + 1) * C)
        gates = gt_ref[:, tok] * LOG2_E
        cum_row = jnp.zeros(gates.shape, F32)
        for part in _split3(gates):
            cum_row = cum_row + _dot(part, tri_scr[...])
        ig_rows = gates[:ML_HEADS]
        b_rows = cum_row[ML_HEADS:]
        key_cols = jnp.concatenate([ig_rows - b_rows, jnp.zeros((LANES - ML_HEADS, C), F32)],
                                   axis=0).T
        for p in range(n_pairs):
            q_c = qm_ref[p * LANES:(p + 1) * LANES, tok]
            k_pair = km_ref[p * LANES:(p + 1) * LANES, tok]
            zero = jnp.zeros_like(q_c)
            state = state_scr[p]
            state_bf = state.astype(BF16)
            new_state = []
            for e_id in range(2):
                h = 2 * p + e_id
                v_rows = slice(h * ML_V_DIM, (h + 1) * ML_V_DIM)
                head_rows = (row128 < ML_QK_DIM) if e_id == 0 else (row128 >= ML_QK_DIM)
                q_pad = jnp.where(head_rows, q_c, zero)
                b_r = b_rows[h:h + 1, :]
                ig_r = ig_rows[h:h + 1, :]
                m_prev = m_scr[h:h + 1, 0:1]
                dm = (b_r + key_cols[:, h:h + 1]) + causal_scr[...]
                inter = b_r + m_prev
                m_row = jnp.maximum(inter, jnp.max(dm, axis=0, keepdims=True))
                w_inter = jnp.exp2(inter - m_row)
                qk = lax.dot_general(k_pair, q_pad, (((0,), (0,)), ((), ())),
                                     preferred_element_type=F32)
                p_t = (qk * jnp.exp2(dm - m_row)).astype(BF16)
                v_ext = jnp.concatenate([vm_ref[v_rows, tok], ones_rows], axis=0)
                num = _dot(v_ext, p_t) + w_inter * _dot(state_bf, q_pad)
                den = num[ML_V_DIM:ML_V_DIM + 1, :]
                hh = num[:ML_V_DIM] * (1.0 / jnp.maximum(jnp.abs(den), jnp.exp2(-m_row)))
                ms = jnp.mean(hh * hh, axis=0, keepdims=True)
                gate = _sigmoid(om_ref[v_rows, tok].astype(F32))
                mix_scr[ATT_Q_W + h * ML_V_DIM:ATT_Q_W + (h + 1) * ML_V_DIM, tok] = (
                    hh * lax.rsqrt(ms + EPS) * mlnw_ref[v_rows, :] * gate).astype(BF16)
                b_last = b_r[:, C - 1:C]
                a_r = b_last - b_r + ig_r
                m_new = jnp.maximum(b_last + m_prev, jnp.max(a_r, axis=-1, keepdims=True))
                sc = jnp.exp2(b_last + m_prev - m_new)
                wsv = (v_ext.astype(F32) * jnp.exp2(a_r - m_new)).astype(BF16)
                new_state.append(sc * state + _dot_nt(wsv, k_pair))
                m_scr[h:h + 1, :] = jnp.broadcast_to(m_new, (1, LANES))
            first = lax.broadcasted_iota(jnp.int32, state.shape, 1) < ML_QK_DIM
            state_scr[p] = jnp.where(first, new_state[0], new_state[1])
            yield

    def out_proj():
        mixt_scr[...] = mix_scr[...].T
        for n in range(D_MODEL // OUT_COLS):
            nc = slice(n * OUT_COLS, (n + 1) * OUT_COLS)
            x1_ref[:, nc] = x_ref[:, nc] + _dot(mixt_scr[...], wout_ref[:, nc])
            yield

    @pl.when(i < n_chunks)
    def _():
        projs = out_proj()
        next(projs, None)
        stage_scores(0)
        pairs = mlstm_chunk(0)
        for j in range(L // A):
            attend(j)
            next(projs, None)
            next(pairs, None)
            next(projs, None)
        for _ in pairs:
            pass
        for _ in projs:
            pass

        @pl.when(i == n_chunks - 1)
        def _():
            for p in range(n_pairs):
                c_t = state_scr[p, :ML_V_DIM, :].T
                for e_id in range(2):
                    ct_ref[0, 2 * p + e_id] = c_t[e_id * ML_QK_DIM:(e_id + 1) * ML_QK_DIM, :]
                nrow_ref[0, p:p + 1, :] = state_scr[p, ML_V_DIM:ML_V_DIM + 1, :]
            for h in range(ML_HEADS):
                m_ref[0, :, h:h + 1] = m_scr[h:h + 1, 0:1] * (1.0 / LOG2_E)
            kt_ref[0] = kvc_ref[:ATT_KV_W, L - WINDOW:]
            vt_ref[0] = kvc_ref[ATT_KV_W:, L - WINDOW:]

    @pl.when(i == n_chunks)
    def _():
        for _ in out_proj():
            pass


def _prompt_mixer_t(batch, seq, sinks, qa, ks, kv, qm, km, vm, om, gt, x2d, wout, mlnw_col):
    tq = MIX_TILE
    nt = seq // tq
    sub = tq // WINDOW
    per_slab = ROW_TILE // tq
    win_per_slab = ROW_TILE // WINDOW
    chunk = lambda b, i: b * nt + jnp.minimum(i, nt - 1)
    lagged = lambda b, i: b * nt + jnp.maximum(i - 1, 0)
    prev_block = lambda b, i: jnp.maximum(chunk(b, i) * sub - 1, 0)
    col = lambda w: pl.BlockSpec(
        (None, w, tq), lambda b, i: (chunk(b, i) // per_slab, 0, chunk(b, i) % per_slab))
    full = lambda a: pl.BlockSpec(a.shape, lambda b, i: (0,) * a.ndim)
    once = lambda a: pl.BlockSpec(a.shape, lambda b, i: (0,) * a.ndim,
                                  pipeline_mode=pl.Buffered(1))
    per_batch = lambda *dims: pl.BlockSpec((1,) + dims, lambda b, i: (b,) + (0,) * len(dims))
    return pl.pallas_call(
        _prompt_mixer_t_kernel,
        grid=(batch, nt + 1),
        in_specs=[pl.BlockSpec(memory_space=pltpu.SMEM),
                  col(ATT_Q_W),
                  pl.BlockSpec((tq, ATT_KV_W), lambda b, i: (chunk(b, i), 0)),
                  pl.BlockSpec((WINDOW, ATT_KV_W), lambda b, i: (prev_block(b, i), 0)),
                  col(2 * ATT_KV_W),
                  pl.BlockSpec((None, 2 * ATT_KV_W, WINDOW),
                               lambda b, i: (prev_block(b, i) // win_per_slab, 0,
                                             prev_block(b, i) % win_per_slab)),
                  col(ML_QK_W), col(ML_QK_W), col(ML_V_W), col(ML_V_W), col(N_GATES),
                  pl.BlockSpec((tq, D_MODEL), lambda b, i: (lagged(b, i), 0)),
                  once(wout), full(mlnw_col)],
        out_specs=[pl.BlockSpec((tq, D_MODEL), lambda b, i: (lagged(b, i), 0)),
                   per_batch(ML_HEADS, ML_QK_DIM, ML_V_DIM),
                   per_batch(ML_HEADS // 2, LANES),
                   per_batch(1, ML_HEADS),
                   per_batch(ATT_KV_W, WINDOW),
                   per_batch(ATT_KV_W, WINDOW)],
        out_shape=[jax.ShapeDtypeStruct((batch * seq, D_MODEL), F32),
                   jax.ShapeDtypeStruct((batch, ML_HEADS, ML_QK_DIM, ML_V_DIM), F32),
                   jax.ShapeDtypeStruct((batch, ML_HEADS // 2, LANES), F32),
                   jax.ShapeDtypeStruct((batch, 1, ML_HEADS), F32),
                   jax.ShapeDtypeStruct((batch, ATT_KV_W, WINDOW), F32),
                   jax.ShapeDtypeStruct((batch, ATT_KV_W, WINDOW), F32)],
        scratch_shapes=[pltpu.VMEM((D_MODEL, tq), BF16),
                        pltpu.VMEM((tq, D_MODEL), BF16),
                        pltpu.VMEM((ML_HEADS // 2, 2 * LANES, LANES), F32),
                        pltpu.VMEM((SUBLANES, LANES), F32),
                        pltpu.VMEM((2, 2 * WINDOW, WINDOW), F32),
                        pltpu.VMEM((ML_CHUNK, ML_CHUNK), F32),
                        pltpu.VMEM((ML_CHUNK, ML_CHUNK), BF16),
                        pltpu.VMEM((2, 2 * WINDOW, ATT_HEADS * WINDOW), F32),
                        pltpu.VMEM((2, 2 * WINDOW, ATT_HEADS * WINDOW), F32),
                        pltpu.VMEM((2 * WINDOW, ATT_HEADS * WINDOW), BF16)],
        compiler_params=pltpu.CompilerParams(dimension_semantics=("arbitrary", "arbitrary"),
                                             vmem_limit_bytes=VMEM_LIMIT),
        name="prompt_mixer_t",
    )(sinks, qa, ks, ks, kv, kv, qm, km, vm, om, gt, x2d, wout, mlnw_col)


def _sample_mixer_kernel(t_len, sinks_ref, qa_ref, kv_ref, ck_ref, cv_ref, qm_ref, km_ref,
                         vm_ref, om_ref, gt_ref, c0_ref, n0_ref, m0_ref, x_ref, wout_ref,
                         mlnw_ref, x1_ref, nk_ref, nv_ref, c_ref, n_ref, m_ref,
                         mix_scr, wperm_scr):
    bt = SAMPLE_BT
    T = t_len
    L = bt * T

    @pl.when(pl.program_id(0) == 0)
    def _():
        _permute_head_rows(wperm_scr, wout_ref)
        wperm_scr[ATT_Q_W:, :] = wout_ref[ATT_Q_W:, :]

    lane3 = lax.broadcasted_iota(jnp.int32, (bt, T, LANES), 2)
    low3 = lane3 < ATT_HEAD_DIM
    lane = lax.broadcasted_iota(jnp.int32, (L, LANES), 1)
    low = lane < ATT_HEAD_DIM

    qa3 = qa_ref[...].astype(F32).reshape(bt, T, ATT_Q_W)
    pieces = []
    for col in range(ATT_GROUP):
        qc = qa3[:, :, col * LANES:(col + 1) * LANES]
        pieces += [jnp.where(low3, qc, 0.0), jnp.where(low3, 0.0, qc)]
    q3 = jnp.concatenate(pieces, axis=1).astype(BF16)
    R = bt * N_STACK * T
    q2 = q3.reshape(R, LANES)
    kv_new = kv_ref[...]
    k_new = kv_new[:, :ATT_KV_W]
    v_new = kv_new[:, ATT_KV_W:]
    ck = ck_ref[...]
    cv = cv_ref[...]
    s_c = jnp.einsum('bqd,bdk->bqk', q3, ck.astype(BF16),
                     preferred_element_type=F32).reshape(R, WINDOW)
    s_n = _dot_nt(q2, k_new.astype(BF16))
    row_c = lax.broadcasted_iota(jnp.int32, (R, WINDOW), 0)
    col_c = lax.broadcasted_iota(jnp.int32, (R, WINDOW), 1)
    s_c = jnp.where(col_c > row_c % T, s_c, -jnp.inf)
    row_n = lax.broadcasted_iota(jnp.int32, (R, L), 0)
    col_n = lax.broadcasted_iota(jnp.int32, (R, L), 1)
    valid_n = (row_n // (N_STACK * T) == col_n // T) & (col_n % T <= row_n % T)
    s_n = jnp.where(valid_n, s_n, -jnp.inf)
    stack_id = (lax.broadcasted_iota(jnp.int32, (R, 1), 0) // T) % N_STACK
    sink = jnp.zeros((R, 1), F32)
    for k_id in range(N_STACK):
        sink = jnp.where(stack_id == k_id, sinks_ref[ATT_HEAD_ORDER[k_id]], sink)
    m = jnp.maximum(jnp.maximum(jnp.max(s_c, axis=-1, keepdims=True),
                                jnp.max(s_n, axis=-1, keepdims=True)), sink)
    e_c = jnp.exp(s_c - m)
    e_n = jnp.exp(s_n - m)
    denom = (jnp.sum(e_c, axis=-1, keepdims=True) + jnp.sum(e_n, axis=-1, keepdims=True)
             + jnp.exp(sink - m))
    o = jnp.einsum('bqk,bdk->bqd', e_c.astype(BF16).reshape(bt, N_STACK * T, WINDOW),
                   cv.astype(BF16), preferred_element_type=F32).reshape(R, LANES)
    o = (o + _dot(e_n.astype(BF16), v_new.astype(BF16))) / denom
    o3 = o.reshape(bt, N_STACK * T, LANES)
    for col in range(ATT_GROUP):
        lo_h = o3[:, (2 * col) * T:(2 * col + 1) * T, :]
        hi_h = o3[:, (2 * col + 1) * T:(2 * col + 2) * T, :]
        mix_scr[:, col * LANES:(col + 1) * LANES] = jnp.where(
            low3, lo_h, hi_h).reshape(L, LANES).astype(BF16)

    keep = lax.broadcasted_iota(jnp.int32, (ATT_KV_W, WINDOW), 1) < WINDOW - T
    k_new_t = k_new.T
    v_new_t = v_new.T
    def roll_caches():
        for q in range(bt):
            shift = (WINDOW - T - q * T) % WINDOW
            nk_ref[q] = jnp.where(keep, pltpu.roll(ck_ref[q], WINDOW - T, axis=1),
                                  pltpu.roll(k_new_t, shift, axis=1))
            nv_ref[q] = jnp.where(keep, pltpu.roll(cv_ref[q], WINDOW - T, axis=1),
                                  pltpu.roll(v_new_t, shift, axis=1))
            if (q + 1) % (bt // ML_HEADS) == 0:
                yield

    rolls = roll_caches()

    r = lax.broadcasted_iota(jnp.int32, (L, L), 0)
    c = lax.broadcasted_iota(jnp.int32, (L, L), 1)
    seg = (r // T == c // T) & (r <= c)
    seg_bias = jnp.where(seg, 0.0, -jnp.inf)
    seg_bf = seg.astype(F32).astype(BF16)
    gates = gt_ref[...] * LOG2_E
    cum_row = jnp.zeros(gates.shape, F32)
    for part in _split3(gates):
        cum_row = cum_row + _dot(part, seg_bf)
    ig_rows = gates[:ML_HEADS]
    b_rows = cum_row[ML_HEADS:]
    gate_cols = jnp.concatenate([ig_rows, b_rows, jnp.zeros((LANES - N_GATES, L), F32)],
                                axis=0).T

    def col_to_row(x_col):
        return jnp.broadcast_to(x_col, (L, LANES)).T[0:1, :]

    ones_rows = (r[:LANES] == 0).astype(F32).astype(BF16)
    qm = qm_ref[...]
    km = km_ref[...]
    qm_f = qm.astype(F32)
    km_f = km.astype(F32)
    n_rep = bt * ML_QK_DIM // LANES
    bd_row = lax.broadcasted_iota(jnp.int32, (L, bt * ML_QK_DIM), 0) // T
    bd_lane = lax.broadcasted_iota(jnp.int32, (L, bt * ML_QK_DIM), 1) // ML_QK_DIM
    block_diag = bd_row == bd_lane

    def spread(x_pair, e):
        other = pltpu.roll(x_pair, ML_QK_DIM, axis=1)
        twice = jnp.where(low, x_pair, other) if e == 0 else jnp.where(low, other, x_pair)
        return jnp.where(block_diag, jnp.concatenate([twice] * n_rep, axis=1), 0.0).astype(BF16)

    def head_stages(h):
        p, e = divmod(h, 2)
        qc = qm[:, p * LANES:(p + 1) * LANES]
        k_pair = km[:, p * LANES:(p + 1) * LANES]
        zero = jnp.zeros_like(qc)
        q_pad = jnp.where(low, qc, zero) if e == 0 else jnp.where(low, zero, qc)
        v_h = vm_ref[:, h * ML_V_DIM:(h + 1) * ML_V_DIM]
        v_ext_t = jnp.concatenate([v_h.astype(F32).T.astype(BF16), ones_rows], axis=0)
        ig_c = gate_cols[:, h:h + 1]
        b_c = gate_cols[:, ML_HEADS + h:ML_HEADS + h + 1]
        b_r = b_rows[h:h + 1, :]
        yield
        m0 = m0_ref[:, :, h:h + 1] * LOG2_E
        inter = b_r + col_to_row(jnp.broadcast_to(m0, (bt, T, 1)).reshape(L, 1))
        dm = (b_r + (ig_c - b_c)) + seg_bias
        m_row = jnp.maximum(inter, jnp.max(dm, axis=0, keepdims=True))
        w_inter = jnp.exp2(inter - m_row)
        yield
        p_t = (_dot_nt(k_pair, q_pad) * jnp.exp2(dm - m_row)).astype(BF16)
        num_t = _dot(v_ext_t, p_t)
        yield
        q_h3 = qm_f[:, h * ML_QK_DIM:(h + 1) * ML_QK_DIM].reshape(bt, T, ML_QK_DIM)
        k_h3 = km_f[:, h * ML_QK_DIM:(h + 1) * ML_QK_DIM].reshape(bt, T, ML_QK_DIM)
        c0 = c0_ref[:, h]
        n0 = n0_ref[:, h:h + 1, :]
        q_c_t = _dot(spread(qm_f[:, p * LANES:(p + 1) * LANES], e),
                     c0.astype(BF16).reshape(bt * ML_QK_DIM, ML_V_DIM)).T
        q_n_r = col_to_row(jnp.sum(q_h3 * n0, axis=-1, keepdims=True).reshape(L, 1))
        yield
        num =num_t[:ML_V_DIM] + w_inter * q_c_t
        den = num_t[ML_V_DIM:ML_V_DIM + 1] + w_inter * q_n_r
        hh = num * (1.0 / jnp.maximum(jnp.abs(den), jnp.exp2(-m_row)))
        ms = jnp.mean(hh * hh, axis=0, keepdims=True)
        yield
        mix_scr[:, ATT_Q_W + h * ML_V_DIM:ATT_Q_W + (h + 1) * ML_V_DIM] = (
            (hh * lax.rsqrt(ms + EPS)).T * mlnw_ref[:, h * ML_V_DIM:(h + 1) * ML_V_DIM]
            * _sigmoid(om_ref[:, h * ML_V_DIM:(h + 1) * ML_V_DIM].astype(F32))).astype(BF16)
        yield
        b3 = b_c.reshape(bt, T, 1)
        b_last = b3[:, T - 1:T, :]
        a3 = b_last - b3 + ig_c.reshape(bt, T, 1)
        m_new = jnp.maximum(b_last + m0, jnp.max(a3, axis=1, keepdims=True))
        sc = jnp.exp2(b_last + m0 - m_new)
        ws = jnp.exp2(a3 - m_new)
        yield
        kw = spread(km_f[:, p * LANES:(p + 1) * LANES] * ws.reshape(L, 1), e)
        d_c = lax.dot_general(kw, v_h, (((0,), (0,)), ((), ())), preferred_element_type=F32)
        c_ref[:, h] = sc * c0 + d_c.reshape(bt, ML_QK_DIM, ML_V_DIM)
        n_ref[:, h:h + 1, :] = sc * n0 + jnp.sum(ws * k_h3, axis=1, keepdims=True)
        m_ref[:, :, h:h + 1] = m_new * (1.0 / LOG2_E)
        yield

    for _ in zip(*[head_stages(h) for h in range(ML_HEADS)]):
        next(rolls, None)
    for _ in rolls:
        pass

    x1_ref[...] = x_ref[...] + _dot(mix_scr[...], wperm_scr[...])


def _sample_mixer(nb, t_len, sinks, qa, kv, ck, cv, qm, km, vm, om, gt, c0, n0, m0, x2d, wout, mlnw):
    bt = SAMPLE_BT
    tl = bt * t_len
    row = lambda w: pl.BlockSpec((tl, w), lambda i: (i, 0))
    full = lambda a: pl.BlockSpec(a.shape, lambda i: (0,) * a.ndim)
    once = lambda a: pl.BlockSpec(a.shape, lambda i: (0,) * a.ndim, pipeline_mode=pl.Buffered(1))
    cache = pl.BlockSpec((bt, ATT_KV_W, WINDOW), lambda i: (i, 0, 0))
    c_spec = pl.BlockSpec((bt, ML_HEADS, ML_QK_DIM, ML_V_DIM), lambda i: (i, 0, 0, 0))
    n_spec = pl.BlockSpec((bt, ML_HEADS, ML_QK_DIM), lambda i: (i, 0, 0))
    m_spec = pl.BlockSpec((bt, 1, ML_HEADS), lambda i: (i, 0, 0))
    return pl.pallas_call(
        functools.partial(_sample_mixer_kernel, t_len),
        grid=(nb // bt,),
        in_specs=[pl.BlockSpec(memory_space=pltpu.SMEM),
                  row(ATT_Q_W), row(2 * ATT_KV_W), cache, cache, row(ML_QK_W), row(ML_QK_W),
                  row(ML_V_W), row(ML_V_W), pl.BlockSpec((N_GATES, tl), lambda i: (0, i)),
                  c_spec, n_spec, m_spec, row(D_MODEL), once(wout), full(mlnw)],
        out_specs=[row(D_MODEL), cache, cache, c_spec, n_spec, m_spec],
        out_shape=[jax.ShapeDtypeStruct((nb * t_len, D_MODEL), F32),
                   jax.ShapeDtypeStruct((nb, ATT_KV_W, WINDOW), F32),
                   jax.ShapeDtypeStruct((nb, ATT_KV_W, WINDOW), F32),
                   jax.ShapeDtypeStruct((nb, ML_HEADS, ML_QK_DIM, ML_V_DIM), F32),
                   jax.ShapeDtypeStruct((nb, ML_HEADS, ML_QK_DIM), F32),
                   jax.ShapeDtypeStruct((nb, 1, ML_HEADS), F32)],
        scratch_shapes=[pltpu.VMEM((tl, D_MODEL), BF16),
                        pltpu.VMEM((D_MODEL, D_MODEL), BF16)],
        compiler_params=pltpu.CompilerParams(dimension_semantics=("arbitrary",),
                                             vmem_limit_bytes=VMEM_LIMIT),
        name="sample_mixer",
    )(sinks, qa, kv, ck, cv, qm, km, vm, om, gt, c0, n0, m0, x2d, wout, mlnw)


def _ffn_kernel(seq_rows, *refs):
    if seq_rows is None:
        (x_ref, nw_ref, w_ref, cw_ref, cb_ref, wd_ref, y_ref, conv_ref,
         gbuf, act_scr, carry) = refs
        hist_ref = None
    else:
        (x_ref, hist_ref, nw_ref, w_ref, cw_ref, cb_ref, wd_ref, y_ref, conv_ref,
         gbuf, act_scr) = refs
        carry = None
    tm = x_ref.shape[0]
    tf = FF_CHUNK
    n_hist = CONV_W - 1
    rows = tm if seq_rows is None else seq_rows
    nseq = tm // rows
    base = SUBLANES
    n_chunks = D_FF // tf

    if carry is not None:
        @pl.when(pl.program_id(1) == 0)
        def _():
            carry[...] = jnp.zeros(carry.shape, F32)

    x = x_ref[...]
    h2 = _rms(x, nw_ref[...]).astype(BF16)

    def proj(f):
        return (_dot(h2, w_ref[:, f * tf:(f + 1) * tf]),
                _dot(h2, w_ref[:, D_FF + f * tf:D_FF + (f + 1) * tf]))

    nxt = proj(0)
    for f in range(n_chunks):
        g, u = nxt
        if f + 1 < n_chunks:
            nxt = proj(f + 1)
        cols = slice(f * tf, (f + 1) * tf)
        s = f % 2
        g3 = g.reshape(nseq, rows, tf)
        if seq_rows is None:
            gbuf[s, :, base - n_hist:base, :] = carry[:, SUBLANES - n_hist:, cols]
            carry[:, SUBLANES - n_hist:, cols] = g3[:, rows - n_hist:, :]
        else:
            gbuf[s, :, base - n_hist:base, :] = hist_ref[:, :, cols]
            conv_ref[:, :, cols] = g3[:, rows - n_hist:, :]
        gbuf[s, :, base:base + rows, :] = g3
        gc = cb_ref[:, cols] + g * cw_ref[CONV_W - 1:CONV_W, cols]
        for d in range(1, CONV_W):
            gm = gbuf[s, :, base - d:base - d + rows, :].reshape(tm, tf)
            gc = gc + gm * cw_ref[CONV_W - 1 - d:CONV_W - d, cols]
        act_scr[:, cols] = (gc * _sigmoid(gc) * u).astype(BF16)
    y_ref[...] = x + _dot(act_scr[...], wd_ref[...])

    if carry is not None:
        @pl.when(pl.program_id(1) == pl.num_programs(1) - 1)
        def _():
            conv_ref[...] = carry[:, SUBLANES - n_hist:, :]


def _ffn_scratch(tm, rows):
    return [pltpu.VMEM((2, tm // rows, SUBLANES + rows, FF_CHUNK), F32),
            pltpu.VMEM((tm, D_FF), BF16)]


def _ffn_prompt(batch, seq, x2d, nw, w, cw, cb, wd):
    tm = FFN_TILE
    nt = seq // tm
    full = lambda a: pl.BlockSpec(a.shape, lambda b, i: (0,) * a.ndim)
    once = lambda a: pl.BlockSpec(a.shape, lambda b, i: (0,) * a.ndim,
                                  pipeline_mode=pl.Buffered(1))
    row = pl.BlockSpec((tm, D_MODEL), lambda b, i: (b * nt + i, 0))
    return pl.pallas_call(
        functools.partial(_ffn_kernel, None),
        grid=(batch, nt),
        in_specs=[row, full(nw), once(w), full(cw), full(cb), once(wd)],
        out_specs=[row, pl.BlockSpec((1, CONV_W - 1, D_FF), lambda b, i: (b, 0, 0))],
        out_shape=[jax.ShapeDtypeStruct((batch * seq, D_MODEL), F32),
                   jax.ShapeDtypeStruct((batch, CONV_W - 1, D_FF), F32)],
        scratch_shapes=_ffn_scratch(tm, tm) + [pltpu.VMEM((1, SUBLANES, D_FF), F32)],
        compiler_params=pltpu.CompilerParams(dimension_semantics=("arbitrary", "arbitrary"),
                                             vmem_limit_bytes=VMEM_LIMIT),
        name="ffn_prompt",
    )(x2d, nw, w, cw, cb, wd)


def _ffn_sample(nb, t_len, x2d, hist, nw, w, cw, cb, wd):
    tm = ROW_TILE
    bt = tm // t_len
    full = lambda a: pl.BlockSpec(a.shape, lambda i: (0,) * a.ndim)
    once = lambda a: pl.BlockSpec(a.shape, lambda i: (0,) * a.ndim, pipeline_mode=pl.Buffered(1))
    row = pl.BlockSpec((tm, D_MODEL), lambda i: (i, 0))
    hist_spec = pl.BlockSpec((bt, CONV_W - 1, D_FF), lambda i: (i, 0, 0))
    return pl.pallas_call(
        functools.partial(_ffn_kernel, t_len),
        grid=(nb // bt,),
        in_specs=[row, hist_spec, full(nw), once(w), full(cw), full(cb), once(wd)],
        out_specs=[row, hist_spec],
        out_shape=[jax.ShapeDtypeStruct((nb * t_len, D_MODEL), F32),
                   jax.ShapeDtypeStruct((nb, CONV_W - 1, D_FF), F32)],
        scratch_shapes=_ffn_scratch(tm, t_len),
        compiler_params=pltpu.CompilerParams(dimension_semantics=("arbitrary",),
                                             vmem_limit_bytes=VMEM_LIMIT),
        name="ffn_sample",
    )(x2d, hist, nw, w, cw, cb, wd)


def _head_mean_matrix(width, head_dim):
    idx = np.arange(width) // head_dim
    return jnp.asarray((idx[:, None] == idx[None, :]).astype(np.float32) / head_dim, dtype=BF16)


def _layer_weights(norm_mix_w, w_in, b_gates, q_norm_w, k_norm_w, sinks, ml_norm_w, w_out,
                   norm_ffn_w, w_ffn_in, conv_w, conv_b, w_down):
    w_in_t = jnp.pad(w_in.T.astype(BF16), ((0, IN_WIDTH_PAD - w_in.shape[1]), (0, 0)))
    return dict(
        nw=norm_mix_w.reshape(1, D_MODEL),
        w_in_t=w_in_t,
        bg=jnp.pad(b_gates, (0, LANES - N_GATES)).reshape(1, LANES),
        qnw=(jnp.tile(q_norm_w, ATT_HEADS) * ATT_SCALE).reshape(1, ATT_Q_W),
        knw=jnp.tile(k_norm_w, ATT_KV_HEADS).reshape(1, ATT_KV_W),
        gq=_head_mean_matrix(ATT_Q_W, ATT_HEAD_DIM),
        gk=_head_mean_matrix(ATT_KV_W, ATT_HEAD_DIM),
        bg_col=b_gates.reshape(N_GATES, 1),
        qnw_col=(jnp.tile(q_norm_w, ATT_HEADS) * (ATT_SCALE * LOG2_E)).reshape(ATT_Q_W, 1),
        knw_col=jnp.tile(k_norm_w, ATT_KV_HEADS).reshape(ATT_KV_W, 1),
        mlnw_col=ml_norm_w.reshape(ML_V_W, 1),
        sinks=sinks,
        mlnw=ml_norm_w.reshape(1, ML_V_W),
        wout=w_out.astype(BF16),
        nfw=norm_ffn_w.reshape(1, D_MODEL),
        wff=w_ffn_in.astype(BF16),
        cw=conv_w,
        cb=conv_b.reshape(1, D_FF),
        wd=w_down.astype(BF16),
    )


def _cache_from_t(a_t):
    n = a_t.shape[0]
    return jnp.transpose(a_t.reshape(n, ATT_KV_HEADS, ATT_HEAD_DIM, WINDOW), (0, 3, 1, 2))


def _cache_to_t(a):
    n = a.shape[0]
    return jnp.transpose(a, (0, 2, 3, 1)).reshape(n, ATT_KV_W, WINDOW)


def _prompt_layer(x, w):
    batch, seq, _ = x.shape
    assert seq % ROW_TILE == 0 and ROW_TILE % MIX_TILE == 0 and MIX_TILE % WINDOW == 0
    assert MIX_TILE == ML_CHUNK
    assert seq % FFN_TILE == 0
    x2d = x.reshape(batch * seq, D_MODEL)
    qa, ks, kv, qm, km, vm, om, gt = _inproj_t(x2d, w["nw"], w["w_in_t"], w["bg_col"],
                                               w["qnw_col"], w["knw_col"])
    x1, c_t, n_row, m, k_t, v_t = _prompt_mixer_t(batch, seq, w["sinks"], qa, ks, kv, qm, km, vm,
                                                  om, gt, x2d, w["wout"], w["mlnw_col"])
    y, conv = _ffn_prompt(batch, seq, x1, w["nfw"], w["wff"], w["cw"], w["cb"], w["wd"])
    return (y.reshape(batch, seq, D_MODEL), _cache_from_t(k_t), _cache_from_t(v_t),
            jnp.swapaxes(c_t, -1, -2), n_row.reshape(batch, ML_HEADS, ML_QK_DIM),
            m.reshape(batch, ML_HEADS), conv)


def _sample_layer(x, ck, cv, c0, n0, m0, conv_buf, w):
    nb, t_len, _ = x.shape
    assert t_len == SUBLANES and SAMPLE_BT * t_len == LANES and nb % SAMPLE_BT == 0
    assert (nb * t_len) % ROW_TILE == 0
    x2d = x.reshape(nb * t_len, D_MODEL)
    qa, kv, qm, km, vm, om, gt = _inproj(x2d, w["nw"], w["w_in_t"], w["bg"], w["qnw"], w["knw"],
                                         w["gq"], w["gk"])
    x1, nk_t, nv_t, c_t, n, m = _sample_mixer(
        nb, t_len, w["sinks"], qa, kv, _cache_to_t(ck), _cache_to_t(cv), qm, km, vm, om, gt,
        jnp.swapaxes(c0, -1, -2), n0, m0.reshape(nb, 1, ML_HEADS), x2d, w["wout"], w["mlnw"])
    y, conv = _ffn_sample(nb, t_len, x1, conv_buf, w["nfw"], w["wff"], w["cw"], w["cb"],
                          w["wd"])
    return (y.reshape(nb, t_len, D_MODEL), _cache_from_t(nk_t), _cache_from_t(nv_t),
            jnp.swapaxes(c_t, -1, -2), n, m.reshape(nb, ML_HEADS), conv)


def kernel(x_prompt, x_sample, cache_attn_k, cache_attn_v, state_mlstm_C, state_mlstm_n,
           state_mlstm_m, cache_ffn_conv, norm_mix_w, w_in, b_gates, q_norm_w, k_norm_w,
           sinks, ml_norm_w, w_out, norm_ffn_w, w_ffn_in, conv_w, conv_b, w_down):
    depth = w_in.shape[0]
    yp, ys = x_prompt, x_sample
    sp = [[] for _ in range(6)]
    ss = [[] for _ in range(6)]
    for l in range(depth):
        w = _layer_weights(norm_mix_w[l], w_in[l], b_gates[l], q_norm_w[l], k_norm_w[l], sinks[l],
                           ml_norm_w[l], w_out[l], norm_ffn_w[l], w_ffn_in[l], conv_w[l],
                           conv_b[l], w_down[l])
        yp, *st_p = _prompt_layer(yp, w)
        ys, *st_s = _sample_layer(ys, cache_attn_k[l], cache_attn_v[l], state_mlstm_C[l],
                                  state_mlstm_n[l], state_mlstm_m[l], cache_ffn_conv[l], w)
        for i in range(6):
            sp[i].append(st_p[i])
            ss[i].append(st_s[i])
    k_p, v_p, c_p, n_p, m_p, conv_p = [jnp.stack(a) for a in sp]
    k_s, v_s, c_s, n_s, m_s, conv_s = [jnp.stack(a) for a in ss]
    return (yp, ys, k_p, v_p, c_p, n_p, m_p, conv_p, k_s, v_s, c_s, n_s, m_s, conv_s)
```

```python
import functools

import numpy as np
import jax
import jax.numpy as jnp
from jax import lax
from jax.experimental import pallas as pl
from jax.experimental.pallas import tpu as pltpu

F32 = jnp.float32
BF16 = jnp.bfloat16

D_MODEL = 1024
ATT_HEADS = 8
ATT_KV_HEADS = 2
ATT_HEAD_DIM = 64
ATT_GROUP = ATT_HEADS // ATT_KV_HEADS
WINDOW = 128
ML_HEADS = 4
ML_V_DIM = 128
ML_QK_DIM = 64
D_FF = 2816
CONV_W = 3
EPS = 1e-6
ATT_SCALE = ATT_HEAD_DIM ** -0.5
ML_SCALE = ML_QK_DIM ** -0.5
LOG2_E = 1.4426950408889634

ATT_Q_W = ATT_HEADS * ATT_HEAD_DIM
ATT_KV_W = ATT_KV_HEADS * ATT_HEAD_DIM
ML_QK_W = ML_HEADS * ML_QK_DIM
ML_V_W = ML_HEADS * ML_V_DIM
N_GATES = 2 * ML_HEADS
N_STACK = 2 * ATT_GROUP

LANES = 128
SUBLANES = 8

OFF_QA = 0
OFF_KV = OFF_QA + ATT_Q_W
OFF_QM = OFF_KV + 2 * ATT_KV_W
OFF_KM = OFF_QM + ML_QK_W
OFF_VM = OFF_KM + ML_QK_W
OFF_OM = OFF_VM + ML_V_W
OFF_GL = OFF_OM + ML_V_W
IN_WIDTH_PAD = OFF_GL + LANES

ATT_HEAD_ORDER = tuple(h for c in range(ATT_GROUP) for h in (c, c + ATT_GROUP))

ROW_TILE = 512
FFN_TILE = 1024
INPROJ_SUB = 256
NORM_ROWS = 64
MIX_TILE = 1024
ML_CHUNK = 256
OUT_COLS = 256
FF_CHUNK = 256
SAMPLE_BT = 16
VMEM_LIMIT = 56 * 1024 * 1024


def _dot(a, b):
    return jnp.dot(a, b, preferred_element_type=F32)


def _dot_nt(a, b):
    return lax.dot_general(a, b, (((1,), (1,)), ((), ())), preferred_element_type=F32)


def _split3(x):
    hi = x.astype(BF16)
    r1 = x - hi.astype(F32)
    mid = r1.astype(BF16)
    lo = (r1 - mid.astype(F32)).astype(BF16)
    return hi, mid, lo


def _rms(x, w):
    ms = jnp.mean(x * x, axis=-1, keepdims=True)
    return x * lax.rsqrt(ms + EPS) * w


def _log_sigmoid(x):
    return jnp.minimum(x, 0.0) - jnp.log1p(jnp.exp(-jnp.abs(x)))


def _sigmoid(x):
    return 1.0 / (1.0 + jnp.exp(-x))


def _permute_head_rows(dst_ref, src_ref):
    for k, h in enumerate(ATT_HEAD_ORDER):
        dst_ref[k * ATT_HEAD_DIM:(k + 1) * ATT_HEAD_DIM, :] = (
            src_ref[h * ATT_HEAD_DIM:(h + 1) * ATT_HEAD_DIM, :])


def _inproj_kernel(x_ref, nw_ref, w_ref, bg_ref, qnw_ref, knw_ref, gq_ref, gk_ref,
                   qa_ref, kv_ref, qm_ref, km_ref, vm_ref, om_ref, gt_ref, wq_scr):
    @pl.when(pl.program_id(0) == 0)
    def _():
        _permute_head_rows(wq_scr, w_ref)

    h = _rms(x_ref[...], nw_ref[...]).astype(BF16)

    def proj(lo, width):
        return _dot_nt(h, w_ref[lo:lo + width, :])

    q = _dot_nt(h, wq_scr[...])
    q_ms = _dot((q * q).astype(BF16), gq_ref[...])
    qa_ref[...] = (q * lax.rsqrt(q_ms + EPS) * qnw_ref[...]).astype(BF16)

    kv = proj(OFF_KV, 2 * ATT_KV_W)
    k = kv[:, :ATT_KV_W]
    k_ms = _dot((k * k).astype(BF16), gk_ref[...])
    kv_ref[:, :ATT_KV_W] = k * lax.rsqrt(k_ms + EPS) * knw_ref[...]
    kv_ref[:, ATT_KV_W:] = kv[:, ATT_KV_W:]

    qm_ref[...] = (proj(OFF_QM, ML_QK_W) * ML_SCALE).astype(BF16)
    km_ref[...] = proj(OFF_KM, ML_QK_W).astype(BF16)
    vm_ref[...] = proj(OFF_VM, ML_V_W).astype(BF16)
    om_ref[...] = proj(OFF_OM, ML_V_W).astype(BF16)

    gl = proj(OFF_GL, LANES) + bg_ref[...]
    lane = lax.broadcasted_iota(jnp.int32, gl.shape, 1)
    g = jnp.where(lane < ML_HEADS, gl, _log_sigmoid(gl))
    gt_ref[...] = g.T[:N_GATES, :]


def _inproj(x2d, nw, w_in_t, bg, qnw, knw, gq, gk):
    n = x2d.shape[0]
    tm = ROW_TILE
    row = lambda w: pl.BlockSpec((tm, w), lambda i: (i, 0))
    full = lambda a: pl.BlockSpec(a.shape, lambda i: (0,) * a.ndim)
    once = lambda a: pl.BlockSpec(a.shape, lambda i: (0,) * a.ndim, pipeline_mode=pl.Buffered(1))
    return pl.pallas_call(
        _inproj_kernel,
        grid=(n // tm,),
        in_specs=[row(D_MODEL), full(nw), once(w_in_t), full(bg), full(qnw), full(knw),
                  full(gq), full(gk)],
        out_specs=[row(ATT_Q_W), row(2 * ATT_KV_W), row(ML_QK_W), row(ML_QK_W),
                   row(ML_V_W), row(ML_V_W), pl.BlockSpec((N_GATES, tm), lambda i: (0, i))],
        out_shape=[jax.ShapeDtypeStruct((n, ATT_Q_W), BF16),
                   jax.ShapeDtypeStruct((n, 2 * ATT_KV_W), F32),
                   jax.ShapeDtypeStruct((n, ML_QK_W), BF16),
                   jax.ShapeDtypeStruct((n, ML_QK_W), BF16),
                   jax.ShapeDtypeStruct((n, ML_V_W), BF16),
                   jax.ShapeDtypeStruct((n, ML_V_W), BF16),
                   jax.ShapeDtypeStruct((N_GATES, n), F32)],
        scratch_shapes=[pltpu.VMEM((ATT_Q_W, D_MODEL), BF16)],
        compiler_params=pltpu.CompilerParams(dimension_semantics=("arbitrary",),
                                             vmem_limit_bytes=VMEM_LIMIT),
        name="inproj",
    )(x2d, nw, w_in_t, bg, qnw, knw, gq, gk)


def _head_norm_t(z, head_dim, w_col):
    rows, tokens = z.shape
    z3 = z.reshape(rows // head_dim, head_dim, tokens)
    ms = jnp.mean(z3 * z3, axis=1, keepdims=True)
    return (z3 * lax.rsqrt(ms + EPS)).reshape(rows, tokens) * w_col


def _inproj_t_kernel(x_ref, nw_ref, w_ref, bg_ref, qnw_ref, knw_ref,
                     qa_ref, ks_ref, kv_ref, qm_ref, km_ref, vm_ref, om_ref, gt_ref, h_scr):
    tm = x_ref.shape[0]
    sub = INPROJ_SUB

    def norm_rows(c):
        for r0 in range(c * sub, (c + 1) * sub, NORM_ROWS):
            rows = slice(r0, r0 + NORM_ROWS)
            h_scr[rows, :] = _rms(x_ref[rows, :], nw_ref[...]).astype(BF16)
            yield

    def project(c):
        tok = slice(c * sub, (c + 1) * sub)
        h = h_scr[tok, :]

        def proj(lo, width):
            return _dot_nt(w_ref[lo:lo + width, :], h)

        qa_ref[:, tok] = _head_norm_t(proj(OFF_QA, ATT_Q_W), ATT_HEAD_DIM,
                                      qnw_ref[...]).astype(BF16)
        yield
        kv = proj(OFF_KV, 2 * ATT_KV_W)
        k = _head_norm_t(kv[:ATT_KV_W], ATT_HEAD_DIM, knw_ref[...])
        kv_ref[:ATT_KV_W, tok] = k
        kv_ref[ATT_KV_W:, tok] = kv[ATT_KV_W:]
        ks_ref[tok, :] = k.T.astype(BF16)
        qm_ref[:, tok] = (proj(OFF_QM, ML_QK_W) * ML_SCALE).astype(BF16)
        yield
        km_ref[:, tok] = proj(OFF_KM, ML_QK_W).astype(BF16)
        vm_ref[:, tok] = proj(OFF_VM, ML_V_W).astype(BF16)
        yield
        om_ref[:, tok] = proj(OFF_OM, ML_V_W).astype(BF16)
        gl = proj(OFF_GL, 2 * SUBLANES)[:N_GATES] + bg_ref[...]
        row = lax.broadcasted_iota(jnp.int32, gl.shape, 0)
        gt_ref[:, tok] = jnp.where(row < ML_HEADS, gl, _log_sigmoid(gl))
        yield

    for _ in norm_rows(0):
        pass
    for c in range(tm // sub):
        norms = norm_rows(c + 1) if c + 1 < tm // sub else iter(())
        for _ in project(c):
            next(norms, None)
        for _ in norms:
            pass


def _inproj_t(x2d, nw, w_in_t, bg_col, qnw_col, knw_col):
    n = x2d.shape[0]
    tm = MIX_TILE
    full = lambda a: pl.BlockSpec(a.shape, lambda i: (0,) * a.ndim)
    once = lambda a: pl.BlockSpec(a.shape, lambda i: (0,) * a.ndim, pipeline_mode=pl.Buffered(1))
    col = lambda w: pl.BlockSpec((None, w, tm), lambda i: (i, 0, 0))
    slab = lambda w, dt: jax.ShapeDtypeStruct((n // tm, w, tm), dt)
    return pl.pallas_call(
        _inproj_t_kernel,
        grid=(n // tm,),
        in_specs=[pl.BlockSpec((tm, D_MODEL), lambda i: (i, 0)), full(nw), once(w_in_t),
                  full(bg_col), full(qnw_col), full(knw_col)],
        out_specs=[col(ATT_Q_W), pl.BlockSpec((tm, ATT_KV_W), lambda i: (i, 0)),
                   col(2 * ATT_KV_W), col(ML_QK_W), col(ML_QK_W), col(ML_V_W), col(ML_V_W),
                   col(N_GATES)],
        out_shape=[slab(ATT_Q_W, BF16),
                   jax.ShapeDtypeStruct((n, ATT_KV_W), BF16),
                   slab(2 * ATT_KV_W, F32), slab(ML_QK_W, BF16), slab(ML_QK_W, BF16),
                   slab(ML_V_W, BF16), slab(ML_V_W, BF16), slab(N_GATES, F32)],
        scratch_shapes=[pltpu.VMEM((tm, D_MODEL), BF16)],
        compiler_params=pltpu.CompilerParams(dimension_semantics=("arbitrary",),
                                             vmem_limit_bytes=VMEM_LIMIT),
        name="inproj_t",
    )(x2d, nw, w_in_t, bg_col, qnw_col, knw_col)


def _gate_forms(gates, seg_mask, want_raw_col):
    L = gates.shape[1]
    m_bf = seg_mask.astype(F32).astype(BF16)
    cum_row = jnp.zeros(gates.shape, F32)
    cum_col = jnp.zeros((L, gates.shape[0]), F32)
    raw_col = None
    if want_raw_col:
        r = lax.broadcasted_iota(jnp.int32, (L, L), 0)
        c = lax.broadcasted_iota(jnp.int32, (L, L), 1)
        eye = (r == c).astype(F32).astype(BF16)
        raw_col = jnp.zeros((L, gates.shape[0]), F32)
    for part in _split3(gates):
        cum_row = cum_row + _dot_nt(part, m_bf)
        cum_col = cum_col + _dot_nt(m_bf, part)
        if want_raw_col:
            raw_col = raw_col + _dot_nt(eye, part)
    return cum_row, cum_col, raw_col


def _mlstm_intra(q_pad, k_pair, v_ext, seg_mask, b_c, b_r, ig_r, m_prev_c):
    dm = jnp.where(seg_mask, b_c + (ig_r - b_r), -jnp.inf)
    inter = b_c + m_prev_c
    m_row = jnp.maximum(inter, jnp.max(dm, axis=-1, keepdims=True))
    w_inter = jnp.exp(inter - m_row)
    p = _dot_nt(q_pad, k_pair) * jnp.exp(dm - m_row)
    return _dot(p.astype(BF16), v_ext), m_row, w_inter


def _mlstm_out(pv, m_row, w_inter, q_c, q_n, mlnw_h, om_h):
    num = pv[:, :ML_V_DIM] + w_inter * q_c
    den = pv[:, ML_V_DIM:ML_V_DIM + 1] + w_inter * q_n
    hh = num / jnp.maximum(jnp.abs(den), jnp.exp(-m_row))
    return (_rms(hh, mlnw_h) * _sigmoid(om_h.astype(F32))).astype(BF16)


def _ones_col(rows):
    lane = lax.broadcasted_iota(jnp.int32, (rows, LANES), 1)
    return (lane == 0).astype(F32).astype(BF16)


def _prompt_mixer_t_kernel(sinks_ref, qa_ref, ksc_ref, ksp_ref, kvc_ref, kvp_ref, qm_ref, km_ref,
                           vm_ref, om_ref, gt_ref, x_ref, wout_ref, mlnw_ref,
                           x1_ref, ct_ref, nrow_ref, m_ref, kt_ref, vt_ref,
                           mix_scr, state_scr, m_scr, band_scr, causal_scr, tri_scr,
                           s_scr_a, s_scr_b, e_scr):
    i = pl.program_id(1)
    A = WINDOW
    L = MIX_TILE
    C = ML_CHUNK
    n_pairs = ML_HEADS // 2

    @pl.when(i == 0)
    def _():
        state_scr[...] = jnp.zeros(state_scr.shape, F32)
        m_scr[...] = jnp.zeros(m_scr.shape, F32)
        kj = lax.broadcasted_iota(jnp.int32, (2 * A, A), 0)
        qi = lax.broadcasted_iota(jnp.int32, (2 * A, A), 1)
        band = (kj > qi) & (kj <= qi + WINDOW)
        band_scr[0] = jnp.where(band, 0.0, -jnp.inf)
        band_scr[1] = jnp.where(band & (kj >= A), 0.0, -jnp.inf)
        r = lax.broadcasted_iota(jnp.int32, (C, C), 0)
        c = lax.broadcasted_iota(jnp.int32, (C, C), 1)
        causal_scr[...] = jnp.where(r <= c, 0.0, -jnp.inf)
        tri_scr[...] = (r <= c).astype(F32).astype(BF16)

    k_all = jnp.concatenate([ksp_ref[...], ksc_ref[...]], axis=0)
    v_all = jnp.concatenate([kvp_ref[ATT_KV_W:, :], kvc_ref[ATT_KV_W:, :]], axis=1).astype(BF16)
    zero_q = jnp.zeros((ATT_HEAD_DIM, A), BF16)
    slot = 0
    s_bufs = (s_scr_a, s_scr_b)

    def stage_scores(j):
        pieces = []
        for h in range(ATT_HEADS):
            q_h = qa_ref[h * ATT_HEAD_DIM:(h + 1) * ATT_HEAD_DIM, j * A:(j + 1) * A]
            pieces.append(jnp.concatenate([q_h, zero_q] if h < ATT_GROUP else [zero_q, q_h],
                                          axis=0))
        s_bufs[j % 2][slot] = _dot(k_all[j * A:(j + 2) * A, :], jnp.concatenate(pieces, axis=1))

    def attend(j):
        cols = slice(j * A, (j + 1) * A)
        vt = v_all[:, j * A:(j + 2) * A]
        if j + 1 < L // A:
            stage_scores(j + 1)
        s_buf = s_bufs[j % 2]
        bias = jnp.where(i > 0, band_scr[0], band_scr[1]) if j == 0 else band_scr[0]
        m_rows = []
        for h in range(ATT_HEADS):
            sb = s_buf[slot, :, h * A:(h + 1) * A] + bias
            m_rows.append(jnp.maximum(jnp.max(sb, axis=0, keepdims=True),
                                      sinks_ref[h] * LOG2_E))
        inv_rows = []
        for h in range(ATT_HEADS):
            e = jnp.exp2(s_buf[slot, :, h * A:(h + 1) * A] + (bias - m_rows[h]))
            e_scr[:, h * A:(h + 1) * A] = e.astype(BF16)
            inv_rows.append(1.0 / (jnp.sum(e, axis=0, keepdims=True)
                                   + jnp.exp2(sinks_ref[h] * LOG2_E - m_rows[h])))
        o = _dot(vt, e_scr[...])
        for h in range(ATT_HEADS):
            g = h // ATT_GROUP
            mix_scr[h * ATT_HEAD_DIM:(h + 1) * ATT_HEAD_DIM, cols] = (
                o[g * ATT_HEAD_DIM:(g + 1) * ATT_HEAD_DIM, h * A:(h + 1) * A]
                * inv_rows[h]).astype(BF16)

    row128 = lax.broadcasted_iota(jnp.int32, (LANES, C), 0)
    ones_rows = (row128 == 0).astype(F32).astype(BF16)

    def mlstm_chunk(ci):
        tok = slice(ci * C, (ci + 1) * C)
        gates = gt_ref[:, tok] * LOG2_E
        cum_row = jnp.zeros(gates.shape, F32)
        for part in _split3(gates):
            cum_row = cum_row + _dot(part, tri_scr[...])
        ig_rows = gates[:ML_HEADS]
        b_rows = cum_row[ML_HEADS:]
        key_cols = jnp.concatenate([ig_rows - b_rows, jnp.zeros((LANES - ML_HEADS, C), F32)],
                                   axis=0).T
        for p in range(n_pairs):
            q_c = qm_ref[p * LANES:(p + 1) * LANES, tok]
            k_pair = km_ref[p * LANES:(p + 1) * LANES, tok]
            zero = jnp.zeros_like(q_c)
            state = state_scr[p]
            state_bf = state.astype(BF16)
            new_state = []
            for e_id in range(2):
                h = 2 * p + e_id
                v_rows = slice(h * ML_V_DIM, (h + 1) * ML_V_DIM)
                head_rows = (row128 < ML_QK_DIM) if e_id == 0 else (row128 >= ML_QK_DIM)
                q_pad = jnp.where(head_rows, q_c, zero)
                b_r = b_rows[h:h + 1, :]
                ig_r = ig_rows[h:h + 1, :]
                m_prev = m_scr[h:h + 1, 0:1]
                dm = (b_r + key_cols[:, h:h + 1]) + causal_scr[...]
                inter = b_r + m_prev
                m_row = jnp.maximum(inter, jnp.max(dm, axis=0, keepdims=True))
                w_inter = jnp.exp2(inter - m_row)
                qk = lax.dot_general(k_pair, q_pad, (((0,), (0,)), ((), ())),
                                     preferred_element_type=F32)
                p_t = (qk * jnp.exp2(dm - m_row)).astype(BF16)
                v_ext = jnp.concatenate([vm_ref[v_rows, tok], ones_rows], axis=0)
                num = _dot(v_ext, p_t) + w_inter * _dot(state_bf, q_pad)
                den = num[ML_V_DIM:ML_V_DIM + 1, :]
                hh = num[:ML_V_DIM] * (1.0 / jnp.maximum(jnp.abs(den), jnp.exp2(-m_row)))
                ms = jnp.mean(hh * hh, axis=0, keepdims=True)
                gate = _sigmoid(om_ref[v_rows, tok].astype(F32))
                mix_scr[ATT_Q_W + h * ML_V_DIM:ATT_Q_W + (h + 1) * ML_V_DIM, tok] = (
                    hh * lax.rsqrt(ms + EPS) * mlnw_ref[v_rows, :] * gate).astype(BF16)
                b_last = b_r[:, C - 1:C]
                a_r = b_last - b_r + ig_r
                m_new = jnp.maximum(b_last + m_prev, jnp.max(a_r, axis=-1, keepdims=True))
                sc = jnp.exp2(b_last + m_prev - m_new)
                wsv = (v_ext.astype(F32) * jnp.exp2(a_r - m_new)).astype(BF16)
                new_state.append(sc * state + _dot_nt(wsv, k_pair))
                m_scr[h:h + 1, :] = jnp.broadcast_to(m_new, (1, LANES))
            first = lax.broadcasted_iota(jnp.int32, state.shape, 1) < ML_QK_DIM
            state_scr[p] = jnp.where(first, new_state[0], new_state[1])
            yield

    def out_proj(ci):
        tok = slice(ci * C, (ci + 1) * C)
        mix_t = mix_scr[:, tok].T
        for n in range(D_MODEL // OUT_COLS):
            nc = slice(n * OUT_COLS, (n + 1) * OUT_COLS)
            x1_ref[tok, nc] = x_ref[tok, nc] + _dot(mix_t, wout_ref[:, nc])
            yield

    stage_scores(0)
    n_sub = C // A
    pairs = (step for ci in range(L // C) for step in mlstm_chunk(ci))
    projs = iter(())
    for j in range(L // A):
        if j and j % n_sub == 0:
            projs = out_proj(j // n_sub - 1)
        attend(j)
        next(projs, None)
        next(pairs, None)
        next(projs, None)
    for _ in pairs:
        pass
    for _ in projs:
        pass
    for _ in out_proj(L // C - 1):
        pass

    @pl.when(i == pl.num_programs(1) - 1)
    def _():
        for p in range(n_pairs):
            c_t = state_scr[p, :ML_V_DIM, :].T
            for e_id in range(2):
                ct_ref[0, 2 * p + e_id] = c_t[e_id * ML_QK_DIM:(e_id + 1) * ML_QK_DIM, :]
            nrow_ref[0, p:p + 1, :] = state_scr[p, ML_V_DIM:ML_V_DIM + 1, :]
        for h in range(ML_HEADS):
            m_ref[0, :, h:h + 1] = m_scr[h:h + 1, 0:1] * (1.0 / LOG2_E)
        kt_ref[0] = kvc_ref[:ATT_KV_W, L - WINDOW:]
        vt_ref[0] = kvc_ref[ATT_KV_W:, L - WINDOW:]


def _prompt_mixer_t(batch, seq, sinks, qa, ks, kv, qm, km, vm, om, gt, x2d, wout, mlnw_col):
    tq = MIX_TILE
    nt = seq // tq
    sub = tq // WINDOW
    col = lambda w: pl.BlockSpec((None, w, tq), lambda b, i: (b * nt + i, 0, 0))
    full = lambda a: pl.BlockSpec(a.shape, lambda b, i: (0,) * a.ndim)
    once = lambda a: pl.BlockSpec(a.shape, lambda b, i: (0,) * a.ndim,
                                  pipeline_mode=pl.Buffered(1))
    prev_block = lambda b, i: jnp.maximum((b * nt + i) * sub - 1, 0)
    per_batch = lambda *dims: pl.BlockSpec((1,) + dims, lambda b, i: (b,) + (0,) * len(dims))
    return pl.pallas_call(
        _prompt_mixer_t_kernel,
        grid=(batch, nt),
        in_specs=[pl.BlockSpec(memory_space=pltpu.SMEM),
                  col(ATT_Q_W),
                  pl.BlockSpec((tq, ATT_KV_W), lambda b, i: (b * nt + i, 0)),
                  pl.BlockSpec((WINDOW, ATT_KV_W), lambda b, i: (prev_block(b, i), 0)),
                  col(2 * ATT_KV_W),
                  pl.BlockSpec((None, 2 * ATT_KV_W, WINDOW),
                               lambda b, i: (prev_block(b, i) // sub, 0, prev_block(b, i) % sub)),
                  col(ML_QK_W), col(ML_QK_W), col(ML_V_W), col(ML_V_W), col(N_GATES),
                  pl.BlockSpec((tq, D_MODEL), lambda b, i: (b * nt + i, 0)),
                  once(wout), full(mlnw_col)],
        out_specs=[pl.BlockSpec((tq, D_MODEL), lambda b, i: (b * nt + i, 0)),
                   per_batch(ML_HEADS, ML_QK_DIM, ML_V_DIM),
                   per_batch(ML_HEADS // 2, LANES),
                   per_batch(1, ML_HEADS),
                   per_batch(ATT_KV_W, WINDOW),
                   per_batch(ATT_KV_W, WINDOW)],
        out_shape=[jax.ShapeDtypeStruct((batch * seq, D_MODEL), F32),
                   jax.ShapeDtypeStruct((batch, ML_HEADS, ML_QK_DIM, ML_V_DIM), F32),
                   jax.ShapeDtypeStruct((batch, ML_HEADS // 2, LANES), F32),
                   jax.ShapeDtypeStruct((batch, 1, ML_HEADS), F32),
                   jax.ShapeDtypeStruct((batch, ATT_KV_W, WINDOW), F32),
                   jax.ShapeDtypeStruct((batch, ATT_KV_W, WINDOW), F32)],
        scratch_shapes=[pltpu.VMEM((D_MODEL, tq), BF16),
                        pltpu.VMEM((ML_HEADS // 2, 2 * LANES, LANES), F32),
                        pltpu.VMEM((SUBLANES, LANES), F32),
                        pltpu.VMEM((2, 2 * WINDOW, WINDOW), F32),
                        pltpu.VMEM((ML_CHUNK, ML_CHUNK), F32),
                        pltpu.VMEM((ML_CHUNK, ML_CHUNK), BF16),
                        pltpu.VMEM((2, 2 * WINDOW, ATT_HEADS * WINDOW), F32),
                        pltpu.VMEM((2, 2 * WINDOW, ATT_HEADS * WINDOW), F32),
                        pltpu.VMEM((2 * WINDOW, ATT_HEADS * WINDOW), BF16)],
        compiler_params=pltpu.CompilerParams(dimension_semantics=("arbitrary", "arbitrary"),
                                             vmem_limit_bytes=VMEM_LIMIT),
        name="prompt_mixer_t",
    )(sinks, qa, ks, ks, kv, kv, qm, km, vm, om, gt, x2d, wout, mlnw_col)


def _sample_mixer_kernel(t_len, sinks_ref, qa_ref, kv_ref, ck_ref, cv_ref, qm_ref, km_ref,
                         vm_ref, om_ref, gt_ref, c0_ref, n0_ref, m0_ref, x_ref, wout_ref,
                         mlnw_ref, x1_ref, nk_ref, nv_ref, c_ref, n_ref, m_ref,
                         mix_scr, wperm_scr):
    bt = SAMPLE_BT
    T = t_len
    L = bt * T

    @pl.when(pl.program_id(0) == 0)
    def _():
        _permute_head_rows(wperm_scr, wout_ref)
        wperm_scr[ATT_Q_W:, :] = wout_ref[ATT_Q_W:, :]

    lane3 = lax.broadcasted_iota(jnp.int32, (bt, T, LANES), 2)
    low3 = lane3 < ATT_HEAD_DIM
    lane = lax.broadcasted_iota(jnp.int32, (L, LANES), 1)
    low = lane < ATT_HEAD_DIM

    qa3 = qa_ref[...].astype(F32).reshape(bt, T, ATT_Q_W)
    pieces = []
    for col in range(ATT_GROUP):
        qc = qa3[:, :, col * LANES:(col + 1) * LANES]
        pieces += [jnp.where(low3, qc, 0.0), jnp.where(low3, 0.0, qc)]
    q3 = jnp.concatenate(pieces, axis=1).astype(BF16)
    R = bt * N_STACK * T
    q2 = q3.reshape(R, LANES)
    kv_new = kv_ref[...]
    k_new = kv_new[:, :ATT_KV_W]
    v_new = kv_new[:, ATT_KV_W:]
    ck = ck_ref[...]
    cv = cv_ref[...]
    s_c = jnp.einsum('bqd,bdk->bqk', q3, ck.astype(BF16),
                     preferred_element_type=F32).reshape(R, WINDOW)
    s_n = _dot_nt(q2, k_new.astype(BF16))
    row_c = lax.broadcasted_iota(jnp.int32, (R, WINDOW), 0)
    col_c = lax.broadcasted_iota(jnp.int32, (R, WINDOW), 1)
    s_c = jnp.where(col_c > row_c % T, s_c, -jnp.inf)
    row_n = lax.broadcasted_iota(jnp.int32, (R, L), 0)
    col_n = lax.broadcasted_iota(jnp.int32, (R, L), 1)
    valid_n = (row_n // (N_STACK * T) == col_n // T) & (col_n % T <= row_n % T)
    s_n = jnp.where(valid_n, s_n, -jnp.inf)
    stack_id = (lax.broadcasted_iota(jnp.int32, (R, 1), 0) // T) % N_STACK
    sink = jnp.zeros((R, 1), F32)
    for k_id in range(N_STACK):
        sink = jnp.where(stack_id == k_id, sinks_ref[ATT_HEAD_ORDER[k_id]], sink)
    m = jnp.maximum(jnp.maximum(jnp.max(s_c, axis=-1, keepdims=True),
                                jnp.max(s_n, axis=-1, keepdims=True)), sink)
    e_c = jnp.exp(s_c - m)
    e_n = jnp.exp(s_n - m)
    denom = (jnp.sum(e_c, axis=-1, keepdims=True) + jnp.sum(e_n, axis=-1, keepdims=True)
             + jnp.exp(sink - m))
    o = jnp.einsum('bqk,bdk->bqd', e_c.astype(BF16).reshape(bt, N_STACK * T, WINDOW),
                   cv.astype(BF16), preferred_element_type=F32).reshape(R, LANES)
    o = (o + _dot(e_n.astype(BF16), v_new.astype(BF16))) / denom
    o3 = o.reshape(bt, N_STACK * T, LANES)
    for col in range(ATT_GROUP):
        lo_h = o3[:, (2 * col) * T:(2 * col + 1) * T, :]
        hi_h = o3[:, (2 * col + 1) * T:(2 * col + 2) * T, :]
        mix_scr[:, col * LANES:(col + 1) * LANES] = jnp.where(
            low3, lo_h, hi_h).reshape(L, LANES).astype(BF16)

    keep = lax.broadcasted_iota(jnp.int32, (ATT_KV_W, WINDOW), 1) < WINDOW - T
    k_new_t = k_new.T
    v_new_t = v_new.T
    def roll_caches():
        for q in range(bt):
            shift = (WINDOW - T - q * T) % WINDOW
            nk_ref[q] = jnp.where(keep, pltpu.roll(ck_ref[q], WINDOW - T, axis=1),
                                  pltpu.roll(k_new_t, shift, axis=1))
            nv_ref[q] = jnp.where(keep, pltpu.roll(cv_ref[q], WINDOW - T, axis=1),
                                  pltpu.roll(v_new_t, shift, axis=1))
            if (q + 1) % (bt // ML_HEADS) == 0:
                yield

    rolls = roll_caches()

    r = lax.broadcasted_iota(jnp.int32, (L, L), 0)
    c = lax.broadcasted_iota(jnp.int32, (L, L), 1)
    seg = (r // T == c // T) & (r <= c)
    seg_bias = jnp.where(seg, 0.0, -jnp.inf)
    seg_bf = seg.astype(F32).astype(BF16)
    gates = gt_ref[...] * LOG2_E
    cum_row = jnp.zeros(gates.shape, F32)
    for part in _split3(gates):
        cum_row = cum_row + _dot(part, seg_bf)
    ig_rows = gates[:ML_HEADS]
    b_rows = cum_row[ML_HEADS:]
    gate_cols = jnp.concatenate([ig_rows, b_rows, jnp.zeros((LANES - N_GATES, L), F32)],
                                axis=0).T

    def col_to_row(x_col):
        return jnp.broadcast_to(x_col, (L, LANES)).T[0:1, :]

    ones_rows = (r[:LANES] == 0).astype(F32).astype(BF16)
    qm = qm_ref[...]
    km = km_ref[...]
    qm_f = qm.astype(F32)
    km_f = km.astype(F32)
    n_rep = bt * ML_QK_DIM // LANES
    bd_row = lax.broadcasted_iota(jnp.int32, (L, bt * ML_QK_DIM), 0) // T
    bd_lane = lax.broadcasted_iota(jnp.int32, (L, bt * ML_QK_DIM), 1) // ML_QK_DIM
    block_diag = bd_row == bd_lane

    def spread(x_pair, e):
        other = pltpu.roll(x_pair, ML_QK_DIM, axis=1)
        twice = jnp.where(low, x_pair, other) if e == 0 else jnp.where(low, other, x_pair)
        return jnp.where(block_diag, jnp.concatenate([twice] * n_rep, axis=1), 0.0).astype(BF16)

    def head_stages(h):
        p, e = divmod(h, 2)
        qc = qm[:, p * LANES:(p + 1) * LANES]
        k_pair = km[:, p * LANES:(p + 1) * LANES]
        zero = jnp.zeros_like(qc)
        q_pad = jnp.where(low, qc, zero) if e == 0 else jnp.where(low, zero, qc)
        v_h = vm_ref[:, h * ML_V_DIM:(h + 1) * ML_V_DIM]
        v_ext_t = jnp.concatenate([v_h.astype(F32).T.astype(BF16), ones_rows], axis=0)
        ig_c = gate_cols[:, h:h + 1]
        b_c = gate_cols[:, ML_HEADS + h:ML_HEADS + h + 1]
        b_r = b_rows[h:h + 1, :]
        yield
        m0 = m0_ref[:, :, h:h + 1] * LOG2_E
        inter = b_r + col_to_row(jnp.broadcast_to(m0, (bt, T, 1)).reshape(L, 1))
        dm = (b_r + (ig_c - b_c)) + seg_bias
        m_row = jnp.maximum(inter, jnp.max(dm, axis=0, keepdims=True))
        w_inter = jnp.exp2(inter - m_row)
        yield
        p_t = (_dot_nt(k_pair, q_pad) * jnp.exp2(dm - m_row)).astype(BF16)
        num_t = _dot(v_ext_t, p_t)
        yield
        q_h3 = qm_f[:, h * ML_QK_DIM:(h + 1) * ML_QK_DIM].reshape(bt, T, ML_QK_DIM)
        k_h3 = km_f[:, h * ML_QK_DIM:(h + 1) * ML_QK_DIM].reshape(bt, T, ML_QK_DIM)
        c0 = c0_ref[:, h]
        n0 = n0_ref[:, h:h + 1, :]
        q_c_t = _dot(spread(qm_f[:, p * LANES:(p + 1) * LANES], e),
                     c0.astype(BF16).reshape(bt * ML_QK_DIM, ML_V_DIM)).T
        q_n_r = col_to_row(jnp.sum(q_h3 * n0, axis=-1, keepdims=True).reshape(L, 1))
        yield
        num =num_t[:ML_V_DIM] + w_inter * q_c_t
        den = num_t[ML_V_DIM:ML_V_DIM + 1] + w_inter * q_n_r
        hh = num * (1.0 / jnp.maximum(jnp.abs(den), jnp.exp2(-m_row)))
        ms = jnp.mean(hh * hh, axis=0, keepdims=True)
        yield
        mix_scr[:, ATT_Q_W + h * ML_V_DIM:ATT_Q_W + (h + 1) * ML_V_DIM] = (
            (hh * lax.rsqrt(ms + EPS)).T * mlnw_ref[:, h * ML_V_DIM:(h + 1) * ML_V_DIM]
            * _sigmoid(om_ref[:, h * ML_V_DIM:(h + 1) * ML_V_DIM].astype(F32))).astype(BF16)
        yield
        b3 = b_c.reshape(bt, T, 1)
        b_last = b3[:, T - 1:T, :]
        a3 = b_last - b3 + ig_c.reshape(bt, T, 1)
        m_new = jnp.maximum(b_last + m0, jnp.max(a3, axis=1, keepdims=True))
        sc = jnp.exp2(b_last + m0 - m_new)
        ws = jnp.exp2(a3 - m_new)
        yield
        kw = spread(km_f[:, p * LANES:(p + 1) * LANES] * ws.reshape(L, 1), e)
        d_c = lax.dot_general(kw, v_h, (((0,), (0,)), ((), ())), preferred_element_type=F32)
        c_ref[:, h] = sc * c0 + d_c.reshape(bt, ML_QK_DIM, ML_V_DIM)
        n_ref[:, h:h + 1, :] = sc * n0 + jnp.sum(ws * k_h3, axis=1, keepdims=True)
        m_ref[:, :, h:h + 1] = m_new * (1.0 / LOG2_E)
        yield

    for _ in zip(*[head_stages(h) for h in range(ML_HEADS)]):
        next(rolls, None)
    for _ in rolls:
        pass

    x1_ref[...] = x_ref[...] + _dot(mix_scr[...], wperm_scr[...])


def _sample_mixer(nb, t_len, sinks, qa, kv, ck, cv, qm, km, vm, om, gt, c0, n0, m0, x2d, wout, mlnw):
    bt = SAMPLE_BT
    tl = bt * t_len
    row = lambda w: pl.BlockSpec((tl, w), lambda i: (i, 0))
    full = lambda a: pl.BlockSpec(a.shape, lambda i: (0,) * a.ndim)
    once = lambda a: pl.BlockSpec(a.shape, lambda i: (0,) * a.ndim, pipeline_mode=pl.Buffered(1))
    cache = pl.BlockSpec((bt, ATT_KV_W, WINDOW), lambda i: (i, 0, 0))
    c_spec = pl.BlockSpec((bt, ML_HEADS, ML_QK_DIM, ML_V_DIM), lambda i: (i, 0, 0, 0))
    n_spec = pl.BlockSpec((bt, ML_HEADS, ML_QK_DIM), lambda i: (i, 0, 0))
    m_spec = pl.BlockSpec((bt, 1, ML_HEADS), lambda i: (i, 0, 0))
    return pl.pallas_call(
        functools.partial(_sample_mixer_kernel, t_len),
        grid=(nb // bt,),
        in_specs=[pl.BlockSpec(memory_space=pltpu.SMEM),
                  row(ATT_Q_W), row(2 * ATT_KV_W), cache, cache, row(ML_QK_W), row(ML_QK_W),
                  row(ML_V_W), row(ML_V_W), pl.BlockSpec((N_GATES, tl), lambda i: (0, i)),
                  c_spec, n_spec, m_spec, row(D_MODEL), once(wout), full(mlnw)],
        out_specs=[row(D_MODEL), cache, cache, c_spec, n_spec, m_spec],
        out_shape=[jax.ShapeDtypeStruct((nb * t_len, D_MODEL), F32),
                   jax.ShapeDtypeStruct((nb, ATT_KV_W, WINDOW), F32),
                   jax.ShapeDtypeStruct((nb, ATT_KV_W, WINDOW), F32),
                   jax.ShapeDtypeStruct((nb, ML_HEADS, ML_QK_DIM, ML_V_DIM), F32),
                   jax.ShapeDtypeStruct((nb, ML_HEADS, ML_QK_DIM), F32),
                   jax.ShapeDtypeStruct((nb, 1, ML_HEADS), F32)],
        scratch_shapes=[pltpu.VMEM((tl, D_MODEL), BF16),
                        pltpu.VMEM((D_MODEL, D_MODEL), BF16)],
        compiler_params=pltpu.CompilerParams(dimension_semantics=("arbitrary",),
                                             vmem_limit_bytes=VMEM_LIMIT),
        name="sample_mixer",
    )(sinks, qa, kv, ck, cv, qm, km, vm, om, gt, c0, n0, m0, x2d, wout, mlnw)


def _ffn_kernel(seq_rows, *refs):
    if seq_rows is None:
        (x_ref, nw_ref, w_ref, cw_ref, cb_ref, wd_ref, y_ref, conv_ref,
         gbuf, act_scr, carry) = refs
        hist_ref = None
    else:
        (x_ref, hist_ref, nw_ref, w_ref, cw_ref, cb_ref, wd_ref, y_ref, conv_ref,
         gbuf, act_scr) = refs
        carry = None
    tm = x_ref.shape[0]
    tf = FF_CHUNK
    n_hist = CONV_W - 1
    rows = tm if seq_rows is None else seq_rows
    nseq = tm // rows
    base = SUBLANES
    n_chunks = D_FF // tf

    if carry is not None:
        @pl.when(pl.program_id(1) == 0)
        def _():
            carry[...] = jnp.zeros(carry.shape, F32)

    x = x_ref[...]
    h2 = _rms(x, nw_ref[...]).astype(BF16)

    def proj(f):
        return (_dot(h2, w_ref[:, f * tf:(f + 1) * tf]),
                _dot(h2, w_ref[:, D_FF + f * tf:D_FF + (f + 1) * tf]))

    nxt = proj(0)
    for f in range(n_chunks):
        g, u = nxt
        if f + 1 < n_chunks:
            nxt = proj(f + 1)
        cols = slice(f * tf, (f + 1) * tf)
        s = f % 2
        g3 = g.reshape(nseq, rows, tf)
        if seq_rows is None:
            gbuf[s, :, base - n_hist:base, :] = carry[:, SUBLANES - n_hist:, cols]
            carry[:, SUBLANES - n_hist:, cols] = g3[:, rows - n_hist:, :]
        else:
            gbuf[s, :, base - n_hist:base, :] = hist_ref[:, :, cols]
            conv_ref[:, :, cols] = g3[:, rows - n_hist:, :]
        gbuf[s, :, base:base + rows, :] = g3
        gc = cb_ref[:, cols] + g * cw_ref[CONV_W - 1:CONV_W, cols]
        for d in range(1, CONV_W):
            gm = gbuf[s, :, base - d:base - d + rows, :].reshape(tm, tf)
            gc = gc + gm * cw_ref[CONV_W - 1 - d:CONV_W - d, cols]
        act_scr[:, cols] = (gc * _sigmoid(gc) * u).astype(BF16)
    y_ref[...] = x + _dot(act_scr[...], wd_ref[...])

    if carry is not None:
        @pl.when(pl.program_id(1) == pl.num_programs(1) - 1)
        def _():
            conv_ref[...] = carry[:, SUBLANES - n_hist:, :]


def _ffn_scratch(tm, rows):
    return [pltpu.VMEM((2, tm // rows, SUBLANES + rows, FF_CHUNK), F32),
            pltpu.VMEM((tm, D_FF), BF16)]


def _ffn_prompt(batch, seq, x2d, nw, w, cw, cb, wd):
    tm = FFN_TILE
    nt = seq // tm
    full = lambda a: pl.BlockSpec(a.shape, lambda b, i: (0,) * a.ndim)
    once = lambda a: pl.BlockSpec(a.shape, lambda b, i: (0,) * a.ndim,
                                  pipeline_mode=pl.Buffered(1))
    row = pl.BlockSpec((tm, D_MODEL), lambda b, i: (b * nt + i, 0))
    return pl.pallas_call(
        functools.partial(_ffn_kernel, None),
        grid=(batch, nt),
        in_specs=[row, full(nw), once(w), full(cw), full(cb), once(wd)],
        out_specs=[row, pl.BlockSpec((1, CONV_W - 1, D_FF), lambda b, i: (b, 0, 0))],
        out_shape=[jax.ShapeDtypeStruct((batch * seq, D_MODEL), F32),
                   jax.ShapeDtypeStruct((batch, CONV_W - 1, D_FF), F32)],
        scratch_shapes=_ffn_scratch(tm, tm) + [pltpu.VMEM((1, SUBLANES, D_FF), F32)],
        compiler_params=pltpu.CompilerParams(dimension_semantics=("arbitrary", "arbitrary"),
                                             vmem_limit_bytes=VMEM_LIMIT),
        name="ffn_prompt",
    )(x2d, nw, w, cw, cb, wd)


def _ffn_sample(nb, t_len, x2d, hist, nw, w, cw, cb, wd):
    tm = ROW_TILE
    bt = tm // t_len
    full = lambda a: pl.BlockSpec(a.shape, lambda i: (0,) * a.ndim)
    once = lambda a: pl.BlockSpec(a.shape, lambda i: (0,) * a.ndim, pipeline_mode=pl.Buffered(1))
    row = pl.BlockSpec((tm, D_MODEL), lambda i: (i, 0))
    hist_spec = pl.BlockSpec((bt, CONV_W - 1, D_FF), lambda i: (i, 0, 0))
    return pl.pallas_call(
        functools.partial(_ffn_kernel, t_len),
        grid=(nb // bt,),
        in_specs=[row, hist_spec, full(nw), once(w), full(cw), full(cb), once(wd)],
        out_specs=[row, hist_spec],
        out_shape=[jax.ShapeDtypeStruct((nb * t_len, D_MODEL), F32),
                   jax.ShapeDtypeStruct((nb, CONV_W - 1, D_FF), F32)],
        scratch_shapes=_ffn_scratch(tm, t_len),
        compiler_params=pltpu.CompilerParams(dimension_semantics=("arbitrary",),
                                             vmem_limit_bytes=VMEM_LIMIT),
        name="ffn_sample",
    )(x2d, hist, nw, w, cw, cb, wd)


def _head_mean_matrix(width, head_dim):
    idx = np.arange(width) // head_dim
    return jnp.asarray((idx[:, None] == idx[None, :]).astype(np.float32) / head_dim, dtype=BF16)


def _layer_weights(norm_mix_w, w_in, b_gates, q_norm_w, k_norm_w, sinks, ml_norm_w, w_out,
                   norm_ffn_w, w_ffn_in, conv_w, conv_b, w_down):
    w_in_t = jnp.pad(w_in.T.astype(BF16), ((0, IN_WIDTH_PAD - w_in.shape[1]), (0, 0)))
    return dict(
        nw=norm_mix_w.reshape(1, D_MODEL),
        w_in_t=w_in_t,
        bg=jnp.pad(b_gates, (0, LANES - N_GATES)).reshape(1, LANES),
        qnw=(jnp.tile(q_norm_w, ATT_HEADS) * ATT_SCALE).reshape(1, ATT_Q_W),
        knw=jnp.tile(k_norm_w, ATT_KV_HEADS).reshape(1, ATT_KV_W),
        gq=_head_mean_matrix(ATT_Q_W, ATT_HEAD_DIM),
        gk=_head_mean_matrix(ATT_KV_W, ATT_HEAD_DIM),
        bg_col=b_gates.reshape(N_GATES, 1),
        qnw_col=(jnp.tile(q_norm_w, ATT_HEADS) * (ATT_SCALE * LOG2_E)).reshape(ATT_Q_W, 1),
        knw_col=jnp.tile(k_norm_w, ATT_KV_HEADS).reshape(ATT_KV_W, 1),
        mlnw_col=ml_norm_w.reshape(ML_V_W, 1),
        sinks=sinks,
        mlnw=ml_norm_w.reshape(1, ML_V_W),
        wout=w_out.astype(BF16),
        nfw=norm_ffn_w.reshape(1, D_MODEL),
        wff=w_ffn_in.astype(BF16),
        cw=conv_w,
        cb=conv_b.reshape(1, D_FF),
        wd=w_down.astype(BF16),
    )


def _cache_from_t(a_t):
    n = a_t.shape[0]
    return jnp.transpose(a_t.reshape(n, ATT_KV_HEADS, ATT_HEAD_DIM, WINDOW), (0, 3, 1, 2))


def _cache_to_t(a):
    n = a.shape[0]
    return jnp.transpose(a, (0, 2, 3, 1)).reshape(n, ATT_KV_W, WINDOW)


def _prompt_layer(x, w):
    batch, seq, _ = x.shape
    assert seq % MIX_TILE == 0 and MIX_TILE % ML_CHUNK == 0 and ML_CHUNK % WINDOW == 0
    assert seq % FFN_TILE == 0
    x2d = x.reshape(batch * seq, D_MODEL)
    qa, ks, kv, qm, km, vm, om, gt = _inproj_t(x2d, w["nw"], w["w_in_t"], w["bg_col"],
                                               w["qnw_col"], w["knw_col"])
    x1, c_t, n_row, m, k_t, v_t = _prompt_mixer_t(batch, seq, w["sinks"], qa, ks, kv, qm, km, vm,
                                                  om, gt, x2d, w["wout"], w["mlnw_col"])
    y, conv = _ffn_prompt(batch, seq, x1, w["nfw"], w["wff"], w["cw"], w["cb"], w["wd"])
    return (y.reshape(batch, seq, D_MODEL), _cache_from_t(k_t), _cache_from_t(v_t),
            jnp.swapaxes(c_t, -1, -2), n_row.reshape(batch, ML_HEADS, ML_QK_DIM),
            m.reshape(batch, ML_HEADS), conv)


def _sample_layer(x, ck, cv, c0, n0, m0, conv_buf, w):
    nb, t_len, _ = x.shape
    assert t_len == SUBLANES and SAMPLE_BT * t_len == LANES and nb % SAMPLE_BT == 0
    assert (nb * t_len) % ROW_TILE == 0
    x2d = x.reshape(nb * t_len, D_MODEL)
    qa, kv, qm, km, vm, om, gt = _inproj(x2d, w["nw"], w["w_in_t"], w["bg"], w["qnw"], w["knw"],
                                         w["gq"], w["gk"])
    x1, nk_t, nv_t, c_t, n, m = _sample_mixer(
        nb, t_len, w["sinks"], qa, kv, _cache_to_t(ck), _cache_to_t(cv), qm, km, vm, om, gt,
        jnp.swapaxes(c0, -1, -2), n0, m0.reshape(nb, 1, ML_HEADS), x2d, w["wout"], w["mlnw"])
    y, conv = _ffn_sample(nb, t_len, x1, conv_buf, w["nfw"], w["wff"], w["cw"], w["cb"],
                          w["wd"])
    return (y.reshape(nb, t_len, D_MODEL), _cache_from_t(nk_t), _cache_from_t(nv_t),
            jnp.swapaxes(c_t, -1, -2), n, m.reshape(nb, ML_HEADS), conv)


def kernel(x_prompt, x_sample, cache_attn_k, cache_attn_v, state_mlstm_C, state_mlstm_n,
           state_mlstm_m, cache_ffn_conv, norm_mix_w, w_in, b_gates, q_norm_w, k_norm_w,
           sinks, ml_norm_w, w_out, norm_ffn_w, w_ffn_in, conv_w, conv_b, w_down):
    depth = w_in.shape[0]
    yp, ys = x_prompt, x_sample
    sp = [[] for _ in range(6)]
    ss = [[] for _ in range(6)]
    for l in range(depth):
        w = _layer_weights(norm_mix_w[l], w_in[l], b_gates[l], q_norm_w[l], k_norm_w[l], sinks[l],
                           ml_norm_w[l], w_out[l], norm_ffn_w[l], w_ffn_in[l], conv_w[l],
                           conv_b[l], w_down[l])
        yp, *st_p = _prompt_layer(yp, w)
        ys, *st_s = _sample_layer(ys, cache_attn_k[l], cache_attn_v[l], state_mlstm_C[l],
                                  state_mlstm_n[l], state_mlstm_m[l], cache_ffn_conv[l], w)
        for i in range(6):
            sp[i].append(st_p[i])
            ss[i].append(st_s[i])
    k_p, v_p, c_p, n_p, m_p, conv_p = [jnp.stack(a) for a in sp]
    k_s, v_s, c_s, n_s, m_s, conv_s = [jnp.stack(a) for a in ss]
    return (yp, ys, k_p, v_p, c_p, n_p, m_p, conv_p, k_s, v_s, c_s, n_s, m_s, conv_s)
```

```python
import functools

import numpy as np
import jax
import jax.numpy as jnp
from jax import lax
from jax.experimental import pallas as pl
from jax.experimental.pallas import tpu as pltpu

F32 = jnp.float32
BF16 = jnp.bfloat16

D_MODEL = 1024
ATT_HEADS = 8
ATT_KV_HEADS = 2
ATT_HEAD_DIM = 64
ATT_GROUP = ATT_HEADS // ATT_KV_HEADS
WINDOW = 128
ML_HEADS = 4
ML_V_DIM = 128
ML_QK_DIM = 64
D_FF = 2816
CONV_W = 3
EPS = 1e-6
ATT_SCALE = ATT_HEAD_DIM ** -0.5
ML_SCALE = ML_QK_DIM ** -0.5
LOG2_E = 1.4426950408889634

ATT_Q_W = ATT_HEADS * ATT_HEAD_DIM
ATT_KV_W = ATT_KV_HEADS * ATT_HEAD_DIM
ML_QK_W = ML_HEADS * ML_QK_DIM
ML_V_W = ML_HEADS * ML_V_DIM
N_GATES = 2 * ML_HEADS
N_STACK = 2 * ATT_GROUP

LANES = 128
SUBLANES = 8

OFF_QA = 0
OFF_KV = OFF_QA + ATT_Q_W
OFF_QM = OFF_KV + 2 * ATT_KV_W
OFF_KM = OFF_QM + ML_QK_W
OFF_VM = OFF_KM + ML_QK_W
OFF_OM = OFF_VM + ML_V_W
OFF_GL = OFF_OM + ML_V_W
IN_WIDTH_PAD = OFF_GL + LANES

ATT_HEAD_ORDER = tuple(h for c in range(ATT_GROUP) for h in (c, c + ATT_GROUP))

ROW_TILE = 512
FFN_TILE = 1024
INPROJ_SUB = 256
NORM_ROWS = 64
INPROJ_TILE = 1024
MIX_TILE = 512
ML_CHUNK = 256
OUT_COLS = 256
FF_CHUNK = 256
SAMPLE_BT = 16
VMEM_LIMIT = 56 * 1024 * 1024


def _dot(a, b):
    return jnp.dot(a, b, preferred_element_type=F32)


def _dot_nt(a, b):
    return lax.dot_general(a, b, (((1,), (1,)), ((), ())), preferred_element_type=F32)


def _split3(x):
    hi = x.astype(BF16)
    r1 = x - hi.astype(F32)
    mid = r1.astype(BF16)
    lo = (r1 - mid.astype(F32)).astype(BF16)
    return hi, mid, lo


def _rms(x, w):
    ms = jnp.mean(x * x, axis=-1, keepdims=True)
    return x * lax.rsqrt(ms + EPS) * w


def _log_sigmoid(x):
    return jnp.minimum(x, 0.0) - jnp.log1p(jnp.exp(-jnp.abs(x)))


def _sigmoid(x):
    return 1.0 / (1.0 + jnp.exp(-x))


def _permute_head_rows(dst_ref, src_ref):
    for k, h in enumerate(ATT_HEAD_ORDER):
        dst_ref[k * ATT_HEAD_DIM:(k + 1) * ATT_HEAD_DIM, :] = (
            src_ref[h * ATT_HEAD_DIM:(h + 1) * ATT_HEAD_DIM, :])


def _inproj_kernel(x_ref, nw_ref, w_ref, bg_ref, qnw_ref, knw_ref, gq_ref, gk_ref,
                   qa_ref, kv_ref, qm_ref, km_ref, vm_ref, om_ref, gt_ref, wq_scr):
    @pl.when(pl.program_id(0) == 0)
    def _():
        _permute_head_rows(wq_scr, w_ref)

    h = _rms(x_ref[...], nw_ref[...]).astype(BF16)

    def proj(lo, width):
        return _dot_nt(h, w_ref[lo:lo + width, :])

    q = _dot_nt(h, wq_scr[...])
    q_ms = _dot((q * q).astype(BF16), gq_ref[...])
    qa_ref[...] = (q * lax.rsqrt(q_ms + EPS) * qnw_ref[...]).astype(BF16)

    kv = proj(OFF_KV, 2 * ATT_KV_W)
    k = kv[:, :ATT_KV_W]
    k_ms = _dot((k * k).astype(BF16), gk_ref[...])
    kv_ref[:, :ATT_KV_W] = k * lax.rsqrt(k_ms + EPS) * knw_ref[...]
    kv_ref[:, ATT_KV_W:] = kv[:, ATT_KV_W:]

    qm_ref[...] = (proj(OFF_QM, ML_QK_W) * ML_SCALE).astype(BF16)
    km_ref[...] = proj(OFF_KM, ML_QK_W).astype(BF16)
    vm_ref[...] = proj(OFF_VM, ML_V_W).astype(BF16)
    om_ref[...] = proj(OFF_OM, ML_V_W).astype(BF16)

    gl = proj(OFF_GL, LANES) + bg_ref[...]
    lane = lax.broadcasted_iota(jnp.int32, gl.shape, 1)
    g = jnp.where(lane < ML_HEADS, gl, _log_sigmoid(gl))
    gt_ref[...] = g.T[:N_GATES, :]


def _inproj(x2d, nw, w_in_t, bg, qnw, knw, gq, gk):
    n = x2d.shape[0]
    tm = ROW_TILE
    row = lambda w: pl.BlockSpec((tm, w), lambda i: (i, 0))
    full = lambda a: pl.BlockSpec(a.shape, lambda i: (0,) * a.ndim)
    once = lambda a: pl.BlockSpec(a.shape, lambda i: (0,) * a.ndim, pipeline_mode=pl.Buffered(1))
    return pl.pallas_call(
        _inproj_kernel,
        grid=(n // tm,),
        in_specs=[row(D_MODEL), full(nw), once(w_in_t), full(bg), full(qnw), full(knw),
                  full(gq), full(gk)],
        out_specs=[row(ATT_Q_W), row(2 * ATT_KV_W), row(ML_QK_W), row(ML_QK_W),
                   row(ML_V_W), row(ML_V_W), pl.BlockSpec((N_GATES, tm), lambda i: (0, i))],
        out_shape=[jax.ShapeDtypeStruct((n, ATT_Q_W), BF16),
                   jax.ShapeDtypeStruct((n, 2 * ATT_KV_W), F32),
                   jax.ShapeDtypeStruct((n, ML_QK_W), BF16),
                   jax.ShapeDtypeStruct((n, ML_QK_W), BF16),
                   jax.ShapeDtypeStruct((n, ML_V_W), BF16),
                   jax.ShapeDtypeStruct((n, ML_V_W), BF16),
                   jax.ShapeDtypeStruct((N_GATES, n), F32)],
        scratch_shapes=[pltpu.VMEM((ATT_Q_W, D_MODEL), BF16)],
        compiler_params=pltpu.CompilerParams(dimension_semantics=("arbitrary",),
                                             vmem_limit_bytes=VMEM_LIMIT),
        name="inproj",
    )(x2d, nw, w_in_t, bg, qnw, knw, gq, gk)


def _head_norm_t(z, head_dim, w_col):
    rows, tokens = z.shape
    z3 = z.reshape(rows // head_dim, head_dim, tokens)
    ms = jnp.mean(z3 * z3, axis=1, keepdims=True)
    return (z3 * lax.rsqrt(ms + EPS)).reshape(rows, tokens) * w_col


def _inproj_t_kernel(x_ref, nw_ref, w_ref, bg_ref, qnw_ref, knw_ref,
                     qa_ref, ks_ref, kv_ref, qm_ref, km_ref, vm_ref, om_ref, gt_ref, h_scr):
    tm = x_ref.shape[0]
    sub = INPROJ_SUB

    def norm_rows(c):
        for r0 in range(c * sub, (c + 1) * sub, NORM_ROWS):
            rows = slice(r0, r0 + NORM_ROWS)
            h_scr[rows, :] = _rms(x_ref[rows, :], nw_ref[...]).astype(BF16)
            yield

    def project(c):
        tok = slice(c * sub, (c + 1) * sub)
        h = h_scr[tok, :]

        def proj(lo, width):
            return _dot_nt(w_ref[lo:lo + width, :], h)

        qa_ref[:, tok] = _head_norm_t(proj(OFF_QA, ATT_Q_W), ATT_HEAD_DIM,
                                      qnw_ref[...]).astype(BF16)
        yield
        kv = proj(OFF_KV, 2 * ATT_KV_W)
        k = _head_norm_t(kv[:ATT_KV_W], ATT_HEAD_DIM, knw_ref[...])
        kv_ref[:ATT_KV_W, tok] = k
        kv_ref[ATT_KV_W:, tok] = kv[ATT_KV_W:]
        ks_ref[tok, :] = k.T.astype(BF16)
        qm_ref[:, tok] = (proj(OFF_QM, ML_QK_W) * ML_SCALE).astype(BF16)
        yield
        km_ref[:, tok] = proj(OFF_KM, ML_QK_W).astype(BF16)
        vm_ref[:, tok] = proj(OFF_VM, ML_V_W).astype(BF16)
        yield
        om_ref[:, tok] = proj(OFF_OM, ML_V_W).astype(BF16)
        gl = proj(OFF_GL, 2 * SUBLANES)[:N_GATES] + bg_ref[...]
        row = lax.broadcasted_iota(jnp.int32, gl.shape, 0)
        gt_ref[:, tok] = jnp.where(row < ML_HEADS, gl, _log_sigmoid(gl))
        yield

    for _ in norm_rows(0):
        pass
    for c in range(tm // sub):
        norms = norm_rows(c + 1) if c + 1 < tm // sub else iter(())
        for _ in project(c):
            next(norms, None)
        for _ in norms:
            pass


def _inproj_t(x2d, nw, w_in_t, bg_col, qnw_col, knw_col):
    n = x2d.shape[0]
    tm = INPROJ_TILE
    full = lambda a: pl.BlockSpec(a.shape, lambda i: (0,) * a.ndim)
    once = lambda a: pl.BlockSpec(a.shape, lambda i: (0,) * a.ndim, pipeline_mode=pl.Buffered(1))
    col = lambda w: pl.BlockSpec((None, w, tm), lambda i: (i, 0, 0))
    slab = lambda w, dt: jax.ShapeDtypeStruct((n // tm, w, tm), dt)
    return pl.pallas_call(
        _inproj_t_kernel,
        grid=(n // tm,),
        in_specs=[pl.BlockSpec((tm, D_MODEL), lambda i: (i, 0)), full(nw), once(w_in_t),
                  full(bg_col), full(qnw_col), full(knw_col)],
        out_specs=[col(ATT_Q_W), pl.BlockSpec((tm, ATT_KV_W), lambda i: (i, 0)),
                   col(2 * ATT_KV_W), col(ML_QK_W), col(ML_QK_W), col(ML_V_W), col(ML_V_W),
                   col(N_GATES)],
        out_shape=[slab(ATT_Q_W, BF16),
                   jax.ShapeDtypeStruct((n, ATT_KV_W), BF16),
                   slab(2 * ATT_KV_W, F32), slab(ML_QK_W, BF16), slab(ML_QK_W, BF16),
                   slab(ML_V_W, BF16), slab(ML_V_W, BF16), slab(N_GATES, F32)],
        scratch_shapes=[pltpu.VMEM((tm, D_MODEL), BF16)],
        compiler_params=pltpu.CompilerParams(dimension_semantics=("arbitrary",),
                                             vmem_limit_bytes=VMEM_LIMIT),
        name="inproj_t",
    )(x2d, nw, w_in_t, bg_col, qnw_col, knw_col)


def _gate_forms(gates, seg_mask, want_raw_col):
    L = gates.shape[1]
    m_bf = seg_mask.astype(F32).astype(BF16)
    cum_row = jnp.zeros(gates.shape, F32)
    cum_col = jnp.zeros((L, gates.shape[0]), F32)
    raw_col = None
    if want_raw_col:
        r = lax.broadcasted_iota(jnp.int32, (L, L), 0)
        c = lax.broadcasted_iota(jnp.int32, (L, L), 1)
        eye = (r == c).astype(F32).astype(BF16)
        raw_col = jnp.zeros((L, gates.shape[0]), F32)
    for part in _split3(gates):
        cum_row = cum_row + _dot_nt(part, m_bf)
        cum_col = cum_col + _dot_nt(m_bf, part)
        if want_raw_col:
            raw_col = raw_col + _dot_nt(eye, part)
    return cum_row, cum_col, raw_col


def _mlstm_intra(q_pad, k_pair, v_ext, seg_mask, b_c, b_r, ig_r, m_prev_c):
    dm = jnp.where(seg_mask, b_c + (ig_r - b_r), -jnp.inf)
    inter = b_c + m_prev_c
    m_row = jnp.maximum(inter, jnp.max(dm, axis=-1, keepdims=True))
    w_inter = jnp.exp(inter - m_row)
    p = _dot_nt(q_pad, k_pair) * jnp.exp(dm - m_row)
    return _dot(p.astype(BF16), v_ext), m_row, w_inter


def _mlstm_out(pv, m_row, w_inter, q_c, q_n, mlnw_h, om_h):
    num = pv[:, :ML_V_DIM] + w_inter * q_c
    den = pv[:, ML_V_DIM:ML_V_DIM + 1] + w_inter * q_n
    hh = num / jnp.maximum(jnp.abs(den), jnp.exp(-m_row))
    return (_rms(hh, mlnw_h) * _sigmoid(om_h.astype(F32))).astype(BF16)


def _ones_col(rows):
    lane = lax.broadcasted_iota(jnp.int32, (rows, LANES), 1)
    return (lane == 0).astype(F32).astype(BF16)


def _prompt_mixer_t_kernel(sinks_ref, qa_ref, ksc_ref, ksp_ref, kvc_ref, kvp_ref, qm_ref, km_ref,
                           vm_ref, om_ref, gt_ref, x_ref, wout_ref, mlnw_ref,
                           x1_ref, ct_ref, nrow_ref, m_ref, kt_ref, vt_ref,
                           mix_scr, state_scr, m_scr, band_scr, causal_scr, tri_scr,
                           s_scr_a, s_scr_b, e_scr):
    i = pl.program_id(1)
    A = WINDOW
    L = MIX_TILE
    C = ML_CHUNK
    n_pairs = ML_HEADS // 2

    @pl.when(i == 0)
    def _():
        state_scr[...] = jnp.zeros(state_scr.shape, F32)
        m_scr[...] = jnp.zeros(m_scr.shape, F32)
        kj = lax.broadcasted_iota(jnp.int32, (2 * A, A), 0)
        qi = lax.broadcasted_iota(jnp.int32, (2 * A, A), 1)
        band = (kj > qi) & (kj <= qi + WINDOW)
        band_scr[0] = jnp.where(band, 0.0, -jnp.inf)
        band_scr[1] = jnp.where(band & (kj >= A), 0.0, -jnp.inf)
        r = lax.broadcasted_iota(jnp.int32, (C, C), 0)
        c = lax.broadcasted_iota(jnp.int32, (C, C), 1)
        causal_scr[...] = jnp.where(r <= c, 0.0, -jnp.inf)
        tri_scr[...] = (r <= c).astype(F32).astype(BF16)

    k_all = jnp.concatenate([ksp_ref[...], ksc_ref[...]], axis=0)
    v_all = jnp.concatenate([kvp_ref[ATT_KV_W:, :], kvc_ref[ATT_KV_W:, :]], axis=1).astype(BF16)
    zero_q = jnp.zeros((ATT_HEAD_DIM, A), BF16)
    slot = 0
    s_bufs = (s_scr_a, s_scr_b)

    def stage_scores(j):
        pieces = []
        for h in range(ATT_HEADS):
            q_h = qa_ref[h * ATT_HEAD_DIM:(h + 1) * ATT_HEAD_DIM, j * A:(j + 1) * A]
            pieces.append(jnp.concatenate([q_h, zero_q] if h < ATT_GROUP else [zero_q, q_h],
                                          axis=0))
        s_bufs[j % 2][slot] = _dot(k_all[j * A:(j + 2) * A, :], jnp.concatenate(pieces, axis=1))

    def attend(j):
        cols = slice(j * A, (j + 1) * A)
        vt = v_all[:, j * A:(j + 2) * A]
        if j + 1 < L // A:
            stage_scores(j + 1)
        s_buf = s_bufs[j % 2]
        bias = jnp.where(i > 0, band_scr[0], band_scr[1]) if j == 0 else band_scr[0]
        m_rows = []
        for h in range(ATT_HEADS):
            sb = s_buf[slot, :, h * A:(h + 1) * A] + bias
            m_rows.append(jnp.maximum(jnp.max(sb, axis=0, keepdims=True),
                                      sinks_ref[h] * LOG2_E))
        inv_rows = []
        for h in range(ATT_HEADS):
            e = jnp.exp2(s_buf[slot, :, h * A:(h + 1) * A] + (bias - m_rows[h]))
            e_scr[:, h * A:(h + 1) * A] = e.astype(BF16)
            inv_rows.append(1.0 / (jnp.sum(e, axis=0, keepdims=True)
                                   + jnp.exp2(sinks_ref[h] * LOG2_E - m_rows[h])))
        o = _dot(vt, e_scr[...])
        for h in range(ATT_HEADS):
            g = h // ATT_GROUP
            mix_scr[h * ATT_HEAD_DIM:(h + 1) * ATT_HEAD_DIM, cols] = (
                o[g * ATT_HEAD_DIM:(g + 1) * ATT_HEAD_DIM, h * A:(h + 1) * A]
                * inv_rows[h]).astype(BF16)

    row128 = lax.broadcasted_iota(jnp.int32, (LANES, C), 0)
    ones_rows = (row128 == 0).astype(F32).astype(BF16)

    def mlstm_chunk(ci):
        tok = slice(ci * C, (ci + 1) * C)
        gates = gt_ref[:, tok] * LOG2_E
        cum_row = jnp.zeros(gates.shape, F32)
        for part in _split3(gates):
            cum_row = cum_row + _dot(part, tri_scr[...])
        ig_rows = gates[:ML_HEADS]
        b_rows = cum_row[ML_HEADS:]
        key_cols = jnp.concatenate([ig_rows - b_rows, jnp.zeros((LANES - ML_HEADS, C), F32)],
                                   axis=0).T
        for p in range(n_pairs):
            q_c = qm_ref[p * LANES:(p + 1) * LANES, tok]
            k_pair = km_ref[p * LANES:(p + 1) * LANES, tok]
            zero = jnp.zeros_like(q_c)
            state = state_scr[p]
            state_bf = state.astype(BF16)
            new_state = []
            for e_id in range(2):
                h = 2 * p + e_id
                v_rows = slice(h * ML_V_DIM, (h + 1) * ML_V_DIM)
                head_rows = (row128 < ML_QK_DIM) if e_id == 0 else (row128 >= ML_QK_DIM)
                q_pad = jnp.where(head_rows, q_c, zero)
                b_r = b_rows[h:h + 1, :]
                ig_r = ig_rows[h:h + 1, :]
                m_prev = m_scr[h:h + 1, 0:1]
                dm = (b_r + key_cols[:, h:h + 1]) + causal_scr[...]
                inter = b_r + m_prev
                m_row = jnp.maximum(inter, jnp.max(dm, axis=0, keepdims=True))
                w_inter = jnp.exp2(inter - m_row)
                qk = lax.dot_general(k_pair, q_pad, (((0,), (0,)), ((), ())),
                                     preferred_element_type=F32)
                p_t = (qk * jnp.exp2(dm - m_row)).astype(BF16)
                v_ext = jnp.concatenate([vm_ref[v_rows, tok], ones_rows], axis=0)
                num = _dot(v_ext, p_t) + w_inter * _dot(state_bf, q_pad)
                den = num[ML_V_DIM:ML_V_DIM + 1, :]
                hh = num[:ML_V_DIM] * (1.0 / jnp.maximum(jnp.abs(den), jnp.exp2(-m_row)))
                ms = jnp.mean(hh * hh, axis=0, keepdims=True)
                gate = _sigmoid(om_ref[v_rows, tok].astype(F32))
                mix_scr[ATT_Q_W + h * ML_V_DIM:ATT_Q_W + (h + 1) * ML_V_DIM, tok] = (
                    hh * lax.rsqrt(ms + EPS) * mlnw_ref[v_rows, :] * gate).astype(BF16)
                b_last = b_r[:, C - 1:C]
                a_r = b_last - b_r + ig_r
                m_new = jnp.maximum(b_last + m_prev, jnp.max(a_r, axis=-1, keepdims=True))
                sc = jnp.exp2(b_last + m_prev - m_new)
                wsv = (v_ext.astype(F32) * jnp.exp2(a_r - m_new)).astype(BF16)
                new_state.append(sc * state + _dot_nt(wsv, k_pair))
                m_scr[h:h + 1, :] = jnp.broadcast_to(m_new, (1, LANES))
            first = lax.broadcasted_iota(jnp.int32, state.shape, 1) < ML_QK_DIM
            state_scr[p] = jnp.where(first, new_state[0], new_state[1])
            yield

    def out_proj(ci):
        tok = slice(ci * C, (ci + 1) * C)
        mix_t = mix_scr[:, tok].T
        for n in range(D_MODEL // OUT_COLS):
            nc = slice(n * OUT_COLS, (n + 1) * OUT_COLS)
            x1_ref[tok, nc] = x_ref[tok, nc] + _dot(mix_t, wout_ref[:, nc])
            yield

    stage_scores(0)
    n_sub = C // A
    pairs = (step for ci in range(L // C) for step in mlstm_chunk(ci))
    projs = iter(())
    for j in range(L // A):
        if j and j % n_sub == 0:
            projs = out_proj(j // n_sub - 1)
        attend(j)
        next(projs, None)
        next(pairs, None)
        next(projs, None)
    for _ in pairs:
        pass
    for _ in projs:
        pass
    for _ in out_proj(L // C - 1):
        pass

    @pl.when(i == pl.num_programs(1) - 1)
    def _():
        for p in range(n_pairs):
            c_t = state_scr[p, :ML_V_DIM, :].T
            for e_id in range(2):
                ct_ref[0, 2 * p + e_id] = c_t[e_id * ML_QK_DIM:(e_id + 1) * ML_QK_DIM, :]
            nrow_ref[0, p:p + 1, :] = state_scr[p, ML_V_DIM:ML_V_DIM + 1, :]
        for h in range(ML_HEADS):
            m_ref[0, :, h:h + 1] = m_scr[h:h + 1, 0:1] * (1.0 / LOG2_E)
        kt_ref[0] = kvc_ref[:ATT_KV_W, L - WINDOW:]
        vt_ref[0] = kvc_ref[ATT_KV_W:, L - WINDOW:]


def _prompt_mixer_t(batch, seq, sinks, qa, ks, kv, qm, km, vm, om, gt, x2d, wout, mlnw_col):
    tq = MIX_TILE
    nt = seq // tq
    sub = tq // WINDOW
    per_slab = INPROJ_TILE // tq
    win_per_slab = INPROJ_TILE // WINDOW
    col = lambda w: pl.BlockSpec(
        (None, w, tq), lambda b, i: ((b * nt + i) // per_slab, 0, (b * nt + i) % per_slab))
    full = lambda a: pl.BlockSpec(a.shape, lambda b, i: (0,) * a.ndim)
    once = lambda a: pl.BlockSpec(a.shape, lambda b, i: (0,) * a.ndim,
                                  pipeline_mode=pl.Buffered(1))
    prev_block = lambda b, i: jnp.maximum((b * nt + i) * sub - 1, 0)
    per_batch = lambda *dims: pl.BlockSpec((1,) + dims, lambda b, i: (b,) + (0,) * len(dims))
    return pl.pallas_call(
        _prompt_mixer_t_kernel,
        grid=(batch, nt),
        in_specs=[pl.BlockSpec(memory_space=pltpu.SMEM),
                  col(ATT_Q_W),
                  pl.BlockSpec((tq, ATT_KV_W), lambda b, i: (b * nt + i, 0)),
                  pl.BlockSpec((WINDOW, ATT_KV_W), lambda b, i: (prev_block(b, i), 0)),
                  col(2 * ATT_KV_W),
                  pl.BlockSpec((None, 2 * ATT_KV_W, WINDOW),
                               lambda b, i: (prev_block(b, i) // win_per_slab, 0,
                                             prev_block(b, i) % win_per_slab)),
                  col(ML_QK_W), col(ML_QK_W), col(ML_V_W), col(ML_V_W), col(N_GATES),
                  pl.BlockSpec((tq, D_MODEL), lambda b, i: (b * nt + i, 0)),
                  once(wout), full(mlnw_col)],
        out_specs=[pl.BlockSpec((tq, D_MODEL), lambda b, i: (b * nt + i, 0)),
                   per_batch(ML_HEADS, ML_QK_DIM, ML_V_DIM),
                   per_batch(ML_HEADS // 2, LANES),
                   per_batch(1, ML_HEADS),
                   per_batch(ATT_KV_W, WINDOW),
                   per_batch(ATT_KV_W, WINDOW)],
        out_shape=[jax.ShapeDtypeStruct((batch * seq, D_MODEL), F32),
                   jax.ShapeDtypeStruct((batch, ML_HEADS, ML_QK_DIM, ML_V_DIM), F32),
                   jax.ShapeDtypeStruct((batch, ML_HEADS // 2, LANES), F32),
                   jax.ShapeDtypeStruct((batch, 1, ML_HEADS), F32),
                   jax.ShapeDtypeStruct((batch, ATT_KV_W, WINDOW), F32),
                   jax.ShapeDtypeStruct((batch, ATT_KV_W, WINDOW), F32)],
        scratch_shapes=[pltpu.VMEM((D_MODEL, tq), BF16),
                        pltpu.VMEM((ML_HEADS // 2, 2 * LANES, LANES), F32),
                        pltpu.VMEM((SUBLANES, LANES), F32),
                        pltpu.VMEM((2, 2 * WINDOW, WINDOW), F32),
                        pltpu.VMEM((ML_CHUNK, ML_CHUNK), F32),
                        pltpu.VMEM((ML_CHUNK, ML_CHUNK), BF16),
                        pltpu.VMEM((2, 2 * WINDOW, ATT_HEADS * WINDOW), F32),
                        pltpu.VMEM((2, 2 * WINDOW, ATT_HEADS * WINDOW), F32),
                        pltpu.VMEM((2 * WINDOW, ATT_HEADS * WINDOW), BF16)],
        compiler_params=pltpu.CompilerParams(dimension_semantics=("arbitrary", "arbitrary"),
                                             vmem_limit_bytes=VMEM_LIMIT),
        name="prompt_mixer_t",
    )(sinks, qa, ks, ks, kv, kv, qm, km, vm, om, gt, x2d, wout, mlnw_col)


def _sample_mixer_kernel(t_len, sinks_ref, qa_ref, kv_ref, ck_ref, cv_ref, qm_ref, km_ref,
                         vm_ref, om_ref, gt_ref, c0_ref, n0_ref, m0_ref, x_ref, wout_ref,
                         mlnw_ref, x1_ref, nk_ref, nv_ref, c_ref, n_ref, m_ref,
                         mix_scr, wperm_scr):
    bt = SAMPLE_BT
    T = t_len
    L = bt * T

    @pl.when(pl.program_id(0) == 0)
    def _():
        _permute_head_rows(wperm_scr, wout_ref)
        wperm_scr[ATT_Q_W:, :] = wout_ref[ATT_Q_W:, :]

    lane3 = lax.broadcasted_iota(jnp.int32, (bt, T, LANES), 2)
    low3 = lane3 < ATT_HEAD_DIM
    lane = lax.broadcasted_iota(jnp.int32, (L, LANES), 1)
    low = lane < ATT_HEAD_DIM

    qa3 = qa_ref[...].astype(F32).reshape(bt, T, ATT_Q_W)
    pieces = []
    for col in range(ATT_GROUP):
        qc = qa3[:, :, col * LANES:(col + 1) * LANES]
        pieces += [jnp.where(low3, qc, 0.0), jnp.where(low3, 0.0, qc)]
    q3 = jnp.concatenate(pieces, axis=1).astype(BF16)
    R = bt * N_STACK * T
    q2 = q3.reshape(R, LANES)
    kv_new = kv_ref[...]
    k_new = kv_new[:, :ATT_KV_W]
    v_new = kv_new[:, ATT_KV_W:]
    ck = ck_ref[...]
    cv = cv_ref[...]
    s_c = jnp.einsum('bqd,bdk->bqk', q3, ck.astype(BF16),
                     preferred_element_type=F32).reshape(R, WINDOW)
    s_n = _dot_nt(q2, k_new.astype(BF16))
    row_c = lax.broadcasted_iota(jnp.int32, (R, WINDOW), 0)
    col_c = lax.broadcasted_iota(jnp.int32, (R, WINDOW), 1)
    s_c = jnp.where(col_c > row_c % T, s_c, -jnp.inf)
    row_n = lax.broadcasted_iota(jnp.int32, (R, L), 0)
    col_n = lax.broadcasted_iota(jnp.int32, (R, L), 1)
    valid_n = (row_n // (N_STACK * T) == col_n // T) & (col_n % T <= row_n % T)
    s_n = jnp.where(valid_n, s_n, -jnp.inf)
    stack_id = (lax.broadcasted_iota(jnp.int32, (R, 1), 0) // T) % N_STACK
    sink = jnp.zeros((R, 1), F32)
    for k_id in range(N_STACK):
        sink = jnp.where(stack_id == k_id, sinks_ref[ATT_HEAD_ORDER[k_id]], sink)
    m = jnp.maximum(jnp.maximum(jnp.max(s_c, axis=-1, keepdims=True),
                                jnp.max(s_n, axis=-1, keepdims=True)), sink)
    e_c = jnp.exp(s_c - m)
    e_n = jnp.exp(s_n - m)
    denom = (jnp.sum(e_c, axis=-1, keepdims=True) + jnp.sum(e_n, axis=-1, keepdims=True)
             + jnp.exp(sink - m))
    o = jnp.einsum('bqk,bdk->bqd', e_c.astype(BF16).reshape(bt, N_STACK * T, WINDOW),
                   cv.astype(BF16), preferred_element_type=F32).reshape(R, LANES)
    o = (o + _dot(e_n.astype(BF16), v_new.astype(BF16))) / denom
    o3 = o.reshape(bt, N_STACK * T, LANES)
    for col in range(ATT_GROUP):
        lo_h = o3[:, (2 * col) * T:(2 * col + 1) * T, :]
        hi_h = o3[:, (2 * col + 1) * T:(2 * col + 2) * T, :]
        mix_scr[:, col * LANES:(col + 1) * LANES] = jnp.where(
            low3, lo_h, hi_h).reshape(L, LANES).astype(BF16)

    keep = lax.broadcasted_iota(jnp.int32, (ATT_KV_W, WINDOW), 1) < WINDOW - T
    k_new_t = k_new.T
    v_new_t = v_new.T
    def roll_caches():
        for q in range(bt):
            shift = (WINDOW - T - q * T) % WINDOW
            nk_ref[q] = jnp.where(keep, pltpu.roll(ck_ref[q], WINDOW - T, axis=1),
                                  pltpu.roll(k_new_t, shift, axis=1))
            nv_ref[q] = jnp.where(keep, pltpu.roll(cv_ref[q], WINDOW - T, axis=1),
                                  pltpu.roll(v_new_t, shift, axis=1))
            if (q + 1) % (bt // ML_HEADS) == 0:
                yield

    rolls = roll_caches()

    r = lax.broadcasted_iota(jnp.int32, (L, L), 0)
    c = lax.broadcasted_iota(jnp.int32, (L, L), 1)
    seg = (r // T == c // T) & (r <= c)
    seg_bias = jnp.where(seg, 0.0, -jnp.inf)
    seg_bf = seg.astype(F32).astype(BF16)
    gates = gt_ref[...] * LOG2_E
    cum_row = jnp.zeros(gates.shape, F32)
    for part in _split3(gates):
        cum_row = cum_row + _dot(part, seg_bf)
    ig_rows = gates[:ML_HEADS]
    b_rows = cum_row[ML_HEADS:]
    gate_cols = jnp.concatenate([ig_rows, b_rows, jnp.zeros((LANES - N_GATES, L), F32)],
                                axis=0).T

    def col_to_row(x_col):
        return jnp.broadcast_to(x_col, (L, LANES)).T[0:1, :]

    ones_rows = (r[:LANES] == 0).astype(F32).astype(BF16)
    qm = qm_ref[...]
    km = km_ref[...]
    qm_f = qm.astype(F32)
    km_f = km.astype(F32)
    n_rep = bt * ML_QK_DIM // LANES
    bd_row = lax.broadcasted_iota(jnp.int32, (L, bt * ML_QK_DIM), 0) // T
    bd_lane = lax.broadcasted_iota(jnp.int32, (L, bt * ML_QK_DIM), 1) // ML_QK_DIM
    block_diag = bd_row == bd_lane

    def spread(x_pair, e):
        other = pltpu.roll(x_pair, ML_QK_DIM, axis=1)
        twice = jnp.where(low, x_pair, other) if e == 0 else jnp.where(low, other, x_pair)
        return jnp.where(block_diag, jnp.concatenate([twice] * n_rep, axis=1), 0.0).astype(BF16)

    def head_stages(h):
        p, e = divmod(h, 2)
        qc = qm[:, p * LANES:(p + 1) * LANES]
        k_pair = km[:, p * LANES:(p + 1) * LANES]
        zero = jnp.zeros_like(qc)
        q_pad = jnp.where(low, qc, zero) if e == 0 else jnp.where(low, zero, qc)
        v_h = vm_ref[:, h * ML_V_DIM:(h + 1) * ML_V_DIM]
        v_ext_t = jnp.concatenate([v_h.astype(F32).T.astype(BF16), ones_rows], axis=0)
        ig_c = gate_cols[:, h:h + 1]
        b_c = gate_cols[:, ML_HEADS + h:ML_HEADS + h + 1]
        b_r = b_rows[h:h + 1, :]
        yield
        m0 = m0_ref[:, :, h:h + 1] * LOG2_E
        inter = b_r + col_to_row(jnp.broadcast_to(m0, (bt, T, 1)).reshape(L, 1))
        dm = (b_r + (ig_c - b_c)) + seg_bias
        m_row = jnp.maximum(inter, jnp.max(dm, axis=0, keepdims=True))
        w_inter = jnp.exp2(inter - m_row)
        yield
        p_t = (_dot_nt(k_pair, q_pad) * jnp.exp2(dm - m_row)).astype(BF16)
        num_t = _dot(v_ext_t, p_t)
        yield
        q_h3 = qm_f[:, h * ML_QK_DIM:(h + 1) * ML_QK_DIM].reshape(bt, T, ML_QK_DIM)
        k_h3 = km_f[:, h * ML_QK_DIM:(h + 1) * ML_QK_DIM].reshape(bt, T, ML_QK_DIM)
        c0 = c0_ref[:, h]
        n0 = n0_ref[:, h:h + 1, :]
        q_c_t = _dot(spread(qm_f[:, p * LANES:(p + 1) * LANES], e),
                     c0.astype(BF16).reshape(bt * ML_QK_DIM, ML_V_DIM)).T
        q_n_r = col_to_row(jnp.sum(q_h3 * n0, axis=-1, keepdims=True).reshape(L, 1))
        yield
        num =num_t[:ML_V_DIM] + w_inter * q_c_t
        den = num_t[ML_V_DIM:ML_V_DIM + 1] + w_inter * q_n_r
        hh = num * (1.0 / jnp.maximum(jnp.abs(den), jnp.exp2(-m_row)))
        ms = jnp.mean(hh * hh, axis=0, keepdims=True)
        yield
        mix_scr[:, ATT_Q_W + h * ML_V_DIM:ATT_Q_W + (h + 1) * ML_V_DIM] = (
            (hh * lax.rsqrt(ms + EPS)).T * mlnw_ref[:, h * ML_V_DIM:(h + 1) * ML_V_DIM]
            * _sigmoid(om_ref[:, h * ML_V_DIM:(h + 1) * ML_V_DIM].astype(F32))).astype(BF16)
        yield
        b3 = b_c.reshape(bt, T, 1)
        b_last = b3[:, T - 1:T, :]
        a3 = b_last - b3 + ig_c.reshape(bt, T, 1)
        m_new = jnp.maximum(b_last + m0, jnp.max(a3, axis=1, keepdims=True))
        sc = jnp.exp2(b_last + m0 - m_new)
        ws = jnp.exp2(a3 - m_new)
        yield
        kw = spread(km_f[:, p * LANES:(p + 1) * LANES] * ws.reshape(L, 1), e)
        d_c = lax.dot_general(kw, v_h, (((0,), (0,)), ((), ())), preferred_element_type=F32)
        c_ref[:, h] = sc * c0 + d_c.reshape(bt, ML_QK_DIM, ML_V_DIM)
        n_ref[:, h:h + 1, :] = sc * n0 + jnp.sum(ws * k_h3, axis=1, keepdims=True)
        m_ref[:, :, h:h + 1] = m_new * (1.0 / LOG2_E)
        yield

    for _ in zip(*[head_stages(h) for h in range(ML_HEADS)]):
        next(rolls, None)
    for _ in rolls:
        pass

    x1_ref[...] = x_ref[...] + _dot(mix_scr[...], wperm_scr[...])


def _sample_mixer(nb, t_len, sinks, qa, kv, ck, cv, qm, km, vm, om, gt, c0, n0, m0, x2d, wout, mlnw):
    bt = SAMPLE_BT
    tl = bt * t_len
    row = lambda w: pl.BlockSpec((tl, w), lambda i: (i, 0))
    full = lambda a: pl.BlockSpec(a.shape, lambda i: (0,) * a.ndim)
    once = lambda a: pl.BlockSpec(a.shape, lambda i: (0,) * a.ndim, pipeline_mode=pl.Buffered(1))
    cache = pl.BlockSpec((bt, ATT_KV_W, WINDOW), lambda i: (i, 0, 0))
    c_spec = pl.BlockSpec((bt, ML_HEADS, ML_QK_DIM, ML_V_DIM), lambda i: (i, 0, 0, 0))
    n_spec = pl.BlockSpec((bt, ML_HEADS, ML_QK_DIM), lambda i: (i, 0, 0))
    m_spec = pl.BlockSpec((bt, 1, ML_HEADS), lambda i: (i, 0, 0))
    return pl.pallas_call(
        functools.partial(_sample_mixer_kernel, t_len),
        grid=(nb // bt,),
        in_specs=[pl.BlockSpec(memory_space=pltpu.SMEM),
                  row(ATT_Q_W), row(2 * ATT_KV_W), cache, cache, row(ML_QK_W), row(ML_QK_W),
                  row(ML_V_W), row(ML_V_W), pl.BlockSpec((N_GATES, tl), lambda i: (0, i)),
                  c_spec, n_spec, m_spec, row(D_MODEL), once(wout), full(mlnw)],
        out_specs=[row(D_MODEL), cache, cache, c_spec, n_spec, m_spec],
        out_shape=[jax.ShapeDtypeStruct((nb * t_len, D_MODEL), F32),
                   jax.ShapeDtypeStruct((nb, ATT_KV_W, WINDOW), F32),
                   jax.ShapeDtypeStruct((nb, ATT_KV_W, WINDOW), F32),
                   jax.ShapeDtypeStruct((nb, ML_HEADS, ML_QK_DIM, ML_V_DIM), F32),
                   jax.ShapeDtypeStruct((nb, ML_HEADS, ML_QK_DIM), F32),
                   jax.ShapeDtypeStruct((nb, 1, ML_HEADS), F32)],
        scratch_shapes=[pltpu.VMEM((tl, D_MODEL), BF16),
                        pltpu.VMEM((D_MODEL, D_MODEL), BF16)],
        compiler_params=pltpu.CompilerParams(dimension_semantics=("arbitrary",),
                                             vmem_limit_bytes=VMEM_LIMIT),
        name="sample_mixer",
    )(sinks, qa, kv, ck, cv, qm, km, vm, om, gt, c0, n0, m0, x2d, wout, mlnw)


def _ffn_kernel(seq_rows, *refs):
    if seq_rows is None:
        (x_ref, nw_ref, w_ref, cw_ref, cb_ref, wd_ref, y_ref, conv_ref,
         gbuf, act_scr, carry) = refs
        hist_ref = None
    else:
        (x_ref, hist_ref, nw_ref, w_ref, cw_ref, cb_ref, wd_ref, y_ref, conv_ref,
         gbuf, act_scr) = refs
        carry = None
    tm = x_ref.shape[0]
    tf = FF_CHUNK
    n_hist = CONV_W - 1
    rows = tm if seq_rows is None else seq_rows
    nseq = tm // rows
    base = SUBLANES
    n_chunks = D_FF // tf

    if carry is not None:
        @pl.when(pl.program_id(1) == 0)
        def _():
            carry[...] = jnp.zeros(carry.shape, F32)

    x = x_ref[...]
    h2 = _rms(x, nw_ref[...]).astype(BF16)

    def proj(f):
        return (_dot(h2, w_ref[:, f * tf:(f + 1) * tf]),
                _dot(h2, w_ref[:, D_FF + f * tf:D_FF + (f + 1) * tf]))

    nxt = proj(0)
    for f in range(n_chunks):
        g, u = nxt
        if f + 1 < n_chunks:
            nxt = proj(f + 1)
        cols = slice(f * tf, (f + 1) * tf)
        s = f % 2
        g3 = g.reshape(nseq, rows, tf)
        if seq_rows is None:
            gbuf[s, :, base - n_hist:base, :] = carry[:, SUBLANES - n_hist:, cols]
            carry[:, SUBLANES - n_hist:, cols] = g3[:, rows - n_hist:, :]
        else:
            gbuf[s, :, base - n_hist:base, :] = hist_ref[:, :, cols]
            conv_ref[:, :, cols] = g3[:, rows - n_hist:, :]
        gbuf[s, :, base:base + rows, :] = g3
        gc = cb_ref[:, cols] + g * cw_ref[CONV_W - 1:CONV_W, cols]
        for d in range(1, CONV_W):
            gm = gbuf[s, :, base - d:base - d + rows, :].reshape(tm, tf)
            gc = gc + gm * cw_ref[CONV_W - 1 - d:CONV_W - d, cols]
        act_scr[:, cols] = (gc * _sigmoid(gc) * u).astype(BF16)
    y_ref[...] = x + _dot(act_scr[...], wd_ref[...])

    if carry is not None:
        @pl.when(pl.program_id(1) == pl.num_programs(1) - 1)
        def _():
            conv_ref[...] = carry[:, SUBLANES - n_hist:, :]


def _ffn_scratch(tm, rows):
    return [pltpu.VMEM((2, tm // rows, SUBLANES + rows, FF_CHUNK), F32),
            pltpu.VMEM((tm, D_FF), BF16)]


def _ffn_prompt(batch, seq, x2d, nw, w, cw, cb, wd):
    tm = FFN_TILE
    nt = seq // tm
    full = lambda a: pl.BlockSpec(a.shape, lambda b, i: (0,) * a.ndim)
    once = lambda a: pl.BlockSpec(a.shape, lambda b, i: (0,) * a.ndim,
                                  pipeline_mode=pl.Buffered(1))
    row = pl.BlockSpec((tm, D_MODEL), lambda b, i: (b * nt + i, 0))
    return pl.pallas_call(
        functools.partial(_ffn_kernel, None),
        grid=(batch, nt),
        in_specs=[row, full(nw), once(w), full(cw), full(cb), once(wd)],
        out_specs=[row, pl.BlockSpec((1, CONV_W - 1, D_FF), lambda b, i: (b, 0, 0))],
        out_shape=[jax.ShapeDtypeStruct((batch * seq, D_MODEL), F32),
                   jax.ShapeDtypeStruct((batch, CONV_W - 1, D_FF), F32)],
        scratch_shapes=_ffn_scratch(tm, tm) + [pltpu.VMEM((1, SUBLANES, D_FF), F32)],
        compiler_params=pltpu.CompilerParams(dimension_semantics=("arbitrary", "arbitrary"),
                                             vmem_limit_bytes=VMEM_LIMIT),
        name="ffn_prompt",
    )(x2d, nw, w, cw, cb, wd)


def _ffn_sample(nb, t_len, x2d, hist, nw, w, cw, cb, wd):
    tm = ROW_TILE
    bt = tm // t_len
    full = lambda a: pl.BlockSpec(a.shape, lambda i: (0,) * a.ndim)
    once = lambda a: pl.BlockSpec(a.shape, lambda i: (0,) * a.ndim, pipeline_mode=pl.Buffered(1))
    row = pl.BlockSpec((tm, D_MODEL), lambda i: (i, 0))
    hist_spec = pl.BlockSpec((bt, CONV_W - 1, D_FF), lambda i: (i, 0, 0))
    return pl.pallas_call(
        functools.partial(_ffn_kernel, t_len),
        grid=(nb // bt,),
        in_specs=[row, hist_spec, full(nw), once(w), full(cw), full(cb), once(wd)],
        out_specs=[row, hist_spec],
        out_shape=[jax.ShapeDtypeStruct((nb * t_len, D_MODEL), F32),
                   jax.ShapeDtypeStruct((nb, CONV_W - 1, D_FF), F32)],
        scratch_shapes=_ffn_scratch(tm, t_len),
        compiler_params=pltpu.CompilerParams(dimension_semantics=("arbitrary",),
                                             vmem_limit_bytes=VMEM_LIMIT),
        name="ffn_sample",
    )(x2d, hist, nw, w, cw, cb, wd)


def _head_mean_matrix(width, head_dim):
    idx = np.arange(width) // head_dim
    return jnp.asarray((idx[:, None] == idx[None, :]).astype(np.float32) / head_dim, dtype=BF16)


def _layer_weights(norm_mix_w, w_in, b_gates, q_norm_w, k_norm_w, sinks, ml_norm_w, w_out,
                   norm_ffn_w, w_ffn_in, conv_w, conv_b, w_down):
    w_in_t = jnp.pad(w_in.T.astype(BF16), ((0, IN_WIDTH_PAD - w_in.shape[1]), (0, 0)))
    return dict(
        nw=norm_mix_w.reshape(1, D_MODEL),
        w_in_t=w_in_t,
        bg=jnp.pad(b_gates, (0, LANES - N_GATES)).reshape(1, LANES),
        qnw=(jnp.tile(q_norm_w, ATT_HEADS) * ATT_SCALE).reshape(1, ATT_Q_W),
        knw=jnp.tile(k_norm_w, ATT_KV_HEADS).reshape(1, ATT_KV_W),
        gq=_head_mean_matrix(ATT_Q_W, ATT_HEAD_DIM),
        gk=_head_mean_matrix(ATT_KV_W, ATT_HEAD_DIM),
        bg_col=b_gates.reshape(N_GATES, 1),
        qnw_col=(jnp.tile(q_norm_w, ATT_HEADS) * (ATT_SCALE * LOG2_E)).reshape(ATT_Q_W, 1),
        knw_col=jnp.tile(k_norm_w, ATT_KV_HEADS).reshape(ATT_KV_W, 1),
        mlnw_col=ml_norm_w.reshape(ML_V_W, 1),
        sinks=sinks,
        mlnw=ml_norm_w.reshape(1, ML_V_W),
        wout=w_out.astype(BF16),
        nfw=norm_ffn_w.reshape(1, D_MODEL),
        wff=w_ffn_in.astype(BF16),
        cw=conv_w,
        cb=conv_b.reshape(1, D_FF),
        wd=w_down.astype(BF16),
    )


def _cache_from_t(a_t):
    n = a_t.shape[0]
    return jnp.transpose(a_t.reshape(n, ATT_KV_HEADS, ATT_HEAD_DIM, WINDOW), (0, 3, 1, 2))


def _cache_to_t(a):
    n = a.shape[0]
    return jnp.transpose(a, (0, 2, 3, 1)).reshape(n, ATT_KV_W, WINDOW)


def _prompt_layer(x, w):
    batch, seq, _ = x.shape
    assert seq % INPROJ_TILE == 0 and INPROJ_TILE % MIX_TILE == 0
    assert MIX_TILE % ML_CHUNK == 0 and ML_CHUNK % WINDOW == 0
    assert seq % FFN_TILE == 0
    x2d = x.reshape(batch * seq, D_MODEL)
    qa, ks, kv, qm, km, vm, om, gt = _inproj_t(x2d, w["nw"], w["w_in_t"], w["bg_col"],
                                               w["qnw_col"], w["knw_col"])
    x1, c_t, n_row, m, k_t, v_t = _prompt_mixer_t(batch, seq, w["sinks"], qa, ks, kv, qm, km, vm,
                                                  om, gt, x2d, w["wout"], w["mlnw_col"])
    y, conv = _ffn_prompt(batch, seq, x1, w["nfw"], w["wff"], w["cw"], w["cb"], w["wd"])
    return (y.reshape(batch, seq, D_MODEL), _cache_from_t(k_t), _cache_from_t(v_t),
            jnp.swapaxes(c_t, -1, -2), n_row.reshape(batch, ML_HEADS, ML_QK_DIM),
            m.reshape(batch, ML_HEADS), conv)


def _sample_layer(x, ck, cv, c0, n0, m0, conv_buf, w):
    nb, t_len, _ = x.shape
    assert t_len == SUBLANES and SAMPLE_BT * t_len == LANES and nb % SAMPLE_BT == 0
    assert (nb * t_len) % ROW_TILE == 0
    x2d = x.reshape(nb * t_len, D_MODEL)
    qa, kv, qm, km, vm, om, gt = _inproj(x2d, w["nw"], w["w_in_t"], w["bg"], w["qnw"], w["knw"],
                                         w["gq"], w["gk"])
    x1, nk_t, nv_t, c_t, n, m = _sample_mixer(
        nb, t_len, w["sinks"], qa, kv, _cache_to_t(ck), _cache_to_t(cv), qm, km, vm, om, gt,
        jnp.swapaxes(c0, -1, -2), n0, m0.reshape(nb, 1, ML_HEADS), x2d, w["wout"], w["mlnw"])
    y, conv = _ffn_sample(nb, t_len, x1, conv_buf, w["nfw"], w["wff"], w["cw"], w["cb"],
                          w["wd"])
    return (y.reshape(nb, t_len, D_MODEL), _cache_from_t(nk_t), _cache_from_t(nv_t),
            jnp.swapaxes(c_t, -1, -2), n, m.reshape(nb, ML_HEADS), conv)


def kernel(x_prompt, x_sample, cache_attn_k, cache_attn_v, state_mlstm_C, state_mlstm_n,
           state_mlstm_m, cache_ffn_conv, norm_mix_w, w_in, b_gates, q_norm_w, k_norm_w,
           sinks, ml_norm_w, w_out, norm_ffn_w, w_ffn_in, conv_w, conv_b, w_down):
    depth = w_in.shape[0]
    yp, ys = x_prompt, x_sample
    sp = [[] for _ in range(6)]
    ss = [[] for _ in range(6)]
    for l in range(depth):
        w = _layer_weights(norm_mix_w[l], w_in[l], b_gates[l], q_norm_w[l], k_norm_w[l], sinks[l],
                           ml_norm_w[l], w_out[l], norm_ffn_w[l], w_ffn_in[l], conv_w[l],
                           conv_b[l], w_down[l])
        yp, *st_p = _prompt_layer(yp, w)
        ys, *st_s = _sample_layer(ys, cache_attn_k[l], cache_attn_v[l], state_mlstm_C[l],
                                  state_mlstm_n[l], state_mlstm_m[l], cache_ffn_conv[l], w)
        for i in range(6):
            sp[i].append(st_p[i])
            ss[i].append(st_s[i])
    k_p, v_p, c_p, n_p, m_p, conv_p = [jnp.stack(a) for a in sp]
    k_s, v_s, c_s, n_s, m_s, conv_s = [jnp.stack(a) for a in ss]
    return (yp, ys, k_p, v_p, c_p, n_p, m_p, conv_p, k_s, v_s, c_s, n_s, m_s, conv_s)
```

```python
import functools

import numpy as np
import jax
import jax.numpy as jnp
from jax import lax
from jax.experimental import pallas as pl
from jax.experimental.pallas import tpu as pltpu

F32 = jnp.float32
BF16 = jnp.bfloat16

D_MODEL = 1024
ATT_HEADS = 8
ATT_KV_HEADS = 2
ATT_HEAD_DIM = 64
ATT_GROUP = ATT_HEADS // ATT_KV_HEADS
WINDOW = 128
ML_HEADS = 4
ML_V_DIM = 128
ML_QK_DIM = 64
D_FF = 2816
CONV_W = 3
EPS = 1e-6
ATT_SCALE = ATT_HEAD_DIM ** -0.5
ML_SCALE = ML_QK_DIM ** -0.5
LOG2_E = 1.4426950408889634

ATT_Q_W = ATT_HEADS * ATT_HEAD_DIM
ATT_KV_W = ATT_KV_HEADS * ATT_HEAD_DIM
ML_QK_W = ML_HEADS * ML_QK_DIM
ML_V_W = ML_HEADS * ML_V_DIM
N_GATES = 2 * ML_HEADS
N_STACK = 2 * ATT_GROUP

LANES = 128
SUBLANES = 8

OFF_QA = 0
OFF_KV = OFF_QA + ATT_Q_W
OFF_QM = OFF_KV + 2 * ATT_KV_W
OFF_KM = OFF_QM + ML_QK_W
OFF_VM = OFF_KM + ML_QK_W
OFF_OM = OFF_VM + ML_V_W
OFF_GL = OFF_OM + ML_V_W
IN_WIDTH_PAD = OFF_GL + LANES

ATT_HEAD_ORDER = tuple(h for c in range(ATT_GROUP) for h in (c, c + ATT_GROUP))

ROW_TILE = 512
FFN_TILE = 1024
INPROJ_SUB = 256
NORM_ROWS = 64
INPROJ_TILE = 2048
MIX_TILE = 512
ML_CHUNK = 256
OUT_COLS = 256
FF_CHUNK = 256
SAMPLE_BT = 16
VMEM_LIMIT = 56 * 1024 * 1024


def _dot(a, b):
    return jnp.dot(a, b, preferred_element_type=F32)


def _dot_nt(a, b):
    return lax.dot_general(a, b, (((1,), (1,)), ((), ())), preferred_element_type=F32)


def _split3(x):
    hi = x.astype(BF16)
    r1 = x - hi.astype(F32)
    mid = r1.astype(BF16)
    lo = (r1 - mid.astype(F32)).astype(BF16)
    return hi, mid, lo


def _rms(x, w):
    ms = jnp.mean(x * x, axis=-1, keepdims=True)
    return x * lax.rsqrt(ms + EPS) * w


def _log_sigmoid(x):
    return jnp.minimum(x, 0.0) - jnp.log1p(jnp.exp(-jnp.abs(x)))


def _sigmoid(x):
    return 1.0 / (1.0 + jnp.exp(-x))


def _permute_head_rows(dst_ref, src_ref):
    for k, h in enumerate(ATT_HEAD_ORDER):
        dst_ref[k * ATT_HEAD_DIM:(k + 1) * ATT_HEAD_DIM, :] = (
            src_ref[h * ATT_HEAD_DIM:(h + 1) * ATT_HEAD_DIM, :])


def _inproj_kernel(x_ref, nw_ref, w_ref, bg_ref, qnw_ref, knw_ref, gq_ref, gk_ref,
                   qa_ref, kv_ref, qm_ref, km_ref, vm_ref, om_ref, gt_ref, wq_scr):
    @pl.when(pl.program_id(0) == 0)
    def _():
        _permute_head_rows(wq_scr, w_ref)

    h = _rms(x_ref[...], nw_ref[...]).astype(BF16)

    def proj(lo, width):
        return _dot_nt(h, w_ref[lo:lo + width, :])

    q = _dot_nt(h, wq_scr[...])
    q_ms = _dot((q * q).astype(BF16), gq_ref[...])
    qa_ref[...] = (q * lax.rsqrt(q_ms + EPS) * qnw_ref[...]).astype(BF16)

    kv = proj(OFF_KV, 2 * ATT_KV_W)
    k = kv[:, :ATT_KV_W]
    k_ms = _dot((k * k).astype(BF16), gk_ref[...])
    kv_ref[:, :ATT_KV_W] = k * lax.rsqrt(k_ms + EPS) * knw_ref[...]
    kv_ref[:, ATT_KV_W:] = kv[:, ATT_KV_W:]

    qm_ref[...] = (proj(OFF_QM, ML_QK_W) * ML_SCALE).astype(BF16)
    km_ref[...] = proj(OFF_KM, ML_QK_W).astype(BF16)
    vm_ref[...] = proj(OFF_VM, ML_V_W).astype(BF16)
    om_ref[...] = proj(OFF_OM, ML_V_W).astype(BF16)

    gl = proj(OFF_GL, LANES) + bg_ref[...]
    lane = lax.broadcasted_iota(jnp.int32, gl.shape, 1)
    g = jnp.where(lane < ML_HEADS, gl, _log_sigmoid(gl))
    gt_ref[...] = g.T[:N_GATES, :]


def _inproj(x2d, nw, w_in_t, bg, qnw, knw, gq, gk):
    n = x2d.shape[0]
    tm = ROW_TILE
    row = lambda w: pl.BlockSpec((tm, w), lambda i: (i, 0))
    full = lambda a: pl.BlockSpec(a.shape, lambda i: (0,) * a.ndim)
    once = lambda a: pl.BlockSpec(a.shape, lambda i: (0,) * a.ndim, pipeline_mode=pl.Buffered(1))
    return pl.pallas_call(
        _inproj_kernel,
        grid=(n // tm,),
        in_specs=[row(D_MODEL), full(nw), once(w_in_t), full(bg), full(qnw), full(knw),
                  full(gq), full(gk)],
        out_specs=[row(ATT_Q_W), row(2 * ATT_KV_W), row(ML_QK_W), row(ML_QK_W),
                   row(ML_V_W), row(ML_V_W), pl.BlockSpec((N_GATES, tm), lambda i: (0, i))],
        out_shape=[jax.ShapeDtypeStruct((n, ATT_Q_W), BF16),
                   jax.ShapeDtypeStruct((n, 2 * ATT_KV_W), F32),
                   jax.ShapeDtypeStruct((n, ML_QK_W), BF16),
                   jax.ShapeDtypeStruct((n, ML_QK_W), BF16),
                   jax.ShapeDtypeStruct((n, ML_V_W), BF16),
                   jax.ShapeDtypeStruct((n, ML_V_W), BF16),
                   jax.ShapeDtypeStruct((N_GATES, n), F32)],
        scratch_shapes=[pltpu.VMEM((ATT_Q_W, D_MODEL), BF16)],
        compiler_params=pltpu.CompilerParams(dimension_semantics=("arbitrary",),
                                             vmem_limit_bytes=VMEM_LIMIT),
        name="inproj",
    )(x2d, nw, w_in_t, bg, qnw, knw, gq, gk)


def _head_norm_t(z, head_dim, w_col):
    rows, tokens = z.shape
    z3 = z.reshape(rows // head_dim, head_dim, tokens)
    ms = jnp.mean(z3 * z3, axis=1, keepdims=True)
    return (z3 * lax.rsqrt(ms + EPS)).reshape(rows, tokens) * w_col


def _inproj_t_kernel(x_ref, nw_ref, w_ref, bg_ref, qnw_ref, knw_ref,
                     qa_ref, ks_ref, kv_ref, qm_ref, km_ref, vm_ref, om_ref, gt_ref, h_scr):
    tm = x_ref.shape[0]
    sub = INPROJ_SUB

    def norm_rows(c):
        for r0 in range(c * sub, (c + 1) * sub, NORM_ROWS):
            rows = slice(r0, r0 + NORM_ROWS)
            h_scr[rows, :] = _rms(x_ref[rows, :], nw_ref[...]).astype(BF16)
            yield

    def project(c):
        tok = slice(c * sub, (c + 1) * sub)
        h = h_scr[tok, :]

        def proj(lo, width):
            return _dot_nt(w_ref[lo:lo + width, :], h)

        qa_ref[:, tok] = _head_norm_t(proj(OFF_QA, ATT_Q_W), ATT_HEAD_DIM,
                                      qnw_ref[...]).astype(BF16)
        yield
        kv = proj(OFF_KV, 2 * ATT_KV_W)
        k = _head_norm_t(kv[:ATT_KV_W], ATT_HEAD_DIM, knw_ref[...])
        kv_ref[:ATT_KV_W, tok] = k
        kv_ref[ATT_KV_W:, tok] = kv[ATT_KV_W:]
        ks_ref[tok, :] = k.T.astype(BF16)
        qm_ref[:, tok] = (proj(OFF_QM, ML_QK_W) * ML_SCALE).astype(BF16)
        yield
        km_ref[:, tok] = proj(OFF_KM, ML_QK_W).astype(BF16)
        vm_ref[:, tok] = proj(OFF_VM, ML_V_W).astype(BF16)
        yield
        om_ref[:, tok] = proj(OFF_OM, ML_V_W).astype(BF16)
        gl = proj(OFF_GL, 2 * SUBLANES)[:N_GATES] + bg_ref[...]
        row = lax.broadcasted_iota(jnp.int32, gl.shape, 0)
        gt_ref[:, tok] = jnp.where(row < ML_HEADS, gl, _log_sigmoid(gl))
        yield

    for _ in norm_rows(0):
        pass
    for c in range(tm // sub):
        norms = norm_rows(c + 1) if c + 1 < tm // sub else iter(())
        for _ in project(c):
            next(norms, None)
        for _ in norms:
            pass


def _inproj_t(x2d, nw, w_in_t, bg_col, qnw_col, knw_col):
    n = x2d.shape[0]
    tm = INPROJ_TILE
    full = lambda a: pl.BlockSpec(a.shape, lambda i: (0,) * a.ndim)
    once = lambda a: pl.BlockSpec(a.shape, lambda i: (0,) * a.ndim, pipeline_mode=pl.Buffered(1))
    col = lambda w: pl.BlockSpec((None, w, tm), lambda i: (i, 0, 0))
    slab = lambda w, dt: jax.ShapeDtypeStruct((n // tm, w, tm), dt)
    return pl.pallas_call(
        _inproj_t_kernel,
        grid=(n // tm,),
        in_specs=[pl.BlockSpec((tm, D_MODEL), lambda i: (i, 0)), full(nw), once(w_in_t),
                  full(bg_col), full(qnw_col), full(knw_col)],
        out_specs=[col(ATT_Q_W), pl.BlockSpec((tm, ATT_KV_W), lambda i: (i, 0)),
                   col(2 * ATT_KV_W), col(ML_QK_W), col(ML_QK_W), col(ML_V_W), col(ML_V_W),
                   col(N_GATES)],
        out_shape=[slab(ATT_Q_W, BF16),
                   jax.ShapeDtypeStruct((n, ATT_KV_W), BF16),
                   slab(2 * ATT_KV_W, F32), slab(ML_QK_W, BF16), slab(ML_QK_W, BF16),
                   slab(ML_V_W, BF16), slab(ML_V_W, BF16), slab(N_GATES, F32)],
        scratch_shapes=[pltpu.VMEM((tm, D_MODEL), BF16)],
        compiler_params=pltpu.CompilerParams(dimension_semantics=("arbitrary",),
                                             vmem_limit_bytes=VMEM_LIMIT),
        name="inproj_t",
    )(x2d, nw, w_in_t, bg_col, qnw_col, knw_col)


def _gate_forms(gates, seg_mask, want_raw_col):
    L = gates.shape[1]
    m_bf = seg_mask.astype(F32).astype(BF16)
    cum_row = jnp.zeros(gates.shape, F32)
    cum_col = jnp.zeros((L, gates.shape[0]), F32)
    raw_col = None
    if want_raw_col:
        r = lax.broadcasted_iota(jnp.int32, (L, L), 0)
        c = lax.broadcasted_iota(jnp.int32, (L, L), 1)
        eye = (r == c).astype(F32).astype(BF16)
        raw_col = jnp.zeros((L, gates.shape[0]), F32)
    for part in _split3(gates):
        cum_row = cum_row + _dot_nt(part, m_bf)
        cum_col = cum_col + _dot_nt(m_bf, part)
        if want_raw_col:
            raw_col = raw_col + _dot_nt(eye, part)
    return cum_row, cum_col, raw_col


def _mlstm_intra(q_pad, k_pair, v_ext, seg_mask, b_c, b_r, ig_r, m_prev_c):
    dm = jnp.where(seg_mask, b_c + (ig_r - b_r), -jnp.inf)
    inter = b_c + m_prev_c
    m_row = jnp.maximum(inter, jnp.max(dm, axis=-1, keepdims=True))
    w_inter = jnp.exp(inter - m_row)
    p = _dot_nt(q_pad, k_pair) * jnp.exp(dm - m_row)
    return _dot(p.astype(BF16), v_ext), m_row, w_inter


def _mlstm_out(pv, m_row, w_inter, q_c, q_n, mlnw_h, om_h):
    num = pv[:, :ML_V_DIM] + w_inter * q_c
    den = pv[:, ML_V_DIM:ML_V_DIM + 1] + w_inter * q_n
    hh = num / jnp.maximum(jnp.abs(den), jnp.exp(-m_row))
    return (_rms(hh, mlnw_h) * _sigmoid(om_h.astype(F32))).astype(BF16)


def _ones_col(rows):
    lane = lax.broadcasted_iota(jnp.int32, (rows, LANES), 1)
    return (lane == 0).astype(F32).astype(BF16)


def _prompt_mixer_t_kernel(sinks_ref, qa_ref, ksc_ref, ksp_ref, kvc_ref, kvp_ref, qm_ref, km_ref,
                           vm_ref, om_ref, gt_ref, x_ref, wout_ref, mlnw_ref,
                           x1_ref, ct_ref, nrow_ref, m_ref, kt_ref, vt_ref,
                           mix_scr, state_scr, m_scr, band_scr, causal_scr, tri_scr,
                           s_scr_a, s_scr_b, e_scr):
    i = pl.program_id(1)
    A = WINDOW
    L = MIX_TILE
    C = ML_CHUNK
    n_pairs = ML_HEADS // 2

    @pl.when(i == 0)
    def _():
        state_scr[...] = jnp.zeros(state_scr.shape, F32)
        m_scr[...] = jnp.zeros(m_scr.shape, F32)
        kj = lax.broadcasted_iota(jnp.int32, (2 * A, A), 0)
        qi = lax.broadcasted_iota(jnp.int32, (2 * A, A), 1)
        band = (kj > qi) & (kj <= qi + WINDOW)
        band_scr[0] = jnp.where(band, 0.0, -jnp.inf)
        band_scr[1] = jnp.where(band & (kj >= A), 0.0, -jnp.inf)
        r = lax.broadcasted_iota(jnp.int32, (C, C), 0)
        c = lax.broadcasted_iota(jnp.int32, (C, C), 1)
        causal_scr[...] = jnp.where(r <= c, 0.0, -jnp.inf)
        tri_scr[...] = (r <= c).astype(F32).astype(BF16)

    k_all = jnp.concatenate([ksp_ref[...], ksc_ref[...]], axis=0)
    v_all = jnp.concatenate([kvp_ref[ATT_KV_W:, :], kvc_ref[ATT_KV_W:, :]], axis=1).astype(BF16)
    zero_q = jnp.zeros((ATT_HEAD_DIM, A), BF16)
    slot = 0
    s_bufs = (s_scr_a, s_scr_b)

    def stage_scores(j):
        pieces = []
        for h in range(ATT_HEADS):
            q_h = qa_ref[h * ATT_HEAD_DIM:(h + 1) * ATT_HEAD_DIM, j * A:(j + 1) * A]
            pieces.append(jnp.concatenate([q_h, zero_q] if h < ATT_GROUP else [zero_q, q_h],
                                          axis=0))
        s_bufs[j % 2][slot] = _dot(k_all[j * A:(j + 2) * A, :], jnp.concatenate(pieces, axis=1))

    def attend(j):
        cols = slice(j * A, (j + 1) * A)
        vt = v_all[:, j * A:(j + 2) * A]
        if j + 1 < L // A:
            stage_scores(j + 1)
        s_buf = s_bufs[j % 2]
        bias = jnp.where(i > 0, band_scr[0], band_scr[1]) if j == 0 else band_scr[0]
        m_rows = []
        for h in range(ATT_HEADS):
            sb = s_buf[slot, :, h * A:(h + 1) * A] + bias
            m_rows.append(jnp.maximum(jnp.max(sb, axis=0, keepdims=True),
                                      sinks_ref[h] * LOG2_E))
        inv_rows = []
        for h in range(ATT_HEADS):
            e = jnp.exp2(s_buf[slot, :, h * A:(h + 1) * A] + (bias - m_rows[h]))
            e_scr[:, h * A:(h + 1) * A] = e.astype(BF16)
            inv_rows.append(1.0 / (jnp.sum(e, axis=0, keepdims=True)
                                   + jnp.exp2(sinks_ref[h] * LOG2_E - m_rows[h])))
        o = _dot(vt, e_scr[...])
        for h in range(ATT_HEADS):
            g = h // ATT_GROUP
            mix_scr[h * ATT_HEAD_DIM:(h + 1) * ATT_HEAD_DIM, cols] = (
                o[g * ATT_HEAD_DIM:(g + 1) * ATT_HEAD_DIM, h * A:(h + 1) * A]
                * inv_rows[h]).astype(BF16)

    row128 = lax.broadcasted_iota(jnp.int32, (LANES, C), 0)
    ones_rows = (row128 == 0).astype(F32).astype(BF16)

    def mlstm_chunk(ci):
        tok = slice(ci * C, (ci + 1) * C)
        gates = gt_ref[:, tok] * LOG2_E
        cum_row = jnp.zeros(gates.shape, F32)
        for part in _split3(gates):
            cum_row = cum_row + _dot(part, tri_scr[...])
        ig_rows = gates[:ML_HEADS]
        b_rows = cum_row[ML_HEADS:]
        key_cols = jnp.concatenate([ig_rows - b_rows, jnp.zeros((LANES - ML_HEADS, C), F32)],
                                   axis=0).T
        for p in range(n_pairs):
            q_c = qm_ref[p * LANES:(p + 1) * LANES, tok]
            k_pair = km_ref[p * LANES:(p + 1) * LANES, tok]
            zero = jnp.zeros_like(q_c)
            state = state_scr[p]
            state_bf = state.astype(BF16)
            new_state = []
            for e_id in range(2):
                h = 2 * p + e_id
                v_rows = slice(h * ML_V_DIM, (h + 1) * ML_V_DIM)
                head_rows = (row128 < ML_QK_DIM) if e_id == 0 else (row128 >= ML_QK_DIM)
                q_pad = jnp.where(head_rows, q_c, zero)
                b_r = b_rows[h:h + 1, :]
                ig_r = ig_rows[h:h + 1, :]
                m_prev = m_scr[h:h + 1, 0:1]
                dm = (b_r + key_cols[:, h:h + 1]) + causal_scr[...]
                inter = b_r + m_prev
                m_row = jnp.maximum(inter, jnp.max(dm, axis=0, keepdims=True))
                w_inter = jnp.exp2(inter - m_row)
                qk = lax.dot_general(k_pair, q_pad, (((0,), (0,)), ((), ())),
                                     preferred_element_type=F32)
                p_t = (qk * jnp.exp2(dm - m_row)).astype(BF16)
                v_ext = jnp.concatenate([vm_ref[v_rows, tok], ones_rows], axis=0)
                num = _dot(v_ext, p_t) + w_inter * _dot(state_bf, q_pad)
                den = num[ML_V_DIM:ML_V_DIM + 1, :]
                hh = num[:ML_V_DIM] * (1.0 / jnp.maximum(jnp.abs(den), jnp.exp2(-m_row)))
                ms = jnp.mean(hh * hh, axis=0, keepdims=True)
                gate = _sigmoid(om_ref[v_rows, tok].astype(F32))
                mix_scr[ATT_Q_W + h * ML_V_DIM:ATT_Q_W + (h + 1) * ML_V_DIM, tok] = (
                    hh * lax.rsqrt(ms + EPS) * mlnw_ref[v_rows, :] * gate).astype(BF16)
                b_last = b_r[:, C - 1:C]
                a_r = b_last - b_r + ig_r
                m_new = jnp.maximum(b_last + m_prev, jnp.max(a_r, axis=-1, keepdims=True))
                sc = jnp.exp2(b_last + m_prev - m_new)
                wsv = (v_ext.astype(F32) * jnp.exp2(a_r - m_new)).astype(BF16)
                new_state.append(sc * state + _dot_nt(wsv, k_pair))
                m_scr[h:h + 1, :] = jnp.broadcast_to(m_new, (1, LANES))
            first = lax.broadcasted_iota(jnp.int32, state.shape, 1) < ML_QK_DIM
            state_scr[p] = jnp.where(first, new_state[0], new_state[1])
            yield

    def out_proj(ci):
        tok = slice(ci * C, (ci + 1) * C)
        mix_t = mix_scr[:, tok].T
        for n in range(D_MODEL // OUT_COLS):
            nc = slice(n * OUT_COLS, (n + 1) * OUT_COLS)
            x1_ref[tok, nc] = x_ref[tok, nc] + _dot(mix_t, wout_ref[:, nc])
            yield

    stage_scores(0)
    n_sub = C // A
    pairs = (step for ci in range(L // C) for step in mlstm_chunk(ci))
    projs = iter(())
    for j in range(L // A):
        if j and j % n_sub == 0:
            projs = out_proj(j // n_sub - 1)
        attend(j)
        next(projs, None)
        next(pairs, None)
        next(projs, None)
    for _ in pairs:
        pass
    for _ in projs:
        pass
    for _ in out_proj(L // C - 1):
        pass

    @pl.when(i == pl.num_programs(1) - 1)
    def _():
        for p in range(n_pairs):
            c_t = state_scr[p, :ML_V_DIM, :].T
            for e_id in range(2):
                ct_ref[0, 2 * p + e_id] = c_t[e_id * ML_QK_DIM:(e_id + 1) * ML_QK_DIM, :]
            nrow_ref[0, p:p + 1, :] = state_scr[p, ML_V_DIM:ML_V_DIM + 1, :]
        for h in range(ML_HEADS):
            m_ref[0, :, h:h + 1] = m_scr[h:h + 1, 0:1] * (1.0 / LOG2_E)
        kt_ref[0] = kvc_ref[:ATT_KV_W, L - WINDOW:]
        vt_ref[0] = kvc_ref[ATT_KV_W:, L - WINDOW:]


def _prompt_mixer_t(batch, seq, sinks, qa, ks, kv, qm, km, vm, om, gt, x2d, wout, mlnw_col):
    tq = MIX_TILE
    nt = seq // tq
    sub = tq // WINDOW
    per_slab = INPROJ_TILE // tq
    win_per_slab = INPROJ_TILE // WINDOW
    col = lambda w: pl.BlockSpec(
        (None, w, tq), lambda b, i: ((b * nt + i) // per_slab, 0, (b * nt + i) % per_slab))
    full = lambda a: pl.BlockSpec(a.shape, lambda b, i: (0,) * a.ndim)
    once = lambda a: pl.BlockSpec(a.shape, lambda b, i: (0,) * a.ndim,
                                  pipeline_mode=pl.Buffered(1))
    prev_block = lambda b, i: jnp.maximum((b * nt + i) * sub - 1, 0)
    per_batch = lambda *dims: pl.BlockSpec((1,) + dims, lambda b, i: (b,) + (0,) * len(dims))
    return pl.pallas_call(
        _prompt_mixer_t_kernel,
        grid=(batch, nt),
        in_specs=[pl.BlockSpec(memory_space=pltpu.SMEM),
                  col(ATT_Q_W),
                  pl.BlockSpec((tq, ATT_KV_W), lambda b, i: (b * nt + i, 0)),
                  pl.BlockSpec((WINDOW, ATT_KV_W), lambda b, i: (prev_block(b, i), 0)),
                  col(2 * ATT_KV_W),
                  pl.BlockSpec((None, 2 * ATT_KV_W, WINDOW),
                               lambda b, i: (prev_block(b, i) // win_per_slab, 0,
                                             prev_block(b, i) % win_per_slab)),
                  col(ML_QK_W), col(ML_QK_W), col(ML_V_W), col(ML_V_W), col(N_GATES),
                  pl.BlockSpec((tq, D_MODEL), lambda b, i: (b * nt + i, 0)),
                  once(wout), full(mlnw_col)],
        out_specs=[pl.BlockSpec((tq, D_MODEL), lambda b, i: (b * nt + i, 0)),
                   per_batch(ML_HEADS, ML_QK_DIM, ML_V_DIM),
                   per_batch(ML_HEADS // 2, LANES),
                   per_batch(1, ML_HEADS),
                   per_batch(ATT_KV_W, WINDOW),
                   per_batch(ATT_KV_W, WINDOW)],
        out_shape=[jax.ShapeDtypeStruct((batch * seq, D_MODEL), F32),
                   jax.ShapeDtypeStruct((batch, ML_HEADS, ML_QK_DIM, ML_V_DIM), F32),
                   jax.ShapeDtypeStruct((batch, ML_HEADS // 2, LANES), F32),
                   jax.ShapeDtypeStruct((batch, 1, ML_HEADS), F32),
                   jax.ShapeDtypeStruct((batch, ATT_KV_W, WINDOW), F32),
                   jax.ShapeDtypeStruct((batch, ATT_KV_W, WINDOW), F32)],
        scratch_shapes=[pltpu.VMEM((D_MODEL, tq), BF16),
                        pltpu.VMEM((ML_HEADS // 2, 2 * LANES, LANES), F32),
                        pltpu.VMEM((SUBLANES, LANES), F32),
                        pltpu.VMEM((2, 2 * WINDOW, WINDOW), F32),
                        pltpu.VMEM((ML_CHUNK, ML_CHUNK), F32),
                        pltpu.VMEM((ML_CHUNK, ML_CHUNK), BF16),
                        pltpu.VMEM((2, 2 * WINDOW, ATT_HEADS * WINDOW), F32),
                        pltpu.VMEM((2, 2 * WINDOW, ATT_HEADS * WINDOW), F32),
                        pltpu.VMEM((2 * WINDOW, ATT_HEADS * WINDOW), BF16)],
        compiler_params=pltpu.CompilerParams(dimension_semantics=("arbitrary", "arbitrary"),
                                             vmem_limit_bytes=VMEM_LIMIT),
        name="prompt_mixer_t",
    )(sinks, qa, ks, ks, kv, kv, qm, km, vm, om, gt, x2d, wout, mlnw_col)


def _sample_mixer_kernel(t_len, sinks_ref, qa_ref, kv_ref, ck_ref, cv_ref, qm_ref, km_ref,
                         vm_ref, om_ref, gt_ref, c0_ref, n0_ref, m0_ref, x_ref, wout_ref,
                         mlnw_ref, x1_ref, nk_ref, nv_ref, c_ref, n_ref, m_ref,
                         mix_scr, wperm_scr):
    bt = SAMPLE_BT
    T = t_len
    L = bt * T

    @pl.when(pl.program_id(0) == 0)
    def _():
        _permute_head_rows(wperm_scr, wout_ref)
        wperm_scr[ATT_Q_W:, :] = wout_ref[ATT_Q_W:, :]

    lane3 = lax.broadcasted_iota(jnp.int32, (bt, T, LANES), 2)
    low3 = lane3 < ATT_HEAD_DIM
    lane = lax.broadcasted_iota(jnp.int32, (L, LANES), 1)
    low = lane < ATT_HEAD_DIM

    qa3 = qa_ref[...].astype(F32).reshape(bt, T, ATT_Q_W)
    pieces = []
    for col in range(ATT_GROUP):
        qc = qa3[:, :, col * LANES:(col + 1) * LANES]
        pieces += [jnp.where(low3, qc, 0.0), jnp.where(low3, 0.0, qc)]
    q3 = jnp.concatenate(pieces, axis=1).astype(BF16)
    R = bt * N_STACK * T
    q2 = q3.reshape(R, LANES)
    kv_new = kv_ref[...]
    k_new = kv_new[:, :ATT_KV_W]
    v_new = kv_new[:, ATT_KV_W:]
    ck = ck_ref[...]
    cv = cv_ref[...]
    s_c = jnp.einsum('bqd,bdk->bqk', q3, ck.astype(BF16),
                     preferred_element_type=F32).reshape(R, WINDOW)
    s_n = _dot_nt(q2, k_new.astype(BF16))
    row_c = lax.broadcasted_iota(jnp.int32, (R, WINDOW), 0)
    col_c = lax.broadcasted_iota(jnp.int32, (R, WINDOW), 1)
    s_c = jnp.where(col_c > row_c % T, s_c, -jnp.inf)
    row_n = lax.broadcasted_iota(jnp.int32, (R, L), 0)
    col_n = lax.broadcasted_iota(jnp.int32, (R, L), 1)
    valid_n = (row_n // (N_STACK * T) == col_n // T) & (col_n % T <= row_n % T)
    s_n = jnp.where(valid_n, s_n, -jnp.inf)
    stack_id = (lax.broadcasted_iota(jnp.int32, (R, 1), 0) // T) % N_STACK
    sink = jnp.zeros((R, 1), F32)
    for k_id in range(N_STACK):
        sink = jnp.where(stack_id == k_id, sinks_ref[ATT_HEAD_ORDER[k_id]], sink)
    m = jnp.maximum(jnp.maximum(jnp.max(s_c, axis=-1, keepdims=True),
                                jnp.max(s_n, axis=-1, keepdims=True)), sink)
    e_c = jnp.exp(s_c - m)
    e_n = jnp.exp(s_n - m)
    denom = (jnp.sum(e_c, axis=-1, keepdims=True) + jnp.sum(e_n, axis=-1, keepdims=True)
             + jnp.exp(sink - m))
    o = jnp.einsum('bqk,bdk->bqd', e_c.astype(BF16).reshape(bt, N_STACK * T, WINDOW),
                   cv.astype(BF16), preferred_element_type=F32).reshape(R, LANES)
    o = (o + _dot(e_n.astype(BF16), v_new.astype(BF16))) / denom
    o3 = o.reshape(bt, N_STACK * T, LANES)
    for col in range(ATT_GROUP):
        lo_h = o3[:, (2 * col) * T:(2 * col + 1) * T, :]
        hi_h = o3[:, (2 * col + 1) * T:(2 * col + 2) * T, :]
        mix_scr[:, col * LANES:(col + 1) * LANES] = jnp.where(
            low3, lo_h, hi_h).reshape(L, LANES).astype(BF16)

    keep = lax.broadcasted_iota(jnp.int32, (ATT_KV_W, WINDOW), 1) < WINDOW - T
    k_new_t = k_new.T
    v_new_t = v_new.T
    def roll_caches():
        for q in range(bt):
            shift = (WINDOW - T - q * T) % WINDOW
            nk_ref[q] = jnp.where(keep, pltpu.roll(ck_ref[q], WINDOW - T, axis=1),
                                  pltpu.roll(k_new_t, shift, axis=1))
            nv_ref[q] = jnp.where(keep, pltpu.roll(cv_ref[q], WINDOW - T, axis=1),
                                  pltpu.roll(v_new_t, shift, axis=1))
            if (q + 1) % (bt // ML_HEADS) == 0:
                yield

    rolls = roll_caches()

    r = lax.broadcasted_iota(jnp.int32, (L, L), 0)
    c = lax.broadcasted_iota(jnp.int32, (L, L), 1)
    seg = (r // T == c // T) & (r <= c)
    seg_bias = jnp.where(seg, 0.0, -jnp.inf)
    seg_bf = seg.astype(F32).astype(BF16)
    gates = gt_ref[...] * LOG2_E
    cum_row = jnp.zeros(gates.shape, F32)
    for part in _split3(gates):
        cum_row = cum_row + _dot(part, seg_bf)
    ig_rows = gates[:ML_HEADS]
    b_rows = cum_row[ML_HEADS:]
    gate_cols = jnp.concatenate([ig_rows, b_rows, jnp.zeros((LANES - N_GATES, L), F32)],
                                axis=0).T

    def col_to_row(x_col):
        return jnp.broadcast_to(x_col, (L, LANES)).T[0:1, :]

    ones_rows = (r[:LANES] == 0).astype(F32).astype(BF16)
    qm = qm_ref[...]
    km = km_ref[...]
    qm_f = qm.astype(F32)
    km_f = km.astype(F32)
    n_rep = bt * ML_QK_DIM // LANES
    bd_row = lax.broadcasted_iota(jnp.int32, (L, bt * ML_QK_DIM), 0) // T
    bd_lane = lax.broadcasted_iota(jnp.int32, (L, bt * ML_QK_DIM), 1) // ML_QK_DIM
    block_diag = bd_row == bd_lane

    def spread(x_pair, e):
        other = pltpu.roll(x_pair, ML_QK_DIM, axis=1)
        twice = jnp.where(low, x_pair, other) if e == 0 else jnp.where(low, other, x_pair)
        return jnp.where(block_diag, jnp.concatenate([twice] * n_rep, axis=1), 0.0).astype(BF16)

    def head_stages(h):
        p, e = divmod(h, 2)
        qc = qm[:, p * LANES:(p + 1) * LANES]
        k_pair = km[:, p * LANES:(p + 1) * LANES]
        zero = jnp.zeros_like(qc)
        q_pad = jnp.where(low, qc, zero) if e == 0 else jnp.where(low, zero, qc)
        v_h = vm_ref[:, h * ML_V_DIM:(h + 1) * ML_V_DIM]
        v_ext_t = jnp.concatenate([v_h.astype(F32).T.astype(BF16), ones_rows], axis=0)
        ig_c = gate_cols[:, h:h + 1]
        b_c = gate_cols[:, ML_HEADS + h:ML_HEADS + h + 1]
        b_r = b_rows[h:h + 1, :]
        yield
        m0 = m0_ref[:, :, h:h + 1] * LOG2_E
        inter = b_r + col_to_row(jnp.broadcast_to(m0, (bt, T, 1)).reshape(L, 1))
        dm = (b_r + (ig_c - b_c)) + seg_bias
        m_row = jnp.maximum(inter, jnp.max(dm, axis=0, keepdims=True))
        w_inter = jnp.exp2(inter - m_row)
        yield
        p_t = (_dot_nt(k_pair, q_pad) * jnp.exp2(dm - m_row)).astype(BF16)
        num_t = _dot(v_ext_t, p_t)
        yield
        q_h3 = qm_f[:, h * ML_QK_DIM:(h + 1) * ML_QK_DIM].reshape(bt, T, ML_QK_DIM)
        k_h3 = km_f[:, h * ML_QK_DIM:(h + 1) * ML_QK_DIM].reshape(bt, T, ML_QK_DIM)
        c0 = c0_ref[:, h]
        n0 = n0_ref[:, h:h + 1, :]
        q_c_t = _dot(spread(qm_f[:, p * LANES:(p + 1) * LANES], e),
                     c0.astype(BF16).reshape(bt * ML_QK_DIM, ML_V_DIM)).T
        q_n_r = col_to_row(jnp.sum(q_h3 * n0, axis=-1, keepdims=True).reshape(L, 1))
        yield
        num =num_t[:ML_V_DIM] + w_inter * q_c_t
        den = num_t[ML_V_DIM:ML_V_DIM + 1] + w_inter * q_n_r
        hh = num * (1.0 / jnp.maximum(jnp.abs(den), jnp.exp2(-m_row)))
        ms = jnp.mean(hh * hh, axis=0, keepdims=True)
        yield
        mix_scr[:, ATT_Q_W + h * ML_V_DIM:ATT_Q_W + (h + 1) * ML_V_DIM] = (
            (hh * lax.rsqrt(ms + EPS)).T * mlnw_ref[:, h * ML_V_DIM:(h + 1) * ML_V_DIM]
            * _sigmoid(om_ref[:, h * ML_V_DIM:(h + 1) * ML_V_DIM].astype(F32))).astype(BF16)
        yield
        b3 = b_c.reshape(bt, T, 1)
        b_last = b3[:, T - 1:T, :]
        a3 = b_last - b3 + ig_c.reshape(bt, T, 1)
        m_new = jnp.maximum(b_last + m0, jnp.max(a3, axis=1, keepdims=True))
        sc = jnp.exp2(b_last + m0 - m_new)
        ws = jnp.exp2(a3 - m_new)
        yield
        kw = spread(km_f[:, p * LANES:(p + 1) * LANES] * ws.reshape(L, 1), e)
        d_c = lax.dot_general(kw, v_h, (((0,), (0,)), ((), ())), preferred_element_type=F32)
        c_ref[:, h] = sc * c0 + d_c.reshape(bt, ML_QK_DIM, ML_V_DIM)
        n_ref[:, h:h + 1, :] = sc * n0 + jnp.sum(ws * k_h3, axis=1, keepdims=True)
        m_ref[:, :, h:h + 1] = m_new * (1.0 / LOG2_E)
        yield

    for _ in zip(*[head_stages(h) for h in range(ML_HEADS)]):
        next(rolls, None)
    for _ in rolls:
        pass

    x1_ref[...] = x_ref[...] + _dot(mix_scr[...], wperm_scr[...])


def _sample_mixer(nb, t_len, sinks, qa, kv, ck, cv, qm, km, vm, om, gt, c0, n0, m0, x2d, wout, mlnw):
    bt = SAMPLE_BT
    tl = bt * t_len
    row = lambda w: pl.BlockSpec((tl, w), lambda i: (i, 0))
    full = lambda a: pl.BlockSpec(a.shape, lambda i: (0,) * a.ndim)
    once = lambda a: pl.BlockSpec(a.shape, lambda i: (0,) * a.ndim, pipeline_mode=pl.Buffered(1))
    cache = pl.BlockSpec((bt, ATT_KV_W, WINDOW), lambda i: (i, 0, 0))
    c_spec = pl.BlockSpec((bt, ML_HEADS, ML_QK_DIM, ML_V_DIM), lambda i: (i, 0, 0, 0))
    n_spec = pl.BlockSpec((bt, ML_HEADS, ML_QK_DIM), lambda i: (i, 0, 0))
    m_spec = pl.BlockSpec((bt, 1, ML_HEADS), lambda i: (i, 0, 0))
    return pl.pallas_call(
        functools.partial(_sample_mixer_kernel, t_len),
        grid=(nb // bt,),
        in_specs=[pl.BlockSpec(memory_space=pltpu.SMEM),
                  row(ATT_Q_W), row(2 * ATT_KV_W), cache, cache, row(ML_QK_W), row(ML_QK_W),
                  row(ML_V_W), row(ML_V_W), pl.BlockSpec((N_GATES, tl), lambda i: (0, i)),
                  c_spec, n_spec, m_spec, row(D_MODEL), once(wout), full(mlnw)],
        out_specs=[row(D_MODEL), cache, cache, c_spec, n_spec, m_spec],
        out_shape=[jax.ShapeDtypeStruct((nb * t_len, D_MODEL), F32),
                   jax.ShapeDtypeStruct((nb, ATT_KV_W, WINDOW), F32),
                   jax.ShapeDtypeStruct((nb, ATT_KV_W, WINDOW), F32),
                   jax.ShapeDtypeStruct((nb, ML_HEADS, ML_QK_DIM, ML_V_DIM), F32),
                   jax.ShapeDtypeStruct((nb, ML_HEADS, ML_QK_DIM), F32),
                   jax.ShapeDtypeStruct((nb, 1, ML_HEADS), F32)],
        scratch_shapes=[pltpu.VMEM((tl, D_MODEL), BF16),
                        pltpu.VMEM((D_MODEL, D_MODEL), BF16)],
        compiler_params=pltpu.CompilerParams(dimension_semantics=("arbitrary",),
                                             vmem_limit_bytes=VMEM_LIMIT),
        name="sample_mixer",
    )(sinks, qa, kv, ck, cv, qm, km, vm, om, gt, c0, n0, m0, x2d, wout, mlnw)


def _ffn_kernel(seq_rows, *refs):
    if seq_rows is None:
        (x_ref, nw_ref, w_ref, cw_ref, cb_ref, wd_ref, y_ref, conv_ref,
         gbuf, act_scr, carry) = refs
        hist_ref = None
    else:
        (x_ref, hist_ref, nw_ref, w_ref, cw_ref, cb_ref, wd_ref, y_ref, conv_ref,
         gbuf, act_scr) = refs
        carry = None
    tm = x_ref.shape[0]
    tf = FF_CHUNK
    n_hist = CONV_W - 1
    rows = tm if seq_rows is None else seq_rows
    nseq = tm // rows
    base = SUBLANES
    n_chunks = D_FF // tf

    if carry is not None:
        @pl.when(pl.program_id(1) == 0)
        def _():
            carry[...] = jnp.zeros(carry.shape, F32)

    x = x_ref[...]
    h2 = _rms(x, nw_ref[...]).astype(BF16)

    def proj(f):
        return (_dot(h2, w_ref[:, f * tf:(f + 1) * tf]),
                _dot(h2, w_ref[:, D_FF + f * tf:D_FF + (f + 1) * tf]))

    nxt = proj(0)
    for f in range(n_chunks):
        g, u = nxt
        if f + 1 < n_chunks:
            nxt = proj(f + 1)
        cols = slice(f * tf, (f + 1) * tf)
        s = f % 2
        g3 = g.reshape(nseq, rows, tf)
        if seq_rows is None:
            gbuf[s, :, base - n_hist:base, :] = carry[:, SUBLANES - n_hist:, cols]
            carry[:, SUBLANES - n_hist:, cols] = g3[:, rows - n_hist:, :]
        else:
            gbuf[s, :, base - n_hist:base, :] = hist_ref[:, :, cols]
            conv_ref[:, :, cols] = g3[:, rows - n_hist:, :]
        gbuf[s, :, base:base + rows, :] = g3
        gc = cb_ref[:, cols] + g * cw_ref[CONV_W - 1:CONV_W, cols]
        for d in range(1, CONV_W):
            gm = gbuf[s, :, base - d:base - d + rows, :].reshape(tm, tf)
            gc = gc + gm * cw_ref[CONV_W - 1 - d:CONV_W - d, cols]
        act_scr[:, cols] = (gc * _sigmoid(gc) * u).astype(BF16)
    y_ref[...] = x + _dot(act_scr[...], wd_ref[...])

    if carry is not None:
        @pl.when(pl.program_id(1) == pl.num_programs(1) - 1)
        def _():
            conv_ref[...] = carry[:, SUBLANES - n_hist:, :]


def _ffn_scratch(tm, rows):
    return [pltpu.VMEM((2, tm // rows, SUBLANES + rows, FF_CHUNK), F32),
            pltpu.VMEM((tm, D_FF), BF16)]


def _ffn_prompt(batch, seq, x2d, nw, w, cw, cb, wd):
    tm = FFN_TILE
    nt = seq // tm
    full = lambda a: pl.BlockSpec(a.shape, lambda b, i: (0,) * a.ndim)
    once = lambda a: pl.BlockSpec(a.shape, lambda b, i: (0,) * a.ndim,
                                  pipeline_mode=pl.Buffered(1))
    row = pl.BlockSpec((tm, D_MODEL), lambda b, i: (b * nt + i, 0))
    return pl.pallas_call(
        functools.partial(_ffn_kernel, None),
        grid=(batch, nt),
        in_specs=[row, full(nw), once(w), full(cw), full(cb), once(wd)],
        out_specs=[row, pl.BlockSpec((1, CONV_W - 1, D_FF), lambda b, i: (b, 0, 0))],
        out_shape=[jax.ShapeDtypeStruct((batch * seq, D_MODEL), F32),
                   jax.ShapeDtypeStruct((batch, CONV_W - 1, D_FF), F32)],
        scratch_shapes=_ffn_scratch(tm, tm) + [pltpu.VMEM((1, SUBLANES, D_FF), F32)],
        compiler_params=pltpu.CompilerParams(dimension_semantics=("arbitrary", "arbitrary"),
                                             vmem_limit_bytes=VMEM_LIMIT),
        name="ffn_prompt",
    )(x2d, nw, w, cw, cb, wd)


def _ffn_sample(nb, t_len, x2d, hist, nw, w, cw, cb, wd):
    tm = ROW_TILE
    bt = tm // t_len
    full = lambda a: pl.BlockSpec(a.shape, lambda i: (0,) * a.ndim)
    once = lambda a: pl.BlockSpec(a.shape, lambda i: (0,) * a.ndim, pipeline_mode=pl.Buffered(1))
    row = pl.BlockSpec((tm, D_MODEL), lambda i: (i, 0))
    hist_spec = pl.BlockSpec((bt, CONV_W - 1, D_FF), lambda i: (i, 0, 0))
    return pl.pallas_call(
        functools.partial(_ffn_kernel, t_len),
        grid=(nb // bt,),
        in_specs=[row, hist_spec, full(nw), once(w), full(cw), full(cb), once(wd)],
        out_specs=[row, hist_spec],
        out_shape=[jax.ShapeDtypeStruct((nb * t_len, D_MODEL), F32),
                   jax.ShapeDtypeStruct((nb, CONV_W - 1, D_FF), F32)],
        scratch_shapes=_ffn_scratch(tm, t_len),
        compiler_params=pltpu.CompilerParams(dimension_semantics=("arbitrary",),
                                             vmem_limit_bytes=VMEM_LIMIT),
        name="ffn_sample",
    )(x2d, hist, nw, w, cw, cb, wd)


def _head_mean_matrix(width, head_dim):
    idx = np.arange(width) // head_dim
    return jnp.asarray((idx[:, None] == idx[None, :]).astype(np.float32) / head_dim, dtype=BF16)


def _layer_weights(norm_mix_w, w_in, b_gates, q_norm_w, k_norm_w, sinks, ml_norm_w, w_out,
                   norm_ffn_w, w_ffn_in, conv_w, conv_b, w_down):
    w_in_t = jnp.pad(w_in.T.astype(BF16), ((0, IN_WIDTH_PAD - w_in.shape[1]), (0, 0)))
    return dict(
        nw=norm_mix_w.reshape(1, D_MODEL),
        w_in_t=w_in_t,
        bg=jnp.pad(b_gates, (0, LANES - N_GATES)).reshape(1, LANES),
        qnw=(jnp.tile(q_norm_w, ATT_HEADS) * ATT_SCALE).reshape(1, ATT_Q_W),
        knw=jnp.tile(k_norm_w, ATT_KV_HEADS).reshape(1, ATT_KV_W),
        gq=_head_mean_matrix(ATT_Q_W, ATT_HEAD_DIM),
        gk=_head_mean_matrix(ATT_KV_W, ATT_HEAD_DIM),
        bg_col=b_gates.reshape(N_GATES, 1),
        qnw_col=(jnp.tile(q_norm_w, ATT_HEADS) * (ATT_SCALE * LOG2_E)).reshape(ATT_Q_W, 1),
        knw_col=jnp.tile(k_norm_w, ATT_KV_HEADS).reshape(ATT_KV_W, 1),
        mlnw_col=ml_norm_w.reshape(ML_V_W, 1),
        sinks=sinks,
        mlnw=ml_norm_w.reshape(1, ML_V_W),
        wout=w_out.astype(BF16),
        nfw=norm_ffn_w.reshape(1, D_MODEL),
        wff=w_ffn_in.astype(BF16),
        cw=conv_w,
        cb=conv_b.reshape(1, D_FF),
        wd=w_down.astype(BF16),
    )


def _cache_from_t(a_t):
    n = a_t.shape[0]
    return jnp.transpose(a_t.reshape(n, ATT_KV_HEADS, ATT_HEAD_DIM, WINDOW), (0, 3, 1, 2))


def _cache_to_t(a):
    n = a.shape[0]
    return jnp.transpose(a, (0, 2, 3, 1)).reshape(n, ATT_KV_W, WINDOW)


def _prompt_layer(x, w):
    batch, seq, _ = x.shape
    assert seq % INPROJ_TILE == 0 and INPROJ_TILE % MIX_TILE == 0
    assert MIX_TILE % ML_CHUNK == 0 and ML_CHUNK % WINDOW == 0
    assert seq % FFN_TILE == 0
    x2d = x.reshape(batch * seq, D_MODEL)
    qa, ks, kv, qm, km, vm, om, gt = _inproj_t(x2d, w["nw"], w["w_in_t"], w["bg_col"],
                                               w["qnw_col"], w["knw_col"])
    x1, c_t, n_row, m, k_t, v_t = _prompt_mixer_t(batch, seq, w["sinks"], qa, ks, kv, qm, km, vm,
                                                  om, gt, x2d, w["wout"], w["mlnw_col"])
    y, conv = _ffn_prompt(batch, seq, x1, w["nfw"], w["wff"], w["cw"], w["cb"], w["wd"])
    return (y.reshape(batch, seq, D_MODEL), _cache_from_t(k_t), _cache_from_t(v_t),
            jnp.swapaxes(c_t, -1, -2), n_row.reshape(batch, ML_HEADS, ML_QK_DIM),
            m.reshape(batch, ML_HEADS), conv)


def _sample_layer(x, ck, cv, c0, n0, m0, conv_buf, w):
    nb, t_len, _ = x.shape
    assert t_len == SUBLANES and SAMPLE_BT * t_len == LANES and nb % SAMPLE_BT == 0
    assert (nb * t_len) % ROW_TILE == 0
    x2d = x.reshape(nb * t_len, D_MODEL)
    qa, kv, qm, km, vm, om, gt = _inproj(x2d, w["nw"], w["w_in_t"], w["bg"], w["qnw"], w["knw"],
                                         w["gq"], w["gk"])
    x1, nk_t, nv_t, c_t, n, m = _sample_mixer(
        nb, t_len, w["sinks"], qa, kv, _cache_to_t(ck), _cache_to_t(cv), qm, km, vm, om, gt,
        jnp.swapaxes(c0, -1, -2), n0, m0.reshape(nb, 1, ML_HEADS), x2d, w["wout"], w["mlnw"])
    y, conv = _ffn_sample(nb, t_len, x1, conv_buf, w["nfw"], w["wff"], w["cw"], w["cb"],
                          w["wd"])
    return (y.reshape(nb, t_len, D_MODEL), _cache_from_t(nk_t), _cache_from_t(nv_t),
            jnp.swapaxes(c_t, -1, -2), n, m.reshape(nb, ML_HEADS), conv)


def kernel(x_prompt, x_sample, cache_attn_k, cache_attn_v, state_mlstm_C, state_mlstm_n,
           state_mlstm_m, cache_ffn_conv, norm_mix_w, w_in, b_gates, q_norm_w, k_norm_w,
           sinks, ml_norm_w, w_out, norm_ffn_w, w_ffn_in, conv_w, conv_b, w_down):
    depth = w_in.shape[0]
    yp, ys = x_prompt, x_sample
    sp = [[] for _ in range(6)]
    ss = [[] for _ in range(6)]
    for l in range(depth):
        w = _layer_weights(norm_mix_w[l], w_in[l], b_gates[l], q_norm_w[l], k_norm_w[l], sinks[l],
                           ml_norm_w[l], w_out[l], norm_ffn_w[l], w_ffn_in[l], conv_w[l],
                           conv_b[l], w_down[l])
        yp, *st_p = _prompt_layer(yp, w)
        ys, *st_s = _sample_layer(ys, cache_attn_k[l], cache_attn_v[l], state_mlstm_C[l],
                                  state_mlstm_n[l], state_mlstm_m[l], cache_ffn_conv[l], w)
        for i in range(6):
            sp[i].append(st_p[i])
            ss[i].append(st_s[i])
    k_p, v_p, c_p, n_p, m_p, conv_p = [jnp.stack(a) for a in sp]
    k_s, v_s, c_s, n_s, m_s, conv_s = [jnp.stack(a) for a in ss]
    return (yp, ys, k_p, v_p, c_p, n_p, m_p, conv_p, k_s, v_s, c_s, n_s, m_s, conv_s)
```

```python
import functools

import numpy as np
import jax
import jax.numpy as jnp
from jax import lax
from jax.experimental import pallas as pl
from jax.experimental.pallas import tpu as pltpu

F32 = jnp.float32
BF16 = jnp.bfloat16

D_MODEL = 1024
ATT_HEADS = 8
ATT_KV_HEADS = 2
ATT_HEAD_DIM = 64
ATT_GROUP = ATT_HEADS // ATT_KV_HEADS
WINDOW = 128
ML_HEADS = 4
ML_V_DIM = 128
ML_QK_DIM = 64
D_FF = 2816
CONV_W = 3
EPS = 1e-6
ATT_SCALE = ATT_HEAD_DIM ** -0.5
ML_SCALE = ML_QK_DIM ** -0.5
LOG2_E = 1.4426950408889634

ATT_Q_W = ATT_HEADS * ATT_HEAD_DIM
ATT_KV_W = ATT_KV_HEADS * ATT_HEAD_DIM
ML_QK_W = ML_HEADS * ML_QK_DIM
ML_V_W = ML_HEADS * ML_V_DIM
N_GATES = 2 * ML_HEADS
N_STACK = 2 * ATT_GROUP

LANES = 128
SUBLANES = 8

OFF_QA = 0
OFF_KV = OFF_QA + ATT_Q_W
OFF_QM = OFF_KV + 2 * ATT_KV_W
OFF_KM = OFF_QM + ML_QK_W
OFF_VM = OFF_KM + ML_QK_W
OFF_OM = OFF_VM + ML_V_W
OFF_GL = OFF_OM + ML_V_W
IN_WIDTH_PAD = OFF_GL + LANES

ATT_HEAD_ORDER = tuple(h for c in range(ATT_GROUP) for h in (c, c + ATT_GROUP))

ROW_TILE = 512
FFN_TILE = 1024
INPROJ_SUB = 256
NORM_ROWS = 64
INPROJ_TILE = 1024
MIX_TILE = 512
ML_CHUNK = 256
OUT_COLS = 256
FF_CHUNK = 256
SAMPLE_BT = 16
VMEM_LIMIT = 56 * 1024 * 1024


def _dot(a, b):
    return jnp.dot(a, b, preferred_element_type=F32)


def _dot_nt(a, b):
    return lax.dot_general(a, b, (((1,), (1,)), ((), ())), preferred_element_type=F32)


def _split3(x):
    hi = x.astype(BF16)
    r1 = x - hi.astype(F32)
    mid = r1.astype(BF16)
    lo = (r1 - mid.astype(F32)).astype(BF16)
    return hi, mid, lo


def _rms(x, w):
    ms = jnp.mean(x * x, axis=-1, keepdims=True)
    return x * lax.rsqrt(ms + EPS) * w


def _log_sigmoid(x):
    return jnp.minimum(x, 0.0) - jnp.log1p(jnp.exp(-jnp.abs(x)))


def _sigmoid(x):
    return 1.0 / (1.0 + jnp.exp(-x))


def _permute_head_rows(dst_ref, src_ref):
    for k, h in enumerate(ATT_HEAD_ORDER):
        dst_ref[k * ATT_HEAD_DIM:(k + 1) * ATT_HEAD_DIM, :] = (
            src_ref[h * ATT_HEAD_DIM:(h + 1) * ATT_HEAD_DIM, :])


def _inproj_kernel(x_ref, nw_ref, w_ref, bg_ref, qnw_ref, knw_ref, gq_ref, gk_ref,
                   qa_ref, kv_ref, qm_ref, km_ref, vm_ref, om_ref, gt_ref, wq_scr):
    @pl.when(pl.program_id(0) == 0)
    def _():
        _permute_head_rows(wq_scr, w_ref)

    h = _rms(x_ref[...], nw_ref[...]).astype(BF16)

    def proj(lo, width):
        return _dot_nt(h, w_ref[lo:lo + width, :])

    q = _dot_nt(h, wq_scr[...])
    q_ms = _dot((q * q).astype(BF16), gq_ref[...])
    qa_ref[...] = (q * lax.rsqrt(q_ms + EPS) * qnw_ref[...]).astype(BF16)

    kv = proj(OFF_KV, 2 * ATT_KV_W)
    k = kv[:, :ATT_KV_W]
    k_ms = _dot((k * k).astype(BF16), gk_ref[...])
    kv_ref[:, :ATT_KV_W] = k * lax.rsqrt(k_ms + EPS) * knw_ref[...]
    kv_ref[:, ATT_KV_W:] = kv[:, ATT_KV_W:]

    qm_ref[...] = (proj(OFF_QM, ML_QK_W) * ML_SCALE).astype(BF16)
    km_ref[...] = proj(OFF_KM, ML_QK_W).astype(BF16)
    vm_ref[...] = proj(OFF_VM, ML_V_W).astype(BF16)
    om_ref[...] = proj(OFF_OM, ML_V_W).astype(BF16)

    gl = proj(OFF_GL, LANES) + bg_ref[...]
    lane = lax.broadcasted_iota(jnp.int32, gl.shape, 1)
    g = jnp.where(lane < ML_HEADS, gl, _log_sigmoid(gl))
    gt_ref[...] = g.T[:N_GATES, :]


def _inproj(x2d, nw, w_in_t, bg, qnw, knw, gq, gk):
    n = x2d.shape[0]
    tm = ROW_TILE
    row = lambda w: pl.BlockSpec((tm, w), lambda i: (i, 0))
    full = lambda a: pl.BlockSpec(a.shape, lambda i: (0,) * a.ndim)
    once = lambda a: pl.BlockSpec(a.shape, lambda i: (0,) * a.ndim, pipeline_mode=pl.Buffered(1))
    return pl.pallas_call(
        _inproj_kernel,
        grid=(n // tm,),
        in_specs=[row(D_MODEL), full(nw), once(w_in_t), full(bg), full(qnw), full(knw),
                  full(gq), full(gk)],
        out_specs=[row(ATT_Q_W), row(2 * ATT_KV_W), row(ML_QK_W), row(ML_QK_W),
                   row(ML_V_W), row(ML_V_W), pl.BlockSpec((N_GATES, tm), lambda i: (0, i))],
        out_shape=[jax.ShapeDtypeStruct((n, ATT_Q_W), BF16),
                   jax.ShapeDtypeStruct((n, 2 * ATT_KV_W), F32),
                   jax.ShapeDtypeStruct((n, ML_QK_W), BF16),
                   jax.ShapeDtypeStruct((n, ML_QK_W), BF16),
                   jax.ShapeDtypeStruct((n, ML_V_W), BF16),
                   jax.ShapeDtypeStruct((n, ML_V_W), BF16),
                   jax.ShapeDtypeStruct((N_GATES, n), F32)],
        scratch_shapes=[pltpu.VMEM((ATT_Q_W, D_MODEL), BF16)],
        compiler_params=pltpu.CompilerParams(dimension_semantics=("arbitrary",),
                                             vmem_limit_bytes=VMEM_LIMIT),
        name="inproj",
    )(x2d, nw, w_in_t, bg, qnw, knw, gq, gk)


def _head_norm_t(z, head_dim, w_col):
    rows, tokens = z.shape
    z3 = z.reshape(rows // head_dim, head_dim, tokens)
    ms = jnp.mean(z3 * z3, axis=1, keepdims=True)
    return (z3 * lax.rsqrt(ms + EPS)).reshape(rows, tokens) * w_col


def _inproj_t_kernel(x_ref, nw_ref, w_ref, bg_ref, qnw_ref, knw_ref,
                     qa_ref, ks_ref, kv_ref, qm_ref, km_ref, vm_ref, om_ref, gt_ref, h_scr):
    tm = x_ref.shape[0]
    sub = INPROJ_SUB

    def norm_rows(c):
        for r0 in range(c * sub, (c + 1) * sub, NORM_ROWS):
            rows = slice(r0, r0 + NORM_ROWS)
            h_scr[rows, :] = _rms(x_ref[rows, :], nw_ref[...]).astype(BF16)
            yield

    def project(c):
        tok = slice(c * sub, (c + 1) * sub)
        h = h_scr[tok, :]

        def proj(lo, width):
            return _dot_nt(w_ref[lo:lo + width, :], h)

        qa_ref[:, tok] = _head_norm_t(proj(OFF_QA, ATT_Q_W), ATT_HEAD_DIM,
                                      qnw_ref[...]).astype(BF16)
        yield
        kv = proj(OFF_KV, 2 * ATT_KV_W)
        k = _head_norm_t(kv[:ATT_KV_W], ATT_HEAD_DIM, knw_ref[...])
        kv_ref[:ATT_KV_W, tok] = k
        kv_ref[ATT_KV_W:, tok] = kv[ATT_KV_W:]
        ks_ref[tok, :] = k.T.astype(BF16)
        qm_ref[:, tok] = (proj(OFF_QM, ML_QK_W) * ML_SCALE).astype(BF16)
        yield
        km_ref[:, tok] = proj(OFF_KM, ML_QK_W).astype(BF16)
        vm_ref[:, tok] = proj(OFF_VM, ML_V_W).astype(BF16)
        yield
        om_ref[:, tok] = proj(OFF_OM, ML_V_W).astype(BF16)
        gl = proj(OFF_GL, 2 * SUBLANES)[:N_GATES] + bg_ref[...]
        row = lax.broadcasted_iota(jnp.int32, gl.shape, 0)
        gt_ref[:, tok] = jnp.where(row < ML_HEADS, gl, _log_sigmoid(gl))
        yield

    for _ in norm_rows(0):
        pass
    for c in range(tm // sub):
        norms = norm_rows(c + 1) if c + 1 < tm // sub else iter(())
        for _ in project(c):
            next(norms, None)
        for _ in norms:
            pass


def _inproj_t(x2d, nw, w_in_t, bg_col, qnw_col, knw_col):
    n = x2d.shape[0]
    tm = INPROJ_TILE
    full = lambda a: pl.BlockSpec(a.shape, lambda i: (0,) * a.ndim)
    once = lambda a: pl.BlockSpec(a.shape, lambda i: (0,) * a.ndim, pipeline_mode=pl.Buffered(1))
    col = lambda w: pl.BlockSpec((None, w, tm), lambda i: (i, 0, 0))
    slab = lambda w, dt: jax.ShapeDtypeStruct((n // tm, w, tm), dt)
    return pl.pallas_call(
        _inproj_t_kernel,
        grid=(n // tm,),
        in_specs=[pl.BlockSpec((tm, D_MODEL), lambda i: (i, 0)), full(nw), once(w_in_t),
                  full(bg_col), full(qnw_col), full(knw_col)],
        out_specs=[col(ATT_Q_W), pl.BlockSpec((tm, ATT_KV_W), lambda i: (i, 0)),
                   col(2 * ATT_KV_W), col(ML_QK_W), col(ML_QK_W), col(ML_V_W), col(ML_V_W),
                   col(N_GATES)],
        out_shape=[slab(ATT_Q_W, BF16),
                   jax.ShapeDtypeStruct((n, ATT_KV_W), BF16),
                   slab(2 * ATT_KV_W, F32), slab(ML_QK_W, BF16), slab(ML_QK_W, BF16),
                   slab(ML_V_W, BF16), slab(ML_V_W, BF16), slab(N_GATES, F32)],
        scratch_shapes=[pltpu.VMEM((tm, D_MODEL), BF16)],
        compiler_params=pltpu.CompilerParams(dimension_semantics=("arbitrary",),
                                             vmem_limit_bytes=VMEM_LIMIT),
        name="inproj_t",
    )(x2d, nw, w_in_t, bg_col, qnw_col, knw_col)


def _gate_forms(gates, seg_mask, want_raw_col):
    L = gates.shape[1]
    m_bf = seg_mask.astype(F32).astype(BF16)
    cum_row = jnp.zeros(gates.shape, F32)
    cum_col = jnp.zeros((L, gates.shape[0]), F32)
    raw_col = None
    if want_raw_col:
        r = lax.broadcasted_iota(jnp.int32, (L, L), 0)
        c = lax.broadcasted_iota(jnp.int32, (L, L), 1)
        eye = (r == c).astype(F32).astype(BF16)
        raw_col = jnp.zeros((L, gates.shape[0]), F32)
    for part in _split3(gates):
        cum_row = cum_row + _dot_nt(part, m_bf)
        cum_col = cum_col + _dot_nt(m_bf, part)
        if want_raw_col:
            raw_col = raw_col + _dot_nt(eye, part)
    return cum_row, cum_col, raw_col


def _mlstm_intra(q_pad, k_pair, v_ext, seg_mask, b_c, b_r, ig_r, m_prev_c):
    dm = jnp.where(seg_mask, b_c + (ig_r - b_r), -jnp.inf)
    inter = b_c + m_prev_c
    m_row = jnp.maximum(inter, jnp.max(dm, axis=-1, keepdims=True))
    w_inter = jnp.exp(inter - m_row)
    p = _dot_nt(q_pad, k_pair) * jnp.exp(dm - m_row)
    return _dot(p.astype(BF16), v_ext), m_row, w_inter


def _mlstm_out(pv, m_row, w_inter, q_c, q_n, mlnw_h, om_h):
    num = pv[:, :ML_V_DIM] + w_inter * q_c
    den = pv[:, ML_V_DIM:ML_V_DIM + 1] + w_inter * q_n
    hh = num / jnp.maximum(jnp.abs(den), jnp.exp(-m_row))
    return (_rms(hh, mlnw_h) * _sigmoid(om_h.astype(F32))).astype(BF16)


def _ones_col(rows):
    lane = lax.broadcasted_iota(jnp.int32, (rows, LANES), 1)
    return (lane == 0).astype(F32).astype(BF16)


def _prompt_mixer_t_kernel(sinks_ref, qa_ref, ksc_ref, ksp_ref, kvc_ref, kvp_ref, qm_ref, km_ref,
                           vm_ref, om_ref, gt_ref, x_ref, wout_ref, mlnw_ref,
                           x1_ref, ct_ref, nrow_ref, m_ref, kt_ref, vt_ref,
                           mix_scr, state_scr, m_scr, band_scr, causal_scr, tri_scr,
                           s_scr_a, s_scr_b, e_scr):
    i = pl.program_id(1)
    A = WINDOW
    L = MIX_TILE
    C = ML_CHUNK
    n_pairs = ML_HEADS // 2

    @pl.when(i == 0)
    def _():
        state_scr[...] = jnp.zeros(state_scr.shape, F32)
        m_scr[...] = jnp.zeros(m_scr.shape, F32)
        kj = lax.broadcasted_iota(jnp.int32, (2 * A, A), 0)
        qi = lax.broadcasted_iota(jnp.int32, (2 * A, A), 1)
        band = (kj > qi) & (kj <= qi + WINDOW)
        band_scr[0] = jnp.where(band, 0.0, -jnp.inf)
        band_scr[1] = jnp.where(band & (kj >= A), 0.0, -jnp.inf)
        r = lax.broadcasted_iota(jnp.int32, (C, C), 0)
        c = lax.broadcasted_iota(jnp.int32, (C, C), 1)
        causal_scr[...] = jnp.where(r <= c, 0.0, -jnp.inf)
        tri_scr[...] = (r <= c).astype(F32).astype(BF16)

    k_all = jnp.concatenate([ksp_ref[...], ksc_ref[...]], axis=0)
    v_all = jnp.concatenate([kvp_ref[ATT_KV_W:, :], kvc_ref[ATT_KV_W:, :]], axis=1).astype(BF16)
    zero_q = jnp.zeros((ATT_HEAD_DIM, A), BF16)
    slot = 0
    s_bufs = (s_scr_a, s_scr_b)

    def stage_scores(j):
        pieces = []
        for h in range(ATT_HEADS):
            q_h = qa_ref[h * ATT_HEAD_DIM:(h + 1) * ATT_HEAD_DIM, j * A:(j + 1) * A]
            pieces.append(jnp.concatenate([q_h, zero_q] if h < ATT_GROUP else [zero_q, q_h],
                                          axis=0))
        s_bufs[j % 2][slot] = _dot(k_all[j * A:(j + 2) * A, :], jnp.concatenate(pieces, axis=1))

    def attend(j):
        cols = slice(j * A, (j + 1) * A)
        vt = v_all[:, j * A:(j + 2) * A]
        if j + 1 < L // A:
            stage_scores(j + 1)
        s_buf = s_bufs[j % 2]
        bias = jnp.where(i > 0, band_scr[0], band_scr[1]) if j == 0 else band_scr[0]
        m_rows = []
        for h in range(ATT_HEADS):
            sb = s_buf[slot, :, h * A:(h + 1) * A] + bias
            m_rows.append(jnp.maximum(jnp.max(sb, axis=0, keepdims=True),
                                      sinks_ref[h] * LOG2_E))
        inv_rows = []
        for h in range(ATT_HEADS):
            e = jnp.exp2(s_buf[slot, :, h * A:(h + 1) * A] + (bias - m_rows[h]))
            e_scr[:, h * A:(h + 1) * A] = e.astype(BF16)
            inv_rows.append(1.0 / (jnp.sum(e, axis=0, keepdims=True)
                                   + jnp.exp2(sinks_ref[h] * LOG2_E - m_rows[h])))
        o = _dot(vt, e_scr[...])
        for h in range(ATT_HEADS):
            g = h // ATT_GROUP
            mix_scr[h * ATT_HEAD_DIM:(h + 1) * ATT_HEAD_DIM, cols] = (
                o[g * ATT_HEAD_DIM:(g + 1) * ATT_HEAD_DIM, h * A:(h + 1) * A]
                * inv_rows[h]).astype(BF16)

    row128 = lax.broadcasted_iota(jnp.int32, (LANES, C), 0)
    ones_rows = (row128 == 0).astype(F32).astype(BF16)

    def mlstm_chunk(ci):
        tok = slice(ci * C, (ci + 1) * C)
        gates = gt_ref[:, tok] * LOG2_E
        cum_row = jnp.zeros(gates.shape, F32)
        for part in _split3(gates):
            cum_row = cum_row + _dot(part, tri_scr[...])
        ig_rows = gates[:ML_HEADS]
        b_rows = cum_row[ML_HEADS:]
        key_cols = jnp.concatenate([ig_rows - b_rows, jnp.zeros((LANES - ML_HEADS, C), F32)],
                                   axis=0).T
        for p in range(n_pairs):
            q_c = qm_ref[p * LANES:(p + 1) * LANES, tok]
            k_pair = km_ref[p * LANES:(p + 1) * LANES, tok]
            zero = jnp.zeros_like(q_c)
            state = state_scr[p]
            state_bf = state.astype(BF16)
            new_state = []
            for e_id in range(2):
                h = 2 * p + e_id
                v_rows = slice(h * ML_V_DIM, (h + 1) * ML_V_DIM)
                head_rows = (row128 < ML_QK_DIM) if e_id == 0 else (row128 >= ML_QK_DIM)
                q_pad = jnp.where(head_rows, q_c, zero)
                b_r = b_rows[h:h + 1, :]
                ig_r = ig_rows[h:h + 1, :]
                m_prev = m_scr[h:h + 1, 0:1]
                dm = (b_r + key_cols[:, h:h + 1]) + causal_scr[...]
                inter = b_r + m_prev
                m_row = jnp.maximum(inter, jnp.max(dm, axis=0, keepdims=True))
                w_inter = jnp.exp2(inter - m_row)
                qk = lax.dot_general(k_pair, q_pad, (((0,), (0,)), ((), ())),
                                     preferred_element_type=F32)
                p_t = (qk * jnp.exp2(dm - m_row)).astype(BF16)
                v_ext = jnp.concatenate([vm_ref[v_rows, tok], ones_rows], axis=0)
                num = _dot(v_ext, p_t) + w_inter * _dot(state_bf, q_pad)
                den = num[ML_V_DIM:ML_V_DIM + 1, :]
                hh = num[:ML_V_DIM] * (1.0 / jnp.maximum(jnp.abs(den), jnp.exp2(-m_row)))
                ms = jnp.mean(hh * hh, axis=0, keepdims=True)
                gate = _sigmoid(om_ref[v_rows, tok].astype(F32))
                mix_scr[ATT_Q_W + h * ML_V_DIM:ATT_Q_W + (h + 1) * ML_V_DIM, tok] = (
                    hh * lax.rsqrt(ms + EPS) * mlnw_ref[v_rows, :] * gate).astype(BF16)
                b_last = b_r[:, C - 1:C]
                a_r = b_last - b_r + ig_r
                m_new = jnp.maximum(b_last + m_prev, jnp.max(a_r, axis=-1, keepdims=True))
                sc = jnp.exp2(b_last + m_prev - m_new)
                wsv = (v_ext.astype(F32) * jnp.exp2(a_r - m_new)).astype(BF16)
                new_state.append(sc * state + _dot_nt(wsv, k_pair))
                m_scr[h:h + 1, :] = jnp.broadcast_to(m_new, (1, LANES))
            first = lax.broadcasted_iota(jnp.int32, state.shape, 1) < ML_QK_DIM
            state_scr[p] = jnp.where(first, new_state[0], new_state[1])
            yield

    def out_proj(ci):
        tok = slice(ci * C, (ci + 1) * C)
        mix_t = mix_scr[:, tok].T
        for n in range(D_MODEL // OUT_COLS):
            nc = slice(n * OUT_COLS, (n + 1) * OUT_COLS)
            x1_ref[tok, nc] = x_ref[tok, nc] + _dot(mix_t, wout_ref[:, nc])
            yield

    stage_scores(0)
    n_sub = C // A
    pairs = (step for ci in range(L // C) for step in mlstm_chunk(ci))
    projs = iter(())
    for j in range(L // A):
        if j and j % n_sub == 0:
            projs = out_proj(j // n_sub - 1)
        attend(j)
        next(projs, None)
        next(pairs, None)
        next(projs, None)
    for _ in pairs:
        pass
    for _ in projs:
        pass
    for _ in out_proj(L // C - 1):
        pass

    @pl.when(i == pl.num_programs(1) - 1)
    def _():
        for p in range(n_pairs):
            c_t = state_scr[p, :ML_V_DIM, :].T
            for e_id in range(2):
                ct_ref[0, 2 * p + e_id] = c_t[e_id * ML_QK_DIM:(e_id + 1) * ML_QK_DIM, :]
            nrow_ref[0, p:p + 1, :] = state_scr[p, ML_V_DIM:ML_V_DIM + 1, :]
        for h in range(ML_HEADS):
            m_ref[0, :, h:h + 1] = m_scr[h:h + 1, 0:1] * (1.0 / LOG2_E)
        kt_ref[0] = kvc_ref[:ATT_KV_W, L - WINDOW:]
        vt_ref[0] = kvc_ref[ATT_KV_W:, L - WINDOW:]


def _prompt_mixer_t(batch, seq, sinks, qa, ks, kv, qm, km, vm, om, gt, x2d, wout, mlnw_col):
    tq = MIX_TILE
    nt = seq // tq
    sub = tq // WINDOW
    per_slab = INPROJ_TILE // tq
    win_per_slab = INPROJ_TILE // WINDOW
    col = lambda w: pl.BlockSpec(
        (None, w, tq), lambda b, i: ((b * nt + i) // per_slab, 0, (b * nt + i) % per_slab))
    full = lambda a: pl.BlockSpec(a.shape, lambda b, i: (0,) * a.ndim)
    once = lambda a: pl.BlockSpec(a.shape, lambda b, i: (0,) * a.ndim,
                                  pipeline_mode=pl.Buffered(1))
    prev_block = lambda b, i: jnp.maximum((b * nt + i) * sub - 1, 0)
    per_batch = lambda *dims: pl.BlockSpec((1,) + dims, lambda b, i: (b,) + (0,) * len(dims))
    return pl.pallas_call(
        _prompt_mixer_t_kernel,
        grid=(batch, nt),
        in_specs=[pl.BlockSpec(memory_space=pltpu.SMEM),
                  col(ATT_Q_W),
                  pl.BlockSpec((tq, ATT_KV_W), lambda b, i: (b * nt + i, 0)),
                  pl.BlockSpec((WINDOW, ATT_KV_W), lambda b, i: (prev_block(b, i), 0)),
                  col(2 * ATT_KV_W),
                  pl.BlockSpec((None, 2 * ATT_KV_W, WINDOW),
                               lambda b, i: (prev_block(b, i) // win_per_slab, 0,
                                             prev_block(b, i) % win_per_slab)),
                  col(ML_QK_W), col(ML_QK_W), col(ML_V_W), col(ML_V_W), col(N_GATES),
                  pl.BlockSpec((tq, D_MODEL), lambda b, i: (b * nt + i, 0)),
                  once(wout), full(mlnw_col)],
        out_specs=[pl.BlockSpec((tq, D_MODEL), lambda b, i: (b * nt + i, 0)),
                   per_batch(ML_HEADS, ML_QK_DIM, ML_V_DIM),
                   per_batch(ML_HEADS // 2, LANES),
                   per_batch(1, ML_HEADS),
                   per_batch(ATT_KV_W, WINDOW),
                   per_batch(ATT_KV_W, WINDOW)],
        out_shape=[jax.ShapeDtypeStruct((batch * seq, D_MODEL), F32),
                   jax.ShapeDtypeStruct((batch, ML_HEADS, ML_QK_DIM, ML_V_DIM), F32),
                   jax.ShapeDtypeStruct((batch, ML_HEADS // 2, LANES), F32),
                   jax.ShapeDtypeStruct((batch, 1, ML_HEADS), F32),
                   jax.ShapeDtypeStruct((batch, ATT_KV_W, WINDOW), F32),
                   jax.ShapeDtypeStruct((batch, ATT_KV_W, WINDOW), F32)],
        scratch_shapes=[pltpu.VMEM((D_MODEL, tq), BF16),
                        pltpu.VMEM((ML_HEADS // 2, 2 * LANES, LANES), F32),
                        pltpu.VMEM((SUBLANES, LANES), F32),
                        pltpu.VMEM((2, 2 * WINDOW, WINDOW), F32),
                        pltpu.VMEM((ML_CHUNK, ML_CHUNK), F32),
                        pltpu.VMEM((ML_CHUNK, ML_CHUNK), BF16),
                        pltpu.VMEM((2, 2 * WINDOW, ATT_HEADS * WINDOW), F32),
                        pltpu.VMEM((2, 2 * WINDOW, ATT_HEADS * WINDOW), F32),
                        pltpu.VMEM((2 * WINDOW, ATT_HEADS * WINDOW), BF16)],
        compiler_params=pltpu.CompilerParams(dimension_semantics=("arbitrary", "arbitrary"),
                                             vmem_limit_bytes=VMEM_LIMIT),
        name="prompt_mixer_t",
    )(sinks, qa, ks, ks, kv, kv, qm, km, vm, om, gt, x2d, wout, mlnw_col)


def _sample_mixer_kernel(t_len, sinks_ref, qa_ref, kv_ref, ck_ref, cv_ref, qm_ref, km_ref,
                         vm_ref, om_ref, gt_ref, c0_ref, n0_ref, m0_ref, x_ref, wout_ref,
                         mlnw_ref, x1_ref, nk_ref, nv_ref, c_ref, n_ref, m_ref,
                         mix_scr, wperm_scr):
    bt = SAMPLE_BT
    T = t_len
    L = bt * T

    @pl.when(pl.program_id(0) == 0)
    def _():
        _permute_head_rows(wperm_scr, wout_ref)
        wperm_scr[ATT_Q_W:, :] = wout_ref[ATT_Q_W:, :]

    lane3 = lax.broadcasted_iota(jnp.int32, (bt, T, LANES), 2)
    low3 = lane3 < ATT_HEAD_DIM
    lane = lax.broadcasted_iota(jnp.int32, (L, LANES), 1)
    low = lane < ATT_HEAD_DIM

    qa3 = qa_ref[...].astype(F32).reshape(bt, T, ATT_Q_W)
    pieces = []
    for col in range(ATT_GROUP):
        qc = qa3[:, :, col * LANES:(col + 1) * LANES]
        pieces += [jnp.where(low3, qc, 0.0), jnp.where(low3, 0.0, qc)]
    q3 = jnp.concatenate(pieces, axis=1).astype(BF16)
    R = bt * N_STACK * T
    q2 = q3.reshape(R, LANES)
    kv_new = kv_ref[...]
    k_new = kv_new[:, :ATT_KV_W]
    v_new = kv_new[:, ATT_KV_W:]
    def attention_stages():
        s_c = jnp.einsum('bqd,bdk->bqk', q3, ck_ref[...].astype(BF16),
                         preferred_element_type=F32).reshape(R, WINDOW)
        s_n = _dot_nt(q2, k_new.astype(BF16))
        row_c = lax.broadcasted_iota(jnp.int32, (R, WINDOW), 0)
        col_c = lax.broadcasted_iota(jnp.int32, (R, WINDOW), 1)
        s_c = jnp.where(col_c > row_c % T, s_c, -jnp.inf)
        row_n = lax.broadcasted_iota(jnp.int32, (R, L), 0)
        col_n = lax.broadcasted_iota(jnp.int32, (R, L), 1)
        valid_n = (row_n // (N_STACK * T) == col_n // T) & (col_n % T <= row_n % T)
        s_n = jnp.where(valid_n, s_n, -jnp.inf)
        yield
        stack_id = (lax.broadcasted_iota(jnp.int32, (R, 1), 0) // T) % N_STACK
        sink = jnp.zeros((R, 1), F32)
        for k_id in range(N_STACK):
            sink = jnp.where(stack_id == k_id, sinks_ref[ATT_HEAD_ORDER[k_id]], sink)
        m = jnp.maximum(jnp.maximum(jnp.max(s_c, axis=-1, keepdims=True),
                                    jnp.max(s_n, axis=-1, keepdims=True)), sink)
        yield
        e_c = jnp.exp(s_c - m)
        e_n = jnp.exp(s_n - m)
        denom = (jnp.sum(e_c, axis=-1, keepdims=True) + jnp.sum(e_n, axis=-1, keepdims=True)
                 + jnp.exp(sink - m))
        yield
        o = jnp.einsum('bqk,bdk->bqd', e_c.astype(BF16).reshape(bt, N_STACK * T, WINDOW),
                       cv_ref[...].astype(BF16), preferred_element_type=F32).reshape(R, LANES)
        o = (o + _dot(e_n.astype(BF16), v_new.astype(BF16))) / denom
        yield
        o3 = o.reshape(bt, N_STACK * T, LANES)
        for col in range(ATT_GROUP):
            lo_h = o3[:, (2 * col) * T:(2 * col + 1) * T, :]
            hi_h = o3[:, (2 * col + 1) * T:(2 * col + 2) * T, :]
            mix_scr[:, col * LANES:(col + 1) * LANES] = jnp.where(
                low3, lo_h, hi_h).reshape(L, LANES).astype(BF16)
        yield

    keep = lax.broadcasted_iota(jnp.int32, (ATT_KV_W, WINDOW), 1) < WINDOW - T
    k_new_t = k_new.T
    v_new_t = v_new.T
    def roll_caches():
        for q in range(bt):
            shift = (WINDOW - T - q * T) % WINDOW
            nk_ref[q] = jnp.where(keep, pltpu.roll(ck_ref[q], WINDOW - T, axis=1),
                                  pltpu.roll(k_new_t, shift, axis=1))
            nv_ref[q] = jnp.where(keep, pltpu.roll(cv_ref[q], WINDOW - T, axis=1),
                                  pltpu.roll(v_new_t, shift, axis=1))
            if (q + 1) % (bt // ML_HEADS) == 0:
                yield

    rolls = roll_caches()

    r = lax.broadcasted_iota(jnp.int32, (L, L), 0)
    c = lax.broadcasted_iota(jnp.int32, (L, L), 1)
    seg = (r // T == c // T) & (r <= c)
    seg_bias = jnp.where(seg, 0.0, -jnp.inf)
    seg_bf = seg.astype(F32).astype(BF16)
    gates = gt_ref[...] * LOG2_E
    cum_row = jnp.zeros(gates.shape, F32)
    for part in _split3(gates):
        cum_row = cum_row + _dot(part, seg_bf)
    ig_rows = gates[:ML_HEADS]
    b_rows = cum_row[ML_HEADS:]
    gate_cols = jnp.concatenate([ig_rows, b_rows, jnp.zeros((LANES - N_GATES, L), F32)],
                                axis=0).T

    def col_to_row(x_col):
        return jnp.broadcast_to(x_col, (L, LANES)).T[0:1, :]

    ones_rows = (r[:LANES] == 0).astype(F32).astype(BF16)
    qm = qm_ref[...]
    km = km_ref[...]
    qm_f = qm.astype(F32)
    km_f = km.astype(F32)
    n_rep = bt * ML_QK_DIM // LANES
    bd_row = lax.broadcasted_iota(jnp.int32, (L, bt * ML_QK_DIM), 0) // T
    bd_lane = lax.broadcasted_iota(jnp.int32, (L, bt * ML_QK_DIM), 1) // ML_QK_DIM
    block_diag = bd_row == bd_lane

    def spread(x_pair, e):
        other = pltpu.roll(x_pair, ML_QK_DIM, axis=1)
        twice = jnp.where(low, x_pair, other) if e == 0 else jnp.where(low, other, x_pair)
        return jnp.where(block_diag, jnp.concatenate([twice] * n_rep, axis=1), 0.0).astype(BF16)

    def head_stages(h):
        p, e = divmod(h, 2)
        qc = qm[:, p * LANES:(p + 1) * LANES]
        k_pair = km[:, p * LANES:(p + 1) * LANES]
        zero = jnp.zeros_like(qc)
        q_pad = jnp.where(low, qc, zero) if e == 0 else jnp.where(low, zero, qc)
        v_h = vm_ref[:, h * ML_V_DIM:(h + 1) * ML_V_DIM]
        v_ext_t = jnp.concatenate([v_h.astype(F32).T.astype(BF16), ones_rows], axis=0)
        ig_c = gate_cols[:, h:h + 1]
        b_c = gate_cols[:, ML_HEADS + h:ML_HEADS + h + 1]
        b_r = b_rows[h:h + 1, :]
        yield
        m0 = m0_ref[:, :, h:h + 1] * LOG2_E
        inter = b_r + col_to_row(jnp.broadcast_to(m0, (bt, T, 1)).reshape(L, 1))
        dm = (b_r + (ig_c - b_c)) + seg_bias
        m_row = jnp.maximum(inter, jnp.max(dm, axis=0, keepdims=True))
        w_inter = jnp.exp2(inter - m_row)
        yield
        p_t = (_dot_nt(k_pair, q_pad) * jnp.exp2(dm - m_row)).astype(BF16)
        num_t = _dot(v_ext_t, p_t)
        yield
        q_h3 = qm_f[:, h * ML_QK_DIM:(h + 1) * ML_QK_DIM].reshape(bt, T, ML_QK_DIM)
        k_h3 = km_f[:, h * ML_QK_DIM:(h + 1) * ML_QK_DIM].reshape(bt, T, ML_QK_DIM)
        c0 = c0_ref[:, h]
        n0 = n0_ref[:, h:h + 1, :]
        q_c_t = _dot(spread(qm_f[:, p * LANES:(p + 1) * LANES], e),
                     c0.astype(BF16).reshape(bt * ML_QK_DIM, ML_V_DIM)).T
        q_n_r = col_to_row(jnp.sum(q_h3 * n0, axis=-1, keepdims=True).reshape(L, 1))
        yield
        num =num_t[:ML_V_DIM] + w_inter * q_c_t
        den = num_t[ML_V_DIM:ML_V_DIM + 1] + w_inter * q_n_r
        hh = num * (1.0 / jnp.maximum(jnp.abs(den), jnp.exp2(-m_row)))
        ms = jnp.mean(hh * hh, axis=0, keepdims=True)
        yield
        mix_scr[:, ATT_Q_W + h * ML_V_DIM:ATT_Q_W + (h + 1) * ML_V_DIM] = (
            (hh * lax.rsqrt(ms + EPS)).T * mlnw_ref[:, h * ML_V_DIM:(h + 1) * ML_V_DIM]
            * _sigmoid(om_ref[:, h * ML_V_DIM:(h + 1) * ML_V_DIM].astype(F32))).astype(BF16)
        yield
        b3 = b_c.reshape(bt, T, 1)
        b_last = b3[:, T - 1:T, :]
        a3 = b_last - b3 + ig_c.reshape(bt, T, 1)
        m_new = jnp.maximum(b_last + m0, jnp.max(a3, axis=1, keepdims=True))
        sc = jnp.exp2(b_last + m0 - m_new)
        ws = jnp.exp2(a3 - m_new)
        yield
        kw = spread(km_f[:, p * LANES:(p + 1) * LANES] * ws.reshape(L, 1), e)
        d_c = lax.dot_general(kw, v_h, (((0,), (0,)), ((), ())), preferred_element_type=F32)
        c_ref[:, h] = sc * c0 + d_c.reshape(bt, ML_QK_DIM, ML_V_DIM)
        n_ref[:, h:h + 1, :] = sc * n0 + jnp.sum(ws * k_h3, axis=1, keepdims=True)
        m_ref[:, :, h:h + 1] = m_new * (1.0 / LOG2_E)
        yield

    att = attention_stages()
    for _ in zip(*[head_stages(h) for h in range(ML_HEADS)]):
        next(att, None)
        next(rolls, None)
    for _ in att:
        pass
    for _ in rolls:
        pass

    x1_ref[...] = x_ref[...] + _dot(mix_scr[...], wperm_scr[...])


def _sample_mixer(nb, t_len, sinks, qa, kv, ck, cv, qm, km, vm, om, gt, c0, n0, m0, x2d, wout, mlnw):
    bt = SAMPLE_BT
    tl = bt * t_len
    row = lambda w: pl.BlockSpec((tl, w), lambda i: (i, 0))
    full = lambda a: pl.BlockSpec(a.shape, lambda i: (0,) * a.ndim)
    once = lambda a: pl.BlockSpec(a.shape, lambda i: (0,) * a.ndim, pipeline_mode=pl.Buffered(1))
    cache = pl.BlockSpec((bt, ATT_KV_W, WINDOW), lambda i: (i, 0, 0))
    c_spec = pl.BlockSpec((bt, ML_HEADS, ML_QK_DIM, ML_V_DIM), lambda i: (i, 0, 0, 0))
    n_spec = pl.BlockSpec((bt, ML_HEADS, ML_QK_DIM), lambda i: (i, 0, 0))
    m_spec = pl.BlockSpec((bt, 1, ML_HEADS), lambda i: (i, 0, 0))
    return pl.pallas_call(
        functools.partial(_sample_mixer_kernel, t_len),
        grid=(nb // bt,),
        in_specs=[pl.BlockSpec(memory_space=pltpu.SMEM),
                  row(ATT_Q_W), row(2 * ATT_KV_W), cache, cache, row(ML_QK_W), row(ML_QK_W),
                  row(ML_V_W), row(ML_V_W), pl.BlockSpec((N_GATES, tl), lambda i: (0, i)),
                  c_spec, n_spec, m_spec, row(D_MODEL), once(wout), full(mlnw)],
        out_specs=[row(D_MODEL), cache, cache, c_spec, n_spec, m_spec],
        out_shape=[jax.ShapeDtypeStruct((nb * t_len, D_MODEL), F32),
                   jax.ShapeDtypeStruct((nb, ATT_KV_W, WINDOW), F32),
                   jax.ShapeDtypeStruct((nb, ATT_KV_W, WINDOW), F32),
                   jax.ShapeDtypeStruct((nb, ML_HEADS, ML_QK_DIM, ML_V_DIM), F32),
                   jax.ShapeDtypeStruct((nb, ML_HEADS, ML_QK_DIM), F32),
                   jax.ShapeDtypeStruct((nb, 1, ML_HEADS), F32)],
        scratch_shapes=[pltpu.VMEM((tl, D_MODEL), BF16),
                        pltpu.VMEM((D_MODEL, D_MODEL), BF16)],
        compiler_params=pltpu.CompilerParams(dimension_semantics=("arbitrary",),
                                             vmem_limit_bytes=VMEM_LIMIT),
        name="sample_mixer",
    )(sinks, qa, kv, ck, cv, qm, km, vm, om, gt, c0, n0, m0, x2d, wout, mlnw)


def _ffn_kernel(seq_rows, *refs):
    if seq_rows is None:
        (x_ref, nw_ref, w_ref, cw_ref, cb_ref, wd_ref, y_ref, conv_ref,
         gbuf, act_scr, carry) = refs
        hist_ref = None
    else:
        (x_ref, hist_ref, nw_ref, w_ref, cw_ref, cb_ref, wd_ref, y_ref, conv_ref,
         gbuf, act_scr) = refs
        carry = None
    tm = x_ref.shape[0]
    tf = FF_CHUNK
    n_hist = CONV_W - 1
    rows = tm if seq_rows is None else seq_rows
    nseq = tm // rows
    base = SUBLANES
    n_chunks = D_FF // tf

    if carry is not None:
        @pl.when(pl.program_id(1) == 0)
        def _():
            carry[...] = jnp.zeros(carry.shape, F32)

    x = x_ref[...]
    h2 = _rms(x, nw_ref[...]).astype(BF16)

    def proj(f):
        return (_dot(h2, w_ref[:, f * tf:(f + 1) * tf]),
                _dot(h2, w_ref[:, D_FF + f * tf:D_FF + (f + 1) * tf]))

    nxt = proj(0)
    for f in range(n_chunks):
        g, u = nxt
        if f + 1 < n_chunks:
            nxt = proj(f + 1)
        cols = slice(f * tf, (f + 1) * tf)
        s = f % 2
        g3 = g.reshape(nseq, rows, tf)
        if seq_rows is None:
            gbuf[s, :, base - n_hist:base, :] = carry[:, SUBLANES - n_hist:, cols]
            carry[:, SUBLANES - n_hist:, cols] = g3[:, rows - n_hist:, :]
        else:
            gbuf[s, :, base - n_hist:base, :] = hist_ref[:, :, cols]
            conv_ref[:, :, cols] = g3[:, rows - n_hist:, :]
        gbuf[s, :, base:base + rows, :] = g3
        gc = cb_ref[:, cols] + g * cw_ref[CONV_W - 1:CONV_W, cols]
        for d in range(1, CONV_W):
            gm = gbuf[s, :, base - d:base - d + rows, :].reshape(tm, tf)
            gc = gc + gm * cw_ref[CONV_W - 1 - d:CONV_W - d, cols]
        act_scr[:, cols] = (gc * _sigmoid(gc) * u).astype(BF16)
    y_ref[...] = x + _dot(act_scr[...], wd_ref[...])

    if carry is not None:
        @pl.when(pl.program_id(1) == pl.num_programs(1) - 1)
        def _():
            conv_ref[...] = carry[:, SUBLANES - n_hist:, :]


def _ffn_scratch(tm, rows):
    return [pltpu.VMEM((2, tm // rows, SUBLANES + rows, FF_CHUNK), F32),
            pltpu.VMEM((tm, D_FF), BF16)]


def _ffn_prompt(batch, seq, x2d, nw, w, cw, cb, wd):
    tm = FFN_TILE
    nt = seq // tm
    full = lambda a: pl.BlockSpec(a.shape, lambda b, i: (0,) * a.ndim)
    once = lambda a: pl.BlockSpec(a.shape, lambda b, i: (0,) * a.ndim,
                                  pipeline_mode=pl.Buffered(1))
    row = pl.BlockSpec((tm, D_MODEL), lambda b, i: (b * nt + i, 0))
    return pl.pallas_call(
        functools.partial(_ffn_kernel, None),
        grid=(batch, nt),
        in_specs=[row, full(nw), once(w), full(cw), full(cb), once(wd)],
        out_specs=[row, pl.BlockSpec((1, CONV_W - 1, D_FF), lambda b, i: (b, 0, 0))],
        out_shape=[jax.ShapeDtypeStruct((batch * seq, D_MODEL), F32),
                   jax.ShapeDtypeStruct((batch, CONV_W - 1, D_FF), F32)],
        scratch_shapes=_ffn_scratch(tm, tm) + [pltpu.VMEM((1, SUBLANES, D_FF), F32)],
        compiler_params=pltpu.CompilerParams(dimension_semantics=("arbitrary", "arbitrary"),
                                             vmem_limit_bytes=VMEM_LIMIT),
        name="ffn_prompt",
    )(x2d, nw, w, cw, cb, wd)


def _ffn_sample(nb, t_len, x2d, hist, nw, w, cw, cb, wd):
    tm = ROW_TILE
    bt = tm // t_len
    full = lambda a: pl.BlockSpec(a.shape, lambda i: (0,) * a.ndim)
    once = lambda a: pl.BlockSpec(a.shape, lambda i: (0,) * a.ndim, pipeline_mode=pl.Buffered(1))
    row = pl.BlockSpec((tm, D_MODEL), lambda i: (i, 0))
    hist_spec = pl.BlockSpec((bt, CONV_W - 1, D_FF), lambda i: (i, 0, 0))
    return pl.pallas_call(
        functools.partial(_ffn_kernel, t_len),
        grid=(nb // bt,),
        in_specs=[row, hist_spec, full(nw), once(w), full(cw), full(cb), once(wd)],
        out_specs=[row, hist_spec],
        out_shape=[jax.ShapeDtypeStruct((nb * t_len, D_MODEL), F32),
                   jax.ShapeDtypeStruct((nb, CONV_W - 1, D_FF), F32)],
        scratch_shapes=_ffn_scratch(tm, t_len),
        compiler_params=pltpu.CompilerParams(dimension_semantics=("arbitrary",),
                                             vmem_limit_bytes=VMEM_LIMIT),
        name="ffn_sample",
    )(x2d, hist, nw, w, cw, cb, wd)


def _head_mean_matrix(width, head_dim):
    idx = np.arange(width) // head_dim
    return jnp.asarray((idx[:, None] == idx[None, :]).astype(np.float32) / head_dim, dtype=BF16)


def _layer_weights(norm_mix_w, w_in, b_gates, q_norm_w, k_norm_w, sinks, ml_norm_w, w_out,
                   norm_ffn_w, w_ffn_in, conv_w, conv_b, w_down):
    w_in_t = jnp.pad(w_in.T.astype(BF16), ((0, IN_WIDTH_PAD - w_in.shape[1]), (0, 0)))
    return dict(
        nw=norm_mix_w.reshape(1, D_MODEL),
        w_in_t=w_in_t,
        bg=jnp.pad(b_gates, (0, LANES - N_GATES)).reshape(1, LANES),
        qnw=(jnp.tile(q_norm_w, ATT_HEADS) * ATT_SCALE).reshape(1, ATT_Q_W),
        knw=jnp.tile(k_norm_w, ATT_KV_HEADS).reshape(1, ATT_KV_W),
        gq=_head_mean_matrix(ATT_Q_W, ATT_HEAD_DIM),
        gk=_head_mean_matrix(ATT_KV_W, ATT_HEAD_DIM),
        bg_col=b_gates.reshape(N_GATES, 1),
        qnw_col=(jnp.tile(q_norm_w, ATT_HEADS) * (ATT_SCALE * LOG2_E)).reshape(ATT_Q_W, 1),
        knw_col=jnp.tile(k_norm_w, ATT_KV_HEADS).reshape(ATT_KV_W, 1),
        mlnw_col=ml_norm_w.reshape(ML_V_W, 1),
        sinks=sinks,
        mlnw=ml_norm_w.reshape(1, ML_V_W),
        wout=w_out.astype(BF16),
        nfw=norm_ffn_w.reshape(1, D_MODEL),
        wff=w_ffn_in.astype(BF16),
        cw=conv_w,
        cb=conv_b.reshape(1, D_FF),
        wd=w_down.astype(BF16),
    )


def _cache_from_t(a_t):
    n = a_t.shape[0]
    return jnp.transpose(a_t.reshape(n, ATT_KV_HEADS, ATT_HEAD_DIM, WINDOW), (0, 3, 1, 2))


def _cache_to_t(a):
    n = a.shape[0]
    return jnp.transpose(a, (0, 2, 3, 1)).reshape(n, ATT_KV_W, WINDOW)


def _prompt_layer(x, w):
    batch, seq, _ = x.shape
    assert seq % INPROJ_TILE == 0 and INPROJ_TILE % MIX_TILE == 0
    assert MIX_TILE % ML_CHUNK == 0 and ML_CHUNK % WINDOW == 0
    assert seq % FFN_TILE == 0
    x2d = x.reshape(batch * seq, D_MODEL)
    qa, ks, kv, qm, km, vm, om, gt = _inproj_t(x2d, w["nw"], w["w_in_t"], w["bg_col"],
                                               w["qnw_col"], w["knw_col"])
    x1, c_t, n_row, m, k_t, v_t = _prompt_mixer_t(batch, seq, w["sinks"], qa, ks, kv, qm, km, vm,
                                                  om, gt, x2d, w["wout"], w["mlnw_col"])
    y, conv = _ffn_prompt(batch, seq, x1, w["nfw"], w["wff"], w["cw"], w["cb"], w["wd"])
    return (y.reshape(batch, seq, D_MODEL), _cache_from_t(k_t), _cache_from_t(v_t),
            jnp.swapaxes(c_t, -1, -2), n_row.reshape(batch, ML_HEADS, ML_QK_DIM),
            m.reshape(batch, ML_HEADS), conv)


def _sample_layer(x, ck, cv, c0, n0, m0, conv_buf, w):
    nb, t_len, _ = x.shape
    assert t_len == SUBLANES and SAMPLE_BT * t_len == LANES and nb % SAMPLE_BT == 0
    assert (nb * t_len) % ROW_TILE == 0
    x2d = x.reshape(nb * t_len, D_MODEL)
    qa, kv, qm, km, vm, om, gt = _inproj(x2d, w["nw"], w["w_in_t"], w["bg"], w["qnw"], w["knw"],
                                         w["gq"], w["gk"])
    x1, nk_t, nv_t, c_t, n, m = _sample_mixer(
        nb, t_len, w["sinks"], qa, kv, _cache_to_t(ck), _cache_to_t(cv), qm, km, vm, om, gt,
        jnp.swapaxes(c0, -1, -2), n0, m0.reshape(nb, 1, ML_HEADS), x2d, w["wout"], w["mlnw"])
    y, conv = _ffn_sample(nb, t_len, x1, conv_buf, w["nfw"], w["wff"], w["cw"], w["cb"],
                          w["wd"])
    return (y.reshape(nb, t_len, D_MODEL), _cache_from_t(nk_t), _cache_from_t(nv_t),
            jnp.swapaxes(c_t, -1, -2), n, m.reshape(nb, ML_HEADS), conv)


def kernel(x_prompt, x_sample, cache_attn_k, cache_attn_v, state_mlstm_C, state_mlstm_n,
           state_mlstm_m, cache_ffn_conv, norm_mix_w, w_in, b_gates, q_norm_w, k_norm_w,
           sinks, ml_norm_w, w_out, norm_ffn_w, w_ffn_in, conv_w, conv_b, w_down):
    depth = w_in.shape[0]
    yp, ys = x_prompt, x_sample
    sp = [[] for _ in range(6)]
    ss = [[] for _ in range(6)]
    for l in range(depth):
        w = _layer_weights(norm_mix_w[l], w_in[l], b_gates[l], q_norm_w[l], k_norm_w[l], sinks[l],
                           ml_norm_w[l], w_out[l], norm_ffn_w[l], w_ffn_in[l], conv_w[l],
                           conv_b[l], w_down[l])
        yp, *st_p = _prompt_layer(yp, w)
        ys, *st_s = _sample_layer(ys, cache_attn_k[l], cache_attn_v[l], state_mlstm_C[l],
                                  state_mlstm_n[l], state_mlstm_m[l], cache_ffn_conv[l], w)
        for i in range(6):
            sp[i].append(st_p[i])
            ss[i].append(st_s[i])
    k_p, v_p, c_p, n_p, m_p, conv_p = [jnp.stack(a) for a in sp]
    k_s, v_s, c_s, n_s, m_s, conv_s = [jnp.stack(a) for a in ss]
    return (yp, ys, k_p, v_p, c_p, n_p, m_p, conv_p, k_s, v_s, c_s, n_s, m_s, conv_s)
```

```python
import functools

import numpy as np
import jax
import jax.numpy as jnp
from jax import lax
from jax.experimental import pallas as pl
from jax.experimental.pallas import tpu as pltpu

F32 = jnp.float32
BF16 = jnp.bfloat16

D_MODEL = 1024
ATT_HEADS = 8
ATT_KV_HEADS = 2
ATT_HEAD_DIM = 64
ATT_GROUP = ATT_HEADS // ATT_KV_HEADS
WINDOW = 128
ML_HEADS = 4
ML_V_DIM = 128
ML_QK_DIM = 64
D_FF = 2816
CONV_W = 3
EPS = 1e-6
ATT_SCALE = ATT_HEAD_DIM ** -0.5
ML_SCALE = ML_QK_DIM ** -0.5
LOG2_E = 1.4426950408889634

ATT_Q_W = ATT_HEADS * ATT_HEAD_DIM
ATT_KV_W = ATT_KV_HEADS * ATT_HEAD_DIM
ML_QK_W = ML_HEADS * ML_QK_DIM
ML_V_W = ML_HEADS * ML_V_DIM
N_GATES = 2 * ML_HEADS
N_STACK = 2 * ATT_GROUP

LANES = 128
SUBLANES = 8

OFF_QA = 0
OFF_KV = OFF_QA + ATT_Q_W
OFF_QM = OFF_KV + 2 * ATT_KV_W
OFF_KM = OFF_QM + ML_QK_W
OFF_VM = OFF_KM + ML_QK_W
OFF_OM = OFF_VM + ML_V_W
OFF_GL = OFF_OM + ML_V_W
IN_WIDTH_PAD = OFF_GL + LANES

ATT_HEAD_ORDER = tuple(h for c in range(ATT_GROUP) for h in (c, c + ATT_GROUP))

ROW_TILE = 512
FFN_TILE = 1024
INPROJ_SUB = 256
NORM_ROWS = 64
INPROJ_TILE = 1024
MIX_TILE = 512
ML_CHUNK = 256
OUT_COLS = 256
FF_CHUNK = 256
SAMPLE_BT = 16
VMEM_LIMIT = 56 * 1024 * 1024


def _dot(a, b):
    return jnp.dot(a, b, preferred_element_type=F32)


def _dot_nt(a, b):
    return lax.dot_general(a, b, (((1,), (1,)), ((), ())), preferred_element_type=F32)


def _split3(x):
    hi = x.astype(BF16)
    r1 = x - hi.astype(F32)
    mid = r1.astype(BF16)
    lo = (r1 - mid.astype(F32)).astype(BF16)
    return hi, mid, lo


def _rms(x, w):
    ms = jnp.mean(x * x, axis=-1, keepdims=True)
    return x * lax.rsqrt(ms + EPS) * w


def _log_sigmoid(x):
    return jnp.minimum(x, 0.0) - jnp.log1p(jnp.exp(-jnp.abs(x)))


def _sigmoid(x):
    return 1.0 / (1.0 + jnp.exp(-x))


def _permute_head_rows(dst_ref, src_ref):
    for k, h in enumerate(ATT_HEAD_ORDER):
        dst_ref[k * ATT_HEAD_DIM:(k + 1) * ATT_HEAD_DIM, :] = (
            src_ref[h * ATT_HEAD_DIM:(h + 1) * ATT_HEAD_DIM, :])


def _inproj_kernel(x_ref, nw_ref, w_ref, bg_ref, qnw_ref, knw_ref, gq_ref, gk_ref,
                   qa_ref, kv_ref, qm_ref, km_ref, vm_ref, om_ref, gt_ref, wq_scr):
    @pl.when(pl.program_id(0) == 0)
    def _():
        _permute_head_rows(wq_scr, w_ref)

    h = _rms(x_ref[...], nw_ref[...]).astype(BF16)

    def proj(lo, width):
        return _dot_nt(h, w_ref[lo:lo + width, :])

    q = _dot_nt(h, wq_scr[...])
    q_ms = _dot((q * q).astype(BF16), gq_ref[...])
    qa_ref[...] = (q * lax.rsqrt(q_ms + EPS) * qnw_ref[...]).astype(BF16)

    kv = proj(OFF_KV, 2 * ATT_KV_W)
    k = kv[:, :ATT_KV_W]
    k_ms = _dot((k * k).astype(BF16), gk_ref[...])
    kv_ref[:, :ATT_KV_W] = k * lax.rsqrt(k_ms + EPS) * knw_ref[...]
    kv_ref[:, ATT_KV_W:] = kv[:, ATT_KV_W:]

    qm_ref[...] = (proj(OFF_QM, ML_QK_W) * ML_SCALE).astype(BF16)
    km_ref[...] = proj(OFF_KM, ML_QK_W).astype(BF16)
    vm_ref[...] = proj(OFF_VM, ML_V_W).astype(BF16)
    om_ref[...] = proj(OFF_OM, ML_V_W).astype(BF16)

    gl = proj(OFF_GL, LANES) + bg_ref[...]
    lane = lax.broadcasted_iota(jnp.int32, gl.shape, 1)
    g = jnp.where(lane < ML_HEADS, gl, _log_sigmoid(gl))
    gt_ref[...] = g.T[:N_GATES, :]


def _inproj(x2d, nw, w_in_t, bg, qnw, knw, gq, gk):
    n = x2d.shape[0]
    tm = ROW_TILE
    row = lambda w: pl.BlockSpec((tm, w), lambda i: (i, 0))
    full = lambda a: pl.BlockSpec(a.shape, lambda i: (0,) * a.ndim)
    once = lambda a: pl.BlockSpec(a.shape, lambda i: (0,) * a.ndim, pipeline_mode=pl.Buffered(1))
    return pl.pallas_call(
        _inproj_kernel,
        grid=(n // tm,),
        in_specs=[row(D_MODEL), full(nw), once(w_in_t), full(bg), full(qnw), full(knw),
                  full(gq), full(gk)],
        out_specs=[row(ATT_Q_W), row(2 * ATT_KV_W), row(ML_QK_W), row(ML_QK_W),
                   row(ML_V_W), row(ML_V_W), pl.BlockSpec((N_GATES, tm), lambda i: (0, i))],
        out_shape=[jax.ShapeDtypeStruct((n, ATT_Q_W), BF16),
                   jax.ShapeDtypeStruct((n, 2 * ATT_KV_W), F32),
                   jax.ShapeDtypeStruct((n, ML_QK_W), BF16),
                   jax.ShapeDtypeStruct((n, ML_QK_W), BF16),
                   jax.ShapeDtypeStruct((n, ML_V_W), BF16),
                   jax.ShapeDtypeStruct((n, ML_V_W), BF16),
                   jax.ShapeDtypeStruct((N_GATES, n), F32)],
        scratch_shapes=[pltpu.VMEM((ATT_Q_W, D_MODEL), BF16)],
        compiler_params=pltpu.CompilerParams(dimension_semantics=("arbitrary",),
                                             vmem_limit_bytes=VMEM_LIMIT),
        name="inproj",
    )(x2d, nw, w_in_t, bg, qnw, knw, gq, gk)


def _head_norm_t(z, head_dim, w_col):
    rows, tokens = z.shape
    z3 = z.reshape(rows // head_dim, head_dim, tokens)
    ms = jnp.mean(z3 * z3, axis=1, keepdims=True)
    return (z3 * lax.rsqrt(ms + EPS)).reshape(rows, tokens) * w_col


def _inproj_t_kernel(x_ref, nw_ref, w_ref, bg_ref, qnw_ref, knw_ref,
                     qa_ref, ks_ref, kv_ref, qm_ref, km_ref, vm_ref, om_ref, gt_ref, h_scr):
    tm = x_ref.shape[0]
    sub = INPROJ_SUB

    def norm_rows(c):
        for r0 in range(c * sub, (c + 1) * sub, NORM_ROWS):
            rows = slice(r0, r0 + NORM_ROWS)
            h_scr[rows, :] = _rms(x_ref[rows, :], nw_ref[...]).astype(BF16)
            yield

    def project(c):
        tok = slice(c * sub, (c + 1) * sub)
        h = h_scr[tok, :]

        def proj(lo, width):
            return _dot_nt(w_ref[lo:lo + width, :], h)

        qa_ref[:, tok] = _head_norm_t(proj(OFF_QA, ATT_Q_W), ATT_HEAD_DIM,
                                      qnw_ref[...]).astype(BF16)
        yield
        kv = proj(OFF_KV, 2 * ATT_KV_W)
        k = _head_norm_t(kv[:ATT_KV_W], ATT_HEAD_DIM, knw_ref[...])
        kv_ref[:ATT_KV_W, tok] = k
        kv_ref[ATT_KV_W:, tok] = kv[ATT_KV_W:]
        ks_ref[tok, :] = k.T.astype(BF16)
        qm_ref[:, tok] = (proj(OFF_QM, ML_QK_W) * ML_SCALE).astype(BF16)
        yield
        km_ref[:, tok] = proj(OFF_KM, ML_QK_W).astype(BF16)
        vm_ref[:, tok] = proj(OFF_VM, ML_V_W).astype(BF16)
        yield
        om_ref[:, tok] = proj(OFF_OM, ML_V_W).astype(BF16)
        gl = proj(OFF_GL, 2 * SUBLANES)[:N_GATES] + bg_ref[...]
        row = lax.broadcasted_iota(jnp.int32, gl.shape, 0)
        gt_ref[:, tok] = jnp.where(row < ML_HEADS, gl, _log_sigmoid(gl))
        yield

    for _ in norm_rows(0):
        pass
    for c in range(tm // sub):
        norms = norm_rows(c + 1) if c + 1 < tm // sub else iter(())
        for _ in project(c):
            next(norms, None)
        for _ in norms:
            pass


def _inproj_t(x2d, nw, w_in_t, bg_col, qnw_col, knw_col):
    n = x2d.shape[0]
    tm = INPROJ_TILE
    full = lambda a: pl.BlockSpec(a.shape, lambda i: (0,) * a.ndim)
    once = lambda a: pl.BlockSpec(a.shape, lambda i: (0,) * a.ndim, pipeline_mode=pl.Buffered(1))
    col = lambda w: pl.BlockSpec((None, w, tm), lambda i: (i, 0, 0))
    slab = lambda w, dt: jax.ShapeDtypeStruct((n // tm, w, tm), dt)
    return pl.pallas_call(
        _inproj_t_kernel,
        grid=(n // tm,),
        in_specs=[pl.BlockSpec((tm, D_MODEL), lambda i: (i, 0)), full(nw), once(w_in_t),
                  full(bg_col), full(qnw_col), full(knw_col)],
        out_specs=[col(ATT_Q_W), pl.BlockSpec((tm, ATT_KV_W), lambda i: (i, 0)),
                   col(2 * ATT_KV_W), col(ML_QK_W), col(ML_QK_W), col(ML_V_W), col(ML_V_W),
                   col(N_GATES)],
        out_shape=[slab(ATT_Q_W, BF16),
                   jax.ShapeDtypeStruct((n, ATT_KV_W), BF16),
                   slab(2 * ATT_KV_W, F32), slab(ML_QK_W, BF16), slab(ML_QK_W, BF16),
                   slab(ML_V_W, BF16), slab(ML_V_W, BF16), slab(N_GATES, F32)],
        scratch_shapes=[pltpu.VMEM((tm, D_MODEL), BF16)],
        compiler_params=pltpu.CompilerParams(dimension_semantics=("arbitrary",),
                                             vmem_limit_bytes=VMEM_LIMIT),
        name="inproj_t",
    )(x2d, nw, w_in_t, bg_col, qnw_col, knw_col)


def _prompt_mixer_t_kernel(sinks_ref, qa_ref, ksc_ref, ksp_ref, kvc_ref, kvp_ref, qm_ref, km_ref,
                           vm_ref, om_ref, gt_ref, x_ref, wout_ref, mlnw_ref,
                           x1_ref, ct_ref, nrow_ref, m_ref, kt_ref, vt_ref,
                           mix_scr, state_scr, m_scr, band_scr, causal_scr, tri_scr,
                           s_scr_a, s_scr_b, e_scr):
    i = pl.program_id(1)
    A = WINDOW
    L = MIX_TILE
    C = ML_CHUNK
    n_pairs = ML_HEADS // 2

    @pl.when(i == 0)
    def _():
        state_scr[...] = jnp.zeros(state_scr.shape, F32)
        m_scr[...] = jnp.zeros(m_scr.shape, F32)
        kj = lax.broadcasted_iota(jnp.int32, (2 * A, A), 0)
        qi = lax.broadcasted_iota(jnp.int32, (2 * A, A), 1)
        band = (kj > qi) & (kj <= qi + WINDOW)
        band_scr[0] = jnp.where(band, 0.0, -jnp.inf)
        band_scr[1] = jnp.where(band & (kj >= A), 0.0, -jnp.inf)
        r = lax.broadcasted_iota(jnp.int32, (C, C), 0)
        c = lax.broadcasted_iota(jnp.int32, (C, C), 1)
        causal_scr[...] = jnp.where(r <= c, 0.0, -jnp.inf)
        tri_scr[...] = (r <= c).astype(F32).astype(BF16)

    k_all = jnp.concatenate([ksp_ref[...], ksc_ref[...]], axis=0)
    v_all = jnp.concatenate([kvp_ref[ATT_KV_W:, :], kvc_ref[ATT_KV_W:, :]], axis=1).astype(BF16)
    zero_q = jnp.zeros((ATT_HEAD_DIM, A), BF16)
    slot = 0
    s_bufs = (s_scr_a, s_scr_b)

    def stage_scores(j):
        pieces = []
        for h in range(ATT_HEADS):
            q_h = qa_ref[h * ATT_HEAD_DIM:(h + 1) * ATT_HEAD_DIM, j * A:(j + 1) * A]
            pieces.append(jnp.concatenate([q_h, zero_q] if h < ATT_GROUP else [zero_q, q_h],
                                          axis=0))
        s_bufs[j % 2][slot] = _dot(k_all[j * A:(j + 2) * A, :], jnp.concatenate(pieces, axis=1))

    def attend(j):
        cols = slice(j * A, (j + 1) * A)
        vt = v_all[:, j * A:(j + 2) * A]
        if j + 1 < L // A:
            stage_scores(j + 1)
        s_buf = s_bufs[j % 2]
        bias = jnp.where(i > 0, band_scr[0], band_scr[1]) if j == 0 else band_scr[0]
        m_rows = []
        for h in range(ATT_HEADS):
            sb = s_buf[slot, :, h * A:(h + 1) * A] + bias
            m_rows.append(jnp.maximum(jnp.max(sb, axis=0, keepdims=True),
                                      sinks_ref[h] * LOG2_E))
        inv_rows = []
        for h in range(ATT_HEADS):
            e = jnp.exp2(s_buf[slot, :, h * A:(h + 1) * A] + (bias - m_rows[h]))
            e_scr[:, h * A:(h + 1) * A] = e.astype(BF16)
            inv_rows.append(1.0 / (jnp.sum(e, axis=0, keepdims=True)
                                   + jnp.exp2(sinks_ref[h] * LOG2_E - m_rows[h])))
        o = _dot(vt, e_scr[...])
        for h in range(ATT_HEADS):
            g = h // ATT_GROUP
            mix_scr[h * ATT_HEAD_DIM:(h + 1) * ATT_HEAD_DIM, cols] = (
                o[g * ATT_HEAD_DIM:(g + 1) * ATT_HEAD_DIM, h * A:(h + 1) * A]
                * inv_rows[h]).astype(BF16)

    row128 = lax.broadcasted_iota(jnp.int32, (LANES, C), 0)
    ones_rows = (row128 == 0).astype(F32).astype(BF16)

    def mlstm_chunk(ci):
        tok = slice(ci * C, (ci + 1) * C)
        gates = gt_ref[:, tok] * LOG2_E
        cum_row = jnp.zeros(gates.shape, F32)
        for part in _split3(gates):
            cum_row = cum_row + _dot(part, tri_scr[...])
        ig_rows = gates[:ML_HEADS]
        b_rows = cum_row[ML_HEADS:]
        key_cols = jnp.concatenate([ig_rows - b_rows, jnp.zeros((LANES - ML_HEADS, C), F32)],
                                   axis=0).T
        for p in range(n_pairs):
            q_c = qm_ref[p * LANES:(p + 1) * LANES, tok]
            k_pair = km_ref[p * LANES:(p + 1) * LANES, tok]
            zero = jnp.zeros_like(q_c)
            state = state_scr[p]
            state_bf = state.astype(BF16)
            new_state = []
            for e_id in range(2):
                h = 2 * p + e_id
                v_rows = slice(h * ML_V_DIM, (h + 1) * ML_V_DIM)
                head_rows = (row128 < ML_QK_DIM) if e_id == 0 else (row128 >= ML_QK_DIM)
                q_pad = jnp.where(head_rows, q_c, zero)
                b_r = b_rows[h:h + 1, :]
                ig_r = ig_rows[h:h + 1, :]
                m_prev = m_scr[h:h + 1, 0:1]
                dm = (b_r + key_cols[:, h:h + 1]) + causal_scr[...]
                inter = b_r + m_prev
                m_row = jnp.maximum(inter, jnp.max(dm, axis=0, keepdims=True))
                w_inter = jnp.exp2(inter - m_row)
                qk = lax.dot_general(k_pair, q_pad, (((0,), (0,)), ((), ())),
                                     preferred_element_type=F32)
                p_t = (qk * jnp.exp2(dm - m_row)).astype(BF16)
                v_ext = jnp.concatenate([vm_ref[v_rows, tok], ones_rows], axis=0)
                num = _dot(v_ext, p_t) + w_inter * _dot(state_bf, q_pad)
                den = num[ML_V_DIM:ML_V_DIM + 1, :]
                hh = num[:ML_V_DIM] * (1.0 / jnp.maximum(jnp.abs(den), jnp.exp2(-m_row)))
                ms = jnp.mean(hh * hh, axis=0, keepdims=True)
                gate = _sigmoid(om_ref[v_rows, tok].astype(F32))
                mix_scr[ATT_Q_W + h * ML_V_DIM:ATT_Q_W + (h + 1) * ML_V_DIM, tok] = (
                    hh * lax.rsqrt(ms + EPS) * mlnw_ref[v_rows, :] * gate).astype(BF16)
                b_last = b_r[:, C - 1:C]
                a_r = b_last - b_r + ig_r
                m_new = jnp.maximum(b_last + m_prev, jnp.max(a_r, axis=-1, keepdims=True))
                sc = jnp.exp2(b_last + m_prev - m_new)
                wsv = (v_ext.astype(F32) * jnp.exp2(a_r - m_new)).astype(BF16)
                new_state.append(sc * state + _dot_nt(wsv, k_pair))
                m_scr[h:h + 1, :] = jnp.broadcast_to(m_new, (1, LANES))
            first = lax.broadcasted_iota(jnp.int32, state.shape, 1) < ML_QK_DIM
            state_scr[p] = jnp.where(first, new_state[0], new_state[1])
            yield

    def out_proj(ci):
        tok = slice(ci * C, (ci + 1) * C)
        mix_t = mix_scr[:, tok].T
        for n in range(D_MODEL // OUT_COLS):
            nc = slice(n * OUT_COLS, (n + 1) * OUT_COLS)
            x1_ref[tok, nc] = x_ref[tok, nc] + _dot(mix_t, wout_ref[:, nc])
            yield

    stage_scores(0)
    n_sub = C // A
    pairs = (step for ci in range(L // C) for step in mlstm_chunk(ci))
    projs = iter(())
    for j in range(L // A):
        if j and j % n_sub == 0:
            projs = out_proj(j // n_sub - 1)
        attend(j)
        next(projs, None)
        next(pairs, None)
        next(projs, None)
    for _ in pairs:
        pass
    for _ in projs:
        pass
    for _ in out_proj(L // C - 1):
        pass

    @pl.when(i == pl.num_programs(1) - 1)
    def _():
        for p in range(n_pairs):
            c_t = state_scr[p, :ML_V_DIM, :].T
            for e_id in range(2):
                ct_ref[0, 2 * p + e_id] = c_t[e_id * ML_QK_DIM:(e_id + 1) * ML_QK_DIM, :]
            nrow_ref[0, p:p + 1, :] = state_scr[p, ML_V_DIM:ML_V_DIM + 1, :]
        for h in range(ML_HEADS):
            m_ref[0, :, h:h + 1] = m_scr[h:h + 1, 0:1] * (1.0 / LOG2_E)
        kt_ref[0] = kvc_ref[:ATT_KV_W, L - WINDOW:]
        vt_ref[0] = kvc_ref[ATT_KV_W:, L - WINDOW:]


def _prompt_mixer_t(batch, seq, sinks, qa, ks, kv, qm, km, vm, om, gt, x2d, wout, mlnw_col):
    tq = MIX_TILE
    nt = seq // tq
    sub = tq // WINDOW
    per_slab = INPROJ_TILE // tq
    win_per_slab = INPROJ_TILE // WINDOW
    col = lambda w: pl.BlockSpec(
        (None, w, tq), lambda b, i: ((b * nt + i) // per_slab, 0, (b * nt + i) % per_slab))
    full = lambda a: pl.BlockSpec(a.shape, lambda b, i: (0,) * a.ndim)
    once = lambda a: pl.BlockSpec(a.shape, lambda b, i: (0,) * a.ndim,
                                  pipeline_mode=pl.Buffered(1))
    prev_block = lambda b, i: jnp.maximum((b * nt + i) * sub - 1, 0)
    per_batch = lambda *dims: pl.BlockSpec((1,) + dims, lambda b, i: (b,) + (0,) * len(dims))
    return pl.pallas_call(
        _prompt_mixer_t_kernel,
        grid=(batch, nt),
        in_specs=[pl.BlockSpec(memory_space=pltpu.SMEM),
                  col(ATT_Q_W),
                  pl.BlockSpec((tq, ATT_KV_W), lambda b, i: (b * nt + i, 0)),
                  pl.BlockSpec((WINDOW, ATT_KV_W), lambda b, i: (prev_block(b, i), 0)),
                  col(2 * ATT_KV_W),
                  pl.BlockSpec((None, 2 * ATT_KV_W, WINDOW),
                               lambda b, i: (prev_block(b, i) // win_per_slab, 0,
                                             prev_block(b, i) % win_per_slab)),
                  col(ML_QK_W), col(ML_QK_W), col(ML_V_W), col(ML_V_W), col(N_GATES),
                  pl.BlockSpec((tq, D_MODEL), lambda b, i: (b * nt + i, 0)),
                  once(wout), full(mlnw_col)],
        out_specs=[pl.BlockSpec((tq, D_MODEL), lambda b, i: (b * nt + i, 0)),
                   per_batch(ML_HEADS, ML_QK_DIM, ML_V_DIM),
                   per_batch(ML_HEADS // 2, LANES),
                   per_batch(1, ML_HEADS),
                   per_batch(ATT_KV_W, WINDOW),
                   per_batch(ATT_KV_W, WINDOW)],
        out_shape=[jax.ShapeDtypeStruct((batch * seq, D_MODEL), F32),
                   jax.ShapeDtypeStruct((batch, ML_HEADS, ML_QK_DIM, ML_V_DIM), F32),
                   jax.ShapeDtypeStruct((batch, ML_HEADS // 2, LANES), F32),
                   jax.ShapeDtypeStruct((batch, 1, ML_HEADS), F32),
                   jax.ShapeDtypeStruct((batch, ATT_KV_W, WINDOW), F32),
                   jax.ShapeDtypeStruct((batch, ATT_KV_W, WINDOW), F32)],
        scratch_shapes=[pltpu.VMEM((D_MODEL, tq), BF16),
                        pltpu.VMEM((ML_HEADS // 2, 2 * LANES, LANES), F32),
                        pltpu.VMEM((SUBLANES, LANES), F32),
                        pltpu.VMEM((2, 2 * WINDOW, WINDOW), F32),
                        pltpu.VMEM((ML_CHUNK, ML_CHUNK), F32),
                        pltpu.VMEM((ML_CHUNK, ML_CHUNK), BF16),
                        pltpu.VMEM((1, 2 * WINDOW, ATT_HEADS * WINDOW), F32),
                        pltpu.VMEM((1, 2 * WINDOW, ATT_HEADS * WINDOW), F32),
                        pltpu.VMEM((2 * WINDOW, ATT_HEADS * WINDOW), BF16)],
        compiler_params=pltpu.CompilerParams(dimension_semantics=("arbitrary", "arbitrary"),
                                             vmem_limit_bytes=VMEM_LIMIT),
        name="prompt_mixer_t",
    )(sinks, qa, ks, ks, kv, kv, qm, km, vm, om, gt, x2d, wout, mlnw_col)


def _sample_mixer_kernel(t_len, sinks_ref, qa_ref, kv_ref, ck_ref, cv_ref, qm_ref, km_ref,
                         vm_ref, om_ref, gt_ref, c0_ref, n0_ref, m0_ref, x_ref, wout_ref,
                         mlnw_ref, x1_ref, nk_ref, nv_ref, c_ref, n_ref, m_ref,
                         mix_scr, wperm_scr):
    bt = SAMPLE_BT
    T = t_len
    L = bt * T

    @pl.when(pl.program_id(0) == 0)
    def _():
        _permute_head_rows(wperm_scr, wout_ref)
        wperm_scr[ATT_Q_W:, :] = wout_ref[ATT_Q_W:, :]

    lane3 = lax.broadcasted_iota(jnp.int32, (bt, T, LANES), 2)
    low3 = lane3 < ATT_HEAD_DIM
    lane = lax.broadcasted_iota(jnp.int32, (L, LANES), 1)
    low = lane < ATT_HEAD_DIM

    qa3 = qa_ref[...].astype(F32).reshape(bt, T, ATT_Q_W)
    pieces = []
    for col in range(ATT_GROUP):
        qc = qa3[:, :, col * LANES:(col + 1) * LANES]
        pieces += [jnp.where(low3, qc, 0.0), jnp.where(low3, 0.0, qc)]
    q3 = jnp.concatenate(pieces, axis=1).astype(BF16)
    R = bt * N_STACK * T
    q2 = q3.reshape(R, LANES)
    kv_new = kv_ref[...]
    k_new = kv_new[:, :ATT_KV_W]
    v_new = kv_new[:, ATT_KV_W:]
    def attention_stages():
        s_c = jnp.einsum('bqd,bdk->bqk', q3, ck_ref[...].astype(BF16),
                         preferred_element_type=F32).reshape(R, WINDOW)
        s_n = _dot_nt(q2, k_new.astype(BF16))
        row_c = lax.broadcasted_iota(jnp.int32, (R, WINDOW), 0)
        col_c = lax.broadcasted_iota(jnp.int32, (R, WINDOW), 1)
        s_c = jnp.where(col_c > row_c % T, s_c, -jnp.inf)
        row_n = lax.broadcasted_iota(jnp.int32, (R, L), 0)
        col_n = lax.broadcasted_iota(jnp.int32, (R, L), 1)
        valid_n = (row_n // (N_STACK * T) == col_n // T) & (col_n % T <= row_n % T)
        s_n = jnp.where(valid_n, s_n, -jnp.inf)
        yield
        stack_id = (lax.broadcasted_iota(jnp.int32, (R, 1), 0) // T) % N_STACK
        sink = jnp.zeros((R, 1), F32)
        for k_id in range(N_STACK):
            sink = jnp.where(stack_id == k_id, sinks_ref[ATT_HEAD_ORDER[k_id]], sink)
        m = jnp.maximum(jnp.maximum(jnp.max(s_c, axis=-1, keepdims=True),
                                    jnp.max(s_n, axis=-1, keepdims=True)), sink)
        yield
        e_c = jnp.exp(s_c - m)
        e_n = jnp.exp(s_n - m)
        denom = (jnp.sum(e_c, axis=-1, keepdims=True) + jnp.sum(e_n, axis=-1, keepdims=True)
                 + jnp.exp(sink - m))
        yield
        o = jnp.einsum('bqk,bdk->bqd', e_c.astype(BF16).reshape(bt, N_STACK * T, WINDOW),
                       cv_ref[...].astype(BF16), preferred_element_type=F32).reshape(R, LANES)
        o = (o + _dot(e_n.astype(BF16), v_new.astype(BF16))) / denom
        yield
        o3 = o.reshape(bt, N_STACK * T, LANES)
        for col in range(ATT_GROUP):
            lo_h = o3[:, (2 * col) * T:(2 * col + 1) * T, :]
            hi_h = o3[:, (2 * col + 1) * T:(2 * col + 2) * T, :]
            mix_scr[:, col * LANES:(col + 1) * LANES] = jnp.where(
                low3, lo_h, hi_h).reshape(L, LANES).astype(BF16)
        yield

    keep = lax.broadcasted_iota(jnp.int32, (ATT_KV_W, WINDOW), 1) < WINDOW - T
    k_new_t = k_new.T
    v_new_t = v_new.T
    def roll_caches():
        for q in range(bt):
            shift = (WINDOW - T - q * T) % WINDOW
            nk_ref[q] = jnp.where(keep, pltpu.roll(ck_ref[q], WINDOW - T, axis=1),
                                  pltpu.roll(k_new_t, shift, axis=1))
            nv_ref[q] = jnp.where(keep, pltpu.roll(cv_ref[q], WINDOW - T, axis=1),
                                  pltpu.roll(v_new_t, shift, axis=1))
            if (q + 1) % (bt // ML_HEADS) == 0:
                yield

    rolls = roll_caches()

    r = lax.broadcasted_iota(jnp.int32, (L, L), 0)
    c = lax.broadcasted_iota(jnp.int32, (L, L), 1)
    seg = (r // T == c // T) & (r <= c)
    seg_bias = jnp.where(seg, 0.0, -jnp.inf)
    seg_bf = seg.astype(F32).astype(BF16)
    gates = gt_ref[...] * LOG2_E
    cum_row = jnp.zeros(gates.shape, F32)
    for part in _split3(gates):
        cum_row = cum_row + _dot(part, seg_bf)
    ig_rows = gates[:ML_HEADS]
    b_rows = cum_row[ML_HEADS:]
    gate_cols = jnp.concatenate([ig_rows, b_rows, jnp.zeros((LANES - N_GATES, L), F32)],
                                axis=0).T

    def col_to_row(x_col):
        return jnp.broadcast_to(x_col, (L, LANES)).T[0:1, :]

    ones_rows = (r[:LANES] == 0).astype(F32).astype(BF16)
    qm = qm_ref[...]
    km = km_ref[...]
    qm_f = qm.astype(F32)
    km_f = km.astype(F32)
    n_rep = bt * ML_QK_DIM // LANES
    bd_row = lax.broadcasted_iota(jnp.int32, (L, bt * ML_QK_DIM), 0) // T
    bd_lane = lax.broadcasted_iota(jnp.int32, (L, bt * ML_QK_DIM), 1) // ML_QK_DIM
    block_diag = bd_row == bd_lane

    def spread(x_pair, e):
        other = pltpu.roll(x_pair, ML_QK_DIM, axis=1)
        twice = jnp.where(low, x_pair, other) if e == 0 else jnp.where(low, other, x_pair)
        return jnp.where(block_diag, jnp.concatenate([twice] * n_rep, axis=1), 0.0).astype(BF16)

    def head_stages(h):
        p, e = divmod(h, 2)
        qc = qm[:, p * LANES:(p + 1) * LANES]
        k_pair = km[:, p * LANES:(p + 1) * LANES]
        zero = jnp.zeros_like(qc)
        q_pad = jnp.where(low, qc, zero) if e == 0 else jnp.where(low, zero, qc)
        v_h = vm_ref[:, h * ML_V_DIM:(h + 1) * ML_V_DIM]
        v_ext_t = jnp.concatenate([v_h.astype(F32).T.astype(BF16), ones_rows], axis=0)
        ig_c = gate_cols[:, h:h + 1]
        b_c = gate_cols[:, ML_HEADS + h:ML_HEADS + h + 1]
        b_r = b_rows[h:h + 1, :]
        yield
        m0 = m0_ref[:, :, h:h + 1] * LOG2_E
        inter = b_r + col_to_row(jnp.broadcast_to(m0, (bt, T, 1)).reshape(L, 1))
        dm = (b_r + (ig_c - b_c)) + seg_bias
        m_row = jnp.maximum(inter, jnp.max(dm, axis=0, keepdims=True))
        w_inter = jnp.exp2(inter - m_row)
        yield
        p_t = (_dot_nt(k_pair, q_pad) * jnp.exp2(dm - m_row)).astype(BF16)
        num_t = _dot(v_ext_t, p_t)
        yield
        q_h3 = qm_f[:, h * ML_QK_DIM:(h + 1) * ML_QK_DIM].reshape(bt, T, ML_QK_DIM)
        k_h3 = km_f[:, h * ML_QK_DIM:(h + 1) * ML_QK_DIM].reshape(bt, T, ML_QK_DIM)
        c0 = c0_ref[:, h]
        n0 = n0_ref[:, h:h + 1, :]
        q_c_t = _dot(spread(qm_f[:, p * LANES:(p + 1) * LANES], e),
                     c0.astype(BF16).reshape(bt * ML_QK_DIM, ML_V_DIM)).T
        q_n_r = col_to_row(jnp.sum(q_h3 * n0, axis=-1, keepdims=True).reshape(L, 1))
        yield
        num =num_t[:ML_V_DIM] + w_inter * q_c_t
        den = num_t[ML_V_DIM:ML_V_DIM + 1] + w_inter * q_n_r
        hh = num * (1.0 / jnp.maximum(jnp.abs(den), jnp.exp2(-m_row)))
        ms = jnp.mean(hh * hh, axis=0, keepdims=True)
        yield
        mix_scr[:, ATT_Q_W + h * ML_V_DIM:ATT_Q_W + (h + 1) * ML_V_DIM] = (
            (hh * lax.rsqrt(ms + EPS)).T * mlnw_ref[:, h * ML_V_DIM:(h + 1) * ML_V_DIM]
            * _sigmoid(om_ref[:, h * ML_V_DIM:(h + 1) * ML_V_DIM].astype(F32))).astype(BF16)
        yield
        b3 = b_c.reshape(bt, T, 1)
        b_last = b3[:, T - 1:T, :]
        a3 = b_last - b3 + ig_c.reshape(bt, T, 1)
        m_new = jnp.maximum(b_last + m0, jnp.max(a3, axis=1, keepdims=True))
        sc = jnp.exp2(b_last + m0 - m_new)
        ws = jnp.exp2(a3 - m_new)
        yield
        kw = spread(km_f[:, p * LANES:(p + 1) * LANES] * ws.reshape(L, 1), e)
        d_c = lax.dot_general(kw, v_h, (((0,), (0,)), ((), ())), preferred_element_type=F32)
        c_ref[:, h] = sc * c0 + d_c.reshape(bt, ML_QK_DIM, ML_V_DIM)
        n_ref[:, h:h + 1, :] = sc * n0 + jnp.sum(ws * k_h3, axis=1, keepdims=True)
        m_ref[:, :, h:h + 1] = m_new * (1.0 / LOG2_E)
        yield

    att = attention_stages()
    for _ in zip(*[head_stages(h) for h in range(ML_HEADS)]):
        next(att, None)
        next(rolls, None)
    for _ in att:
        pass
    for _ in rolls:
        pass

    x1_ref[...] = x_ref[...] + _dot(mix_scr[...], wperm_scr[...])


def _sample_mixer(nb, t_len, sinks, qa, kv, ck, cv, qm, km, vm, om, gt, c0, n0, m0, x2d, wout, mlnw):
    bt = SAMPLE_BT
    tl = bt * t_len
    row = lambda w: pl.BlockSpec((tl, w), lambda i: (i, 0))
    full = lambda a: pl.BlockSpec(a.shape, lambda i: (0,) * a.ndim)
    once = lambda a: pl.BlockSpec(a.shape, lambda i: (0,) * a.ndim, pipeline_mode=pl.Buffered(1))
    cache = pl.BlockSpec((bt, ATT_KV_W, WINDOW), lambda i: (i, 0, 0))
    c_spec = pl.BlockSpec((bt, ML_HEADS, ML_QK_DIM, ML_V_DIM), lambda i: (i, 0, 0, 0))
    n_spec = pl.BlockSpec((bt, ML_HEADS, ML_QK_DIM), lambda i: (i, 0, 0))
    m_spec = pl.BlockSpec((bt, 1, ML_HEADS), lambda i: (i, 0, 0))
    return pl.pallas_call(
        functools.partial(_sample_mixer_kernel, t_len),
        grid=(nb // bt,),
        in_specs=[pl.BlockSpec(memory_space=pltpu.SMEM),
                  row(ATT_Q_W), row(2 * ATT_KV_W), cache, cache, row(ML_QK_W), row(ML_QK_W),
                  row(ML_V_W), row(ML_V_W), pl.BlockSpec((N_GATES, tl), lambda i: (0, i)),
                  c_spec, n_spec, m_spec, row(D_MODEL), once(wout), full(mlnw)],
        out_specs=[row(D_MODEL), cache, cache, c_spec, n_spec, m_spec],
        out_shape=[jax.ShapeDtypeStruct((nb * t_len, D_MODEL), F32),
                   jax.ShapeDtypeStruct((nb, ATT_KV_W, WINDOW), F32),
                   jax.ShapeDtypeStruct((nb, ATT_KV_W, WINDOW), F32),
                   jax.ShapeDtypeStruct((nb, ML_HEADS, ML_QK_DIM, ML_V_DIM), F32),
                   jax.ShapeDtypeStruct((nb, ML_HEADS, ML_QK_DIM), F32),
                   jax.ShapeDtypeStruct((nb, 1, ML_HEADS), F32)],
        scratch_shapes=[pltpu.VMEM((tl, D_MODEL), BF16),
                        pltpu.VMEM((D_MODEL, D_MODEL), BF16)],
        compiler_params=pltpu.CompilerParams(dimension_semantics=("arbitrary",),
                                             vmem_limit_bytes=VMEM_LIMIT),
        name="sample_mixer",
    )(sinks, qa, kv, ck, cv, qm, km, vm, om, gt, c0, n0, m0, x2d, wout, mlnw)


def _ffn_kernel(seq_rows, *refs):
    if seq_rows is None:
        (x_ref, nw_ref, w_ref, cw_ref, cb_ref, wd_ref, y_ref, conv_ref,
         gbuf, act_scr, carry) = refs
        hist_ref = None
    else:
        (x_ref, hist_ref, nw_ref, w_ref, cw_ref, cb_ref, wd_ref, y_ref, conv_ref,
         gbuf, act_scr) = refs
        carry = None
    tm = x_ref.shape[0]
    tf = FF_CHUNK
    n_hist = CONV_W - 1
    rows = tm if seq_rows is None else seq_rows
    nseq = tm // rows
    base = SUBLANES
    n_chunks = D_FF // tf

    if carry is not None:
        @pl.when(pl.program_id(1) == 0)
        def _():
            carry[...] = jnp.zeros(carry.shape, F32)

    x = x_ref[...]
    h2 = _rms(x, nw_ref[...]).astype(BF16)

    def proj(f):
        return (_dot(h2, w_ref[:, f * tf:(f + 1) * tf]),
                _dot(h2, w_ref[:, D_FF + f * tf:D_FF + (f + 1) * tf]))

    nxt = proj(0)
    for f in range(n_chunks):
        g, u = nxt
        if f + 1 < n_chunks:
            nxt = proj(f + 1)
        cols = slice(f * tf, (f + 1) * tf)
        s = f % 2
        g3 = g.reshape(nseq, rows, tf)
        if seq_rows is None:
            gbuf[s, :, base - n_hist:base, :] = carry[:, SUBLANES - n_hist:, cols]
            carry[:, SUBLANES - n_hist:, cols] = g3[:, rows - n_hist:, :]
        else:
            gbuf[s, :, base - n_hist:base, :] = hist_ref[:, :, cols]
            conv_ref[:, :, cols] = g3[:, rows - n_hist:, :]
        gbuf[s, :, base:base + rows, :] = g3
        gc = cb_ref[:, cols] + g * cw_ref[CONV_W - 1:CONV_W, cols]
        for d in range(1, CONV_W):
            gm = gbuf[s, :, base - d:base - d + rows, :].reshape(tm, tf)
            gc = gc + gm * cw_ref[CONV_W - 1 - d:CONV_W - d, cols]
        act_scr[:, cols] = (gc * _sigmoid(gc) * u).astype(BF16)
    y_ref[...] = x + _dot(act_scr[...], wd_ref[...])

    if carry is not None:
        @pl.when(pl.program_id(1) == pl.num_programs(1) - 1)
        def _():
            conv_ref[...] = carry[:, SUBLANES - n_hist:, :]


def _ffn_scratch(tm, rows):
    return [pltpu.VMEM((2, tm // rows, SUBLANES + rows, FF_CHUNK), F32),
            pltpu.VMEM((tm, D_FF), BF16)]


def _ffn_prompt(batch, seq, x2d, nw, w, cw, cb, wd):
    tm = FFN_TILE
    nt = seq // tm
    full = lambda a: pl.BlockSpec(a.shape, lambda b, i: (0,) * a.ndim)
    once = lambda a: pl.BlockSpec(a.shape, lambda b, i: (0,) * a.ndim,
                                  pipeline_mode=pl.Buffered(1))
    row = pl.BlockSpec((tm, D_MODEL), lambda b, i: (b * nt + i, 0))
    return pl.pallas_call(
        functools.partial(_ffn_kernel, None),
        grid=(batch, nt),
        in_specs=[row, full(nw), once(w), full(cw), full(cb), once(wd)],
        out_specs=[row, pl.BlockSpec((1, CONV_W - 1, D_FF), lambda b, i: (b, 0, 0))],
        out_shape=[jax.ShapeDtypeStruct((batch * seq, D_MODEL), F32),
                   jax.ShapeDtypeStruct((batch, CONV_W - 1, D_FF), F32)],
        scratch_shapes=_ffn_scratch(tm, tm) + [pltpu.VMEM((1, SUBLANES, D_FF), F32)],
        compiler_params=pltpu.CompilerParams(dimension_semantics=("arbitrary", "arbitrary"),
                                             vmem_limit_bytes=VMEM_LIMIT),
        name="ffn_prompt",
    )(x2d, nw, w, cw, cb, wd)


def _ffn_sample(nb, t_len, x2d, hist, nw, w, cw, cb, wd):
    tm = ROW_TILE
    bt = tm // t_len
    full = lambda a: pl.BlockSpec(a.shape, lambda i: (0,) * a.ndim)
    once = lambda a: pl.BlockSpec(a.shape, lambda i: (0,) * a.ndim, pipeline_mode=pl.Buffered(1))
    row = pl.BlockSpec((tm, D_MODEL), lambda i: (i, 0))
    hist_spec = pl.BlockSpec((bt, CONV_W - 1, D_FF), lambda i: (i, 0, 0))
    return pl.pallas_call(
        functools.partial(_ffn_kernel, t_len),
        grid=(nb // bt,),
        in_specs=[row, hist_spec, full(nw), once(w), full(cw), full(cb), once(wd)],
        out_specs=[row, hist_spec],
        out_shape=[jax.ShapeDtypeStruct((nb * t_len, D_MODEL), F32),
                   jax.ShapeDtypeStruct((nb, CONV_W - 1, D_FF), F32)],
        scratch_shapes=_ffn_scratch(tm, t_len),
        compiler_params=pltpu.CompilerParams(dimension_semantics=("arbitrary",),
                                             vmem_limit_bytes=VMEM_LIMIT),
        name="ffn_sample",
    )(x2d, hist, nw, w, cw, cb, wd)


def _head_mean_matrix(width, head_dim):
    idx = np.arange(width) // head_dim
    return jnp.asarray((idx[:, None] == idx[None, :]).astype(np.float32) / head_dim, dtype=BF16)


def _layer_weights(norm_mix_w, w_in, b_gates, q_norm_w, k_norm_w, sinks, ml_norm_w, w_out,
                   norm_ffn_w, w_ffn_in, conv_w, conv_b, w_down):
    w_in_t = jnp.pad(w_in.T.astype(BF16), ((0, IN_WIDTH_PAD - w_in.shape[1]), (0, 0)))
    return dict(
        nw=norm_mix_w.reshape(1, D_MODEL),
        w_in_t=w_in_t,
        bg=jnp.pad(b_gates, (0, LANES - N_GATES)).reshape(1, LANES),
        qnw=(jnp.tile(q_norm_w, ATT_HEADS) * ATT_SCALE).reshape(1, ATT_Q_W),
        knw=jnp.tile(k_norm_w, ATT_KV_HEADS).reshape(1, ATT_KV_W),
        gq=_head_mean_matrix(ATT_Q_W, ATT_HEAD_DIM),
        gk=_head_mean_matrix(ATT_KV_W, ATT_HEAD_DIM),
        bg_col=b_gates.reshape(N_GATES, 1),
        qnw_col=(jnp.tile(q_norm_w, ATT_HEADS) * (ATT_SCALE * LOG2_E)).reshape(ATT_Q_W, 1),
        knw_col=jnp.tile(k_norm_w, ATT_KV_HEADS).reshape(ATT_KV_W, 1),
        mlnw_col=ml_norm_w.reshape(ML_V_W, 1),
        sinks=sinks,
        mlnw=ml_norm_w.reshape(1, ML_V_W),
        wout=w_out.astype(BF16),
        nfw=norm_ffn_w.reshape(1, D_MODEL),
        wff=w_ffn_in.astype(BF16),
        cw=conv_w,
        cb=conv_b.reshape(1, D_FF),
        wd=w_down.astype(BF16),
    )


def _cache_from_t(a_t):
    n = a_t.shape[0]
    return jnp.transpose(a_t.reshape(n, ATT_KV_HEADS, ATT_HEAD_DIM, WINDOW), (0, 3, 1, 2))


def _cache_to_t(a):
    n = a.shape[0]
    return jnp.transpose(a, (0, 2, 3, 1)).reshape(n, ATT_KV_W, WINDOW)


def _prompt_layer(x, w):
    batch, seq, _ = x.shape
    assert seq % INPROJ_TILE == 0 and INPROJ_TILE % MIX_TILE == 0
    assert MIX_TILE % ML_CHUNK == 0 and ML_CHUNK % WINDOW == 0
    assert seq % FFN_TILE == 0
    x2d = x.reshape(batch * seq, D_MODEL)
    qa, ks, kv, qm, km, vm, om, gt = _inproj_t(x2d, w["nw"], w["w_in_t"], w["bg_col"],
                                               w["qnw_col"], w["knw_col"])
    x1, c_t, n_row, m, k_t, v_t = _prompt_mixer_t(batch, seq, w["sinks"], qa, ks, kv, qm, km, vm,
                                                  om, gt, x2d, w["wout"], w["mlnw_col"])
    y, conv = _ffn_prompt(batch, seq, x1, w["nfw"], w["wff"], w["cw"], w["cb"], w["wd"])
    return (y.reshape(batch, seq, D_MODEL), _cache_from_t(k_t), _cache_from_t(v_t),
            jnp.swapaxes(c_t, -1, -2), n_row.reshape(batch, ML_HEADS, ML_QK_DIM),
            m.reshape(batch, ML_HEADS), conv)


def _sample_layer(x, ck, cv, c0, n0, m0, conv_buf, w):
    nb, t_len, _ = x.shape
    assert t_len == SUBLANES and SAMPLE_BT * t_len == LANES and nb % SAMPLE_BT == 0
    assert (nb * t_len) % ROW_TILE == 0
    x2d = x.reshape(nb * t_len, D_MODEL)
    qa, kv, qm, km, vm, om, gt = _inproj(x2d, w["nw"], w["w_in_t"], w["bg"], w["qnw"], w["knw"],
                                         w["gq"], w["gk"])
    x1, nk_t, nv_t, c_t, n, m = _sample_mixer(
        nb, t_len, w["sinks"], qa, kv, _cache_to_t(ck), _cache_to_t(cv), qm, km, vm, om, gt,
        jnp.swapaxes(c0, -1, -2), n0, m0.reshape(nb, 1, ML_HEADS), x2d, w["wout"], w["mlnw"])
    y, conv = _ffn_sample(nb, t_len, x1, conv_buf, w["nfw"], w["wff"], w["cw"], w["cb"],
                          w["wd"])
    return (y.reshape(nb, t_len, D_MODEL), _cache_from_t(nk_t), _cache_from_t(nv_t),
            jnp.swapaxes(c_t, -1, -2), n, m.reshape(nb, ML_HEADS), conv)


def kernel(x_prompt, x_sample, cache_attn_k, cache_attn_v, state_mlstm_C, state_mlstm_n,
           state_mlstm_m, cache_ffn_conv, norm_mix_w, w_in, b_gates, q_norm_w, k_norm_w,
           sinks, ml_norm_w, w_out, norm_ffn_w, w_ffn_in, conv_w, conv_b, w_down):
    depth = w_in.shape[0]
    yp, ys = x_prompt, x_sample
    sp = [[] for _ in range(6)]
    ss = [[] for _ in range(6)]
    for l in range(depth):
        w = _layer_weights(norm_mix_w[l], w_in[l], b_gates[l], q_norm_w[l], k_norm_w[l], sinks[l],
                           ml_norm_w[l], w_out[l], norm_ffn_w[l], w_ffn_in[l], conv_w[l],
                           conv_b[l], w_down[l])
        yp, *st_p = _prompt_layer(yp, w)
        ys, *st_s = _sample_layer(ys, cache_attn_k[l], cache_attn_v[l], state_mlstm_C[l],
                                  state_mlstm_n[l], state_mlstm_m[l], cache_ffn_conv[l], w)
        for i in range(6):
            sp[i].append(st_p[i])
            ss[i].append(st_s[i])
    k_p, v_p, c_p, n_p, m_p, conv_p = [jnp.stack(a) for a in sp]
    k_s, v_s, c_s, n_s, m_s, conv_s = [jnp.stack(a) for a in ss]
    return (yp, ys, k_p, v_p, c_p, n_p, m_p, conv_p, k_s, v_s, c_s, n_s, m_s, conv_s)
```

```python
import functools

import numpy as np
import jax
import jax.numpy as jnp
from jax import lax
from jax.experimental import pallas as pl
from jax.experimental.pallas import tpu as pltpu

F32 = jnp.float32
BF16 = jnp.bfloat16

D_MODEL = 1024
ATT_HEADS = 8
ATT_KV_HEADS = 2
ATT_HEAD_DIM = 64
ATT_GROUP = ATT_HEADS // ATT_KV_HEADS
WINDOW = 128
ML_HEADS = 4
ML_V_DIM = 128
ML_QK_DIM = 64
D_FF = 2816
CONV_W = 3
EPS = 1e-6
ATT_SCALE = ATT_HEAD_DIM ** -0.5
ML_SCALE = ML_QK_DIM ** -0.5
LOG2_E = 1.4426950408889634

ATT_Q_W = ATT_HEADS * ATT_HEAD_DIM
ATT_KV_W = ATT_KV_HEADS * ATT_HEAD_DIM
ML_QK_W = ML_HEADS * ML_QK_DIM
ML_V_W = ML_HEADS * ML_V_DIM
N_GATES = 2 * ML_HEADS
N_STACK = 2 * ATT_GROUP

LANES = 128
SUBLANES = 8

OFF_QA = 0
OFF_KV = OFF_QA + ATT_Q_W
OFF_QM = OFF_KV + 2 * ATT_KV_W
OFF_KM = OFF_QM + ML_QK_W
OFF_VM = OFF_KM + ML_QK_W
OFF_OM = OFF_VM + ML_V_W
OFF_GL = OFF_OM + ML_V_W
IN_WIDTH_PAD = OFF_GL + LANES

ATT_HEAD_ORDER = tuple(h for c in range(ATT_GROUP) for h in (c, c + ATT_GROUP))

ROW_TILE = 512
FFN_TILE = 1024
FFN_NORM_BLOCKS = 4
INPROJ_SUB = 256
NORM_ROWS = 64
INPROJ_TILE = 1024
MIX_TILE = 512
ML_CHUNK = 256
OUT_COLS = 256
FF_CHUNK = 256
SAMPLE_BT = 16
VMEM_LIMIT = 56 * 1024 * 1024


def _dot(a, b):
    return jnp.dot(a, b, preferred_element_type=F32)


def _dot_nt(a, b):
    return lax.dot_general(a, b, (((1,), (1,)), ((), ())), preferred_element_type=F32)


def _split3(x):
    hi = x.astype(BF16)
    r1 = x - hi.astype(F32)
    mid = r1.astype(BF16)
    lo = (r1 - mid.astype(F32)).astype(BF16)
    return hi, mid, lo


def _rms(x, w):
    ms = jnp.mean(x * x, axis=-1, keepdims=True)
    return x * lax.rsqrt(ms + EPS) * w


def _log_sigmoid(x):
    return jnp.minimum(x, 0.0) - jnp.log1p(jnp.exp(-jnp.abs(x)))


def _sigmoid(x):
    return 1.0 / (1.0 + jnp.exp(-x))


def _permute_head_rows(dst_ref, src_ref):
    for k, h in enumerate(ATT_HEAD_ORDER):
        dst_ref[k * ATT_HEAD_DIM:(k + 1) * ATT_HEAD_DIM, :] = (
            src_ref[h * ATT_HEAD_DIM:(h + 1) * ATT_HEAD_DIM, :])


def _inproj_kernel(x_ref, nw_ref, w_ref, bg_ref, qnw_ref, knw_ref, gq_ref, gk_ref,
                   qa_ref, kv_ref, qm_ref, km_ref, vm_ref, om_ref, gt_ref, wq_scr):
    @pl.when(pl.program_id(0) == 0)
    def _():
        _permute_head_rows(wq_scr, w_ref)

    h = _rms(x_ref[...], nw_ref[...]).astype(BF16)

    def proj(lo, width):
        return _dot_nt(h, w_ref[lo:lo + width, :])

    q = _dot_nt(h, wq_scr[...])
    q_ms = _dot((q * q).astype(BF16), gq_ref[...])
    qa_ref[...] = (q * lax.rsqrt(q_ms + EPS) * qnw_ref[...]).astype(BF16)

    kv = proj(OFF_KV, 2 * ATT_KV_W)
    k = kv[:, :ATT_KV_W]
    k_ms = _dot((k * k).astype(BF16), gk_ref[...])
    kv_ref[:, :ATT_KV_W] = k * lax.rsqrt(k_ms + EPS) * knw_ref[...]
    kv_ref[:, ATT_KV_W:] = kv[:, ATT_KV_W:]

    qm_ref[...] = (proj(OFF_QM, ML_QK_W) * ML_SCALE).astype(BF16)
    km_ref[...] = proj(OFF_KM, ML_QK_W).astype(BF16)
    vm_ref[...] = proj(OFF_VM, ML_V_W).astype(BF16)
    om_ref[...] = proj(OFF_OM, ML_V_W).astype(BF16)

    gl = proj(OFF_GL, LANES) + bg_ref[...]
    lane = lax.broadcasted_iota(jnp.int32, gl.shape, 1)
    g = jnp.where(lane < ML_HEADS, gl, _log_sigmoid(gl))
    gt_ref[...] = g.T[:N_GATES, :]


def _inproj(x2d, nw, w_in_t, bg, qnw, knw, gq, gk):
    n = x2d.shape[0]
    tm = ROW_TILE
    row = lambda w: pl.BlockSpec((tm, w), lambda i: (i, 0))
    full = lambda a: pl.BlockSpec(a.shape, lambda i: (0,) * a.ndim)
    once = lambda a: pl.BlockSpec(a.shape, lambda i: (0,) * a.ndim, pipeline_mode=pl.Buffered(1))
    return pl.pallas_call(
        _inproj_kernel,
        grid=(n // tm,),
        in_specs=[row(D_MODEL), full(nw), once(w_in_t), full(bg), full(qnw), full(knw),
                  full(gq), full(gk)],
        out_specs=[row(ATT_Q_W), row(2 * ATT_KV_W), row(ML_QK_W), row(ML_QK_W),
                   row(ML_V_W), row(ML_V_W), pl.BlockSpec((N_GATES, tm), lambda i: (0, i))],
        out_shape=[jax.ShapeDtypeStruct((n, ATT_Q_W), BF16),
                   jax.ShapeDtypeStruct((n, 2 * ATT_KV_W), F32),
                   jax.ShapeDtypeStruct((n, ML_QK_W), BF16),
                   jax.ShapeDtypeStruct((n, ML_QK_W), BF16),
                   jax.ShapeDtypeStruct((n, ML_V_W), BF16),
                   jax.ShapeDtypeStruct((n, ML_V_W), BF16),
                   jax.ShapeDtypeStruct((N_GATES, n), F32)],
        scratch_shapes=[pltpu.VMEM((ATT_Q_W, D_MODEL), BF16)],
        compiler_params=pltpu.CompilerParams(dimension_semantics=("arbitrary",),
                                             vmem_limit_bytes=VMEM_LIMIT),
        name="inproj",
    )(x2d, nw, w_in_t, bg, qnw, knw, gq, gk)


def _head_norm_t(z, head_dim, w_col):
    rows, tokens = z.shape
    z3 = z.reshape(rows // head_dim, head_dim, tokens)
    ms = jnp.mean(z3 * z3, axis=1, keepdims=True)
    return (z3 * lax.rsqrt(ms + EPS)).reshape(rows, tokens) * w_col


def _inproj_t_kernel(x_ref, nw_ref, w_ref, bg_ref, qnw_ref, knw_ref,
                     qa_ref, ks_ref, kv_ref, qm_ref, km_ref, vm_ref, om_ref, gt_ref, h_scr):
    tm = x_ref.shape[0]
    sub = INPROJ_SUB

    def norm_rows(c):
        for r0 in range(c * sub, (c + 1) * sub, NORM_ROWS):
            rows = slice(r0, r0 + NORM_ROWS)
            h_scr[rows, :] = _rms(x_ref[rows, :], nw_ref[...]).astype(BF16)
            yield

    def project(c):
        tok = slice(c * sub, (c + 1) * sub)
        h = h_scr[tok, :]

        def proj(lo, width):
            return _dot_nt(w_ref[lo:lo + width, :], h)

        qa_ref[:, tok] = _head_norm_t(proj(OFF_QA, ATT_Q_W), ATT_HEAD_DIM,
                                      qnw_ref[...]).astype(BF16)
        yield
        kv = proj(OFF_KV, 2 * ATT_KV_W)
        k = _head_norm_t(kv[:ATT_KV_W], ATT_HEAD_DIM, knw_ref[...])
        kv_ref[:ATT_KV_W, tok] = k
        kv_ref[ATT_KV_W:, tok] = kv[ATT_KV_W:]
        ks_ref[tok, :] = k.T.astype(BF16)
        qm_ref[:, tok] = (proj(OFF_QM, ML_QK_W) * ML_SCALE).astype(BF16)
        yield
        km_ref[:, tok] = proj(OFF_KM, ML_QK_W).astype(BF16)
        vm_ref[:, tok] = proj(OFF_VM, ML_V_W).astype(BF16)
        yield
        om_ref[:, tok] = proj(OFF_OM, ML_V_W).astype(BF16)
        gl = proj(OFF_GL, 2 * SUBLANES)[:N_GATES] + bg_ref[...]
        row = lax.broadcasted_iota(jnp.int32, gl.shape, 0)
        gt_ref[:, tok] = jnp.where(row < ML_HEADS, gl, _log_sigmoid(gl))
        yield

    for _ in norm_rows(0):
        pass
    for c in range(tm // sub):
        norms = norm_rows(c + 1) if c + 1 < tm // sub else iter(())
        for _ in project(c):
            next(norms, None)
        for _ in norms:
            pass


def _inproj_t(x2d, nw, w_in_t, bg_col, qnw_col, knw_col):
    n = x2d.shape[0]
    tm = INPROJ_TILE
    full = lambda a: pl.BlockSpec(a.shape, lambda i: (0,) * a.ndim)
    once = lambda a: pl.BlockSpec(a.shape, lambda i: (0,) * a.ndim, pipeline_mode=pl.Buffered(1))
    col = lambda w: pl.BlockSpec((None, w, tm), lambda i: (i, 0, 0))
    slab = lambda w, dt: jax.ShapeDtypeStruct((n // tm, w, tm), dt)
    return pl.pallas_call(
        _inproj_t_kernel,
        grid=(n // tm,),
        in_specs=[pl.BlockSpec((tm, D_MODEL), lambda i: (i, 0)), full(nw), once(w_in_t),
                  full(bg_col), full(qnw_col), full(knw_col)],
        out_specs=[col(ATT_Q_W), pl.BlockSpec((tm, ATT_KV_W), lambda i: (i, 0)),
                   col(2 * ATT_KV_W), col(ML_QK_W), col(ML_QK_W), col(ML_V_W), col(ML_V_W),
                   col(N_GATES)],
        out_shape=[slab(ATT_Q_W, BF16),
                   jax.ShapeDtypeStruct((n, ATT_KV_W), BF16),
                   slab(2 * ATT_KV_W, F32), slab(ML_QK_W, BF16), slab(ML_QK_W, BF16),
                   slab(ML_V_W, BF16), slab(ML_V_W, BF16), slab(N_GATES, F32)],
        scratch_shapes=[pltpu.VMEM((tm, D_MODEL), BF16)],
        compiler_params=pltpu.CompilerParams(dimension_semantics=("arbitrary",),
                                             vmem_limit_bytes=VMEM_LIMIT),
        name="inproj_t",
    )(x2d, nw, w_in_t, bg_col, qnw_col, knw_col)


def _prompt_mixer_t_kernel(sinks_ref, qa_ref, ksc_ref, ksp_ref, kvc_ref, kvp_ref, qm_ref, km_ref,
                           vm_ref, om_ref, gt_ref, x_ref, wout_ref, mlnw_ref,
                           x1_ref, ct_ref, nrow_ref, m_ref, kt_ref, vt_ref,
                           mix_scr, state_scr, m_scr, band_scr, causal_scr, tri_scr,
                           s_scr_a, s_scr_b, e_scr):
    i = pl.program_id(1)
    A = WINDOW
    L = MIX_TILE
    C = ML_CHUNK
    n_pairs = ML_HEADS // 2

    @pl.when(i == 0)
    def _():
        state_scr[...] = jnp.zeros(state_scr.shape, F32)
        m_scr[...] = jnp.zeros(m_scr.shape, F32)
        kj = lax.broadcasted_iota(jnp.int32, (2 * A, A), 0)
        qi = lax.broadcasted_iota(jnp.int32, (2 * A, A), 1)
        band = (kj > qi) & (kj <= qi + WINDOW)
        band_scr[0] = jnp.where(band, 0.0, -jnp.inf)
        band_scr[1] = jnp.where(band & (kj >= A), 0.0, -jnp.inf)
        r = lax.broadcasted_iota(jnp.int32, (C, C), 0)
        c = lax.broadcasted_iota(jnp.int32, (C, C), 1)
        causal_scr[...] = jnp.where(r <= c, 0.0, -jnp.inf)
        tri_scr[...] = (r <= c).astype(F32).astype(BF16)

    k_all = jnp.concatenate([ksp_ref[...], ksc_ref[...]], axis=0)
    v_all = jnp.concatenate([kvp_ref[ATT_KV_W:, :], kvc_ref[ATT_KV_W:, :]], axis=1).astype(BF16)
    zero_q = jnp.zeros((ATT_HEAD_DIM, A), BF16)
    slot = 0
    s_bufs = (s_scr_a, s_scr_b)

    def stage_scores(j):
        pieces = []
        for h in range(ATT_HEADS):
            q_h = qa_ref[h * ATT_HEAD_DIM:(h + 1) * ATT_HEAD_DIM, j * A:(j + 1) * A]
            pieces.append(jnp.concatenate([q_h, zero_q] if h < ATT_GROUP else [zero_q, q_h],
                                          axis=0))
        s_bufs[j % 2][slot] = _dot(k_all[j * A:(j + 2) * A, :], jnp.concatenate(pieces, axis=1))

    def attend(j):
        cols = slice(j * A, (j + 1) * A)
        vt = v_all[:, j * A:(j + 2) * A]
        if j + 1 < L // A:
            stage_scores(j + 1)
        s_buf = s_bufs[j % 2]
        bias = jnp.where(i > 0, band_scr[0], band_scr[1]) if j == 0 else band_scr[0]
        m_rows = []
        for h in range(ATT_HEADS):
            sb = s_buf[slot, :, h * A:(h + 1) * A] + bias
            m_rows.append(jnp.maximum(jnp.max(sb, axis=0, keepdims=True),
                                      sinks_ref[h] * LOG2_E))
        inv_rows = []
        for h in range(ATT_HEADS):
            e = jnp.exp2(s_buf[slot, :, h * A:(h + 1) * A] + (bias - m_rows[h]))
            e_scr[:, h * A:(h + 1) * A] = e.astype(BF16)
            inv_rows.append(1.0 / (jnp.sum(e, axis=0, keepdims=True)
                                   + jnp.exp2(sinks_ref[h] * LOG2_E - m_rows[h])))
        o = _dot(vt, e_scr[...])
        for h in range(ATT_HEADS):
            g = h // ATT_GROUP
            mix_scr[h * ATT_HEAD_DIM:(h + 1) * ATT_HEAD_DIM, cols] = (
                o[g * ATT_HEAD_DIM:(g + 1) * ATT_HEAD_DIM, h * A:(h + 1) * A]
                * inv_rows[h]).astype(BF16)

    row128 = lax.broadcasted_iota(jnp.int32, (LANES, C), 0)
    ones_rows = (row128 == 0).astype(F32).astype(BF16)

    def mlstm_chunk(ci):
        tok = slice(ci * C, (ci + 1) * C)
        gates = gt_ref[:, tok] * LOG2_E
        cum_row = jnp.zeros(gates.shape, F32)
        for part in _split3(gates):
            cum_row = cum_row + _dot(part, tri_scr[...])
        ig_rows = gates[:ML_HEADS]
        b_rows = cum_row[ML_HEADS:]
        key_cols = jnp.concatenate([ig_rows - b_rows, jnp.zeros((LANES - ML_HEADS, C), F32)],
                                   axis=0).T
        for p in range(n_pairs):
            q_c = qm_ref[p * LANES:(p + 1) * LANES, tok]
            k_pair = km_ref[p * LANES:(p + 1) * LANES, tok]
            zero = jnp.zeros_like(q_c)
            state = state_scr[p]
            state_bf = state.astype(BF16)
            new_state = []
            for e_id in range(2):
                h = 2 * p + e_id
                v_rows = slice(h * ML_V_DIM, (h + 1) * ML_V_DIM)
                head_rows = (row128 < ML_QK_DIM) if e_id == 0 else (row128 >= ML_QK_DIM)
                q_pad = jnp.where(head_rows, q_c, zero)
                b_r = b_rows[h:h + 1, :]
                ig_r = ig_rows[h:h + 1, :]
                m_prev = m_scr[h:h + 1, 0:1]
                dm = (b_r + key_cols[:, h:h + 1]) + causal_scr[...]
                inter = b_r + m_prev
                m_row = jnp.maximum(inter, jnp.max(dm, axis=0, keepdims=True))
                w_inter = jnp.exp2(inter - m_row)
                qk = lax.dot_general(k_pair, q_pad, (((0,), (0,)), ((), ())),
                                     preferred_element_type=F32)
                p_t = (qk * jnp.exp2(dm - m_row)).astype(BF16)
                v_ext = jnp.concatenate([vm_ref[v_rows, tok], ones_rows], axis=0)
                num = _dot(v_ext, p_t) + w_inter * _dot(state_bf, q_pad)
                den = num[ML_V_DIM:ML_V_DIM + 1, :]
                hh = num[:ML_V_DIM] * (1.0 / jnp.maximum(jnp.abs(den), jnp.exp2(-m_row)))
                ms = jnp.mean(hh * hh, axis=0, keepdims=True)
                gate = _sigmoid(om_ref[v_rows, tok].astype(F32))
                mix_scr[ATT_Q_W + h * ML_V_DIM:ATT_Q_W + (h + 1) * ML_V_DIM, tok] = (
                    hh * lax.rsqrt(ms + EPS) * mlnw_ref[v_rows, :] * gate).astype(BF16)
                b_last = b_r[:, C - 1:C]
                a_r = b_last - b_r + ig_r
                m_new = jnp.maximum(b_last + m_prev, jnp.max(a_r, axis=-1, keepdims=True))
                sc = jnp.exp2(b_last + m_prev - m_new)
                wsv = (v_ext.astype(F32) * jnp.exp2(a_r - m_new)).astype(BF16)
                new_state.append(sc * state + _dot_nt(wsv, k_pair))
                m_scr[h:h + 1, :] = jnp.broadcast_to(m_new, (1, LANES))
            first = lax.broadcasted_iota(jnp.int32, state.shape, 1) < ML_QK_DIM
            state_scr[p] = jnp.where(first, new_state[0], new_state[1])
            yield

    def out_proj(ci):
        tok = slice(ci * C, (ci + 1) * C)
        mix_t = mix_scr[:, tok].T
        for n in range(D_MODEL // OUT_COLS):
            nc = slice(n * OUT_COLS, (n + 1) * OUT_COLS)
            x1_ref[tok, nc] = x_ref[tok, nc] + _dot(mix_t, wout_ref[:, nc])
            yield

    stage_scores(0)
    n_sub = C // A
    pairs = (step for ci in range(L // C) for step in mlstm_chunk(ci))
    projs = iter(())
    for j in range(L // A):
        if j and j % n_sub == 0:
            projs = out_proj(j // n_sub - 1)
        attend(j)
        next(projs, None)
        next(pairs, None)
        next(projs, None)
    for _ in pairs:
        pass
    for _ in projs:
        pass
    for _ in out_proj(L // C - 1):
        pass

    @pl.when(i == pl.num_programs(1) - 1)
    def _():
        for p in range(n_pairs):
            c_t = state_scr[p, :ML_V_DIM, :].T
            for e_id in range(2):
                ct_ref[0, 2 * p + e_id] = c_t[e_id * ML_QK_DIM:(e_id + 1) * ML_QK_DIM, :]
            nrow_ref[0, p:p + 1, :] = state_scr[p, ML_V_DIM:ML_V_DIM + 1, :]
        for h in range(ML_HEADS):
            m_ref[0, :, h:h + 1] = m_scr[h:h + 1, 0:1] * (1.0 / LOG2_E)
        kt_ref[0] = kvc_ref[:ATT_KV_W, L - WINDOW:]
        vt_ref[0] = kvc_ref[ATT_KV_W:, L - WINDOW:]


def _prompt_mixer_t(batch, seq, sinks, qa, ks, kv, qm, km, vm, om, gt, x2d, wout, mlnw_col):
    tq = MIX_TILE
    nt = seq // tq
    sub = tq // WINDOW
    per_slab = INPROJ_TILE // tq
    win_per_slab = INPROJ_TILE // WINDOW
    col = lambda w: pl.BlockSpec(
        (None, w, tq), lambda b, i: ((b * nt + i) // per_slab, 0, (b * nt + i) % per_slab))
    full = lambda a: pl.BlockSpec(a.shape, lambda b, i: (0,) * a.ndim)
    once = lambda a: pl.BlockSpec(a.shape, lambda b, i: (0,) * a.ndim,
                                  pipeline_mode=pl.Buffered(1))
    prev_block = lambda b, i: jnp.maximum((b * nt + i) * sub - 1, 0)
    per_batch = lambda *dims: pl.BlockSpec((1,) + dims, lambda b, i: (b,) + (0,) * len(dims))
    return pl.pallas_call(
        _prompt_mixer_t_kernel,
        grid=(batch, nt),
        in_specs=[pl.BlockSpec(memory_space=pltpu.SMEM),
                  col(ATT_Q_W),
                  pl.BlockSpec((tq, ATT_KV_W), lambda b, i: (b * nt + i, 0)),
                  pl.BlockSpec((WINDOW, ATT_KV_W), lambda b, i: (prev_block(b, i), 0)),
                  col(2 * ATT_KV_W),
                  pl.BlockSpec((None, 2 * ATT_KV_W, WINDOW),
                               lambda b, i: (prev_block(b, i) // win_per_slab, 0,
                                             prev_block(b, i) % win_per_slab)),
                  col(ML_QK_W), col(ML_QK_W), col(ML_V_W), col(ML_V_W), col(N_GATES),
                  pl.BlockSpec((tq, D_MODEL), lambda b, i: (b * nt + i, 0)),
                  once(wout), full(mlnw_col)],
        out_specs=[pl.BlockSpec((tq, D_MODEL), lambda b, i: (b * nt + i, 0)),
                   per_batch(ML_HEADS, ML_QK_DIM, ML_V_DIM),
                   per_batch(ML_HEADS // 2, LANES),
                   per_batch(1, ML_HEADS),
                   per_batch(ATT_KV_W, WINDOW),
                   per_batch(ATT_KV_W, WINDOW)],
        out_shape=[jax.ShapeDtypeStruct((batch * seq, D_MODEL), F32),
                   jax.ShapeDtypeStruct((batch, ML_HEADS, ML_QK_DIM, ML_V_DIM), F32),
                   jax.ShapeDtypeStruct((batch, ML_HEADS // 2, LANES), F32),
                   jax.ShapeDtypeStruct((batch, 1, ML_HEADS), F32),
                   jax.ShapeDtypeStruct((batch, ATT_KV_W, WINDOW), F32),
                   jax.ShapeDtypeStruct((batch, ATT_KV_W, WINDOW), F32)],
        scratch_shapes=[pltpu.VMEM((D_MODEL, tq), BF16),
                        pltpu.VMEM((ML_HEADS // 2, 2 * LANES, LANES), F32),
                        pltpu.VMEM((SUBLANES, LANES), F32),
                        pltpu.VMEM((2, 2 * WINDOW, WINDOW), F32),
                        pltpu.VMEM((ML_CHUNK, ML_CHUNK), F32),
                        pltpu.VMEM((ML_CHUNK, ML_CHUNK), BF16),
                        pltpu.VMEM((1, 2 * WINDOW, ATT_HEADS * WINDOW), F32),
                        pltpu.VMEM((1, 2 * WINDOW, ATT_HEADS * WINDOW), F32),
                        pltpu.VMEM((2 * WINDOW, ATT_HEADS * WINDOW), BF16)],
        compiler_params=pltpu.CompilerParams(dimension_semantics=("arbitrary", "arbitrary"),
                                             vmem_limit_bytes=VMEM_LIMIT),
        name="prompt_mixer_t",
    )(sinks, qa, ks, ks, kv, kv, qm, km, vm, om, gt, x2d, wout, mlnw_col)


def _sample_mixer_kernel(t_len, sinks_ref, qa_ref, kv_ref, ck_ref, cv_ref, qm_ref, km_ref,
                         vm_ref, om_ref, gt_ref, c0_ref, n0_ref, m0_ref, x_ref, wout_ref,
                         mlnw_ref, x1_ref, nk_ref, nv_ref, c_ref, n_ref, m_ref,
                         mix_scr, wperm_scr):
    bt = SAMPLE_BT
    T = t_len
    L = bt * T

    @pl.when(pl.program_id(0) == 0)
    def _():
        _permute_head_rows(wperm_scr, wout_ref)
        wperm_scr[ATT_Q_W:, :] = wout_ref[ATT_Q_W:, :]

    lane3 = lax.broadcasted_iota(jnp.int32, (bt, T, LANES), 2)
    low3 = lane3 < ATT_HEAD_DIM
    lane = lax.broadcasted_iota(jnp.int32, (L, LANES), 1)
    low = lane < ATT_HEAD_DIM

    qa3 = qa_ref[...].astype(F32).reshape(bt, T, ATT_Q_W)
    pieces = []
    for col in range(ATT_GROUP):
        qc = qa3[:, :, col * LANES:(col + 1) * LANES]
        pieces += [jnp.where(low3, qc, 0.0), jnp.where(low3, 0.0, qc)]
    q3 = jnp.concatenate(pieces, axis=1).astype(BF16)
    R = bt * N_STACK * T
    q2 = q3.reshape(R, LANES)
    kv_new = kv_ref[...]
    k_new = kv_new[:, :ATT_KV_W]
    v_new = kv_new[:, ATT_KV_W:]
    def attention_stages():
        s_c = jnp.einsum('bqd,bdk->bqk', q3, ck_ref[...].astype(BF16),
                         preferred_element_type=F32).reshape(R, WINDOW)
        s_n = _dot_nt(q2, k_new.astype(BF16))
        row_c = lax.broadcasted_iota(jnp.int32, (R, WINDOW), 0)
        col_c = lax.broadcasted_iota(jnp.int32, (R, WINDOW), 1)
        s_c = jnp.where(col_c > row_c % T, s_c, -jnp.inf)
        row_n = lax.broadcasted_iota(jnp.int32, (R, L), 0)
        col_n = lax.broadcasted_iota(jnp.int32, (R, L), 1)
        valid_n = (row_n // (N_STACK * T) == col_n // T) & (col_n % T <= row_n % T)
        s_n = jnp.where(valid_n, s_n, -jnp.inf)
        yield
        stack_id = (lax.broadcasted_iota(jnp.int32, (R, 1), 0) // T) % N_STACK
        sink = jnp.zeros((R, 1), F32)
        for k_id in range(N_STACK):
            sink = jnp.where(stack_id == k_id, sinks_ref[ATT_HEAD_ORDER[k_id]], sink)
        m = jnp.maximum(jnp.maximum(jnp.max(s_c, axis=-1, keepdims=True),
                                    jnp.max(s_n, axis=-1, keepdims=True)), sink)
        yield
        e_c = jnp.exp(s_c - m)
        e_n = jnp.exp(s_n - m)
        denom = (jnp.sum(e_c, axis=-1, keepdims=True) + jnp.sum(e_n, axis=-1, keepdims=True)
                 + jnp.exp(sink - m))
        yield
        o = jnp.einsum('bqk,bdk->bqd', e_c.astype(BF16).reshape(bt, N_STACK * T, WINDOW),
                       cv_ref[...].astype(BF16), preferred_element_type=F32).reshape(R, LANES)
        o = (o + _dot(e_n.astype(BF16), v_new.astype(BF16))) / denom
        yield
        o3 = o.reshape(bt, N_STACK * T, LANES)
        for col in range(ATT_GROUP):
            lo_h = o3[:, (2 * col) * T:(2 * col + 1) * T, :]
            hi_h = o3[:, (2 * col + 1) * T:(2 * col + 2) * T, :]
            mix_scr[:, col * LANES:(col + 1) * LANES] = jnp.where(
                low3, lo_h, hi_h).reshape(L, LANES).astype(BF16)
        yield

    keep = lax.broadcasted_iota(jnp.int32, (ATT_KV_W, WINDOW), 1) < WINDOW - T
    k_new_t = k_new.T
    v_new_t = v_new.T
    def roll_caches():
        for q in range(bt):
            shift = (WINDOW - T - q * T) % WINDOW
            nk_ref[q] = jnp.where(keep, pltpu.roll(ck_ref[q], WINDOW - T, axis=1),
                                  pltpu.roll(k_new_t, shift, axis=1))
            nv_ref[q] = jnp.where(keep, pltpu.roll(cv_ref[q], WINDOW - T, axis=1),
                                  pltpu.roll(v_new_t, shift, axis=1))
            if (q + 1) % (bt // ML_HEADS) == 0:
                yield

    rolls = roll_caches()

    r = lax.broadcasted_iota(jnp.int32, (L, L), 0)
    c = lax.broadcasted_iota(jnp.int32, (L, L), 1)
    seg = (r // T == c // T) & (r <= c)
    seg_bias = jnp.where(seg, 0.0, -jnp.inf)
    seg_bf = seg.astype(F32).astype(BF16)
    gates = gt_ref[...] * LOG2_E
    cum_row = jnp.zeros(gates.shape, F32)
    for part in _split3(gates):
        cum_row = cum_row + _dot(part, seg_bf)
    ig_rows = gates[:ML_HEADS]
    b_rows = cum_row[ML_HEADS:]
    gate_cols = jnp.concatenate([ig_rows, b_rows, jnp.zeros((LANES - N_GATES, L), F32)],
                                axis=0).T

    def col_to_row(x_col):
        return jnp.broadcast_to(x_col, (L, LANES)).T[0:1, :]

    ones_rows = (r[:LANES] == 0).astype(F32).astype(BF16)
    qm = qm_ref[...]
    km = km_ref[...]
    qm_f = qm.astype(F32)
    km_f = km.astype(F32)
    n_rep = bt * ML_QK_DIM // LANES
    bd_row = lax.broadcasted_iota(jnp.int32, (L, bt * ML_QK_DIM), 0) // T
    bd_lane = lax.broadcasted_iota(jnp.int32, (L, bt * ML_QK_DIM), 1) // ML_QK_DIM
    block_diag = bd_row == bd_lane

    def spread(x_pair, e):
        other = pltpu.roll(x_pair, ML_QK_DIM, axis=1)
        twice = jnp.where(low, x_pair, other) if e == 0 else jnp.where(low, other, x_pair)
        return jnp.where(block_diag, jnp.concatenate([twice] * n_rep, axis=1), 0.0).astype(BF16)

    def head_stages(h):
        p, e = divmod(h, 2)
        qc = qm[:, p * LANES:(p + 1) * LANES]
        k_pair = km[:, p * LANES:(p + 1) * LANES]
        zero = jnp.zeros_like(qc)
        q_pad = jnp.where(low, qc, zero) if e == 0 else jnp.where(low, zero, qc)
        v_h = vm_ref[:, h * ML_V_DIM:(h + 1) * ML_V_DIM]
        v_ext_t = jnp.concatenate([v_h.astype(F32).T.astype(BF16), ones_rows], axis=0)
        ig_c = gate_cols[:, h:h + 1]
        b_c = gate_cols[:, ML_HEADS + h:ML_HEADS + h + 1]
        b_r = b_rows[h:h + 1, :]
        yield
        m0 = m0_ref[:, :, h:h + 1] * LOG2_E
        inter = b_r + col_to_row(jnp.broadcast_to(m0, (bt, T, 1)).reshape(L, 1))
        dm = (b_r + (ig_c - b_c)) + seg_bias
        m_row = jnp.maximum(inter, jnp.max(dm, axis=0, keepdims=True))
        w_inter = jnp.exp2(inter - m_row)
        yield
        p_t = (_dot_nt(k_pair, q_pad) * jnp.exp2(dm - m_row)).astype(BF16)
        num_t = _dot(v_ext_t, p_t)
        yield
        q_h3 = qm_f[:, h * ML_QK_DIM:(h + 1) * ML_QK_DIM].reshape(bt, T, ML_QK_DIM)
        k_h3 = km_f[:, h * ML_QK_DIM:(h + 1) * ML_QK_DIM].reshape(bt, T, ML_QK_DIM)
        c0 = c0_ref[:, h]
        n0 = n0_ref[:, h:h + 1, :]
        q_c_t = _dot(spread(qm_f[:, p * LANES:(p + 1) * LANES], e),
                     c0.astype(BF16).reshape(bt * ML_QK_DIM, ML_V_DIM)).T
        q_n_r = col_to_row(jnp.sum(q_h3 * n0, axis=-1, keepdims=True).reshape(L, 1))
        yield
        num =num_t[:ML_V_DIM] + w_inter * q_c_t
        den = num_t[ML_V_DIM:ML_V_DIM + 1] + w_inter * q_n_r
        hh = num * (1.0 / jnp.maximum(jnp.abs(den), jnp.exp2(-m_row)))
        ms = jnp.mean(hh * hh, axis=0, keepdims=True)
        yield
        mix_scr[:, ATT_Q_W + h * ML_V_DIM:ATT_Q_W + (h + 1) * ML_V_DIM] = (
            (hh * lax.rsqrt(ms + EPS)).T * mlnw_ref[:, h * ML_V_DIM:(h + 1) * ML_V_DIM]
            * _sigmoid(om_ref[:, h * ML_V_DIM:(h + 1) * ML_V_DIM].astype(F32))).astype(BF16)
        yield
        b3 = b_c.reshape(bt, T, 1)
        b_last = b3[:, T - 1:T, :]
        a3 = b_last - b3 + ig_c.reshape(bt, T, 1)
        m_new = jnp.maximum(b_last + m0, jnp.max(a3, axis=1, keepdims=True))
        sc = jnp.exp2(b_last + m0 - m_new)
        ws = jnp.exp2(a3 - m_new)
        yield
        kw = spread(km_f[:, p * LANES:(p + 1) * LANES] * ws.reshape(L, 1), e)
        d_c = lax.dot_general(kw, v_h, (((0,), (0,)), ((), ())), preferred_element_type=F32)
        c_ref[:, h] = sc * c0 + d_c.reshape(bt, ML_QK_DIM, ML_V_DIM)
        n_ref[:, h:h + 1, :] = sc * n0 + jnp.sum(ws * k_h3, axis=1, keepdims=True)
        m_ref[:, :, h:h + 1] = m_new * (1.0 / LOG2_E)
        yield

    att = attention_stages()
    for _ in zip(*[head_stages(h) for h in range(ML_HEADS)]):
        next(att, None)
        next(rolls, None)
    for _ in att:
        pass
    for _ in rolls:
        pass

    x1_ref[...] = x_ref[...] + _dot(mix_scr[...], wperm_scr[...])


def _sample_mixer(nb, t_len, sinks, qa, kv, ck, cv, qm, km, vm, om, gt, c0, n0, m0, x2d, wout, mlnw):
    bt = SAMPLE_BT
    tl = bt * t_len
    row = lambda w: pl.BlockSpec((tl, w), lambda i: (i, 0))
    full = lambda a: pl.BlockSpec(a.shape, lambda i: (0,) * a.ndim)
    once = lambda a: pl.BlockSpec(a.shape, lambda i: (0,) * a.ndim, pipeline_mode=pl.Buffered(1))
    cache = pl.BlockSpec((bt, ATT_KV_W, WINDOW), lambda i: (i, 0, 0))
    c_spec = pl.BlockSpec((bt, ML_HEADS, ML_QK_DIM, ML_V_DIM), lambda i: (i, 0, 0, 0))
    n_spec = pl.BlockSpec((bt, ML_HEADS, ML_QK_DIM), lambda i: (i, 0, 0))
    m_spec = pl.BlockSpec((bt, 1, ML_HEADS), lambda i: (i, 0, 0))
    return pl.pallas_call(
        functools.partial(_sample_mixer_kernel, t_len),
        grid=(nb // bt,),
        in_specs=[pl.BlockSpec(memory_space=pltpu.SMEM),
                  row(ATT_Q_W), row(2 * ATT_KV_W), cache, cache, row(ML_QK_W), row(ML_QK_W),
                  row(ML_V_W), row(ML_V_W), pl.BlockSpec((N_GATES, tl), lambda i: (0, i)),
                  c_spec, n_spec, m_spec, row(D_MODEL), once(wout), full(mlnw)],
        out_specs=[row(D_MODEL), cache, cache, c_spec, n_spec, m_spec],
        out_shape=[jax.ShapeDtypeStruct((nb * t_len, D_MODEL), F32),
                   jax.ShapeDtypeStruct((nb, ATT_KV_W, WINDOW), F32),
                   jax.ShapeDtypeStruct((nb, ATT_KV_W, WINDOW), F32),
                   jax.ShapeDtypeStruct((nb, ML_HEADS, ML_QK_DIM, ML_V_DIM), F32),
                   jax.ShapeDtypeStruct((nb, ML_HEADS, ML_QK_DIM), F32),
                   jax.ShapeDtypeStruct((nb, 1, ML_HEADS), F32)],
        scratch_shapes=[pltpu.VMEM((tl, D_MODEL), BF16),
                        pltpu.VMEM((D_MODEL, D_MODEL), BF16)],
        compiler_params=pltpu.CompilerParams(dimension_semantics=("arbitrary",),
                                             vmem_limit_bytes=VMEM_LIMIT),
        name="sample_mixer",
    )(sinks, qa, kv, ck, cv, qm, km, vm, om, gt, c0, n0, m0, x2d, wout, mlnw)


def _ffn_kernel(seq_rows, *refs):
    if seq_rows is None:
        (x_ref, nw_ref, w_ref, cw_ref, cb_ref, wd_ref, y_ref, conv_ref,
         gbuf, act_scr, carry) = refs
        hist_ref = None
    else:
        (x_ref, hist_ref, nw_ref, w_ref, cw_ref, cb_ref, wd_ref, y_ref, conv_ref,
         gbuf, act_scr) = refs
        carry = None
    tm = x_ref.shape[0]
    tf = FF_CHUNK
    n_hist = CONV_W - 1
    rows = tm if seq_rows is None else seq_rows
    nseq = tm // rows
    base = SUBLANES
    n_chunks = D_FF // tf

    if carry is not None:
        @pl.when(pl.program_id(1) == 0)
        def _():
            carry[...] = jnp.zeros(carry.shape, F32)

    x = x_ref[...]
    tb = tm // FFN_NORM_BLOCKS
    h_blocks = [_rms(x_ref[r * tb:(r + 1) * tb, :], nw_ref[...]).astype(BF16)
                for r in range(FFN_NORM_BLOCKS)]
    h2 = jnp.concatenate(h_blocks, axis=0)

    def proj(f, lhs=None):
        lhs = h2 if lhs is None else lhs
        return (_dot(lhs, w_ref[:, f * tf:(f + 1) * tf]),
                _dot(lhs, w_ref[:, D_FF + f * tf:D_FF + (f + 1) * tf]))

    first = [proj(0, hb) for hb in h_blocks]
    nxt = (jnp.concatenate([gu[0] for gu in first], axis=0),
           jnp.concatenate([gu[1] for gu in first], axis=0))
    for f in range(n_chunks):
        g, u = nxt
        if f + 1 < n_chunks:
            nxt = proj(f + 1)
        cols = slice(f * tf, (f + 1) * tf)
        s = f % 2
        g3 = g.reshape(nseq, rows, tf)
        if seq_rows is None:
            gbuf[s, :, base - n_hist:base, :] = carry[:, SUBLANES - n_hist:, cols]
            carry[:, SUBLANES - n_hist:, cols] = g3[:, rows - n_hist:, :]
        else:
            gbuf[s, :, base - n_hist:base, :] = hist_ref[:, :, cols]
            conv_ref[:, :, cols] = g3[:, rows - n_hist:, :]
        gbuf[s, :, base:base + rows, :] = g3
        gc = cb_ref[:, cols] + g * cw_ref[CONV_W - 1:CONV_W, cols]
        for d in range(1, CONV_W):
            gm = gbuf[s, :, base - d:base - d + rows, :].reshape(tm, tf)
            gc = gc + gm * cw_ref[CONV_W - 1 - d:CONV_W - d, cols]
        act_scr[:, cols] = (gc * _sigmoid(gc) * u).astype(BF16)
    y_ref[...] = x + _dot(act_scr[...], wd_ref[...])

    if carry is not None:
        @pl.when(pl.program_id(1) == pl.num_programs(1) - 1)
        def _():
            conv_ref[...] = carry[:, SUBLANES - n_hist:, :]


def _ffn_scratch(tm, rows):
    return [pltpu.VMEM((2, tm // rows, SUBLANES + rows, FF_CHUNK), F32),
            pltpu.VMEM((tm, D_FF), BF16)]


def _ffn_prompt(batch, seq, x2d, nw, w, cw, cb, wd):
    tm = FFN_TILE
    nt = seq // tm
    full = lambda a: pl.BlockSpec(a.shape, lambda b, i: (0,) * a.ndim)
    once = lambda a: pl.BlockSpec(a.shape, lambda b, i: (0,) * a.ndim,
                                  pipeline_mode=pl.Buffered(1))
    row = pl.BlockSpec((tm, D_MODEL), lambda b, i: (b * nt + i, 0))
    return pl.pallas_call(
        functools.partial(_ffn_kernel, None),
        grid=(batch, nt),
        in_specs=[row, full(nw), once(w), full(cw), full(cb), once(wd)],
        out_specs=[row, pl.BlockSpec((1, CONV_W - 1, D_FF), lambda b, i: (b, 0, 0))],
        out_shape=[jax.ShapeDtypeStruct((batch * seq, D_MODEL), F32),
                   jax.ShapeDtypeStruct((batch, CONV_W - 1, D_FF), F32)],
        scratch_shapes=_ffn_scratch(tm, tm) + [pltpu.VMEM((1, SUBLANES, D_FF), F32)],
        compiler_params=pltpu.CompilerParams(dimension_semantics=("arbitrary", "arbitrary"),
                                             vmem_limit_bytes=VMEM_LIMIT),
        name="ffn_prompt",
    )(x2d, nw, w, cw, cb, wd)


def _ffn_sample(nb, t_len, x2d, hist, nw, w, cw, cb, wd):
    tm = ROW_TILE
    bt = tm // t_len
    full = lambda a: pl.BlockSpec(a.shape, lambda i: (0,) * a.ndim)
    once = lambda a: pl.BlockSpec(a.shape, lambda i: (0,) * a.ndim, pipeline_mode=pl.Buffered(1))
    row = pl.BlockSpec((tm, D_MODEL), lambda i: (i, 0))
    hist_spec = pl.BlockSpec((bt, CONV_W - 1, D_FF), lambda i: (i, 0, 0))
    return pl.pallas_call(
        functools.partial(_ffn_kernel, t_len),
        grid=(nb // bt,),
        in_specs=[row, hist_spec, full(nw), once(w), full(cw), full(cb), once(wd)],
        out_specs=[row, hist_spec],
        out_shape=[jax.ShapeDtypeStruct((nb * t_len, D_MODEL), F32),
                   jax.ShapeDtypeStruct((nb, CONV_W - 1, D_FF), F32)],
        scratch_shapes=_ffn_scratch(tm, t_len),
        compiler_params=pltpu.CompilerParams(dimension_semantics=("arbitrary",),
                                             vmem_limit_bytes=VMEM_LIMIT),
        name="ffn_sample",
    )(x2d, hist, nw, w, cw, cb, wd)


def _head_mean_matrix(width, head_dim):
    idx = np.arange(width) // head_dim
    return jnp.asarray((idx[:, None] == idx[None, :]).astype(np.float32) / head_dim, dtype=BF16)


def _layer_weights(norm_mix_w, w_in, b_gates, q_norm_w, k_norm_w, sinks, ml_norm_w, w_out,
                   norm_ffn_w, w_ffn_in, conv_w, conv_b, w_down):
    w_in_t = jnp.pad(w_in.T.astype(BF16), ((0, IN_WIDTH_PAD - w_in.shape[1]), (0, 0)))
    return dict(
        nw=norm_mix_w.reshape(1, D_MODEL),
        w_in_t=w_in_t,
        bg=jnp.pad(b_gates, (0, LANES - N_GATES)).reshape(1, LANES),
        qnw=(jnp.tile(q_norm_w, ATT_HEADS) * ATT_SCALE).reshape(1, ATT_Q_W),
        knw=jnp.tile(k_norm_w, ATT_KV_HEADS).reshape(1, ATT_KV_W),
        gq=_head_mean_matrix(ATT_Q_W, ATT_HEAD_DIM),
        gk=_head_mean_matrix(ATT_KV_W, ATT_HEAD_DIM),
        bg_col=b_gates.reshape(N_GATES, 1),
        qnw_col=(jnp.tile(q_norm_w, ATT_HEADS) * (ATT_SCALE * LOG2_E)).reshape(ATT_Q_W, 1),
        knw_col=jnp.tile(k_norm_w, ATT_KV_HEADS).reshape(ATT_KV_W, 1),
        mlnw_col=ml_norm_w.reshape(ML_V_W, 1),
        sinks=sinks,
        mlnw=ml_norm_w.reshape(1, ML_V_W),
        wout=w_out.astype(BF16),
        nfw=norm_ffn_w.reshape(1, D_MODEL),
        wff=w_ffn_in.astype(BF16),
        cw=conv_w,
        cb=conv_b.reshape(1, D_FF),
        wd=w_down.astype(BF16),
    )


def _cache_from_t(a_t):
    n = a_t.shape[0]
    return jnp.transpose(a_t.reshape(n, ATT_KV_HEADS, ATT_HEAD_DIM, WINDOW), (0, 3, 1, 2))


def _cache_to_t(a):
    n = a.shape[0]
    return jnp.transpose(a, (0, 2, 3, 1)).reshape(n, ATT_KV_W, WINDOW)


def _prompt_layer(x, w):
    batch, seq, _ = x.shape
    assert seq % INPROJ_TILE == 0 and INPROJ_TILE % MIX_TILE == 0
    assert MIX_TILE % ML_CHUNK == 0 and ML_CHUNK % WINDOW == 0
    assert seq % FFN_TILE == 0
    x2d = x.reshape(batch * seq, D_MODEL)
    qa, ks, kv, qm, km, vm, om, gt = _inproj_t(x2d, w["nw"], w["w_in_t"], w["bg_col"],
                                               w["qnw_col"], w["knw_col"])
    x1, c_t, n_row, m, k_t, v_t = _prompt_mixer_t(batch, seq, w["sinks"], qa, ks, kv, qm, km, vm,
                                                  om, gt, x2d, w["wout"], w["mlnw_col"])
    y, conv = _ffn_prompt(batch, seq, x1, w["nfw"], w["wff"], w["cw"], w["cb"], w["wd"])
    return (y.reshape(batch, seq, D_MODEL), _cache_from_t(k_t), _cache_from_t(v_t),
            jnp.swapaxes(c_t, -1, -2), n_row.reshape(batch, ML_HEADS, ML_QK_DIM),
            m.reshape(batch, ML_HEADS), conv)


def _sample_layer(x, ck, cv, c0, n0, m0, conv_buf, w):
    nb, t_len, _ = x.shape
    assert t_len == SUBLANES and SAMPLE_BT * t_len == LANES and nb % SAMPLE_BT == 0
    assert (nb * t_len) % ROW_TILE == 0
    x2d = x.reshape(nb * t_len, D_MODEL)
    qa, kv, qm, km, vm, om, gt = _inproj(x2d, w["nw"], w["w_in_t"], w["bg"], w["qnw"], w["knw"],
                                         w["gq"], w["gk"])
    x1, nk_t, nv_t, c_t, n, m = _sample_mixer(
        nb, t_len, w["sinks"], qa, kv, _cache_to_t(ck), _cache_to_t(cv), qm, km, vm, om, gt,
        jnp.swapaxes(c0, -1, -2), n0, m0.reshape(nb, 1, ML_HEADS), x2d, w["wout"], w["mlnw"])
    y, conv = _ffn_sample(nb, t_len, x1, conv_buf, w["nfw"], w["wff"], w["cw"], w["cb"],
                          w["wd"])
    return (y.reshape(nb, t_len, D_MODEL), _cache_from_t(nk_t), _cache_from_t(nv_t),
            jnp.swapaxes(c_t, -1, -2), n, m.reshape(nb, ML_HEADS), conv)


def kernel(x_prompt, x_sample, cache_attn_k, cache_attn_v, state_mlstm_C, state_mlstm_n,
           state_mlstm_m, cache_ffn_conv, norm_mix_w, w_in, b_gates, q_norm_w, k_norm_w,
           sinks, ml_norm_w, w_out, norm_ffn_w, w_ffn_in, conv_w, conv_b, w_down):
    depth = w_in.shape[0]
    yp, ys = x_prompt, x_sample
    sp = [[] for _ in range(6)]
    ss = [[] for _ in range(6)]
    for l in range(depth):
        w = _layer_weights(norm_mix_w[l], w_in[l], b_gates[l], q_norm_w[l], k_norm_w[l], sinks[l],
                           ml_norm_w[l], w_out[l], norm_ffn_w[l], w_ffn_in[l], conv_w[l],
                           conv_b[l], w_down[l])
        yp, *st_p = _prompt_layer(yp, w)
        ys, *st_s = _sample_layer(ys, cache_attn_k[l], cache_attn_v[l], state_mlstm_C[l],
                                  state_mlstm_n[l], state_mlstm_m[l], cache_ffn_conv[l], w)
        for i in range(6):
            sp[i].append(st_p[i])
            ss[i].append(st_s[i])
    k_p, v_p, c_p, n_p, m_p, conv_p = [jnp.stack(a) for a in sp]
    k_s, v_s, c_s, n_s, m_s, conv_s = [jnp.stack(a) for a in ss]
    return (yp, ys, k_p, v_p, c_p, n_p, m_p, conv_p, k_s, v_s, c_s, n_s, m_s, conv_s)
```

```python
import functools

import numpy as np
import jax
import jax.numpy as jnp
from jax import lax
from jax.experimental import pallas as pl
from jax.experimental.pallas import tpu as pltpu

F32 = jnp.float32
BF16 = jnp.bfloat16

D_MODEL = 1024
ATT_HEADS = 8
ATT_KV_HEADS = 2
ATT_HEAD_DIM = 64
ATT_GROUP = ATT_HEADS // ATT_KV_HEADS
WINDOW = 128
ML_HEADS = 4
ML_V_DIM = 128
ML_QK_DIM = 64
D_FF = 2816
CONV_W = 3
EPS = 1e-6
ATT_SCALE = ATT_HEAD_DIM ** -0.5
ML_SCALE = ML_QK_DIM ** -0.5
LOG2_E = 1.4426950408889634

ATT_Q_W = ATT_HEADS * ATT_HEAD_DIM
ATT_KV_W = ATT_KV_HEADS * ATT_HEAD_DIM
ML_QK_W = ML_HEADS * ML_QK_DIM
ML_V_W = ML_HEADS * ML_V_DIM
N_GATES = 2 * ML_HEADS
N_STACK = 2 * ATT_GROUP

LANES = 128
SUBLANES = 8

OFF_QA = 0
OFF_KV = OFF_QA + ATT_Q_W
OFF_QM = OFF_KV + 2 * ATT_KV_W
OFF_KM = OFF_QM + ML_QK_W
OFF_VM = OFF_KM + ML_QK_W
OFF_OM = OFF_VM + ML_V_W
OFF_GL = OFF_OM + ML_V_W
IN_WIDTH_PAD = OFF_GL + LANES

ATT_HEAD_ORDER = tuple(h for c in range(ATT_GROUP) for h in (c, c + ATT_GROUP))

ROW_TILE = 512
FFN_TILE = 1024
INPROJ_SUB = 256
NORM_ROWS = 128
INPROJ_TILE = 1024
MIX_TILE = 512
ML_CHUNK = 256
OUT_COLS = 256
FF_CHUNK = 256
SAMPLE_BT = 16
VMEM_LIMIT = 56 * 1024 * 1024


def _dot(a, b):
    return jnp.dot(a, b, preferred_element_type=F32)


def _dot_nt(a, b):
    return lax.dot_general(a, b, (((1,), (1,)), ((), ())), preferred_element_type=F32)


def _split3(x):
    hi = x.astype(BF16)
    r1 = x - hi.astype(F32)
    mid = r1.astype(BF16)
    lo = (r1 - mid.astype(F32)).astype(BF16)
    return hi, mid, lo


def _rms(x, w):
    ms = jnp.mean(x * x, axis=-1, keepdims=True)
    return x * lax.rsqrt(ms + EPS) * w


def _log_sigmoid(x):
    return jnp.minimum(x, 0.0) - jnp.log1p(jnp.exp(-jnp.abs(x)))


def _sigmoid(x):
    return 1.0 / (1.0 + jnp.exp(-x))


def _permute_head_rows(dst_ref, src_ref):
    for k, h in enumerate(ATT_HEAD_ORDER):
        dst_ref[k * ATT_HEAD_DIM:(k + 1) * ATT_HEAD_DIM, :] = (
            src_ref[h * ATT_HEAD_DIM:(h + 1) * ATT_HEAD_DIM, :])


def _inproj_kernel(x_ref, nw_ref, w_ref, bg_ref, qnw_ref, knw_ref, gq_ref, gk_ref,
                   qa_ref, kv_ref, qm_ref, km_ref, vm_ref, om_ref, gt_ref, wq_scr):
    @pl.when(pl.program_id(0) == 0)
    def _():
        _permute_head_rows(wq_scr, w_ref)

    h = _rms(x_ref[...], nw_ref[...]).astype(BF16)

    def proj(lo, width):
        return _dot_nt(h, w_ref[lo:lo + width, :])

    q = _dot_nt(h, wq_scr[...])
    q_ms = _dot((q * q).astype(BF16), gq_ref[...])
    qa_ref[...] = (q * lax.rsqrt(q_ms + EPS) * qnw_ref[...]).astype(BF16)

    kv = proj(OFF_KV, 2 * ATT_KV_W)
    k = kv[:, :ATT_KV_W]
    k_ms = _dot((k * k).astype(BF16), gk_ref[...])
    kv_ref[:, :ATT_KV_W] = k * lax.rsqrt(k_ms + EPS) * knw_ref[...]
    kv_ref[:, ATT_KV_W:] = kv[:, ATT_KV_W:]

    qm_ref[...] = (proj(OFF_QM, ML_QK_W) * ML_SCALE).astype(BF16)
    km_ref[...] = proj(OFF_KM, ML_QK_W).astype(BF16)
    vm_ref[...] = proj(OFF_VM, ML_V_W).astype(BF16)
    om_ref[...] = proj(OFF_OM, ML_V_W).astype(BF16)

    gl = proj(OFF_GL, LANES) + bg_ref[...]
    lane = lax.broadcasted_iota(jnp.int32, gl.shape, 1)
    g = jnp.where(lane < ML_HEADS, gl, _log_sigmoid(gl))
    gt_ref[...] = g.T[:N_GATES, :]


def _inproj(x2d, nw, w_in_t, bg, qnw, knw, gq, gk):
    n = x2d.shape[0]
    tm = ROW_TILE
    row = lambda w: pl.BlockSpec((tm, w), lambda i: (i, 0))
    full = lambda a: pl.BlockSpec(a.shape, lambda i: (0,) * a.ndim)
    once = lambda a: pl.BlockSpec(a.shape, lambda i: (0,) * a.ndim, pipeline_mode=pl.Buffered(1))
    return pl.pallas_call(
        _inproj_kernel,
        grid=(n // tm,),
        in_specs=[row(D_MODEL), full(nw), once(w_in_t), full(bg), full(qnw), full(knw),
                  full(gq), full(gk)],
        out_specs=[row(ATT_Q_W), row(2 * ATT_KV_W), row(ML_QK_W), row(ML_QK_W),
                   row(ML_V_W), row(ML_V_W), pl.BlockSpec((N_GATES, tm), lambda i: (0, i))],
        out_shape=[jax.ShapeDtypeStruct((n, ATT_Q_W), BF16),
                   jax.ShapeDtypeStruct((n, 2 * ATT_KV_W), F32),
                   jax.ShapeDtypeStruct((n, ML_QK_W), BF16),
                   jax.ShapeDtypeStruct((n, ML_QK_W), BF16),
                   jax.ShapeDtypeStruct((n, ML_V_W), BF16),
                   jax.ShapeDtypeStruct((n, ML_V_W), BF16),
                   jax.ShapeDtypeStruct((N_GATES, n), F32)],
        scratch_shapes=[pltpu.VMEM((ATT_Q_W, D_MODEL), BF16)],
        compiler_params=pltpu.CompilerParams(dimension_semantics=("arbitrary",),
                                             vmem_limit_bytes=VMEM_LIMIT),
        name="inproj",
    )(x2d, nw, w_in_t, bg, qnw, knw, gq, gk)


def _head_norm_t(z, head_dim, w_col):
    rows, tokens = z.shape
    z3 = z.reshape(rows // head_dim, head_dim, tokens)
    ms = jnp.mean(z3 * z3, axis=1, keepdims=True)
    return (z3 * lax.rsqrt(ms + EPS)).reshape(rows, tokens) * w_col


def _inproj_t_kernel(x_ref, nw_ref, w_ref, bg_ref, qnw_ref, knw_ref,
                     qa_ref, ks_ref, kv_ref, qm_ref, km_ref, vm_ref, om_ref, gt_ref, h_scr):
    tm = x_ref.shape[0]
    sub = INPROJ_SUB

    def norm_rows(c):
        for r0 in range(c * sub, (c + 1) * sub, NORM_ROWS):
            rows = slice(r0, r0 + NORM_ROWS)
            h_scr[:, rows] = _rms(x_ref[rows, :], nw_ref[...]).T.astype(BF16)
            yield

    def project(c):
        tok = slice(c * sub, (c + 1) * sub)
        h_t = h_scr[:, tok]

        def proj(lo, width):
            return _dot(w_ref[lo:lo + width, :], h_t)

        qa_ref[:, tok] = _head_norm_t(proj(OFF_QA, ATT_Q_W), ATT_HEAD_DIM,
                                      qnw_ref[...]).astype(BF16)
        yield
        kv = proj(OFF_KV, 2 * ATT_KV_W)
        k = _head_norm_t(kv[:ATT_KV_W], ATT_HEAD_DIM, knw_ref[...])
        kv_ref[:ATT_KV_W, tok] = k
        kv_ref[ATT_KV_W:, tok] = kv[ATT_KV_W:]
        ks_ref[tok, :] = k.T.astype(BF16)
        qm_ref[:, tok] = (proj(OFF_QM, ML_QK_W) * ML_SCALE).astype(BF16)
        yield
        km_ref[:, tok] = proj(OFF_KM, ML_QK_W).astype(BF16)
        vm_ref[:, tok] = proj(OFF_VM, ML_V_W).astype(BF16)
        yield
        om_ref[:, tok] = proj(OFF_OM, ML_V_W).astype(BF16)
        gl = proj(OFF_GL, 2 * SUBLANES)[:N_GATES] + bg_ref[...]
        row = lax.broadcasted_iota(jnp.int32, gl.shape, 0)
        gt_ref[:, tok] = jnp.where(row < ML_HEADS, gl, _log_sigmoid(gl))
        yield

    for _ in norm_rows(0):
        pass
    for c in range(tm // sub):
        norms = norm_rows(c + 1) if c + 1 < tm // sub else iter(())
        for _ in project(c):
            next(norms, None)
        for _ in norms:
            pass


def _inproj_t(x2d, nw, w_in_t, bg_col, qnw_col, knw_col):
    n = x2d.shape[0]
    tm = INPROJ_TILE
    full = lambda a: pl.BlockSpec(a.shape, lambda i: (0,) * a.ndim)
    once = lambda a: pl.BlockSpec(a.shape, lambda i: (0,) * a.ndim, pipeline_mode=pl.Buffered(1))
    col = lambda w: pl.BlockSpec((None, w, tm), lambda i: (i, 0, 0))
    slab = lambda w, dt: jax.ShapeDtypeStruct((n // tm, w, tm), dt)
    return pl.pallas_call(
        _inproj_t_kernel,
        grid=(n // tm,),
        in_specs=[pl.BlockSpec((tm, D_MODEL), lambda i: (i, 0)), full(nw), once(w_in_t),
                  full(bg_col), full(qnw_col), full(knw_col)],
        out_specs=[col(ATT_Q_W), pl.BlockSpec((tm, ATT_KV_W), lambda i: (i, 0)),
                   col(2 * ATT_KV_W), col(ML_QK_W), col(ML_QK_W), col(ML_V_W), col(ML_V_W),
                   col(N_GATES)],
        out_shape=[slab(ATT_Q_W, BF16),
                   jax.ShapeDtypeStruct((n, ATT_KV_W), BF16),
                   slab(2 * ATT_KV_W, F32), slab(ML_QK_W, BF16), slab(ML_QK_W, BF16),
                   slab(ML_V_W, BF16), slab(ML_V_W, BF16), slab(N_GATES, F32)],
        scratch_shapes=[pltpu.VMEM((D_MODEL, tm), BF16)],
        compiler_params=pltpu.CompilerParams(dimension_semantics=("arbitrary",),
                                             vmem_limit_bytes=VMEM_LIMIT),
        name="inproj_t",
    )(x2d, nw, w_in_t, bg_col, qnw_col, knw_col)


def _prompt_mixer_t_kernel(sinks_ref, qa_ref, ksc_ref, ksp_ref, kvc_ref, kvp_ref, qm_ref, km_ref,
                           vm_ref, om_ref, gt_ref, x_ref, wout_ref, mlnw_ref,
                           x1_ref, ct_ref, nrow_ref, m_ref, kt_ref, vt_ref,
                           mix_scr, state_scr, m_scr, band_scr, causal_scr, tri_scr,
                           s_scr_a, s_scr_b, e_scr):
    i = pl.program_id(1)
    A = WINDOW
    L = MIX_TILE
    C = ML_CHUNK
    n_pairs = ML_HEADS // 2

    @pl.when(i == 0)
    def _():
        state_scr[...] = jnp.zeros(state_scr.shape, F32)
        m_scr[...] = jnp.zeros(m_scr.shape, F32)
        kj = lax.broadcasted_iota(jnp.int32, (2 * A, A), 0)
        qi = lax.broadcasted_iota(jnp.int32, (2 * A, A), 1)
        band = (kj > qi) & (kj <= qi + WINDOW)
        band_scr[0] = jnp.where(band, 0.0, -jnp.inf)
        band_scr[1] = jnp.where(band & (kj >= A), 0.0, -jnp.inf)
        r = lax.broadcasted_iota(jnp.int32, (C, C), 0)
        c = lax.broadcasted_iota(jnp.int32, (C, C), 1)
        causal_scr[...] = jnp.where(r <= c, 0.0, -jnp.inf)
        tri_scr[...] = (r <= c).astype(F32).astype(BF16)

    k_all = jnp.concatenate([ksp_ref[...], ksc_ref[...]], axis=0)
    v_all = jnp.concatenate([kvp_ref[ATT_KV_W:, :], kvc_ref[ATT_KV_W:, :]], axis=1).astype(BF16)
    zero_q = jnp.zeros((ATT_HEAD_DIM, A), BF16)
    slot = 0
    s_bufs = (s_scr_a, s_scr_b)

    def stage_scores(j):
        pieces = []
        for h in range(ATT_HEADS):
            q_h = qa_ref[h * ATT_HEAD_DIM:(h + 1) * ATT_HEAD_DIM, j * A:(j + 1) * A]
            pieces.append(jnp.concatenate([q_h, zero_q] if h < ATT_GROUP else [zero_q, q_h],
                                          axis=0))
        s_bufs[j % 2][slot] = _dot(k_all[j * A:(j + 2) * A, :], jnp.concatenate(pieces, axis=1))

    def attend(j):
        cols = slice(j * A, (j + 1) * A)
        vt = v_all[:, j * A:(j + 2) * A]
        if j + 1 < L // A:
            stage_scores(j + 1)
        s_buf = s_bufs[j % 2]
        bias = jnp.where(i > 0, band_scr[0], band_scr[1]) if j == 0 else band_scr[0]
        m_rows = []
        for h in range(ATT_HEADS):
            sb = s_buf[slot, :, h * A:(h + 1) * A] + bias
            m_rows.append(jnp.maximum(jnp.max(sb, axis=0, keepdims=True),
                                      sinks_ref[h] * LOG2_E))
        inv_rows = []
        for h in range(ATT_HEADS):
            e = jnp.exp2(s_buf[slot, :, h * A:(h + 1) * A] + (bias - m_rows[h]))
            e_scr[:, h * A:(h + 1) * A] = e.astype(BF16)
            inv_rows.append(1.0 / (jnp.sum(e, axis=0, keepdims=True)
                                   + jnp.exp2(sinks_ref[h] * LOG2_E - m_rows[h])))
        o = _dot(vt, e_scr[...])
        for h in range(ATT_HEADS):
            g = h // ATT_GROUP
            mix_scr[h * ATT_HEAD_DIM:(h + 1) * ATT_HEAD_DIM, cols] = (
                o[g * ATT_HEAD_DIM:(g + 1) * ATT_HEAD_DIM, h * A:(h + 1) * A]
                * inv_rows[h]).astype(BF16)

    row128 = lax.broadcasted_iota(jnp.int32, (LANES, C), 0)
    ones_rows = (row128 == 0).astype(F32).astype(BF16)

    def mlstm_chunk(ci):
        tok = slice(ci * C, (ci + 1) * C)
        gates = gt_ref[:, tok] * LOG2_E
        cum_row = jnp.zeros(gates.shape, F32)
        for part in _split3(gates):
            cum_row = cum_row + _dot(part, tri_scr[...])
        ig_rows = gates[:ML_HEADS]
        b_rows = cum_row[ML_HEADS:]
        key_cols = jnp.concatenate([ig_rows - b_rows, jnp.zeros((LANES - ML_HEADS, C), F32)],
                                   axis=0).T
        for p in range(n_pairs):
            q_c = qm_ref[p * LANES:(p + 1) * LANES, tok]
            k_pair = km_ref[p * LANES:(p + 1) * LANES, tok]
            zero = jnp.zeros_like(q_c)
            state = state_scr[p]
            state_bf = state.astype(BF16)
            new_state = []
            for e_id in range(2):
                h = 2 * p + e_id
                v_rows = slice(h * ML_V_DIM, (h + 1) * ML_V_DIM)
                head_rows = (row128 < ML_QK_DIM) if e_id == 0 else (row128 >= ML_QK_DIM)
                q_pad = jnp.where(head_rows, q_c, zero)
                b_r = b_rows[h:h + 1, :]
                ig_r = ig_rows[h:h + 1, :]
                m_prev = m_scr[h:h + 1, 0:1]
                dm = (b_r + key_cols[:, h:h + 1]) + causal_scr[...]
                inter = b_r + m_prev
                m_row = jnp.maximum(inter, jnp.max(dm, axis=0, keepdims=True))
                w_inter = jnp.exp2(inter - m_row)
                qk = lax.dot_general(k_pair, q_pad, (((0,), (0,)), ((), ())),
                                     preferred_element_type=F32)
                p_t = (qk * jnp.exp2(dm - m_row)).astype(BF16)
                v_ext = jnp.concatenate([vm_ref[v_rows, tok], ones_rows], axis=0)
                num = _dot(v_ext, p_t) + w_inter * _dot(state_bf, q_pad)
                den = num[ML_V_DIM:ML_V_DIM + 1, :]
                hh = num[:ML_V_DIM] * (1.0 / jnp.maximum(jnp.abs(den), jnp.exp2(-m_row)))
                ms = jnp.mean(hh * hh, axis=0, keepdims=True)
                gate = _sigmoid(om_ref[v_rows, tok].astype(F32))
                mix_scr[ATT_Q_W + h * ML_V_DIM:ATT_Q_W + (h + 1) * ML_V_DIM, tok] = (
                    hh * lax.rsqrt(ms + EPS) * mlnw_ref[v_rows, :] * gate).astype(BF16)
                b_last = b_r[:, C - 1:C]
                a_r = b_last - b_r + ig_r
                m_new = jnp.maximum(b_last + m_prev, jnp.max(a_r, axis=-1, keepdims=True))
                sc = jnp.exp2(b_last + m_prev - m_new)
                wsv = (v_ext.astype(F32) * jnp.exp2(a_r - m_new)).astype(BF16)
                new_state.append(sc * state + _dot_nt(wsv, k_pair))
                m_scr[h:h + 1, :] = jnp.broadcast_to(m_new, (1, LANES))
            first = lax.broadcasted_iota(jnp.int32, state.shape, 1) < ML_QK_DIM
            state_scr[p] = jnp.where(first, new_state[0], new_state[1])
            yield

    def out_proj(ci):
        tok = slice(ci * C, (ci + 1) * C)
        mix_t = mix_scr[:, tok].T
        for n in range(D_MODEL // OUT_COLS):
            nc = slice(n * OUT_COLS, (n + 1) * OUT_COLS)
            x1_ref[tok, nc] = x_ref[tok, nc] + _dot(mix_t, wout_ref[:, nc])
            yield

    stage_scores(0)
    n_sub = C // A
    pairs = (step for ci in range(L // C) for step in mlstm_chunk(ci))
    projs = iter(())
    for j in range(L // A):
        if j and j % n_sub == 0:
            projs = out_proj(j // n_sub - 1)
        attend(j)
        next(projs, None)
        next(pairs, None)
        next(projs, None)
    for _ in pairs:
        pass
    for _ in projs:
        pass
    for _ in out_proj(L // C - 1):
        pass

    @pl.when(i == pl.num_programs(1) - 1)
    def _():
        for p in range(n_pairs):
            c_t = state_scr[p, :ML_V_DIM, :].T
            for e_id in range(2):
                ct_ref[0, 2 * p + e_id] = c_t[e_id * ML_QK_DIM:(e_id + 1) * ML_QK_DIM, :]
            nrow_ref[0, p:p + 1, :] = state_scr[p, ML_V_DIM:ML_V_DIM + 1, :]
        for h in range(ML_HEADS):
            m_ref[0, :, h:h + 1] = m_scr[h:h + 1, 0:1] * (1.0 / LOG2_E)
        kt_ref[0] = kvc_ref[:ATT_KV_W, L - WINDOW:]
        vt_ref[0] = kvc_ref[ATT_KV_W:, L - WINDOW:]


def _prompt_mixer_t(batch, seq, sinks, qa, ks, kv, qm, km, vm, om, gt, x2d, wout, mlnw_col):
    tq = MIX_TILE
    nt = seq // tq
    sub = tq // WINDOW
    per_slab = INPROJ_TILE // tq
    win_per_slab = INPROJ_TILE // WINDOW
    col = lambda w: pl.BlockSpec(
        (None, w, tq), lambda b, i: ((b * nt + i) // per_slab, 0, (b * nt + i) % per_slab))
    full = lambda a: pl.BlockSpec(a.shape, lambda b, i: (0,) * a.ndim)
    once = lambda a: pl.BlockSpec(a.shape, lambda b, i: (0,) * a.ndim,
                                  pipeline_mode=pl.Buffered(1))
    prev_block = lambda b, i: jnp.maximum((b * nt + i) * sub - 1, 0)
    per_batch = lambda *dims: pl.BlockSpec((1,) + dims, lambda b, i: (b,) + (0,) * len(dims))
    return pl.pallas_call(
        _prompt_mixer_t_kernel,
        grid=(batch, nt),
        in_specs=[pl.BlockSpec(memory_space=pltpu.SMEM),
                  col(ATT_Q_W),
                  pl.BlockSpec((tq, ATT_KV_W), lambda b, i: (b * nt + i, 0)),
                  pl.BlockSpec((WINDOW, ATT_KV_W), lambda b, i: (prev_block(b, i), 0)),
                  col(2 * ATT_KV_W),
                  pl.BlockSpec((None, 2 * ATT_KV_W, WINDOW),
                               lambda b, i: (prev_block(b, i) // win_per_slab, 0,
                                             prev_block(b, i) % win_per_slab)),
                  col(ML_QK_W), col(ML_QK_W), col(ML_V_W), col(ML_V_W), col(N_GATES),
                  pl.BlockSpec((tq, D_MODEL), lambda b, i: (b * nt + i, 0)),
                  once(wout), full(mlnw_col)],
        out_specs=[pl.BlockSpec((tq, D_MODEL), lambda b, i: (b * nt + i, 0)),
                   per_batch(ML_HEADS, ML_QK_DIM, ML_V_DIM),
                   per_batch(ML_HEADS // 2, LANES),
                   per_batch(1, ML_HEADS),
                   per_batch(ATT_KV_W, WINDOW),
                   per_batch(ATT_KV_W, WINDOW)],
        out_shape=[jax.ShapeDtypeStruct((batch * seq, D_MODEL), F32),
                   jax.ShapeDtypeStruct((batch, ML_HEADS, ML_QK_DIM, ML_V_DIM), F32),
                   jax.ShapeDtypeStruct((batch, ML_HEADS // 2, LANES), F32),
                   jax.ShapeDtypeStruct((batch, 1, ML_HEADS), F32),
                   jax.ShapeDtypeStruct((batch, ATT_KV_W, WINDOW), F32),
                   jax.ShapeDtypeStruct((batch, ATT_KV_W, WINDOW), F32)],
        scratch_shapes=[pltpu.VMEM((D_MODEL, tq), BF16),
                        pltpu.VMEM((ML_HEADS // 2, 2 * LANES, LANES), F32),
                        pltpu.VMEM((SUBLANES, LANES), F32),
                        pltpu.VMEM((2, 2 * WINDOW, WINDOW), F32),
                        pltpu.VMEM((ML_CHUNK, ML_CHUNK), F32),
                        pltpu.VMEM((ML_CHUNK, ML_CHUNK), BF16),
                        pltpu.VMEM((1, 2 * WINDOW, ATT_HEADS * WINDOW), F32),
                        pltpu.VMEM((1, 2 * WINDOW, ATT_HEADS * WINDOW), F32),
                        pltpu.VMEM((2 * WINDOW, ATT_HEADS * WINDOW), BF16)],
        compiler_params=pltpu.CompilerParams(dimension_semantics=("arbitrary", "arbitrary"),
                                             vmem_limit_bytes=VMEM_LIMIT),
        name="prompt_mixer_t",
    )(sinks, qa, ks, ks, kv, kv, qm, km, vm, om, gt, x2d, wout, mlnw_col)


def _sample_mixer_kernel(t_len, sinks_ref, qa_ref, kv_ref, ck_ref, cv_ref, qm_ref, km_ref,
                         vm_ref, om_ref, gt_ref, c0_ref, n0_ref, m0_ref, x_ref, wout_ref,
                         mlnw_ref, x1_ref, nk_ref, nv_ref, c_ref, n_ref, m_ref,
                         mix_scr, wperm_scr):
    bt = SAMPLE_BT
    T = t_len
    L = bt * T

    @pl.when(pl.program_id(0) == 0)
    def _():
        _permute_head_rows(wperm_scr, wout_ref)
        wperm_scr[ATT_Q_W:, :] = wout_ref[ATT_Q_W:, :]

    lane3 = lax.broadcasted_iota(jnp.int32, (bt, T, LANES), 2)
    low3 = lane3 < ATT_HEAD_DIM
    lane = lax.broadcasted_iota(jnp.int32, (L, LANES), 1)
    low = lane < ATT_HEAD_DIM

    qa3 = qa_ref[...].astype(F32).reshape(bt, T, ATT_Q_W)
    pieces = []
    for col in range(ATT_GROUP):
        qc = qa3[:, :, col * LANES:(col + 1) * LANES]
        pieces += [jnp.where(low3, qc, 0.0), jnp.where(low3, 0.0, qc)]
    q3 = jnp.concatenate(pieces, axis=1).astype(BF16)
    R = bt * N_STACK * T
    q2 = q3.reshape(R, LANES)
    kv_new = kv_ref[...]
    k_new = kv_new[:, :ATT_KV_W]
    v_new = kv_new[:, ATT_KV_W:]
    def attention_stages():
        s_c = jnp.einsum('bqd,bdk->bqk', q3, ck_ref[...].astype(BF16),
                         preferred_element_type=F32).reshape(R, WINDOW)
        s_n = _dot_nt(q2, k_new.astype(BF16))
        row_c = lax.broadcasted_iota(jnp.int32, (R, WINDOW), 0)
        col_c = lax.broadcasted_iota(jnp.int32, (R, WINDOW), 1)
        s_c = jnp.where(col_c > row_c % T, s_c, -jnp.inf)
        row_n = lax.broadcasted_iota(jnp.int32, (R, L), 0)
        col_n = lax.broadcasted_iota(jnp.int32, (R, L), 1)
        valid_n = (row_n // (N_STACK * T) == col_n // T) & (col_n % T <= row_n % T)
        s_n = jnp.where(valid_n, s_n, -jnp.inf)
        yield
        stack_id = (lax.broadcasted_iota(jnp.int32, (R, 1), 0) // T) % N_STACK
        sink = jnp.zeros((R, 1), F32)
        for k_id in range(N_STACK):
            sink = jnp.where(stack_id == k_id, sinks_ref[ATT_HEAD_ORDER[k_id]], sink)
        m = jnp.maximum(jnp.maximum(jnp.max(s_c, axis=-1, keepdims=True),
                                    jnp.max(s_n, axis=-1, keepdims=True)), sink)
        yield
        e_c = jnp.exp(s_c - m)
        e_n = jnp.exp(s_n - m)
        denom = (jnp.sum(e_c, axis=-1, keepdims=True) + jnp.sum(e_n, axis=-1, keepdims=True)
                 + jnp.exp(sink - m))
        yield
        o = jnp.einsum('bqk,bdk->bqd', e_c.astype(BF16).reshape(bt, N_STACK * T, WINDOW),
                       cv_ref[...].astype(BF16), preferred_element_type=F32).reshape(R, LANES)
        o = (o + _dot(e_n.astype(BF16), v_new.astype(BF16))) / denom
        yield
        o3 = o.reshape(bt, N_STACK * T, LANES)
        for col in range(ATT_GROUP):
            lo_h = o3[:, (2 * col) * T:(2 * col + 1) * T, :]
            hi_h = o3[:, (2 * col + 1) * T:(2 * col + 2) * T, :]
            mix_scr[:, col * LANES:(col + 1) * LANES] = jnp.where(
                low3, lo_h, hi_h).reshape(L, LANES).astype(BF16)
        yield

    keep = lax.broadcasted_iota(jnp.int32, (ATT_KV_W, WINDOW), 1) < WINDOW - T
    k_new_t = k_new.T
    v_new_t = v_new.T
    def roll_caches():
        for q in range(bt):
            shift = (WINDOW - T - q * T) % WINDOW
            nk_ref[q] = jnp.where(keep, pltpu.roll(ck_ref[q], WINDOW - T, axis=1),
                                  pltpu.roll(k_new_t, shift, axis=1))
            nv_ref[q] = jnp.where(keep, pltpu.roll(cv_ref[q], WINDOW - T, axis=1),
                                  pltpu.roll(v_new_t, shift, axis=1))
            if (q + 1) % (bt // ML_HEADS) == 0:
                yield

    rolls = roll_caches()

    r = lax.broadcasted_iota(jnp.int32, (L, L), 0)
    c = lax.broadcasted_iota(jnp.int32, (L, L), 1)
    seg = (r // T == c // T) & (r <= c)
    seg_bias = jnp.where(seg, 0.0, -jnp.inf)
    seg_bf = seg.astype(F32).astype(BF16)
    gates = gt_ref[...] * LOG2_E
    cum_row = jnp.zeros(gates.shape, F32)
    for part in _split3(gates):
        cum_row = cum_row + _dot(part, seg_bf)
    ig_rows = gates[:ML_HEADS]
    b_rows = cum_row[ML_HEADS:]
    gate_cols = jnp.concatenate([ig_rows, b_rows, jnp.zeros((LANES - N_GATES, L), F32)],
                                axis=0).T

    def col_to_row(x_col):
        return jnp.broadcast_to(x_col, (L, LANES)).T[0:1, :]

    ones_rows = (r[:LANES] == 0).astype(F32).astype(BF16)
    qm = qm_ref[...]
    km = km_ref[...]
    qm_f = qm.astype(F32)
    km_f = km.astype(F32)
    n_rep = bt * ML_QK_DIM // LANES
    bd_row = lax.broadcasted_iota(jnp.int32, (L, bt * ML_QK_DIM), 0) // T
    bd_lane = lax.broadcasted_iota(jnp.int32, (L, bt * ML_QK_DIM), 1) // ML_QK_DIM
    block_diag = bd_row == bd_lane

    def spread(x_pair, e):
        other = pltpu.roll(x_pair, ML_QK_DIM, axis=1)
        twice = jnp.where(low, x_pair, other) if e == 0 else jnp.where(low, other, x_pair)
        return jnp.where(block_diag, jnp.concatenate([twice] * n_rep, axis=1), 0.0).astype(BF16)

    def head_stages(h):
        p, e = divmod(h, 2)
        qc = qm[:, p * LANES:(p + 1) * LANES]
        k_pair = km[:, p * LANES:(p + 1) * LANES]
        zero = jnp.zeros_like(qc)
        q_pad = jnp.where(low, qc, zero) if e == 0 else jnp.where(low, zero, qc)
        v_h = vm_ref[:, h * ML_V_DIM:(h + 1) * ML_V_DIM]
        v_ext_t = jnp.concatenate([v_h.astype(F32).T.astype(BF16), ones_rows], axis=0)
        ig_c = gate_cols[:, h:h + 1]
        b_c = gate_cols[:, ML_HEADS + h:ML_HEADS + h + 1]
        b_r = b_rows[h:h + 1, :]
        yield
        m0 = m0_ref[:, :, h:h + 1] * LOG2_E
        inter = b_r + col_to_row(jnp.broadcast_to(m0, (bt, T, 1)).reshape(L, 1))
        dm = (b_r + (ig_c - b_c)) + seg_bias
        m_row = jnp.maximum(inter, jnp.max(dm, axis=0, keepdims=True))
        w_inter = jnp.exp2(inter - m_row)
        yield
        p_t = (_dot_nt(k_pair, q_pad) * jnp.exp2(dm - m_row)).astype(BF16)
        num_t = _dot(v_ext_t, p_t)
        yield
        q_h3 = qm_f[:, h * ML_QK_DIM:(h + 1) * ML_QK_DIM].reshape(bt, T, ML_QK_DIM)
        k_h3 = km_f[:, h * ML_QK_DIM:(h + 1) * ML_QK_DIM].reshape(bt, T, ML_QK_DIM)
        c0 = c0_ref[:, h]
        n0 = n0_ref[:, h:h + 1, :]
        q_c_t = _dot(spread(qm_f[:, p * LANES:(p + 1) * LANES], e),
                     c0.astype(BF16).reshape(bt * ML_QK_DIM, ML_V_DIM)).T
        q_n_r = col_to_row(jnp.sum(q_h3 * n0, axis=-1, keepdims=True).reshape(L, 1))
        yield
        num =num_t[:ML_V_DIM] + w_inter * q_c_t
        den = num_t[ML_V_DIM:ML_V_DIM + 1] + w_inter * q_n_r
        hh = num * (1.0 / jnp.maximum(jnp.abs(den), jnp.exp2(-m_row)))
        ms = jnp.mean(hh * hh, axis=0, keepdims=True)
        yield
        mix_scr[:, ATT_Q_W + h * ML_V_DIM:ATT_Q_W + (h + 1) * ML_V_DIM] = (
            (hh * lax.rsqrt(ms + EPS)).T * mlnw_ref[:, h * ML_V_DIM:(h + 1) * ML_V_DIM]
            * _sigmoid(om_ref[:, h * ML_V_DIM:(h + 1) * ML_V_DIM].astype(F32))).astype(BF16)
        yield
        b3 = b_c.reshape(bt, T, 1)
        b_last = b3[:, T - 1:T, :]
        a3 = b_last - b3 + ig_c.reshape(bt, T, 1)
        m_new = jnp.maximum(b_last + m0, jnp.max(a3, axis=1, keepdims=True))
        sc = jnp.exp2(b_last + m0 - m_new)
        ws = jnp.exp2(a3 - m_new)
        yield
        kw = spread(km_f[:, p * LANES:(p + 1) * LANES] * ws.reshape(L, 1), e)
        d_c = lax.dot_general(kw, v_h, (((0,), (0,)), ((), ())), preferred_element_type=F32)
        c_ref[:, h] = sc * c0 + d_c.reshape(bt, ML_QK_DIM, ML_V_DIM)
        n_ref[:, h:h + 1, :] = sc * n0 + jnp.sum(ws * k_h3, axis=1, keepdims=True)
        m_ref[:, :, h:h + 1] = m_new * (1.0 / LOG2_E)
        yield

    att = attention_stages()
    for _ in zip(*[head_stages(h) for h in range(ML_HEADS)]):
        next(att, None)
        next(rolls, None)
    for _ in att:
        pass
    for _ in rolls:
        pass

    x1_ref[...] = x_ref[...] + _dot(mix_scr[...], wperm_scr[...])


def _sample_mixer(nb, t_len, sinks, qa, kv, ck, cv, qm, km, vm, om, gt, c0, n0, m0, x2d, wout, mlnw):
    bt = SAMPLE_BT
    tl = bt * t_len
    row = lambda w: pl.BlockSpec((tl, w), lambda i: (i, 0))
    full = lambda a: pl.BlockSpec(a.shape, lambda i: (0,) * a.ndim)
    once = lambda a: pl.BlockSpec(a.shape, lambda i: (0,) * a.ndim, pipeline_mode=pl.Buffered(1))
    cache = pl.BlockSpec((bt, ATT_KV_W, WINDOW), lambda i: (i, 0, 0))
    c_spec = pl.BlockSpec((bt, ML_HEADS, ML_QK_DIM, ML_V_DIM), lambda i: (i, 0, 0, 0))
    n_spec = pl.BlockSpec((bt, ML_HEADS, ML_QK_DIM), lambda i: (i, 0, 0))
    m_spec = pl.BlockSpec((bt, 1, ML_HEADS), lambda i: (i, 0, 0))
    return pl.pallas_call(
        functools.partial(_sample_mixer_kernel, t_len),
        grid=(nb // bt,),
        in_specs=[pl.BlockSpec(memory_space=pltpu.SMEM),
                  row(ATT_Q_W), row(2 * ATT_KV_W), cache, cache, row(ML_QK_W), row(ML_QK_W),
                  row(ML_V_W), row(ML_V_W), pl.BlockSpec((N_GATES, tl), lambda i: (0, i)),
                  c_spec, n_spec, m_spec, row(D_MODEL), once(wout), full(mlnw)],
        out_specs=[row(D_MODEL), cache, cache, c_spec, n_spec, m_spec],
        out_shape=[jax.ShapeDtypeStruct((nb * t_len, D_MODEL), F32),
                   jax.ShapeDtypeStruct((nb, ATT_KV_W, WINDOW), F32),
                   jax.ShapeDtypeStruct((nb, ATT_KV_W, WINDOW), F32),
                   jax.ShapeDtypeStruct((nb, ML_HEADS, ML_QK_DIM, ML_V_DIM), F32),
                   jax.ShapeDtypeStruct((nb, ML_HEADS, ML_QK_DIM), F32),
                   jax.ShapeDtypeStruct((nb, 1, ML_HEADS), F32)],
        scratch_shapes=[pltpu.VMEM((tl, D_MODEL), BF16),
                        pltpu.VMEM((D_MODEL, D_MODEL), BF16)],
        compiler_params=pltpu.CompilerParams(dimension_semantics=("arbitrary",),
                                             vmem_limit_bytes=VMEM_LIMIT),
        name="sample_mixer",
    )(sinks, qa, kv, ck, cv, qm, km, vm, om, gt, c0, n0, m0, x2d, wout, mlnw)


def _ffn_kernel(seq_rows, *refs):
    if seq_rows is None:
        (x_ref, nw_ref, w_ref, cw_ref, cb_ref, wd_ref, y_ref, conv_ref,
         gbuf, act_scr, carry) = refs
        hist_ref = None
    else:
        (x_ref, hist_ref, nw_ref, w_ref, cw_ref, cb_ref, wd_ref, y_ref, conv_ref,
         gbuf, act_scr) = refs
        carry = None
    tm = x_ref.shape[0]
    tf = FF_CHUNK
    n_hist = CONV_W - 1
    rows = tm if seq_rows is None else seq_rows
    nseq = tm // rows
    base = SUBLANES
    n_chunks = D_FF // tf

    if carry is not None:
        @pl.when(pl.program_id(1) == 0)
        def _():
            carry[...] = jnp.zeros(carry.shape, F32)

    x = x_ref[...]
    h2 = _rms(x, nw_ref[...]).astype(BF16)

    def proj(f):
        return (_dot(h2, w_ref[:, f * tf:(f + 1) * tf]),
                _dot(h2, w_ref[:, D_FF + f * tf:D_FF + (f + 1) * tf]))

    nxt = proj(0)
    for f in range(n_chunks):
        g, u = nxt
        if f + 1 < n_chunks:
            nxt = proj(f + 1)
        cols = slice(f * tf, (f + 1) * tf)
        s = f % 2
        g3 = g.reshape(nseq, rows, tf)
        if seq_rows is None:
            gbuf[s, :, base - n_hist:base, :] = carry[:, SUBLANES - n_hist:, cols]
            carry[:, SUBLANES - n_hist:, cols] = g3[:, rows - n_hist:, :]
        else:
            gbuf[s, :, base - n_hist:base, :] = hist_ref[:, :, cols]
            conv_ref[:, :, cols] = g3[:, rows - n_hist:, :]
        gbuf[s, :, base:base + rows, :] = g3
        gc = cb_ref[:, cols] + g * cw_ref[CONV_W - 1:CONV_W, cols]
        for d in range(1, CONV_W):
            gm = gbuf[s, :, base - d:base - d + rows, :].reshape(tm, tf)
            gc = gc + gm * cw_ref[CONV_W - 1 - d:CONV_W - d, cols]
        act_scr[:, cols] = (gc * _sigmoid(gc) * u).astype(BF16)
    y_ref[...] = x + _dot(act_scr[...], wd_ref[...])

    if carry is not None:
        @pl.when(pl.program_id(1) == pl.num_programs(1) - 1)
        def _():
            conv_ref[...] = carry[:, SUBLANES - n_hist:, :]


def _ffn_scratch(tm, rows):
    return [pltpu.VMEM((2, tm // rows, SUBLANES + rows, FF_CHUNK), F32),
            pltpu.VMEM((tm, D_FF), BF16)]


def _ffn_prompt(batch, seq, x2d, nw, w, cw, cb, wd):
    tm = FFN_TILE
    nt = seq // tm
    full = lambda a: pl.BlockSpec(a.shape, lambda b, i: (0,) * a.ndim)
    once = lambda a: pl.BlockSpec(a.shape, lambda b, i: (0,) * a.ndim,
                                  pipeline_mode=pl.Buffered(1))
    row = pl.BlockSpec((tm, D_MODEL), lambda b, i: (b * nt + i, 0))
    return pl.pallas_call(
        functools.partial(_ffn_kernel, None),
        grid=(batch, nt),
        in_specs=[row, full(nw), once(w), full(cw), full(cb), once(wd)],
        out_specs=[row, pl.BlockSpec((1, CONV_W - 1, D_FF), lambda b, i: (b, 0, 0))],
        out_shape=[jax.ShapeDtypeStruct((batch * seq, D_MODEL), F32),
                   jax.ShapeDtypeStruct((batch, CONV_W - 1, D_FF), F32)],
        scratch_shapes=_ffn_scratch(tm, tm) + [pltpu.VMEM((1, SUBLANES, D_FF), F32)],
        compiler_params=pltpu.CompilerParams(dimension_semantics=("arbitrary", "arbitrary"),
                                             vmem_limit_bytes=VMEM_LIMIT),
        name="ffn_prompt",
    )(x2d, nw, w, cw, cb, wd)


def _ffn_sample(nb, t_len, x2d, hist, nw, w, cw, cb, wd):
    tm = ROW_TILE
    bt = tm // t_len
    full = lambda a: pl.BlockSpec(a.shape, lambda i: (0,) * a.ndim)
    once = lambda a: pl.BlockSpec(a.shape, lambda i: (0,) * a.ndim, pipeline_mode=pl.Buffered(1))
    row = pl.BlockSpec((tm, D_MODEL), lambda i: (i, 0))
    hist_spec = pl.BlockSpec((bt, CONV_W - 1, D_FF), lambda i: (i, 0, 0))
    return pl.pallas_call(
        functools.partial(_ffn_kernel, t_len),
        grid=(nb // bt,),
        in_specs=[row, hist_spec, full(nw), once(w), full(cw), full(cb), once(wd)],
        out_specs=[row, hist_spec],
        out_shape=[jax.ShapeDtypeStruct((nb * t_len, D_MODEL), F32),
                   jax.ShapeDtypeStruct((nb, CONV_W - 1, D_FF), F32)],
        scratch_shapes=_ffn_scratch(tm, t_len),
        compiler_params=pltpu.CompilerParams(dimension_semantics=("arbitrary",),
                                             vmem_limit_bytes=VMEM_LIMIT),
        name="ffn_sample",
    )(x2d, hist, nw, w, cw, cb, wd)


def _head_mean_matrix(width, head_dim):
    idx = np.arange(width) // head_dim
    return jnp.asarray((idx[:, None] == idx[None, :]).astype(np.float32) / head_dim, dtype=BF16)


def _layer_weights(norm_mix_w, w_in, b_gates, q_norm_w, k_norm_w, sinks, ml_norm_w, w_out,
                   norm_ffn_w, w_ffn_in, conv_w, conv_b, w_down):
    w_in_t = jnp.pad(w_in.T.astype(BF16), ((0, IN_WIDTH_PAD - w_in.shape[1]), (0, 0)))
    return dict(
        nw=norm_mix_w.reshape(1, D_MODEL),
        w_in_t=w_in_t,
        bg=jnp.pad(b_gates, (0, LANES - N_GATES)).reshape(1, LANES),
        qnw=(jnp.tile(q_norm_w, ATT_HEADS) * ATT_SCALE).reshape(1, ATT_Q_W),
        knw=jnp.tile(k_norm_w, ATT_KV_HEADS).reshape(1, ATT_KV_W),
        gq=_head_mean_matrix(ATT_Q_W, ATT_HEAD_DIM),
        gk=_head_mean_matrix(ATT_KV_W, ATT_HEAD_DIM),
        bg_col=b_gates.reshape(N_GATES, 1),
        qnw_col=(jnp.tile(q_norm_w, ATT_HEADS) * (ATT_SCALE * LOG2_E)).reshape(ATT_Q_W, 1),
        knw_col=jnp.tile(k_norm_w, ATT_KV_HEADS).reshape(ATT_KV_W, 1),
        mlnw_col=ml_norm_w.reshape(ML_V_W, 1),
        sinks=sinks,
        mlnw=ml_norm_w.reshape(1, ML_V_W),
        wout=w_out.astype(BF16),
        nfw=norm_ffn_w.reshape(1, D_MODEL),
        wff=w_ffn_in.astype(BF16),
        cw=conv_w,
        cb=conv_b.reshape(1, D_FF),
        wd=w_down.astype(BF16),
    )


def _cache_from_t(a_t):
    n = a_t.shape[0]
    return jnp.transpose(a_t.reshape(n, ATT_KV_HEADS, ATT_HEAD_DIM, WINDOW), (0, 3, 1, 2))


def _cache_to_t(a):
    n = a.shape[0]
    return jnp.transpose(a, (0, 2, 3, 1)).reshape(n, ATT_KV_W, WINDOW)


def _prompt_layer(x, w):
    batch, seq, _ = x.shape
    assert seq % INPROJ_TILE == 0 and INPROJ_TILE % MIX_TILE == 0
    assert MIX_TILE % ML_CHUNK == 0 and ML_CHUNK % WINDOW == 0
    assert seq % FFN_TILE == 0
    x2d = x.reshape(batch * seq, D_MODEL)
    qa, ks, kv, qm, km, vm, om, gt = _inproj_t(x2d, w["nw"], w["w_in_t"], w["bg_col"],
                                               w["qnw_col"], w["knw_col"])
    x1, c_t, n_row, m, k_t, v_t = _prompt_mixer_t(batch, seq, w["sinks"], qa, ks, kv, qm, km, vm,
                                                  om, gt, x2d, w["wout"], w["mlnw_col"])
    y, conv = _ffn_prompt(batch, seq, x1, w["nfw"], w["wff"], w["cw"], w["cb"], w["wd"])
    return (y.reshape(batch, seq, D_MODEL), _cache_from_t(k_t), _cache_from_t(v_t),
            jnp.swapaxes(c_t, -1, -2), n_row.reshape(batch, ML_HEADS, ML_QK_DIM),
            m.reshape(batch, ML_HEADS), conv)


def _sample_layer(x, ck, cv, c0, n0, m0, conv_buf, w):
    nb, t_len, _ = x.shape
    assert t_len == SUBLANES and SAMPLE_BT * t_len == LANES and nb % SAMPLE_BT == 0
    assert (nb * t_len) % ROW_TILE == 0
    x2d = x.reshape(nb * t_len, D_MODEL)
    qa, kv, qm, km, vm, om, gt = _inproj(x2d, w["nw"], w["w_in_t"], w["bg"], w["qnw"], w["knw"],
                                         w["gq"], w["gk"])
    x1, nk_t, nv_t, c_t, n, m = _sample_mixer(
        nb, t_len, w["sinks"], qa, kv, _cache_to_t(ck), _cache_to_t(cv), qm, km, vm, om, gt,
        jnp.swapaxes(c0, -1, -2), n0, m0.reshape(nb, 1, ML_HEADS), x2d, w["wout"], w["mlnw"])
    y, conv = _ffn_sample(nb, t_len, x1, conv_buf, w["nfw"], w["wff"], w["cw"], w["cb"],
                          w["wd"])
    return (y.reshape(nb, t_len, D_MODEL), _cache_from_t(nk_t), _cache_from_t(nv_t),
            jnp.swapaxes(c_t, -1, -2), n, m.reshape(nb, ML_HEADS), conv)


def kernel(x_prompt, x_sample, cache_attn_k, cache_attn_v, state_mlstm_C, state_mlstm_n,
           state_mlstm_m, cache_ffn_conv, norm_mix_w, w_in, b_gates, q_norm_w, k_norm_w,
           sinks, ml_norm_w, w_out, norm_ffn_w, w_ffn_in, conv_w, conv_b, w_down):
    depth = w_in.shape[0]
    yp, ys = x_prompt, x_sample
    sp = [[] for _ in range(6)]
    ss = [[] for _ in range(6)]
    for l in range(depth):
        w = _layer_weights(norm_mix_w[l], w_in[l], b_gates[l], q_norm_w[l], k_norm_w[l], sinks[l],
                           ml_norm_w[l], w_out[l], norm_ffn_w[l], w_ffn_in[l], conv_w[l],
                           conv_b[l], w_down[l])
        yp, *st_p = _prompt_layer(yp, w)
        ys, *st_s = _sample_layer(ys, cache_attn_k[l], cache_attn_v[l], state_mlstm_C[l],
                                  state_mlstm_n[l], state_mlstm_m[l], cache_ffn_conv[l], w)
        for i in range(6):
            sp[i].append(st_p[i])
            ss[i].append(st_s[i])
    k_p, v_p, c_p, n_p, m_p, conv_p = [jnp.stack(a) for a in sp]
    k_s, v_s, c_s, n_s, m_s, conv_s = [jnp.stack(a) for a in ss]
    return (yp, ys, k_p, v_p, c_p, n_p, m_p, conv_p, k_s, v_s, c_s, n_s, m_s, conv_s)
```

```python
import functools

import numpy as np
import jax
import jax.numpy as jnp
from jax import lax
from jax.experimental import pallas as pl
from jax.experimental.pallas import tpu as pltpu

F32 = jnp.float32
BF16 = jnp.bfloat16

D_MODEL = 1024
ATT_HEADS = 8
ATT_KV_HEADS = 2
ATT_HEAD_DIM = 64
ATT_GROUP = ATT_HEADS // ATT_KV_HEADS
WINDOW = 128
ML_HEADS = 4
ML_V_DIM = 128
ML_QK_DIM = 64
D_FF = 2816
CONV_W = 3
EPS = 1e-6
ATT_SCALE = ATT_HEAD_DIM ** -0.5
ML_SCALE = ML_QK_DIM ** -0.5
LOG2_E = 1.4426950408889634

ATT_Q_W = ATT_HEADS * ATT_HEAD_DIM
ATT_KV_W = ATT_KV_HEADS * ATT_HEAD_DIM
ML_QK_W = ML_HEADS * ML_QK_DIM
ML_V_W = ML_HEADS * ML_V_DIM
N_GATES = 2 * ML_HEADS
N_STACK = 2 * ATT_GROUP

LANES = 128
SUBLANES = 8

OFF_QA = 0
OFF_KV = OFF_QA + ATT_Q_W
OFF_QM = OFF_KV + 2 * ATT_KV_W
OFF_KM = OFF_QM + ML_QK_W
OFF_VM = OFF_KM + ML_QK_W
OFF_OM = OFF_VM + ML_V_W
OFF_GL = OFF_OM + ML_V_W
IN_WIDTH_PAD = OFF_GL + LANES

ATT_HEAD_ORDER = tuple(h for c in range(ATT_GROUP) for h in (c, c + ATT_GROUP))

ROW_TILE = 512
FFN_TILE = 1024
INPROJ_SUB = 256
NORM_ROWS = 128
INPROJ_TILE = 1024
MIX_TILE = 512
ML_CHUNK = 256
OUT_COLS = 256
FF_CHUNK = 256
SAMPLE_BT = 16
ROLL_SEQS = 4
VMEM_LIMIT = 56 * 1024 * 1024


def _dot(a, b):
    return jnp.dot(a, b, preferred_element_type=F32)


def _dot_nt(a, b):
    return lax.dot_general(a, b, (((1,), (1,)), ((), ())), preferred_element_type=F32)


def _split3(x):
    hi = x.astype(BF16)
    r1 = x - hi.astype(F32)
    mid = r1.astype(BF16)
    lo = (r1 - mid.astype(F32)).astype(BF16)
    return hi, mid, lo


def _rms(x, w):
    ms = jnp.mean(x * x, axis=-1, keepdims=True)
    return x * lax.rsqrt(ms + EPS) * w


def _log_sigmoid(x):
    return jnp.minimum(x, 0.0) - jnp.log1p(jnp.exp(-jnp.abs(x)))


def _sigmoid(x):
    return 1.0 / (1.0 + jnp.exp(-x))


def _permute_head_rows(dst_ref, src_ref):
    for k, h in enumerate(ATT_HEAD_ORDER):
        dst_ref[k * ATT_HEAD_DIM:(k + 1) * ATT_HEAD_DIM, :] = (
            src_ref[h * ATT_HEAD_DIM:(h + 1) * ATT_HEAD_DIM, :])


def _inproj_kernel(x_ref, nw_ref, w_ref, bg_ref, qnw_ref, knw_ref, gq_ref, gk_ref,
                   qa_ref, kv_ref, qm_ref, km_ref, vm_ref, om_ref, gt_ref, wq_scr):
    @pl.when(pl.program_id(0) == 0)
    def _():
        _permute_head_rows(wq_scr, w_ref)

    h = _rms(x_ref[...], nw_ref[...]).astype(BF16)

    def proj(lo, width):
        return _dot_nt(h, w_ref[lo:lo + width, :])

    q = _dot_nt(h, wq_scr[...])
    q_ms = _dot((q * q).astype(BF16), gq_ref[...])
    qa_ref[...] = (q * lax.rsqrt(q_ms + EPS) * qnw_ref[...]).astype(BF16)

    kv = proj(OFF_KV, 2 * ATT_KV_W)
    k = kv[:, :ATT_KV_W]
    k_ms = _dot((k * k).astype(BF16), gk_ref[...])
    kv_ref[:, :ATT_KV_W] = k * lax.rsqrt(k_ms + EPS) * knw_ref[...]
    kv_ref[:, ATT_KV_W:] = kv[:, ATT_KV_W:]

    qm_ref[...] = (proj(OFF_QM, ML_QK_W) * ML_SCALE).astype(BF16)
    km_ref[...] = proj(OFF_KM, ML_QK_W).astype(BF16)
    vm_ref[...] = proj(OFF_VM, ML_V_W).astype(BF16)
    om_ref[...] = proj(OFF_OM, ML_V_W).astype(BF16)

    gl = proj(OFF_GL, LANES) + bg_ref[...]
    lane = lax.broadcasted_iota(jnp.int32, gl.shape, 1)
    g = jnp.where(lane < ML_HEADS, gl, _log_sigmoid(gl))
    gt_ref[...] = g.T[:N_GATES, :]


def _inproj(x2d, nw, w_in_t, bg, qnw, knw, gq, gk):
    n = x2d.shape[0]
    tm = ROW_TILE
    row = lambda w: pl.BlockSpec((tm, w), lambda i: (i, 0))
    full = lambda a: pl.BlockSpec(a.shape, lambda i: (0,) * a.ndim)
    once = lambda a: pl.BlockSpec(a.shape, lambda i: (0,) * a.ndim, pipeline_mode=pl.Buffered(1))
    return pl.pallas_call(
        _inproj_kernel,
        grid=(n // tm,),
        in_specs=[row(D_MODEL), full(nw), once(w_in_t), full(bg), full(qnw), full(knw),
                  full(gq), full(gk)],
        out_specs=[row(ATT_Q_W), row(2 * ATT_KV_W), row(ML_QK_W), row(ML_QK_W),
                   row(ML_V_W), row(ML_V_W), pl.BlockSpec((N_GATES, tm), lambda i: (0, i))],
        out_shape=[jax.ShapeDtypeStruct((n, ATT_Q_W), BF16),
                   jax.ShapeDtypeStruct((n, 2 * ATT_KV_W), F32),
                   jax.ShapeDtypeStruct((n, ML_QK_W), BF16),
                   jax.ShapeDtypeStruct((n, ML_QK_W), BF16),
                   jax.ShapeDtypeStruct((n, ML_V_W), BF16),
                   jax.ShapeDtypeStruct((n, ML_V_W), BF16),
                   jax.ShapeDtypeStruct((N_GATES, n), F32)],
        scratch_shapes=[pltpu.VMEM((ATT_Q_W, D_MODEL), BF16)],
        compiler_params=pltpu.CompilerParams(dimension_semantics=("arbitrary",),
                                             vmem_limit_bytes=VMEM_LIMIT),
        name="inproj",
    )(x2d, nw, w_in_t, bg, qnw, knw, gq, gk)


def _head_norm_t(z, head_dim, w_col):
    rows, tokens = z.shape
    z3 = z.reshape(rows // head_dim, head_dim, tokens)
    ms = jnp.mean(z3 * z3, axis=1, keepdims=True)
    return (z3 * lax.rsqrt(ms + EPS)).reshape(rows, tokens) * w_col


def _inproj_t_kernel(x_ref, nw_ref, w_ref, bg_ref, qnw_ref, knw_ref,
                     qa_ref, ks_ref, kv_ref, qm_ref, km_ref, vm_ref, om_ref, gt_ref, h_scr):
    tm = x_ref.shape[0]
    sub = INPROJ_SUB

    def norm_rows(c):
        for r0 in range(c * sub, (c + 1) * sub, NORM_ROWS):
            rows = slice(r0, r0 + NORM_ROWS)
            h_scr[:, rows] = _rms(x_ref[rows, :], nw_ref[...]).T.astype(BF16)
            yield

    def project(c):
        tok = slice(c * sub, (c + 1) * sub)
        h_t = h_scr[:, tok]

        def proj(lo, width):
            return _dot(w_ref[lo:lo + width, :], h_t)

        qa_ref[:, tok] = _head_norm_t(proj(OFF_QA, ATT_Q_W), ATT_HEAD_DIM,
                                      qnw_ref[...]).astype(BF16)
        yield
        kv = proj(OFF_KV, 2 * ATT_KV_W)
        k = _head_norm_t(kv[:ATT_KV_W], ATT_HEAD_DIM, knw_ref[...])
        kv_ref[:ATT_KV_W, tok] = k
        kv_ref[ATT_KV_W:, tok] = kv[ATT_KV_W:]
        ks_ref[tok, :] = k.T.astype(BF16)
        qm_ref[:, tok] = (proj(OFF_QM, ML_QK_W) * ML_SCALE).astype(BF16)
        yield
        km_ref[:, tok] = proj(OFF_KM, ML_QK_W).astype(BF16)
        vm_ref[:, tok] = proj(OFF_VM, ML_V_W).astype(BF16)
        yield
        om_ref[:, tok] = proj(OFF_OM, ML_V_W).astype(BF16)
        gl = proj(OFF_GL, 2 * SUBLANES)[:N_GATES] + bg_ref[...]
        row = lax.broadcasted_iota(jnp.int32, gl.shape, 0)
        gt_ref[:, tok] = jnp.where(row < ML_HEADS, gl, _log_sigmoid(gl))
        yield

    for _ in norm_rows(0):
        pass
    for c in range(tm // sub):
        norms = norm_rows(c + 1) if c + 1 < tm // sub else iter(())
        for _ in project(c):
            next(norms, None)
        for _ in norms:
            pass


def _inproj_t(x2d, nw, w_in_t, bg_col, qnw_col, knw_col):
    n = x2d.shape[0]
    tm = INPROJ_TILE
    full = lambda a: pl.BlockSpec(a.shape, lambda i: (0,) * a.ndim)
    once = lambda a: pl.BlockSpec(a.shape, lambda i: (0,) * a.ndim, pipeline_mode=pl.Buffered(1))
    col = lambda w: pl.BlockSpec((None, w, tm), lambda i: (i, 0, 0))
    slab = lambda w, dt: jax.ShapeDtypeStruct((n // tm, w, tm), dt)
    return pl.pallas_call(
        _inproj_t_kernel,
        grid=(n // tm,),
        in_specs=[pl.BlockSpec((tm, D_MODEL), lambda i: (i, 0)), full(nw), once(w_in_t),
                  full(bg_col), full(qnw_col), full(knw_col)],
        out_specs=[col(ATT_Q_W), pl.BlockSpec((tm, ATT_KV_W), lambda i: (i, 0)),
                   col(2 * ATT_KV_W), col(ML_QK_W), col(ML_QK_W), col(ML_V_W), col(ML_V_W),
                   col(N_GATES)],
        out_shape=[slab(ATT_Q_W, BF16),
                   jax.ShapeDtypeStruct((n, ATT_KV_W), BF16),
                   slab(2 * ATT_KV_W, F32), slab(ML_QK_W, BF16), slab(ML_QK_W, BF16),
                   slab(ML_V_W, BF16), slab(ML_V_W, BF16), slab(N_GATES, F32)],
        scratch_shapes=[pltpu.VMEM((D_MODEL, tm), BF16)],
        compiler_params=pltpu.CompilerParams(dimension_semantics=("arbitrary",),
                                             vmem_limit_bytes=VMEM_LIMIT),
        name="inproj_t",
    )(x2d, nw, w_in_t, bg_col, qnw_col, knw_col)


def _prompt_mixer_t_kernel(sinks_ref, qa_ref, ksc_ref, ksp_ref, kvc_ref, kvp_ref, qm_ref, km_ref,
                           vm_ref, om_ref, gt_ref, x_ref, wout_ref, mlnw_ref,
                           x1_ref, ct_ref, nrow_ref, m_ref, kt_ref, vt_ref,
                           mix_scr, state_scr, m_scr, band_scr, causal_scr, tri_scr,
                           s_scr_a, s_scr_b, e_scr):
    i = pl.program_id(1)
    A = WINDOW
    L = MIX_TILE
    C = ML_CHUNK
    n_pairs = ML_HEADS // 2

    @pl.when(i == 0)
    def _():
        state_scr[...] = jnp.zeros(state_scr.shape, F32)
        m_scr[...] = jnp.zeros(m_scr.shape, F32)
        kj = lax.broadcasted_iota(jnp.int32, (2 * A, A), 0)
        qi = lax.broadcasted_iota(jnp.int32, (2 * A, A), 1)
        band = (kj > qi) & (kj <= qi + WINDOW)
        band_scr[0] = jnp.where(band, 0.0, -jnp.inf)
        band_scr[1] = jnp.where(band & (kj >= A), 0.0, -jnp.inf)
        r = lax.broadcasted_iota(jnp.int32, (C, C), 0)
        c = lax.broadcasted_iota(jnp.int32, (C, C), 1)
        causal_scr[...] = jnp.where(r <= c, 0.0, -jnp.inf)
        tri_scr[...] = (r <= c).astype(F32).astype(BF16)

    k_all = jnp.concatenate([ksp_ref[...], ksc_ref[...]], axis=0)
    v_all = jnp.concatenate([kvp_ref[ATT_KV_W:, :], kvc_ref[ATT_KV_W:, :]], axis=1).astype(BF16)
    zero_q = jnp.zeros((ATT_HEAD_DIM, A), BF16)
    slot = 0
    s_bufs = (s_scr_a, s_scr_b)

    def stage_scores(j):
        pieces = []
        for h in range(ATT_HEADS):
            q_h = qa_ref[h * ATT_HEAD_DIM:(h + 1) * ATT_HEAD_DIM, j * A:(j + 1) * A]
            pieces.append(jnp.concatenate([q_h, zero_q] if h < ATT_GROUP else [zero_q, q_h],
                                          axis=0))
        s_bufs[j % 2][slot] = _dot(k_all[j * A:(j + 2) * A, :], jnp.concatenate(pieces, axis=1))

    def attend(j):
        cols = slice(j * A, (j + 1) * A)
        vt = v_all[:, j * A:(j + 2) * A]
        if j + 1 < L // A:
            stage_scores(j + 1)
        s_buf = s_bufs[j % 2]
        bias = jnp.where(i > 0, band_scr[0], band_scr[1]) if j == 0 else band_scr[0]
        m_rows = []
        for h in range(ATT_HEADS):
            sb = s_buf[slot, :, h * A:(h + 1) * A] + bias
            m_rows.append(jnp.maximum(jnp.max(sb, axis=0, keepdims=True),
                                      sinks_ref[h] * LOG2_E))
        inv_rows = []
        for h in range(ATT_HEADS):
            e = jnp.exp2(s_buf[slot, :, h * A:(h + 1) * A] + (bias - m_rows[h]))
            e_scr[:, h * A:(h + 1) * A] = e.astype(BF16)
            inv_rows.append(1.0 / (jnp.sum(e, axis=0, keepdims=True)
                                   + jnp.exp2(sinks_ref[h] * LOG2_E - m_rows[h])))
        o = _dot(vt, e_scr[...])
        for h in range(ATT_HEADS):
            g = h // ATT_GROUP
            mix_scr[h * ATT_HEAD_DIM:(h + 1) * ATT_HEAD_DIM, cols] = (
                o[g * ATT_HEAD_DIM:(g + 1) * ATT_HEAD_DIM, h * A:(h + 1) * A]
                * inv_rows[h]).astype(BF16)

    row128 = lax.broadcasted_iota(jnp.int32, (LANES, C), 0)
    ones_rows = (row128 == 0).astype(F32).astype(BF16)

    def mlstm_chunk(ci):
        tok = slice(ci * C, (ci + 1) * C)
        gates = gt_ref[:, tok] * LOG2_E
        cum_row = jnp.zeros(gates.shape, F32)
        for part in _split3(gates):
            cum_row = cum_row + _dot(part, tri_scr[...])
        ig_rows = gates[:ML_HEADS]
        b_rows = cum_row[ML_HEADS:]
        key_cols = jnp.concatenate([ig_rows - b_rows, jnp.zeros((LANES - ML_HEADS, C), F32)],
                                   axis=0).T
        for p in range(n_pairs):
            q_c = qm_ref[p * LANES:(p + 1) * LANES, tok]
            k_pair = km_ref[p * LANES:(p + 1) * LANES, tok]
            zero = jnp.zeros_like(q_c)
            state = state_scr[p]
            state_bf = state.astype(BF16)
            new_state = []
            for e_id in range(2):
                h = 2 * p + e_id
                v_rows = slice(h * ML_V_DIM, (h + 1) * ML_V_DIM)
                head_rows = (row128 < ML_QK_DIM) if e_id == 0 else (row128 >= ML_QK_DIM)
                q_pad = jnp.where(head_rows, q_c, zero)
                b_r = b_rows[h:h + 1, :]
                ig_r = ig_rows[h:h + 1, :]
                m_prev = m_scr[h:h + 1, 0:1]
                dm = (b_r + key_cols[:, h:h + 1]) + causal_scr[...]
                inter = b_r + m_prev
                m_row = jnp.maximum(inter, jnp.max(dm, axis=0, keepdims=True))
                w_inter = jnp.exp2(inter - m_row)
                qk = lax.dot_general(k_pair, q_pad, (((0,), (0,)), ((), ())),
                                     preferred_element_type=F32)
                p_t = (qk * jnp.exp2(dm - m_row)).astype(BF16)
                v_ext = jnp.concatenate([vm_ref[v_rows, tok], ones_rows], axis=0)
                num = _dot(v_ext, p_t) + w_inter * _dot(state_bf, q_pad)
                den = num[ML_V_DIM:ML_V_DIM + 1, :]
                hh = num[:ML_V_DIM] * (1.0 / jnp.maximum(jnp.abs(den), jnp.exp2(-m_row)))
                ms = jnp.mean(hh * hh, axis=0, keepdims=True)
                gate = _sigmoid(om_ref[v_rows, tok].astype(F32))
                mix_scr[ATT_Q_W + h * ML_V_DIM:ATT_Q_W + (h + 1) * ML_V_DIM, tok] = (
                    hh * lax.rsqrt(ms + EPS) * mlnw_ref[v_rows, :] * gate).astype(BF16)
                b_last = b_r[:, C - 1:C]
                a_r = b_last - b_r + ig_r
                m_new = jnp.maximum(b_last + m_prev, jnp.max(a_r, axis=-1, keepdims=True))
                sc = jnp.exp2(b_last + m_prev - m_new)
                wsv = (v_ext.astype(F32) * jnp.exp2(a_r - m_new)).astype(BF16)
                new_state.append(sc * state + _dot_nt(wsv, k_pair))
                m_scr[h:h + 1, :] = jnp.broadcast_to(m_new, (1, LANES))
            first = lax.broadcasted_iota(jnp.int32, state.shape, 1) < ML_QK_DIM
            state_scr[p] = jnp.where(first, new_state[0], new_state[1])
            yield

    def out_proj(ci):
        tok = slice(ci * C, (ci + 1) * C)
        mix_t = mix_scr[:, tok].T
        for n in range(D_MODEL // OUT_COLS):
            nc = slice(n * OUT_COLS, (n + 1) * OUT_COLS)
            x1_ref[tok, nc] = x_ref[tok, nc] + _dot(mix_t, wout_ref[:, nc])
            yield

    stage_scores(0)
    n_sub = C // A
    pairs = (step for ci in range(L // C) for step in mlstm_chunk(ci))
    projs = iter(())
    for j in range(L // A):
        if j and j % n_sub == 0:
            projs = out_proj(j // n_sub - 1)
        attend(j)
        next(projs, None)
        next(pairs, None)
        next(projs, None)
    for _ in pairs:
        pass
    for _ in projs:
        pass
    for _ in out_proj(L // C - 1):
        pass

    @pl.when(i == pl.num_programs(1) - 1)
    def _():
        for p in range(n_pairs):
            c_t = state_scr[p, :ML_V_DIM, :].T
            for e_id in range(2):
                ct_ref[0, 2 * p + e_id] = c_t[e_id * ML_QK_DIM:(e_id + 1) * ML_QK_DIM, :]
            nrow_ref[0, p:p + 1, :] = state_scr[p, ML_V_DIM:ML_V_DIM + 1, :]
        for h in range(ML_HEADS):
            m_ref[0, :, h:h + 1] = m_scr[h:h + 1, 0:1] * (1.0 / LOG2_E)
        kt_ref[0] = kvc_ref[:ATT_KV_W, L - WINDOW:]
        vt_ref[0] = kvc_ref[ATT_KV_W:, L - WINDOW:]


def _prompt_mixer_t(batch, seq, sinks, qa, ks, kv, qm, km, vm, om, gt, x2d, wout, mlnw_col):
    tq = MIX_TILE
    nt = seq // tq
    sub = tq // WINDOW
    per_slab = INPROJ_TILE // tq
    win_per_slab = INPROJ_TILE // WINDOW
    col = lambda w: pl.BlockSpec(
        (None, w, tq), lambda b, i: ((b * nt + i) // per_slab, 0, (b * nt + i) % per_slab))
    full = lambda a: pl.BlockSpec(a.shape, lambda b, i: (0,) * a.ndim)
    once = lambda a: pl.BlockSpec(a.shape, lambda b, i: (0,) * a.ndim,
                                  pipeline_mode=pl.Buffered(1))
    prev_block = lambda b, i: jnp.maximum((b * nt + i) * sub - 1, 0)
    per_batch = lambda *dims: pl.BlockSpec((1,) + dims, lambda b, i: (b,) + (0,) * len(dims))
    return pl.pallas_call(
        _prompt_mixer_t_kernel,
        grid=(batch, nt),
        in_specs=[pl.BlockSpec(memory_space=pltpu.SMEM),
                  col(ATT_Q_W),
                  pl.BlockSpec((tq, ATT_KV_W), lambda b, i: (b * nt + i, 0)),
                  pl.BlockSpec((WINDOW, ATT_KV_W), lambda b, i: (prev_block(b, i), 0)),
                  col(2 * ATT_KV_W),
                  pl.BlockSpec((None, 2 * ATT_KV_W, WINDOW),
                               lambda b, i: (prev_block(b, i) // win_per_slab, 0,
                                             prev_block(b, i) % win_per_slab)),
                  col(ML_QK_W), col(ML_QK_W), col(ML_V_W), col(ML_V_W), col(N_GATES),
                  pl.BlockSpec((tq, D_MODEL), lambda b, i: (b * nt + i, 0)),
                  once(wout), full(mlnw_col)],
        out_specs=[pl.BlockSpec((tq, D_MODEL), lambda b, i: (b * nt + i, 0)),
                   per_batch(ML_HEADS, ML_QK_DIM, ML_V_DIM),
                   per_batch(ML_HEADS // 2, LANES),
                   per_batch(1, ML_HEADS),
                   per_batch(ATT_KV_W, WINDOW),
                   per_batch(ATT_KV_W, WINDOW)],
        out_shape=[jax.ShapeDtypeStruct((batch * seq, D_MODEL), F32),
                   jax.ShapeDtypeStruct((batch, ML_HEADS, ML_QK_DIM, ML_V_DIM), F32),
                   jax.ShapeDtypeStruct((batch, ML_HEADS // 2, LANES), F32),
                   jax.ShapeDtypeStruct((batch, 1, ML_HEADS), F32),
                   jax.ShapeDtypeStruct((batch, ATT_KV_W, WINDOW), F32),
                   jax.ShapeDtypeStruct((batch, ATT_KV_W, WINDOW), F32)],
        scratch_shapes=[pltpu.VMEM((D_MODEL, tq), BF16),
                        pltpu.VMEM((ML_HEADS // 2, 2 * LANES, LANES), F32),
                        pltpu.VMEM((SUBLANES, LANES), F32),
                        pltpu.VMEM((2, 2 * WINDOW, WINDOW), F32),
                        pltpu.VMEM((ML_CHUNK, ML_CHUNK), F32),
                        pltpu.VMEM((ML_CHUNK, ML_CHUNK), BF16),
                        pltpu.VMEM((1, 2 * WINDOW, ATT_HEADS * WINDOW), F32),
                        pltpu.VMEM((1, 2 * WINDOW, ATT_HEADS * WINDOW), F32),
                        pltpu.VMEM((2 * WINDOW, ATT_HEADS * WINDOW), BF16)],
        compiler_params=pltpu.CompilerParams(dimension_semantics=("arbitrary", "arbitrary"),
                                             vmem_limit_bytes=VMEM_LIMIT),
        name="prompt_mixer_t",
    )(sinks, qa, ks, ks, kv, kv, qm, km, vm, om, gt, x2d, wout, mlnw_col)


def _sample_mixer_kernel(t_len, sinks_ref, qa_ref, kv_ref, ck_ref, cv_ref, qm_ref, km_ref,
                         vm_ref, om_ref, gt_ref, c0_ref, n0_ref, m0_ref, x_ref, wout_ref,
                         mlnw_ref, x1_ref, nk_ref, nv_ref, c_ref, n_ref, m_ref,
                         mix_scr, wperm_scr):
    bt = SAMPLE_BT
    T = t_len
    L = bt * T

    @pl.when(pl.program_id(0) == 0)
    def _():
        _permute_head_rows(wperm_scr, wout_ref)
        wperm_scr[ATT_Q_W:, :] = wout_ref[ATT_Q_W:, :]

    lane3 = lax.broadcasted_iota(jnp.int32, (bt, T, LANES), 2)
    low3 = lane3 < ATT_HEAD_DIM
    lane = lax.broadcasted_iota(jnp.int32, (L, LANES), 1)
    low = lane < ATT_HEAD_DIM

    qa3 = qa_ref[...].astype(F32).reshape(bt, T, ATT_Q_W)
    pieces = []
    for col in range(ATT_GROUP):
        qc = qa3[:, :, col * LANES:(col + 1) * LANES]
        pieces += [jnp.where(low3, qc, 0.0), jnp.where(low3, 0.0, qc)]
    q3 = jnp.concatenate(pieces, axis=1).astype(BF16)
    R = bt * N_STACK * T
    q2 = q3.reshape(R, LANES)
    kv_new = kv_ref[...]
    k_new = kv_new[:, :ATT_KV_W]
    v_new = kv_new[:, ATT_KV_W:]
    def attention_stages():
        s_c = jnp.einsum('bqd,bdk->bqk', q3, ck_ref[...].astype(BF16),
                         preferred_element_type=F32).reshape(R, WINDOW)
        s_n = _dot_nt(q2, k_new.astype(BF16))
        row_c = lax.broadcasted_iota(jnp.int32, (R, WINDOW), 0)
        col_c = lax.broadcasted_iota(jnp.int32, (R, WINDOW), 1)
        s_c = jnp.where(col_c > row_c % T, s_c, -jnp.inf)
        row_n = lax.broadcasted_iota(jnp.int32, (R, L), 0)
        col_n = lax.broadcasted_iota(jnp.int32, (R, L), 1)
        valid_n = (row_n // (N_STACK * T) == col_n // T) & (col_n % T <= row_n % T)
        s_n = jnp.where(valid_n, s_n, -jnp.inf)
        yield
        stack_id = (lax.broadcasted_iota(jnp.int32, (R, 1), 0) // T) % N_STACK
        sink = jnp.zeros((R, 1), F32)
        for k_id in range(N_STACK):
            sink = jnp.where(stack_id == k_id, sinks_ref[ATT_HEAD_ORDER[k_id]], sink)
        m = jnp.maximum(jnp.maximum(jnp.max(s_c, axis=-1, keepdims=True),
                                    jnp.max(s_n, axis=-1, keepdims=True)), sink)
        yield
        e_c = jnp.exp(s_c - m)
        e_n = jnp.exp(s_n - m)
        denom = (jnp.sum(e_c, axis=-1, keepdims=True) + jnp.sum(e_n, axis=-1, keepdims=True)
                 + jnp.exp(sink - m))
        yield
        o = jnp.einsum('bqk,bdk->bqd', e_c.astype(BF16).reshape(bt, N_STACK * T, WINDOW),
                       cv_ref[...].astype(BF16), preferred_element_type=F32).reshape(R, LANES)
        o = (o + _dot(e_n.astype(BF16), v_new.astype(BF16))) / denom
        yield
        o3 = o.reshape(bt, N_STACK * T, LANES)
        for col in range(ATT_GROUP):
            lo_h = o3[:, (2 * col) * T:(2 * col + 1) * T, :]
            hi_h = o3[:, (2 * col + 1) * T:(2 * col + 2) * T, :]
            mix_scr[:, col * LANES:(col + 1) * LANES] = jnp.where(
                low3, lo_h, hi_h).reshape(L, LANES).astype(BF16)
        yield

    keep = lax.broadcasted_iota(jnp.int32, (ATT_KV_W, WINDOW), 1) < WINDOW - T
    k_new_t = k_new.T
    v_new_t = v_new.T
    def roll_caches():
        for q in range(bt):
            shift = (WINDOW - T - q * T) % WINDOW
            nk_ref[q] = jnp.where(keep, pltpu.roll(ck_ref[q], WINDOW - T, axis=1),
                                  pltpu.roll(k_new_t, shift, axis=1))
            nv_ref[q] = jnp.where(keep, pltpu.roll(cv_ref[q], WINDOW - T, axis=1),
                                  pltpu.roll(v_new_t, shift, axis=1))
            if (q + 1) % ROLL_SEQS == 0:
                yield

    rolls = roll_caches()

    r = lax.broadcasted_iota(jnp.int32, (L, L), 0)
    c = lax.broadcasted_iota(jnp.int32, (L, L), 1)
    seg = (r // T == c // T) & (r <= c)
    seg_bias = jnp.where(seg, 0.0, -jnp.inf)
    seg_bf = seg.astype(F32).astype(BF16)
    gates = gt_ref[...] * LOG2_E
    cum_row = jnp.zeros(gates.shape, F32)
    for part in _split3(gates):
        cum_row = cum_row + _dot(part, seg_bf)
    ig_rows = gates[:ML_HEADS]
    b_rows = cum_row[ML_HEADS:]
    gate_cols = jnp.concatenate([ig_rows, b_rows, jnp.zeros((LANES - N_GATES, L), F32)],
                                axis=0).T

    def col_to_row(x_col):
        return jnp.broadcast_to(x_col, (L, LANES)).T[0:1, :]

    ones_rows = (r[:LANES] == 0).astype(F32).astype(BF16)
    qm = qm_ref[...]
    km = km_ref[...]
    qm_f = qm.astype(F32)
    km_f = km.astype(F32)
    n_rep = bt * ML_QK_DIM // LANES
    bd_row = lax.broadcasted_iota(jnp.int32, (L, bt * ML_QK_DIM), 0) // T
    bd_lane = lax.broadcasted_iota(jnp.int32, (L, bt * ML_QK_DIM), 1) // ML_QK_DIM
    block_diag = bd_row == bd_lane

    def spread(x_pair, e):
        other = pltpu.roll(x_pair, ML_QK_DIM, axis=1)
        twice = jnp.where(low, x_pair, other) if e == 0 else jnp.where(low, other, x_pair)
        return jnp.where(block_diag, jnp.concatenate([twice] * n_rep, axis=1), 0.0).astype(BF16)

    def head_stages(h):
        p, e = divmod(h, 2)
        qc = qm[:, p * LANES:(p + 1) * LANES]
        k_pair = km[:, p * LANES:(p + 1) * LANES]
        zero = jnp.zeros_like(qc)
        q_pad = jnp.where(low, qc, zero) if e == 0 else jnp.where(low, zero, qc)
        v_h = vm_ref[:, h * ML_V_DIM:(h + 1) * ML_V_DIM]
        v_ext_t = jnp.concatenate([v_h.astype(F32).T.astype(BF16), ones_rows], axis=0)
        ig_c = gate_cols[:, h:h + 1]
        b_c = gate_cols[:, ML_HEADS + h:ML_HEADS + h + 1]
        b_r = b_rows[h:h + 1, :]
        yield
        m0 = m0_ref[:, :, h:h + 1] * LOG2_E
        inter = b_r + col_to_row(jnp.broadcast_to(m0, (bt, T, 1)).reshape(L, 1))
        dm = (b_r + (ig_c - b_c)) + seg_bias
        m_row = jnp.maximum(inter, jnp.max(dm, axis=0, keepdims=True))
        w_inter = jnp.exp2(inter - m_row)
        yield
        p_t = (_dot_nt(k_pair, q_pad) * jnp.exp2(dm - m_row)).astype(BF16)
        num_t = _dot(v_ext_t, p_t)
        yield
        q_h3 = qm_f[:, h * ML_QK_DIM:(h + 1) * ML_QK_DIM].reshape(bt, T, ML_QK_DIM)
        k_h3 = km_f[:, h * ML_QK_DIM:(h + 1) * ML_QK_DIM].reshape(bt, T, ML_QK_DIM)
        c0 = c0_ref[:, h]
        n0 = n0_ref[:, h:h + 1, :]
        q_c_t = _dot(spread(qm_f[:, p * LANES:(p + 1) * LANES], e),
                     c0.astype(BF16).reshape(bt * ML_QK_DIM, ML_V_DIM)).T
        q_n_r = col_to_row(jnp.sum(q_h3 * n0, axis=-1, keepdims=True).reshape(L, 1))
        yield
        num =num_t[:ML_V_DIM] + w_inter * q_c_t
        den = num_t[ML_V_DIM:ML_V_DIM + 1] + w_inter * q_n_r
        hh = num * (1.0 / jnp.maximum(jnp.abs(den), jnp.exp2(-m_row)))
        ms = jnp.mean(hh * hh, axis=0, keepdims=True)
        yield
        mix_scr[:, ATT_Q_W + h * ML_V_DIM:ATT_Q_W + (h + 1) * ML_V_DIM] = (
            (hh * lax.rsqrt(ms + EPS)).T * mlnw_ref[:, h * ML_V_DIM:(h + 1) * ML_V_DIM]
            * _sigmoid(om_ref[:, h * ML_V_DIM:(h + 1) * ML_V_DIM].astype(F32))).astype(BF16)
        yield
        b3 = b_c.reshape(bt, T, 1)
        b_last = b3[:, T - 1:T, :]
        a3 = b_last - b3 + ig_c.reshape(bt, T, 1)
        m_new = jnp.maximum(b_last + m0, jnp.max(a3, axis=1, keepdims=True))
        sc = jnp.exp2(b_last + m0 - m_new)
        ws = jnp.exp2(a3 - m_new)
        yield
        kw = spread(km_f[:, p * LANES:(p + 1) * LANES] * ws.reshape(L, 1), e)
        d_c = lax.dot_general(kw, v_h, (((0,), (0,)), ((), ())), preferred_element_type=F32)
        c_ref[:, h] = sc * c0 + d_c.reshape(bt, ML_QK_DIM, ML_V_DIM)
        n_ref[:, h:h + 1, :] = sc * n0 + jnp.sum(ws * k_h3, axis=1, keepdims=True)
        m_ref[:, :, h:h + 1] = m_new * (1.0 / LOG2_E)
        yield

    att = attention_stages()
    for _ in zip(*[head_stages(h) for h in range(ML_HEADS)]):
        next(rolls, None)
        next(att, None)
    for _ in att:
        pass
    for _ in rolls:
        pass

    x1_ref[...] = x_ref[...] + _dot(mix_scr[...], wperm_scr[...])


def _sample_mixer(nb, t_len, sinks, qa, kv, ck, cv, qm, km, vm, om, gt, c0, n0, m0, x2d, wout, mlnw):
    bt = SAMPLE_BT
    tl = bt * t_len
    row = lambda w: pl.BlockSpec((tl, w), lambda i: (i, 0))
    full = lambda a: pl.BlockSpec(a.shape, lambda i: (0,) * a.ndim)
    once = lambda a: pl.BlockSpec(a.shape, lambda i: (0,) * a.ndim, pipeline_mode=pl.Buffered(1))
    cache = pl.BlockSpec((bt, ATT_KV_W, WINDOW), lambda i: (i, 0, 0))
    c_spec = pl.BlockSpec((bt, ML_HEADS, ML_QK_DIM, ML_V_DIM), lambda i: (i, 0, 0, 0))
    n_spec = pl.BlockSpec((bt, ML_HEADS, ML_QK_DIM), lambda i: (i, 0, 0))
    m_spec = pl.BlockSpec((bt, 1, ML_HEADS), lambda i: (i, 0, 0))
    return pl.pallas_call(
        functools.partial(_sample_mixer_kernel, t_len),
        grid=(nb // bt,),
        in_specs=[pl.BlockSpec(memory_space=pltpu.SMEM),
                  row(ATT_Q_W), row(2 * ATT_KV_W), cache, cache, row(ML_QK_W), row(ML_QK_W),
                  row(ML_V_W), row(ML_V_W), pl.BlockSpec((N_GATES, tl), lambda i: (0, i)),
                  c_spec, n_spec, m_spec, row(D_MODEL), once(wout), full(mlnw)],
        out_specs=[row(D_MODEL), cache, cache, c_spec, n_spec, m_spec],
        out_shape=[jax.ShapeDtypeStruct((nb * t_len, D_MODEL), F32),
                   jax.ShapeDtypeStruct((nb, ATT_KV_W, WINDOW), F32),
                   jax.ShapeDtypeStruct((nb, ATT_KV_W, WINDOW), F32),
                   jax.ShapeDtypeStruct((nb, ML_HEADS, ML_QK_DIM, ML_V_DIM), F32),
                   jax.ShapeDtypeStruct((nb, ML_HEADS, ML_QK_DIM), F32),
                   jax.ShapeDtypeStruct((nb, 1, ML_HEADS), F32)],
        scratch_shapes=[pltpu.VMEM((tl, D_MODEL), BF16),
                        pltpu.VMEM((D_MODEL, D_MODEL), BF16)],
        compiler_params=pltpu.CompilerParams(dimension_semantics=("arbitrary",),
                                             vmem_limit_bytes=VMEM_LIMIT),
        name="sample_mixer",
    )(sinks, qa, kv, ck, cv, qm, km, vm, om, gt, c0, n0, m0, x2d, wout, mlnw)


def _ffn_kernel(seq_rows, *refs):
    if seq_rows is None:
        (x_ref, nw_ref, w_ref, cw_ref, cb_ref, wd_ref, y_ref, conv_ref,
         gbuf, act_scr, carry) = refs
        hist_ref = None
    else:
        (x_ref, hist_ref, nw_ref, w_ref, cw_ref, cb_ref, wd_ref, y_ref, conv_ref,
         gbuf, act_scr) = refs
        carry = None
    tm = x_ref.shape[0]
    tf = FF_CHUNK
    n_hist = CONV_W - 1
    rows = tm if seq_rows is None else seq_rows
    nseq = tm // rows
    base = SUBLANES
    n_chunks = D_FF // tf

    if carry is not None:
        @pl.when(pl.program_id(1) == 0)
        def _():
            carry[...] = jnp.zeros(carry.shape, F32)

    x = x_ref[...]
    h2 = _rms(x, nw_ref[...]).astype(BF16)

    def proj(f):
        return (_dot(h2, w_ref[:, f * tf:(f + 1) * tf]),
                _dot(h2, w_ref[:, D_FF + f * tf:D_FF + (f + 1) * tf]))

    nxt = proj(0)
    for f in range(n_chunks):
        g, u = nxt
        if f + 1 < n_chunks:
            nxt = proj(f + 1)
        cols = slice(f * tf, (f + 1) * tf)
        s = f % 2
        g3 = g.reshape(nseq, rows, tf)
        if seq_rows is None:
            gbuf[s, :, base - n_hist:base, :] = carry[:, SUBLANES - n_hist:, cols]
            carry[:, SUBLANES - n_hist:, cols] = g3[:, rows - n_hist:, :]
        else:
            gbuf[s, :, base - n_hist:base, :] = hist_ref[:, :, cols]
            conv_ref[:, :, cols] = g3[:, rows - n_hist:, :]
        gbuf[s, :, base:base + rows, :] = g3
        gc = cb_ref[:, cols] + g * cw_ref[CONV_W - 1:CONV_W, cols]
        for d in range(1, CONV_W):
            gm = gbuf[s, :, base - d:base - d + rows, :].reshape(tm, tf)
            gc = gc + gm * cw_ref[CONV_W - 1 - d:CONV_W - d, cols]
        act_scr[:, cols] = (gc * _sigmoid(gc) * u).astype(BF16)
    y_ref[...] = x + _dot(act_scr[...], wd_ref[...])

    if carry is not None:
        @pl.when(pl.program_id(1) == pl.num_programs(1) - 1)
        def _():
            conv_ref[...] = carry[:, SUBLANES - n_hist:, :]


def _ffn_scratch(tm, rows):
    return [pltpu.VMEM((2, tm // rows, SUBLANES + rows, FF_CHUNK), F32),
            pltpu.VMEM((tm, D_FF), BF16)]


def _ffn_prompt(batch, seq, x2d, nw, w, cw, cb, wd):
    tm = FFN_TILE
    nt = seq // tm
    full = lambda a: pl.BlockSpec(a.shape, lambda b, i: (0,) * a.ndim)
    once = lambda a: pl.BlockSpec(a.shape, lambda b, i: (0,) * a.ndim,
                                  pipeline_mode=pl.Buffered(1))
    row = pl.BlockSpec((tm, D_MODEL), lambda b, i: (b * nt + i, 0))
    return pl.pallas_call(
        functools.partial(_ffn_kernel, None),
        grid=(batch, nt),
        in_specs=[row, full(nw), once(w), full(cw), full(cb), once(wd)],
        out_specs=[row, pl.BlockSpec((1, CONV_W - 1, D_FF), lambda b, i: (b, 0, 0))],
        out_shape=[jax.ShapeDtypeStruct((batch * seq, D_MODEL), F32),
                   jax.ShapeDtypeStruct((batch, CONV_W - 1, D_FF), F32)],
        scratch_shapes=_ffn_scratch(tm, tm) + [pltpu.VMEM((1, SUBLANES, D_FF), F32)],
        compiler_params=pltpu.CompilerParams(dimension_semantics=("arbitrary", "arbitrary"),
                                             vmem_limit_bytes=VMEM_LIMIT),
        name="ffn_prompt",
    )(x2d, nw, w, cw, cb, wd)


def _ffn_sample(nb, t_len, x2d, hist, nw, w, cw, cb, wd):
    tm = ROW_TILE
    bt = tm // t_len
    full = lambda a: pl.BlockSpec(a.shape, lambda i: (0,) * a.ndim)
    once = lambda a: pl.BlockSpec(a.shape, lambda i: (0,) * a.ndim, pipeline_mode=pl.Buffered(1))
    row = pl.BlockSpec((tm, D_MODEL), lambda i: (i, 0))
    hist_spec = pl.BlockSpec((bt, CONV_W - 1, D_FF), lambda i: (i, 0, 0))
    return pl.pallas_call(
        functools.partial(_ffn_kernel, t_len),
        grid=(nb // bt,),
        in_specs=[row, hist_spec, full(nw), once(w), full(cw), full(cb), once(wd)],
        out_specs=[row, hist_spec],
        out_shape=[jax.ShapeDtypeStruct((nb * t_len, D_MODEL), F32),
                   jax.ShapeDtypeStruct((nb, CONV_W - 1, D_FF), F32)],
        scratch_shapes=_ffn_scratch(tm, t_len),
        compiler_params=pltpu.CompilerParams(dimension_semantics=("arbitrary",),
                                             vmem_limit_bytes=VMEM_LIMIT),
        name="ffn_sample",
    )(x2d, hist, nw, w, cw, cb, wd)


def _head_mean_matrix(width, head_dim):
    idx = np.arange(width) // head_dim
    return jnp.asarray((idx[:, None] == idx[None, :]).astype(np.float32) / head_dim, dtype=BF16)


def _layer_weights(norm_mix_w, w_in, b_gates, q_norm_w, k_norm_w, sinks, ml_norm_w, w_out,
                   norm_ffn_w, w_ffn_in, conv_w, conv_b, w_down):
    w_in_t = jnp.pad(w_in.T.astype(BF16), ((0, IN_WIDTH_PAD - w_in.shape[1]), (0, 0)))
    return dict(
        nw=norm_mix_w.reshape(1, D_MODEL),
        w_in_t=w_in_t,
        bg=jnp.pad(b_gates, (0, LANES - N_GATES)).reshape(1, LANES),
        qnw=(jnp.tile(q_norm_w, ATT_HEADS) * ATT_SCALE).reshape(1, ATT_Q_W),
        knw=jnp.tile(k_norm_w, ATT_KV_HEADS).reshape(1, ATT_KV_W),
        gq=_head_mean_matrix(ATT_Q_W, ATT_HEAD_DIM),
        gk=_head_mean_matrix(ATT_KV_W, ATT_HEAD_DIM),
        bg_col=b_gates.reshape(N_GATES, 1),
        qnw_col=(jnp.tile(q_norm_w, ATT_HEADS) * (ATT_SCALE * LOG2_E)).reshape(ATT_Q_W, 1),
        knw_col=jnp.tile(k_norm_w, ATT_KV_HEADS).reshape(ATT_KV_W, 1),
        mlnw_col=ml_norm_w.reshape(ML_V_W, 1),
        sinks=sinks,
        mlnw=ml_norm_w.reshape(1, ML_V_W),
        wout=w_out.astype(BF16),
        nfw=norm_ffn_w.reshape(1, D_MODEL),
        wff=w_ffn_in.astype(BF16),
        cw=conv_w,
        cb=conv_b.reshape(1, D_FF),
        wd=w_down.astype(BF16),
    )


def _cache_from_t(a_t):
    n = a_t.shape[0]
    return jnp.transpose(a_t.reshape(n, ATT_KV_HEADS, ATT_HEAD_DIM, WINDOW), (0, 3, 1, 2))


def _cache_to_t(a):
    n = a.shape[0]
    return jnp.transpose(a, (0, 2, 3, 1)).reshape(n, ATT_KV_W, WINDOW)


def _prompt_layer(x, w):
    batch, seq, _ = x.shape
    assert seq % INPROJ_TILE == 0 and INPROJ_TILE % MIX_TILE == 0
    assert MIX_TILE % ML_CHUNK == 0 and ML_CHUNK % WINDOW == 0
    assert seq % FFN_TILE == 0
    x2d = x.reshape(batch * seq, D_MODEL)
    qa, ks, kv, qm, km, vm, om, gt = _inproj_t(x2d, w["nw"], w["w_in_t"], w["bg_col"],
                                               w["qnw_col"], w["knw_col"])
    x1, c_t, n_row, m, k_t, v_t = _prompt_mixer_t(batch, seq, w["sinks"], qa, ks, kv, qm, km, vm,
                                                  om, gt, x2d, w["wout"], w["mlnw_col"])
    y, conv = _ffn_prompt(batch, seq, x1, w["nfw"], w["wff"], w["cw"], w["cb"], w["wd"])
    return (y.reshape(batch, seq, D_MODEL), _cache_from_t(k_t), _cache_from_t(v_t),
            jnp.swapaxes(c_t, -1, -2), n_row.reshape(batch, ML_HEADS, ML_QK_DIM),
            m.reshape(batch, ML_HEADS), conv)


def _sample_layer(x, ck, cv, c0, n0, m0, conv_buf, w):
    nb, t_len, _ = x.shape
    assert t_len == SUBLANES and SAMPLE_BT * t_len == LANES and nb % SAMPLE_BT == 0
    assert (nb * t_len) % ROW_TILE == 0
    x2d = x.reshape(nb * t_len, D_MODEL)
    qa, kv, qm, km, vm, om, gt = _inproj(x2d, w["nw"], w["w_in_t"], w["bg"], w["qnw"], w["knw"],
                                         w["gq"], w["gk"])
    x1, nk_t, nv_t, c_t, n, m = _sample_mixer(
        nb, t_len, w["sinks"], qa, kv, _cache_to_t(ck), _cache_to_t(cv), qm, km, vm, om, gt,
        jnp.swapaxes(c0, -1, -2), n0, m0.reshape(nb, 1, ML_HEADS), x2d, w["wout"], w["mlnw"])
    y, conv = _ffn_sample(nb, t_len, x1, conv_buf, w["nfw"], w["wff"], w["cw"], w["cb"],
                          w["wd"])
    return (y.reshape(nb, t_len, D_MODEL), _cache_from_t(nk_t), _cache_from_t(nv_t),
            jnp.swapaxes(c_t, -1, -2), n, m.reshape(nb, ML_HEADS), conv)


def kernel(x_prompt, x_sample, cache_attn_k, cache_attn_v, state_mlstm_C, state_mlstm_n,
           state_mlstm_m, cache_ffn_conv, norm_mix_w, w_in, b_gates, q_norm_w, k_norm_w,
           sinks, ml_norm_w, w_out, norm_ffn_w, w_ffn_in, conv_w, conv_b, w_down):
    depth = w_in.shape[0]
    yp, ys = x_prompt, x_sample
    sp = [[] for _ in range(6)]
    ss = [[] for _ in range(6)]
    for l in range(depth):
        w = _layer_weights(norm_mix_w[l], w_in[l], b_gates[l], q_norm_w[l], k_norm_w[l], sinks[l],
                           ml_norm_w[l], w_out[l], norm_ffn_w[l], w_ffn_in[l], conv_w[l],
                           conv_b[l], w_down[l])
        yp, *st_p = _prompt_layer(yp, w)
        ys, *st_s = _sample_layer(ys, cache_attn_k[l], cache_attn_v[l], state_mlstm_C[l],
                                  state_mlstm_n[l], state_mlstm_m[l], cache_ffn_conv[l], w)
        for i in range(6):
            sp[i].append(st_p[i])
            ss[i].append(st_s[i])
    k_p, v_p, c_p, n_p, m_p, conv_p = [jnp.stack(a) for a in sp]
    k_s, v_s, c_s, n_s, m_s, conv_s = [jnp.stack(a) for a in ss]
    return (yp, ys, k_p, v_p, c_p, n_p, m_p, conv_p, k_s, v_s, c_s, n_s, m_s, conv_s)
```

```python
import functools

import numpy as np
import jax
import jax.numpy as jnp
from jax import lax
from jax.experimental import pallas as pl
from jax.experimental.pallas import tpu as pltpu

F32 = jnp.float32
BF16 = jnp.bfloat16

D_MODEL = 1024
ATT_HEADS = 8
ATT_KV_HEADS = 2
ATT_HEAD_DIM = 64
ATT_GROUP = ATT_HEADS // ATT_KV_HEADS
WINDOW = 128
ML_HEADS = 4
ML_V_DIM = 128
ML_QK_DIM = 64
D_FF = 2816
CONV_W = 3
EPS = 1e-6
ATT_SCALE = ATT_HEAD_DIM ** -0.5
ML_SCALE = ML_QK_DIM ** -0.5
LOG2_E = 1.4426950408889634

ATT_Q_W = ATT_HEADS * ATT_HEAD_DIM
ATT_KV_W = ATT_KV_HEADS * ATT_HEAD_DIM
ML_QK_W = ML_HEADS * ML_QK_DIM
ML_V_W = ML_HEADS * ML_V_DIM
N_GATES = 2 * ML_HEADS
N_STACK = 2 * ATT_GROUP

LANES = 128
SUBLANES = 8

OFF_QA = 0
OFF_KV = OFF_QA + ATT_Q_W
OFF_QM = OFF_KV + 2 * ATT_KV_W
OFF_KM = OFF_QM + ML_QK_W
OFF_VM = OFF_KM + ML_QK_W
OFF_OM = OFF_VM + ML_V_W
OFF_GL = OFF_OM + ML_V_W
IN_WIDTH_PAD = OFF_GL + LANES

ATT_HEAD_ORDER = tuple(h for c in range(ATT_GROUP) for h in (c, c + ATT_GROUP))

ROW_TILE = 512
FFN_TILE = 1024
INPROJ_SUB = 256
NORM_ROWS = 128
INPROJ_TILE = 1024
MIX_TILE = 512
ML_CHUNK = 256
OUT_COLS = 256
FF_CHUNK = 256
SAMPLE_BT = 16
ROLL_SEQS = 4
VMEM_LIMIT = 56 * 1024 * 1024


def _dot(a, b):
    return jnp.dot(a, b, preferred_element_type=F32)


def _dot_nt(a, b):
    return lax.dot_general(a, b, (((1,), (1,)), ((), ())), preferred_element_type=F32)


def _split3(x):
    hi = x.astype(BF16)
    r1 = x - hi.astype(F32)
    mid = r1.astype(BF16)
    lo = (r1 - mid.astype(F32)).astype(BF16)
    return hi, mid, lo


def _rms(x, w):
    ms = jnp.mean(x * x, axis=-1, keepdims=True)
    return x * lax.rsqrt(ms + EPS) * w


def _log_sigmoid(x):
    return jnp.minimum(x, 0.0) - jnp.log1p(jnp.exp(-jnp.abs(x)))


def _sigmoid(x):
    return 1.0 / (1.0 + jnp.exp(-x))


def _permute_head_rows(dst_ref, src_ref):
    for k, h in enumerate(ATT_HEAD_ORDER):
        dst_ref[k * ATT_HEAD_DIM:(k + 1) * ATT_HEAD_DIM, :] = (
            src_ref[h * ATT_HEAD_DIM:(h + 1) * ATT_HEAD_DIM, :])


def _inproj_kernel(x_ref, nw_ref, w_ref, bg_ref, qnw_ref, knw_ref, gq_ref, gk_ref,
                   qa_ref, kv_ref, qm_ref, km_ref, vm_ref, om_ref, gt_ref, wq_scr):
    @pl.when(pl.program_id(0) == 0)
    def _():
        _permute_head_rows(wq_scr, w_ref)

    h = _rms(x_ref[...], nw_ref[...]).astype(BF16)

    def proj(lo, width):
        return _dot_nt(h, w_ref[lo:lo + width, :])

    q = _dot_nt(h, wq_scr[...])
    q_ms = _dot((q * q).astype(BF16), gq_ref[...])
    qa_ref[...] = (q * lax.rsqrt(q_ms + EPS) * qnw_ref[...]).astype(BF16)

    kv = proj(OFF_KV, 2 * ATT_KV_W)
    k = kv[:, :ATT_KV_W]
    k_ms = _dot((k * k).astype(BF16), gk_ref[...])
    kv_ref[:, :ATT_KV_W] = k * lax.rsqrt(k_ms + EPS) * knw_ref[...]
    kv_ref[:, ATT_KV_W:] = kv[:, ATT_KV_W:]

    qm_ref[...] = (proj(OFF_QM, ML_QK_W) * ML_SCALE).astype(BF16)
    km_ref[...] = proj(OFF_KM, ML_QK_W).astype(BF16)
    vm_ref[...] = proj(OFF_VM, ML_V_W).astype(BF16)
    om_ref[...] = proj(OFF_OM, ML_V_W).astype(BF16)

    gl = proj(OFF_GL, LANES) + bg_ref[...]
    lane = lax.broadcasted_iota(jnp.int32, gl.shape, 1)
    g = jnp.where(lane < ML_HEADS, gl, _log_sigmoid(gl))
    gt_ref[...] = g.T[:N_GATES, :]


def _inproj(x2d, nw, w_in_t, bg, qnw, knw, gq, gk):
    n = x2d.shape[0]
    tm = ROW_TILE
    row = lambda w: pl.BlockSpec((tm, w), lambda i: (i, 0))
    full = lambda a: pl.BlockSpec(a.shape, lambda i: (0,) * a.ndim)
    once = lambda a: pl.BlockSpec(a.shape, lambda i: (0,) * a.ndim, pipeline_mode=pl.Buffered(1))
    return pl.pallas_call(
        _inproj_kernel,
        grid=(n // tm,),
        in_specs=[row(D_MODEL), full(nw), once(w_in_t), full(bg), full(qnw), full(knw),
                  full(gq), full(gk)],
        out_specs=[row(ATT_Q_W), row(2 * ATT_KV_W), row(ML_QK_W), row(ML_QK_W),
                   row(ML_V_W), row(ML_V_W), pl.BlockSpec((N_GATES, tm), lambda i: (0, i))],
        out_shape=[jax.ShapeDtypeStruct((n, ATT_Q_W), BF16),
                   jax.ShapeDtypeStruct((n, 2 * ATT_KV_W), F32),
                   jax.ShapeDtypeStruct((n, ML_QK_W), BF16),
                   jax.ShapeDtypeStruct((n, ML_QK_W), BF16),
                   jax.ShapeDtypeStruct((n, ML_V_W), BF16),
                   jax.ShapeDtypeStruct((n, ML_V_W), BF16),
                   jax.ShapeDtypeStruct((N_GATES, n), F32)],
        scratch_shapes=[pltpu.VMEM((ATT_Q_W, D_MODEL), BF16)],
        compiler_params=pltpu.CompilerParams(dimension_semantics=("arbitrary",),
                                             vmem_limit_bytes=VMEM_LIMIT),
        name="inproj",
    )(x2d, nw, w_in_t, bg, qnw, knw, gq, gk)


def _head_norm_t(z, head_dim, w_col):
    rows, tokens = z.shape
    z3 = z.reshape(rows // head_dim, head_dim, tokens)
    ms = jnp.mean(z3 * z3, axis=1, keepdims=True)
    return (z3 * lax.rsqrt(ms + EPS)).reshape(rows, tokens) * w_col


def _inproj_t_kernel(x_ref, nw_ref, w_ref, bg_ref, qnw_ref, knw_ref,
                     qa_ref, ks_ref, kv_ref, qm_ref, km_ref, vm_ref, om_ref, gt_ref, h_scr):
    tm = x_ref.shape[0]
    sub = INPROJ_SUB

    def norm_rows(c):
        for r0 in range(c * sub, (c + 1) * sub, NORM_ROWS):
            rows = slice(r0, r0 + NORM_ROWS)
            h_scr[:, rows] = _rms(x_ref[rows, :], nw_ref[...]).T.astype(BF16)
            yield

    def project(c):
        tok = slice(c * sub, (c + 1) * sub)
        h_t = h_scr[:, tok]

        def proj(lo, width):
            return _dot(w_ref[lo:lo + width, :], h_t)

        qa_ref[:, tok] = _head_norm_t(proj(OFF_QA, ATT_Q_W), ATT_HEAD_DIM,
                                      qnw_ref[...]).astype(BF16)
        yield
        kv = proj(OFF_KV, 2 * ATT_KV_W)
        k = _head_norm_t(kv[:ATT_KV_W], ATT_HEAD_DIM, knw_ref[...])
        kv_ref[:ATT_KV_W, tok] = k
        kv_ref[ATT_KV_W:, tok] = kv[ATT_KV_W:]
        ks_ref[tok, :] = k.T.astype(BF16)
        qm_ref[:, tok] = (proj(OFF_QM, ML_QK_W) * ML_SCALE).astype(BF16)
        yield
        km_ref[:, tok] = proj(OFF_KM, ML_QK_W).astype(BF16)
        vm_ref[:, tok] = proj(OFF_VM, ML_V_W).astype(BF16)
        yield
        om_ref[:, tok] = proj(OFF_OM, ML_V_W).astype(BF16)
        gl = proj(OFF_GL, 2 * SUBLANES)[:N_GATES] + bg_ref[...]
        row = lax.broadcasted_iota(jnp.int32, gl.shape, 0)
        gt_ref[:, tok] = jnp.where(row < ML_HEADS, gl, _log_sigmoid(gl))
        yield

    for _ in norm_rows(0):
        pass
    for c in range(tm // sub):
        norms = norm_rows(c + 1) if c + 1 < tm // sub else iter(())
        for _ in project(c):
            next(norms, None)
        for _ in norms:
            pass


def _inproj_t(x2d, nw, w_in_t, bg_col, qnw_col, knw_col):
    n = x2d.shape[0]
    tm = INPROJ_TILE
    full = lambda a: pl.BlockSpec(a.shape, lambda i: (0,) * a.ndim)
    once = lambda a: pl.BlockSpec(a.shape, lambda i: (0,) * a.ndim, pipeline_mode=pl.Buffered(1))
    col = lambda w: pl.BlockSpec((None, w, tm), lambda i: (i, 0, 0))
    slab = lambda w, dt: jax.ShapeDtypeStruct((n // tm, w, tm), dt)
    return pl.pallas_call(
        _inproj_t_kernel,
        grid=(n // tm,),
        in_specs=[pl.BlockSpec((tm, D_MODEL), lambda i: (i, 0)), full(nw), once(w_in_t),
                  full(bg_col), full(qnw_col), full(knw_col)],
        out_specs=[col(ATT_Q_W), pl.BlockSpec((tm, ATT_KV_W), lambda i: (i, 0)),
                   col(2 * ATT_KV_W), col(ML_QK_W), col(ML_QK_W), col(ML_V_W), col(ML_V_W),
                   col(N_GATES)],
        out_shape=[slab(ATT_Q_W, BF16),
                   jax.ShapeDtypeStruct((n, ATT_KV_W), BF16),
                   slab(2 * ATT_KV_W, F32), slab(ML_QK_W, BF16), slab(ML_QK_W, BF16),
                   slab(ML_V_W, BF16), slab(ML_V_W, BF16), slab(N_GATES, F32)],
        scratch_shapes=[pltpu.VMEM((D_MODEL, tm), BF16)],
        compiler_params=pltpu.CompilerParams(dimension_semantics=("arbitrary",),
                                             vmem_limit_bytes=VMEM_LIMIT),
        name="inproj_t",
    )(x2d, nw, w_in_t, bg_col, qnw_col, knw_col)


def _prompt_mixer_t_kernel(sinks_ref, qa_ref, ksc_ref, ksp_ref, kvc_ref, kvp_ref, qm_ref, km_ref,
                           vm_ref, om_ref, gt_ref, x_ref, wout_ref, mlnw_ref,
                           x1_ref, ct_ref, nrow_ref, m_ref, kt_ref, vt_ref,
                           mix_scr, state_scr, m_scr, band_scr, causal_scr, tri_scr,
                           s_scr_a, s_scr_b, e_scr):
    i = pl.program_id(1)
    A = WINDOW
    L = MIX_TILE
    C = ML_CHUNK
    n_pairs = ML_HEADS // 2

    @pl.when(i == 0)
    def _():
        state_scr[...] = jnp.zeros(state_scr.shape, F32)
        m_scr[...] = jnp.zeros(m_scr.shape, F32)
        kj = lax.broadcasted_iota(jnp.int32, (2 * A, A), 0)
        qi = lax.broadcasted_iota(jnp.int32, (2 * A, A), 1)
        band = (kj > qi) & (kj <= qi + WINDOW)
        band_scr[0] = jnp.where(band, 0.0, -jnp.inf)
        band_scr[1] = jnp.where(band & (kj >= A), 0.0, -jnp.inf)
        r = lax.broadcasted_iota(jnp.int32, (C, C), 0)
        c = lax.broadcasted_iota(jnp.int32, (C, C), 1)
        causal_scr[...] = jnp.where(r <= c, 0.0, -jnp.inf)
        tri_scr[...] = (r <= c).astype(F32).astype(BF16)

    k_all = jnp.concatenate([ksp_ref[...], ksc_ref[...]], axis=0)
    v_all = jnp.concatenate([kvp_ref[ATT_KV_W:, :], kvc_ref[ATT_KV_W:, :]], axis=1).astype(BF16)
    zero_q = jnp.zeros((ATT_HEAD_DIM, A), BF16)
    slot = 0
    s_bufs = (s_scr_a, s_scr_b)

    def stage_scores(j):
        pieces = []
        for h in range(ATT_HEADS):
            q_h = qa_ref[h * ATT_HEAD_DIM:(h + 1) * ATT_HEAD_DIM, j * A:(j + 1) * A]
            pieces.append(jnp.concatenate([q_h, zero_q] if h < ATT_GROUP else [zero_q, q_h],
                                          axis=0))
        s_bufs[j % 2][slot] = _dot(k_all[j * A:(j + 2) * A, :], jnp.concatenate(pieces, axis=1))

    def attend(j):
        cols = slice(j * A, (j + 1) * A)
        vt = v_all[:, j * A:(j + 2) * A]
        if j + 1 < L // A:
            stage_scores(j + 1)
        s_buf = s_bufs[j % 2]
        bias = jnp.where(i > 0, band_scr[0], band_scr[1]) if j == 0 else band_scr[0]
        m_rows = []
        for h in range(ATT_HEADS):
            sb = s_buf[slot, :, h * A:(h + 1) * A] + bias
            m_rows.append(jnp.maximum(jnp.max(sb, axis=0, keepdims=True),
                                      sinks_ref[h] * LOG2_E))
        inv_rows = []
        for h in range(ATT_HEADS):
            e = jnp.exp2(s_buf[slot, :, h * A:(h + 1) * A] + (bias - m_rows[h]))
            e_scr[:, h * A:(h + 1) * A] = e.astype(BF16)
            inv_rows.append(1.0 / (jnp.sum(e, axis=0, keepdims=True)
                                   + jnp.exp2(sinks_ref[h] * LOG2_E - m_rows[h])))
        o = _dot(vt, e_scr[...])
        for h in range(ATT_HEADS):
            g = h // ATT_GROUP
            mix_scr[h * ATT_HEAD_DIM:(h + 1) * ATT_HEAD_DIM, cols] = (
                o[g * ATT_HEAD_DIM:(g + 1) * ATT_HEAD_DIM, h * A:(h + 1) * A]
                * inv_rows[h]).astype(BF16)

    row128 = lax.broadcasted_iota(jnp.int32, (LANES, C), 0)
    ones_rows = (row128 == 0).astype(F32).astype(BF16)

    def mlstm_chunk(ci):
        tok = slice(ci * C, (ci + 1) * C)
        gates = gt_ref[:, tok] * LOG2_E
        cum_row = jnp.zeros(gates.shape, F32)
        for part in _split3(gates):
            cum_row = cum_row + _dot(part, tri_scr[...])
        ig_rows = gates[:ML_HEADS]
        b_rows = cum_row[ML_HEADS:]
        key_cols = jnp.concatenate([ig_rows - b_rows, jnp.zeros((LANES - ML_HEADS, C), F32)],
                                   axis=0).T
        for p in range(n_pairs):
            q_c = qm_ref[p * LANES:(p + 1) * LANES, tok]
            k_pair = km_ref[p * LANES:(p + 1) * LANES, tok]
            zero = jnp.zeros_like(q_c)
            state = state_scr[p]
            state_bf = state.astype(BF16)
            new_state = []
            for e_id in range(2):
                h = 2 * p + e_id
                v_rows = slice(h * ML_V_DIM, (h + 1) * ML_V_DIM)
                head_rows = (row128 < ML_QK_DIM) if e_id == 0 else (row128 >= ML_QK_DIM)
                q_pad = jnp.where(head_rows, q_c, zero)
                b_r = b_rows[h:h + 1, :]
                ig_r = ig_rows[h:h + 1, :]
                m_prev = m_scr[h:h + 1, 0:1]
                dm = (b_r + key_cols[:, h:h + 1]) + causal_scr[...]
                inter = b_r + m_prev
                m_row = jnp.maximum(inter, jnp.max(dm, axis=0, keepdims=True))
                w_inter = jnp.exp2(inter - m_row)
                qk = lax.dot_general(k_pair, q_pad, (((0,), (0,)), ((), ())),
                                     preferred_element_type=F32)
                p_t = (qk * jnp.exp2(dm - m_row)).astype(BF16)
                v_ext = jnp.concatenate([vm_ref[v_rows, tok], ones_rows], axis=0)
                num = _dot(v_ext, p_t) + w_inter * _dot(state_bf, q_pad)
                den = num[ML_V_DIM:ML_V_DIM + 1, :]
                hh = num[:ML_V_DIM] * (1.0 / jnp.maximum(jnp.abs(den), jnp.exp2(-m_row)))
                ms = jnp.mean(hh * hh, axis=0, keepdims=True)
                gate = _sigmoid(om_ref[v_rows, tok].astype(F32))
                mix_scr[ATT_Q_W + h * ML_V_DIM:ATT_Q_W + (h + 1) * ML_V_DIM, tok] = (
                    hh * lax.rsqrt(ms + EPS) * mlnw_ref[v_rows, :] * gate).astype(BF16)
                b_last = b_r[:, C - 1:C]
                a_r = b_last - b_r + ig_r
                m_new = jnp.maximum(b_last + m_prev, jnp.max(a_r, axis=-1, keepdims=True))
                sc = jnp.exp2(b_last + m_prev - m_new)
                wsv = (v_ext.astype(F32) * jnp.exp2(a_r - m_new)).astype(BF16)
                new_state.append(sc * state + _dot_nt(wsv, k_pair))
                m_scr[h:h + 1, :] = jnp.broadcast_to(m_new, (1, LANES))
            first = lax.broadcasted_iota(jnp.int32, state.shape, 1) < ML_QK_DIM
            state_scr[p] = jnp.where(first, new_state[0], new_state[1])
            yield

    def out_proj(ci):
        tok = slice(ci * C, (ci + 1) * C)
        mix_t = mix_scr[:, tok].T
        for n in range(D_MODEL // OUT_COLS):
            nc = slice(n * OUT_COLS, (n + 1) * OUT_COLS)
            x1_ref[tok, nc] = x_ref[tok, nc] + _dot(mix_t, wout_ref[:, nc])
            yield

    stage_scores(0)
    n_sub = C // A
    pairs = (step for ci in range(L // C) for step in mlstm_chunk(ci))
    projs = iter(())
    for j in range(L // A):
        if j and j % n_sub == 0:
            projs = out_proj(j // n_sub - 1)
        attend(j)
        next(projs, None)
        next(pairs, None)
        next(projs, None)
    for _ in pairs:
        pass
    for _ in projs:
        pass
    for _ in out_proj(L // C - 1):
        pass

    @pl.when(i == pl.num_programs(1) - 1)
    def _():
        for p in range(n_pairs):
            c_t = state_scr[p, :ML_V_DIM, :].T
            for e_id in range(2):
                ct_ref[0, 2 * p + e_id] = c_t[e_id * ML_QK_DIM:(e_id + 1) * ML_QK_DIM, :]
            nrow_ref[0, p:p + 1, :] = state_scr[p, ML_V_DIM:ML_V_DIM + 1, :]
        for h in range(ML_HEADS):
            m_ref[0, :, h:h + 1] = m_scr[h:h + 1, 0:1] * (1.0 / LOG2_E)
        kt_ref[0] = kvc_ref[:ATT_KV_W, L - WINDOW:]
        vt_ref[0] = kvc_ref[ATT_KV_W:, L - WINDOW:]


def _prompt_mixer_t(batch, seq, sinks, qa, ks, kv, qm, km, vm, om, gt, x2d, wout, mlnw_col):
    tq = MIX_TILE
    nt = seq // tq
    sub = tq // WINDOW
    per_slab = INPROJ_TILE // tq
    win_per_slab = INPROJ_TILE // WINDOW
    col = lambda w: pl.BlockSpec(
        (None, w, tq), lambda b, i: ((b * nt + i) // per_slab, 0, (b * nt + i) % per_slab))
    full = lambda a: pl.BlockSpec(a.shape, lambda b, i: (0,) * a.ndim)
    once = lambda a: pl.BlockSpec(a.shape, lambda b, i: (0,) * a.ndim,
                                  pipeline_mode=pl.Buffered(1))
    prev_block = lambda b, i: jnp.maximum((b * nt + i) * sub - 1, 0)
    per_batch = lambda *dims: pl.BlockSpec((1,) + dims, lambda b, i: (b,) + (0,) * len(dims))
    return pl.pallas_call(
        _prompt_mixer_t_kernel,
        grid=(batch, nt),
        in_specs=[pl.BlockSpec(memory_space=pltpu.SMEM),
                  col(ATT_Q_W),
                  pl.BlockSpec((tq, ATT_KV_W), lambda b, i: (b * nt + i, 0)),
                  pl.BlockSpec((WINDOW, ATT_KV_W), lambda b, i: (prev_block(b, i), 0)),
                  col(2 * ATT_KV_W),
                  pl.BlockSpec((None, 2 * ATT_KV_W, WINDOW),
                               lambda b, i: (prev_block(b, i) // win_per_slab, 0,
                                             prev_block(b, i) % win_per_slab)),
                  col(ML_QK_W), col(ML_QK_W), col(ML_V_W), col(ML_V_W), col(N_GATES),
                  pl.BlockSpec((tq, D_MODEL), lambda b, i: (b * nt + i, 0)),
                  once(wout), full(mlnw_col)],
        out_specs=[pl.BlockSpec((tq, D_MODEL), lambda b, i: (b * nt + i, 0)),
                   per_batch(ML_HEADS, ML_QK_DIM, ML_V_DIM),
                   per_batch(ML_HEADS // 2, LANES),
                   per_batch(1, ML_HEADS),
                   per_batch(ATT_KV_W, WINDOW),
                   per_batch(ATT_KV_W, WINDOW)],
        out_shape=[jax.ShapeDtypeStruct((batch * seq, D_MODEL), F32),
                   jax.ShapeDtypeStruct((batch, ML_HEADS, ML_QK_DIM, ML_V_DIM), F32),
                   jax.ShapeDtypeStruct((batch, ML_HEADS // 2, LANES), F32),
                   jax.ShapeDtypeStruct((batch, 1, ML_HEADS), F32),
                   jax.ShapeDtypeStruct((batch, ATT_KV_W, WINDOW), F32),
                   jax.ShapeDtypeStruct((batch, ATT_KV_W, WINDOW), F32)],
        scratch_shapes=[pltpu.VMEM((D_MODEL, tq), BF16),
                        pltpu.VMEM((ML_HEADS // 2, 2 * LANES, LANES), F32),
                        pltpu.VMEM((SUBLANES, LANES), F32),
                        pltpu.VMEM((2, 2 * WINDOW, WINDOW), F32),
                        pltpu.VMEM((ML_CHUNK, ML_CHUNK), F32),
                        pltpu.VMEM((ML_CHUNK, ML_CHUNK), BF16),
                        pltpu.VMEM((1, 2 * WINDOW, ATT_HEADS * WINDOW), F32),
                        pltpu.VMEM((1, 2 * WINDOW, ATT_HEADS * WINDOW), F32),
                        pltpu.VMEM((2 * WINDOW, ATT_HEADS * WINDOW), BF16)],
        compiler_params=pltpu.CompilerParams(dimension_semantics=("arbitrary", "arbitrary"),
                                             vmem_limit_bytes=VMEM_LIMIT),
        name="prompt_mixer_t",
    )(sinks, qa, ks, ks, kv, kv, qm, km, vm, om, gt, x2d, wout, mlnw_col)


def _sample_mixer_kernel(t_len, sinks_ref, qa_ref, kv_ref, ck_ref, cv_ref, qm_ref, km_ref,
                         vm_ref, om_ref, gt_ref, c0_ref, n0_ref, m0_ref,
                         mlnw_ref, mix_scr, nk_ref, nv_ref, c_ref, n_ref, m_ref):
    bt = SAMPLE_BT
    T = t_len
    L = bt * T

    lane3 = lax.broadcasted_iota(jnp.int32, (bt, T, LANES), 2)
    low3 = lane3 < ATT_HEAD_DIM
    lane = lax.broadcasted_iota(jnp.int32, (L, LANES), 1)
    low = lane < ATT_HEAD_DIM

    qa3 = qa_ref[...].astype(F32).reshape(bt, T, ATT_Q_W)
    pieces = []
    for col in range(ATT_GROUP):
        qc = qa3[:, :, col * LANES:(col + 1) * LANES]
        pieces += [jnp.where(low3, qc, 0.0), jnp.where(low3, 0.0, qc)]
    q3 = jnp.concatenate(pieces, axis=1).astype(BF16)
    R = bt * N_STACK * T
    q2 = q3.reshape(R, LANES)
    kv_new = kv_ref[...]
    k_new = kv_new[:, :ATT_KV_W]
    v_new = kv_new[:, ATT_KV_W:]
    def attention_stages():
        s_c = jnp.einsum('bqd,bdk->bqk', q3, ck_ref[...].astype(BF16),
                         preferred_element_type=F32).reshape(R, WINDOW)
        s_n = _dot_nt(q2, k_new.astype(BF16))
        row_c = lax.broadcasted_iota(jnp.int32, (R, WINDOW), 0)
        col_c = lax.broadcasted_iota(jnp.int32, (R, WINDOW), 1)
        s_c = jnp.where(col_c > row_c % T, s_c, -jnp.inf)
        row_n = lax.broadcasted_iota(jnp.int32, (R, L), 0)
        col_n = lax.broadcasted_iota(jnp.int32, (R, L), 1)
        valid_n = (row_n // (N_STACK * T) == col_n // T) & (col_n % T <= row_n % T)
        s_n = jnp.where(valid_n, s_n, -jnp.inf)
        yield
        stack_id = (lax.broadcasted_iota(jnp.int32, (R, 1), 0) // T) % N_STACK
        sink = jnp.zeros((R, 1), F32)
        for k_id in range(N_STACK):
            sink = jnp.where(stack_id == k_id, sinks_ref[ATT_HEAD_ORDER[k_id]], sink)
        m = jnp.maximum(jnp.maximum(jnp.max(s_c, axis=-1, keepdims=True),
                                    jnp.max(s_n, axis=-1, keepdims=True)), sink)
        yield
        e_c = jnp.exp(s_c - m)
        e_n = jnp.exp(s_n - m)
        denom = (jnp.sum(e_c, axis=-1, keepdims=True) + jnp.sum(e_n, axis=-1, keepdims=True)
                 + jnp.exp(sink - m))
        yield
        o = jnp.einsum('bqk,bdk->bqd', e_c.astype(BF16).reshape(bt, N_STACK * T, WINDOW),
                       cv_ref[...].astype(BF16), preferred_element_type=F32).reshape(R, LANES)
        o = (o + _dot(e_n.astype(BF16), v_new.astype(BF16))) / denom
        yield
        o3 = o.reshape(bt, N_STACK * T, LANES)
        for col in range(ATT_GROUP):
            lo_h = o3[:, (2 * col) * T:(2 * col + 1) * T, :]
            hi_h = o3[:, (2 * col + 1) * T:(2 * col + 2) * T, :]
            mix_scr[:, col * LANES:(col + 1) * LANES] = jnp.where(
                low3, lo_h, hi_h).reshape(L, LANES).astype(BF16)
        yield

    keep = lax.broadcasted_iota(jnp.int32, (ATT_KV_W, WINDOW), 1) < WINDOW - T
    k_new_t = k_new.T
    v_new_t = v_new.T
    def roll_caches():
        for q in range(bt):
            shift = (WINDOW - T - q * T) % WINDOW
            nk_ref[q] = jnp.where(keep, pltpu.roll(ck_ref[q], WINDOW - T, axis=1),
                                  pltpu.roll(k_new_t, shift, axis=1))
            nv_ref[q] = jnp.where(keep, pltpu.roll(cv_ref[q], WINDOW - T, axis=1),
                                  pltpu.roll(v_new_t, shift, axis=1))
            if (q + 1) % ROLL_SEQS == 0:
                yield

    rolls = roll_caches()

    r = lax.broadcasted_iota(jnp.int32, (L, L), 0)
    c = lax.broadcasted_iota(jnp.int32, (L, L), 1)
    seg = (r // T == c // T) & (r <= c)
    seg_bias = jnp.where(seg, 0.0, -jnp.inf)
    seg_bf = seg.astype(F32).astype(BF16)
    gates = gt_ref[...] * LOG2_E
    cum_row = jnp.zeros(gates.shape, F32)
    for part in _split3(gates):
        cum_row = cum_row + _dot(part, seg_bf)
    ig_rows = gates[:ML_HEADS]
    b_rows = cum_row[ML_HEADS:]
    gate_cols = jnp.concatenate([ig_rows, b_rows, jnp.zeros((LANES - N_GATES, L), F32)],
                                axis=0).T

    def col_to_row(x_col):
        return jnp.broadcast_to(x_col, (L, LANES)).T[0:1, :]

    ones_rows = (r[:LANES] == 0).astype(F32).astype(BF16)
    qm = qm_ref[...]
    km = km_ref[...]
    qm_f = qm.astype(F32)
    km_f = km.astype(F32)
    n_rep = bt * ML_QK_DIM // LANES
    bd_row = lax.broadcasted_iota(jnp.int32, (L, bt * ML_QK_DIM), 0) // T
    bd_lane = lax.broadcasted_iota(jnp.int32, (L, bt * ML_QK_DIM), 1) // ML_QK_DIM
    block_diag = bd_row == bd_lane

    def spread(x_pair, e):
        other = pltpu.roll(x_pair, ML_QK_DIM, axis=1)
        twice = jnp.where(low, x_pair, other) if e == 0 else jnp.where(low, other, x_pair)
        return jnp.where(block_diag, jnp.concatenate([twice] * n_rep, axis=1), 0.0).astype(BF16)

    def head_stages(h):
        p, e = divmod(h, 2)
        qc = qm[:, p * LANES:(p + 1) * LANES]
        k_pair = km[:, p * LANES:(p + 1) * LANES]
        zero = jnp.zeros_like(qc)
        q_pad = jnp.where(low, qc, zero) if e == 0 else jnp.where(low, zero, qc)
        v_h = vm_ref[:, h * ML_V_DIM:(h + 1) * ML_V_DIM]
        v_ext_t = jnp.concatenate([v_h.astype(F32).T.astype(BF16), ones_rows], axis=0)
        ig_c = gate_cols[:, h:h + 1]
        b_c = gate_cols[:, ML_HEADS + h:ML_HEADS + h + 1]
        b_r = b_rows[h:h + 1, :]
        yield
        m0 = m0_ref[:, :, h:h + 1] * LOG2_E
        inter = b_r + col_to_row(jnp.broadcast_to(m0, (bt, T, 1)).reshape(L, 1))
        dm = (b_r + (ig_c - b_c)) + seg_bias
        m_row = jnp.maximum(inter, jnp.max(dm, axis=0, keepdims=True))
        w_inter = jnp.exp2(inter - m_row)
        yield
        p_t = (_dot_nt(k_pair, q_pad) * jnp.exp2(dm - m_row)).astype(BF16)
        num_t = _dot(v_ext_t, p_t)
        yield
        q_h3 = qm_f[:, h * ML_QK_DIM:(h + 1) * ML_QK_DIM].reshape(bt, T, ML_QK_DIM)
        k_h3 = km_f[:, h * ML_QK_DIM:(h + 1) * ML_QK_DIM].reshape(bt, T, ML_QK_DIM)
        c0 = c0_ref[:, h]
        n0 = n0_ref[:, h:h + 1, :]
        q_c_t = _dot(spread(qm_f[:, p * LANES:(p + 1) * LANES], e),
                     c0.astype(BF16).reshape(bt * ML_QK_DIM, ML_V_DIM)).T
        q_n_r = col_to_row(jnp.sum(q_h3 * n0, axis=-1, keepdims=True).reshape(L, 1))
        yield
        num =num_t[:ML_V_DIM] + w_inter * q_c_t
        den = num_t[ML_V_DIM:ML_V_DIM + 1] + w_inter * q_n_r
        hh = num * (1.0 / jnp.maximum(jnp.abs(den), jnp.exp2(-m_row)))
        ms = jnp.mean(hh * hh, axis=0, keepdims=True)
        yield
        mix_scr[:, ATT_Q_W + h * ML_V_DIM:ATT_Q_W + (h + 1) * ML_V_DIM] = (
            (hh * lax.rsqrt(ms + EPS)).T * mlnw_ref[:, h * ML_V_DIM:(h + 1) * ML_V_DIM]
            * _sigmoid(om_ref[:, h * ML_V_DIM:(h + 1) * ML_V_DIM].astype(F32))).astype(BF16)
        yield
        b3 = b_c.reshape(bt, T, 1)
        b_last = b3[:, T - 1:T, :]
        a3 = b_last - b3 + ig_c.reshape(bt, T, 1)
        m_new = jnp.maximum(b_last + m0, jnp.max(a3, axis=1, keepdims=True))
        sc = jnp.exp2(b_last + m0 - m_new)
        ws = jnp.exp2(a3 - m_new)
        yield
        kw = spread(km_f[:, p * LANES:(p + 1) * LANES] * ws.reshape(L, 1), e)
        d_c = lax.dot_general(kw, v_h, (((0,), (0,)), ((), ())), preferred_element_type=F32)
        c_ref[:, h] = sc * c0 + d_c.reshape(bt, ML_QK_DIM, ML_V_DIM)
        n_ref[:, h:h + 1, :] = sc * n0 + jnp.sum(ws * k_h3, axis=1, keepdims=True)
        m_ref[:, :, h:h + 1] = m_new * (1.0 / LOG2_E)
        yield

    att = attention_stages()
    for _ in zip(*[head_stages(h) for h in range(ML_HEADS)]):
        next(rolls, None)
        next(att, None)
    for _ in att:
        pass
    for _ in rolls:
        pass


def _sample_mixer(nb, t_len, sinks, qa, kv, ck, cv, qm, km, vm, om, gt, c0, n0, m0, mlnw):
    bt = SAMPLE_BT
    tl = bt * t_len
    row = lambda w: pl.BlockSpec((tl, w), lambda i: (i, 0))
    full = lambda a: pl.BlockSpec(a.shape, lambda i: (0,) * a.ndim)
    cache = pl.BlockSpec((bt, ATT_KV_W, WINDOW), lambda i: (i, 0, 0))
    c_spec = pl.BlockSpec((bt, ML_HEADS, ML_QK_DIM, ML_V_DIM), lambda i: (i, 0, 0, 0))
    n_spec = pl.BlockSpec((bt, ML_HEADS, ML_QK_DIM), lambda i: (i, 0, 0))
    m_spec = pl.BlockSpec((bt, 1, ML_HEADS), lambda i: (i, 0, 0))
    return pl.pallas_call(
        functools.partial(_sample_mixer_kernel, t_len),
        grid=(nb // bt,),
        in_specs=[pl.BlockSpec(memory_space=pltpu.SMEM),
                  row(ATT_Q_W), row(2 * ATT_KV_W), cache, cache, row(ML_QK_W), row(ML_QK_W),
                  row(ML_V_W), row(ML_V_W), pl.BlockSpec((N_GATES, tl), lambda i: (0, i)),
                  c_spec, n_spec, m_spec, full(mlnw)],
        out_specs=[row(D_MODEL), cache, cache, c_spec, n_spec, m_spec],
        out_shape=[jax.ShapeDtypeStruct((nb * t_len, D_MODEL), BF16),
                   jax.ShapeDtypeStruct((nb, ATT_KV_W, WINDOW), F32),
                   jax.ShapeDtypeStruct((nb, ATT_KV_W, WINDOW), F32),
                   jax.ShapeDtypeStruct((nb, ML_HEADS, ML_QK_DIM, ML_V_DIM), F32),
                   jax.ShapeDtypeStruct((nb, ML_HEADS, ML_QK_DIM), F32),
                   jax.ShapeDtypeStruct((nb, 1, ML_HEADS), F32)],
        compiler_params=pltpu.CompilerParams(dimension_semantics=("arbitrary",),
                                             vmem_limit_bytes=VMEM_LIMIT),
        name="sample_mixer",
    )(sinks, qa, kv, ck, cv, qm, km, vm, om, gt, c0, n0, m0, mlnw)


def _ffn_kernel(seq_rows, *refs):
    if seq_rows is None:
        (x_ref, nw_ref, w_ref, cw_ref, cb_ref, wd_ref, y_ref, conv_ref,
         gbuf, act_scr, carry) = refs
        hist_ref = mix_ref = None
    else:
        (x_ref, mix_ref, wout_ref, hist_ref, nw_ref, w_ref, cw_ref, cb_ref, wd_ref, y_ref,
         conv_ref, gbuf, act_scr, wperm_scr) = refs
        carry = None
    tm = x_ref.shape[0]
    tf = FF_CHUNK
    n_hist = CONV_W - 1
    rows = tm if seq_rows is None else seq_rows
    nseq = tm // rows
    base = SUBLANES
    n_chunks = D_FF // tf

    if carry is not None:
        @pl.when(pl.program_id(1) == 0)
        def _():
            carry[...] = jnp.zeros(carry.shape, F32)

    if mix_ref is None:
        x = x_ref[...]
    else:
        @pl.when(pl.program_id(0) == 0)
        def _():
            _permute_head_rows(wperm_scr, wout_ref)
            wperm_scr[ATT_Q_W:, :] = wout_ref[ATT_Q_W:, :]
        x = x_ref[...] + _dot(mix_ref[...], wperm_scr[...])
    h2 = _rms(x, nw_ref[...]).astype(BF16)

    def proj(f):
        return (_dot(h2, w_ref[:, f * tf:(f + 1) * tf]),
                _dot(h2, w_ref[:, D_FF + f * tf:D_FF + (f + 1) * tf]))

    nxt = proj(0)
    for f in range(n_chunks):
        g, u = nxt
        if f + 1 < n_chunks:
            nxt = proj(f + 1)
        cols = slice(f * tf, (f + 1) * tf)
        s = f % 2
        g3 = g.reshape(nseq, rows, tf)
        if seq_rows is None:
            gbuf[s, :, base - n_hist:base, :] = carry[:, SUBLANES - n_hist:, cols]
            carry[:, SUBLANES - n_hist:, cols] = g3[:, rows - n_hist:, :]
        else:
            gbuf[s, :, base - n_hist:base, :] = hist_ref[:, :, cols]
            conv_ref[:, :, cols] = g3[:, rows - n_hist:, :]
        gbuf[s, :, base:base + rows, :] = g3
        gc = cb_ref[:, cols] + g * cw_ref[CONV_W - 1:CONV_W, cols]
        for d in range(1, CONV_W):
            gm = gbuf[s, :, base - d:base - d + rows, :].reshape(tm, tf)
            gc = gc + gm * cw_ref[CONV_W - 1 - d:CONV_W - d, cols]
        act_scr[:, cols] = (gc * _sigmoid(gc) * u).astype(BF16)
    y_ref[...] = x + _dot(act_scr[...], wd_ref[...])

    if carry is not None:
        @pl.when(pl.program_id(1) == pl.num_programs(1) - 1)
        def _():
            conv_ref[...] = carry[:, SUBLANES - n_hist:, :]


def _ffn_scratch(tm, rows):
    return [pltpu.VMEM((2, tm // rows, SUBLANES + rows, FF_CHUNK), F32),
            pltpu.VMEM((tm, D_FF), BF16)]


def _ffn_prompt(batch, seq, x2d, nw, w, cw, cb, wd):
    tm = FFN_TILE
    nt = seq // tm
    full = lambda a: pl.BlockSpec(a.shape, lambda b, i: (0,) * a.ndim)
    once = lambda a: pl.BlockSpec(a.shape, lambda b, i: (0,) * a.ndim,
                                  pipeline_mode=pl.Buffered(1))
    row = pl.BlockSpec((tm, D_MODEL), lambda b, i: (b * nt + i, 0))
    return pl.pallas_call(
        functools.partial(_ffn_kernel, None),
        grid=(batch, nt),
        in_specs=[row, full(nw), once(w), full(cw), full(cb), once(wd)],
        out_specs=[row, pl.BlockSpec((1, CONV_W - 1, D_FF), lambda b, i: (b, 0, 0))],
        out_shape=[jax.ShapeDtypeStruct((batch * seq, D_MODEL), F32),
                   jax.ShapeDtypeStruct((batch, CONV_W - 1, D_FF), F32)],
        scratch_shapes=_ffn_scratch(tm, tm) + [pltpu.VMEM((1, SUBLANES, D_FF), F32)],
        compiler_params=pltpu.CompilerParams(dimension_semantics=("arbitrary", "arbitrary"),
                                             vmem_limit_bytes=VMEM_LIMIT),
        name="ffn_prompt",
    )(x2d, nw, w, cw, cb, wd)


def _ffn_sample(nb, t_len, x2d, mix, wout, hist, nw, w, cw, cb, wd):
    tm = ROW_TILE
    bt = tm // t_len
    full = lambda a: pl.BlockSpec(a.shape, lambda i: (0,) * a.ndim)
    once = lambda a: pl.BlockSpec(a.shape, lambda i: (0,) * a.ndim, pipeline_mode=pl.Buffered(1))
    row = pl.BlockSpec((tm, D_MODEL), lambda i: (i, 0))
    hist_spec = pl.BlockSpec((bt, CONV_W - 1, D_FF), lambda i: (i, 0, 0))
    return pl.pallas_call(
        functools.partial(_ffn_kernel, t_len),
        grid=(nb // bt,),
        in_specs=[row, row, once(wout), hist_spec, full(nw), once(w), full(cw), full(cb),
                  once(wd)],
        out_specs=[row, hist_spec],
        out_shape=[jax.ShapeDtypeStruct((nb * t_len, D_MODEL), F32),
                   jax.ShapeDtypeStruct((nb, CONV_W - 1, D_FF), F32)],
        scratch_shapes=_ffn_scratch(tm, t_len) + [pltpu.VMEM((D_MODEL, D_MODEL), BF16)],
        compiler_params=pltpu.CompilerParams(dimension_semantics=("arbitrary",),
                                             vmem_limit_bytes=VMEM_LIMIT),
        name="ffn_sample",
    )(x2d, mix, wout, hist, nw, w, cw, cb, wd)


def _head_mean_matrix(width, head_dim):
    idx = np.arange(width) // head_dim
    return jnp.asarray((idx[:, None] == idx[None, :]).astype(np.float32) / head_dim, dtype=BF16)


def _layer_weights(norm_mix_w, w_in, b_gates, q_norm_w, k_norm_w, sinks, ml_norm_w, w_out,
                   norm_ffn_w, w_ffn_in, conv_w, conv_b, w_down):
    w_in_t = jnp.pad(w_in.T.astype(BF16), ((0, IN_WIDTH_PAD - w_in.shape[1]), (0, 0)))
    return dict(
        nw=norm_mix_w.reshape(1, D_MODEL),
        w_in_t=w_in_t,
        bg=jnp.pad(b_gates, (0, LANES - N_GATES)).reshape(1, LANES),
        qnw=(jnp.tile(q_norm_w, ATT_HEADS) * ATT_SCALE).reshape(1, ATT_Q_W),
        knw=jnp.tile(k_norm_w, ATT_KV_HEADS).reshape(1, ATT_KV_W),
        gq=_head_mean_matrix(ATT_Q_W, ATT_HEAD_DIM),
        gk=_head_mean_matrix(ATT_KV_W, ATT_HEAD_DIM),
        bg_col=b_gates.reshape(N_GATES, 1),
        qnw_col=(jnp.tile(q_norm_w, ATT_HEADS) * (ATT_SCALE * LOG2_E)).reshape(ATT_Q_W, 1),
        knw_col=jnp.tile(k_norm_w, ATT_KV_HEADS).reshape(ATT_KV_W, 1),
        mlnw_col=ml_norm_w.reshape(ML_V_W, 1),
        sinks=sinks,
        mlnw=ml_norm_w.reshape(1, ML_V_W),
        wout=w_out.astype(BF16),
        nfw=norm_ffn_w.reshape(1, D_MODEL),
        wff=w_ffn_in.astype(BF16),
        cw=conv_w,
        cb=conv_b.reshape(1, D_FF),
        wd=w_down.astype(BF16),
    )


def _cache_from_t(a_t):
    n = a_t.shape[0]
    return jnp.transpose(a_t.reshape(n, ATT_KV_HEADS, ATT_HEAD_DIM, WINDOW), (0, 3, 1, 2))


def _cache_to_t(a):
    n = a.shape[0]
    return jnp.transpose(a, (0, 2, 3, 1)).reshape(n, ATT_KV_W, WINDOW)


def _prompt_layer(x, w):
    batch, seq, _ = x.shape
    assert seq % INPROJ_TILE == 0 and INPROJ_TILE % MIX_TILE == 0
    assert MIX_TILE % ML_CHUNK == 0 and ML_CHUNK % WINDOW == 0
    assert seq % FFN_TILE == 0
    x2d = x.reshape(batch * seq, D_MODEL)
    qa, ks, kv, qm, km, vm, om, gt = _inproj_t(x2d, w["nw"], w["w_in_t"], w["bg_col"],
                                               w["qnw_col"], w["knw_col"])
    x1, c_t, n_row, m, k_t, v_t = _prompt_mixer_t(batch, seq, w["sinks"], qa, ks, kv, qm, km, vm,
                                                  om, gt, x2d, w["wout"], w["mlnw_col"])
    y, conv = _ffn_prompt(batch, seq, x1, w["nfw"], w["wff"], w["cw"], w["cb"], w["wd"])
    return (y.reshape(batch, seq, D_MODEL), _cache_from_t(k_t), _cache_from_t(v_t),
            jnp.swapaxes(c_t, -1, -2), n_row.reshape(batch, ML_HEADS, ML_QK_DIM),
            m.reshape(batch, ML_HEADS), conv)


def _sample_layer(x, ck, cv, c0, n0, m0, conv_buf, w):
    nb, t_len, _ = x.shape
    assert t_len == SUBLANES and SAMPLE_BT * t_len == LANES and nb % SAMPLE_BT == 0
    assert (nb * t_len) % ROW_TILE == 0
    x2d = x.reshape(nb * t_len, D_MODEL)
    qa, kv, qm, km, vm, om, gt = _inproj(x2d, w["nw"], w["w_in_t"], w["bg"], w["qnw"], w["knw"],
                                         w["gq"], w["gk"])
    mix, nk_t, nv_t, c_t, n, m = _sample_mixer(
        nb, t_len, w["sinks"], qa, kv, _cache_to_t(ck), _cache_to_t(cv), qm, km, vm, om, gt,
        jnp.swapaxes(c0, -1, -2), n0, m0.reshape(nb, 1, ML_HEADS), w["mlnw"])
    y, conv = _ffn_sample(nb, t_len, x2d, mix, w["wout"], conv_buf, w["nfw"], w["wff"],
                          w["cw"], w["cb"], w["wd"])
    return (y.reshape(nb, t_len, D_MODEL), _cache_from_t(nk_t), _cache_from_t(nv_t),
            jnp.swapaxes(c_t, -1, -2), n, m.reshape(nb, ML_HEADS), conv)


def kernel(x_prompt, x_sample, cache_attn_k, cache_attn_v, state_mlstm_C, state_mlstm_n,
           state_mlstm_m, cache_ffn_conv, norm_mix_w, w_in, b_gates, q_norm_w, k_norm_w,
           sinks, ml_norm_w, w_out, norm_ffn_w, w_ffn_in, conv_w, conv_b, w_down):
    depth = w_in.shape[0]
    yp, ys = x_prompt, x_sample
    sp = [[] for _ in range(6)]
    ss = [[] for _ in range(6)]
    for l in range(depth):
        w = _layer_weights(norm_mix_w[l], w_in[l], b_gates[l], q_norm_w[l], k_norm_w[l], sinks[l],
                           ml_norm_w[l], w_out[l], norm_ffn_w[l], w_ffn_in[l], conv_w[l],
                           conv_b[l], w_down[l])
        yp, *st_p = _prompt_layer(yp, w)
        ys, *st_s = _sample_layer(ys, cache_attn_k[l], cache_attn_v[l], state_mlstm_C[l],
                                  state_mlstm_n[l], state_mlstm_m[l], cache_ffn_conv[l], w)
        for i in range(6):
            sp[i].append(st_p[i])
            ss[i].append(st_s[i])
    k_p, v_p, c_p, n_p, m_p, conv_p = [jnp.stack(a) for a in sp]
    k_s, v_s, c_s, n_s, m_s, conv_s = [jnp.stack(a) for a in ss]
    return (yp, ys, k_p, v_p, c_p, n_p, m_p, conv_p, k_s, v_s, c_s, n_s, m_s, conv_s)
```
